```python
import jax, jax.numpy as jnp
from jax import lax
import numpy as np

D_MODEL = 1024
BATCH = 4
SEQ = 4096
DEPTH = 1

PLE_DIM = 256
CONV_WIDTH = D_MODEL // 2
CONV_HEADS = 8
CONV_K = 3
SGU_WIDTH = D_MODEL // 2
SGU_HEADS = 8
SGU_HEAD_DIM = SGU_WIDTH // SGU_HEADS
CHUNK = 128
MIX_WIDTH = CONV_WIDTH + SGU_WIDTH
IN_PROJ_WIDTH = 3 * CONV_WIDTH + 2 * SGU_WIDTH
N_GROUPS = 4
EXPERTS_PER_GROUP = 8
TOP_K_INNER = 2
D_EXPERT = D_MODEL // 2
EPS = 1e-6

kernel_name = "hymba_conv_sgu_hmoe_block"


def rmsnorm(x, g):
    xf = x.astype(jnp.float32)
    y = xf * lax.rsqrt(jnp.mean(xf * xf, axis=-1, keepdims=True) + EPS)
    return (y * g.astype(jnp.float32)).astype(x.dtype)


def layernorm(x, g):
    xf = x.astype(jnp.float32)
    mu = jnp.mean(xf, axis=-1, keepdims=True)
    xc = xf - mu
    y = xc * lax.rsqrt(jnp.mean(xc * xc, axis=-1, keepdims=True) + EPS)
    return (y * g.astype(jnp.float32)).astype(x.dtype)


def short_conv_mixer(b_gate, c_gate, x_in, w_conv):
    seq = x_in.shape[1]
    z = c_gate * x_in
    zp = jnp.pad(z, ((0, 0), (CONV_K - 1, 0), (0, 0)))
    conv = sum(zp[:, k:k + seq, :] * w_conv[k] for k in range(CONV_K))
    return b_gate * conv


def sgu_mixer(u, v, g_sgu, w_spatial, b_spatial):
    bsz, seq, _ = v.shape
    v = layernorm(v, g_sgu)
    vc = v.reshape(bsz, seq // CHUNK, CHUNK, SGU_HEADS, SGU_HEAD_DIM)
    causal = jnp.tril(jnp.ones((CHUNK, CHUNK), dtype=bool))
    ws = jnp.where(causal[None], w_spatial, jnp.zeros_like(w_spatial))
    mixed = jnp.einsum('hts,bcshd->bcthd', ws, vc) + b_spatial.T[None, None, :, :, None]
    return u * mixed.reshape(bsz, seq, SGU_WIDTH)


def hierarchical_moe(h, w_group, b_group, w_router, b_router, w_gate, w_up, w_down):
    bsz, seq, d = h.shape
    t = h.reshape(bsz * seq, d)
    g_logits = (t @ w_group).astype(jnp.float32) + b_group.astype(jnp.float32)
    g_prob = jax.nn.softmax(g_logits, axis=-1)
    g_w, g_idx = lax.top_k(g_prob, 1)
    e_logits = jnp.einsum('td,gde->tge', t, w_router).astype(jnp.float32) + b_router.astype(jnp.float32)
    e_sel = jnp.take_along_axis(e_logits, g_idx[:, :, None], axis=1)[:, 0]
    e_val, e_idx = lax.top_k(e_sel, TOP_K_INNER)
    e_w = jax.nn.softmax(e_val, axis=-1) * g_w
    inner = jnp.sum(jax.nn.one_hot(e_idx, EXPERTS_PER_GROUP, dtype=jnp.float32) * e_w[..., None], axis=1)
    comb = (jax.nn.one_hot(g_idx[:, 0], N_GROUPS, dtype=jnp.float32)[:, :, None]
            * inner[:, None, :]).astype(h.dtype)
    y = jnp.zeros_like(t)
    for g in range(N_GROUPS):
        a = jnp.einsum('td,edf->tef', t, w_gate[g])
        b = jnp.einsum('td,edf->tef', t, w_up[g])
        hid = jax.nn.silu(a) * b * comb[:, g, :, None]
        y = y + jnp.einsum('tef,efd->td', hid, w_down[g])
    return y.reshape(bsz, seq, d)


def setup_inputs(seed: int = 0) -> dict:
    key = jax.random.key(seed)
    ks = jax.random.split(key, 24)
    f32 = jnp.float32
    n = lambda k, shape, scale: jax.random.normal(k, shape, f32) * scale
    gain = lambda k, shape: 1.0 + 0.05 * jax.random.normal(k, shape, f32)
    L, G, E = DEPTH, N_GROUPS, EXPERTS_PER_GROUP
    return {
        "x": n(ks[0], (BATCH, SEQ, D_MODEL), 1.0),
        "p": n(ks[1], (DEPTH, BATCH, SEQ, PLE_DIM), 1.0),
        "g_mix": gain(ks[2], (L, D_MODEL)),
        "w_in": n(ks[3], (L, D_MODEL, IN_PROJ_WIDTH), D_MODEL ** -0.5),
        "w_conv": n(ks[4], (L, CONV_K, CONV_WIDTH), CONV_K ** -0.5),
        "g_sgu": gain(ks[5], (L, SGU_WIDTH)),
        "w_spatial": n(ks[6], (L, SGU_HEADS, CHUNK, CHUNK), CHUNK ** -0.5),
        "b_spatial": 1.0 + n(ks[7], (L, SGU_HEADS, CHUNK), 0.1),
        "w_out": n(ks[8], (L, MIX_WIDTH, D_MODEL), MIX_WIDTH ** -0.5),
        "g_ffn": gain(ks[9], (L, D_MODEL)),
        "w_group": n(ks[10], (L, D_MODEL, G), D_MODEL ** -0.5),
        "b_group": n(ks[11], (L, G), 0.01),
        "w_router": n(ks[12], (L, G, D_MODEL, E), D_MODEL ** -0.5),
        "b_router": n(ks[13], (L, G, E), 0.01),
        "w_gate": n(ks[14], (L, G, E, D_MODEL, D_EXPERT), D_MODEL ** -0.5),
        "w_up": n(ks[15], (L, G, E, D_MODEL, D_EXPERT), D_MODEL ** -0.5),
        "w_down": n(ks[16], (L, G, E, D_EXPERT, D_MODEL), D_EXPERT ** -0.5),
        "g_ple": gain(ks[17], (L, D_MODEL)),
        "w_ple_gate": n(ks[18], (L, D_MODEL, D_MODEL), D_MODEL ** -0.5),
        "w_ple_proj": n(ks[19], (L, PLE_DIM, D_MODEL), PLE_DIM ** -0.5),
        "g_final": gain(ks[20], (D_MODEL,)),
    }


def reference(x, p, g_mix, w_in, w_conv, g_sgu, w_spatial, b_spatial, w_out, g_ffn,
              w_group, b_group, w_router, b_router, w_gate, w_up, w_down,
              g_ple, w_ple_gate, w_ple_proj, g_final):
    splits = [CONV_WIDTH, 2 * CONV_WIDTH, 3 * CONV_WIDTH, 3 * CONV_WIDTH + SGU_WIDTH]
    for i in range(DEPTH):
        h = rmsnorm(x, g_mix[i])
        z = h @ w_in[i]
        b_gate, c_gate, x_in, u, v = jnp.split(z, splits, axis=-1)
        y_conv = short_conv_mixer(b_gate, c_gate, x_in, w_conv[i])
        y_sgu = sgu_mixer(jax.nn.gelu(u), jax.nn.gelu(v), g_sgu[i],
                          w_spatial[i], b_spatial[i])
        x = x + jnp.concatenate([y_conv, y_sgu], axis=-1) @ w_out[i]
        x = x + hierarchical_moe(rmsnorm(x, g_ffn[i]), w_group[i], b_group[i], w_router[i],
                                 b_router[i], w_gate[i], w_up[i], w_down[i])
        gate = jax.nn.sigmoid(rmsnorm(x, g_ple[i]) @ w_ple_gate[i])
        x = x + gate * (p[i] @ w_ple_proj[i])
    return rmsnorm(x, g_final)
```

```python
import functools

import jax
import jax.numpy as jnp
from jax import lax
from jax.experimental import pallas as pl
from jax.experimental.pallas import tpu as pltpu

F32 = jnp.float32
BF16 = jnp.bfloat16
U32 = jnp.uint32
I32 = jnp.int32

EPS = 1e-6
D_MODEL = 1024
HALF = D_MODEL // 2
CONV_WIDTH = 512
SGU_WIDTH = 512
SGU_HEADS = 8
HEAD_PAIRS = SGU_HEADS // 2
CHUNK = 128
N_GROUPS = 4
EXPERTS_PER_GROUP = 8
N_EXPERTS = N_GROUPS * EXPERTS_PER_GROUP
D_EXPERT = 512
PLE_DIM = 256
LANES = 128
ROUTE_LANE0 = N_GROUPS

TM = 512
TR = 256
VMEM_LIMIT = 56 * 1024 * 1024


def _rms(x, g):
    return x * lax.rsqrt(jnp.mean(x * x, axis=-1, keepdims=True) + EPS) * g


def _pack_bf16_pair(hi_bf16, lo_bf16):
    hi = lax.bitcast_convert_type(hi_bf16.astype(F32), U32)
    lo = lax.bitcast_convert_type(lo_bf16.astype(F32), U32)
    return hi | (lo >> 16)


def _unpack_bf16_pair(p):
    hi = lax.bitcast_convert_type(p & jnp.uint32(0xFFFF0000), F32)
    lo = lax.bitcast_convert_type(p << 16, F32)
    return hi, lo


def _mixer_router_kernel(tiles_per_seq,
                         x_ref, gmix_ref, win_ref, wconv_ref, gsgu_ref, wsp_ref, bsp_ref,
                         wout_ref, gffn_ref, wrt_ref, brt_ref,
                         x1_ref, h2p_ref, route_ref, cnt_ref,
                         wcat_s, tri_s, halo_s, carry_s, mix_s):
    i = pl.program_id(0)
    nch = TM // CHUNK

    @pl.when(i == 0)
    def _init():
        r = lax.broadcasted_iota(I32, (CHUNK, CHUNK), 0)
        c = lax.broadcasted_iota(I32, (CHUNK, CHUNK), 1)
        causal = c <= r
        for j in range(HEAD_PAIRS):
            wa = jnp.where(causal, wsp_ref[2 * j], 0.0)
            wb = jnp.where(causal, wsp_ref[2 * j + 1], 0.0)
            wcat_s[j] = jnp.concatenate([wa, wb], axis=1).astype(BF16)
        rr = lax.broadcasted_iota(I32, (TM, TM), 0)
        cc = lax.broadcasted_iota(I32, (TM, TM), 1)
        tri_s[...] = (cc < rr).astype(BF16)
        carry_s[...] = jnp.zeros_like(carry_s)

    @pl.when(i % tiles_per_seq == 0)
    def _seq_start():
        halo_s[...] = jnp.zeros_like(halo_s)

    x = x_ref[...]
    hb = _rms(x, gmix_ref[...]).astype(BF16)

    def proj(k):
        return jnp.dot(hb, win_ref[:, k * 512:(k + 1) * 512], preferred_element_type=F32)

    zc = proj(1) * proj(2)
    row = lax.broadcasted_iota(I32, (TM, CONV_WIDTH), 0)
    h6 = halo_s[6:7, :]
    h7 = halo_s[7:8, :]
    z1 = jnp.where(row == 0, h7, pltpu.roll(zc, 1, 0))
    z2 = jnp.where(row == 0, h6, jnp.where(row == 1, h7, pltpu.roll(zc, 2, 0)))
    conv = z2 * wconv_ref[0:1, :] + z1 * wconv_ref[1:2, :] + zc * wconv_ref[2:3, :]
    halo_s[...] = zc[TM - 8:TM, :]
    mix_s[:, 0:CONV_WIDTH] = (proj(0) * conv).astype(BF16)

    u = jax.nn.gelu(proj(3))
    v = jax.nn.gelu(proj(4))
    vc = v - jnp.mean(v, axis=-1, keepdims=True)
    vn = vc * lax.rsqrt(jnp.mean(vc * vc, axis=-1, keepdims=True) + EPS) * gsgu_ref[...]
    vnb = vn.astype(BF16)
    left = lax.broadcasted_iota(I32, (CHUNK, LANES), 1) < (LANES // 2)
    zero = jnp.zeros((CHUNK, LANES), BF16)
    for j in range(HEAD_PAIRS):
        cols = []
        for c in range(nch):
            s = vnb[c * CHUNK:(c + 1) * CHUNK, j * LANES:(j + 1) * LANES]
            cols.append(jnp.concatenate([jnp.where(left, s, zero), jnp.where(left, zero, s)], axis=0))
        rhs = jnp.concatenate(cols, axis=1)
        res = jnp.dot(wcat_s[j], rhs, preferred_element_type=F32)
        bias = bsp_ref[:, j * LANES:(j + 1) * LANES]
        for c in range(nch):
            mixed = res[:, c * LANES:(c + 1) * LANES] + bias
            uu = u[c * CHUNK:(c + 1) * CHUNK, j * LANES:(j + 1) * LANES]
            mix_s[c * CHUNK:(c + 1) * CHUNK,
                  CONV_WIDTH + j * LANES:CONV_WIDTH + (j + 1) * LANES] = (uu * mixed).astype(BF16)

    x1 = x + jnp.dot(mix_s[...], wout_ref[...], preferred_element_type=F32)
    x1_ref[...] = x1

    h2b = _rms(x1, gffn_ref[...]).astype(BF16)
    h2p_ref[...] = _pack_bf16_pair(h2b[:, :HALF], h2b[:, HALF:])
    logits = jnp.dot(h2b, wrt_ref[...], preferred_element_type=F32) + brt_ref[...]
    lane = lax.broadcasted_iota(I32, (TM, LANES), 1).astype(F32)
    neg = jnp.float32(-jnp.inf)
    big = jnp.float32(1e9)
    is_g = lane < N_GROUPS
    gl = jnp.where(is_g, logits, neg)
    gmax = jnp.max(gl, axis=1, keepdims=True)
    gsum = jnp.sum(jnp.where(is_g, jnp.exp(gl - gmax), 0.0), axis=1, keepdims=True)
    g_w = 1.0 / gsum
    g_idx = jnp.min(jnp.where(gl == gmax, lane, big), axis=1, keepdims=True)
    lo_lane = ROUTE_LANE0 + EXPERTS_PER_GROUP * g_idx
    in_grp = (lane >= lo_lane) & (lane < lo_lane + EXPERTS_PER_GROUP)
    el = jnp.where(in_grp, logits, neg)
    v1 = jnp.max(el, axis=1, keepdims=True)
    i1 = jnp.min(jnp.where(el == v1, lane, big), axis=1, keepdims=True)
    el2 = jnp.where(lane == i1, neg, el)
    v2 = jnp.max(el2, axis=1, keepdims=True)
    i2 = jnp.min(jnp.where(el2 == v2, lane, big), axis=1, keepdims=True)
    e21 = jnp.exp(v2 - v1)
    w1 = g_w / (1.0 + e21)
    w2 = g_w * e21 / (1.0 + e21)

    sel1 = lane == i1
    sel2 = lane == i2
    onehot = jnp.where(sel1 | sel2, 1.0, 0.0)
    cum = jnp.dot(tri_s[...], onehot.astype(BF16), preferred_element_type=F32) + carry_s[...]
    rank1 = jnp.sum(jnp.where(sel1, cum, 0.0), axis=1, keepdims=True)
    rank2 = jnp.sum(jnp.where(sel2, cum, 0.0), axis=1, keepdims=True)
    carry_s[...] = carry_s[...] + jnp.sum(onehot, axis=0, keepdims=True)

    route = jnp.where(lane == 0, i1 - ROUTE_LANE0, 0.0)
    route = jnp.where(lane == 1, i2 - ROUTE_LANE0, route)
    route = jnp.where(lane == 2, w1, route)
    route = jnp.where(lane == 3, w2, route)
    route = jnp.where(lane == 4, rank1, route)
    route = jnp.where(lane == 5, rank2, route)
    route_ref[...] = route
    cnt_ref[...] = jnp.broadcast_to(carry_s[...], cnt_ref.shape)


def _mixer_router(x2d, g_mix, w_in, w_conv, g_sgu, w_sp, b_sp, w_out, g_ffn, w_rt, b_rt, seq):
    t = x2d.shape[0]
    full = lambda a: pl.BlockSpec(a.shape, lambda i: (0,) * a.ndim)
    ins = (g_mix, w_in, w_conv, g_sgu, w_sp, b_sp, w_out, g_ffn, w_rt, b_rt)
    return pl.pallas_call(
        functools.partial(_mixer_router_kernel, seq // TM),
        grid=(t // TM,),
        in_specs=[pl.BlockSpec((TM, D_MODEL), lambda i: (i, 0))] + [full(a) for a in ins],
        out_specs=[pl.BlockSpec((TM, D_MODEL), lambda i: (i, 0)),
                   pl.BlockSpec((TM, HALF), lambda i: (i, 0)),
                   pl.BlockSpec((TM, LANES), lambda i: (i, 0)),
                   pl.BlockSpec((8, LANES), lambda i: (0, 0))],
        out_shape=[jax.ShapeDtypeStruct((t, D_MODEL), F32),
                   jax.ShapeDtypeStruct((t, HALF), U32),
                   jax.ShapeDtypeStruct((t, LANES), F32),
                   jax.ShapeDtypeStruct((8, LANES), F32)],
        scratch_shapes=[pltpu.VMEM((HEAD_PAIRS, CHUNK, 2 * CHUNK), BF16),
                        pltpu.VMEM((TM, TM), BF16),
                        pltpu.VMEM((8, CONV_WIDTH), F32),
                        pltpu.VMEM((1, LANES), F32),
                        pltpu.VMEM((TM, D_MODEL), BF16)],
        compiler_params=pltpu.CompilerParams(dimension_semantics=("arbitrary",),
                                             vmem_limit_bytes=VMEM_LIMIT),
        name="mixer_router",
    )(x2d, *ins)


def _dispatch_kernel(pos1_ref, pos2_ref, segend_ref, cnt_ref, h2p_ref, xs_ref, zero_s, sem):
    i = pl.program_id(0)

    def zero_copy(e):
        start = pl.multiple_of(segend_ref[e] - TR, TR)
        return pltpu.make_async_copy(zero_s, xs_ref.at[pl.ds(start, TR)], sem.at[0])

    @pl.when(i == 0)
    def _clear_padding():
        zero_s[...] = jnp.zeros_like(zero_s)
        for e in range(N_EXPERTS):
            @pl.when(cnt_ref[e] > 0)
            def _start(e=e):
                zero_copy(e).start()
        for e in range(N_EXPERTS):
            @pl.when(cnt_ref[e] > 0)
            def _wait(e=e):
                zero_copy(e).wait()

        def tail_copy(k):
            return pltpu.make_async_copy(
                zero_s, xs_ref.at[pl.ds(pl.multiple_of(k * TR, TR), TR)], sem.at[0])

        first_unused = segend_ref[N_EXPERTS - 1] // TR
        n_tiles = xs_ref.shape[0] // TR
        lax.fori_loop(first_unused, n_tiles, lambda k, c: (tail_copy(k).start(), c)[1], 0)
        lax.fori_loop(first_unused, n_tiles, lambda k, c: (tail_copy(k).wait(), c)[1], 0)

    base = i * TM

    def row_copy(r, pos_ref):
        return pltpu.make_async_copy(h2p_ref.at[pl.ds(r, 1)],
                                     xs_ref.at[pl.ds(pos_ref[base + r], 1)], sem.at[1])

    def issue(r, carry):
        row_copy(r, pos1_ref).start()
        row_copy(r, pos2_ref).start()
        return carry

    lax.fori_loop(0, TM, issue, 0, unroll=8)

    def drain(r, carry):
        row_copy(r, pos1_ref).wait()
        row_copy(r, pos2_ref).wait()
        return carry

    lax.fori_loop(0, TM, drain, 0, unroll=8)


def _dispatch(pos1, pos2, seg_end, cnt, h2p, n_rows):
    t = h2p.shape[0]
    return pl.pallas_call(
        _dispatch_kernel,
        grid_spec=pltpu.PrefetchScalarGridSpec(
            num_scalar_prefetch=4,
            grid=(t // TM,),
            in_specs=[pl.BlockSpec((TM, HALF), lambda i, *_: (i, 0))],
            out_specs=pl.BlockSpec(memory_space=pl.ANY),
            scratch_shapes=[pltpu.VMEM((TR, HALF), U32), pltpu.SemaphoreType.DMA((2,))]),
        out_shape=jax.ShapeDtypeStruct((n_rows, HALF), U32),
        compiler_params=pltpu.CompilerParams(dimension_semantics=("arbitrary",)),
        name="dispatch",
    )(pos1, pos2, seg_end, cnt, h2p)


def _expert_kernel(te_ref, nu_ref, xs_ref, wg_ref, wu_ref, wd_ref, os_ref, wg_s, wu_s, wd_s):
    i = pl.program_id(0)

    @pl.when(i >= nu_ref[0])
    def _unused_tile():
        os_ref[...] = jnp.zeros_like(os_ref)

    @pl.when(i < nu_ref[0])
    def _tile():
        e = te_ref[i]
        prev = te_ref[jnp.maximum(i - 1, 0)]

        @pl.when((i == 0) | (e != prev))
        def _new_expert():
            wg_s[...] = wg_ref[0].astype(BF16)
            wu_s[...] = wu_ref[0].astype(BF16)
            wd_s[...] = wd_ref[0].astype(BF16)

        hi, lo = _unpack_bf16_pair(xs_ref[...])
        hi = hi.astype(BF16)
        lo = lo.astype(BF16)
        a = (jnp.dot(hi, wg_s[0:HALF, :], preferred_element_type=F32)
             + jnp.dot(lo, wg_s[HALF:D_MODEL, :], preferred_element_type=F32))
        b = (jnp.dot(hi, wu_s[0:HALF, :], preferred_element_type=F32)
             + jnp.dot(lo, wu_s[HALF:D_MODEL, :], preferred_element_type=F32))
        hid = (a * jax.nn.sigmoid(a) * b).astype(BF16)
        o = jnp.dot(hid, wd_s[...], preferred_element_type=F32).astype(BF16)
        os_ref[...] = _pack_bf16_pair(o[:, :HALF], o[:, HALF:])


def _experts(tile_expert, n_used, xs, w_gate, w_up, w_down):
    n_rows = xs.shape[0]
    n_tiles = n_rows // TR

    def row_map(i, te, nu):
        return (jnp.minimum(i, nu[0] - 1), 0)

    def out_map(i, te, nu):
        return (i, 0)

    def w_map(i, te, nu):
        return (te[i], 0, 0)

    return pl.pallas_call(
        _expert_kernel,
        grid_spec=pltpu.PrefetchScalarGridSpec(
            num_scalar_prefetch=2,
            grid=(n_tiles,),
            in_specs=[pl.BlockSpec((TR, HALF), row_map),
                      pl.BlockSpec((1, D_MODEL, D_EXPERT), w_map),
                      pl.BlockSpec((1, D_MODEL, D_EXPERT), w_map),
                      pl.BlockSpec((1, D_EXPERT, D_MODEL), w_map)],
            out_specs=pl.BlockSpec((TR, HALF), out_map),
            scratch_shapes=[pltpu.VMEM((D_MODEL, D_EXPERT), BF16),
                            pltpu.VMEM((D_MODEL, D_EXPERT), BF16),
                            pltpu.VMEM((D_EXPERT, D_MODEL), BF16)]),
        out_shape=jax.ShapeDtypeStruct((n_rows, HALF), U32),
        compiler_params=pltpu.CompilerParams(dimension_semantics=("arbitrary",),
                                             vmem_limit_bytes=VMEM_LIMIT),
        name="experts",
    )(tile_expert, n_used, xs, w_gate, w_up, w_down)


def _combine_kernel(pos1_ref, pos2_ref, x1_ref, route_ref, p_ref, gple_ref, wpg_ref, wpp_ref,
                    gfin_ref, os_ref, out_ref, gbuf, sem):
    i = pl.program_id(0)
    n = pl.num_programs(0)

    def row_copy(tile, slot, k, r):
        pos_ref = pos1_ref if k == 0 else pos2_ref
        return pltpu.make_async_copy(os_ref.at[pl.ds(pos_ref[tile * TM + r], 1)],
                                     gbuf.at[slot, k, pl.ds(r, 1)], sem.at[slot])

    def issue(tile, slot):
        def body(r, carry):
            row_copy(tile, slot, 0, r).start()
            row_copy(tile, slot, 1, r).start()
            return carry
        lax.fori_loop(0, TM, body, 0, unroll=8)

    slot = i % 2

    @pl.when(i == 0)
    def _first():
        issue(0, 0)

    @pl.when(i + 1 < n)
    def _prefetch():
        issue(i + 1, 1 - slot)

    def drain(r, carry):
        row_copy(i, slot, 0, r).wait()
        row_copy(i, slot, 1, r).wait()
        return carry

    lax.fori_loop(0, TM, drain, 0, unroll=8)

    route = route_ref[...]
    w1 = route[:, 2:3]
    w2 = route[:, 3:4]
    hi1, lo1 = _unpack_bf16_pair(gbuf[slot, 0])
    hi2, lo2 = _unpack_bf16_pair(gbuf[slot, 1])
    y = jnp.concatenate([w1 * hi1 + w2 * hi2, w1 * lo1 + w2 * lo2], axis=1)
    x2 = x1_ref[...] + y
    hg = _rms(x2, gple_ref[...]).astype(BF16)
    gate = jax.nn.sigmoid(jnp.dot(hg, wpg_ref[...], preferred_element_type=F32))
    pe = jnp.dot(p_ref[...].astype(BF16), wpp_ref[...], preferred_element_type=F32)
    x3 = x2 + gate * pe
    out_ref[...] = _rms(x3, gfin_ref[...])


def _combine(pos1, pos2, x1, route, p2d, g_ple, w_pg, w_pp, g_fin, os_):
    t = x1.shape[0]
    full = lambda a: pl.BlockSpec(a.shape, lambda i, *_: (0,) * a.ndim)
    return pl.pallas_call(
        _combine_kernel,
        grid_spec=pltpu.PrefetchScalarGridSpec(
            num_scalar_prefetch=2,
            grid=(t // TM,),
            in_specs=[pl.BlockSpec((TM, D_MODEL), lambda i, *_: (i, 0)),
                      pl.BlockSpec((TM, LANES), lambda i, *_: (i, 0)),
                      pl.BlockSpec((TM, PLE_DIM), lambda i, *_: (i, 0)),
                      full(g_ple), full(w_pg), full(w_pp), full(g_fin),
                      pl.BlockSpec(memory_space=pl.ANY)],
            out_specs=pl.BlockSpec((TM, D_MODEL), lambda i, *_: (i, 0)),
            scratch_shapes=[pltpu.VMEM((2, 2, TM, HALF), U32), pltpu.SemaphoreType.DMA((2,))]),
        out_shape=jax.ShapeDtypeStruct((t, D_MODEL), F32),
        compiler_params=pltpu.CompilerParams(dimension_semantics=("arbitrary",),
                                             vmem_limit_bytes=VMEM_LIMIT),
        name="combine_ple",
    )(pos1, pos2, x1, route, p2d, g_ple, w_pg, w_pp, g_fin, os_)


def kernel(x, p, g_mix, w_in, w_conv, g_sgu, w_spatial, b_spatial, w_out, g_ffn, w_group, b_group,
           w_router, b_router, w_gate, w_up, w_down, g_ple, w_ple_gate, w_ple_proj, g_final):
    bsz, seq, d = x.shape
    t = bsz * seq
    assert w_in.shape[0] == 1, "single-layer block"
    assert d == D_MODEL and seq % TM == 0 and TM % CHUNK == 0
    n_tiles = (N_EXPERTS * (TR - 1) + 2 * t) // TR + 1
    n_rows = n_tiles * TR

    xf = x.reshape(t, d)
    for l in range(1):
        w_rt = jnp.concatenate(
            [w_group[l], jnp.transpose(w_router[l], (1, 0, 2)).reshape(d, N_EXPERTS)], axis=1)
        w_rt = jnp.pad(w_rt, ((0, 0), (0, LANES - w_rt.shape[1]))).astype(BF16)
        b_rt = jnp.pad(jnp.concatenate([b_group[l], b_router[l].reshape(-1)]),
                       (0, LANES - N_GROUPS - N_EXPERTS)).reshape(1, LANES)
        b_sp = jnp.repeat(b_spatial[l].T, SGU_WIDTH // SGU_HEADS, axis=1)

        x1, h2p, route, cnt = _mixer_router(
            xf, g_mix[l].reshape(1, d), w_in[l].astype(BF16), w_conv[l], g_sgu[l].reshape(1, -1),
            w_spatial[l], b_sp, w_out[l].astype(BF16), g_ffn[l].reshape(1, d), w_rt, b_rt, seq)

        counts = cnt[0, ROUTE_LANE0:ROUTE_LANE0 + N_EXPERTS].astype(I32)
        tiles_e = (counts + TR - 1) // TR
        tile_end = jnp.cumsum(tiles_e)
        seg_start = (tile_end - tiles_e) * TR
        n_used = tile_end[-1:].astype(I32)
        id1 = route[:, 0].astype(I32)
        id2 = route[:, 1].astype(I32)
        pos1 = seg_start[id1] + route[:, 4].astype(I32)
        pos2 = seg_start[id2] + route[:, 5].astype(I32)
        tile_ids = jnp.minimum(jnp.arange(n_tiles, dtype=I32), n_used[0] - 1)
        tile_expert = jnp.sum(tile_ids[:, None] >= tile_end[None, :], axis=1).astype(I32)

        xs = _dispatch(pos1, pos2, (tile_end * TR).astype(I32), counts, h2p, n_rows)
        os_ = _experts(tile_expert, n_used, xs,
                       w_gate[l].reshape(N_EXPERTS, d, D_EXPERT),
                       w_up[l].reshape(N_EXPERTS, d, D_EXPERT),
                       w_down[l].reshape(N_EXPERTS, D_EXPERT, d))
        xf = _combine(pos1, pos2, x1, route, p[l].reshape(t, PLE_DIM), g_ple[l].reshape(1, d),
                      w_ple_gate[l].astype(BF16), w_ple_proj[l].astype(BF16),
                      g_final.reshape(1, d), os_)
    return xf.reshape(bsz, seq, d)
```

```python
import functools

import jax
import jax.numpy as jnp
from jax import lax
from jax.experimental import pallas as pl
from jax.experimental.pallas import tpu as pltpu

F32 = jnp.float32
BF16 = jnp.bfloat16
I32 = jnp.int32

EPS = 1e-6
D_MODEL = 1024
CONV_WIDTH = 512
SGU_WIDTH = 512
SGU_HEADS = 8
HEAD_PAIRS = SGU_HEADS // 2
CHUNK = 128
N_GROUPS = 4
EXPERTS_PER_GROUP = 8
N_EXPERTS = N_GROUPS * EXPERTS_PER_GROUP
TOP_K = 2
D_EXPERT = 512
PLE_DIM = 256
LANES = 128
SUBLANES = 8
ROUTE_LANE0 = N_GROUPS

TM = 512
TR = 256
BLOCKS_PER_ROW_TILE = TR // SUBLANES
KOUT = TOP_K * TM + N_EXPERTS * SUBLANES
STAGE_W = D_MODEL + LANES
POS_SPLIT = 32
VMEM_LIMIT = 56 * 1024 * 1024


def _rms(x, g):
    return x * lax.rsqrt(jnp.mean(x * x, axis=-1, keepdims=True) + EPS) * g


def _split_bf16(w):
    hi = w.astype(BF16).astype(F32)
    lo = (w - hi).astype(BF16).astype(F32)
    return hi, lo


def _mixer_router_kernel(tiles_per_seq,
                         x_ref, gmix_ref, win_ref, wconv_ref, gsgu_ref, wsp_ref, bsp_ref,
                         wout_ref, gffn_ref, wrt_ref, brt_ref,
                         x1_ref, route_ref, cnt_ref, stage_ref,
                         wcat_s, tri_s, halo_s, mix_s, hx_s):
    i = pl.program_id(0)
    nch = TM // CHUNK

    @pl.when(i == 0)
    def _init():
        r = lax.broadcasted_iota(I32, (CHUNK, CHUNK), 0)
        c = lax.broadcasted_iota(I32, (CHUNK, CHUNK), 1)
        causal = c <= r
        for j in range(HEAD_PAIRS):
            wa = jnp.where(causal, wsp_ref[2 * j], 0.0)
            wb = jnp.where(causal, wsp_ref[2 * j + 1], 0.0)
            wcat_s[j] = jnp.concatenate([wa, wb], axis=1).astype(BF16)
        rr = lax.broadcasted_iota(I32, (TM, TM), 0)
        cc = lax.broadcasted_iota(I32, (TM, TM), 1)
        tri_s[...] = (cc < rr).astype(BF16)

    @pl.when(i % tiles_per_seq == 0)
    def _seq_start():
        halo_s[...] = jnp.zeros_like(halo_s)

    x = x_ref[...]
    hb = _rms(x, gmix_ref[...]).astype(BF16)

    def proj(k):
        return jnp.dot(hb, win_ref[:, k * 512:(k + 1) * 512], preferred_element_type=F32)

    zc = proj(1) * proj(2)
    row = lax.broadcasted_iota(I32, (TM, CONV_WIDTH), 0)
    h6 = halo_s[6:7, :]
    h7 = halo_s[7:8, :]
    z1 = jnp.where(row == 0, h7, pltpu.roll(zc, 1, 0))
    z2 = jnp.where(row == 0, h6, jnp.where(row == 1, h7, pltpu.roll(zc, 2, 0)))
    conv = z2 * wconv_ref[0:1, :] + z1 * wconv_ref[1:2, :] + zc * wconv_ref[2:3, :]
    halo_s[...] = zc[TM - 8:TM, :]
    mix_s[:, 0:CONV_WIDTH] = (proj(0) * conv).astype(BF16)

    u = jax.nn.gelu(proj(3))
    v = jax.nn.gelu(proj(4))
    vc = v - jnp.mean(v, axis=-1, keepdims=True)
    vn = vc * lax.rsqrt(jnp.mean(vc * vc, axis=-1, keepdims=True) + EPS) * gsgu_ref[...]
    vnb = vn.astype(BF16)
    left = lax.broadcasted_iota(I32, (CHUNK, LANES), 1) < (LANES // 2)
    zero = jnp.zeros((CHUNK, LANES), BF16)
    for j in range(HEAD_PAIRS):
        cols = []
        for c in range(nch):
            s = vnb[c * CHUNK:(c + 1) * CHUNK, j * LANES:(j + 1) * LANES]
            cols.append(jnp.concatenate([jnp.where(left, s, zero), jnp.where(left, zero, s)], axis=0))
        rhs = jnp.concatenate(cols, axis=1)
        res = jnp.dot(wcat_s[j], rhs, preferred_element_type=F32)
        bias = bsp_ref[:, j * LANES:(j + 1) * LANES]
        for c in range(nch):
            mixed = res[:, c * LANES:(c + 1) * LANES] + bias
            uu = u[c * CHUNK:(c + 1) * CHUNK, j * LANES:(j + 1) * LANES]
            mix_s[c * CHUNK:(c + 1) * CHUNK,
                  CONV_WIDTH + j * LANES:CONV_WIDTH + (j + 1) * LANES] = (uu * mixed).astype(BF16)

    x1 = x + jnp.dot(mix_s[...], wout_ref[...], preferred_element_type=F32)
    x1_ref[...] = x1

    h2b = _rms(x1, gffn_ref[...]).astype(BF16)
    logits = jnp.dot(h2b, wrt_ref[...], preferred_element_type=F32) + brt_ref[...]
    lane = lax.broadcasted_iota(I32, (TM, LANES), 1).astype(F32)
    neg = jnp.float32(-jnp.inf)
    big = jnp.float32(1e9)
    is_g = lane < N_GROUPS
    gl = jnp.where(is_g, logits, neg)
    gmax = jnp.max(gl, axis=1, keepdims=True)
    gsum = jnp.sum(jnp.where(is_g, jnp.exp(gl - gmax), 0.0), axis=1, keepdims=True)
    g_w = 1.0 / gsum
    g_idx = jnp.min(jnp.where(gl == gmax, lane, big), axis=1, keepdims=True)
    lo_lane = ROUTE_LANE0 + EXPERTS_PER_GROUP * g_idx
    in_grp = (lane >= lo_lane) & (lane < lo_lane + EXPERTS_PER_GROUP)
    el = jnp.where(in_grp, logits, neg)
    v1 = jnp.max(el, axis=1, keepdims=True)
    i1 = jnp.min(jnp.where(el == v1, lane, big), axis=1, keepdims=True)
    el2 = jnp.where(lane == i1, neg, el)
    v2 = jnp.max(el2, axis=1, keepdims=True)
    i2 = jnp.min(jnp.where(el2 == v2, lane, big), axis=1, keepdims=True)
    e21 = jnp.exp(v2 - v1)
    w1 = g_w / (1.0 + e21)
    w2 = g_w * e21 / (1.0 + e21)

    sel1 = lane == i1
    sel2 = lane == i2
    onehot = jnp.where(sel1 | sel2, 1.0, 0.0)
    counts = jnp.sum(onehot, axis=0, keepdims=True)
    blocks = jnp.ceil(counts * (1.0 / SUBLANES))
    er = lax.broadcasted_iota(I32, (LANES, LANES), 0)
    ec = lax.broadcasted_iota(I32, (LANES, LANES), 1)
    before = (er < ec).astype(BF16)
    run_start = SUBLANES * jnp.dot(jnp.broadcast_to(blocks, (SUBLANES, LANES)).astype(BF16), before,
                                   preferred_element_type=F32)[0:1, :]
    rank = jnp.dot(tri_s[...], onehot.astype(BF16), preferred_element_type=F32) + run_start
    pos1 = jnp.sum(jnp.where(sel1, rank, 0.0), axis=1, keepdims=True)
    pos2 = jnp.sum(jnp.where(sel2, rank, 0.0), axis=1, keepdims=True)

    route = jnp.where(lane == 0, i1 - ROUTE_LANE0, 0.0)
    route = jnp.where(lane == 1, i2 - ROUTE_LANE0, route)
    route = jnp.where(lane == 2, w1, route)
    route = jnp.where(lane == 3, w2, route)
    route = jnp.where(lane == 4, pos1, route)
    route = jnp.where(lane == 5, pos2, route)
    route_ref[...] = route
    cnt_ref[...] = jnp.broadcast_to(counts, cnt_ref.shape)

    p1h = jnp.floor(pos1 * (1.0 / POS_SPLIT))
    p2h = jnp.floor(pos2 * (1.0 / POS_SPLIT))
    pv = jnp.where(lane == 0, p1h, 0.0)
    pv = jnp.where(lane == 1, pos1 - POS_SPLIT * p1h, pv)
    pv = jnp.where(lane == 2, p2h, pv)
    pv = jnp.where(lane == 3, pos2 - POS_SPLIT * p2h, pv)
    pick = (lax.broadcasted_iota(I32, (SUBLANES, LANES), 0)
            == lax.broadcasted_iota(I32, (SUBLANES, LANES), 1)).astype(BF16)
    prow = lax.dot_general(pick, pv.astype(BF16), (((1,), (1,)), ((), ())),
                           preferred_element_type=F32)
    pos1_row = POS_SPLIT * prow[0:1, :] + prow[1:2, :]
    pos2_row = POS_SPLIT * prow[2:3, :] + prow[3:4, :]
    out_row = lax.broadcasted_iota(I32, (KOUT, TM), 0).astype(F32)
    sort_mat = jnp.where((out_row == pos1_row) | (out_row == pos2_row), 1.0, 0.0).astype(BF16)

    w1h, w1l = _split_bf16(w1)
    w2h, w2l = _split_bf16(w2)
    aux = jnp.where(lane == 0, i1 - ROUTE_LANE0, 0.0)
    aux = jnp.where(lane == 1, w1h, aux)
    aux = jnp.where(lane == 2, w1l, aux)
    aux = jnp.where(lane == 3, w2h, aux)
    aux = jnp.where(lane == 4, w2l, aux)
    hx_s[:, 0:D_MODEL] = h2b
    hx_s[:, D_MODEL:STAGE_W] = aux.astype(BF16)
    stage_ref[...] = jnp.dot(sort_mat, hx_s[...], preferred_element_type=F32)


def _mixer_router(x2d, g_mix, w_in, w_conv, g_sgu, w_sp, b_sp, w_out, g_ffn, w_rt, b_rt, seq):
    t = x2d.shape[0]
    nt = t // TM
    full = lambda a: pl.BlockSpec(a.shape, lambda i: (0,) * a.ndim)
    ins = (g_mix, w_in, w_conv, g_sgu, w_sp, b_sp, w_out, g_ffn, w_rt, b_rt)
    return pl.pallas_call(
        functools.partial(_mixer_router_kernel, seq // TM),
        grid=(nt,),
        in_specs=[pl.BlockSpec((TM, D_MODEL), lambda i: (i, 0))] + [full(a) for a in ins],
        out_specs=[pl.BlockSpec((TM, D_MODEL), lambda i: (i, 0)),
                   pl.BlockSpec((TM, LANES), lambda i: (i, 0)),
                   pl.BlockSpec((SUBLANES, LANES), lambda i: (i, 0)),
                   pl.BlockSpec((KOUT, STAGE_W), lambda i: (i, 0))],
        out_shape=[jax.ShapeDtypeStruct((t, D_MODEL), F32),
                   jax.ShapeDtypeStruct((t, LANES), F32),
                   jax.ShapeDtypeStruct((nt * SUBLANES, LANES), F32),
                   jax.ShapeDtypeStruct((nt * KOUT, STAGE_W), F32)],
        scratch_shapes=[pltpu.VMEM((HEAD_PAIRS, CHUNK, 2 * CHUNK), BF16),
                        pltpu.VMEM((TM, TM), BF16),
                        pltpu.VMEM((8, CONV_WIDTH), F32),
                        pltpu.VMEM((TM, D_MODEL), BF16),
                        pltpu.VMEM((TM, STAGE_W), BF16)],
        compiler_params=pltpu.CompilerParams(dimension_semantics=("arbitrary",),
                                             vmem_limit_bytes=VMEM_LIMIT),
        name="mixer_router",
    )(x2d, *ins)


def _expert_kernel(te_ref, nu_ref, src_ref, stage_ref, wg_ref, wu_ref, wd_ref, os_ref,
                   xbuf, wg_s, wu_s, wd_s, sem):
    i = pl.program_id(0)
    n_used = nu_ref[0]
    slot = i % 2

    def block_copy(tile, slot_, j):
        src = pl.multiple_of(src_ref[tile * BLOCKS_PER_ROW_TILE + j] * SUBLANES, SUBLANES)
        return pltpu.make_async_copy(stage_ref.at[pl.ds(src, SUBLANES)],
                                     xbuf.at[slot_, pl.ds(j * SUBLANES, SUBLANES)], sem.at[slot_])

    def issue(tile, slot_):
        for j in range(BLOCKS_PER_ROW_TILE):
            block_copy(tile, slot_, j).start()

    @pl.when(i == 0)
    def _first():
        issue(0, 0)

    @pl.when(i + 1 < n_used)
    def _prefetch():
        issue(i + 1, 1 - slot)

    @pl.when(i >= n_used)
    def _unused_tile():
        os_ref[...] = jnp.zeros_like(os_ref)

    @pl.when(i < n_used)
    def _tile():
        e = te_ref[i]
        prev = te_ref[jnp.maximum(i - 1, 0)]

        @pl.when((i == 0) | (e != prev))
        def _new_expert():
            wg_s[...] = wg_ref[0].astype(BF16)
            wu_s[...] = wu_ref[0].astype(BF16)
            wd_s[...] = wd_ref[0].astype(BF16)

        for j in range(BLOCKS_PER_ROW_TILE):
            block_copy(i, slot, j).wait()

        hb = xbuf[slot, :, 0:D_MODEL].astype(BF16)
        aux = xbuf[slot, :, D_MODEL:STAGE_W]
        first = aux[:, 0:1] == e.astype(F32)
        w = jnp.where(first, aux[:, 1:2] + aux[:, 2:3], aux[:, 3:4] + aux[:, 4:5])
        a = jnp.dot(hb, wg_s[...], preferred_element_type=F32)
        b = jnp.dot(hb, wu_s[...], preferred_element_type=F32)
        hid = (a * jax.nn.sigmoid(a) * b).astype(BF16)
        os_ref[...] = jnp.dot(hid, wd_s[...], preferred_element_type=F32) * w


def _experts(tile_expert, n_used, src_block, stage, w_gate, w_up, w_down, n_tiles):
    def w_map(i, te, nu, src):
        return (te[i], 0, 0)

    return pl.pallas_call(
        _expert_kernel,
        grid_spec=pltpu.PrefetchScalarGridSpec(
            num_scalar_prefetch=3,
            grid=(n_tiles,),
            in_specs=[pl.BlockSpec(memory_space=pl.ANY),
                      pl.BlockSpec((1, D_MODEL, D_EXPERT), w_map),
                      pl.BlockSpec((1, D_MODEL, D_EXPERT), w_map),
                      pl.BlockSpec((1, D_EXPERT, D_MODEL), w_map)],
            out_specs=pl.BlockSpec((TR, D_MODEL), lambda i, *_: (i, 0)),
            scratch_shapes=[pltpu.VMEM((2, TR, STAGE_W), F32),
                            pltpu.VMEM((D_MODEL, D_EXPERT), BF16),
                            pltpu.VMEM((D_MODEL, D_EXPERT), BF16),
                            pltpu.VMEM((D_EXPERT, D_MODEL), BF16),
                            pltpu.SemaphoreType.DMA((2,))]),
        out_shape=jax.ShapeDtypeStruct((n_tiles * TR, D_MODEL), F32),
        compiler_params=pltpu.CompilerParams(dimension_semantics=("arbitrary",),
                                             vmem_limit_bytes=VMEM_LIMIT),
        name="experts",
    )(tile_expert, n_used, src_block, stage, w_gate, w_up, w_down)


def _combine_kernel(dst0_ref, nblk_ref, loc_ref, x1_ref, route_ref, p_ref, gple_ref, wpg_ref,
                    wpp_ref, gfin_ref, os_ref, out_ref, gbuf, sem):
    i = pl.program_id(0)
    n = pl.num_programs(0)
    slot = i % 2

    def for_each_block(tile, slot_, fn):
        def expert_body(e, carry):
            k = tile * N_EXPERTS + e
            d0 = dst0_ref[k]
            l0 = loc_ref[k]

            def block_body(j, carry2):
                src = pl.multiple_of((d0 + j) * SUBLANES, SUBLANES)
                dst = pl.multiple_of((l0 + j) * SUBLANES, SUBLANES)
                fn(pltpu.make_async_copy(os_ref.at[pl.ds(src, SUBLANES)],
                                         gbuf.at[slot_, pl.ds(dst, SUBLANES)], sem.at[slot_]))
                return carry2

            return lax.fori_loop(0, nblk_ref[k], block_body, carry)

        lax.fori_loop(0, N_EXPERTS, expert_body, 0)

    @pl.when(i == 0)
    def _first():
        gbuf[...] = jnp.zeros_like(gbuf)
        for_each_block(0, 0, lambda cp: cp.start())

    @pl.when(i + 1 < n)
    def _prefetch():
        for_each_block(i + 1, 1 - slot, lambda cp: cp.start())

    for_each_block(i, slot, lambda cp: cp.wait())

    route = route_ref[...]
    col = lax.broadcasted_iota(I32, (TM, KOUT), 1).astype(F32)
    unsort = jnp.where((col == route[:, 4:5]) | (col == route[:, 5:6]), 1.0, 0.0).astype(BF16)
    y = jnp.dot(unsort, gbuf[slot].astype(BF16), preferred_element_type=F32)
    x2 = x1_ref[...] + y
    hg = _rms(x2, gple_ref[...]).astype(BF16)
    gate = jax.nn.sigmoid(jnp.dot(hg, wpg_ref[...], preferred_element_type=F32))
    pe = jnp.dot(p_ref[...].astype(BF16), wpp_ref[...], preferred_element_type=F32)
    x3 = x2 + gate * pe
    out_ref[...] = _rms(x3, gfin_ref[...])


def _combine(dst0, nblk, loc, x1, route, p2d, g_ple, w_pg, w_pp, g_fin, os_):
    t = x1.shape[0]
    full = lambda a: pl.BlockSpec(a.shape, lambda i, *_: (0,) * a.ndim)
    return pl.pallas_call(
        _combine_kernel,
        grid_spec=pltpu.PrefetchScalarGridSpec(
            num_scalar_prefetch=3,
            grid=(t // TM,),
            in_specs=[pl.BlockSpec((TM, D_MODEL), lambda i, *_: (i, 0)),
                      pl.BlockSpec((TM, LANES), lambda i, *_: (i, 0)),
                      pl.BlockSpec((TM, PLE_DIM), lambda i, *_: (i, 0)),
                      full(g_ple), full(w_pg), full(w_pp), full(g_fin),
                      pl.BlockSpec(memory_space=pl.ANY)],
            out_specs=pl.BlockSpec((TM, D_MODEL), lambda i, *_: (i, 0)),
            scratch_shapes=[pltpu.VMEM((2, KOUT, D_MODEL), F32),
                            pltpu.SemaphoreType.DMA((2,))]),
        out_shape=jax.ShapeDtypeStruct((t, D_MODEL), F32),
        compiler_params=pltpu.CompilerParams(dimension_semantics=("arbitrary",),
                                             vmem_limit_bytes=VMEM_LIMIT),
        name="combine_ple",
    )(dst0, nblk, loc, x1, route, p2d, g_ple, w_pg, w_pp, g_fin, os_)


def _routing_tables(cnt, n_tiles):
    nt = cnt.shape[0] // SUBLANES
    n = cnt.reshape(nt, SUBLANES, LANES)[:, 0, ROUTE_LANE0:ROUTE_LANE0 + N_EXPERTS].astype(I32)
    nblk = (n + SUBLANES - 1) // SUBLANES
    loc = jnp.cumsum(nblk, axis=1) - nblk
    carry = jnp.cumsum(nblk, axis=0) - nblk
    total = jnp.sum(nblk, axis=0)
    tiles_e = (total + BLOCKS_PER_ROW_TILE - 1) // BLOCKS_PER_ROW_TILE
    tile_end = jnp.cumsum(tiles_e)
    seg0 = (tile_end - tiles_e) * BLOCKS_PER_ROW_TILE
    n_used = tile_end[-1:]
    dst0 = seg0[None, :] + carry

    tile_ids = jnp.minimum(jnp.arange(n_tiles, dtype=I32), n_used[0] - 1)
    tile_expert = jnp.sum(tile_ids[:, None] >= tile_end[None, :], axis=1).astype(I32)

    nb = n_tiles * BLOCKS_PER_ROW_TILE
    b = jnp.arange(nb, dtype=I32)
    e_of_b = jnp.repeat(tile_expert, BLOCKS_PER_ROW_TILE)
    oh_e = (e_of_b[:, None] == jnp.arange(N_EXPERTS, dtype=I32)[None, :]).astype(I32)
    off = b - jnp.sum(oh_e * seg0[None, :], axis=1)
    run_end = jnp.sum(oh_e[:, None, :] * (carry + nblk)[None, :, :], axis=2)
    run_beg = jnp.sum(oh_e[:, None, :] * carry[None, :, :], axis=2)
    run_loc = jnp.sum(oh_e[:, None, :] * loc[None, :, :], axis=2)
    tile_of_b = jnp.sum((run_end <= off[:, None]).astype(I32), axis=1)
    valid = (tile_of_b < nt) & (b < n_used[0] * BLOCKS_PER_ROW_TILE)
    oh_t = (jnp.minimum(tile_of_b, nt - 1)[:, None] == jnp.arange(nt, dtype=I32)[None, :]).astype(I32)
    src = (jnp.sum(oh_t * (run_loc - run_beg), axis=1) + off
           + jnp.minimum(tile_of_b, nt - 1) * (KOUT // SUBLANES))
    zero_block = KOUT // SUBLANES - 1
    src = jnp.where(valid, src, zero_block).astype(I32)
    return (tile_expert, n_used.astype(I32), src,
            dst0.reshape(-1).astype(I32), nblk.reshape(-1).astype(I32), loc.reshape(-1).astype(I32))


def kernel(x, p, g_mix, w_in, w_conv, g_sgu, w_spatial, b_spatial, w_out, g_ffn, w_group, b_group,
           w_router, b_router, w_gate, w_up, w_down, g_ple, w_ple_gate, w_ple_proj, g_final):
    bsz, seq, d = x.shape
    t = bsz * seq
    assert w_in.shape[0] == 1, "single-layer block"
    assert d == D_MODEL and seq % TM == 0 and TM % CHUNK == 0
    nt = t // TM
    max_rows = TOP_K * t + nt * N_EXPERTS * (SUBLANES - 1)
    n_tiles = (max_rows + N_EXPERTS * (TR - 1)) // TR + 1
    l = 0

    w_rt = jnp.concatenate(
        [w_group[l], jnp.transpose(w_router[l], (1, 0, 2)).reshape(d, N_EXPERTS)], axis=1)
    w_rt = jnp.pad(w_rt, ((0, 0), (0, LANES - w_rt.shape[1]))).astype(BF16)
    b_rt = jnp.pad(jnp.concatenate([b_group[l], b_router[l].reshape(-1)]),
                   (0, LANES - N_GROUPS - N_EXPERTS)).reshape(1, LANES)
    b_sp = jnp.repeat(b_spatial[l].T, SGU_WIDTH // SGU_HEADS, axis=1)

    x1, route, cnt, stage = _mixer_router(
        x.reshape(t, d), g_mix[l].reshape(1, d), w_in[l].astype(BF16), w_conv[l],
        g_sgu[l].reshape(1, -1), w_spatial[l], b_sp, w_out[l].astype(BF16),
        g_ffn[l].reshape(1, d), w_rt, b_rt, seq)

    tile_expert, n_used, src, dst0, nblk, loc = _routing_tables(cnt, n_tiles)

    os_ = _experts(tile_expert, n_used, src, stage,
                   w_gate[l].reshape(N_EXPERTS, d, D_EXPERT),
                   w_up[l].reshape(N_EXPERTS, d, D_EXPERT),
                   w_down[l].reshape(N_EXPERTS, D_EXPERT, d), n_tiles)
    out = _combine(dst0, nblk, loc, x1, route, p[l].reshape(t, PLE_DIM), g_ple[l].reshape(1, d),
                   w_ple_gate[l].astype(BF16), w_ple_proj[l].astype(BF16),
                   g_final.reshape(1, d), os_)
    return out.reshape(bsz, seq, d)
```

```python
import functools

import jax
import jax.numpy as jnp
from jax import lax
from jax.experimental import pallas as pl
from jax.experimental.pallas import tpu as pltpu

F32 = jnp.float32
BF16 = jnp.bfloat16
I32 = jnp.int32

EPS = 1e-6
D_MODEL = 1024
CONV_WIDTH = 512
SGU_WIDTH = 512
SGU_HEADS = 8
HEAD_PAIRS = SGU_HEADS // 2
CHUNK = 128
N_GROUPS = 4
EXPERTS_PER_GROUP = 8
N_EXPERTS = N_GROUPS * EXPERTS_PER_GROUP
TOP_K = 2
D_EXPERT = 512
PLE_DIM = 256
LANES = 128
SUBLANES = 8
BLOCK = 2 * SUBLANES
ROUTE_LANE0 = N_GROUPS

TM = 512
TR = 512
BLOCKS_PER_ROW_TILE = TR // BLOCK
KOUT = TOP_K * TM + N_EXPERTS * BLOCK
STAGE_W = D_MODEL + LANES
POS_SPLIT = 32
VMEM_LIMIT = 56 * 1024 * 1024


def _rms(x, g):
    return x * lax.rsqrt(jnp.mean(x * x, axis=-1, keepdims=True) + EPS) * g


def _split_bf16(w):
    hi = w.astype(BF16).astype(F32)
    lo = (w - hi).astype(BF16).astype(F32)
    return hi, lo


def _mixer_router_kernel(tiles_per_seq,
                         x_ref, gmix_ref, win_ref, wconv_ref, gsgu_ref, wsp_ref, bsp_ref,
                         wout_ref, gffn_ref, wrt_ref, brt_ref,
                         x1_ref, route_ref, cnt_ref, stage_ref,
                         wcat_s, tri_s, halo_s, mix_s, hx_s):
    i = pl.program_id(0)
    nch = TM // CHUNK

    @pl.when(i == 0)
    def _init():
        r = lax.broadcasted_iota(I32, (CHUNK, CHUNK), 0)
        c = lax.broadcasted_iota(I32, (CHUNK, CHUNK), 1)
        causal = c <= r
        for j in range(HEAD_PAIRS):
            wa = jnp.where(causal, wsp_ref[2 * j], 0.0)
            wb = jnp.where(causal, wsp_ref[2 * j + 1], 0.0)
            wcat_s[j] = jnp.concatenate([wa, wb], axis=1).astype(BF16)
        rr = lax.broadcasted_iota(I32, (TM, TM), 0)
        cc = lax.broadcasted_iota(I32, (TM, TM), 1)
        tri_s[...] = (cc < rr).astype(BF16)

    @pl.when(i % tiles_per_seq == 0)
    def _seq_start():
        halo_s[...] = jnp.zeros_like(halo_s)

    x = x_ref[...]
    hb = _rms(x, gmix_ref[...]).astype(BF16)

    def proj(k):
        return jnp.dot(hb, win_ref[:, k * 512:(k + 1) * 512], preferred_element_type=F32)

    zc = proj(1) * proj(2)
    row = lax.broadcasted_iota(I32, (TM, CONV_WIDTH), 0)
    h6 = halo_s[6:7, :]
    h7 = halo_s[7:8, :]
    z1 = jnp.where(row == 0, h7, pltpu.roll(zc, 1, 0))
    z2 = jnp.where(row == 0, h6, jnp.where(row == 1, h7, pltpu.roll(zc, 2, 0)))
    conv = z2 * wconv_ref[0:1, :] + z1 * wconv_ref[1:2, :] + zc * wconv_ref[2:3, :]
    halo_s[...] = zc[TM - 8:TM, :]
    mix_s[:, 0:CONV_WIDTH] = (proj(0) * conv).astype(BF16)

    u = jax.nn.gelu(proj(3))
    v = jax.nn.gelu(proj(4))
    vc = v - jnp.mean(v, axis=-1, keepdims=True)
    vn = vc * lax.rsqrt(jnp.mean(vc * vc, axis=-1, keepdims=True) + EPS) * gsgu_ref[...]
    vnb = vn.astype(BF16)
    left = lax.broadcasted_iota(I32, (CHUNK, LANES), 1) < (LANES // 2)
    zero = jnp.zeros((CHUNK, LANES), BF16)
    for j in range(HEAD_PAIRS):
        cols = []
        for c in range(nch):
            s = vnb[c * CHUNK:(c + 1) * CHUNK, j * LANES:(j + 1) * LANES]
            cols.append(jnp.concatenate([jnp.where(left, s, zero), jnp.where(left, zero, s)], axis=0))
        rhs = jnp.concatenate(cols, axis=1)
        res = jnp.dot(wcat_s[j], rhs, preferred_element_type=F32)
        bias = bsp_ref[:, j * LANES:(j + 1) * LANES]
        for c in range(nch):
            mixed = res[:, c * LANES:(c + 1) * LANES] + bias
            uu = u[c * CHUNK:(c + 1) * CHUNK, j * LANES:(j + 1) * LANES]
            mix_s[c * CHUNK:(c + 1) * CHUNK,
                  CONV_WIDTH + j * LANES:CONV_WIDTH + (j + 1) * LANES] = (uu * mixed).astype(BF16)

    x1 = x + jnp.dot(mix_s[...], wout_ref[...], preferred_element_type=F32)
    x1_ref[...] = x1

    h2b = _rms(x1, gffn_ref[...]).astype(BF16)
    logits = jnp.dot(h2b, wrt_ref[...], preferred_element_type=F32) + brt_ref[...]
    lane = lax.broadcasted_iota(I32, (TM, LANES), 1).astype(F32)
    neg = jnp.float32(-jnp.inf)
    big = jnp.float32(1e9)
    is_g = lane < N_GROUPS
    gl = jnp.where(is_g, logits, neg)
    gmax = jnp.max(gl, axis=1, keepdims=True)
    gsum = jnp.sum(jnp.where(is_g, jnp.exp(gl - gmax), 0.0), axis=1, keepdims=True)
    g_w = 1.0 / gsum
    g_idx = jnp.min(jnp.where(gl == gmax, lane, big), axis=1, keepdims=True)
    lo_lane = ROUTE_LANE0 + EXPERTS_PER_GROUP * g_idx
    in_grp = (lane >= lo_lane) & (lane < lo_lane + EXPERTS_PER_GROUP)
    el = jnp.where(in_grp, logits, neg)
    v1 = jnp.max(el, axis=1, keepdims=True)
    i1 = jnp.min(jnp.where(el == v1, lane, big), axis=1, keepdims=True)
    el2 = jnp.where(lane == i1, neg, el)
    v2 = jnp.max(el2, axis=1, keepdims=True)
    i2 = jnp.min(jnp.where(el2 == v2, lane, big), axis=1, keepdims=True)
    e21 = jnp.exp(v2 - v1)
    w1 = g_w / (1.0 + e21)
    w2 = g_w * e21 / (1.0 + e21)

    sel1 = lane == i1
    sel2 = lane == i2
    onehot = jnp.where(sel1 | sel2, 1.0, 0.0)
    counts = jnp.sum(onehot, axis=0, keepdims=True)
    blocks = jnp.ceil(counts * (1.0 / BLOCK))
    er = lax.broadcasted_iota(I32, (LANES, LANES), 0)
    ec = lax.broadcasted_iota(I32, (LANES, LANES), 1)
    before = (er < ec).astype(BF16)
    run_start = BLOCK * jnp.dot(jnp.broadcast_to(blocks, (SUBLANES, LANES)).astype(BF16), before,
                                   preferred_element_type=F32)[0:1, :]
    rank = jnp.dot(tri_s[...], onehot.astype(BF16), preferred_element_type=F32) + run_start
    pos1 = jnp.sum(jnp.where(sel1, rank, 0.0), axis=1, keepdims=True)
    pos2 = jnp.sum(jnp.where(sel2, rank, 0.0), axis=1, keepdims=True)

    route = jnp.where(lane == 0, i1 - ROUTE_LANE0, 0.0)
    route = jnp.where(lane == 1, i2 - ROUTE_LANE0, route)
    route = jnp.where(lane == 2, w1, route)
    route = jnp.where(lane == 3, w2, route)
    route = jnp.where(lane == 4, pos1, route)
    route = jnp.where(lane == 5, pos2, route)
    route_ref[...] = route
    cnt_ref[...] = jnp.broadcast_to(counts, cnt_ref.shape)

    p1h = jnp.floor(pos1 * (1.0 / POS_SPLIT))
    p2h = jnp.floor(pos2 * (1.0 / POS_SPLIT))
    pv = jnp.where(lane == 0, p1h, 0.0)
    pv = jnp.where(lane == 1, pos1 - POS_SPLIT * p1h, pv)
    pv = jnp.where(lane == 2, p2h, pv)
    pv = jnp.where(lane == 3, pos2 - POS_SPLIT * p2h, pv)
    pick = (lax.broadcasted_iota(I32, (SUBLANES, LANES), 0)
            == lax.broadcasted_iota(I32, (SUBLANES, LANES), 1)).astype(BF16)
    prow = lax.dot_general(pick, pv.astype(BF16), (((1,), (1,)), ((), ())),
                           preferred_element_type=F32)
    pos1_row = POS_SPLIT * prow[0:1, :] + prow[1:2, :]
    pos2_row = POS_SPLIT * prow[2:3, :] + prow[3:4, :]
    out_row = lax.broadcasted_iota(I32, (KOUT, TM), 0).astype(F32)
    sort_mat = jnp.where((out_row == pos1_row) | (out_row == pos2_row), 1.0, 0.0).astype(BF16)

    w1h, w1l = _split_bf16(w1)
    w2h, w2l = _split_bf16(w2)
    aux = jnp.where(lane == 0, i1 - ROUTE_LANE0, 0.0)
    aux = jnp.where(lane == 1, w1h, aux)
    aux = jnp.where(lane == 2, w1l, aux)
    aux = jnp.where(lane == 3, w2h, aux)
    aux = jnp.where(lane == 4, w2l, aux)
    hx_s[:, 0:D_MODEL] = h2b
    hx_s[:, D_MODEL:STAGE_W] = aux.astype(BF16)
    stage_ref[...] = jnp.dot(sort_mat, hx_s[...], preferred_element_type=F32).astype(BF16)


def _mixer_router(x2d, g_mix, w_in, w_conv, g_sgu, w_sp, b_sp, w_out, g_ffn, w_rt, b_rt, seq):
    t = x2d.shape[0]
    nt = t // TM
    full = lambda a: pl.BlockSpec(a.shape, lambda i: (0,) * a.ndim)
    ins = (g_mix, w_in, w_conv, g_sgu, w_sp, b_sp, w_out, g_ffn, w_rt, b_rt)
    return pl.pallas_call(
        functools.partial(_mixer_router_kernel, seq // TM),
        grid=(nt,),
        in_specs=[pl.BlockSpec((TM, D_MODEL), lambda i: (i, 0))] + [full(a) for a in ins],
        out_specs=[pl.BlockSpec((TM, D_MODEL), lambda i: (i, 0)),
                   pl.BlockSpec((TM, LANES), lambda i: (i, 0)),
                   pl.BlockSpec((SUBLANES, LANES), lambda i: (i, 0)),
                   pl.BlockSpec((KOUT, STAGE_W), lambda i: (i, 0))],
        out_shape=[jax.ShapeDtypeStruct((t, D_MODEL), F32),
                   jax.ShapeDtypeStruct((t, LANES), F32),
                   jax.ShapeDtypeStruct((nt * SUBLANES, LANES), F32),
                   jax.ShapeDtypeStruct((nt * KOUT, STAGE_W), BF16)],
        scratch_shapes=[pltpu.VMEM((HEAD_PAIRS, CHUNK, 2 * CHUNK), BF16),
                        pltpu.VMEM((TM, TM), BF16),
                        pltpu.VMEM((8, CONV_WIDTH), F32),
                        pltpu.VMEM((TM, D_MODEL), BF16),
                        pltpu.VMEM((TM, STAGE_W), BF16)],
        compiler_params=pltpu.CompilerParams(dimension_semantics=("arbitrary",),
                                             vmem_limit_bytes=VMEM_LIMIT),
        name="mixer_router",
    )(x2d, *ins)


def _expert_kernel(te_ref, nu_ref, nxt_ref, src_ref, stage_ref, wg_hbm, wu_hbm, wd_hbm, os_ref,
                   xbuf, wg_f, wu_f, wd_f, wg_s, wu_s, wd_s, sem, wsem):
    i = pl.program_id(0)
    n_used = nu_ref[0]
    slot = i % 2

    def block_copy(tile, slot_, j):
        src = pl.multiple_of(src_ref[tile * BLOCKS_PER_ROW_TILE + j] * BLOCK, BLOCK)
        return pltpu.make_async_copy(stage_ref.at[pl.ds(src, BLOCK)],
                                     xbuf.at[slot_, pl.ds(j * BLOCK, BLOCK)], sem.at[slot_])

    def issue(tile, slot_):
        for j in range(BLOCKS_PER_ROW_TILE):
            block_copy(tile, slot_, j).start()

    def weight_copies(e):
        return (pltpu.make_async_copy(wg_hbm.at[e], wg_f, wsem.at[0]),
                pltpu.make_async_copy(wu_hbm.at[e], wu_f, wsem.at[1]),
                pltpu.make_async_copy(wd_hbm.at[e], wd_f, wsem.at[2]))

    @pl.when(i == 0)
    def _first():
        issue(0, 0)
        for cp in weight_copies(te_ref[0]):
            cp.start()

    @pl.when(i + 1 < n_used)
    def _prefetch():
        issue(i + 1, 1 - slot)

    @pl.when(i >= n_used)
    def _unused_tile():
        os_ref[...] = jnp.zeros_like(os_ref)

    @pl.when(i < n_used)
    def _tile():
        e = te_ref[i]
        prev = te_ref[jnp.maximum(i - 1, 0)]

        @pl.when((i == 0) | (e != prev))
        def _new_expert():
            for cp in weight_copies(e):
                cp.wait()
            wg_s[...] = wg_f[...].astype(BF16)
            wu_s[...] = wu_f[...].astype(BF16)
            wd_s[...] = wd_f[...].astype(BF16)
            nxt = nxt_ref[i]

            @pl.when(nxt >= 0)
            def _next_weights():
                for cp in weight_copies(nxt):
                    cp.start()

        for j in range(BLOCKS_PER_ROW_TILE):
            block_copy(i, slot, j).wait()

        hb = xbuf[slot, :, 0:D_MODEL]
        aux = xbuf[slot, :, D_MODEL:STAGE_W].astype(F32)
        first = aux[:, 0:1] == e.astype(F32)
        w = jnp.where(first, aux[:, 1:2] + aux[:, 2:3], aux[:, 3:4] + aux[:, 4:5])
        a = jnp.dot(hb, wg_s[...], preferred_element_type=F32)
        b = jnp.dot(hb, wu_s[...], preferred_element_type=F32)
        hid = (a * jax.nn.sigmoid(a) * b).astype(BF16)
        os_ref[...] = (jnp.dot(hid, wd_s[...], preferred_element_type=F32) * w).astype(BF16)


def _experts(tile_expert, n_used, next_expert, src_block, stage, w_gate, w_up, w_down, n_tiles):
    any_spec = pl.BlockSpec(memory_space=pl.ANY)
    return pl.pallas_call(
        _expert_kernel,
        grid_spec=pltpu.PrefetchScalarGridSpec(
            num_scalar_prefetch=4,
            grid=(n_tiles,),
            in_specs=[any_spec, any_spec, any_spec, any_spec],
            out_specs=pl.BlockSpec((TR, D_MODEL), lambda i, *_: (i, 0)),
            scratch_shapes=[pltpu.VMEM((2, TR, STAGE_W), BF16),
                            pltpu.VMEM((D_MODEL, D_EXPERT), F32),
                            pltpu.VMEM((D_MODEL, D_EXPERT), F32),
                            pltpu.VMEM((D_EXPERT, D_MODEL), F32),
                            pltpu.VMEM((D_MODEL, D_EXPERT), BF16),
                            pltpu.VMEM((D_MODEL, D_EXPERT), BF16),
                            pltpu.VMEM((D_EXPERT, D_MODEL), BF16),
                            pltpu.SemaphoreType.DMA((2,)),
                            pltpu.SemaphoreType.DMA((3,))]),
        out_shape=jax.ShapeDtypeStruct((n_tiles * TR, D_MODEL), BF16),
        compiler_params=pltpu.CompilerParams(dimension_semantics=("arbitrary",),
                                             vmem_limit_bytes=VMEM_LIMIT),
        name="experts",
    )(tile_expert, n_used, next_expert, src_block, stage, w_gate, w_up, w_down)


def _combine_kernel(dst0_ref, nblk_ref, loc_ref, x1_ref, route_ref, p_ref, gple_ref, wpg_ref,
                    wpp_ref, gfin_ref, os_ref, out_ref, gbuf, sem):
    i = pl.program_id(0)
    n = pl.num_programs(0)
    slot = i % 2

    def for_each_block(tile, slot_, fn):
        def expert_body(e, carry):
            k = tile * N_EXPERTS + e
            d0 = dst0_ref[k]
            l0 = loc_ref[k]

            def block_body(j, carry2):
                src = pl.multiple_of((d0 + j) * BLOCK, BLOCK)
                dst = pl.multiple_of((l0 + j) * BLOCK, BLOCK)
                fn(pltpu.make_async_copy(os_ref.at[pl.ds(src, BLOCK)],
                                         gbuf.at[slot_, pl.ds(dst, BLOCK)], sem.at[slot_]))
                return carry2

            return lax.fori_loop(0, nblk_ref[k], block_body, carry)

        lax.fori_loop(0, N_EXPERTS, expert_body, 0)

    @pl.when(i == 0)
    def _first():
        gbuf[...] = jnp.zeros_like(gbuf)
        for_each_block(0, 0, lambda cp: cp.start())

    @pl.when(i + 1 < n)
    def _prefetch():
        for_each_block(i + 1, 1 - slot, lambda cp: cp.start())

    for_each_block(i, slot, lambda cp: cp.wait())

    route = route_ref[...]
    col = lax.broadcasted_iota(I32, (TM, KOUT), 1).astype(F32)
    unsort = jnp.where((col == route[:, 4:5]) | (col == route[:, 5:6]), 1.0, 0.0).astype(BF16)
    y = jnp.dot(unsort, gbuf[slot], preferred_element_type=F32)
    x2 = x1_ref[...] + y
    hg = _rms(x2, gple_ref[...]).astype(BF16)
    gate = jax.nn.sigmoid(jnp.dot(hg, wpg_ref[...], preferred_element_type=F32))
    pe = jnp.dot(p_ref[...].astype(BF16), wpp_ref[...], preferred_element_type=F32)
    x3 = x2 + gate * pe
    out_ref[...] = _rms(x3, gfin_ref[...])


def _combine(dst0, nblk, loc, x1, route, p2d, g_ple, w_pg, w_pp, g_fin, os_):
    t = x1.shape[0]
    full = lambda a: pl.BlockSpec(a.shape, lambda i, *_: (0,) * a.ndim)
    return pl.pallas_call(
        _combine_kernel,
        grid_spec=pltpu.PrefetchScalarGridSpec(
            num_scalar_prefetch=3,
            grid=(t // TM,),
            in_specs=[pl.BlockSpec((TM, D_MODEL), lambda i, *_: (i, 0)),
                      pl.BlockSpec((TM, LANES), lambda i, *_: (i, 0)),
                      pl.BlockSpec((TM, PLE_DIM), lambda i, *_: (i, 0)),
                      full(g_ple), full(w_pg), full(w_pp), full(g_fin),
                      pl.BlockSpec(memory_space=pl.ANY)],
            out_specs=pl.BlockSpec((TM, D_MODEL), lambda i, *_: (i, 0)),
            scratch_shapes=[pltpu.VMEM((2, KOUT, D_MODEL), BF16),
                            pltpu.SemaphoreType.DMA((2,))]),
        out_shape=jax.ShapeDtypeStruct((t, D_MODEL), F32),
        compiler_params=pltpu.CompilerParams(dimension_semantics=("arbitrary",),
                                             vmem_limit_bytes=VMEM_LIMIT),
        name="combine_ple",
    )(dst0, nblk, loc, x1, route, p2d, g_ple, w_pg, w_pp, g_fin, os_)


def _routing_tables(cnt, n_tiles):
    nt = cnt.shape[0] // SUBLANES
    n = cnt.reshape(nt, SUBLANES, LANES)[:, 0, ROUTE_LANE0:ROUTE_LANE0 + N_EXPERTS].astype(I32)
    nblk = (n + BLOCK - 1) // BLOCK
    loc = jnp.cumsum(nblk, axis=1) - nblk
    carry = jnp.cumsum(nblk, axis=0) - nblk
    total = jnp.sum(nblk, axis=0)
    tiles_e = (total + BLOCKS_PER_ROW_TILE - 1) // BLOCKS_PER_ROW_TILE
    tile_end = jnp.cumsum(tiles_e)
    seg0 = (tile_end - tiles_e) * BLOCKS_PER_ROW_TILE
    n_used = tile_end[-1:]
    dst0 = seg0[None, :] + carry

    tile_ids = jnp.minimum(jnp.arange(n_tiles, dtype=I32), n_used[0] - 1)
    tile_expert = jnp.sum(tile_ids[:, None] >= tile_end[None, :], axis=1).astype(I32)

    nb = n_tiles * BLOCKS_PER_ROW_TILE
    b = jnp.arange(nb, dtype=I32)
    e_of_b = jnp.repeat(tile_expert, BLOCKS_PER_ROW_TILE)
    oh_e = (e_of_b[:, None] == jnp.arange(N_EXPERTS, dtype=I32)[None, :]).astype(I32)
    off = b - jnp.sum(oh_e * seg0[None, :], axis=1)
    run_end = jnp.sum(oh_e[:, None, :] * (carry + nblk)[None, :, :], axis=2)
    run_beg = jnp.sum(oh_e[:, None, :] * carry[None, :, :], axis=2)
    run_loc = jnp.sum(oh_e[:, None, :] * loc[None, :, :], axis=2)
    tile_of_b = jnp.sum((run_end <= off[:, None]).astype(I32), axis=1)
    valid = (tile_of_b < nt) & (b < n_used[0] * BLOCKS_PER_ROW_TILE)
    oh_t = (jnp.minimum(tile_of_b, nt - 1)[:, None] == jnp.arange(nt, dtype=I32)[None, :]).astype(I32)
    src = (jnp.sum(oh_t * (run_loc - run_beg), axis=1) + off
           + jnp.minimum(tile_of_b, nt - 1) * (KOUT // BLOCK))
    zero_block = KOUT // BLOCK - 1
    src = jnp.where(valid, src, zero_block).astype(I32)

    ids = jnp.arange(N_EXPERTS, dtype=I32)
    later = (tiles_e > 0)[None, :] & (ids[None, :] > ids[:, None])
    next_used = jnp.min(jnp.where(later, ids[None, :], N_EXPERTS), axis=1)
    next_used = jnp.where(next_used == N_EXPERTS, -1, next_used)
    next_expert = jnp.sum((tile_expert[:, None] == ids[None, :]) * next_used[None, :], axis=1)
    return (tile_expert, n_used.astype(I32), next_expert.astype(I32), src,
            dst0.reshape(-1).astype(I32), nblk.reshape(-1).astype(I32), loc.reshape(-1).astype(I32))


def kernel(x, p, g_mix, w_in, w_conv, g_sgu, w_spatial, b_spatial, w_out, g_ffn, w_group, b_group,
           w_router, b_router, w_gate, w_up, w_down, g_ple, w_ple_gate, w_ple_proj, g_final):
    bsz, seq, d = x.shape
    t = bsz * seq
    assert w_in.shape[0] == 1, "single-layer block"
    assert d == D_MODEL and seq % TM == 0 and TM % CHUNK == 0
    nt = t // TM
    max_rows = TOP_K * t + nt * N_EXPERTS * (BLOCK - 1)
    n_tiles = (max_rows + N_EXPERTS * (TR - 1)) // TR + 1
    l = 0

    w_rt = jnp.concatenate(
        [w_group[l], jnp.transpose(w_router[l], (1, 0, 2)).reshape(d, N_EXPERTS)], axis=1)
    w_rt = jnp.pad(w_rt, ((0, 0), (0, LANES - w_rt.shape[1]))).astype(BF16)
    b_rt = jnp.pad(jnp.concatenate([b_group[l], b_router[l].reshape(-1)]),
                   (0, LANES - N_GROUPS - N_EXPERTS)).reshape(1, LANES)
    b_sp = jnp.repeat(b_spatial[l].T, SGU_WIDTH // SGU_HEADS, axis=1)

    x1, route, cnt, stage = _mixer_router(
        x.reshape(t, d), g_mix[l].reshape(1, d), w_in[l].astype(BF16), w_conv[l],
        g_sgu[l].reshape(1, -1), w_spatial[l], b_sp, w_out[l].astype(BF16),
        g_ffn[l].reshape(1, d), w_rt, b_rt, seq)

    tile_expert, n_used, next_expert, src, dst0, nblk, loc = _routing_tables(cnt, n_tiles)

    os_ = _experts(tile_expert, n_used, next_expert, src, stage,
                   w_gate[l].reshape(N_EXPERTS, d, D_EXPERT),
                   w_up[l].reshape(N_EXPERTS, d, D_EXPERT),
                   w_down[l].reshape(N_EXPERTS, D_EXPERT, d), n_tiles)
    out = _combine(dst0, nblk, loc, x1, route, p[l].reshape(t, PLE_DIM), g_ple[l].reshape(1, d),
                   w_ple_gate[l].astype(BF16), w_ple_proj[l].astype(BF16),
                   g_final.reshape(1, d), os_)
    return out.reshape(bsz, seq, d)
```

```python
import functools

import jax
import jax.numpy as jnp
from jax import lax
from jax.experimental import pallas as pl
from jax.experimental.pallas import tpu as pltpu

F32 = jnp.float32
BF16 = jnp.bfloat16
I32 = jnp.int32

EPS = 1e-6
D_MODEL = 1024
CONV_WIDTH = 512
SGU_WIDTH = 512
SGU_HEADS = 8
HEAD_PAIRS = SGU_HEADS // 2
CHUNK = 128
N_GROUPS = 4
EXPERTS_PER_GROUP = 8
N_EXPERTS = N_GROUPS * EXPERTS_PER_GROUP
TOP_K = 2
D_EXPERT = 512
PLE_DIM = 256
LANES = 128
SUBLANES = 8
BLOCK = 2 * SUBLANES
ROUTE_LANE0 = N_GROUPS

TM = 512
TR = 512
BLOCKS_PER_ROW_TILE = TR // BLOCK
KOUT = TOP_K * TM + N_EXPERTS * BLOCK
TAIL = 128
SORT_MAIN = KOUT - 2 * TAIL
POS_SPLIT = 32
VMEM_LIMIT = 56 * 1024 * 1024


def _rms(x, g):
    return x * lax.rsqrt(jnp.mean(x * x, axis=-1, keepdims=True) + EPS) * g


def _mixer_router_kernel(tiles_per_seq,
                         x_ref, gmix_ref, win_ref, wconv_ref, gsgu_ref, wsp_ref, bsp_ref,
                         wout_ref, gffn_ref, wrt_ref, brt_ref,
                         x1_ref, route_ref, cnt_ref, stage_ref,
                         wcat_s, tri_s, halo_s, mix_s):
    i = pl.program_id(0)
    nch = TM // CHUNK

    @pl.when(i == 0)
    def _init():
        r = lax.broadcasted_iota(I32, (CHUNK, CHUNK), 0)
        c = lax.broadcasted_iota(I32, (CHUNK, CHUNK), 1)
        causal = c <= r
        for j in range(HEAD_PAIRS):
            wa = jnp.where(causal, wsp_ref[2 * j], 0.0)
            wb = jnp.where(causal, wsp_ref[2 * j + 1], 0.0)
            wcat_s[j] = jnp.concatenate([wa, wb], axis=1).astype(BF16)
        rr = lax.broadcasted_iota(I32, (TM, TM), 0)
        cc = lax.broadcasted_iota(I32, (TM, TM), 1)
        tri_s[...] = (cc < rr).astype(BF16)

    @pl.when(i % tiles_per_seq == 0)
    def _seq_start():
        halo_s[...] = jnp.zeros_like(halo_s)

    x = x_ref[...]
    hb = _rms(x, gmix_ref[...]).astype(BF16)

    def proj(k):
        return jnp.dot(hb, win_ref[:, k * 512:(k + 1) * 512], preferred_element_type=F32)

    zc = proj(1) * proj(2)
    row = lax.broadcasted_iota(I32, (TM, CONV_WIDTH), 0)
    h6 = halo_s[6:7, :]
    h7 = halo_s[7:8, :]
    z1 = jnp.where(row == 0, h7, pltpu.roll(zc, 1, 0))
    z2 = jnp.where(row == 0, h6, jnp.where(row == 1, h7, pltpu.roll(zc, 2, 0)))
    conv = z2 * wconv_ref[0:1, :] + z1 * wconv_ref[1:2, :] + zc * wconv_ref[2:3, :]
    halo_s[...] = zc[TM - 8:TM, :]
    mix_s[:, 0:CONV_WIDTH] = (proj(0) * conv).astype(BF16)

    u = jax.nn.gelu(proj(3))
    v = jax.nn.gelu(proj(4))
    vc = v - jnp.mean(v, axis=-1, keepdims=True)
    vn = vc * lax.rsqrt(jnp.mean(vc * vc, axis=-1, keepdims=True) + EPS) * gsgu_ref[...]
    vnb = vn.astype(BF16)
    left = lax.broadcasted_iota(I32, (CHUNK, LANES), 1) < (LANES // 2)
    zero = jnp.zeros((CHUNK, LANES), BF16)
    for j in range(HEAD_PAIRS):
        cols = []
        for c in range(nch):
            s = vnb[c * CHUNK:(c + 1) * CHUNK, j * LANES:(j + 1) * LANES]
            cols.append(jnp.concatenate([jnp.where(left, s, zero), jnp.where(left, zero, s)], axis=0))
        rhs = jnp.concatenate(cols, axis=1)
        res = jnp.dot(wcat_s[j], rhs, preferred_element_type=F32)
        bias = bsp_ref[:, j * LANES:(j + 1) * LANES]
        for c in range(nch):
            mixed = res[:, c * LANES:(c + 1) * LANES] + bias
            uu = u[c * CHUNK:(c + 1) * CHUNK, j * LANES:(j + 1) * LANES]
            mix_s[c * CHUNK:(c + 1) * CHUNK,
                  CONV_WIDTH + j * LANES:CONV_WIDTH + (j + 1) * LANES] = (uu * mixed).astype(BF16)

    x1 = x + jnp.dot(mix_s[...], wout_ref[...], preferred_element_type=F32)
    x1_ref[...] = x1

    h2b = _rms(x1, gffn_ref[...]).astype(BF16)
    logits = jnp.dot(h2b, wrt_ref[...], preferred_element_type=F32) + brt_ref[...]
    lane = lax.broadcasted_iota(I32, (TM, LANES), 1).astype(F32)
    neg = jnp.float32(-jnp.inf)
    big = jnp.float32(1e9)
    is_g = lane < N_GROUPS
    gl = jnp.where(is_g, logits, neg)
    gmax = jnp.max(gl, axis=1, keepdims=True)
    gsum = jnp.sum(jnp.where(is_g, jnp.exp(gl - gmax), 0.0), axis=1, keepdims=True)
    g_w = 1.0 / gsum
    g_idx = jnp.min(jnp.where(gl == gmax, lane, big), axis=1, keepdims=True)
    lo_lane = ROUTE_LANE0 + EXPERTS_PER_GROUP * g_idx
    in_grp = (lane >= lo_lane) & (lane < lo_lane + EXPERTS_PER_GROUP)
    el = jnp.where(in_grp, logits, neg)
    v1 = jnp.max(el, axis=1, keepdims=True)
    i1 = jnp.min(jnp.where(el == v1, lane, big), axis=1, keepdims=True)
    el2 = jnp.where(lane == i1, neg, el)
    v2 = jnp.max(el2, axis=1, keepdims=True)
    i2 = jnp.min(jnp.where(el2 == v2, lane, big), axis=1, keepdims=True)
    e21 = jnp.exp(v2 - v1)
    w1 = g_w / (1.0 + e21)
    w2 = g_w * e21 / (1.0 + e21)

    sel1 = lane == i1
    sel2 = lane == i2
    onehot = jnp.where(sel1 | sel2, 1.0, 0.0)
    counts = jnp.sum(onehot, axis=0, keepdims=True)
    blocks = jnp.ceil(counts * (1.0 / BLOCK))
    er = lax.broadcasted_iota(I32, (LANES, LANES), 0)
    ec = lax.broadcasted_iota(I32, (LANES, LANES), 1)
    before = (er < ec).astype(BF16)
    run_start = BLOCK * jnp.dot(jnp.broadcast_to(blocks, (SUBLANES, LANES)).astype(BF16), before,
                                   preferred_element_type=F32)[0:1, :]
    rank = jnp.dot(tri_s[...], onehot.astype(BF16), preferred_element_type=F32) + run_start
    pos1 = jnp.sum(jnp.where(sel1, rank, 0.0), axis=1, keepdims=True)
    pos2 = jnp.sum(jnp.where(sel2, rank, 0.0), axis=1, keepdims=True)

    route = jnp.where(lane == 0, i1 - ROUTE_LANE0, 0.0)
    route = jnp.where(lane == 1, i2 - ROUTE_LANE0, route)
    route = jnp.where(lane == 2, w1, route)
    route = jnp.where(lane == 3, w2, route)
    route = jnp.where(lane == 4, pos1, route)
    route = jnp.where(lane == 5, pos2, route)
    route_ref[...] = route
    cnt_ref[...] = jnp.broadcast_to(counts, cnt_ref.shape)

    p1h = jnp.floor(pos1 * (1.0 / POS_SPLIT))
    p2h = jnp.floor(pos2 * (1.0 / POS_SPLIT))
    pv = jnp.where(lane == 0, p1h, 0.0)
    pv = jnp.where(lane == 1, pos1 - POS_SPLIT * p1h, pv)
    pv = jnp.where(lane == 2, p2h, pv)
    pv = jnp.where(lane == 3, pos2 - POS_SPLIT * p2h, pv)
    pick = (lax.broadcasted_iota(I32, (SUBLANES, LANES), 0)
            == lax.broadcasted_iota(I32, (SUBLANES, LANES), 1)).astype(BF16)
    prow = lax.dot_general(pick, pv.astype(BF16), (((1,), (1,)), ((), ())),
                           preferred_element_type=F32)
    pos1_row = POS_SPLIT * prow[0:1, :] + prow[1:2, :]
    pos2_row = POS_SPLIT * prow[2:3, :] + prow[3:4, :]

    def sort_rows(lo, hi):
        out_row = (lax.broadcasted_iota(I32, (hi - lo, TM), 0) + lo).astype(F32)
        sort_mat = jnp.where((out_row == pos1_row) | (out_row == pos2_row), 1.0, 0.0).astype(BF16)
        stage_ref[lo:hi, :] = jnp.dot(sort_mat, h2b, preferred_element_type=F32).astype(BF16)

    sort_rows(0, SORT_MAIN)
    used_rows = BLOCK * jnp.sum(blocks)
    for lo in range(SORT_MAIN, KOUT, TAIL):
        @pl.when(used_rows > lo)
        def _tail(lo=lo):
            sort_rows(lo, lo + TAIL)

        @pl.when(used_rows <= lo)
        def _empty_tail(lo=lo):
            stage_ref[lo:lo + TAIL, :] = jnp.zeros((TAIL, D_MODEL), BF16)


def _mixer_router(x2d, g_mix, w_in, w_conv, g_sgu, w_sp, b_sp, w_out, g_ffn, w_rt, b_rt, seq):
    t = x2d.shape[0]
    nt = t // TM
    full = lambda a: pl.BlockSpec(a.shape, lambda i: (0,) * a.ndim)
    ins = (g_mix, w_in, w_conv, g_sgu, w_sp, b_sp, w_out, g_ffn, w_rt, b_rt)
    return pl.pallas_call(
        functools.partial(_mixer_router_kernel, seq // TM),
        grid=(nt,),
        in_specs=[pl.BlockSpec((TM, D_MODEL), lambda i: (i, 0))] + [full(a) for a in ins],
        out_specs=[pl.BlockSpec((TM, D_MODEL), lambda i: (i, 0)),
                   pl.BlockSpec((TM, LANES), lambda i: (i, 0)),
                   pl.BlockSpec((SUBLANES, LANES), lambda i: (i, 0)),
                   pl.BlockSpec((KOUT, D_MODEL), lambda i: (i, 0))],
        out_shape=[jax.ShapeDtypeStruct((t, D_MODEL), F32),
                   jax.ShapeDtypeStruct((t, LANES), F32),
                   jax.ShapeDtypeStruct((nt * SUBLANES, LANES), F32),
                   jax.ShapeDtypeStruct((nt * KOUT, D_MODEL), BF16)],
        scratch_shapes=[pltpu.VMEM((HEAD_PAIRS, CHUNK, 2 * CHUNK), BF16),
                        pltpu.VMEM((TM, TM), BF16),
                        pltpu.VMEM((8, CONV_WIDTH), F32),
                        pltpu.VMEM((TM, D_MODEL), BF16)],
        compiler_params=pltpu.CompilerParams(dimension_semantics=("arbitrary",),
                                             vmem_limit_bytes=VMEM_LIMIT),
        name="mixer_router",
    )(x2d, *ins)


def _expert_kernel(te_ref, nu_ref, nxt_ref, src_ref, stage_ref, wg_hbm, wu_hbm, wd_hbm, os_ref,
                   xbuf, wg_f, wu_f, wd_f, wg_s, wu_s, wd_s, sem, wsem):
    i = pl.program_id(0)
    n_used = nu_ref[0]
    slot = i % 2

    def block_copy(tile, slot_, j):
        src = pl.multiple_of(src_ref[tile * BLOCKS_PER_ROW_TILE + j] * BLOCK, BLOCK)
        return pltpu.make_async_copy(stage_ref.at[pl.ds(src, BLOCK)],
                                     xbuf.at[slot_, pl.ds(j * BLOCK, BLOCK)], sem.at[slot_])

    def issue(tile, slot_):
        for j in range(BLOCKS_PER_ROW_TILE):
            block_copy(tile, slot_, j).start()

    def weight_copies(e):
        return (pltpu.make_async_copy(wg_hbm.at[e], wg_f, wsem.at[0]),
                pltpu.make_async_copy(wu_hbm.at[e], wu_f, wsem.at[1]),
                pltpu.make_async_copy(wd_hbm.at[e], wd_f, wsem.at[2]))

    @pl.when(i == 0)
    def _first():
        issue(0, 0)
        for cp in weight_copies(te_ref[0]):
            cp.start()

    @pl.when(i + 1 < n_used)
    def _prefetch():
        issue(i + 1, 1 - slot)

    @pl.when(i >= n_used)
    def _unused_tile():
        os_ref[...] = jnp.zeros_like(os_ref)

    @pl.when(i < n_used)
    def _tile():
        e = te_ref[i]
        prev = te_ref[jnp.maximum(i - 1, 0)]

        @pl.when((i == 0) | (e != prev))
        def _new_expert():
            for cp in weight_copies(e):
                cp.wait()
            wg_s[...] = wg_f[...].astype(BF16)
            wu_s[...] = wu_f[...].astype(BF16)
            wd_s[...] = wd_f[...].astype(BF16)
            nxt = nxt_ref[i]

            @pl.when(nxt >= 0)
            def _next_weights():
                for cp in weight_copies(nxt):
                    cp.start()

        for j in range(BLOCKS_PER_ROW_TILE):
            block_copy(i, slot, j).wait()

        hb = xbuf[slot]
        a = jnp.dot(hb, wg_s[...], preferred_element_type=F32)
        b = jnp.dot(hb, wu_s[...], preferred_element_type=F32)
        hid = (a * jax.nn.sigmoid(a) * b).astype(BF16)
        os_ref[...] = jnp.dot(hid, wd_s[...], preferred_element_type=F32).astype(BF16)


def _experts(tile_expert, n_used, next_expert, src_block, stage, w_gate, w_up, w_down, n_tiles):
    any_spec = pl.BlockSpec(memory_space=pl.ANY)
    return pl.pallas_call(
        _expert_kernel,
        grid_spec=pltpu.PrefetchScalarGridSpec(
            num_scalar_prefetch=4,
            grid=(n_tiles,),
            in_specs=[any_spec, any_spec, any_spec, any_spec],
            out_specs=pl.BlockSpec((TR, D_MODEL), lambda i, *_: (i, 0)),
            scratch_shapes=[pltpu.VMEM((2, TR, D_MODEL), BF16),
                            pltpu.VMEM((D_MODEL, D_EXPERT), F32),
                            pltpu.VMEM((D_MODEL, D_EXPERT), F32),
                            pltpu.VMEM((D_EXPERT, D_MODEL), F32),
                            pltpu.VMEM((D_MODEL, D_EXPERT), BF16),
                            pltpu.VMEM((D_MODEL, D_EXPERT), BF16),
                            pltpu.VMEM((D_EXPERT, D_MODEL), BF16),
                            pltpu.SemaphoreType.DMA((2,)),
                            pltpu.SemaphoreType.DMA((3,))]),
        out_shape=jax.ShapeDtypeStruct((n_tiles * TR, D_MODEL), BF16),
        compiler_params=pltpu.CompilerParams(dimension_semantics=("arbitrary",),
                                             vmem_limit_bytes=VMEM_LIMIT),
        name="experts",
    )(tile_expert, n_used, next_expert, src_block, stage, w_gate, w_up, w_down)


def _combine_kernel(dst0_ref, nblk_ref, loc_ref, used_ref, x1_ref, route_ref, p_ref, gple_ref,
                    wpg_ref, wpp_ref, gfin_ref, os_ref, out_ref, gbuf, y_s, sem):
    i = pl.program_id(0)
    n = pl.num_programs(0)
    slot = i % 2

    def for_each_block(tile, slot_, fn):
        def expert_body(e, carry):
            k = tile * N_EXPERTS + e
            d0 = dst0_ref[k]
            l0 = loc_ref[k]

            def block_body(j, carry2):
                src = pl.multiple_of((d0 + j) * BLOCK, BLOCK)
                dst = pl.multiple_of((l0 + j) * BLOCK, BLOCK)
                fn(pltpu.make_async_copy(os_ref.at[pl.ds(src, BLOCK)],
                                         gbuf.at[slot_, pl.ds(dst, BLOCK)], sem.at[slot_]))
                return carry2

            return lax.fori_loop(0, nblk_ref[k], block_body, carry)

        lax.fori_loop(0, N_EXPERTS, expert_body, 0)

    @pl.when(i == 0)
    def _first():
        gbuf[...] = jnp.zeros_like(gbuf)
        for_each_block(0, 0, lambda cp: cp.start())

    @pl.when(i + 1 < n)
    def _prefetch():
        for_each_block(i + 1, 1 - slot, lambda cp: cp.start())

    for_each_block(i, slot, lambda cp: cp.wait())

    route = route_ref[...]
    w1, w2, pos1, pos2 = (route[:, k:k + 1] for k in (2, 3, 4, 5))

    def unsort(lo, hi):
        col = (lax.broadcasted_iota(I32, (TM, hi - lo), 1) + lo).astype(F32)
        mat = (jnp.where(col == pos1, w1, 0.0) + jnp.where(col == pos2, w2, 0.0)).astype(BF16)
        return jnp.dot(mat, gbuf[slot, lo:hi, :], preferred_element_type=F32)

    y_s[...] = unsort(0, SORT_MAIN)
    for lo in range(SORT_MAIN, KOUT, TAIL):
        @pl.when(used_ref[i] > lo)
        def _tail(lo=lo):
            y_s[...] += unsort(lo, lo + TAIL)

    x2 = x1_ref[...] + y_s[...]
    hg = _rms(x2, gple_ref[...]).astype(BF16)
    gate = jax.nn.sigmoid(jnp.dot(hg, wpg_ref[...], preferred_element_type=F32))
    pe = jnp.dot(p_ref[...].astype(BF16), wpp_ref[...], preferred_element_type=F32)
    x3 = x2 + gate * pe
    out_ref[...] = _rms(x3, gfin_ref[...])


def _combine(dst0, nblk, loc, used, x1, route, p2d, g_ple, w_pg, w_pp, g_fin, os_):
    t = x1.shape[0]
    full = lambda a: pl.BlockSpec(a.shape, lambda i, *_: (0,) * a.ndim)
    return pl.pallas_call(
        _combine_kernel,
        grid_spec=pltpu.PrefetchScalarGridSpec(
            num_scalar_prefetch=4,
            grid=(t // TM,),
            in_specs=[pl.BlockSpec((TM, D_MODEL), lambda i, *_: (i, 0)),
                      pl.BlockSpec((TM, LANES), lambda i, *_: (i, 0)),
                      pl.BlockSpec((TM, PLE_DIM), lambda i, *_: (i, 0)),
                      full(g_ple), full(w_pg), full(w_pp), full(g_fin),
                      pl.BlockSpec(memory_space=pl.ANY)],
            out_specs=pl.BlockSpec((TM, D_MODEL), lambda i, *_: (i, 0)),
            scratch_shapes=[pltpu.VMEM((2, KOUT, D_MODEL), BF16),
                            pltpu.VMEM((TM, D_MODEL), F32),
                            pltpu.SemaphoreType.DMA((2,))]),
        out_shape=jax.ShapeDtypeStruct((t, D_MODEL), F32),
        compiler_params=pltpu.CompilerParams(dimension_semantics=("arbitrary",),
                                             vmem_limit_bytes=VMEM_LIMIT),
        name="combine_ple",
    )(dst0, nblk, loc, used, x1, route, p2d, g_ple, w_pg, w_pp, g_fin, os_)


def _routing_tables(cnt, n_tiles):
    nt = cnt.shape[0] // SUBLANES
    n = cnt.reshape(nt, SUBLANES, LANES)[:, 0, ROUTE_LANE0:ROUTE_LANE0 + N_EXPERTS].astype(I32)
    nblk = (n + BLOCK - 1) // BLOCK
    loc = jnp.cumsum(nblk, axis=1) - nblk
    carry = jnp.cumsum(nblk, axis=0) - nblk
    total = jnp.sum(nblk, axis=0)
    tiles_e = (total + BLOCKS_PER_ROW_TILE - 1) // BLOCKS_PER_ROW_TILE
    tile_end = jnp.cumsum(tiles_e)
    seg0 = (tile_end - tiles_e) * BLOCKS_PER_ROW_TILE
    n_used = tile_end[-1:]
    dst0 = seg0[None, :] + carry

    tile_ids = jnp.minimum(jnp.arange(n_tiles, dtype=I32), n_used[0] - 1)
    tile_expert = jnp.sum(tile_ids[:, None] >= tile_end[None, :], axis=1).astype(I32)

    nb = n_tiles * BLOCKS_PER_ROW_TILE
    b = jnp.arange(nb, dtype=I32)
    oh_e = (tile_expert[:, None] == jnp.arange(N_EXPERTS, dtype=I32)[None, :]).astype(I32)
    per_block = lambda a: jnp.repeat(a, BLOCKS_PER_ROW_TILE, axis=0)
    pick_e = lambda tab: per_block(jnp.sum(oh_e[:, None, :] * tab[None, :, :], axis=2))
    off = b - per_block(jnp.sum(oh_e * seg0[None, :], axis=1))
    run_end = pick_e(carry + nblk)
    run_beg = pick_e(carry)
    run_loc = pick_e(loc)
    tile_of_b = jnp.sum((run_end <= off[:, None]).astype(I32), axis=1)
    valid = (tile_of_b < nt) & (b < n_used[0] * BLOCKS_PER_ROW_TILE)
    oh_t = (jnp.minimum(tile_of_b, nt - 1)[:, None] == jnp.arange(nt, dtype=I32)[None, :]).astype(I32)
    src = (jnp.sum(oh_t * (run_loc - run_beg), axis=1) + off
           + jnp.minimum(tile_of_b, nt - 1) * (KOUT // BLOCK))
    zero_block = KOUT // BLOCK - 1
    src = jnp.where(valid, src, zero_block).astype(I32)

    ids = jnp.arange(N_EXPERTS, dtype=I32)
    later = (tiles_e > 0)[None, :] & (ids[None, :] > ids[:, None])
    next_used = jnp.min(jnp.where(later, ids[None, :], N_EXPERTS), axis=1)
    next_used = jnp.where(next_used == N_EXPERTS, -1, next_used)
    next_expert = jnp.sum((tile_expert[:, None] == ids[None, :]) * next_used[None, :], axis=1)
    used_rows = jnp.sum(nblk, axis=1) * BLOCK
    return (tile_expert, n_used.astype(I32), next_expert.astype(I32), src,
            dst0.reshape(-1).astype(I32), nblk.reshape(-1).astype(I32), loc.reshape(-1).astype(I32),
            used_rows.astype(I32))


def kernel(x, p, g_mix, w_in, w_conv, g_sgu, w_spatial, b_spatial, w_out, g_ffn, w_group, b_group,
           w_router, b_router, w_gate, w_up, w_down, g_ple, w_ple_gate, w_ple_proj, g_final):
    bsz, seq, d = x.shape
    t = bsz * seq
    assert w_in.shape[0] == 1, "single-layer block"
    assert d == D_MODEL and seq % TM == 0 and TM % CHUNK == 0
    nt = t // TM
    max_rows = TOP_K * t + nt * N_EXPERTS * (BLOCK - 1)
    n_tiles = (max_rows + N_EXPERTS * (TR - 1)) // TR + 1
    l = 0

    w_rt = jnp.concatenate(
        [w_group[l], jnp.transpose(w_router[l], (1, 0, 2)).reshape(d, N_EXPERTS)], axis=1)
    w_rt = jnp.pad(w_rt, ((0, 0), (0, LANES - w_rt.shape[1]))).astype(BF16)
    b_rt = jnp.pad(jnp.concatenate([b_group[l], b_router[l].reshape(-1)]),
                   (0, LANES - N_GROUPS - N_EXPERTS)).reshape(1, LANES)
    b_sp = jnp.repeat(b_spatial[l].T, SGU_WIDTH // SGU_HEADS, axis=1)

    x1, route, cnt, stage = _mixer_router(
        x.reshape(t, d), g_mix[l].reshape(1, d), w_in[l].astype(BF16), w_conv[l],
        g_sgu[l].reshape(1, -1), w_spatial[l], b_sp, w_out[l].astype(BF16),
        g_ffn[l].reshape(1, d), w_rt, b_rt, seq)

    tile_expert, n_used, next_expert, src, dst0, nblk, loc, used = _routing_tables(cnt, n_tiles)

    os_ = _experts(tile_expert, n_used, next_expert, src, stage,
                   w_gate[l].reshape(N_EXPERTS, d, D_EXPERT),
                   w_up[l].reshape(N_EXPERTS, d, D_EXPERT),
                   w_down[l].reshape(N_EXPERTS, D_EXPERT, d), n_tiles)
    out = _combine(dst0, nblk, loc, used, x1, route, p[l].reshape(t, PLE_DIM), g_ple[l].reshape(1, d),
                   w_ple_gate[l].astype(BF16), w_ple_proj[l].astype(BF16),
                   g_final.reshape(1, d), os_)
    return out.reshape(bsz, seq, d)
```

```python
import functools

import jax
import jax.numpy as jnp
from jax import lax
from jax.experimental import pallas as pl
from jax.experimental.pallas import tpu as pltpu

F32 = jnp.float32
BF16 = jnp.bfloat16
I32 = jnp.int32

EPS = 1e-6
D_MODEL = 1024
CONV_WIDTH = 512
SGU_WIDTH = 512
SGU_HEADS = 8
HEAD_PAIRS = SGU_HEADS // 2
CHUNK = 128
N_GROUPS = 4
EXPERTS_PER_GROUP = 8
N_EXPERTS = N_GROUPS * EXPERTS_PER_GROUP
TOP_K = 2
D_EXPERT = 512
PLE_DIM = 256
LANES = 128
SUBLANES = 8
BLOCK = 2 * SUBLANES
ROUTE_LANE0 = N_GROUPS

TM = 512
TR = 512
BLOCKS_PER_ROW_TILE = TR // BLOCK
KOUT = TOP_K * TM + N_EXPERTS * BLOCK
TAIL = 128
SORT_MAIN = KOUT - 2 * TAIL
DUMP_BLOCKS = 2 * BLOCKS_PER_ROW_TILE
POS_SPLIT = 32
VMEM_LIMIT = 56 * 1024 * 1024


def _rms(x, g):
    return x * lax.rsqrt(jnp.mean(x * x, axis=-1, keepdims=True) + EPS) * g


def _mixer_router_kernel(tiles_per_seq,
                         x_ref, gmix_ref, win_f32, wconv_ref, gsgu_ref, wsp_ref, bsp_ref,
                         wout_f32, gffn_ref, wrt_f32, brt_ref,
                         x1_ref, route_ref, cnt_ref, stage_ref,
                         wcat_s, tri_s, halo_s, mix_s, h2_s, win_ref, wout_ref, wrt_ref):
    i = pl.program_id(0)
    nch = TM // CHUNK

    @pl.when(i == 0)
    def _init():
        r = lax.broadcasted_iota(I32, (CHUNK, CHUNK), 0)
        c = lax.broadcasted_iota(I32, (CHUNK, CHUNK), 1)
        causal = c <= r
        for j in range(HEAD_PAIRS):
            wa = jnp.where(causal, wsp_ref[2 * j], 0.0)
            wb = jnp.where(causal, wsp_ref[2 * j + 1], 0.0)
            wcat_s[j] = jnp.concatenate([wa, wb], axis=1).astype(BF16)
        rr = lax.broadcasted_iota(I32, (TM, TM), 0)
        cc = lax.broadcasted_iota(I32, (TM, TM), 1)
        tri_s[...] = (cc < rr).astype(BF16)
        h2_s[...] = jnp.zeros_like(h2_s)
        win_ref[...] = win_f32[...].astype(BF16)
        wout_ref[...] = wout_f32[...].astype(BF16)
        wrt_ref[...] = wrt_f32[...].astype(BF16)

    @pl.when(i % tiles_per_seq == 0)
    def _seq_start():
        halo_s[...] = jnp.zeros_like(halo_s)

    x = x_ref[...]
    hb = _rms(x, gmix_ref[...]).astype(BF16)
    h2b = h2_s[(i + 1) % 2]

    def proj(k):
        return jnp.dot(hb, win_ref[:, k * 512:(k + 1) * 512], preferred_element_type=F32)

    logits = jnp.dot(h2b, wrt_ref[...], preferred_element_type=F32) + brt_ref[...]
    zc = proj(1) * proj(2)

    lane = lax.broadcasted_iota(I32, (TM, LANES), 1).astype(F32)
    neg = jnp.float32(-jnp.inf)
    big = jnp.float32(1e9)
    is_g = lane < N_GROUPS
    gl = jnp.where(is_g, logits, neg)
    gmax = jnp.max(gl, axis=1, keepdims=True)
    gsum = jnp.sum(jnp.where(is_g, jnp.exp(gl - gmax), 0.0), axis=1, keepdims=True)
    g_w = 1.0 / gsum
    g_idx = jnp.min(jnp.where(gl == gmax, lane, big), axis=1, keepdims=True)
    lo_lane = ROUTE_LANE0 + EXPERTS_PER_GROUP * g_idx
    in_grp = (lane >= lo_lane) & (lane < lo_lane + EXPERTS_PER_GROUP)
    el = jnp.where(in_grp, logits, neg)
    v1 = jnp.max(el, axis=1, keepdims=True)
    i1 = jnp.min(jnp.where(el == v1, lane, big), axis=1, keepdims=True)
    el2 = jnp.where(lane == i1, neg, el)
    v2 = jnp.max(el2, axis=1, keepdims=True)
    i2 = jnp.min(jnp.where(el2 == v2, lane, big), axis=1, keepdims=True)
    e21 = jnp.exp(v2 - v1)
    w1 = g_w / (1.0 + e21)
    w2 = g_w * e21 / (1.0 + e21)

    row = lax.broadcasted_iota(I32, (TM, CONV_WIDTH), 0)
    h6 = halo_s[6:7, :]
    h7 = halo_s[7:8, :]
    z1 = jnp.where(row == 0, h7, pltpu.roll(zc, 1, 0))
    z2 = jnp.where(row == 0, h6, jnp.where(row == 1, h7, pltpu.roll(zc, 2, 0)))
    conv = z2 * wconv_ref[0:1, :] + z1 * wconv_ref[1:2, :] + zc * wconv_ref[2:3, :]
    halo_s[...] = zc[TM - 8:TM, :]
    mix_s[:, 0:CONV_WIDTH] = (proj(0) * conv).astype(BF16)

    sel1 = lane == i1
    sel2 = lane == i2
    onehot = jnp.where(sel1 | sel2, 1.0, 0.0)
    counts = jnp.sum(onehot, axis=0, keepdims=True)
    blocks = jnp.ceil(counts * (1.0 / BLOCK))
    er = lax.broadcasted_iota(I32, (LANES, LANES), 0)
    ec = lax.broadcasted_iota(I32, (LANES, LANES), 1)
    before = (er < ec).astype(BF16)
    run_start = BLOCK * jnp.dot(jnp.broadcast_to(blocks, (SUBLANES, LANES)).astype(BF16), before,
                                   preferred_element_type=F32)[0:1, :]
    rank = jnp.dot(tri_s[...], onehot.astype(BF16), preferred_element_type=F32) + run_start
    pos1 = jnp.sum(jnp.where(sel1, rank, 0.0), axis=1, keepdims=True)
    pos2 = jnp.sum(jnp.where(sel2, rank, 0.0), axis=1, keepdims=True)

    route = jnp.where(lane == 0, i1 - ROUTE_LANE0, 0.0)
    route = jnp.where(lane == 1, i2 - ROUTE_LANE0, route)
    route = jnp.where(lane == 2, w1, route)
    route = jnp.where(lane == 3, w2, route)
    route = jnp.where(lane == 4, pos1, route)
    route = jnp.where(lane == 5, pos2, route)
    route_ref[...] = route
    cnt_ref[...] = jnp.broadcast_to(counts, cnt_ref.shape)

    p1h = jnp.floor(pos1 * (1.0 / POS_SPLIT))
    p2h = jnp.floor(pos2 * (1.0 / POS_SPLIT))
    pv = jnp.where(lane == 0, p1h, 0.0)
    pv = jnp.where(lane == 1, pos1 - POS_SPLIT * p1h, pv)
    pv = jnp.where(lane == 2, p2h, pv)
    pv = jnp.where(lane == 3, pos2 - POS_SPLIT * p2h, pv)
    pick = (lax.broadcasted_iota(I32, (SUBLANES, LANES), 0)
            == lax.broadcasted_iota(I32, (SUBLANES, LANES), 1)).astype(BF16)
    prow = lax.dot_general(pick, pv.astype(BF16), (((1,), (1,)), ((), ())),
                           preferred_element_type=F32)
    pos1_row = POS_SPLIT * prow[0:1, :] + prow[1:2, :]
    pos2_row = POS_SPLIT * prow[2:3, :] + prow[3:4, :]

    def sort_rows(lo, hi):
        out_row = (lax.broadcasted_iota(I32, (hi - lo, TM), 0) + lo).astype(F32)
        sort_mat = jnp.where((out_row == pos1_row) | (out_row == pos2_row), 1.0, 0.0).astype(BF16)
        stage_ref[lo:hi, :] = jnp.dot(sort_mat, h2b, preferred_element_type=F32).astype(BF16)

    u = jax.nn.gelu(proj(3))
    sort_rows(0, SORT_MAIN // 2)
    v = jax.nn.gelu(proj(4))
    vc = v - jnp.mean(v, axis=-1, keepdims=True)
    vn = vc * lax.rsqrt(jnp.mean(vc * vc, axis=-1, keepdims=True) + EPS) * gsgu_ref[...]
    vnb = vn.astype(BF16)
    left = lax.broadcasted_iota(I32, (CHUNK, LANES), 1) < (LANES // 2)
    zero = jnp.zeros((CHUNK, LANES), BF16)
    for j in range(HEAD_PAIRS):
        cols = []
        for c in range(nch):
            s = vnb[c * CHUNK:(c + 1) * CHUNK, j * LANES:(j + 1) * LANES]
            cols.append(jnp.concatenate([jnp.where(left, s, zero), jnp.where(left, zero, s)], axis=0))
        rhs = jnp.concatenate(cols, axis=1)
        res = jnp.dot(wcat_s[j], rhs, preferred_element_type=F32)
        bias = bsp_ref[:, j * LANES:(j + 1) * LANES]
        for c in range(nch):
            mixed = res[:, c * LANES:(c + 1) * LANES] + bias
            uu = u[c * CHUNK:(c + 1) * CHUNK, j * LANES:(j + 1) * LANES]
            mix_s[c * CHUNK:(c + 1) * CHUNK,
                  CONV_WIDTH + j * LANES:CONV_WIDTH + (j + 1) * LANES] = (uu * mixed).astype(BF16)

    sort_rows(SORT_MAIN // 2, SORT_MAIN)
    x1 = x + jnp.dot(mix_s[...], wout_ref[...], preferred_element_type=F32)
    x1_ref[...] = x1
    h2_s[i % 2] = _rms(x1, gffn_ref[...]).astype(BF16)

    used_rows = BLOCK * jnp.sum(blocks)
    for lo in range(SORT_MAIN, KOUT, TAIL):
        @pl.when(used_rows > lo)
        def _tail(lo=lo):
            sort_rows(lo, lo + TAIL)

        @pl.when(used_rows <= lo)
        def _empty_tail(lo=lo):
            stage_ref[lo:lo + TAIL, :] = jnp.zeros((TAIL, D_MODEL), BF16)


def _mixer_router(x2d, g_mix, w_in, w_conv, g_sgu, w_sp, b_sp, w_out, g_ffn, w_rt, b_rt, seq):
    t = x2d.shape[0]
    nt = t // TM
    full = lambda a: pl.BlockSpec(a.shape, lambda i: (0,) * a.ndim)
    ins = (g_mix, w_in, w_conv, g_sgu, w_sp, b_sp, w_out, g_ffn, w_rt, b_rt)
    prev = lambda i: (jnp.maximum(i - 1, 0), 0)
    return pl.pallas_call(
        functools.partial(_mixer_router_kernel, seq // TM),
        grid=(nt + 1,),
        in_specs=[pl.BlockSpec((TM, D_MODEL), lambda i: (jnp.minimum(i, nt - 1), 0))]
        + [full(a) for a in ins],
        out_specs=[pl.BlockSpec((TM, D_MODEL), lambda i: (i, 0)),
                   pl.BlockSpec((TM, LANES), prev),
                   pl.BlockSpec((SUBLANES, LANES), prev),
                   pl.BlockSpec((KOUT, D_MODEL), prev)],
        out_shape=[jax.ShapeDtypeStruct((t + TM, D_MODEL), F32),
                   jax.ShapeDtypeStruct((t, LANES), F32),
                   jax.ShapeDtypeStruct((nt * SUBLANES, LANES), F32),
                   jax.ShapeDtypeStruct((nt * KOUT + DUMP_BLOCKS * BLOCK, D_MODEL), BF16)],
        scratch_shapes=[pltpu.VMEM((HEAD_PAIRS, CHUNK, 2 * CHUNK), BF16),
                        pltpu.VMEM((TM, TM), BF16),
                        pltpu.VMEM((8, CONV_WIDTH), F32),
                        pltpu.VMEM((TM, D_MODEL), BF16),
                        pltpu.VMEM((2, TM, D_MODEL), BF16),
                        pltpu.VMEM(w_in.shape, BF16),
                        pltpu.VMEM(w_out.shape, BF16),
                        pltpu.VMEM(w_rt.shape, BF16)],
        compiler_params=pltpu.CompilerParams(dimension_semantics=("arbitrary",),
                                             vmem_limit_bytes=VMEM_LIMIT),
        name="mixer_router",
    )(x2d, *ins)


def _expert_kernel(te_ref, nu_ref, nxt_ref, src_ref, dst_ref, stage_ref, wg_hbm, wu_hbm, wd_hbm,
                   out_ref, xbuf, obuf, wg_f, wu_f, wd_f, wg_s, wu_s, wd_s, sem, osem, wsem):
    i = pl.program_id(0)
    n_used = nu_ref[0]
    slot = i % 2

    def gather_copy(tile, slot_, j):
        src = pl.multiple_of(src_ref[tile * BLOCKS_PER_ROW_TILE + j] * BLOCK, BLOCK)
        return pltpu.make_async_copy(stage_ref.at[pl.ds(src, BLOCK)],
                                     xbuf.at[slot_, pl.ds(j * BLOCK, BLOCK)], sem.at[slot_])

    def scatter_copy(tile, slot_, j):
        dst = pl.multiple_of(dst_ref[tile * BLOCKS_PER_ROW_TILE + j] * BLOCK, BLOCK)
        return pltpu.make_async_copy(obuf.at[slot_, pl.ds(j * BLOCK, BLOCK)],
                                     out_ref.at[pl.ds(dst, BLOCK)], osem.at[slot_])

    def dump_fill(half):
        row0 = out_ref.shape[0] - DUMP_BLOCKS * BLOCK + half * TR
        return pltpu.make_async_copy(obuf.at[0], out_ref.at[pl.ds(row0, TR)], osem.at[0])

    def weight_copies(e):
        return (pltpu.make_async_copy(wg_hbm.at[e], wg_f, wsem.at[0]),
                pltpu.make_async_copy(wu_hbm.at[e], wu_f, wsem.at[1]),
                pltpu.make_async_copy(wd_hbm.at[e], wd_f, wsem.at[2]))

    @pl.when(i == 0)
    def _first():
        for j in range(BLOCKS_PER_ROW_TILE):
            gather_copy(0, 0, j).start()
        for cp in weight_copies(te_ref[0]):
            cp.start()
        obuf[0] = jnp.zeros((TR, D_MODEL), BF16)
        for half in range(2):
            dump_fill(half).start()
        for half in range(2):
            dump_fill(half).wait()

    @pl.when((i >= 2) & (i - 2 < n_used))
    def _free_out_slot():
        for j in range(BLOCKS_PER_ROW_TILE):
            scatter_copy(i - 2, slot, j).wait()

    @pl.when(i + 1 < n_used)
    def _prefetch():
        for j in range(BLOCKS_PER_ROW_TILE):
            gather_copy(i + 1, 1 - slot, j).start()

    @pl.when(i < n_used)
    def _tile():
        e = te_ref[i]
        prev = te_ref[jnp.maximum(i - 1, 0)]

        @pl.when((i == 0) | (e != prev))
        def _new_expert():
            for cp in weight_copies(e):
                cp.wait()
            wg_s[...] = wg_f[...].astype(BF16)
            wu_s[...] = wu_f[...].astype(BF16)
            wd_s[...] = wd_f[...].astype(BF16)
            nxt = nxt_ref[i]

            @pl.when(nxt >= 0)
            def _next_weights():
                for cp in weight_copies(nxt):
                    cp.start()

        for j in range(BLOCKS_PER_ROW_TILE):
            gather_copy(i, slot, j).wait()

        hb = xbuf[slot]
        a = jnp.dot(hb, wg_s[...], preferred_element_type=F32)
        b = jnp.dot(hb, wu_s[...], preferred_element_type=F32)
        hid = (a * jax.nn.sigmoid(a) * b).astype(BF16)
        obuf[slot] = jnp.dot(hid, wd_s[...], preferred_element_type=F32).astype(BF16)
        for j in range(BLOCKS_PER_ROW_TILE):
            scatter_copy(i, slot, j).start()


def _experts(tile_expert, n_used, next_expert, src_block, dst_block, stage, w_gate, w_up, w_down,
             n_tiles):
    any_spec = pl.BlockSpec(memory_space=pl.ANY)
    return pl.pallas_call(
        _expert_kernel,
        grid_spec=pltpu.PrefetchScalarGridSpec(
            num_scalar_prefetch=5,
            grid=(n_tiles,),
            in_specs=[any_spec, any_spec, any_spec, any_spec],
            out_specs=any_spec,
            scratch_shapes=[pltpu.VMEM((2, TR, D_MODEL), BF16),
                            pltpu.VMEM((2, TR, D_MODEL), BF16),
                            pltpu.VMEM((D_MODEL, D_EXPERT), F32),
                            pltpu.VMEM((D_MODEL, D_EXPERT), F32),
                            pltpu.VMEM((D_EXPERT, D_MODEL), F32),
                            pltpu.VMEM((D_MODEL, D_EXPERT), BF16),
                            pltpu.VMEM((D_MODEL, D_EXPERT), BF16),
                            pltpu.VMEM((D_EXPERT, D_MODEL), BF16),
                            pltpu.SemaphoreType.DMA((2,)),
                            pltpu.SemaphoreType.DMA((2,)),
                            pltpu.SemaphoreType.DMA((3,))]),
        out_shape=jax.ShapeDtypeStruct(stage.shape, stage.dtype),
        input_output_aliases={5: 0},
        compiler_params=pltpu.CompilerParams(dimension_semantics=("arbitrary",),
                                             vmem_limit_bytes=VMEM_LIMIT),
        name="experts",
    )(tile_expert, n_used, next_expert, src_block, dst_block, stage, w_gate, w_up, w_down)


def _combine_kernel(used_ref, x1_ref, route_ref, p_ref, gple_ref, wpg_f32, wpp_f32, gfin_ref,
                    srt_ref, out_ref, y_s, wpg_ref, wpp_ref):
    i = pl.program_id(0)

    @pl.when(i == 0)
    def _init():
        wpg_ref[...] = wpg_f32[...].astype(BF16)
        wpp_ref[...] = wpp_f32[...].astype(BF16)

    route = route_ref[...]
    w1, w2, pos1, pos2 = (route[:, k:k + 1] for k in (2, 3, 4, 5))

    def unsort(lo, hi):
        col = (lax.broadcasted_iota(I32, (TM, hi - lo), 1) + lo).astype(F32)
        mat = (jnp.where(col == pos1, w1, 0.0) + jnp.where(col == pos2, w2, 0.0)).astype(BF16)
        return jnp.dot(mat, srt_ref[lo:hi, :], preferred_element_type=F32)

    y_s[...] = unsort(0, SORT_MAIN)
    for lo in range(SORT_MAIN, KOUT, TAIL):
        @pl.when(used_ref[i] > lo)
        def _tail(lo=lo):
            y_s[...] += unsort(lo, lo + TAIL)

    x2 = x1_ref[...] + y_s[...]
    hg = _rms(x2, gple_ref[...]).astype(BF16)
    gate = jax.nn.sigmoid(jnp.dot(hg, wpg_ref[...], preferred_element_type=F32))
    pe = jnp.dot(p_ref[...].astype(BF16), wpp_ref[...], preferred_element_type=F32)
    x3 = x2 + gate * pe
    out_ref[...] = _rms(x3, gfin_ref[...])


def _combine(used, x1, route, p2d, g_ple, w_pg, w_pp, g_fin, sorted_out):
    t = route.shape[0]
    full = lambda a: pl.BlockSpec(a.shape, lambda i, *_: (0,) * a.ndim)
    return pl.pallas_call(
        _combine_kernel,
        grid_spec=pltpu.PrefetchScalarGridSpec(
            num_scalar_prefetch=1,
            grid=(t // TM,),
            in_specs=[pl.BlockSpec((TM, D_MODEL), lambda i, *_: (i, 0)),
                      pl.BlockSpec((TM, LANES), lambda i, *_: (i, 0)),
                      pl.BlockSpec((TM, PLE_DIM), lambda i, *_: (i, 0)),
                      full(g_ple), full(w_pg), full(w_pp), full(g_fin),
                      pl.BlockSpec((KOUT, D_MODEL), lambda i, *_: (i, 0))],
            out_specs=pl.BlockSpec((TM, D_MODEL), lambda i, *_: (i, 0)),
            scratch_shapes=[pltpu.VMEM((TM, D_MODEL), F32),
                            pltpu.VMEM(w_pg.shape, BF16),
                            pltpu.VMEM(w_pp.shape, BF16)]),
        out_shape=jax.ShapeDtypeStruct((t, D_MODEL), F32),
        compiler_params=pltpu.CompilerParams(dimension_semantics=("arbitrary",),
                                             vmem_limit_bytes=VMEM_LIMIT),
        name="combine_ple",
    )(used, x1, route, p2d, g_ple, w_pg, w_pp, g_fin, sorted_out)


def _routing_tables(cnt, n_tiles):
    nt = cnt.shape[0] // SUBLANES
    n = cnt.reshape(nt, SUBLANES, LANES)[:, 0, ROUTE_LANE0:ROUTE_LANE0 + N_EXPERTS].astype(I32)
    nblk = (n + BLOCK - 1) // BLOCK
    loc = jnp.cumsum(nblk, axis=1) - nblk
    carry = jnp.cumsum(nblk, axis=0) - nblk
    total = jnp.sum(nblk, axis=0)
    tiles_e = (total + BLOCKS_PER_ROW_TILE - 1) // BLOCKS_PER_ROW_TILE
    tile_end = jnp.cumsum(tiles_e)
    seg0 = (tile_end - tiles_e) * BLOCKS_PER_ROW_TILE
    n_used = tile_end[-1:]

    tile_ids = jnp.minimum(jnp.arange(n_tiles, dtype=I32), n_used[0] - 1)
    tile_expert = jnp.sum(tile_ids[:, None] >= tile_end[None, :], axis=1).astype(I32)

    nb = n_tiles * BLOCKS_PER_ROW_TILE
    b = jnp.arange(nb, dtype=I32)
    oh_e = (tile_expert[:, None] == jnp.arange(N_EXPERTS, dtype=I32)[None, :]).astype(I32)
    per_block = lambda a: jnp.repeat(a, BLOCKS_PER_ROW_TILE, axis=0)
    pick_e = lambda tab: per_block(jnp.sum(oh_e[:, None, :] * tab[None, :, :], axis=2))
    off = b - per_block(jnp.sum(oh_e * seg0[None, :], axis=1))
    run_end = pick_e(carry + nblk)
    run_beg = pick_e(carry)
    run_loc = pick_e(loc)
    tile_of_b = jnp.sum((run_end <= off[:, None]).astype(I32), axis=1)
    valid = (tile_of_b < nt) & (b < n_used[0] * BLOCKS_PER_ROW_TILE)
    oh_t = (jnp.minimum(tile_of_b, nt - 1)[:, None] == jnp.arange(nt, dtype=I32)[None, :]).astype(I32)
    src = (jnp.sum(oh_t * (run_loc - run_beg), axis=1) + off
           + jnp.minimum(tile_of_b, nt - 1) * (KOUT // BLOCK))
    zero_block = KOUT // BLOCK - 1
    dump = nt * (KOUT // BLOCK) + b % DUMP_BLOCKS
    dst = jnp.where(valid, src, dump).astype(I32)
    src = jnp.where(valid, src, zero_block).astype(I32)

    ids = jnp.arange(N_EXPERTS, dtype=I32)
    later = (tiles_e > 0)[None, :] & (ids[None, :] > ids[:, None])
    next_used = jnp.min(jnp.where(later, ids[None, :], N_EXPERTS), axis=1)
    next_used = jnp.where(next_used == N_EXPERTS, -1, next_used)
    next_expert = jnp.sum((tile_expert[:, None] == ids[None, :]) * next_used[None, :], axis=1)
    used_rows = jnp.sum(nblk, axis=1) * BLOCK
    return tile_expert, n_used.astype(I32), next_expert.astype(I32), src, dst, used_rows.astype(I32)


def kernel(x, p, g_mix, w_in, w_conv, g_sgu, w_spatial, b_spatial, w_out, g_ffn, w_group, b_group,
           w_router, b_router, w_gate, w_up, w_down, g_ple, w_ple_gate, w_ple_proj, g_final):
    bsz, seq, d = x.shape
    t = bsz * seq
    assert w_in.shape[0] == 1, "single-layer block"
    assert d == D_MODEL and seq % TM == 0 and TM % CHUNK == 0
    nt = t // TM
    max_rows = TOP_K * t + nt * N_EXPERTS * (BLOCK - 1)
    n_tiles = (max_rows + N_EXPERTS * (TR - 1)) // TR + 2
    l = 0

    w_rt = jnp.concatenate(
        [w_group[l], jnp.transpose(w_router[l], (1, 0, 2)).reshape(d, N_EXPERTS)], axis=1)
    w_rt = jnp.pad(w_rt, ((0, 0), (0, LANES - w_rt.shape[1])))
    b_rt = jnp.pad(jnp.concatenate([b_group[l], b_router[l].reshape(-1)]),
                   (0, LANES - N_GROUPS - N_EXPERTS)).reshape(1, LANES)
    b_sp = jnp.repeat(b_spatial[l].T, SGU_WIDTH // SGU_HEADS, axis=1)

    x1, route, cnt, stage = _mixer_router(
        x.reshape(t, d), g_mix[l].reshape(1, d), w_in[l], w_conv[l],
        g_sgu[l].reshape(1, -1), w_spatial[l], b_sp, w_out[l],
        g_ffn[l].reshape(1, d), w_rt, b_rt, seq)

    tile_expert, n_used, next_expert, src, dst, used = _routing_tables(cnt, n_tiles)

    sorted_out = _experts(tile_expert, n_used, next_expert, src, dst, stage,
                          w_gate[l].reshape(N_EXPERTS, d, D_EXPERT),
                          w_up[l].reshape(N_EXPERTS, d, D_EXPERT),
                          w_down[l].reshape(N_EXPERTS, D_EXPERT, d), n_tiles)
    out = _combine(used, x1, route, p[l].reshape(t, PLE_DIM), g_ple[l].reshape(1, d),
                   w_ple_gate[l], w_ple_proj[l], g_final.reshape(1, d), sorted_out)
    return out.reshape(bsz, seq, d)
```

```python
import functools

import jax
import jax.numpy as jnp
from jax import lax
from jax.experimental import pallas as pl
from jax.experimental.pallas import tpu as pltpu

F32 = jnp.float32
BF16 = jnp.bfloat16
I32 = jnp.int32

EPS = 1e-6
D_MODEL = 1024
CONV_WIDTH = 512
SGU_WIDTH = 512
SGU_HEADS = 8
HEAD_PAIRS = SGU_HEADS // 2
CHUNK = 128
N_GROUPS = 4
EXPERTS_PER_GROUP = 8
N_EXPERTS = N_GROUPS * EXPERTS_PER_GROUP
TOP_K = 2
D_EXPERT = 512
PLE_DIM = 256
LANES = 128
SUBLANES = 8
BLOCK = 2 * SUBLANES
ROUTE_LANE0 = N_GROUPS

TM = 512
TR = 512
BLOCKS_PER_ROW_TILE = TR // BLOCK
KOUT = TOP_K * TM + N_EXPERTS * BLOCK
TAIL = 128
SORT_MAIN = KOUT - 2 * TAIL
DUMP_BLOCKS = 2 * BLOCKS_PER_ROW_TILE
assert DUMP_BLOCKS * BLOCK <= KOUT
POS_SPLIT = 32
VMEM_LIMIT = 56 * 1024 * 1024


def _rms(x, g):
    return x * lax.rsqrt(jnp.mean(x * x, axis=-1, keepdims=True) + EPS) * g


def _mixer_router_kernel(tiles_per_seq,
                         x_ref, gmix_ref, win_f32, wconv_ref, gsgu_ref, wsp_ref, bsp_ref,
                         wout_f32, gffn_ref, wrt_f32, brt_ref,
                         x1_ref, route_ref, cnt_ref, stage_ref,
                         wcat_s, tri_s, halo_s, mix_s, h2_s, win_ref, wout_ref, wrt_ref):
    i = pl.program_id(0)
    nch = TM // CHUNK

    @pl.when(i == 0)
    def _init():
        r = lax.broadcasted_iota(I32, (CHUNK, CHUNK), 0)
        c = lax.broadcasted_iota(I32, (CHUNK, CHUNK), 1)
        causal = c <= r
        for j in range(HEAD_PAIRS):
            wa = jnp.where(causal, wsp_ref[2 * j], 0.0)
            wb = jnp.where(causal, wsp_ref[2 * j + 1], 0.0)
            wcat_s[j] = jnp.concatenate([wa, wb], axis=1).astype(BF16)
        rr = lax.broadcasted_iota(I32, (TM, TM), 0)
        cc = lax.broadcasted_iota(I32, (TM, TM), 1)
        tri_s[...] = (cc < rr).astype(BF16)
        h2_s[...] = jnp.zeros_like(h2_s)
        win_ref[...] = win_f32[...].astype(BF16)
        wout_ref[...] = wout_f32[...].astype(BF16)
        wrt_ref[...] = wrt_f32[...].astype(BF16)

    @pl.when(i % tiles_per_seq == 0)
    def _seq_start():
        halo_s[...] = jnp.zeros_like(halo_s)

    x = x_ref[...]
    hb = _rms(x, gmix_ref[...]).astype(BF16)
    h2b = h2_s[(i + 1) % 2]

    def proj(k):
        return jnp.dot(hb, win_ref[:, k * 512:(k + 1) * 512], preferred_element_type=F32)

    logits = jnp.dot(h2b, wrt_ref[...], preferred_element_type=F32) + brt_ref[...]
    zc = proj(1) * proj(2)

    lane = lax.broadcasted_iota(I32, (TM, LANES), 1).astype(F32)
    neg = jnp.float32(-jnp.inf)
    big = jnp.float32(1e9)
    is_g = lane < N_GROUPS
    gl = jnp.where(is_g, logits, neg)
    gmax = jnp.max(gl, axis=1, keepdims=True)
    gsum = jnp.sum(jnp.where(is_g, jnp.exp(gl - gmax), 0.0), axis=1, keepdims=True)
    g_w = 1.0 / gsum
    g_idx = jnp.min(jnp.where(gl == gmax, lane, big), axis=1, keepdims=True)
    lo_lane = ROUTE_LANE0 + EXPERTS_PER_GROUP * g_idx
    in_grp = (lane >= lo_lane) & (lane < lo_lane + EXPERTS_PER_GROUP)
    el = jnp.where(in_grp, logits, neg)
    v1 = jnp.max(el, axis=1, keepdims=True)
    i1 = jnp.min(jnp.where(el == v1, lane, big), axis=1, keepdims=True)
    el2 = jnp.where(lane == i1, neg, el)
    v2 = jnp.max(el2, axis=1, keepdims=True)
    i2 = jnp.min(jnp.where(el2 == v2, lane, big), axis=1, keepdims=True)
    e21 = jnp.exp(v2 - v1)
    w1 = g_w / (1.0 + e21)
    w2 = g_w * e21 / (1.0 + e21)

    row = lax.broadcasted_iota(I32, (TM, CONV_WIDTH), 0)
    h6 = halo_s[6:7, :]
    h7 = halo_s[7:8, :]
    z1 = jnp.where(row == 0, h7, pltpu.roll(zc, 1, 0))
    z2 = jnp.where(row == 0, h6, jnp.where(row == 1, h7, pltpu.roll(zc, 2, 0)))
    conv = z2 * wconv_ref[0:1, :] + z1 * wconv_ref[1:2, :] + zc * wconv_ref[2:3, :]
    halo_s[...] = zc[TM - 8:TM, :]
    mix_s[:, 0:CONV_WIDTH] = (proj(0) * conv).astype(BF16)

    sel1 = lane == i1
    sel2 = lane == i2
    onehot = jnp.where(sel1 | sel2, 1.0, 0.0)
    counts = jnp.sum(onehot, axis=0, keepdims=True)
    blocks = jnp.ceil(counts * (1.0 / BLOCK))
    er = lax.broadcasted_iota(I32, (LANES, LANES), 0)
    ec = lax.broadcasted_iota(I32, (LANES, LANES), 1)
    before = (er < ec).astype(BF16)
    run_start = BLOCK * jnp.dot(jnp.broadcast_to(blocks, (SUBLANES, LANES)).astype(BF16), before,
                                   preferred_element_type=F32)[0:1, :]
    rank = jnp.dot(tri_s[...], onehot.astype(BF16), preferred_element_type=F32) + run_start
    pos1 = jnp.sum(jnp.where(sel1, rank, 0.0), axis=1, keepdims=True)
    pos2 = jnp.sum(jnp.where(sel2, rank, 0.0), axis=1, keepdims=True)

    route = jnp.where(lane == 0, i1 - ROUTE_LANE0, 0.0)
    route = jnp.where(lane == 1, i2 - ROUTE_LANE0, route)
    route = jnp.where(lane == 2, w1, route)
    route = jnp.where(lane == 3, w2, route)
    route = jnp.where(lane == 4, pos1, route)
    route = jnp.where(lane == 5, pos2, route)
    route_ref[...] = route
    cnt_ref[...] = jnp.broadcast_to(counts, cnt_ref.shape)

    p1h = jnp.floor(pos1 * (1.0 / POS_SPLIT))
    p2h = jnp.floor(pos2 * (1.0 / POS_SPLIT))
    pv = jnp.where(lane == 0, p1h, 0.0)
    pv = jnp.where(lane == 1, pos1 - POS_SPLIT * p1h, pv)
    pv = jnp.where(lane == 2, p2h, pv)
    pv = jnp.where(lane == 3, pos2 - POS_SPLIT * p2h, pv)
    pick = (lax.broadcasted_iota(I32, (SUBLANES, LANES), 0)
            == lax.broadcasted_iota(I32, (SUBLANES, LANES), 1)).astype(BF16)
    prow = lax.dot_general(pick, pv.astype(BF16), (((1,), (1,)), ((), ())),
                           preferred_element_type=F32)
    pos1_row = POS_SPLIT * prow[0:1, :] + prow[1:2, :]
    pos2_row = POS_SPLIT * prow[2:3, :] + prow[3:4, :]

    def sort_rows(lo, hi):
        out_row = (lax.broadcasted_iota(I32, (hi - lo, TM), 0) + lo).astype(F32)
        sort_mat = jnp.where((out_row == pos1_row) | (out_row == pos2_row), 1.0, 0.0).astype(BF16)
        stage_ref[lo:hi, :] = jnp.dot(sort_mat, h2b, preferred_element_type=F32).astype(BF16)

    u = jax.nn.gelu(proj(3))
    sort_rows(0, SORT_MAIN // 2)
    v = jax.nn.gelu(proj(4))
    vc = v - jnp.mean(v, axis=-1, keepdims=True)
    vn = vc * lax.rsqrt(jnp.mean(vc * vc, axis=-1, keepdims=True) + EPS) * gsgu_ref[...]
    vnb = vn.astype(BF16)
    left = lax.broadcasted_iota(I32, (CHUNK, LANES), 1) < (LANES // 2)
    zero = jnp.zeros((CHUNK, LANES), BF16)
    for j in range(HEAD_PAIRS):
        cols = []
        for c in range(nch):
            s = vnb[c * CHUNK:(c + 1) * CHUNK, j * LANES:(j + 1) * LANES]
            cols.append(jnp.concatenate([jnp.where(left, s, zero), jnp.where(left, zero, s)], axis=0))
        rhs = jnp.concatenate(cols, axis=1)
        res = jnp.dot(wcat_s[j], rhs, preferred_element_type=F32)
        bias = bsp_ref[:, j * LANES:(j + 1) * LANES]
        for c in range(nch):
            mixed = res[:, c * LANES:(c + 1) * LANES] + bias
            uu = u[c * CHUNK:(c + 1) * CHUNK, j * LANES:(j + 1) * LANES]
            mix_s[c * CHUNK:(c + 1) * CHUNK,
                  CONV_WIDTH + j * LANES:CONV_WIDTH + (j + 1) * LANES] = (uu * mixed).astype(BF16)

    sort_rows(SORT_MAIN // 2, SORT_MAIN)
    x1 = x + jnp.dot(mix_s[...], wout_ref[...], preferred_element_type=F32)
    x1_ref[...] = x1
    h2_s[i % 2] = _rms(x1, gffn_ref[...]).astype(BF16)

    used_rows = BLOCK * jnp.sum(blocks)
    for lo in range(SORT_MAIN, KOUT, TAIL):
        @pl.when(used_rows > lo)
        def _tail(lo=lo):
            sort_rows(lo, lo + TAIL)

        @pl.when(used_rows <= lo)
        def _empty_tail(lo=lo):
            stage_ref[lo:lo + TAIL, :] = jnp.zeros((TAIL, D_MODEL), BF16)


def _mixer_router(x2d, g_mix, w_in, w_conv, g_sgu, w_sp, b_sp, w_out, g_ffn, w_rt, b_rt, seq):
    t = x2d.shape[0]
    nt = t // TM
    full = lambda a: pl.BlockSpec(a.shape, lambda i: (0,) * a.ndim)
    ins = (g_mix, w_in, w_conv, g_sgu, w_sp, b_sp, w_out, g_ffn, w_rt, b_rt)
    prev = lambda i: (jnp.where(i == 0, nt, i - 1), 0)
    return pl.pallas_call(
        functools.partial(_mixer_router_kernel, seq // TM),
        grid=(nt + 1,),
        in_specs=[pl.BlockSpec((TM, D_MODEL), lambda i: (jnp.minimum(i, nt - 1), 0))]
        + [full(a) for a in ins],
        out_specs=[pl.BlockSpec((TM, D_MODEL), lambda i: (i, 0)),
                   pl.BlockSpec((TM, LANES), prev),
                   pl.BlockSpec((SUBLANES, LANES), prev),
                   pl.BlockSpec((KOUT, D_MODEL), prev)],
        out_shape=[jax.ShapeDtypeStruct(((nt + 1) * TM, D_MODEL), F32),
                   jax.ShapeDtypeStruct(((nt + 1) * TM, LANES), F32),
                   jax.ShapeDtypeStruct(((nt + 1) * SUBLANES, LANES), F32),
                   jax.ShapeDtypeStruct(((nt + 1) * KOUT, D_MODEL), BF16)],
        scratch_shapes=[pltpu.VMEM((HEAD_PAIRS, CHUNK, 2 * CHUNK), BF16),
                        pltpu.VMEM((TM, TM), BF16),
                        pltpu.VMEM((8, CONV_WIDTH), F32),
                        pltpu.VMEM((TM, D_MODEL), BF16),
                        pltpu.VMEM((2, TM, D_MODEL), BF16),
                        pltpu.VMEM(w_in.shape, BF16),
                        pltpu.VMEM(w_out.shape, BF16),
                        pltpu.VMEM(w_rt.shape, BF16)],
        compiler_params=pltpu.CompilerParams(dimension_semantics=("arbitrary",),
                                             vmem_limit_bytes=VMEM_LIMIT),
        name="mixer_router",
    )(x2d, *ins)


def _expert_kernel(te_ref, nu_ref, nxt_ref, src_ref, dst_ref, stage_ref, wg_hbm, wu_hbm, wd_hbm,
                   out_ref, xbuf, obuf, wg_f, wu_f, wd_f, wg_s, wu_s, wd_s, sem, osem, wsem):
    i = pl.program_id(0)
    n_used = nu_ref[0]
    slot = i % 2

    def gather_copy(tile, slot_, j):
        src = pl.multiple_of(src_ref[tile * BLOCKS_PER_ROW_TILE + j] * BLOCK, BLOCK)
        return pltpu.make_async_copy(stage_ref.at[pl.ds(src, BLOCK)],
                                     xbuf.at[slot_, pl.ds(j * BLOCK, BLOCK)], sem.at[slot_])

    def scatter_copy(tile, j):
        dst = pl.multiple_of(dst_ref[(tile + 1) * BLOCKS_PER_ROW_TILE + j] * BLOCK, BLOCK)
        slot_ = (tile + 2) % 2
        return pltpu.make_async_copy(obuf.at[slot_, pl.ds(j * BLOCK, BLOCK)],
                                     out_ref.at[pl.ds(dst, BLOCK)], osem.at[slot_])

    def weight_copies(e):
        return (pltpu.make_async_copy(wg_hbm.at[e], wg_f, wsem.at[0]),
                pltpu.make_async_copy(wu_hbm.at[e], wu_f, wsem.at[1]),
                pltpu.make_async_copy(wd_hbm.at[e], wd_f, wsem.at[2]))

    @pl.when(i == 0)
    def _first():
        for j in range(BLOCKS_PER_ROW_TILE):
            gather_copy(0, 0, j).start()
        for cp in weight_copies(te_ref[0]):
            cp.start()
        obuf[1] = jnp.zeros((TR, D_MODEL), BF16)

    @pl.when((i >= 1) & (i - 2 < n_used))
    def _free_out_slot():
        for j in range(BLOCKS_PER_ROW_TILE):
            scatter_copy(i - 2, j).wait()

    @pl.when(i == n_used)
    def _after_last_tile():
        for j in range(BLOCKS_PER_ROW_TILE):
            gather_copy(n_used - 1, slot, j).wait()
        for j in range(BLOCKS_PER_ROW_TILE):
            scatter_copy(n_used - 1, j).start()

    @pl.when(i < n_used)
    def _tile():
        e = te_ref[i]
        prev = te_ref[jnp.maximum(i - 1, 0)]

        @pl.when((i == 0) | (e != prev))
        def _new_expert():
            for cp in weight_copies(e):
                cp.wait()
            wg_s[...] = wg_f[...].astype(BF16)
            wu_s[...] = wu_f[...].astype(BF16)
            wd_s[...] = wd_f[...].astype(BF16)
            nxt = nxt_ref[i]

            @pl.when(nxt >= 0)
            def _next_weights():
                for cp in weight_copies(nxt):
                    cp.start()

        for j in range(BLOCKS_PER_ROW_TILE):
            gather_copy(i, slot, j).wait()

        hb = xbuf[slot]
        a = jnp.dot(hb, wg_s[...], preferred_element_type=F32)
        b = jnp.dot(hb, wu_s[...], preferred_element_type=F32)
        nxt_tile = jnp.minimum(i + 1, n_used - 1)
        for j in range(BLOCKS_PER_ROW_TILE):
            gather_copy(nxt_tile, 1 - slot, j).start()
        for j in range(BLOCKS_PER_ROW_TILE):
            scatter_copy(i - 1, j).start()
        hid = (a * jax.nn.sigmoid(a) * b).astype(BF16)
        obuf[slot] = jnp.dot(hid, wd_s[...], preferred_element_type=F32).astype(BF16)


def _experts(tile_expert, n_used, next_expert, src_block, dst_block, stage, w_gate, w_up, w_down,
             n_tiles):
    any_spec = pl.BlockSpec(memory_space=pl.ANY)
    return pl.pallas_call(
        _expert_kernel,
        grid_spec=pltpu.PrefetchScalarGridSpec(
            num_scalar_prefetch=5,
            grid=(n_tiles,),
            in_specs=[any_spec, any_spec, any_spec, any_spec],
            out_specs=any_spec,
            scratch_shapes=[pltpu.VMEM((2, TR, D_MODEL), BF16),
                            pltpu.VMEM((2, TR, D_MODEL), BF16),
                            pltpu.VMEM((D_MODEL, D_EXPERT), F32),
                            pltpu.VMEM((D_MODEL, D_EXPERT), F32),
                            pltpu.VMEM((D_EXPERT, D_MODEL), F32),
                            pltpu.VMEM((D_MODEL, D_EXPERT), BF16),
                            pltpu.VMEM((D_MODEL, D_EXPERT), BF16),
                            pltpu.VMEM((D_EXPERT, D_MODEL), BF16),
                            pltpu.SemaphoreType.DMA((2,)),
                            pltpu.SemaphoreType.DMA((2,)),
                            pltpu.SemaphoreType.DMA((3,))]),
        out_shape=jax.ShapeDtypeStruct(stage.shape, stage.dtype),
        input_output_aliases={5: 0},
        compiler_params=pltpu.CompilerParams(dimension_semantics=("arbitrary",),
                                             vmem_limit_bytes=VMEM_LIMIT),
        name="experts",
    )(tile_expert, n_used, next_expert, src_block, dst_block, stage, w_gate, w_up, w_down)


def _combine_kernel(used_ref, x1_ref, route_ref, p_ref, gple_ref, wpg_f32, wpp_f32, gfin_ref,
                    srt_ref, out_ref, y_s, wpg_ref, wpp_ref):
    i = pl.program_id(0)

    @pl.when(i == 0)
    def _init():
        wpg_ref[...] = wpg_f32[...].astype(BF16)
        wpp_ref[...] = wpp_f32[...].astype(BF16)

    route = route_ref[...]
    w1, w2, pos1, pos2 = (route[:, k:k + 1] for k in (2, 3, 4, 5))

    def unsort(lo, hi):
        col = (lax.broadcasted_iota(I32, (TM, hi - lo), 1) + lo).astype(F32)
        mat = (jnp.where(col == pos1, w1, 0.0) + jnp.where(col == pos2, w2, 0.0)).astype(BF16)
        return jnp.dot(mat, srt_ref[lo:hi, :], preferred_element_type=F32)

    y_s[...] = unsort(0, SORT_MAIN)
    for lo in range(SORT_MAIN, KOUT, TAIL):
        @pl.when(used_ref[i] > lo)
        def _tail(lo=lo):
            y_s[...] += unsort(lo, lo + TAIL)

    x2 = x1_ref[...] + y_s[...]
    hg = _rms(x2, gple_ref[...]).astype(BF16)
    gate = jax.nn.sigmoid(jnp.dot(hg, wpg_ref[...], preferred_element_type=F32))
    pe = jnp.dot(p_ref[...].astype(BF16), wpp_ref[...], preferred_element_type=F32)
    x3 = x2 + gate * pe
    out_ref[...] = _rms(x3, gfin_ref[...])


def _combine(used, x1, route, p2d, g_ple, w_pg, w_pp, g_fin, sorted_out):
    t = p2d.shape[0]
    full = lambda a: pl.BlockSpec(a.shape, lambda i, *_: (0,) * a.ndim)
    return pl.pallas_call(
        _combine_kernel,
        grid_spec=pltpu.PrefetchScalarGridSpec(
            num_scalar_prefetch=1,
            grid=(t // TM,),
            in_specs=[pl.BlockSpec((TM, D_MODEL), lambda i, *_: (i, 0)),
                      pl.BlockSpec((TM, LANES), lambda i, *_: (i, 0)),
                      pl.BlockSpec((TM, PLE_DIM), lambda i, *_: (i, 0)),
                      full(g_ple), full(w_pg), full(w_pp), full(g_fin),
                      pl.BlockSpec((KOUT, D_MODEL), lambda i, *_: (i, 0))],
            out_specs=pl.BlockSpec((TM, D_MODEL), lambda i, *_: (i, 0)),
            scratch_shapes=[pltpu.VMEM((TM, D_MODEL), F32),
                            pltpu.VMEM(w_pg.shape, BF16),
                            pltpu.VMEM(w_pp.shape, BF16)]),
        out_shape=jax.ShapeDtypeStruct((t, D_MODEL), F32),
        compiler_params=pltpu.CompilerParams(dimension_semantics=("arbitrary",),
                                             vmem_limit_bytes=VMEM_LIMIT),
        name="combine_ple",
    )(used, x1, route, p2d, g_ple, w_pg, w_pp, g_fin, sorted_out)


def _routing_tables(cnt, nt, n_tiles):
    n = cnt.reshape(-1, SUBLANES, LANES)[:nt, 0, ROUTE_LANE0:ROUTE_LANE0 + N_EXPERTS].astype(I32)
    nblk = (n + BLOCK - 1) // BLOCK
    loc = jnp.cumsum(nblk, axis=1) - nblk
    carry = jnp.cumsum(nblk, axis=0) - nblk
    total = jnp.sum(nblk, axis=0)
    tiles_e = (total + BLOCKS_PER_ROW_TILE - 1) // BLOCKS_PER_ROW_TILE
    tile_end = jnp.cumsum(tiles_e)
    seg0 = (tile_end - tiles_e) * BLOCKS_PER_ROW_TILE
    n_used = tile_end[-1:]

    tile_ids = jnp.minimum(jnp.arange(n_tiles, dtype=I32), n_used[0] - 1)
    tile_expert = jnp.sum(tile_ids[:, None] >= tile_end[None, :], axis=1).astype(I32)

    nb = n_tiles * BLOCKS_PER_ROW_TILE
    b = jnp.arange(nb, dtype=I32)
    oh_e = (tile_expert[:, None] == jnp.arange(N_EXPERTS, dtype=I32)[None, :]).astype(I32)
    per_block = lambda a: jnp.repeat(a, BLOCKS_PER_ROW_TILE, axis=0)
    pick_e = lambda tab: per_block(jnp.sum(oh_e[:, None, :] * tab[None, :, :], axis=2))
    off = b - per_block(jnp.sum(oh_e * seg0[None, :], axis=1))
    run_end = pick_e(carry + nblk)
    run_beg = pick_e(carry)
    run_loc = pick_e(loc)
    tile_of_b = jnp.sum((run_end <= off[:, None]).astype(I32), axis=1)
    valid = (tile_of_b < nt) & (b < n_used[0] * BLOCKS_PER_ROW_TILE)
    oh_t = (jnp.minimum(tile_of_b, nt - 1)[:, None] == jnp.arange(nt, dtype=I32)[None, :]).astype(I32)
    src = (jnp.sum(oh_t * (run_loc - run_beg), axis=1) + off
           + jnp.minimum(tile_of_b, nt - 1) * (KOUT // BLOCK))
    zero_block = KOUT // BLOCK - 1
    dump = nt * (KOUT // BLOCK) + b % DUMP_BLOCKS
    dst = jnp.where(valid, src, dump).astype(I32)
    dst = jnp.concatenate([dump[:BLOCKS_PER_ROW_TILE].astype(I32), dst])
    src = jnp.where(valid, src, zero_block).astype(I32)

    ids = jnp.arange(N_EXPERTS, dtype=I32)
    later = (tiles_e > 0)[None, :] & (ids[None, :] > ids[:, None])
    next_used = jnp.min(jnp.where(later, ids[None, :], N_EXPERTS), axis=1)
    next_used = jnp.where(next_used == N_EXPERTS, -1, next_used)
    next_expert = jnp.sum((tile_expert[:, None] == ids[None, :]) * next_used[None, :], axis=1)
    used_rows = jnp.sum(nblk, axis=1) * BLOCK
    return tile_expert, n_used.astype(I32), next_expert.astype(I32), src, dst, used_rows.astype(I32)


def kernel(x, p, g_mix, w_in, w_conv, g_sgu, w_spatial, b_spatial, w_out, g_ffn, w_group, b_group,
           w_router, b_router, w_gate, w_up, w_down, g_ple, w_ple_gate, w_ple_proj, g_final):
    bsz, seq, d = x.shape
    t = bsz * seq
    assert w_in.shape[0] == 1, "single-layer block"
    assert d == D_MODEL and seq % TM == 0 and TM % CHUNK == 0
    nt = t // TM
    max_rows = TOP_K * t + nt * N_EXPERTS * (BLOCK - 1)
    n_tiles = (max_rows + N_EXPERTS * (TR - 1)) // TR + 2
    l = 0

    w_rt = jnp.concatenate(
        [w_group[l], jnp.transpose(w_router[l], (1, 0, 2)).reshape(d, N_EXPERTS)], axis=1)
    w_rt = jnp.pad(w_rt, ((0, 0), (0, LANES - w_rt.shape[1])))
    b_rt = jnp.pad(jnp.concatenate([b_group[l], b_router[l].reshape(-1)]),
                   (0, LANES - N_GROUPS - N_EXPERTS)).reshape(1, LANES)
    b_sp = jnp.repeat(b_spatial[l].T, SGU_WIDTH // SGU_HEADS, axis=1)

    x1, route, cnt, stage = _mixer_router(
        x.reshape(t, d), g_mix[l].reshape(1, d), w_in[l], w_conv[l],
        g_sgu[l].reshape(1, -1), w_spatial[l], b_sp, w_out[l],
        g_ffn[l].reshape(1, d), w_rt, b_rt, seq)

    tile_expert, n_used, next_expert, src, dst, used = _routing_tables(cnt, nt, n_tiles)

    sorted_out = _experts(tile_expert, n_used, next_expert, src, dst, stage,
                          w_gate[l].reshape(N_EXPERTS, d, D_EXPERT),
                          w_up[l].reshape(N_EXPERTS, d, D_EXPERT),
                          w_down[l].reshape(N_EXPERTS, D_EXPERT, d), n_tiles)
    out = _combine(used, x1, route, p[l].reshape(t, PLE_DIM), g_ple[l].reshape(1, d),
                   w_ple_gate[l], w_ple_proj[l], g_final.reshape(1, d), sorted_out)
    return out.reshape(bsz, seq, d)
```

```python
import functools

import jax
import jax.numpy as jnp
from jax import lax
from jax.experimental import pallas as pl
from jax.experimental.pallas import tpu as pltpu

F32 = jnp.float32
BF16 = jnp.bfloat16
I32 = jnp.int32

EPS = 1e-6
D_MODEL = 1024
CONV_WIDTH = 512
SGU_WIDTH = 512
SGU_HEADS = 8
HEAD_PAIRS = SGU_HEADS // 2
CHUNK = 128
N_GROUPS = 4
EXPERTS_PER_GROUP = 8
N_EXPERTS = N_GROUPS * EXPERTS_PER_GROUP
TOP_K = 2
D_EXPERT = 512
PLE_DIM = 256
LANES = 128
SUBLANES = 8
BLOCK = 2 * SUBLANES
ROUTE_LANE0 = N_GROUPS

TM = 512
TR = 512
BLOCKS_PER_ROW_TILE = TR // BLOCK
KOUT = TOP_K * TM + N_EXPERTS * BLOCK
TAIL = 128
SORT_MAIN = KOUT - 2 * TAIL
DUMP_BLOCKS = 2 * BLOCKS_PER_ROW_TILE
assert DUMP_BLOCKS * BLOCK <= KOUT
WEIGHT_DMA_PRIORITY = 1
POS_SPLIT = 32
VMEM_LIMIT = 56 * 1024 * 1024


def _rms(x, g):
    return x * lax.rsqrt(jnp.mean(x * x, axis=-1, keepdims=True) + EPS) * g


def _mixer_router_kernel(tiles_per_seq,
                         x_ref, gmix_ref, win_f32, wconv_ref, gsgu_ref, wsp_ref, bsp_ref,
                         wout_f32, gffn_ref, wrt_f32, brt_ref,
                         x1_ref, route_ref, cnt_ref, stage_ref,
                         wcat_s, tri_s, halo_s, mix_s, h2_s, win_ref, wout_ref, wrt_ref):
    i = pl.program_id(0)
    nch = TM // CHUNK

    @pl.when(i == 0)
    def _init():
        r = lax.broadcasted_iota(I32, (CHUNK, CHUNK), 0)
        c = lax.broadcasted_iota(I32, (CHUNK, CHUNK), 1)
        causal = c <= r
        for j in range(HEAD_PAIRS):
            wa = jnp.where(causal, wsp_ref[2 * j], 0.0)
            wb = jnp.where(causal, wsp_ref[2 * j + 1], 0.0)
            wcat_s[j] = jnp.concatenate([wa, wb], axis=1).astype(BF16)
        rr = lax.broadcasted_iota(I32, (TM, TM), 0)
        cc = lax.broadcasted_iota(I32, (TM, TM), 1)
        tri_s[...] = (cc < rr).astype(BF16)
        h2_s[...] = jnp.zeros_like(h2_s)
        win_ref[...] = win_f32[...].astype(BF16)
        wout_ref[...] = wout_f32[...].astype(BF16)
        wrt_ref[...] = wrt_f32[...].astype(BF16)

    @pl.when(i % tiles_per_seq == 0)
    def _seq_start():
        halo_s[...] = jnp.zeros_like(halo_s)

    x = x_ref[...]
    hb = _rms(x, gmix_ref[...]).astype(BF16)
    h2b = h2_s[(i + 1) % 2]

    def proj(k):
        return jnp.dot(hb, win_ref[:, k * 512:(k + 1) * 512], preferred_element_type=F32)

    logits = jnp.dot(h2b, wrt_ref[...], preferred_element_type=F32) + brt_ref[...]
    zc = proj(1) * proj(2)

    lane = lax.broadcasted_iota(I32, (TM, LANES), 1).astype(F32)
    neg = jnp.float32(-jnp.inf)
    big = jnp.float32(1e9)
    is_g = lane < N_GROUPS
    gl = jnp.where(is_g, logits, neg)
    gmax = jnp.max(gl, axis=1, keepdims=True)
    gsum = jnp.sum(jnp.where(is_g, jnp.exp(gl - gmax), 0.0), axis=1, keepdims=True)
    g_w = 1.0 / gsum
    g_idx = jnp.min(jnp.where(gl == gmax, lane, big), axis=1, keepdims=True)
    lo_lane = ROUTE_LANE0 + EXPERTS_PER_GROUP * g_idx
    in_grp = (lane >= lo_lane) & (lane < lo_lane + EXPERTS_PER_GROUP)
    el = jnp.where(in_grp, logits, neg)
    v1 = jnp.max(el, axis=1, keepdims=True)
    i1 = jnp.min(jnp.where(el == v1, lane, big), axis=1, keepdims=True)
    el2 = jnp.where(lane == i1, neg, el)
    v2 = jnp.max(el2, axis=1, keepdims=True)
    i2 = jnp.min(jnp.where(el2 == v2, lane, big), axis=1, keepdims=True)
    e21 = jnp.exp(v2 - v1)
    w1 = g_w / (1.0 + e21)
    w2 = g_w * e21 / (1.0 + e21)

    row = lax.broadcasted_iota(I32, (TM, CONV_WIDTH), 0)
    h6 = halo_s[6:7, :]
    h7 = halo_s[7:8, :]
    z1 = jnp.where(row == 0, h7, pltpu.roll(zc, 1, 0))
    z2 = jnp.where(row == 0, h6, jnp.where(row == 1, h7, pltpu.roll(zc, 2, 0)))
    conv = z2 * wconv_ref[0:1, :] + z1 * wconv_ref[1:2, :] + zc * wconv_ref[2:3, :]
    halo_s[...] = zc[TM - 8:TM, :]
    mix_s[:, 0:CONV_WIDTH] = (proj(0) * conv).astype(BF16)

    sel1 = lane == i1
    sel2 = lane == i2
    onehot = jnp.where(sel1 | sel2, 1.0, 0.0)
    counts = jnp.sum(onehot, axis=0, keepdims=True)
    blocks = jnp.ceil(counts * (1.0 / BLOCK))
    er = lax.broadcasted_iota(I32, (LANES, LANES), 0)
    ec = lax.broadcasted_iota(I32, (LANES, LANES), 1)
    before = (er < ec).astype(BF16)
    run_start = BLOCK * jnp.dot(jnp.broadcast_to(blocks, (SUBLANES, LANES)).astype(BF16), before,
                                   preferred_element_type=F32)[0:1, :]
    rank = jnp.dot(tri_s[...], onehot.astype(BF16), preferred_element_type=F32) + run_start
    pos1 = jnp.sum(jnp.where(sel1, rank, 0.0), axis=1, keepdims=True)
    pos2 = jnp.sum(jnp.where(sel2, rank, 0.0), axis=1, keepdims=True)

    route = jnp.where(lane == 0, i1 - ROUTE_LANE0, 0.0)
    route = jnp.where(lane == 1, i2 - ROUTE_LANE0, route)
    route = jnp.where(lane == 2, w1, route)
    route = jnp.where(lane == 3, w2, route)
    route = jnp.where(lane == 4, pos1, route)
    route = jnp.where(lane == 5, pos2, route)
    route_ref[...] = route
    cnt_ref[...] = jnp.broadcast_to(counts, cnt_ref.shape)

    p1h = jnp.floor(pos1 * (1.0 / POS_SPLIT))
    p2h = jnp.floor(pos2 * (1.0 / POS_SPLIT))
    pv = jnp.where(lane == 0, p1h, 0.0)
    pv = jnp.where(lane == 1, pos1 - POS_SPLIT * p1h, pv)
    pv = jnp.where(lane == 2, p2h, pv)
    pv = jnp.where(lane == 3, pos2 - POS_SPLIT * p2h, pv)
    pick = (lax.broadcasted_iota(I32, (SUBLANES, LANES), 0)
            == lax.broadcasted_iota(I32, (SUBLANES, LANES), 1)).astype(BF16)
    prow = lax.dot_general(pick, pv.astype(BF16), (((1,), (1,)), ((), ())),
                           preferred_element_type=F32)
    pos1_row = POS_SPLIT * prow[0:1, :] + prow[1:2, :]
    pos2_row = POS_SPLIT * prow[2:3, :] + prow[3:4, :]

    def sort_rows(lo, hi):
        out_row = (lax.broadcasted_iota(I32, (hi - lo, TM), 0) + lo).astype(F32)
        sort_mat = jnp.where((out_row == pos1_row) | (out_row == pos2_row), 1.0, 0.0).astype(BF16)
        stage_ref[lo:hi, :] = jnp.dot(sort_mat, h2b, preferred_element_type=F32).astype(BF16)

    u = jax.nn.gelu(proj(3))
    sort_rows(0, SORT_MAIN // 2)
    v = jax.nn.gelu(proj(4))
    vc = v - jnp.mean(v, axis=-1, keepdims=True)
    vn = vc * lax.rsqrt(jnp.mean(vc * vc, axis=-1, keepdims=True) + EPS) * gsgu_ref[...]
    vnb = vn.astype(BF16)
    left = lax.broadcasted_iota(I32, (CHUNK, LANES), 1) < (LANES // 2)
    zero = jnp.zeros((CHUNK, LANES), BF16)
    for j in range(HEAD_PAIRS):
        cols = []
        for c in range(nch):
            s = vnb[c * CHUNK:(c + 1) * CHUNK, j * LANES:(j + 1) * LANES]
            cols.append(jnp.concatenate([jnp.where(left, s, zero), jnp.where(left, zero, s)], axis=0))
        rhs = jnp.concatenate(cols, axis=1)
        res = jnp.dot(wcat_s[j], rhs, preferred_element_type=F32)
        bias = bsp_ref[:, j * LANES:(j + 1) * LANES]
        for c in range(nch):
            mixed = res[:, c * LANES:(c + 1) * LANES] + bias
            uu = u[c * CHUNK:(c + 1) * CHUNK, j * LANES:(j + 1) * LANES]
            mix_s[c * CHUNK:(c + 1) * CHUNK,
                  CONV_WIDTH + j * LANES:CONV_WIDTH + (j + 1) * LANES] = (uu * mixed).astype(BF16)

    sort_rows(SORT_MAIN // 2, SORT_MAIN)
    x1 = x + jnp.dot(mix_s[...], wout_ref[...], preferred_element_type=F32)
    x1_ref[...] = x1
    h2_s[i % 2] = _rms(x1, gffn_ref[...]).astype(BF16)

    used_rows = BLOCK * jnp.sum(blocks)
    for lo in range(SORT_MAIN, KOUT, TAIL):
        @pl.when(used_rows > lo)
        def _tail(lo=lo):
            sort_rows(lo, lo + TAIL)

        @pl.when(used_rows <= lo)
        def _empty_tail(lo=lo):
            stage_ref[lo:lo + TAIL, :] = jnp.zeros((TAIL, D_MODEL), BF16)


def _mixer_router(x2d, g_mix, w_in, w_conv, g_sgu, w_sp, b_sp, w_out, g_ffn, w_rt, b_rt, seq):
    t = x2d.shape[0]
    nt = t // TM
    full = lambda a: pl.BlockSpec(a.shape, lambda i: (0,) * a.ndim)
    ins = (g_mix, w_in, w_conv, g_sgu, w_sp, b_sp, w_out, g_ffn, w_rt, b_rt)
    prev = lambda i: (jnp.where(i == 0, nt, i - 1), 0)
    return pl.pallas_call(
        functools.partial(_mixer_router_kernel, seq // TM),
        grid=(nt + 1,),
        in_specs=[pl.BlockSpec((TM, D_MODEL), lambda i: (jnp.minimum(i, nt - 1), 0))]
        + [full(a) for a in ins],
        out_specs=[pl.BlockSpec((TM, D_MODEL), lambda i: (i, 0)),
                   pl.BlockSpec((TM, LANES), prev),
                   pl.BlockSpec((SUBLANES, LANES), prev),
                   pl.BlockSpec((KOUT, D_MODEL), prev)],
        out_shape=[jax.ShapeDtypeStruct(((nt + 1) * TM, D_MODEL), F32),
                   jax.ShapeDtypeStruct(((nt + 1) * TM, LANES), F32),
                   jax.ShapeDtypeStruct(((nt + 1) * SUBLANES, LANES), F32),
                   jax.ShapeDtypeStruct(((nt + 1) * KOUT, D_MODEL), BF16)],
        scratch_shapes=[pltpu.VMEM((HEAD_PAIRS, CHUNK, 2 * CHUNK), BF16),
                        pltpu.VMEM((TM, TM), BF16),
                        pltpu.VMEM((8, CONV_WIDTH), F32),
                        pltpu.VMEM((TM, D_MODEL), BF16),
                        pltpu.VMEM((2, TM, D_MODEL), BF16),
                        pltpu.VMEM(w_in.shape, BF16),
                        pltpu.VMEM(w_out.shape, BF16),
                        pltpu.VMEM(w_rt.shape, BF16)],
        compiler_params=pltpu.CompilerParams(dimension_semantics=("arbitrary",),
                                             vmem_limit_bytes=VMEM_LIMIT),
        name="mixer_router",
    )(x2d, *ins)


def _expert_kernel(te_ref, nu_ref, nxt_ref, src_ref, dst_ref, stage_ref, wg_hbm, wu_hbm, wd_hbm,
                   out_ref, xbuf, obuf, wg_f, wu_f, wd_f, wg_s, wu_s, wd_s, sem, osem, wsem):
    i = pl.program_id(0)
    n_used = nu_ref[0]
    slot = i % 2

    def gather_copy(tile, slot_, j):
        src = pl.multiple_of(src_ref[tile * BLOCKS_PER_ROW_TILE + j] * BLOCK, BLOCK)
        return pltpu.make_async_copy(stage_ref.at[pl.ds(src, BLOCK)],
                                     xbuf.at[slot_, pl.ds(j * BLOCK, BLOCK)], sem.at[slot_])

    def scatter_copy(tile, j):
        dst = pl.multiple_of(dst_ref[(tile + 1) * BLOCKS_PER_ROW_TILE + j] * BLOCK, BLOCK)
        slot_ = (tile + 2) % 2
        return pltpu.make_async_copy(obuf.at[slot_, pl.ds(j * BLOCK, BLOCK)],
                                     out_ref.at[pl.ds(dst, BLOCK)], osem.at[slot_])

    def weight_copies(e):
        return (pltpu.make_async_copy(wg_hbm.at[e], wg_f, wsem.at[0]),
                pltpu.make_async_copy(wu_hbm.at[e], wu_f, wsem.at[1]),
                pltpu.make_async_copy(wd_hbm.at[e], wd_f, wsem.at[2]))

    @pl.when(i == 0)
    def _first():
        for j in range(BLOCKS_PER_ROW_TILE):
            gather_copy(0, 0, j).start()
        for cp in weight_copies(te_ref[0]):
            cp.start(priority=WEIGHT_DMA_PRIORITY)
        obuf[1] = jnp.zeros((TR, D_MODEL), BF16)

    @pl.when((i >= 1) & (i - 2 < n_used))
    def _free_out_slot():
        for j in range(BLOCKS_PER_ROW_TILE):
            scatter_copy(i - 2, j).wait()

    @pl.when(i == n_used)
    def _after_last_tile():
        for j in range(BLOCKS_PER_ROW_TILE):
            gather_copy(n_used - 1, slot, j).wait()
        for j in range(BLOCKS_PER_ROW_TILE):
            scatter_copy(n_used - 1, j).start()

    @pl.when(i < n_used)
    def _tile():
        e = te_ref[i]
        prev = te_ref[jnp.maximum(i - 1, 0)]

        @pl.when((i == 0) | (e != prev))
        def _new_expert():
            for cp in weight_copies(e):
                cp.wait()
            wg_s[...] = wg_f[...].astype(BF16)
            wu_s[...] = wu_f[...].astype(BF16)
            wd_s[...] = wd_f[...].astype(BF16)
            nxt = nxt_ref[i]

            @pl.when(nxt >= 0)
            def _next_weights():
                for cp in weight_copies(nxt):
                    cp.start(priority=WEIGHT_DMA_PRIORITY)

        for j in range(BLOCKS_PER_ROW_TILE):
            gather_copy(i, slot, j).wait()

        hb = xbuf[slot]
        a = jnp.dot(hb, wg_s[...], preferred_element_type=F32)
        b = jnp.dot(hb, wu_s[...], preferred_element_type=F32)
        nxt_tile = jnp.minimum(i + 1, n_used - 1)
        for j in range(BLOCKS_PER_ROW_TILE):
            gather_copy(nxt_tile, 1 - slot, j).start()
        for j in range(BLOCKS_PER_ROW_TILE):
            scatter_copy(i - 1, j).start()
        hid = (a * jax.nn.sigmoid(a) * b).astype(BF16)
        obuf[slot] = jnp.dot(hid, wd_s[...], preferred_element_type=F32).astype(BF16)


def _experts(tile_expert, n_used, next_expert, src_block, dst_block, stage, w_gate, w_up, w_down,
             n_tiles):
    any_spec = pl.BlockSpec(memory_space=pl.ANY)
    return pl.pallas_call(
        _expert_kernel,
        grid_spec=pltpu.PrefetchScalarGridSpec(
            num_scalar_prefetch=5,
            grid=(n_tiles,),
            in_specs=[any_spec, any_spec, any_spec, any_spec],
            out_specs=any_spec,
            scratch_shapes=[pltpu.VMEM((2, TR, D_MODEL), BF16),
                            pltpu.VMEM((2, TR, D_MODEL), BF16),
                            pltpu.VMEM((D_MODEL, D_EXPERT), F32),
                            pltpu.VMEM((D_MODEL, D_EXPERT), F32),
                            pltpu.VMEM((D_EXPERT, D_MODEL), F32),
                            pltpu.VMEM((D_MODEL, D_EXPERT), BF16),
                            pltpu.VMEM((D_MODEL, D_EXPERT), BF16),
                            pltpu.VMEM((D_EXPERT, D_MODEL), BF16),
                            pltpu.SemaphoreType.DMA((2,)),
                            pltpu.SemaphoreType.DMA((2,)),
                            pltpu.SemaphoreType.DMA((3,))]),
        out_shape=jax.ShapeDtypeStruct(stage.shape, stage.dtype),
        input_output_aliases={5: 0},
        compiler_params=pltpu.CompilerParams(dimension_semantics=("arbitrary",),
                                             vmem_limit_bytes=VMEM_LIMIT),
        name="experts",
    )(tile_expert, n_used, next_expert, src_block, dst_block, stage, w_gate, w_up, w_down)


def _combine_kernel(used_ref, x1_ref, route_ref, p_ref, gple_ref, wpg_f32, wpp_f32, gfin_ref,
                    srt_ref, out_ref, y_s, wpg_ref, wpp_ref):
    i = pl.program_id(0)

    @pl.when(i == 0)
    def _init():
        wpg_ref[...] = wpg_f32[...].astype(BF16)
        wpp_ref[...] = wpp_f32[...].astype(BF16)

    route = route_ref[...]
    w1, w2, pos1, pos2 = (route[:, k:k + 1] for k in (2, 3, 4, 5))

    def unsort(lo, hi):
        col = (lax.broadcasted_iota(I32, (TM, hi - lo), 1) + lo).astype(F32)
        mat = (jnp.where(col == pos1, w1, 0.0) + jnp.where(col == pos2, w2, 0.0)).astype(BF16)
        return jnp.dot(mat, srt_ref[lo:hi, :], preferred_element_type=F32)

    y_s[...] = unsort(0, SORT_MAIN)
    for lo in range(SORT_MAIN, KOUT, TAIL):
        @pl.when(used_ref[i] > lo)
        def _tail(lo=lo):
            y_s[...] += unsort(lo, lo + TAIL)

    x2 = x1_ref[...] + y_s[...]
    hg = _rms(x2, gple_ref[...]).astype(BF16)
    gate = jax.nn.sigmoid(jnp.dot(hg, wpg_ref[...], preferred_element_type=F32))
    pe = jnp.dot(p_ref[...].astype(BF16), wpp_ref[...], preferred_element_type=F32)
    x3 = x2 + gate * pe
    out_ref[...] = _rms(x3, gfin_ref[...])


def _combine(used, x1, route, p2d, g_ple, w_pg, w_pp, g_fin, sorted_out):
    t = p2d.shape[0]
    full = lambda a: pl.BlockSpec(a.shape, lambda i, *_: (0,) * a.ndim)
    return pl.pallas_call(
        _combine_kernel,
        grid_spec=pltpu.PrefetchScalarGridSpec(
            num_scalar_prefetch=1,
            grid=(t // TM,),
            in_specs=[pl.BlockSpec((TM, D_MODEL), lambda i, *_: (i, 0)),
                      pl.BlockSpec((TM, LANES), lambda i, *_: (i, 0)),
                      pl.BlockSpec((TM, PLE_DIM), lambda i, *_: (i, 0)),
                      full(g_ple), full(w_pg), full(w_pp), full(g_fin),
                      pl.BlockSpec((KOUT, D_MODEL), lambda i, *_: (i, 0))],
            out_specs=pl.BlockSpec((TM, D_MODEL), lambda i, *_: (i, 0)),
            scratch_shapes=[pltpu.VMEM((TM, D_MODEL), F32),
                            pltpu.VMEM(w_pg.shape, BF16),
                            pltpu.VMEM(w_pp.shape, BF16)]),
        out_shape=jax.ShapeDtypeStruct((t, D_MODEL), F32),
        compiler_params=pltpu.CompilerParams(dimension_semantics=("arbitrary",),
                                             vmem_limit_bytes=VMEM_LIMIT),
        name="combine_ple",
    )(used, x1, route, p2d, g_ple, w_pg, w_pp, g_fin, sorted_out)


def _routing_tables(cnt, nt, n_tiles):
    n = cnt.reshape(-1, SUBLANES, LANES)[:nt, 0, ROUTE_LANE0:ROUTE_LANE0 + N_EXPERTS].astype(I32)
    nblk = (n + BLOCK - 1) // BLOCK
    loc = jnp.cumsum(nblk, axis=1) - nblk
    carry = jnp.cumsum(nblk, axis=0) - nblk
    total = jnp.sum(nblk, axis=0)
    tiles_e = (total + BLOCKS_PER_ROW_TILE - 1) // BLOCKS_PER_ROW_TILE
    tile_end = jnp.cumsum(tiles_e)
    seg0 = (tile_end - tiles_e) * BLOCKS_PER_ROW_TILE
    n_used = tile_end[-1:]

    tile_ids = jnp.minimum(jnp.arange(n_tiles, dtype=I32), n_used[0] - 1)
    tile_expert = jnp.sum(tile_ids[:, None] >= tile_end[None, :], axis=1).astype(I32)

    nb = n_tiles * BLOCKS_PER_ROW_TILE
    b = jnp.arange(nb, dtype=I32)
    oh_e = (tile_expert[:, None] == jnp.arange(N_EXPERTS, dtype=I32)[None, :]).astype(I32)
    per_block = lambda a: jnp.repeat(a, BLOCKS_PER_ROW_TILE, axis=0)
    pick_e = lambda tab: per_block(jnp.sum(oh_e[:, None, :] * tab[None, :, :], axis=2))
    off = b - per_block(jnp.sum(oh_e * seg0[None, :], axis=1))
    run_end = pick_e(carry + nblk)
    run_beg = pick_e(carry)
    run_loc = pick_e(loc)
    tile_of_b = jnp.sum((run_end <= off[:, None]).astype(I32), axis=1)
    valid = (tile_of_b < nt) & (b < n_used[0] * BLOCKS_PER_ROW_TILE)
    oh_t = (jnp.minimum(tile_of_b, nt - 1)[:, None] == jnp.arange(nt, dtype=I32)[None, :]).astype(I32)
    src = (jnp.sum(oh_t * (run_loc - run_beg), axis=1) + off
           + jnp.minimum(tile_of_b, nt - 1) * (KOUT // BLOCK))
    zero_block = KOUT // BLOCK - 1
    dump = nt * (KOUT // BLOCK) + b % DUMP_BLOCKS
    dst = jnp.where(valid, src, dump).astype(I32)
    dst = jnp.concatenate([dump[:BLOCKS_PER_ROW_TILE].astype(I32), dst])
    src = jnp.where(valid, src, zero_block).astype(I32)

    ids = jnp.arange(N_EXPERTS, dtype=I32)
    later = (tiles_e > 0)[None, :] & (ids[None, :] > ids[:, None])
    next_used = jnp.min(jnp.where(later, ids[None, :], N_EXPERTS), axis=1)
    next_used = jnp.where(next_used == N_EXPERTS, -1, next_used)
    next_expert = jnp.sum((tile_expert[:, None] == ids[None, :]) * next_used[None, :], axis=1)
    used_rows = jnp.sum(nblk, axis=1) * BLOCK
    return tile_expert, n_used.astype(I32), next_expert.astype(I32), src, dst, used_rows.astype(I32)


def kernel(x, p, g_mix, w_in, w_conv, g_sgu, w_spatial, b_spatial, w_out, g_ffn, w_group, b_group,
           w_router, b_router, w_gate, w_up, w_down, g_ple, w_ple_gate, w_ple_proj, g_final):
    bsz, seq, d = x.shape
    t = bsz * seq
    assert w_in.shape[0] == 1, "single-layer block"
    assert d == D_MODEL and seq % TM == 0 and TM % CHUNK == 0
    nt = t // TM
    max_rows = TOP_K * t + nt * N_EXPERTS * (BLOCK - 1)
    n_tiles = (max_rows + N_EXPERTS * (TR - 1)) // TR + 2
    l = 0

    w_rt = jnp.concatenate(
        [w_group[l], jnp.transpose(w_router[l], (1, 0, 2)).reshape(d, N_EXPERTS)], axis=1)
    w_rt = jnp.pad(w_rt, ((0, 0), (0, LANES - w_rt.shape[1])))
    b_rt = jnp.pad(jnp.concatenate([b_group[l], b_router[l].reshape(-1)]),
                   (0, LANES - N_GROUPS - N_EXPERTS)).reshape(1, LANES)
    b_sp = jnp.repeat(b_spatial[l].T, SGU_WIDTH // SGU_HEADS, axis=1)

    x1, route, cnt, stage = _mixer_router(
        x.reshape(t, d), g_mix[l].reshape(1, d), w_in[l], w_conv[l],
        g_sgu[l].reshape(1, -1), w_spatial[l], b_sp, w_out[l],
        g_ffn[l].reshape(1, d), w_rt, b_rt, seq)

    tile_expert, n_used, next_expert, src, dst, used = _routing_tables(cnt, nt, n_tiles)

    sorted_out = _experts(tile_expert, n_used, next_expert, src, dst, stage,
                          w_gate[l].reshape(N_EXPERTS, d, D_EXPERT),
                          w_up[l].reshape(N_EXPERTS, d, D_EXPERT),
                          w_down[l].reshape(N_EXPERTS, D_EXPERT, d), n_tiles)
    out = _combine(used, x1, route, p[l].reshape(t, PLE_DIM), g_ple[l].reshape(1, d),
                   w_ple_gate[l], w_ple_proj[l], g_final.reshape(1, d), sorted_out)
    return out.reshape(bsz, seq, d)
```

```python
import functools

import jax
import jax.numpy as jnp
from jax import lax
from jax.experimental import pallas as pl
from jax.experimental.pallas import tpu as pltpu

F32 = jnp.float32
BF16 = jnp.bfloat16
I32 = jnp.int32

EPS = 1e-6
D_MODEL = 1024
CONV_WIDTH = 512
SGU_WIDTH = 512
SGU_HEADS = 8
HEAD_PAIRS = SGU_HEADS // 2
CHUNK = 128
N_GROUPS = 4
EXPERTS_PER_GROUP = 8
N_EXPERTS = N_GROUPS * EXPERTS_PER_GROUP
TOP_K = 2
D_EXPERT = 512
PLE_DIM = 256
LANES = 128
SUBLANES = 8
BLOCK = 2 * SUBLANES
ROUTE_LANE0 = N_GROUPS

TM = 512
TR = 512
BLOCKS_PER_ROW_TILE = TR // BLOCK
KOUT = TOP_K * TM + N_EXPERTS * BLOCK
TAIL = 128
SORT_MAIN = KOUT - 2 * TAIL
N_SLOTS = 3
DUMP_BLOCKS = N_SLOTS * BLOCKS_PER_ROW_TILE
assert DUMP_BLOCKS * BLOCK <= KOUT
WEIGHT_DMA_PRIORITY = 1
POS_SPLIT = 32
VMEM_LIMIT = 56 * 1024 * 1024


def _rms(x, g):
    return x * lax.rsqrt(jnp.mean(x * x, axis=-1, keepdims=True) + EPS) * g


def _mixer_router_kernel(tiles_per_seq,
                         x_ref, gmix_ref, win_f32, wconv_ref, gsgu_ref, wsp_ref, bsp_ref,
                         wout_f32, gffn_ref, wrt_f32, brt_ref,
                         x1_ref, route_ref, cnt_ref, stage_ref,
                         wcat_s, tri_s, halo_s, mix_s, h2_s, win_ref, wout_ref, wrt_ref):
    i = pl.program_id(0)
    nch = TM // CHUNK

    @pl.when(i == 0)
    def _init():
        r = lax.broadcasted_iota(I32, (CHUNK, CHUNK), 0)
        c = lax.broadcasted_iota(I32, (CHUNK, CHUNK), 1)
        causal = c <= r
        for j in range(HEAD_PAIRS):
            wa = jnp.where(causal, wsp_ref[2 * j], 0.0)
            wb = jnp.where(causal, wsp_ref[2 * j + 1], 0.0)
            wcat_s[j] = jnp.concatenate([wa, wb], axis=1).astype(BF16)
        rr = lax.broadcasted_iota(I32, (TM, TM), 0)
        cc = lax.broadcasted_iota(I32, (TM, TM), 1)
        tri_s[...] = (cc < rr).astype(BF16)
        h2_s[...] = jnp.zeros_like(h2_s)
        win_ref[...] = win_f32[...].astype(BF16)
        wout_ref[...] = wout_f32[...].astype(BF16)
        wrt_ref[...] = wrt_f32[...].astype(BF16)

    @pl.when(i % tiles_per_seq == 0)
    def _seq_start():
        halo_s[...] = jnp.zeros_like(halo_s)

    x = x_ref[...]
    hb = _rms(x, gmix_ref[...]).astype(BF16)
    h2b = h2_s[(i + 1) % 2]

    def proj(k):
        return jnp.dot(hb, win_ref[:, k * 512:(k + 1) * 512], preferred_element_type=F32)

    logits = jnp.dot(h2b, wrt_ref[...], preferred_element_type=F32) + brt_ref[...]
    zc = proj(1) * proj(2)

    lane = lax.broadcasted_iota(I32, (TM, LANES), 1).astype(F32)
    neg = jnp.float32(-jnp.inf)
    big = jnp.float32(1e9)
    is_g = lane < N_GROUPS
    gl = jnp.where(is_g, logits, neg)
    gmax = jnp.max(gl, axis=1, keepdims=True)
    gsum = jnp.sum(jnp.where(is_g, jnp.exp(gl - gmax), 0.0), axis=1, keepdims=True)
    g_w = 1.0 / gsum
    g_idx = jnp.min(jnp.where(gl == gmax, lane, big), axis=1, keepdims=True)
    lo_lane = ROUTE_LANE0 + EXPERTS_PER_GROUP * g_idx
    in_grp = (lane >= lo_lane) & (lane < lo_lane + EXPERTS_PER_GROUP)
    el = jnp.where(in_grp, logits, neg)
    v1 = jnp.max(el, axis=1, keepdims=True)
    i1 = jnp.min(jnp.where(el == v1, lane, big), axis=1, keepdims=True)
    el2 = jnp.where(lane == i1, neg, el)
    v2 = jnp.max(el2, axis=1, keepdims=True)
    i2 = jnp.min(jnp.where(el2 == v2, lane, big), axis=1, keepdims=True)
    e21 = jnp.exp(v2 - v1)
    w1 = g_w / (1.0 + e21)
    w2 = g_w * e21 / (1.0 + e21)

    row = lax.broadcasted_iota(I32, (TM, CONV_WIDTH), 0)
    h6 = halo_s[6:7, :]
    h7 = halo_s[7:8, :]
    z1 = jnp.where(row == 0, h7, pltpu.roll(zc, 1, 0))
    z2 = jnp.where(row == 0, h6, jnp.where(row == 1, h7, pltpu.roll(zc, 2, 0)))
    conv = z2 * wconv_ref[0:1, :] + z1 * wconv_ref[1:2, :] + zc * wconv_ref[2:3, :]
    halo_s[...] = zc[TM - 8:TM, :]
    mix_s[:, 0:CONV_WIDTH] = (proj(0) * conv).astype(BF16)

    sel1 = lane == i1
    sel2 = lane == i2
    onehot = jnp.where(sel1 | sel2, 1.0, 0.0)
    counts = jnp.sum(onehot, axis=0, keepdims=True)
    blocks = jnp.ceil(counts * (1.0 / BLOCK))
    er = lax.broadcasted_iota(I32, (LANES, LANES), 0)
    ec = lax.broadcasted_iota(I32, (LANES, LANES), 1)
    before = (er < ec).astype(BF16)
    run_start = BLOCK * jnp.dot(jnp.broadcast_to(blocks, (SUBLANES, LANES)).astype(BF16), before,
                                   preferred_element_type=F32)[0:1, :]
    rank = jnp.dot(tri_s[...], onehot.astype(BF16), preferred_element_type=F32) + run_start
    pos1 = jnp.sum(jnp.where(sel1, rank, 0.0), axis=1, keepdims=True)
    pos2 = jnp.sum(jnp.where(sel2, rank, 0.0), axis=1, keepdims=True)

    route = jnp.where(lane == 0, i1 - ROUTE_LANE0, 0.0)
    route = jnp.where(lane == 1, i2 - ROUTE_LANE0, route)
    route = jnp.where(lane == 2, w1, route)
    route = jnp.where(lane == 3, w2, route)
    route = jnp.where(lane == 4, pos1, route)
    route = jnp.where(lane == 5, pos2, route)
    route_ref[...] = route
    cnt_ref[...] = jnp.broadcast_to(counts, cnt_ref.shape)

    p1h = jnp.floor(pos1 * (1.0 / POS_SPLIT))
    p2h = jnp.floor(pos2 * (1.0 / POS_SPLIT))
    pv = jnp.where(lane == 0, p1h, 0.0)
    pv = jnp.where(lane == 1, pos1 - POS_SPLIT * p1h, pv)
    pv = jnp.where(lane == 2, p2h, pv)
    pv = jnp.where(lane == 3, pos2 - POS_SPLIT * p2h, pv)
    pick = (lax.broadcasted_iota(I32, (SUBLANES, LANES), 0)
            == lax.broadcasted_iota(I32, (SUBLANES, LANES), 1)).astype(BF16)
    prow = lax.dot_general(pick, pv.astype(BF16), (((1,), (1,)), ((), ())),
                           preferred_element_type=F32)
    pos1_row = POS_SPLIT * prow[0:1, :] + prow[1:2, :]
    pos2_row = POS_SPLIT * prow[2:3, :] + prow[3:4, :]

    def sort_rows(lo, hi):
        out_row = (lax.broadcasted_iota(I32, (hi - lo, TM), 0) + lo).astype(F32)
        sort_mat = jnp.where((out_row == pos1_row) | (out_row == pos2_row), 1.0, 0.0).astype(BF16)
        stage_ref[lo:hi, :] = jnp.dot(sort_mat, h2b, preferred_element_type=F32).astype(BF16)

    u = jax.nn.gelu(proj(3))
    sort_rows(0, SORT_MAIN // 2)
    v = jax.nn.gelu(proj(4))
    vc = v - jnp.mean(v, axis=-1, keepdims=True)
    vn = vc * lax.rsqrt(jnp.mean(vc * vc, axis=-1, keepdims=True) + EPS) * gsgu_ref[...]
    vnb = vn.astype(BF16)
    left = lax.broadcasted_iota(I32, (CHUNK, LANES), 1) < (LANES // 2)
    zero = jnp.zeros((CHUNK, LANES), BF16)
    for j in range(HEAD_PAIRS):
        cols = []
        for c in range(nch):
            s = vnb[c * CHUNK:(c + 1) * CHUNK, j * LANES:(j + 1) * LANES]
            cols.append(jnp.concatenate([jnp.where(left, s, zero), jnp.where(left, zero, s)], axis=0))
        rhs = jnp.concatenate(cols, axis=1)
        res = jnp.dot(wcat_s[j], rhs, preferred_element_type=F32)
        bias = bsp_ref[:, j * LANES:(j + 1) * LANES]
        for c in range(nch):
            mixed = res[:, c * LANES:(c + 1) * LANES] + bias
            uu = u[c * CHUNK:(c + 1) * CHUNK, j * LANES:(j + 1) * LANES]
            mix_s[c * CHUNK:(c + 1) * CHUNK,
                  CONV_WIDTH + j * LANES:CONV_WIDTH + (j + 1) * LANES] = (uu * mixed).astype(BF16)

    sort_rows(SORT_MAIN // 2, SORT_MAIN)
    x1 = x + jnp.dot(mix_s[...], wout_ref[...], preferred_element_type=F32)
    x1_ref[...] = x1
    h2_s[i % 2] = _rms(x1, gffn_ref[...]).astype(BF16)

    used_rows = BLOCK * jnp.sum(blocks)
    for lo in range(SORT_MAIN, KOUT, TAIL):
        @pl.when(used_rows > lo)
        def _tail(lo=lo):
            sort_rows(lo, lo + TAIL)

        @pl.when(used_rows <= lo)
        def _empty_tail(lo=lo):
            stage_ref[lo:lo + TAIL, :] = jnp.zeros((TAIL, D_MODEL), BF16)


def _mixer_router(x2d, g_mix, w_in, w_conv, g_sgu, w_sp, b_sp, w_out, g_ffn, w_rt, b_rt, seq):
    t = x2d.shape[0]
    nt = t // TM
    full = lambda a: pl.BlockSpec(a.shape, lambda i: (0,) * a.ndim)
    ins = (g_mix, w_in, w_conv, g_sgu, w_sp, b_sp, w_out, g_ffn, w_rt, b_rt)
    prev = lambda i: (jnp.where(i == 0, nt, i - 1), 0)
    return pl.pallas_call(
        functools.partial(_mixer_router_kernel, seq // TM),
        grid=(nt + 1,),
        in_specs=[pl.BlockSpec((TM, D_MODEL), lambda i: (jnp.minimum(i, nt - 1), 0))]
        + [full(a) for a in ins],
        out_specs=[pl.BlockSpec((TM, D_MODEL), lambda i: (i, 0)),
                   pl.BlockSpec((TM, LANES), prev),
                   pl.BlockSpec((SUBLANES, LANES), prev),
                   pl.BlockSpec((KOUT, D_MODEL), prev)],
        out_shape=[jax.ShapeDtypeStruct(((nt + 1) * TM, D_MODEL), F32),
                   jax.ShapeDtypeStruct(((nt + 1) * TM, LANES), F32),
                   jax.ShapeDtypeStruct(((nt + 1) * SUBLANES, LANES), F32),
                   jax.ShapeDtypeStruct(((nt + 1) * KOUT, D_MODEL), BF16)],
        scratch_shapes=[pltpu.VMEM((HEAD_PAIRS, CHUNK, 2 * CHUNK), BF16),
                        pltpu.VMEM((TM, TM), BF16),
                        pltpu.VMEM((8, CONV_WIDTH), F32),
                        pltpu.VMEM((TM, D_MODEL), BF16),
                        pltpu.VMEM((2, TM, D_MODEL), BF16),
                        pltpu.VMEM(w_in.shape, BF16),
                        pltpu.VMEM(w_out.shape, BF16),
                        pltpu.VMEM(w_rt.shape, BF16)],
        compiler_params=pltpu.CompilerParams(dimension_semantics=("arbitrary",),
                                             vmem_limit_bytes=VMEM_LIMIT),
        name="mixer_router",
    )(x2d, *ins)


def _expert_kernel(te_ref, nu_ref, nxt_ref, src_ref, dst_ref, stage_ref, wg_hbm, wu_hbm, wd_hbm,
                   out_ref, xbuf, obuf, wg_f, wu_f, wd_f, wg_s, wu_s, wd_s, sem, osem, wsem):
    i = pl.program_id(0)
    n_used = nu_ref[0]
    slot = i % N_SLOTS

    def gather_copy(seq, j):
        tile = jnp.minimum(seq, n_used - 1)
        src = pl.multiple_of(src_ref[tile * BLOCKS_PER_ROW_TILE + j] * BLOCK, BLOCK)
        slot_ = seq % N_SLOTS
        return pltpu.make_async_copy(stage_ref.at[pl.ds(src, BLOCK)],
                                     xbuf.at[slot_, pl.ds(j * BLOCK, BLOCK)], sem.at[slot_])

    def scatter_copy(tile, j):
        dst = pl.multiple_of(dst_ref[(tile + 1) * BLOCKS_PER_ROW_TILE + j] * BLOCK, BLOCK)
        slot_ = (tile + N_SLOTS) % N_SLOTS
        return pltpu.make_async_copy(obuf.at[slot_, pl.ds(j * BLOCK, BLOCK)],
                                     out_ref.at[pl.ds(dst, BLOCK)], osem.at[slot_])

    def each_block(fn):
        for j in range(BLOCKS_PER_ROW_TILE):
            fn(j)

    def weight_copies(e):
        return (pltpu.make_async_copy(wg_hbm.at[e], wg_f, wsem.at[0]),
                pltpu.make_async_copy(wu_hbm.at[e], wu_f, wsem.at[1]),
                pltpu.make_async_copy(wd_hbm.at[e], wd_f, wsem.at[2]))

    @pl.when(i == 0)
    def _first():
        each_block(lambda j: gather_copy(0, j).start())
        each_block(lambda j: gather_copy(1, j).start())
        for cp in weight_copies(te_ref[0]):
            cp.start(priority=WEIGHT_DMA_PRIORITY)
        obuf[N_SLOTS - 1] = jnp.zeros((TR, D_MODEL), BF16)

    @pl.when((i >= 2) & (i - 3 < n_used))
    def _free_out_slot():
        each_block(lambda j: scatter_copy(i - 3, j).wait())

    @pl.when(i == n_used)
    def _after_last_tile():
        each_block(lambda j: gather_copy(n_used, j).wait())
        each_block(lambda j: gather_copy(n_used + 1, j).wait())
        each_block(lambda j: scatter_copy(n_used - 1, j).start())

    @pl.when(i < n_used)
    def _tile():
        e = te_ref[i]
        prev = te_ref[jnp.maximum(i - 1, 0)]

        @pl.when((i == 0) | (e != prev))
        def _new_expert():
            for cp in weight_copies(e):
                cp.wait()
            wg_s[...] = wg_f[...].astype(BF16)
            wu_s[...] = wu_f[...].astype(BF16)
            wd_s[...] = wd_f[...].astype(BF16)
            nxt = nxt_ref[i]

            @pl.when(nxt >= 0)
            def _next_weights():
                for cp in weight_copies(nxt):
                    cp.start(priority=WEIGHT_DMA_PRIORITY)

        each_block(lambda j: gather_copy(i, j).wait())

        hb = xbuf[slot]
        a = jnp.dot(hb, wg_s[...], preferred_element_type=F32)
        b = jnp.dot(hb, wu_s[...], preferred_element_type=F32)
        each_block(lambda j: gather_copy(i + 2, j).start())
        each_block(lambda j: scatter_copy(i - 1, j).start())
        hid = (a * jax.nn.sigmoid(a) * b).astype(BF16)
        obuf[slot] = jnp.dot(hid, wd_s[...], preferred_element_type=F32).astype(BF16)


def _experts(tile_expert, n_used, next_expert, src_block, dst_block, stage, w_gate, w_up, w_down,
             n_tiles):
    any_spec = pl.BlockSpec(memory_space=pl.ANY)
    return pl.pallas_call(
        _expert_kernel,
        grid_spec=pltpu.PrefetchScalarGridSpec(
            num_scalar_prefetch=5,
            grid=(n_tiles,),
            in_specs=[any_spec, any_spec, any_spec, any_spec],
            out_specs=any_spec,
            scratch_shapes=[pltpu.VMEM((N_SLOTS, TR, D_MODEL), BF16),
                            pltpu.VMEM((N_SLOTS, TR, D_MODEL), BF16),
                            pltpu.VMEM((D_MODEL, D_EXPERT), F32),
                            pltpu.VMEM((D_MODEL, D_EXPERT), F32),
                            pltpu.VMEM((D_EXPERT, D_MODEL), F32),
                            pltpu.VMEM((D_MODEL, D_EXPERT), BF16),
                            pltpu.VMEM((D_MODEL, D_EXPERT), BF16),
                            pltpu.VMEM((D_EXPERT, D_MODEL), BF16),
                            pltpu.SemaphoreType.DMA((N_SLOTS,)),
                            pltpu.SemaphoreType.DMA((N_SLOTS,)),
                            pltpu.SemaphoreType.DMA((3,))]),
        out_shape=jax.ShapeDtypeStruct(stage.shape, stage.dtype),
        input_output_aliases={5: 0},
        compiler_params=pltpu.CompilerParams(dimension_semantics=("arbitrary",),
                                             vmem_limit_bytes=VMEM_LIMIT),
        name="experts",
    )(tile_expert, n_used, next_expert, src_block, dst_block, stage, w_gate, w_up, w_down)


def _combine_kernel(used_ref, x1_ref, route_ref, p_ref, gple_ref, wpg_f32, wpp_f32, gfin_ref,
                    srt_ref, out_ref, y_s, wpg_ref, wpp_ref):
    i = pl.program_id(0)

    @pl.when(i == 0)
    def _init():
        wpg_ref[...] = wpg_f32[...].astype(BF16)
        wpp_ref[...] = wpp_f32[...].astype(BF16)

    route = route_ref[...]
    w1, w2, pos1, pos2 = (route[:, k:k + 1] for k in (2, 3, 4, 5))

    def unsort(lo, hi):
        col = (lax.broadcasted_iota(I32, (TM, hi - lo), 1) + lo).astype(F32)
        mat = (jnp.where(col == pos1, w1, 0.0) + jnp.where(col == pos2, w2, 0.0)).astype(BF16)
        return jnp.dot(mat, srt_ref[lo:hi, :], preferred_element_type=F32)

    y_s[...] = unsort(0, SORT_MAIN)
    for lo in range(SORT_MAIN, KOUT, TAIL):
        @pl.when(used_ref[i] > lo)
        def _tail(lo=lo):
            y_s[...] += unsort(lo, lo + TAIL)

    x2 = x1_ref[...] + y_s[...]
    hg = _rms(x2, gple_ref[...]).astype(BF16)
    gate = jax.nn.sigmoid(jnp.dot(hg, wpg_ref[...], preferred_element_type=F32))
    pe = jnp.dot(p_ref[...].astype(BF16), wpp_ref[...], preferred_element_type=F32)
    x3 = x2 + gate * pe
    out_ref[...] = _rms(x3, gfin_ref[...])


def _combine(used, x1, route, p2d, g_ple, w_pg, w_pp, g_fin, sorted_out):
    t = p2d.shape[0]
    full = lambda a: pl.BlockSpec(a.shape, lambda i, *_: (0,) * a.ndim)
    return pl.pallas_call(
        _combine_kernel,
        grid_spec=pltpu.PrefetchScalarGridSpec(
            num_scalar_prefetch=1,
            grid=(t // TM,),
            in_specs=[pl.BlockSpec((TM, D_MODEL), lambda i, *_: (i, 0)),
                      pl.BlockSpec((TM, LANES), lambda i, *_: (i, 0)),
                      pl.BlockSpec((TM, PLE_DIM), lambda i, *_: (i, 0)),
                      full(g_ple), full(w_pg), full(w_pp), full(g_fin),
                      pl.BlockSpec((KOUT, D_MODEL), lambda i, *_: (i, 0))],
            out_specs=pl.BlockSpec((TM, D_MODEL), lambda i, *_: (i, 0)),
            scratch_shapes=[pltpu.VMEM((TM, D_MODEL), F32),
                            pltpu.VMEM(w_pg.shape, BF16),
                            pltpu.VMEM(w_pp.shape, BF16)]),
        out_shape=jax.ShapeDtypeStruct((t, D_MODEL), F32),
        compiler_params=pltpu.CompilerParams(dimension_semantics=("arbitrary",),
                                             vmem_limit_bytes=VMEM_LIMIT),
        name="combine_ple",
    )(used, x1, route, p2d, g_ple, w_pg, w_pp, g_fin, sorted_out)


def _routing_tables(cnt, nt, n_tiles):
    n = cnt.reshape(-1, SUBLANES, LANES)[:nt, 0, ROUTE_LANE0:ROUTE_LANE0 + N_EXPERTS].astype(I32)
    nblk = (n + BLOCK - 1) // BLOCK
    loc = jnp.cumsum(nblk, axis=1) - nblk
    carry = jnp.cumsum(nblk, axis=0) - nblk
    total = jnp.sum(nblk, axis=0)
    tiles_e = (total + BLOCKS_PER_ROW_TILE - 1) // BLOCKS_PER_ROW_TILE
    tile_end = jnp.cumsum(tiles_e)
    seg0 = (tile_end - tiles_e) * BLOCKS_PER_ROW_TILE
    n_used = tile_end[-1:]

    tile_ids = jnp.minimum(jnp.arange(n_tiles, dtype=I32), n_used[0] - 1)
    tile_expert = jnp.sum(tile_ids[:, None] >= tile_end[None, :], axis=1).astype(I32)

    nb = n_tiles * BLOCKS_PER_ROW_TILE
    b = jnp.arange(nb, dtype=I32)
    oh_e = (tile_expert[:, None] == jnp.arange(N_EXPERTS, dtype=I32)[None, :]).astype(I32)
    per_block = lambda a: jnp.repeat(a, BLOCKS_PER_ROW_TILE, axis=0)
    pick_e = lambda tab: per_block(jnp.sum(oh_e[:, None, :] * tab[None, :, :], axis=2))
    off = b - per_block(jnp.sum(oh_e * seg0[None, :], axis=1))
    run_end = pick_e(carry + nblk)
    run_beg = pick_e(carry)
    run_loc = pick_e(loc)
    tile_of_b = jnp.sum((run_end <= off[:, None]).astype(I32), axis=1)
    valid = (tile_of_b < nt) & (b < n_used[0] * BLOCKS_PER_ROW_TILE)
    oh_t = (jnp.minimum(tile_of_b, nt - 1)[:, None] == jnp.arange(nt, dtype=I32)[None, :]).astype(I32)
    src = (jnp.sum(oh_t * (run_loc - run_beg), axis=1) + off
           + jnp.minimum(tile_of_b, nt - 1) * (KOUT // BLOCK))
    zero_block = KOUT // BLOCK - 1
    dump = nt * (KOUT // BLOCK) + b % DUMP_BLOCKS
    dst = jnp.where(valid, src, dump).astype(I32)
    dump_m1 = nt * (KOUT // BLOCK) + DUMP_BLOCKS - BLOCKS_PER_ROW_TILE + jnp.arange(
        BLOCKS_PER_ROW_TILE, dtype=I32)
    dst = jnp.concatenate([dump_m1, dst])
    src = jnp.where(valid, src, zero_block).astype(I32)

    ids = jnp.arange(N_EXPERTS, dtype=I32)
    later = (tiles_e > 0)[None, :] & (ids[None, :] > ids[:, None])
    next_used = jnp.min(jnp.where(later, ids[None, :], N_EXPERTS), axis=1)
    next_used = jnp.where(next_used == N_EXPERTS, -1, next_used)
    next_expert = jnp.sum((tile_expert[:, None] == ids[None, :]) * next_used[None, :], axis=1)
    used_rows = jnp.sum(nblk, axis=1) * BLOCK
    return tile_expert, n_used.astype(I32), next_expert.astype(I32), src, dst, used_rows.astype(I32)


def kernel(x, p, g_mix, w_in, w_conv, g_sgu, w_spatial, b_spatial, w_out, g_ffn, w_group, b_group,
           w_router, b_router, w_gate, w_up, w_down, g_ple, w_ple_gate, w_ple_proj, g_final):
    bsz, seq, d = x.shape
    t = bsz * seq
    assert w_in.shape[0] == 1, "single-layer block"
    assert d == D_MODEL and seq % TM == 0 and TM % CHUNK == 0
    nt = t // TM
    max_rows = TOP_K * t + nt * N_EXPERTS * (BLOCK - 1)
    n_tiles = (max_rows + N_EXPERTS * (TR - 1)) // TR + N_SLOTS
    l = 0

    w_rt = jnp.concatenate(
        [w_group[l], jnp.transpose(w_router[l], (1, 0, 2)).reshape(d, N_EXPERTS)], axis=1)
    w_rt = jnp.pad(w_rt, ((0, 0), (0, LANES - w_rt.shape[1])))
    b_rt = jnp.pad(jnp.concatenate([b_group[l], b_router[l].reshape(-1)]),
                   (0, LANES - N_GROUPS - N_EXPERTS)).reshape(1, LANES)
    b_sp = jnp.repeat(b_spatial[l].T, SGU_WIDTH // SGU_HEADS, axis=1)

    x1, route, cnt, stage = _mixer_router(
        x.reshape(t, d), g_mix[l].reshape(1, d), w_in[l], w_conv[l],
        g_sgu[l].reshape(1, -1), w_spatial[l], b_sp, w_out[l],
        g_ffn[l].reshape(1, d), w_rt, b_rt, seq)

    tile_expert, n_used, next_expert, src, dst, used = _routing_tables(cnt, nt, n_tiles)

    sorted_out = _experts(tile_expert, n_used, next_expert, src, dst, stage,
                          w_gate[l].reshape(N_EXPERTS, d, D_EXPERT),
                          w_up[l].reshape(N_EXPERTS, d, D_EXPERT),
                          w_down[l].reshape(N_EXPERTS, D_EXPERT, d), n_tiles)
    out = _combine(used, x1, route, p[l].reshape(t, PLE_DIM), g_ple[l].reshape(1, d),
                   w_ple_gate[l], w_ple_proj[l], g_final.reshape(1, d), sorted_out)
    return out.reshape(bsz, seq, d)
```

```python
import functools

import jax
import jax.numpy as jnp
from jax import lax
from jax.experimental import pallas as pl
from jax.experimental.pallas import tpu as pltpu

F32 = jnp.float32
BF16 = jnp.bfloat16
I32 = jnp.int32

EPS = 1e-6
D_MODEL = 1024
CONV_WIDTH = 512
SGU_WIDTH = 512
SGU_HEADS = 8
HEAD_PAIRS = SGU_HEADS // 2
CHUNK = 128
N_GROUPS = 4
EXPERTS_PER_GROUP = 8
N_EXPERTS = N_GROUPS * EXPERTS_PER_GROUP
TOP_K = 2
D_EXPERT = 512
PLE_DIM = 256
LANES = 128
SUBLANES = 8
BLOCK = 2 * SUBLANES
ROUTE_LANE0 = N_GROUPS

TM = 512
TR = 512
BLOCKS_PER_ROW_TILE = TR // BLOCK
KOUT = TOP_K * TM + N_EXPERTS * BLOCK
TAIL = 128
SORT_MAIN = KOUT - 2 * TAIL
ROW_STEP = 128
N_SLOTS = 3
DUMP_BLOCKS = N_SLOTS * BLOCKS_PER_ROW_TILE
assert DUMP_BLOCKS * BLOCK <= KOUT
WEIGHT_DMA_PRIORITY = 1
POS_SPLIT = 32
VMEM_LIMIT = 56 * 1024 * 1024


def _rms(x, g):
    return x * lax.rsqrt(jnp.mean(x * x, axis=-1, keepdims=True) + EPS) * g


def _mixer_router_kernel(tiles_per_seq,
                         x_ref, gmix_ref, win_f32, wconv_ref, gsgu_ref, wsp_ref, bsp_ref,
                         wout_f32, gffn_ref, wrt_f32, brt_ref,
                         x1_ref, route_ref, cnt_ref, stage_ref,
                         wcat_s, tri_s, halo_s, mix_s, h2_s, win_ref, wout_ref, wrt_ref):
    i = pl.program_id(0)
    nch = TM // CHUNK

    @pl.when(i == 0)
    def _init():
        r = lax.broadcasted_iota(I32, (CHUNK, CHUNK), 0)
        c = lax.broadcasted_iota(I32, (CHUNK, CHUNK), 1)
        causal = c <= r
        for j in range(HEAD_PAIRS):
            wa = jnp.where(causal, wsp_ref[2 * j], 0.0)
            wb = jnp.where(causal, wsp_ref[2 * j + 1], 0.0)
            wcat_s[j] = jnp.concatenate([wa, wb], axis=1).astype(BF16)
        rr = lax.broadcasted_iota(I32, (TM, TM), 0)
        cc = lax.broadcasted_iota(I32, (TM, TM), 1)
        tri_s[...] = (cc < rr).astype(BF16)
        h2_s[...] = jnp.zeros_like(h2_s)
        win_ref[...] = win_f32[...].astype(BF16)
        wout_ref[...] = wout_f32[...].astype(BF16)
        wrt_ref[...] = wrt_f32[...].astype(BF16)

    @pl.when(i % tiles_per_seq == 0)
    def _seq_start():
        halo_s[...] = jnp.zeros_like(halo_s)

    x = x_ref[...]
    hb = _rms(x, gmix_ref[...]).astype(BF16)
    h2b = h2_s[(i + 1) % 2]

    def proj(k):
        return jnp.dot(hb, win_ref[:, k * 512:(k + 1) * 512], preferred_element_type=F32)

    logits = jnp.dot(h2b, wrt_ref[...], preferred_element_type=F32) + brt_ref[...]
    zc = proj(1) * proj(2)

    lane = lax.broadcasted_iota(I32, (TM, LANES), 1).astype(F32)
    neg = jnp.float32(-jnp.inf)
    big = jnp.float32(1e9)
    is_g = lane < N_GROUPS
    gl = jnp.where(is_g, logits, neg)
    gmax = jnp.max(gl, axis=1, keepdims=True)
    gsum = jnp.sum(jnp.where(is_g, jnp.exp(gl - gmax), 0.0), axis=1, keepdims=True)
    g_w = 1.0 / gsum
    g_idx = jnp.min(jnp.where(gl == gmax, lane, big), axis=1, keepdims=True)
    lo_lane = ROUTE_LANE0 + EXPERTS_PER_GROUP * g_idx
    in_grp = (lane >= lo_lane) & (lane < lo_lane + EXPERTS_PER_GROUP)
    el = jnp.where(in_grp, logits, neg)
    v1 = jnp.max(el, axis=1, keepdims=True)
    i1 = jnp.min(jnp.where(el == v1, lane, big), axis=1, keepdims=True)
    el2 = jnp.where(lane == i1, neg, el)
    v2 = jnp.max(el2, axis=1, keepdims=True)
    i2 = jnp.min(jnp.where(el2 == v2, lane, big), axis=1, keepdims=True)
    e21 = jnp.exp(v2 - v1)
    w1 = g_w / (1.0 + e21)
    w2 = g_w * e21 / (1.0 + e21)

    row = lax.broadcasted_iota(I32, (TM, CONV_WIDTH), 0)
    h6 = halo_s[6:7, :]
    h7 = halo_s[7:8, :]
    z1 = jnp.where(row == 0, h7, pltpu.roll(zc, 1, 0))
    z2 = jnp.where(row == 0, h6, jnp.where(row == 1, h7, pltpu.roll(zc, 2, 0)))
    conv = z2 * wconv_ref[0:1, :] + z1 * wconv_ref[1:2, :] + zc * wconv_ref[2:3, :]
    halo_s[...] = zc[TM - 8:TM, :]
    mix_s[:, 0:CONV_WIDTH] = (proj(0) * conv).astype(BF16)

    sel1 = lane == i1
    sel2 = lane == i2
    onehot = jnp.where(sel1 | sel2, 1.0, 0.0)
    counts = jnp.sum(onehot, axis=0, keepdims=True)
    blocks = jnp.ceil(counts * (1.0 / BLOCK))
    er = lax.broadcasted_iota(I32, (LANES, LANES), 0)
    ec = lax.broadcasted_iota(I32, (LANES, LANES), 1)
    before = (er < ec).astype(BF16)
    run_start = BLOCK * jnp.dot(jnp.broadcast_to(blocks, (SUBLANES, LANES)).astype(BF16), before,
                                   preferred_element_type=F32)[0:1, :]
    rank = jnp.dot(tri_s[...], onehot.astype(BF16), preferred_element_type=F32) + run_start
    pos1 = jnp.sum(jnp.where(sel1, rank, 0.0), axis=1, keepdims=True)
    pos2 = jnp.sum(jnp.where(sel2, rank, 0.0), axis=1, keepdims=True)

    route = jnp.where(lane == 0, i1 - ROUTE_LANE0, 0.0)
    route = jnp.where(lane == 1, i2 - ROUTE_LANE0, route)
    route = jnp.where(lane == 2, w1, route)
    route = jnp.where(lane == 3, w2, route)
    route = jnp.where(lane == 4, pos1, route)
    route = jnp.where(lane == 5, pos2, route)
    route_ref[...] = route
    cnt_ref[...] = jnp.broadcast_to(counts, cnt_ref.shape)

    p1h = jnp.floor(pos1 * (1.0 / POS_SPLIT))
    p2h = jnp.floor(pos2 * (1.0 / POS_SPLIT))
    pv = jnp.where(lane == 0, p1h, 0.0)
    pv = jnp.where(lane == 1, pos1 - POS_SPLIT * p1h, pv)
    pv = jnp.where(lane == 2, p2h, pv)
    pv = jnp.where(lane == 3, pos2 - POS_SPLIT * p2h, pv)
    pick = (lax.broadcasted_iota(I32, (SUBLANES, LANES), 0)
            == lax.broadcasted_iota(I32, (SUBLANES, LANES), 1)).astype(BF16)
    prow = lax.dot_general(pick, pv.astype(BF16), (((1,), (1,)), ((), ())),
                           preferred_element_type=F32)
    pos1_row = POS_SPLIT * prow[0:1, :] + prow[1:2, :]
    pos2_row = POS_SPLIT * prow[2:3, :] + prow[3:4, :]

    def sort_rows(lo, hi):
        out_row = (lax.broadcasted_iota(I32, (hi - lo, TM), 0) + lo).astype(F32)
        sort_mat = jnp.where((out_row == pos1_row) | (out_row == pos2_row), 1.0, 0.0).astype(BF16)
        stage_ref[lo:hi, :] = jnp.dot(sort_mat, h2b, preferred_element_type=F32).astype(BF16)

    q = SORT_MAIN // 4
    sort_rows(0, q)
    u = jax.nn.gelu(proj(3))
    sort_rows(q, 2 * q)
    v = jax.nn.gelu(proj(4))
    vc = v - jnp.mean(v, axis=-1, keepdims=True)
    vn = vc * lax.rsqrt(jnp.mean(vc * vc, axis=-1, keepdims=True) + EPS) * gsgu_ref[...]
    vnb = vn.astype(BF16)
    sort_rows(2 * q, 3 * q)
    left = lax.broadcasted_iota(I32, (CHUNK, LANES), 1) < (LANES // 2)
    zero = jnp.zeros((CHUNK, LANES), BF16)
    for j in range(HEAD_PAIRS):
        cols = []
        for c in range(nch):
            s = vnb[c * CHUNK:(c + 1) * CHUNK, j * LANES:(j + 1) * LANES]
            cols.append(jnp.concatenate([jnp.where(left, s, zero), jnp.where(left, zero, s)], axis=0))
        rhs = jnp.concatenate(cols, axis=1)
        res = jnp.dot(wcat_s[j], rhs, preferred_element_type=F32)
        bias = bsp_ref[:, j * LANES:(j + 1) * LANES]
        for c in range(nch):
            mixed = res[:, c * LANES:(c + 1) * LANES] + bias
            uu = u[c * CHUNK:(c + 1) * CHUNK, j * LANES:(j + 1) * LANES]
            mix_s[c * CHUNK:(c + 1) * CHUNK,
                  CONV_WIDTH + j * LANES:CONV_WIDTH + (j + 1) * LANES] = (uu * mixed).astype(BF16)

    sort_rows(3 * q, SORT_MAIN)
    x1 = x + jnp.dot(mix_s[...], wout_ref[...], preferred_element_type=F32)
    x1_ref[...] = x1
    h2_s[i % 2] = _rms(x1, gffn_ref[...]).astype(BF16)

    used_rows = BLOCK * jnp.sum(blocks)
    for lo in range(SORT_MAIN, KOUT, TAIL):
        @pl.when(used_rows > lo)
        def _tail(lo=lo):
            sort_rows(lo, lo + TAIL)

        @pl.when(used_rows <= lo)
        def _empty_tail(lo=lo):
            stage_ref[lo:lo + TAIL, :] = jnp.zeros((TAIL, D_MODEL), BF16)


def _mixer_router(x2d, g_mix, w_in, w_conv, g_sgu, w_sp, b_sp, w_out, g_ffn, w_rt, b_rt, seq):
    t = x2d.shape[0]
    nt = t // TM
    full = lambda a: pl.BlockSpec(a.shape, lambda i: (0,) * a.ndim)
    ins = (g_mix, w_in, w_conv, g_sgu, w_sp, b_sp, w_out, g_ffn, w_rt, b_rt)
    prev = lambda i: (jnp.where(i == 0, nt, i - 1), 0)
    return pl.pallas_call(
        functools.partial(_mixer_router_kernel, seq // TM),
        grid=(nt + 1,),
        in_specs=[pl.BlockSpec((TM, D_MODEL), lambda i: (jnp.minimum(i, nt - 1), 0))]
        + [full(a) for a in ins],
        out_specs=[pl.BlockSpec((TM, D_MODEL), lambda i: (i, 0)),
                   pl.BlockSpec((TM, LANES), prev),
                   pl.BlockSpec((SUBLANES, LANES), prev),
                   pl.BlockSpec((KOUT, D_MODEL), prev)],
        out_shape=[jax.ShapeDtypeStruct(((nt + 1) * TM, D_MODEL), F32),
                   jax.ShapeDtypeStruct(((nt + 1) * TM, LANES), F32),
                   jax.ShapeDtypeStruct(((nt + 1) * SUBLANES, LANES), F32),
                   jax.ShapeDtypeStruct(((nt + 1) * KOUT, D_MODEL), BF16)],
        scratch_shapes=[pltpu.VMEM((HEAD_PAIRS, CHUNK, 2 * CHUNK), BF16),
                        pltpu.VMEM((TM, TM), BF16),
                        pltpu.VMEM((8, CONV_WIDTH), F32),
                        pltpu.VMEM((TM, D_MODEL), BF16),
                        pltpu.VMEM((2, TM, D_MODEL), BF16),
                        pltpu.VMEM(w_in.shape, BF16),
                        pltpu.VMEM(w_out.shape, BF16),
                        pltpu.VMEM(w_rt.shape, BF16)],
        compiler_params=pltpu.CompilerParams(dimension_semantics=("arbitrary",),
                                             vmem_limit_bytes=VMEM_LIMIT),
        name="mixer_router",
    )(x2d, *ins)


def _expert_kernel(te_ref, nu_ref, nxt_ref, rows_ref, src_ref, dst_ref, stage_ref,
                   wg_hbm, wu_hbm, wd_hbm, out_ref, xbuf, obuf, wg_f, wu_f, wd_f, wg_s, wu_s, wd_s, sem, osem, wsem):
    i = pl.program_id(0)
    n_used = nu_ref[0]
    slot = i % N_SLOTS

    def gather_copy(seq, j):
        tile = jnp.minimum(seq, n_used - 1)
        src = pl.multiple_of(src_ref[tile * BLOCKS_PER_ROW_TILE + j] * BLOCK, BLOCK)
        slot_ = seq % N_SLOTS
        return pltpu.make_async_copy(stage_ref.at[pl.ds(src, BLOCK)],
                                     xbuf.at[slot_, pl.ds(j * BLOCK, BLOCK)], sem.at[slot_])

    def scatter_copy(tile, j):
        dst = pl.multiple_of(dst_ref[(tile + 1) * BLOCKS_PER_ROW_TILE + j] * BLOCK, BLOCK)
        slot_ = (tile + N_SLOTS) % N_SLOTS
        return pltpu.make_async_copy(obuf.at[slot_, pl.ds(j * BLOCK, BLOCK)],
                                     out_ref.at[pl.ds(dst, BLOCK)], osem.at[slot_])

    def each_block(fn):
        for j in range(BLOCKS_PER_ROW_TILE):
            fn(j)

    def weight_copies(e):
        return (pltpu.make_async_copy(wg_hbm.at[e], wg_f, wsem.at[0]),
                pltpu.make_async_copy(wu_hbm.at[e], wu_f, wsem.at[1]),
                pltpu.make_async_copy(wd_hbm.at[e], wd_f, wsem.at[2]))

    @pl.when(i == 0)
    def _first():
        each_block(lambda j: gather_copy(0, j).start())
        each_block(lambda j: gather_copy(1, j).start())
        for cp in weight_copies(te_ref[0]):
            cp.start(priority=WEIGHT_DMA_PRIORITY)
        obuf[...] = jnp.zeros_like(obuf)

    @pl.when((i >= 2) & (i - 3 < n_used))
    def _free_out_slot():
        each_block(lambda j: scatter_copy(i - 3, j).wait())

    @pl.when(i == n_used)
    def _after_last_tile():
        each_block(lambda j: gather_copy(n_used, j).wait())
        each_block(lambda j: gather_copy(n_used + 1, j).wait())
        each_block(lambda j: scatter_copy(n_used - 1, j).start())

    @pl.when(i < n_used)
    def _tile():
        e = te_ref[i]
        prev = te_ref[jnp.maximum(i - 1, 0)]

        @pl.when((i == 0) | (e != prev))
        def _new_expert():
            for cp in weight_copies(e):
                cp.wait()
            wg_s[...] = wg_f[...].astype(BF16)
            wu_s[...] = wu_f[...].astype(BF16)
            wd_s[...] = wd_f[...].astype(BF16)
            nxt = nxt_ref[i]

            @pl.when(nxt >= 0)
            def _next_weights():
                for cp in weight_copies(nxt):
                    cp.start(priority=WEIGHT_DMA_PRIORITY)

        each_block(lambda j: gather_copy(i, j).wait())

        def mlp(rows):
            hb = xbuf[slot, 0:rows, :]
            a = jnp.dot(hb, wg_s[...], preferred_element_type=F32)
            b = jnp.dot(hb, wu_s[...], preferred_element_type=F32)
            each_block(lambda j: gather_copy(i + 2, j).start())
            each_block(lambda j: scatter_copy(i - 1, j).start())
            hid = (a * jax.nn.sigmoid(a) * b).astype(BF16)
            obuf[slot, 0:rows, :] = jnp.dot(hid, wd_s[...], preferred_element_type=F32).astype(BF16)

        rows_needed = rows_ref[i]
        for rows in range(ROW_STEP, TR + 1, ROW_STEP):
            pl.when(rows_needed == rows)(functools.partial(mlp, rows))


def _experts(tile_expert, n_used, next_expert, tile_rows, src_block, dst_block, stage,
             w_gate, w_up, w_down, n_tiles):
    any_spec = pl.BlockSpec(memory_space=pl.ANY)
    return pl.pallas_call(
        _expert_kernel,
        grid_spec=pltpu.PrefetchScalarGridSpec(
            num_scalar_prefetch=6,
            grid=(n_tiles,),
            in_specs=[any_spec, any_spec, any_spec, any_spec],
            out_specs=any_spec,
            scratch_shapes=[pltpu.VMEM((N_SLOTS, TR, D_MODEL), BF16),
                            pltpu.VMEM((N_SLOTS, TR, D_MODEL), BF16),
                            pltpu.VMEM((D_MODEL, D_EXPERT), F32),
                            pltpu.VMEM((D_MODEL, D_EXPERT), F32),
                            pltpu.VMEM((D_EXPERT, D_MODEL), F32),
                            pltpu.VMEM((D_MODEL, D_EXPERT), BF16),
                            pltpu.VMEM((D_MODEL, D_EXPERT), BF16),
                            pltpu.VMEM((D_EXPERT, D_MODEL), BF16),
                            pltpu.SemaphoreType.DMA((N_SLOTS,)),
                            pltpu.SemaphoreType.DMA((N_SLOTS,)),
                            pltpu.SemaphoreType.DMA((3,))]),
        out_shape=jax.ShapeDtypeStruct(stage.shape, stage.dtype),
        input_output_aliases={6: 0},
        compiler_params=pltpu.CompilerParams(dimension_semantics=("arbitrary",),
                                             vmem_limit_bytes=VMEM_LIMIT),
        name="experts",
    )(tile_expert, n_used, next_expert, tile_rows, src_block, dst_block, stage, w_gate, w_up, w_down)


def _combine_kernel(used_ref, x1_ref, route_ref, p_ref, gple_ref, wpg_f32, wpp_f32, gfin_ref,
                    srt_ref, out_ref, y_s, wpg_ref, wpp_ref):
    i = pl.program_id(0)

    @pl.when(i == 0)
    def _init():
        wpg_ref[...] = wpg_f32[...].astype(BF16)
        wpp_ref[...] = wpp_f32[...].astype(BF16)

    route = route_ref[...]
    w1, w2, pos1, pos2 = (route[:, k:k + 1] for k in (2, 3, 4, 5))

    def unsort(lo, hi):
        col = (lax.broadcasted_iota(I32, (TM, hi - lo), 1) + lo).astype(F32)
        mat = (jnp.where(col == pos1, w1, 0.0) + jnp.where(col == pos2, w2, 0.0)).astype(BF16)
        return jnp.dot(mat, srt_ref[lo:hi, :], preferred_element_type=F32)

    y_s[...] = unsort(0, SORT_MAIN)
    for lo in range(SORT_MAIN, KOUT, TAIL):
        @pl.when(used_ref[i] > lo)
        def _tail(lo=lo):
            y_s[...] += unsort(lo, lo + TAIL)

    x2 = x1_ref[...] + y_s[...]
    hg = _rms(x2, gple_ref[...]).astype(BF16)
    gate = jax.nn.sigmoid(jnp.dot(hg, wpg_ref[...], preferred_element_type=F32))
    pe = jnp.dot(p_ref[...].astype(BF16), wpp_ref[...], preferred_element_type=F32)
    x3 = x2 + gate * pe
    out_ref[...] = _rms(x3, gfin_ref[...])


def _combine(used, x1, route, p2d, g_ple, w_pg, w_pp, g_fin, sorted_out):
    t = p2d.shape[0]
    full = lambda a: pl.BlockSpec(a.shape, lambda i, *_: (0,) * a.ndim)
    return pl.pallas_call(
        _combine_kernel,
        grid_spec=pltpu.PrefetchScalarGridSpec(
            num_scalar_prefetch=1,
            grid=(t // TM,),
            in_specs=[pl.BlockSpec((TM, D_MODEL), lambda i, *_: (i, 0)),
                      pl.BlockSpec((TM, LANES), lambda i, *_: (i, 0)),
                      pl.BlockSpec((TM, PLE_DIM), lambda i, *_: (i, 0)),
                      full(g_ple), full(w_pg), full(w_pp), full(g_fin),
                      pl.BlockSpec((KOUT, D_MODEL), lambda i, *_: (i, 0))],
            out_specs=pl.BlockSpec((TM, D_MODEL), lambda i, *_: (i, 0)),
            scratch_shapes=[pltpu.VMEM((TM, D_MODEL), F32),
                            pltpu.VMEM(w_pg.shape, BF16),
                            pltpu.VMEM(w_pp.shape, BF16)]),
        out_shape=jax.ShapeDtypeStruct((t, D_MODEL), F32),
        compiler_params=pltpu.CompilerParams(dimension_semantics=("arbitrary",),
                                             vmem_limit_bytes=VMEM_LIMIT),
        name="combine_ple",
    )(used, x1, route, p2d, g_ple, w_pg, w_pp, g_fin, sorted_out)


def _routing_tables(cnt, nt, n_tiles):
    n = cnt.reshape(-1, SUBLANES, LANES)[:nt, 0, ROUTE_LANE0:ROUTE_LANE0 + N_EXPERTS].astype(I32)
    nblk = (n + BLOCK - 1) // BLOCK
    loc = jnp.cumsum(nblk, axis=1) - nblk
    carry = jnp.cumsum(nblk, axis=0) - nblk
    total = jnp.sum(nblk, axis=0)
    tiles_e = (total + BLOCKS_PER_ROW_TILE - 1) // BLOCKS_PER_ROW_TILE
    tile_end = jnp.cumsum(tiles_e)
    seg0 = (tile_end - tiles_e) * BLOCKS_PER_ROW_TILE
    n_used = tile_end[-1:]

    tile_ids = jnp.minimum(jnp.arange(n_tiles, dtype=I32), n_used[0] - 1)
    tile_expert = jnp.sum(tile_ids[:, None] >= tile_end[None, :], axis=1).astype(I32)

    nb = n_tiles * BLOCKS_PER_ROW_TILE
    b = jnp.arange(nb, dtype=I32)
    oh_e = (tile_expert[:, None] == jnp.arange(N_EXPERTS, dtype=I32)[None, :]).astype(I32)
    per_block = lambda a: jnp.repeat(a, BLOCKS_PER_ROW_TILE, axis=0)
    pick_e = lambda tab: per_block(jnp.sum(oh_e[:, None, :] * tab[None, :, :], axis=2))
    off = b - per_block(jnp.sum(oh_e * seg0[None, :], axis=1))
    run_end = pick_e(carry + nblk)
    run_beg = pick_e(carry)
    run_loc = pick_e(loc)
    tile_of_b = jnp.sum((run_end <= off[:, None]).astype(I32), axis=1)
    valid = (tile_of_b < nt) & (b < n_used[0] * BLOCKS_PER_ROW_TILE)
    oh_t = (jnp.minimum(tile_of_b, nt - 1)[:, None] == jnp.arange(nt, dtype=I32)[None, :]).astype(I32)
    src = (jnp.sum(oh_t * (run_loc - run_beg), axis=1) + off
           + jnp.minimum(tile_of_b, nt - 1) * (KOUT // BLOCK))
    zero_block = KOUT // BLOCK - 1
    dump = nt * (KOUT // BLOCK) + b % DUMP_BLOCKS
    dst = jnp.where(valid, src, dump).astype(I32)
    dump_m1 = nt * (KOUT // BLOCK) + DUMP_BLOCKS - BLOCKS_PER_ROW_TILE + jnp.arange(
        BLOCKS_PER_ROW_TILE, dtype=I32)
    dst = jnp.concatenate([dump_m1, dst])
    src = jnp.where(valid, src, zero_block).astype(I32)

    ids = jnp.arange(N_EXPERTS, dtype=I32)
    later = (tiles_e > 0)[None, :] & (ids[None, :] > ids[:, None])
    next_used = jnp.min(jnp.where(later, ids[None, :], N_EXPERTS), axis=1)
    next_used = jnp.where(next_used == N_EXPERTS, -1, next_used)
    next_expert = jnp.sum((tile_expert[:, None] == ids[None, :]) * next_used[None, :], axis=1)
    used_rows = jnp.sum(nblk, axis=1) * BLOCK
    tiles = jnp.arange(n_tiles, dtype=I32)
    seg_rows = jnp.sum(oh_e * (total * BLOCK)[None, :], axis=1)
    seg_tile0 = jnp.sum(oh_e * (tile_end - tiles_e)[None, :], axis=1)
    tile_rows = jnp.clip(seg_rows - (tiles - seg_tile0) * TR, ROW_STEP, TR)
    tile_rows = (tile_rows + ROW_STEP - 1) // ROW_STEP * ROW_STEP
    return (tile_expert, n_used.astype(I32), next_expert.astype(I32), tile_rows.astype(I32), src, dst,
            used_rows.astype(I32))


def kernel(x, p, g_mix, w_in, w_conv, g_sgu, w_spatial, b_spatial, w_out, g_ffn, w_group, b_group,
           w_router, b_router, w_gate, w_up, w_down, g_ple, w_ple_gate, w_ple_proj, g_final):
    bsz, seq, d = x.shape
    t = bsz * seq
    assert w_in.shape[0] == 1, "single-layer block"
    assert d == D_MODEL and seq % TM == 0 and TM % CHUNK == 0
    nt = t // TM
    max_rows = TOP_K * t + nt * N_EXPERTS * (BLOCK - 1)
    n_tiles = (max_rows + N_EXPERTS * (TR - 1)) // TR + N_SLOTS
    l = 0

    w_rt = jnp.concatenate(
        [w_group[l], jnp.transpose(w_router[l], (1, 0, 2)).reshape(d, N_EXPERTS)], axis=1)
    w_rt = jnp.pad(w_rt, ((0, 0), (0, LANES - w_rt.shape[1])))
    b_rt = jnp.pad(jnp.concatenate([b_group[l], b_router[l].reshape(-1)]),
                   (0, LANES - N_GROUPS - N_EXPERTS)).reshape(1, LANES)
    b_sp = jnp.repeat(b_spatial[l].T, SGU_WIDTH // SGU_HEADS, axis=1)

    x1, route, cnt, stage = _mixer_router(
        x.reshape(t, d), g_mix[l].reshape(1, d), w_in[l], w_conv[l],
        g_sgu[l].reshape(1, -1), w_spatial[l], b_sp, w_out[l],
        g_ffn[l].reshape(1, d), w_rt, b_rt, seq)

    tile_expert, n_used, next_expert, tile_rows, src, dst, used = _routing_tables(cnt, nt, n_tiles)

    sorted_out = _experts(tile_expert, n_used, next_expert, tile_rows, src, dst, stage,
                          w_gate[l].reshape(N_EXPERTS, d, D_EXPERT),
                          w_up[l].reshape(N_EXPERTS, d, D_EXPERT),
                          w_down[l].reshape(N_EXPERTS, D_EXPERT, d), n_tiles)
    out = _combine(used, x1, route, p[l].reshape(t, PLE_DIM), g_ple[l].reshape(1, d),
                   w_ple_gate[l], w_ple_proj[l], g_final.reshape(1, d), sorted_out)
    return out.reshape(bsz, seq, d)
```

```python
import functools

import jax
import jax.numpy as jnp
from jax import lax
from jax.experimental import pallas as pl
from jax.experimental.pallas import tpu as pltpu

F32 = jnp.float32
BF16 = jnp.bfloat16
I32 = jnp.int32

EPS = 1e-6
D_MODEL = 1024
CONV_WIDTH = 512
SGU_WIDTH = 512
SGU_HEADS = 8
HEAD_PAIRS = SGU_HEADS // 2
CHUNK = 128
N_GROUPS = 4
EXPERTS_PER_GROUP = 8
N_EXPERTS = N_GROUPS * EXPERTS_PER_GROUP
TOP_K = 2
D_EXPERT = 512
PLE_DIM = 256
LANES = 128
SUBLANES = 8
BLOCK = 2 * SUBLANES
ROUTE_LANE0 = N_GROUPS

TM = 512
TR = 512
BLOCKS_PER_ROW_TILE = TR // BLOCK
KOUT = TOP_K * TM + N_EXPERTS * BLOCK
TAIL = 128
SORT_MAIN = KOUT - 2 * TAIL
ROW_STEP = 128
N_SLOTS = 3
DUMP_BLOCKS = N_SLOTS * BLOCKS_PER_ROW_TILE
assert DUMP_BLOCKS * BLOCK <= KOUT
WEIGHT_DMA_PRIORITY = 1
POS_SPLIT = 32
VMEM_LIMIT = 56 * 1024 * 1024


def _rms(x, g):
    return x * lax.rsqrt(jnp.mean(x * x, axis=-1, keepdims=True) + EPS) * g


def _mixer_router_kernel(tiles_per_seq,
                         x_ref, gmix_ref, win_f32, wconv_ref, gsgu_ref, wsp_ref, bsp_ref,
                         wout_f32, gffn_ref, wrt_f32, brt_ref,
                         x1_ref, route_ref, cnt_ref, stage_ref,
                         wcat_s, tri_s, halo_s, mix_s, h2_s, win_ref, wout_ref, wrt_ref):
    i = pl.program_id(0)
    nch = TM // CHUNK

    @pl.when(i == 0)
    def _init():
        r = lax.broadcasted_iota(I32, (CHUNK, CHUNK), 0)
        c = lax.broadcasted_iota(I32, (CHUNK, CHUNK), 1)
        causal = c <= r
        for j in range(HEAD_PAIRS):
            wa = jnp.where(causal, wsp_ref[2 * j], 0.0)
            wb = jnp.where(causal, wsp_ref[2 * j + 1], 0.0)
            wcat_s[j] = jnp.concatenate([wa, wb], axis=1).astype(BF16)
        rr = lax.broadcasted_iota(I32, (TM, TM), 0)
        cc = lax.broadcasted_iota(I32, (TM, TM), 1)
        tri_s[...] = (cc < rr).astype(BF16)
        h2_s[...] = jnp.zeros_like(h2_s)
        win_ref[...] = win_f32[...].astype(BF16)
        wout_ref[...] = wout_f32[...].astype(BF16)
        wrt_ref[...] = wrt_f32[...].astype(BF16)

    @pl.when(i % tiles_per_seq == 0)
    def _seq_start():
        halo_s[...] = jnp.zeros_like(halo_s)

    x = x_ref[...]
    hb = _rms(x, gmix_ref[...]).astype(BF16)
    h2b = h2_s[(i + 1) % 2]

    def proj(k):
        return jnp.dot(hb, win_ref[:, k * 512:(k + 1) * 512], preferred_element_type=F32)

    logits = jnp.dot(h2b, wrt_ref[...], preferred_element_type=F32) + brt_ref[...]
    zc = proj(1) * proj(2)

    lane = lax.broadcasted_iota(I32, (TM, LANES), 1).astype(F32)
    neg = jnp.float32(-jnp.inf)
    big = jnp.float32(1e9)
    is_g = lane < N_GROUPS
    gl = jnp.where(is_g, logits, neg)
    gmax = jnp.max(gl, axis=1, keepdims=True)
    gsum = jnp.sum(jnp.where(is_g, jnp.exp(gl - gmax), 0.0), axis=1, keepdims=True)
    g_w = 1.0 / gsum
    g_idx = jnp.min(jnp.where(gl == gmax, lane, big), axis=1, keepdims=True)
    lo_lane = ROUTE_LANE0 + EXPERTS_PER_GROUP * g_idx
    in_grp = (lane >= lo_lane) & (lane < lo_lane + EXPERTS_PER_GROUP)
    el = jnp.where(in_grp, logits, neg)
    v1 = jnp.max(el, axis=1, keepdims=True)
    i1 = jnp.min(jnp.where(el == v1, lane, big), axis=1, keepdims=True)
    el2 = jnp.where(lane == i1, neg, el)
    v2 = jnp.max(el2, axis=1, keepdims=True)
    i2 = jnp.min(jnp.where(el2 == v2, lane, big), axis=1, keepdims=True)
    e21 = jnp.exp(v2 - v1)
    w1 = g_w / (1.0 + e21)
    w2 = g_w * e21 / (1.0 + e21)

    row = lax.broadcasted_iota(I32, (TM, CONV_WIDTH), 0)
    h6 = halo_s[6:7, :]
    h7 = halo_s[7:8, :]
    z1 = jnp.where(row == 0, h7, pltpu.roll(zc, 1, 0))
    z2 = jnp.where(row == 0, h6, jnp.where(row == 1, h7, pltpu.roll(zc, 2, 0)))
    conv = z2 * wconv_ref[0:1, :] + z1 * wconv_ref[1:2, :] + zc * wconv_ref[2:3, :]
    halo_s[...] = zc[TM - 8:TM, :]
    mix_s[:, 0:CONV_WIDTH] = (proj(0) * conv).astype(BF16)

    sel1 = lane == i1
    sel2 = lane == i2
    onehot = jnp.where(sel1 | sel2, 1.0, 0.0)
    counts = jnp.sum(onehot, axis=0, keepdims=True)
    blocks = jnp.ceil(counts * (1.0 / BLOCK))
    er = lax.broadcasted_iota(I32, (LANES, LANES), 0)
    ec = lax.broadcasted_iota(I32, (LANES, LANES), 1)
    before = (er < ec).astype(BF16)
    run_start = BLOCK * jnp.dot(jnp.broadcast_to(blocks, (SUBLANES, LANES)).astype(BF16), before,
                                   preferred_element_type=F32)[0:1, :]
    rank = jnp.dot(tri_s[...], onehot.astype(BF16), preferred_element_type=F32) + run_start
    pos1 = jnp.sum(jnp.where(sel1, rank, 0.0), axis=1, keepdims=True)
    pos2 = jnp.sum(jnp.where(sel2, rank, 0.0), axis=1, keepdims=True)

    route = jnp.where(lane == 0, i1 - ROUTE_LANE0, 0.0)
    route = jnp.where(lane == 1, i2 - ROUTE_LANE0, route)
    route = jnp.where(lane == 2, w1, route)
    route = jnp.where(lane == 3, w2, route)
    route = jnp.where(lane == 4, pos1, route)
    route = jnp.where(lane == 5, pos2, route)
    route_ref[...] = route
    cnt_ref[...] = jnp.broadcast_to(counts, cnt_ref.shape)

    p1h = jnp.floor(pos1 * (1.0 / POS_SPLIT))
    p2h = jnp.floor(pos2 * (1.0 / POS_SPLIT))
    pv = jnp.where(lane == 0, p1h, 0.0)
    pv = jnp.where(lane == 1, pos1 - POS_SPLIT * p1h, pv)
    pv = jnp.where(lane == 2, p2h, pv)
    pv = jnp.where(lane == 3, pos2 - POS_SPLIT * p2h, pv)
    pick = (lax.broadcasted_iota(I32, (SUBLANES, LANES), 0)
            == lax.broadcasted_iota(I32, (SUBLANES, LANES), 1)).astype(BF16)
    prow = lax.dot_general(pick, pv.astype(BF16), (((1,), (1,)), ((), ())),
                           preferred_element_type=F32)
    pos1_row = POS_SPLIT * prow[0:1, :] + prow[1:2, :]
    pos2_row = POS_SPLIT * prow[2:3, :] + prow[3:4, :]

    def sort_rows(lo, hi):
        out_row = (lax.broadcasted_iota(I32, (hi - lo, TM), 0) + lo).astype(F32)
        sort_mat = jnp.where((out_row == pos1_row) | (out_row == pos2_row), 1.0, 0.0).astype(BF16)
        stage_ref[lo:hi, :] = jnp.dot(sort_mat, h2b, preferred_element_type=F32).astype(BF16)

    u = jax.nn.gelu(proj(3))
    sort_rows(0, SORT_MAIN // 2)
    v = jax.nn.gelu(proj(4))
    vc = v - jnp.mean(v, axis=-1, keepdims=True)
    vn = vc * lax.rsqrt(jnp.mean(vc * vc, axis=-1, keepdims=True) + EPS) * gsgu_ref[...]
    vnb = vn.astype(BF16)
    left = lax.broadcasted_iota(I32, (CHUNK, LANES), 1) < (LANES // 2)
    zero = jnp.zeros((CHUNK, LANES), BF16)
    for j in range(HEAD_PAIRS):
        cols = []
        for c in range(nch):
            s = vnb[c * CHUNK:(c + 1) * CHUNK, j * LANES:(j + 1) * LANES]
            cols.append(jnp.concatenate([jnp.where(left, s, zero), jnp.where(left, zero, s)], axis=0))
        rhs = jnp.concatenate(cols, axis=1)
        res = jnp.dot(wcat_s[j], rhs, preferred_element_type=F32)
        bias = bsp_ref[:, j * LANES:(j + 1) * LANES]
        for c in range(nch):
            mixed = res[:, c * LANES:(c + 1) * LANES] + bias
            uu = u[c * CHUNK:(c + 1) * CHUNK, j * LANES:(j + 1) * LANES]
            mix_s[c * CHUNK:(c + 1) * CHUNK,
                  CONV_WIDTH + j * LANES:CONV_WIDTH + (j + 1) * LANES] = (uu * mixed).astype(BF16)

    sort_rows(SORT_MAIN // 2, SORT_MAIN)
    x1 = x + jnp.dot(mix_s[...], wout_ref[...], preferred_element_type=F32)
    x1_ref[...] = x1
    h2_s[i % 2] = _rms(x1, gffn_ref[...]).astype(BF16)

    used_rows = BLOCK * jnp.sum(blocks)
    for lo in range(SORT_MAIN, KOUT, TAIL):
        @pl.when(used_rows > lo)
        def _tail(lo=lo):
            sort_rows(lo, lo + TAIL)

        @pl.when(used_rows <= lo)
        def _empty_tail(lo=lo):
            stage_ref[lo:lo + TAIL, :] = jnp.zeros((TAIL, D_MODEL), BF16)


def _mixer_router(x2d, g_mix, w_in, w_conv, g_sgu, w_sp, b_sp, w_out, g_ffn, w_rt, b_rt, seq):
    t = x2d.shape[0]
    nt = t // TM
    full = lambda a: pl.BlockSpec(a.shape, lambda i: (0,) * a.ndim)
    ins = (g_mix, w_in, w_conv, g_sgu, w_sp, b_sp, w_out, g_ffn, w_rt, b_rt)
    prev = lambda i: (jnp.where(i == 0, nt, i - 1), 0)
    return pl.pallas_call(
        functools.partial(_mixer_router_kernel, seq // TM),
        grid=(nt + 1,),
        in_specs=[pl.BlockSpec((TM, D_MODEL), lambda i: (jnp.minimum(i, nt - 1), 0))]
        + [full(a) for a in ins],
        out_specs=[pl.BlockSpec((TM, D_MODEL), lambda i: (i, 0)),
                   pl.BlockSpec((TM, LANES), prev),
                   pl.BlockSpec((SUBLANES, LANES), prev),
                   pl.BlockSpec((KOUT, D_MODEL), prev)],
        out_shape=[jax.ShapeDtypeStruct(((nt + 1) * TM, D_MODEL), F32),
                   jax.ShapeDtypeStruct(((nt + 1) * TM, LANES), F32),
                   jax.ShapeDtypeStruct(((nt + 1) * SUBLANES, LANES), F32),
                   jax.ShapeDtypeStruct(((nt + 1) * KOUT, D_MODEL), BF16)],
        scratch_shapes=[pltpu.VMEM((HEAD_PAIRS, CHUNK, 2 * CHUNK), BF16),
                        pltpu.VMEM((TM, TM), BF16),
                        pltpu.VMEM((8, CONV_WIDTH), F32),
                        pltpu.VMEM((TM, D_MODEL), BF16),
                        pltpu.VMEM((2, TM, D_MODEL), BF16),
                        pltpu.VMEM(w_in.shape, BF16),
                        pltpu.VMEM(w_out.shape, BF16),
                        pltpu.VMEM(w_rt.shape, BF16)],
        compiler_params=pltpu.CompilerParams(dimension_semantics=("arbitrary",),
                                             vmem_limit_bytes=VMEM_LIMIT),
        name="mixer_router",
    )(x2d, *ins)


def _expert_kernel(te_ref, nu_ref, nxt_ref, rows_ref, src_ref, dst_ref, stage_ref,
                   wg_hbm, wu_hbm, wd_hbm, out_ref, xbuf, obuf, wg_f, wu_f, wd_f, wg_s, wu_s, wd_s, sem, osem, wsem):
    i = pl.program_id(0)
    n_used = nu_ref[0]
    slot = i % N_SLOTS

    def gather_copy(seq, j):
        tile = jnp.minimum(seq, n_used - 1)
        src = pl.multiple_of(src_ref[tile * BLOCKS_PER_ROW_TILE + j] * BLOCK, BLOCK)
        slot_ = seq % N_SLOTS
        return pltpu.make_async_copy(stage_ref.at[pl.ds(src, BLOCK)],
                                     xbuf.at[slot_, pl.ds(j * BLOCK, BLOCK)], sem.at[slot_])

    def scatter_copy(tile, j):
        dst = pl.multiple_of(dst_ref[(tile + 1) * BLOCKS_PER_ROW_TILE + j] * BLOCK, BLOCK)
        slot_ = (tile + N_SLOTS) % N_SLOTS
        return pltpu.make_async_copy(obuf.at[slot_, pl.ds(j * BLOCK, BLOCK)],
                                     out_ref.at[pl.ds(dst, BLOCK)], osem.at[slot_])

    def each_block(fn):
        for j in range(BLOCKS_PER_ROW_TILE):
            fn(j)

    def weight_copies(e):
        return (pltpu.make_async_copy(wg_hbm.at[e], wg_f, wsem.at[0]),
                pltpu.make_async_copy(wu_hbm.at[e], wu_f, wsem.at[1]),
                pltpu.make_async_copy(wd_hbm.at[e], wd_f, wsem.at[2]))

    @pl.when(i == 0)
    def _first():
        each_block(lambda j: gather_copy(0, j).start())
        each_block(lambda j: gather_copy(1, j).start())
        for cp in weight_copies(te_ref[0]):
            cp.start(priority=WEIGHT_DMA_PRIORITY)
        obuf[...] = jnp.zeros_like(obuf)

    @pl.when((i >= 2) & (i - 3 < n_used))
    def _free_out_slot():
        each_block(lambda j: scatter_copy(i - 3, j).wait())

    @pl.when(i == n_used)
    def _after_last_tile():
        each_block(lambda j: gather_copy(n_used, j).wait())
        each_block(lambda j: gather_copy(n_used + 1, j).wait())
        each_block(lambda j: scatter_copy(n_used - 1, j).start())

    @pl.when(i < n_used)
    def _tile():
        e = te_ref[i]
        prev = te_ref[jnp.maximum(i - 1, 0)]

        @pl.when((i == 0) | (e != prev))
        def _new_expert():
            for cp in weight_copies(e):
                cp.wait()
            wg_s[...] = wg_f[...].astype(BF16)
            wu_s[...] = wu_f[...].astype(BF16)
            wd_s[...] = wd_f[...].astype(BF16)
            nxt = nxt_ref[i]

            @pl.when(nxt >= 0)
            def _next_weights():
                for cp in weight_copies(nxt):
                    cp.start(priority=WEIGHT_DMA_PRIORITY)

        each_block(lambda j: gather_copy(i, j).wait())

        def mlp(rows):
            hb = xbuf[slot, 0:rows, :]
            a = jnp.dot(hb, wg_s[...], preferred_element_type=F32)
            b = jnp.dot(hb, wu_s[...], preferred_element_type=F32)
            each_block(lambda j: gather_copy(i + 2, j).start())
            each_block(lambda j: scatter_copy(i - 1, j).start())
            hid = (a * jax.nn.sigmoid(a) * b).astype(BF16)
            obuf[slot, 0:rows, :] = jnp.dot(hid, wd_s[...], preferred_element_type=F32).astype(BF16)

        rows_needed = rows_ref[i]
        for rows in range(ROW_STEP, TR + 1, ROW_STEP):
            pl.when(rows_needed == rows)(functools.partial(mlp, rows))


def _experts(tile_expert, n_used, next_expert, tile_rows, src_block, dst_block, stage,
             w_gate, w_up, w_down, n_tiles):
    any_spec = pl.BlockSpec(memory_space=pl.ANY)
    return pl.pallas_call(
        _expert_kernel,
        grid_spec=pltpu.PrefetchScalarGridSpec(
            num_scalar_prefetch=6,
            grid=(n_tiles,),
            in_specs=[any_spec, any_spec, any_spec, any_spec],
            out_specs=any_spec,
            scratch_shapes=[pltpu.VMEM((N_SLOTS, TR, D_MODEL), BF16),
                            pltpu.VMEM((N_SLOTS, TR, D_MODEL), BF16),
                            pltpu.VMEM((D_MODEL, D_EXPERT), F32),
                            pltpu.VMEM((D_MODEL, D_EXPERT), F32),
                            pltpu.VMEM((D_EXPERT, D_MODEL), F32),
                            pltpu.VMEM((D_MODEL, D_EXPERT), BF16),
                            pltpu.VMEM((D_MODEL, D_EXPERT), BF16),
                            pltpu.VMEM((D_EXPERT, D_MODEL), BF16),
                            pltpu.SemaphoreType.DMA((N_SLOTS,)),
                            pltpu.SemaphoreType.DMA((N_SLOTS,)),
                            pltpu.SemaphoreType.DMA((3,))]),
        out_shape=jax.ShapeDtypeStruct(stage.shape, stage.dtype),
        input_output_aliases={6: 0},
        compiler_params=pltpu.CompilerParams(dimension_semantics=("arbitrary",),
                                             vmem_limit_bytes=VMEM_LIMIT),
        name="experts",
    )(tile_expert, n_used, next_expert, tile_rows, src_block, dst_block, stage, w_gate, w_up, w_down)


def _combine_kernel(used_ref, x1_ref, route_ref, p_ref, gple_ref, wpg_f32, wpp_f32, gfin_ref,
                    srt_ref, out_ref, y_s, wpg_ref, wpp_ref):
    i = pl.program_id(0)

    @pl.when(i == 0)
    def _init():
        wpg_ref[...] = wpg_f32[...].astype(BF16)
        wpp_ref[...] = wpp_f32[...].astype(BF16)

    route = route_ref[...]
    w1, w2, pos1, pos2 = (route[:, k:k + 1] for k in (2, 3, 4, 5))

    def unsort(lo, hi):
        col = (lax.broadcasted_iota(I32, (TM, hi - lo), 1) + lo).astype(F32)
        mat = jnp.where(col == pos1, w1, jnp.where(col == pos2, w2, 0.0)).astype(BF16)
        return jnp.dot(mat, srt_ref[lo:hi, :], preferred_element_type=F32)

    pe = jnp.dot(p_ref[...].astype(BF16), wpp_ref[...], preferred_element_type=F32)
    y_s[...] = unsort(0, SORT_MAIN)
    for lo in range(SORT_MAIN, KOUT, TAIL):
        @pl.when(used_ref[i] > lo)
        def _tail(lo=lo):
            y_s[...] += unsort(lo, lo + TAIL)

    x2 = x1_ref[...] + y_s[...]
    hg = _rms(x2, gple_ref[...]).astype(BF16)
    gate = jax.nn.sigmoid(jnp.dot(hg, wpg_ref[...], preferred_element_type=F32))
    x3 = x2 + gate * pe
    out_ref[...] = _rms(x3, gfin_ref[...])


def _combine(used, x1, route, p2d, g_ple, w_pg, w_pp, g_fin, sorted_out):
    t = p2d.shape[0]
    full = lambda a: pl.BlockSpec(a.shape, lambda i, *_: (0,) * a.ndim)
    return pl.pallas_call(
        _combine_kernel,
        grid_spec=pltpu.PrefetchScalarGridSpec(
            num_scalar_prefetch=1,
            grid=(t // TM,),
            in_specs=[pl.BlockSpec((TM, D_MODEL), lambda i, *_: (i, 0)),
                      pl.BlockSpec((TM, LANES), lambda i, *_: (i, 0)),
                      pl.BlockSpec((TM, PLE_DIM), lambda i, *_: (i, 0)),
                      full(g_ple), full(w_pg), full(w_pp), full(g_fin),
                      pl.BlockSpec((KOUT, D_MODEL), lambda i, *_: (i, 0))],
            out_specs=pl.BlockSpec((TM, D_MODEL), lambda i, *_: (i, 0)),
            scratch_shapes=[pltpu.VMEM((TM, D_MODEL), F32),
                            pltpu.VMEM(w_pg.shape, BF16),
                            pltpu.VMEM(w_pp.shape, BF16)]),
        out_shape=jax.ShapeDtypeStruct((t, D_MODEL), F32),
        compiler_params=pltpu.CompilerParams(dimension_semantics=("arbitrary",),
                                             vmem_limit_bytes=VMEM_LIMIT),
        name="combine_ple",
    )(used, x1, route, p2d, g_ple, w_pg, w_pp, g_fin, sorted_out)


def _routing_tables(cnt, nt, n_tiles):
    n = cnt.reshape(-1, SUBLANES, LANES)[:nt, 0, ROUTE_LANE0:ROUTE_LANE0 + N_EXPERTS].astype(I32)
    nblk = (n + BLOCK - 1) // BLOCK
    loc = jnp.cumsum(nblk, axis=1) - nblk
    carry = jnp.cumsum(nblk, axis=0) - nblk
    total = jnp.sum(nblk, axis=0)
    tiles_e = (total + BLOCKS_PER_ROW_TILE - 1) // BLOCKS_PER_ROW_TILE
    tile_end = jnp.cumsum(tiles_e)
    seg0 = (tile_end - tiles_e) * BLOCKS_PER_ROW_TILE
    n_used = tile_end[-1:]

    tile_ids = jnp.minimum(jnp.arange(n_tiles, dtype=I32), n_used[0] - 1)
    tile_expert = jnp.sum(tile_ids[:, None] >= tile_end[None, :], axis=1).astype(I32)

    nb = n_tiles * BLOCKS_PER_ROW_TILE
    b = jnp.arange(nb, dtype=I32)
    oh_e = (tile_expert[:, None] == jnp.arange(N_EXPERTS, dtype=I32)[None, :]).astype(I32)
    per_block = lambda a: jnp.repeat(a, BLOCKS_PER_ROW_TILE, axis=0)
    pick_e = lambda tab: per_block(jnp.sum(oh_e[:, None, :] * tab[None, :, :], axis=2))
    off = b - per_block(jnp.sum(oh_e * seg0[None, :], axis=1))
    run_end = pick_e(carry + nblk)
    run_beg = pick_e(carry)
    run_loc = pick_e(loc)
    tile_of_b = jnp.sum((run_end <= off[:, None]).astype(I32), axis=1)
    valid = (tile_of_b < nt) & (b < n_used[0] * BLOCKS_PER_ROW_TILE)
    oh_t = (jnp.minimum(tile_of_b, nt - 1)[:, None] == jnp.arange(nt, dtype=I32)[None, :]).astype(I32)
    src = (jnp.sum(oh_t * (run_loc - run_beg), axis=1) + off
           + jnp.minimum(tile_of_b, nt - 1) * (KOUT // BLOCK))
    zero_block = KOUT // BLOCK - 1
    dump = nt * (KOUT // BLOCK) + b % DUMP_BLOCKS
    dst = jnp.where(valid, src, dump).astype(I32)
    dump_m1 = nt * (KOUT // BLOCK) + DUMP_BLOCKS - BLOCKS_PER_ROW_TILE + jnp.arange(
        BLOCKS_PER_ROW_TILE, dtype=I32)
    dst = jnp.concatenate([dump_m1, dst])
    src = jnp.where(valid, src, zero_block).astype(I32)

    ids = jnp.arange(N_EXPERTS, dtype=I32)
    later = (tiles_e > 0)[None, :] & (ids[None, :] > ids[:, None])
    next_used = jnp.min(jnp.where(later, ids[None, :], N_EXPERTS), axis=1)
    next_used = jnp.where(next_used == N_EXPERTS, -1, next_used)
    next_expert = jnp.sum((tile_expert[:, None] == ids[None, :]) * next_used[None, :], axis=1)
    used_rows = jnp.sum(nblk, axis=1) * BLOCK
    tiles = jnp.arange(n_tiles, dtype=I32)
    seg_rows = jnp.sum(oh_e * (total * BLOCK)[None, :], axis=1)
    seg_tile0 = jnp.sum(oh_e * (tile_end - tiles_e)[None, :], axis=1)
    tile_rows = jnp.clip(seg_rows - (tiles - seg_tile0) * TR, ROW_STEP, TR)
    tile_rows = (tile_rows + ROW_STEP - 1) // ROW_STEP * ROW_STEP
    return (tile_expert, n_used.astype(I32), next_expert.astype(I32), tile_rows.astype(I32), src, dst,
            used_rows.astype(I32))


def kernel(x, p, g_mix, w_in, w_conv, g_sgu, w_spatial, b_spatial, w_out, g_ffn, w_group, b_group,
           w_router, b_router, w_gate, w_up, w_down, g_ple, w_ple_gate, w_ple_proj, g_final):
    bsz, seq, d = x.shape
    t = bsz * seq
    assert w_in.shape[0] == 1, "single-layer block"
    assert d == D_MODEL and seq % TM == 0 and TM % CHUNK == 0
    nt = t // TM
    max_rows = TOP_K * t + nt * N_EXPERTS * (BLOCK - 1)
    n_tiles = (max_rows + N_EXPERTS * (TR - 1)) // TR + N_SLOTS
    l = 0

    w_rt = jnp.concatenate(
        [w_group[l], jnp.transpose(w_router[l], (1, 0, 2)).reshape(d, N_EXPERTS)], axis=1)
    w_rt = jnp.pad(w_rt, ((0, 0), (0, LANES - w_rt.shape[1])))
    b_rt = jnp.pad(jnp.concatenate([b_group[l], b_router[l].reshape(-1)]),
                   (0, LANES - N_GROUPS - N_EXPERTS)).reshape(1, LANES)
    b_sp = jnp.repeat(b_spatial[l].T, SGU_WIDTH // SGU_HEADS, axis=1)

    x1, route, cnt, stage = _mixer_router(
        x.reshape(t, d), g_mix[l].reshape(1, d), w_in[l], w_conv[l],
        g_sgu[l].reshape(1, -1), w_spatial[l], b_sp, w_out[l],
        g_ffn[l].reshape(1, d), w_rt, b_rt, seq)

    tile_expert, n_used, next_expert, tile_rows, src, dst, used = _routing_tables(cnt, nt, n_tiles)

    sorted_out = _experts(tile_expert, n_used, next_expert, tile_rows, src, dst, stage,
                          w_gate[l].reshape(N_EXPERTS, d, D_EXPERT),
                          w_up[l].reshape(N_EXPERTS, d, D_EXPERT),
                          w_down[l].reshape(N_EXPERTS, D_EXPERT, d), n_tiles)
    out = _combine(used, x1, route, p[l].reshape(t, PLE_DIM), g_ple[l].reshape(1, d),
                   w_ple_gate[l], w_ple_proj[l], g_final.reshape(1, d), sorted_out)
    return out.reshape(bsz, seq, d)
```

```python
import functools

import jax
import jax.numpy as jnp
from jax import lax
from jax.experimental import pallas as pl
from jax.experimental.pallas import tpu as pltpu

F32 = jnp.float32
BF16 = jnp.bfloat16
I32 = jnp.int32

EPS = 1e-6
D_MODEL = 1024
CONV_WIDTH = 512
SGU_WIDTH = 512
SGU_HEADS = 8
HEAD_PAIRS = SGU_HEADS // 2
CHUNK = 128
N_GROUPS = 4
EXPERTS_PER_GROUP = 8
N_EXPERTS = N_GROUPS * EXPERTS_PER_GROUP
TOP_K = 2
D_EXPERT = 512
PLE_DIM = 256
LANES = 128
SUBLANES = 8
BLOCK = 2 * SUBLANES
ROUTE_LANE0 = N_GROUPS
ROUTE_ROWS = 40

TM = 512
TR = 512
BLOCKS_PER_ROW_TILE = TR // BLOCK
KOUT = TOP_K * TM + N_EXPERTS * BLOCK
TAIL = 128
SORT_MAIN = KOUT - 2 * TAIL
ROW_STEP = 128
N_SLOTS = 3
DUMP_BLOCKS = N_SLOTS * BLOCKS_PER_ROW_TILE
assert DUMP_BLOCKS * BLOCK <= KOUT
WEIGHT_DMA_PRIORITY = 1
VMEM_LIMIT = 56 * 1024 * 1024


def _rms(x, g):
    return x * lax.rsqrt(jnp.mean(x * x, axis=-1, keepdims=True) + EPS) * g


def _mixer_router_kernel(tiles_per_seq,
                         x_ref, gmix_ref, win_f32, wconv_ref, gsgu_ref, wsp_ref, bsp_ref,
                         wout_f32, gffn_ref, wrt_f32, brt_ref,
                         x1_ref, route_ref, cnt_ref, stage_ref,
                         wcat_s, tri_s, halo_s, mix_s, h2_s, win_ref, wout_ref, wrt_ref):
    i = pl.program_id(0)
    nch = TM // CHUNK

    @pl.when(i == 0)
    def _init():
        r = lax.broadcasted_iota(I32, (CHUNK, CHUNK), 0)
        c = lax.broadcasted_iota(I32, (CHUNK, CHUNK), 1)
        causal = c <= r
        for j in range(HEAD_PAIRS):
            wa = jnp.where(causal, wsp_ref[2 * j], 0.0)
            wb = jnp.where(causal, wsp_ref[2 * j + 1], 0.0)
            wcat_s[j] = jnp.concatenate([wa, wb], axis=1).astype(BF16)
        rr = lax.broadcasted_iota(I32, (TM, TM), 0)
        cc = lax.broadcasted_iota(I32, (TM, TM), 1)
        tri_s[...] = (rr < cc).astype(BF16)
        h2_s[...] = jnp.zeros_like(h2_s)
        win_ref[...] = win_f32[...].astype(BF16)
        wout_ref[...] = wout_f32[...].astype(BF16)
        wrt_ref[...] = wrt_f32[...].astype(BF16)

    @pl.when(i % tiles_per_seq == 0)
    def _seq_start():
        halo_s[...] = jnp.zeros_like(halo_s)

    x = x_ref[...]
    hb = _rms(x, gmix_ref[...]).astype(BF16)
    h2b = h2_s[(i + 1) % 2]

    def proj(k):
        return jnp.dot(hb, win_ref[:, k * 512:(k + 1) * 512], preferred_element_type=F32)

    logits = lax.dot_general(wrt_ref[...], h2b, (((1,), (1,)), ((), ())),
                             preferred_element_type=F32)[0:ROUTE_ROWS, :] + brt_ref[0:ROUTE_ROWS, :]
    pu = proj(3)
    pv = proj(4)

    ridx = lax.broadcasted_iota(I32, (ROUTE_ROWS, TM), 0).astype(F32)
    neg = jnp.float32(-jnp.inf)
    big = jnp.float32(1e9)
    is_g = ridx < N_GROUPS
    gl = jnp.where(is_g, logits, neg)
    gmax = jnp.max(gl, axis=0, keepdims=True)
    gsum = jnp.sum(jnp.where(is_g, jnp.exp(gl - gmax), 0.0), axis=0, keepdims=True)
    g_w = 1.0 / gsum
    g_idx = jnp.min(jnp.where(gl == gmax, ridx, big), axis=0, keepdims=True)
    lo_row = ROUTE_LANE0 + EXPERTS_PER_GROUP * g_idx
    in_grp = (ridx >= lo_row) & (ridx < lo_row + EXPERTS_PER_GROUP)
    el = jnp.where(in_grp, logits, neg)
    v1 = jnp.max(el, axis=0, keepdims=True)
    i1 = jnp.min(jnp.where(el == v1, ridx, big), axis=0, keepdims=True)
    el2 = jnp.where(ridx == i1, neg, el)
    v2 = jnp.max(el2, axis=0, keepdims=True)
    i2 = jnp.min(jnp.where(el2 == v2, ridx, big), axis=0, keepdims=True)
    e21 = jnp.exp(v2 - v1)
    w1 = g_w / (1.0 + e21)
    w2 = g_w * e21 / (1.0 + e21)

    u = jax.nn.gelu(pu)
    zc = proj(1) * proj(2)
    v = jax.nn.gelu(pv)
    vc = v - jnp.mean(v, axis=-1, keepdims=True)
    vn = vc * lax.rsqrt(jnp.mean(vc * vc, axis=-1, keepdims=True) + EPS) * gsgu_ref[...]
    vnb = vn.astype(BF16)

    row = lax.broadcasted_iota(I32, (TM, CONV_WIDTH), 0)
    h6 = halo_s[6:7, :]
    h7 = halo_s[7:8, :]
    z1 = jnp.where(row == 0, h7, pltpu.roll(zc, 1, 0))
    z2 = jnp.where(row == 0, h6, jnp.where(row == 1, h7, pltpu.roll(zc, 2, 0)))
    conv = z2 * wconv_ref[0:1, :] + z1 * wconv_ref[1:2, :] + zc * wconv_ref[2:3, :]
    halo_s[...] = zc[TM - 8:TM, :]
    mix_s[:, 0:CONV_WIDTH] = (proj(0) * conv).astype(BF16)

    sel1 = ridx == i1
    sel2 = ridx == i2
    onehot = jnp.where(sel1 | sel2, 1.0, 0.0)
    counts = jnp.sum(onehot, axis=1, keepdims=True)
    pad_rows = jnp.zeros((LANES - ROUTE_ROWS, LANES), F32)
    counts_sq = jnp.concatenate([jnp.broadcast_to(counts, (ROUTE_ROWS, LANES)), pad_rows], axis=0)
    blocks_sq = jnp.ceil(counts_sq * (1.0 / BLOCK))
    er = lax.broadcasted_iota(I32, (LANES, LANES), 0)
    ec = lax.broadcasted_iota(I32, (LANES, LANES), 1)
    before = (ec < er).astype(BF16)
    run_start = BLOCK * jnp.dot(before, blocks_sq.astype(BF16),
                                preferred_element_type=F32)[0:ROUTE_ROWS, 0:1]
    rank = jnp.dot(onehot.astype(BF16), tri_s[...], preferred_element_type=F32) + run_start
    pos1_row = jnp.sum(jnp.where(sel1, rank, 0.0), axis=0, keepdims=True)
    pos2_row = jnp.sum(jnp.where(sel2, rank, 0.0), axis=0, keepdims=True)
    cnt_ref[...] = counts_sq

    r8 = lax.broadcasted_iota(I32, (SUBLANES, TM), 0)
    rows8 = jnp.where(r8 == 0, i1 - ROUTE_LANE0, 0.0)
    rows8 = jnp.where(r8 == 1, i2 - ROUTE_LANE0, rows8)
    rows8 = jnp.where(r8 == 2, w1, rows8)
    rows8 = jnp.where(r8 == 3, w2, rows8)
    rows8 = jnp.where(r8 == 4, pos1_row, rows8)
    rows8 = jnp.where(r8 == 5, pos2_row, rows8)
    route_ref[...] = jnp.concatenate(
        [rows8, jnp.zeros((LANES - SUBLANES, TM), F32)], axis=0).T

    def sort_rows(lo, hi):
        out_row = (lax.broadcasted_iota(I32, (hi - lo, TM), 0) + lo).astype(F32)
        sort_mat = jnp.where((out_row == pos1_row) | (out_row == pos2_row), 1.0, 0.0).astype(BF16)
        stage_ref[lo:hi, :] = jnp.dot(sort_mat, h2b, preferred_element_type=F32).astype(BF16)

    sort_rows(0, SORT_MAIN // 2)
    left = lax.broadcasted_iota(I32, (CHUNK, LANES), 1) < (LANES // 2)
    zero = jnp.zeros((CHUNK, LANES), BF16)
    for j in range(HEAD_PAIRS):
        cols = []
        for c in range(nch):
            s = vnb[c * CHUNK:(c + 1) * CHUNK, j * LANES:(j + 1) * LANES]
            cols.append(jnp.concatenate([jnp.where(left, s, zero), jnp.where(left, zero, s)], axis=0))
        rhs = jnp.concatenate(cols, axis=1)
        res = jnp.dot(wcat_s[j], rhs, preferred_element_type=F32)
        bias = bsp_ref[:, j * LANES:(j + 1) * LANES]
        for c in range(nch):
            mixed = res[:, c * LANES:(c + 1) * LANES] + bias
            uu = u[c * CHUNK:(c + 1) * CHUNK, j * LANES:(j + 1) * LANES]
            mix_s[c * CHUNK:(c + 1) * CHUNK,
                  CONV_WIDTH + j * LANES:CONV_WIDTH + (j + 1) * LANES] = (uu * mixed).astype(BF16)

    sort_rows(SORT_MAIN // 2, SORT_MAIN)
    x1 = x + jnp.dot(mix_s[...], wout_ref[...], preferred_element_type=F32)
    x1_ref[...] = x1
    h2_s[i % 2] = _rms(x1, gffn_ref[...]).astype(BF16)

    used_rows = BLOCK * jnp.sum(blocks_sq[:, 0:1])
    for lo in range(SORT_MAIN, KOUT, TAIL):
        @pl.when(used_rows > lo)
        def _tail(lo=lo):
            sort_rows(lo, lo + TAIL)

        @pl.when(used_rows <= lo)
        def _empty_tail(lo=lo):
            stage_ref[lo:lo + TAIL, :] = jnp.zeros((TAIL, D_MODEL), BF16)


def _mixer_router(x2d, g_mix, w_in, w_conv, g_sgu, w_sp, b_sp, w_out, g_ffn, w_rt, b_rt, seq):
    t = x2d.shape[0]
    nt = t // TM
    full = lambda a: pl.BlockSpec(a.shape, lambda i: (0,) * a.ndim)
    ins = (g_mix, w_in, w_conv, g_sgu, w_sp, b_sp, w_out, g_ffn, w_rt, b_rt)
    prev = lambda i: (jnp.where(i == 0, nt, i - 1), 0)
    return pl.pallas_call(
        functools.partial(_mixer_router_kernel, seq // TM),
        grid=(nt + 1,),
        in_specs=[pl.BlockSpec((TM, D_MODEL), lambda i: (jnp.minimum(i, nt - 1), 0))]
        + [full(a) for a in ins],
        out_specs=[pl.BlockSpec((TM, D_MODEL), lambda i: (i, 0)),
                   pl.BlockSpec((TM, LANES), prev),
                   pl.BlockSpec((LANES, LANES), prev),
                   pl.BlockSpec((KOUT, D_MODEL), prev)],
        out_shape=[jax.ShapeDtypeStruct(((nt + 1) * TM, D_MODEL), F32),
                   jax.ShapeDtypeStruct(((nt + 1) * TM, LANES), F32),
                   jax.ShapeDtypeStruct(((nt + 1) * LANES, LANES), F32),
                   jax.ShapeDtypeStruct(((nt + 1) * KOUT, D_MODEL), BF16)],
        scratch_shapes=[pltpu.VMEM((HEAD_PAIRS, CHUNK, 2 * CHUNK), BF16),
                        pltpu.VMEM((TM, TM), BF16),
                        pltpu.VMEM((8, CONV_WIDTH), F32),
                        pltpu.VMEM((TM, D_MODEL), BF16),
                        pltpu.VMEM((2, TM, D_MODEL), BF16),
                        pltpu.VMEM(w_in.shape, BF16),
                        pltpu.VMEM(w_out.shape, BF16),
                        pltpu.VMEM(w_rt.shape, BF16)],
        compiler_params=pltpu.CompilerParams(dimension_semantics=("arbitrary",),
                                             vmem_limit_bytes=VMEM_LIMIT),
        name="mixer_router",
    )(x2d, *ins)


def _expert_kernel(te_ref, nu_ref, nxt_ref, rows_ref, src_ref, dst_ref, stage_ref,
                   wg_hbm, wu_hbm, wd_hbm, out_ref, xbuf, obuf, wg_f, wu_f, wd_f, wg_s, wu_s, wd_s, sem, osem, wsem):
    i = pl.program_id(0)
    n_used = nu_ref[0]
    slot = i % N_SLOTS

    def gather_copy(seq, j):
        tile = jnp.minimum(seq, n_used - 1)
        src = pl.multiple_of(src_ref[tile * BLOCKS_PER_ROW_TILE + j] * BLOCK, BLOCK)
        slot_ = seq % N_SLOTS
        return pltpu.make_async_copy(stage_ref.at[pl.ds(src, BLOCK)],
                                     xbuf.at[slot_, pl.ds(j * BLOCK, BLOCK)], sem.at[slot_])

    def scatter_copy(tile, j):
        dst = pl.multiple_of(dst_ref[(tile + 1) * BLOCKS_PER_ROW_TILE + j] * BLOCK, BLOCK)
        slot_ = (tile + N_SLOTS) % N_SLOTS
        return pltpu.make_async_copy(obuf.at[slot_, pl.ds(j * BLOCK, BLOCK)],
                                     out_ref.at[pl.ds(dst, BLOCK)], osem.at[slot_])

    def each_block(fn):
        for j in range(BLOCKS_PER_ROW_TILE):
            fn(j)

    def weight_copies(e):
        return (pltpu.make_async_copy(wg_hbm.at[e], wg_f, wsem.at[0]),
                pltpu.make_async_copy(wu_hbm.at[e], wu_f, wsem.at[1]),
                pltpu.make_async_copy(wd_hbm.at[e], wd_f, wsem.at[2]))

    @pl.when(i == 0)
    def _first():
        each_block(lambda j: gather_copy(0, j).start())
        each_block(lambda j: gather_copy(1, j).start())
        for cp in weight_copies(te_ref[0]):
            cp.start(priority=WEIGHT_DMA_PRIORITY)
        obuf[...] = jnp.zeros_like(obuf)

    @pl.when((i >= 2) & (i - 3 < n_used))
    def _free_out_slot():
        each_block(lambda j: scatter_copy(i - 3, j).wait())

    @pl.when(i == n_used)
    def _after_last_tile():
        each_block(lambda j: gather_copy(n_used, j).wait())
        each_block(lambda j: gather_copy(n_used + 1, j).wait())
        each_block(lambda j: scatter_copy(n_used - 1, j).start())

    @pl.when(i < n_used)
    def _tile():
        e = te_ref[i]
        prev = te_ref[jnp.maximum(i - 1, 0)]

        @pl.when((i == 0) | (e != prev))
        def _new_expert():
            for cp in weight_copies(e):
                cp.wait()
            wg_s[...] = wg_f[...].astype(BF16)
            wu_s[...] = wu_f[...].astype(BF16)
            wd_s[...] = wd_f[...].astype(BF16)
            nxt = nxt_ref[i]

            @pl.when(nxt >= 0)
            def _next_weights():
                for cp in weight_copies(nxt):
                    cp.start(priority=WEIGHT_DMA_PRIORITY)

        each_block(lambda j: gather_copy(i, j).wait())

        def mlp(rows):
            hb = xbuf[slot, 0:rows, :]
            a = jnp.dot(hb, wg_s[...], preferred_element_type=F32)
            b = jnp.dot(hb, wu_s[...], preferred_element_type=F32)
            each_block(lambda j: gather_copy(i + 2, j).start())
            each_block(lambda j: scatter_copy(i - 1, j).start())
            hid = (a * jax.nn.sigmoid(a) * b).astype(BF16)
            obuf[slot, 0:rows, :] = jnp.dot(hid, wd_s[...], preferred_element_type=F32).astype(BF16)

        rows_needed = rows_ref[i]
        for rows in range(ROW_STEP, TR + 1, ROW_STEP):
            pl.when(rows_needed == rows)(functools.partial(mlp, rows))


def _experts(tile_expert, n_used, next_expert, tile_rows, src_block, dst_block, stage,
             w_gate, w_up, w_down, n_tiles):
    any_spec = pl.BlockSpec(memory_space=pl.ANY)
    return pl.pallas_call(
        _expert_kernel,
        grid_spec=pltpu.PrefetchScalarGridSpec(
            num_scalar_prefetch=6,
            grid=(n_tiles,),
            in_specs=[any_spec, any_spec, any_spec, any_spec],
            out_specs=any_spec,
            scratch_shapes=[pltpu.VMEM((N_SLOTS, TR, D_MODEL), BF16),
                            pltpu.VMEM((N_SLOTS, TR, D_MODEL), BF16),
                            pltpu.VMEM((D_MODEL, D_EXPERT), F32),
                            pltpu.VMEM((D_MODEL, D_EXPERT), F32),
                            pltpu.VMEM((D_EXPERT, D_MODEL), F32),
                            pltpu.VMEM((D_MODEL, D_EXPERT), BF16),
                            pltpu.VMEM((D_MODEL, D_EXPERT), BF16),
                            pltpu.VMEM((D_EXPERT, D_MODEL), BF16),
                            pltpu.SemaphoreType.DMA((N_SLOTS,)),
                            pltpu.SemaphoreType.DMA((N_SLOTS,)),
                            pltpu.SemaphoreType.DMA((3,))]),
        out_shape=jax.ShapeDtypeStruct(stage.shape, stage.dtype),
        input_output_aliases={6: 0},
        compiler_params=pltpu.CompilerParams(dimension_semantics=("arbitrary",),
                                             vmem_limit_bytes=VMEM_LIMIT),
        name="experts",
    )(tile_expert, n_used, next_expert, tile_rows, src_block, dst_block, stage, w_gate, w_up, w_down)


def _combine_kernel(used_ref, x1_ref, route_ref, p_ref, gple_ref, wpg_f32, wpp_f32, gfin_ref,
                    srt_ref, out_ref, y_s, wpg_ref, wpp_ref):
    i = pl.program_id(0)

    @pl.when(i == 0)
    def _init():
        wpg_ref[...] = wpg_f32[...].astype(BF16)
        wpp_ref[...] = wpp_f32[...].astype(BF16)

    route = route_ref[...]
    w1, w2, pos1, pos2 = (route[:, k:k + 1] for k in (2, 3, 4, 5))

    def unsort(lo, hi):
        col = (lax.broadcasted_iota(I32, (TM, hi - lo), 1) + lo).astype(F32)
        mat = jnp.where(col == pos1, w1, jnp.where(col == pos2, w2, 0.0)).astype(BF16)
        return jnp.dot(mat, srt_ref[lo:hi, :], preferred_element_type=F32)

    pe = jnp.dot(p_ref[...].astype(BF16), wpp_ref[...], preferred_element_type=F32)
    y_s[...] = unsort(0, SORT_MAIN)
    for lo in range(SORT_MAIN, KOUT, TAIL):
        @pl.when(used_ref[i] > lo)
        def _tail(lo=lo):
            y_s[...] += unsort(lo, lo + TAIL)

    x2 = x1_ref[...] + y_s[...]
    hg = _rms(x2, gple_ref[...]).astype(BF16)
    gate = jax.nn.sigmoid(jnp.dot(hg, wpg_ref[...], preferred_element_type=F32))
    x3 = x2 + gate * pe
    out_ref[...] = _rms(x3, gfin_ref[...])


def _combine(used, x1, route, p2d, g_ple, w_pg, w_pp, g_fin, sorted_out):
    t = p2d.shape[0]
    full = lambda a: pl.BlockSpec(a.shape, lambda i, *_: (0,) * a.ndim)
    return pl.pallas_call(
        _combine_kernel,
        grid_spec=pltpu.PrefetchScalarGridSpec(
            num_scalar_prefetch=1,
            grid=(t // TM,),
            in_specs=[pl.BlockSpec((TM, D_MODEL), lambda i, *_: (i, 0)),
                      pl.BlockSpec((TM, LANES), lambda i, *_: (i, 0)),
                      pl.BlockSpec((TM, PLE_DIM), lambda i, *_: (i, 0)),
                      full(g_ple), full(w_pg), full(w_pp), full(g_fin),
                      pl.BlockSpec((KOUT, D_MODEL), lambda i, *_: (i, 0))],
            out_specs=pl.BlockSpec((TM, D_MODEL), lambda i, *_: (i, 0)),
            scratch_shapes=[pltpu.VMEM((TM, D_MODEL), F32),
                            pltpu.VMEM(w_pg.shape, BF16),
                            pltpu.VMEM(w_pp.shape, BF16)]),
        out_shape=jax.ShapeDtypeStruct((t, D_MODEL), F32),
        compiler_params=pltpu.CompilerParams(dimension_semantics=("arbitrary",),
                                             vmem_limit_bytes=VMEM_LIMIT),
        name="combine_ple",
    )(used, x1, route, p2d, g_ple, w_pg, w_pp, g_fin, sorted_out)


def _routing_tables(cnt, nt, n_tiles):
    n = cnt.reshape(-1, LANES, LANES)[:nt, ROUTE_LANE0:ROUTE_LANE0 + N_EXPERTS, 0].astype(I32)
    nblk = (n + BLOCK - 1) // BLOCK
    loc = jnp.cumsum(nblk, axis=1) - nblk
    carry = jnp.cumsum(nblk, axis=0) - nblk
    total = jnp.sum(nblk, axis=0)
    tiles_e = (total + BLOCKS_PER_ROW_TILE - 1) // BLOCKS_PER_ROW_TILE
    tile_end = jnp.cumsum(tiles_e)
    seg0 = (tile_end - tiles_e) * BLOCKS_PER_ROW_TILE
    n_used = tile_end[-1:]

    tile_ids = jnp.minimum(jnp.arange(n_tiles, dtype=I32), n_used[0] - 1)
    tile_expert = jnp.sum(tile_ids[:, None] >= tile_end[None, :], axis=1).astype(I32)

    nb = n_tiles * BLOCKS_PER_ROW_TILE
    b = jnp.arange(nb, dtype=I32)
    oh_e = (tile_expert[:, None] == jnp.arange(N_EXPERTS, dtype=I32)[None, :]).astype(I32)
    per_block = lambda a: jnp.repeat(a, BLOCKS_PER_ROW_TILE, axis=0)
    pick_e = lambda tab: per_block(jnp.sum(oh_e[:, None, :] * tab[None, :, :], axis=2))
    off = b - per_block(jnp.sum(oh_e * seg0[None, :], axis=1))
    run_end = pick_e(carry + nblk)
    run_beg = pick_e(carry)
    run_loc = pick_e(loc)
    tile_of_b = jnp.sum((run_end <= off[:, None]).astype(I32), axis=1)
    valid = (tile_of_b < nt) & (b < n_used[0] * BLOCKS_PER_ROW_TILE)
    oh_t = (jnp.minimum(tile_of_b, nt - 1)[:, None] == jnp.arange(nt, dtype=I32)[None, :]).astype(I32)
    src = (jnp.sum(oh_t * (run_loc - run_beg), axis=1) + off
           + jnp.minimum(tile_of_b, nt - 1) * (KOUT // BLOCK))
    zero_block = KOUT // BLOCK - 1
    dump = nt * (KOUT // BLOCK) + b % DUMP_BLOCKS
    dst = jnp.where(valid, src, dump).astype(I32)
    dump_m1 = nt * (KOUT // BLOCK) + DUMP_BLOCKS - BLOCKS_PER_ROW_TILE + jnp.arange(
        BLOCKS_PER_ROW_TILE, dtype=I32)
    dst = jnp.concatenate([dump_m1, dst])
    src = jnp.where(valid, src, zero_block).astype(I32)

    ids = jnp.arange(N_EXPERTS, dtype=I32)
    later = (tiles_e > 0)[None, :] & (ids[None, :] > ids[:, None])
    next_used = jnp.min(jnp.where(later, ids[None, :], N_EXPERTS), axis=1)
    next_used = jnp.where(next_used == N_EXPERTS, -1, next_used)
    next_expert = jnp.sum((tile_expert[:, None] == ids[None, :]) * next_used[None, :], axis=1)
    used_rows = jnp.sum(nblk, axis=1) * BLOCK
    tiles = jnp.arange(n_tiles, dtype=I32)
    seg_rows = jnp.sum(oh_e * (total * BLOCK)[None, :], axis=1)
    seg_tile0 = jnp.sum(oh_e * (tile_end - tiles_e)[None, :], axis=1)
    tile_rows = jnp.clip(seg_rows - (tiles - seg_tile0) * TR, ROW_STEP, TR)
    tile_rows = (tile_rows + ROW_STEP - 1) // ROW_STEP * ROW_STEP
    return (tile_expert, n_used.astype(I32), next_expert.astype(I32), tile_rows.astype(I32), src, dst,
            used_rows.astype(I32))


def kernel(x, p, g_mix, w_in, w_conv, g_sgu, w_spatial, b_spatial, w_out, g_ffn, w_group, b_group,
           w_router, b_router, w_gate, w_up, w_down, g_ple, w_ple_gate, w_ple_proj, g_final):
    bsz, seq, d = x.shape
    t = bsz * seq
    assert w_in.shape[0] == 1, "single-layer block"
    assert d == D_MODEL and seq % TM == 0 and TM % CHUNK == 0
    nt = t // TM
    max_rows = TOP_K * t + nt * N_EXPERTS * (BLOCK - 1)
    n_tiles = (max_rows + N_EXPERTS * (TR - 1)) // TR + N_SLOTS
    l = 0

    w_rt = jnp.concatenate(
        [w_group[l], jnp.transpose(w_router[l], (1, 0, 2)).reshape(d, N_EXPERTS)], axis=1)
    w_rt = jnp.pad(w_rt, ((0, 0), (0, LANES - w_rt.shape[1]))).T
    b_rt = jnp.pad(jnp.concatenate([b_group[l], b_router[l].reshape(-1)]),
                   (0, LANES - N_GROUPS - N_EXPERTS)).reshape(LANES, 1)
    b_sp = jnp.repeat(b_spatial[l].T, SGU_WIDTH // SGU_HEADS, axis=1)

    x1, route, cnt, stage = _mixer_router(
        x.reshape(t, d), g_mix[l].reshape(1, d), w_in[l], w_conv[l],
        g_sgu[l].reshape(1, -1), w_spatial[l], b_sp, w_out[l],
        g_ffn[l].reshape(1, d), w_rt, b_rt, seq)

    tile_expert, n_used, next_expert, tile_rows, src, dst, used = _routing_tables(cnt, nt, n_tiles)

    sorted_out = _experts(tile_expert, n_used, next_expert, tile_rows, src, dst, stage,
                          w_gate[l].reshape(N_EXPERTS, d, D_EXPERT),
                          w_up[l].reshape(N_EXPERTS, d, D_EXPERT),
                          w_down[l].reshape(N_EXPERTS, D_EXPERT, d), n_tiles)
    out = _combine(used, x1, route, p[l].reshape(t, PLE_DIM), g_ple[l].reshape(1, d),
                   w_ple_gate[l], w_ple_proj[l], g_final.reshape(1, d), sorted_out)
    return out.reshape(bsz, seq, d)
```

```python
import functools

import jax
import jax.numpy as jnp
from jax import lax
from jax.experimental import pallas as pl
from jax.experimental.pallas import tpu as pltpu

F32 = jnp.float32
BF16 = jnp.bfloat16
I32 = jnp.int32

EPS = 1e-6
D_MODEL = 1024
CONV_WIDTH = 512
SGU_WIDTH = 512
SGU_HEADS = 8
HEAD_PAIRS = SGU_HEADS // 2
CHUNK = 128
N_GROUPS = 4
EXPERTS_PER_GROUP = 8
N_EXPERTS = N_GROUPS * EXPERTS_PER_GROUP
TOP_K = 2
D_EXPERT = 512
PLE_DIM = 256
LANES = 128
SUBLANES = 8
BLOCK = 2 * SUBLANES
ROUTE_LANE0 = N_GROUPS
ROUTE_ROWS = 40

TM = 512
TR = 512
BLOCKS_PER_ROW_TILE = TR // BLOCK
KOUT = TOP_K * TM + N_EXPERTS * BLOCK
TAIL = 128
SORT_MAIN = KOUT - 2 * TAIL
ROW_STEP = 128
N_SLOTS = 3
DUMP_BLOCKS = N_SLOTS * BLOCKS_PER_ROW_TILE
assert DUMP_BLOCKS * BLOCK <= KOUT
WEIGHT_DMA_PRIORITY = 1
VMEM_LIMIT = 56 * 1024 * 1024


def _rms(x, g):
    return x * lax.rsqrt(jnp.mean(x * x, axis=-1, keepdims=True) + EPS) * g


def _build_tables(n_s, nt, meta_ref, src_ref, dst_ref, used_ref):
    tp = meta_ref.shape[0]
    bpt = BLOCKS_PER_ROW_TILE
    kb = KOUT // BLOCK
    n = n_s[0:nt, :]
    lane_n = lax.broadcasted_iota(I32, (nt, LANES), 1)
    is_e = (lane_n >= ROUTE_LANE0) & (lane_n < ROUTE_LANE0 + N_EXPERTS)
    nblk = jnp.where(is_e, jnp.floor((n + (BLOCK - 1)) * (1.0 / BLOCK)), 0.0)
    er = lax.broadcasted_iota(I32, (LANES, LANES), 0)
    ec = lax.broadcasted_iota(I32, (LANES, LANES), 1)
    loc = jnp.dot(nblk.astype(BF16), (er < ec).astype(BF16), preferred_element_type=F32)
    ti = lax.broadcasted_iota(I32, (nt, nt), 0)
    tj = lax.broadcasted_iota(I32, (nt, nt), 1)
    carry = jnp.dot((tj < ti).astype(BF16), nblk.astype(BF16), preferred_element_type=F32)
    total = jnp.sum(nblk, axis=0, keepdims=True)
    tiles_e = jnp.floor((total + (bpt - 1)) * (1.0 / bpt))
    tile_end = jnp.dot(jnp.broadcast_to(tiles_e, (SUBLANES, LANES)).astype(BF16),
                       (er <= ec).astype(BF16), preferred_element_type=F32)[0:1, :]
    tile0 = tile_end - tiles_e
    n_used = jnp.max(tile_end, axis=1, keepdims=True)

    tau = lax.broadcasted_iota(I32, (tp, LANES), 0).astype(F32)
    lane_i = lax.broadcasted_iota(I32, (tp, LANES), 1)
    lane_t = lane_i.astype(F32)
    is_e_t = (lane_i >= ROUTE_LANE0) & (lane_i < ROUTE_LANE0 + N_EXPERTS)
    te = jnp.sum(jnp.where(is_e_t & (tile_end <= jnp.minimum(tau, n_used - 1.0)), 1.0, 0.0),
                 axis=1, keepdims=True)
    oh = lane_t == te + ROUTE_LANE0
    pick = lambda row: jnp.sum(jnp.where(oh, row, 0.0), axis=1, keepdims=True)
    seg0_t = pick(tile0) * bpt
    e_lane = lane_t - ROUTE_LANE0
    cand = jnp.where(is_e_t & (e_lane > te) & (tiles_e > 0.0), e_lane, 99.0)
    nxt = jnp.min(cand, axis=1, keepdims=True)
    nxt = jnp.where(nxt == 99.0, -1.0, nxt)
    rows_t = jnp.clip(pick(total) * BLOCK - (tau[:, 0:1] - pick(tile0)) * TR, ROW_STEP, TR)
    rows_t = jnp.ceil(rows_t * (1.0 / ROW_STEP)) * ROW_STEP
    meta = jnp.where(lane_i == 0, te, 0.0)
    meta = jnp.where(lane_i == 1, nxt, meta)
    meta = jnp.where(lane_i == 2, rows_t, meta)
    meta = jnp.where(lane_i == 3, n_used, meta)
    meta_ref[...] = meta.astype(I32)

    bidx = tau * bpt + lane_t
    off = bidx - seg0_t
    ohb = jnp.where(oh, 1.0, 0.0).astype(BF16)

    def per_tile(tab):
        hi = jnp.floor(tab * (1.0 / 32))
        lo = tab - 32.0 * hi
        nt_dot = lambda a: lax.dot_general(ohb, a.astype(BF16), (((1,), (1,)), ((), ())),
                                           preferred_element_type=F32)
        return 32.0 * nt_dot(hi) + nt_dot(lo)

    run_end = per_tile(carry + nblk)
    run_beg = per_tile(carry)
    run_loc = per_tile(loc)
    tile_of = jnp.zeros((tp, LANES), F32)
    for it in range(nt):
        tile_of = tile_of + jnp.where(run_end[:, it:it + 1] <= off, 1.0, 0.0)
    srcv = off
    for it in range(nt):
        srcv = srcv + jnp.where(tile_of == it,
                                run_loc[:, it:it + 1] - run_beg[:, it:it + 1] + it * kb, 0.0)
    valid = (tile_of < nt) & (bidx < n_used * bpt)
    slot3 = tau - N_SLOTS * jnp.floor((tau + 0.5) * (1.0 / N_SLOTS))
    dump = nt * kb + slot3 * bpt + lane_t
    src_ref[...] = jnp.where(valid, srcv, kb - 1.0).astype(I32)
    dst_ref[0:tp, :] = jnp.where(valid, srcv, dump).astype(I32)
    dst_ref[tp:tp + SUBLANES, :] = (nt * kb + DUMP_BLOCKS - bpt
                                    + lax.broadcasted_iota(I32, (SUBLANES, LANES), 1))
    used_ref[...] = jnp.broadcast_to(jnp.sum(nblk, axis=1, keepdims=True) * BLOCK,
                                     (nt, LANES)).astype(I32)


def _mixer_router_kernel(tiles_per_seq, nt,
                         x_ref, gmix_ref, win_f32, wconv_ref, gsgu_ref, wsp_ref, bsp_ref,
                         wout_f32, gffn_ref, wrt_f32, brt_ref,
                         x1_ref, route_ref, stage_ref, meta_ref, src_ref, dst_ref, used_ref,
                         wcat_s, tri_s, halo_s, mix_s, h2_s, win_ref, wout_ref, wrt_ref, n_s):
    i = pl.program_id(0)
    nch = TM // CHUNK

    @pl.when(i == 0)
    def _init():
        r = lax.broadcasted_iota(I32, (CHUNK, CHUNK), 0)
        c = lax.broadcasted_iota(I32, (CHUNK, CHUNK), 1)
        causal = c <= r
        for j in range(HEAD_PAIRS):
            wa = jnp.where(causal, wsp_ref[2 * j], 0.0)
            wb = jnp.where(causal, wsp_ref[2 * j + 1], 0.0)
            wcat_s[j] = jnp.concatenate([wa, wb], axis=1).astype(BF16)
        rr = lax.broadcasted_iota(I32, (TM, TM), 0)
        cc = lax.broadcasted_iota(I32, (TM, TM), 1)
        tri_s[...] = (rr < cc).astype(BF16)
        h2_s[...] = jnp.zeros_like(h2_s)
        win_ref[...] = win_f32[...].astype(BF16)
        wout_ref[...] = wout_f32[...].astype(BF16)
        wrt_ref[...] = wrt_f32[...].astype(BF16)

    @pl.when(i % tiles_per_seq == 0)
    def _seq_start():
        halo_s[...] = jnp.zeros_like(halo_s)

    x = x_ref[...]
    hb = _rms(x, gmix_ref[...]).astype(BF16)
    h2b = h2_s[(i + 1) % 2]

    def proj(k):
        return jnp.dot(hb, win_ref[:, k * 512:(k + 1) * 512], preferred_element_type=F32)

    logits = lax.dot_general(wrt_ref[...], h2b, (((1,), (1,)), ((), ())),
                             preferred_element_type=F32)[0:ROUTE_ROWS, :] + brt_ref[0:ROUTE_ROWS, :]
    pu = proj(3)
    pv = proj(4)

    ridx = lax.broadcasted_iota(I32, (ROUTE_ROWS, TM), 0).astype(F32)
    neg = jnp.float32(-jnp.inf)
    big = jnp.float32(1e9)
    is_g = ridx < N_GROUPS
    gl = jnp.where(is_g, logits, neg)
    gmax = jnp.max(gl, axis=0, keepdims=True)
    gsum = jnp.sum(jnp.where(is_g, jnp.exp(gl - gmax), 0.0), axis=0, keepdims=True)
    g_w = 1.0 / gsum
    g_idx = jnp.min(jnp.where(gl == gmax, ridx, big), axis=0, keepdims=True)
    lo_row = ROUTE_LANE0 + EXPERTS_PER_GROUP * g_idx
    in_grp = (ridx >= lo_row) & (ridx < lo_row + EXPERTS_PER_GROUP)
    el = jnp.where(in_grp, logits, neg)
    v1 = jnp.max(el, axis=0, keepdims=True)
    i1 = jnp.min(jnp.where(el == v1, ridx, big), axis=0, keepdims=True)
    el2 = jnp.where(ridx == i1, neg, el)
    v2 = jnp.max(el2, axis=0, keepdims=True)
    i2 = jnp.min(jnp.where(el2 == v2, ridx, big), axis=0, keepdims=True)
    e21 = jnp.exp(v2 - v1)
    w1 = g_w / (1.0 + e21)
    w2 = g_w * e21 / (1.0 + e21)

    u = jax.nn.gelu(pu)
    zc = proj(1) * proj(2)
    v = jax.nn.gelu(pv)
    vc = v - jnp.mean(v, axis=-1, keepdims=True)
    vn = vc * lax.rsqrt(jnp.mean(vc * vc, axis=-1, keepdims=True) + EPS) * gsgu_ref[...]
    vnb = vn.astype(BF16)

    row = lax.broadcasted_iota(I32, (TM, CONV_WIDTH), 0)
    h6 = halo_s[6:7, :]
    h7 = halo_s[7:8, :]
    z1 = jnp.where(row == 0, h7, pltpu.roll(zc, 1, 0))
    z2 = jnp.where(row == 0, h6, jnp.where(row == 1, h7, pltpu.roll(zc, 2, 0)))
    conv = z2 * wconv_ref[0:1, :] + z1 * wconv_ref[1:2, :] + zc * wconv_ref[2:3, :]
    halo_s[...] = zc[TM - 8:TM, :]
    mix_s[:, 0:CONV_WIDTH] = (proj(0) * conv).astype(BF16)

    sel1 = ridx == i1
    sel2 = ridx == i2
    onehot = jnp.where(sel1 | sel2, 1.0, 0.0)
    counts = jnp.sum(onehot, axis=1, keepdims=True)
    pad_rows = jnp.zeros((LANES - ROUTE_ROWS, LANES), F32)
    counts_sq = jnp.concatenate([jnp.broadcast_to(counts, (ROUTE_ROWS, LANES)), pad_rows], axis=0)
    blocks_sq = jnp.ceil(counts_sq * (1.0 / BLOCK))
    er = lax.broadcasted_iota(I32, (LANES, LANES), 0)
    ec = lax.broadcasted_iota(I32, (LANES, LANES), 1)
    before = (ec < er).astype(BF16)
    run_start = BLOCK * jnp.dot(before, blocks_sq.astype(BF16),
                                preferred_element_type=F32)[0:ROUTE_ROWS, 0:1]
    rank = jnp.dot(onehot.astype(BF16), tri_s[...], preferred_element_type=F32) + run_start
    pos1_row = jnp.sum(jnp.where(sel1, rank, 0.0), axis=0, keepdims=True)
    pos2_row = jnp.sum(jnp.where(sel2, rank, 0.0), axis=0, keepdims=True)
    n_s[pl.ds(jnp.where(i == 0, nt, i - 1), 1), :] = counts_sq.T[0:1, :]

    r8 = lax.broadcasted_iota(I32, (SUBLANES, TM), 0)
    rows8 = jnp.where(r8 == 0, i1 - ROUTE_LANE0, 0.0)
    rows8 = jnp.where(r8 == 1, i2 - ROUTE_LANE0, rows8)
    rows8 = jnp.where(r8 == 2, w1, rows8)
    rows8 = jnp.where(r8 == 3, w2, rows8)
    rows8 = jnp.where(r8 == 4, pos1_row, rows8)
    rows8 = jnp.where(r8 == 5, pos2_row, rows8)
    route_ref[...] = jnp.concatenate(
        [rows8, jnp.zeros((LANES - SUBLANES, TM), F32)], axis=0).T

    def sort_rows(lo, hi):
        out_row = (lax.broadcasted_iota(I32, (hi - lo, TM), 0) + lo).astype(F32)
        sort_mat = jnp.where((out_row == pos1_row) | (out_row == pos2_row), 1.0, 0.0).astype(BF16)
        stage_ref[lo:hi, :] = jnp.dot(sort_mat, h2b, preferred_element_type=F32).astype(BF16)

    sort_rows(0, SORT_MAIN // 2)
    left = lax.broadcasted_iota(I32, (CHUNK, LANES), 1) < (LANES // 2)
    zero = jnp.zeros((CHUNK, LANES), BF16)
    for j in range(HEAD_PAIRS):
        cols = []
        for c in range(nch):
            s = vnb[c * CHUNK:(c + 1) * CHUNK, j * LANES:(j + 1) * LANES]
            cols.append(jnp.concatenate([jnp.where(left, s, zero), jnp.where(left, zero, s)], axis=0))
        rhs = jnp.concatenate(cols, axis=1)
        res = jnp.dot(wcat_s[j], rhs, preferred_element_type=F32)
        bias = bsp_ref[:, j * LANES:(j + 1) * LANES]
        for c in range(nch):
            mixed = res[:, c * LANES:(c + 1) * LANES] + bias
            uu = u[c * CHUNK:(c + 1) * CHUNK, j * LANES:(j + 1) * LANES]
            mix_s[c * CHUNK:(c + 1) * CHUNK,
                  CONV_WIDTH + j * LANES:CONV_WIDTH + (j + 1) * LANES] = (uu * mixed).astype(BF16)

    sort_rows(SORT_MAIN // 2, SORT_MAIN)
    x1 = x + jnp.dot(mix_s[...], wout_ref[...], preferred_element_type=F32)
    x1_ref[...] = x1
    h2_s[i % 2] = _rms(x1, gffn_ref[...]).astype(BF16)

    used_rows = BLOCK * jnp.sum(blocks_sq[:, 0:1])
    for lo in range(SORT_MAIN, KOUT, TAIL):
        @pl.when(used_rows > lo)
        def _tail(lo=lo):
            sort_rows(lo, lo + TAIL)

        @pl.when(used_rows <= lo)
        def _empty_tail(lo=lo):
            stage_ref[lo:lo + TAIL, :] = jnp.zeros((TAIL, D_MODEL), BF16)

    @pl.when(i == nt)
    def _tables():
        _build_tables(n_s, nt, meta_ref, src_ref, dst_ref, used_ref)


def _mixer_router(x2d, g_mix, w_in, w_conv, g_sgu, w_sp, b_sp, w_out, g_ffn, w_rt, b_rt, seq,
                  n_tiles):
    t = x2d.shape[0]
    nt = t // TM
    tp = -(-n_tiles // SUBLANES) * SUBLANES
    whole = lambda r: pl.BlockSpec((r, LANES), lambda i: (0, 0))
    full = lambda a: pl.BlockSpec(a.shape, lambda i: (0,) * a.ndim)
    ins = (g_mix, w_in, w_conv, g_sgu, w_sp, b_sp, w_out, g_ffn, w_rt, b_rt)
    prev = lambda i: (jnp.where(i == 0, nt, i - 1), 0)
    return pl.pallas_call(
        functools.partial(_mixer_router_kernel, seq // TM, nt),
        grid=(nt + 1,),
        in_specs=[pl.BlockSpec((TM, D_MODEL), lambda i: (jnp.minimum(i, nt - 1), 0))]
        + [full(a) for a in ins],
        out_specs=[pl.BlockSpec((TM, D_MODEL), lambda i: (i, 0)),
                   pl.BlockSpec((TM, LANES), prev),
                   pl.BlockSpec((KOUT, D_MODEL), prev),
                   whole(tp), whole(tp), whole(tp + SUBLANES), whole(nt)],
        out_shape=[jax.ShapeDtypeStruct(((nt + 1) * TM, D_MODEL), F32),
                   jax.ShapeDtypeStruct(((nt + 1) * TM, LANES), F32),
                   jax.ShapeDtypeStruct(((nt + 1) * KOUT, D_MODEL), BF16),
                   jax.ShapeDtypeStruct((tp, LANES), I32),
                   jax.ShapeDtypeStruct((tp, LANES), I32),
                   jax.ShapeDtypeStruct((tp + SUBLANES, LANES), I32),
                   jax.ShapeDtypeStruct((nt, LANES), I32)],
        scratch_shapes=[pltpu.VMEM((HEAD_PAIRS, CHUNK, 2 * CHUNK), BF16),
                        pltpu.VMEM((TM, TM), BF16),
                        pltpu.VMEM((8, CONV_WIDTH), F32),
                        pltpu.VMEM((TM, D_MODEL), BF16),
                        pltpu.VMEM((2, TM, D_MODEL), BF16),
                        pltpu.VMEM(w_in.shape, BF16),
                        pltpu.VMEM(w_out.shape, BF16),
                        pltpu.VMEM(w_rt.shape, BF16),
                        pltpu.VMEM((nt + SUBLANES, LANES), F32)],
        compiler_params=pltpu.CompilerParams(dimension_semantics=("arbitrary",),
                                             vmem_limit_bytes=VMEM_LIMIT),
        name="mixer_router",
    )(x2d, *ins)


def _expert_kernel(meta_ref, src_ref, dst_ref, stage_ref,
                   wg_hbm, wu_hbm, wd_hbm, out_ref,
                   xbuf, obuf, wg_f, wu_f, wd_f, wg_s, wu_s, wd_s, sem, osem, wsem):
    i = pl.program_id(0)
    n_used = meta_ref[3]
    slot = i % N_SLOTS
    dump_row = dst_ref.shape[0] // LANES - SUBLANES
    expert_of = lambda tile: meta_ref[tile * LANES]

    def gather_copy(seq, j):
        tile = jnp.minimum(seq, n_used - 1)
        src = pl.multiple_of(src_ref[tile * LANES + j] * BLOCK, BLOCK)
        slot_ = seq % N_SLOTS
        return pltpu.make_async_copy(stage_ref.at[pl.ds(src, BLOCK)],
                                     xbuf.at[slot_, pl.ds(j * BLOCK, BLOCK)], sem.at[slot_])

    def scatter_copy(tile, j):
        row = jnp.where(tile < 0, dump_row, tile)
        dst = pl.multiple_of(dst_ref[row * LANES + j] * BLOCK, BLOCK)
        slot_ = (tile + N_SLOTS) % N_SLOTS
        return pltpu.make_async_copy(obuf.at[slot_, pl.ds(j * BLOCK, BLOCK)],
                                     out_ref.at[pl.ds(dst, BLOCK)], osem.at[slot_])

    def each_block(fn):
        for j in range(BLOCKS_PER_ROW_TILE):
            fn(j)

    def weight_copies(e):
        return (pltpu.make_async_copy(wg_hbm.at[e], wg_f, wsem.at[0]),
                pltpu.make_async_copy(wu_hbm.at[e], wu_f, wsem.at[1]),
                pltpu.make_async_copy(wd_hbm.at[e], wd_f, wsem.at[2]))

    @pl.when(i == 0)
    def _first():
        each_block(lambda j: gather_copy(0, j).start())
        each_block(lambda j: gather_copy(1, j).start())
        for cp in weight_copies(expert_of(0)):
            cp.start(priority=WEIGHT_DMA_PRIORITY)
        obuf[...] = jnp.zeros_like(obuf)

    @pl.when((i >= 2) & (i - 3 < n_used))
    def _free_out_slot():
        each_block(lambda j: scatter_copy(i - 3, j).wait())

    @pl.when(i == n_used)
    def _after_last_tile():
        each_block(lambda j: gather_copy(n_used, j).wait())
        each_block(lambda j: gather_copy(n_used + 1, j).wait())
        each_block(lambda j: scatter_copy(n_used - 1, j).start())

    @pl.when(i < n_used)
    def _tile():
        e = expert_of(i)
        prev = expert_of(jnp.maximum(i - 1, 0))

        @pl.when((i == 0) | (e != prev))
        def _new_expert():
            for cp in weight_copies(e):
                cp.wait()
            wg_s[...] = wg_f[...].astype(BF16)
            wu_s[...] = wu_f[...].astype(BF16)
            wd_s[...] = wd_f[...].astype(BF16)
            nxt = meta_ref[i * LANES + 1]

            @pl.when(nxt >= 0)
            def _next_weights():
                for cp in weight_copies(nxt):
                    cp.start(priority=WEIGHT_DMA_PRIORITY)

        each_block(lambda j: gather_copy(i, j).wait())

        def mlp(rows):
            hb = xbuf[slot, 0:rows, :]
            a = jnp.dot(hb, wg_s[...], preferred_element_type=F32)
            b = jnp.dot(hb, wu_s[...], preferred_element_type=F32)
            each_block(lambda j: gather_copy(i + 2, j).start())
            each_block(lambda j: scatter_copy(i - 1, j).start())
            hid = (a * jax.nn.sigmoid(a) * b).astype(BF16)
            obuf[slot, 0:rows, :] = jnp.dot(hid, wd_s[...], preferred_element_type=F32).astype(BF16)

        rows_needed = meta_ref[i * LANES + 2]
        for rows in range(ROW_STEP, TR + 1, ROW_STEP):
            pl.when(rows_needed == rows)(functools.partial(mlp, rows))


def _experts(meta, src_block, dst_block, stage, w_gate, w_up, w_down, n_tiles):
    any_spec = pl.BlockSpec(memory_space=pl.ANY)
    return pl.pallas_call(
        _expert_kernel,
        grid_spec=pltpu.PrefetchScalarGridSpec(
            num_scalar_prefetch=3,
            grid=(n_tiles,),
            in_specs=[any_spec, any_spec, any_spec, any_spec],
            out_specs=any_spec,
            scratch_shapes=[pltpu.VMEM((N_SLOTS, TR, D_MODEL), BF16),
                            pltpu.VMEM((N_SLOTS, TR, D_MODEL), BF16),
                            pltpu.VMEM((D_MODEL, D_EXPERT), F32),
                            pltpu.VMEM((D_MODEL, D_EXPERT), F32),
                            pltpu.VMEM((D_EXPERT, D_MODEL), F32),
                            pltpu.VMEM((D_MODEL, D_EXPERT), BF16),
                            pltpu.VMEM((D_MODEL, D_EXPERT), BF16),
                            pltpu.VMEM((D_EXPERT, D_MODEL), BF16),
                            pltpu.SemaphoreType.DMA((N_SLOTS,)),
                            pltpu.SemaphoreType.DMA((N_SLOTS,)),
                            pltpu.SemaphoreType.DMA((3,))]),
        out_shape=jax.ShapeDtypeStruct(stage.shape, stage.dtype),
        input_output_aliases={3: 0},
        compiler_params=pltpu.CompilerParams(dimension_semantics=("arbitrary",),
                                             vmem_limit_bytes=VMEM_LIMIT),
        name="experts",
    )(meta.reshape(-1), src_block.reshape(-1), dst_block.reshape(-1), stage, w_gate, w_up, w_down)


def _combine_kernel(used_ref, x1_ref, route_ref, p_ref, gple_ref, wpg_f32, wpp_f32, gfin_ref,
                    srt_ref, out_ref, y_s, wpg_ref, wpp_ref):
    i = pl.program_id(0)

    @pl.when(i == 0)
    def _init():
        wpg_ref[...] = wpg_f32[...].astype(BF16)
        wpp_ref[...] = wpp_f32[...].astype(BF16)

    route = route_ref[...]
    w1, w2, pos1, pos2 = (route[:, k:k + 1] for k in (2, 3, 4, 5))

    def unsort(lo, hi):
        col = (lax.broadcasted_iota(I32, (TM, hi - lo), 1) + lo).astype(F32)
        mat = jnp.where(col == pos1, w1, jnp.where(col == pos2, w2, 0.0)).astype(BF16)
        return jnp.dot(mat, srt_ref[lo:hi, :], preferred_element_type=F32)

    pe = jnp.dot(p_ref[...].astype(BF16), wpp_ref[...], preferred_element_type=F32)
    y_s[...] = unsort(0, SORT_MAIN)
    for lo in range(SORT_MAIN, KOUT, TAIL):
        @pl.when(used_ref[i * LANES] > lo)
        def _tail(lo=lo):
            y_s[...] += unsort(lo, lo + TAIL)

    x2 = x1_ref[...] + y_s[...]
    hg = _rms(x2, gple_ref[...]).astype(BF16)
    gate = jax.nn.sigmoid(jnp.dot(hg, wpg_ref[...], preferred_element_type=F32))
    x3 = x2 + gate * pe
    out_ref[...] = _rms(x3, gfin_ref[...])


def _combine(used, x1, route, p2d, g_ple, w_pg, w_pp, g_fin, sorted_out):
    t = p2d.shape[0]
    full = lambda a: pl.BlockSpec(a.shape, lambda i, *_: (0,) * a.ndim)
    return pl.pallas_call(
        _combine_kernel,
        grid_spec=pltpu.PrefetchScalarGridSpec(
            num_scalar_prefetch=1,
            grid=(t // TM,),
            in_specs=[pl.BlockSpec((TM, D_MODEL), lambda i, *_: (i, 0)),
                      pl.BlockSpec((TM, LANES), lambda i, *_: (i, 0)),
                      pl.BlockSpec((TM, PLE_DIM), lambda i, *_: (i, 0)),
                      full(g_ple), full(w_pg), full(w_pp), full(g_fin),
                      pl.BlockSpec((KOUT, D_MODEL), lambda i, *_: (i, 0))],
            out_specs=pl.BlockSpec((TM, D_MODEL), lambda i, *_: (i, 0)),
            scratch_shapes=[pltpu.VMEM((TM, D_MODEL), F32),
                            pltpu.VMEM(w_pg.shape, BF16),
                            pltpu.VMEM(w_pp.shape, BF16)]),
        out_shape=jax.ShapeDtypeStruct((t, D_MODEL), F32),
        compiler_params=pltpu.CompilerParams(dimension_semantics=("arbitrary",),
                                             vmem_limit_bytes=VMEM_LIMIT),
        name="combine_ple",
    )(used.reshape(-1), x1, route, p2d, g_ple, w_pg, w_pp, g_fin, sorted_out)


def kernel(x, p, g_mix, w_in, w_conv, g_sgu, w_spatial, b_spatial, w_out, g_ffn, w_group, b_group,
           w_router, b_router, w_gate, w_up, w_down, g_ple, w_ple_gate, w_ple_proj, g_final):
    bsz, seq, d = x.shape
    t = bsz * seq
    assert w_in.shape[0] == 1, "single-layer block"
    assert d == D_MODEL and seq % TM == 0 and TM % CHUNK == 0
    nt = t // TM
    max_rows = TOP_K * t + nt * N_EXPERTS * (BLOCK - 1)
    n_tiles = (max_rows + N_EXPERTS * (TR - 1)) // TR + N_SLOTS
    l = 0

    w_rt = jnp.concatenate(
        [w_group[l], jnp.transpose(w_router[l], (1, 0, 2)).reshape(d, N_EXPERTS)], axis=1)
    w_rt = jnp.pad(w_rt, ((0, 0), (0, LANES - w_rt.shape[1]))).T
    b_rt = jnp.pad(jnp.concatenate([b_group[l], b_router[l].reshape(-1)]),
                   (0, LANES - N_GROUPS - N_EXPERTS)).reshape(LANES, 1)
    b_sp = jnp.repeat(b_spatial[l].T, SGU_WIDTH // SGU_HEADS, axis=1)

    x1, route, stage, meta, src, dst, used = _mixer_router(
        x.reshape(t, d), g_mix[l].reshape(1, d), w_in[l], w_conv[l],
        g_sgu[l].reshape(1, -1), w_spatial[l], b_sp, w_out[l],
        g_ffn[l].reshape(1, d), w_rt, b_rt, seq, n_tiles)

    sorted_out = _experts(meta, src, dst, stage,
                          w_gate[l].reshape(N_EXPERTS, d, D_EXPERT),
                          w_up[l].reshape(N_EXPERTS, d, D_EXPERT),
                          w_down[l].reshape(N_EXPERTS, D_EXPERT, d), n_tiles)
    out = _combine(used, x1, route, p[l].reshape(t, PLE_DIM), g_ple[l].reshape(1, d),
                   w_ple_gate[l], w_ple_proj[l], g_final.reshape(1, d), sorted_out)
    return out.reshape(bsz, seq, d)
```

```python
import functools

import jax
import jax.numpy as jnp
from jax import lax
from jax.experimental import pallas as pl
from jax.experimental.pallas import tpu as pltpu

F32 = jnp.float32
BF16 = jnp.bfloat16
I32 = jnp.int32

EPS = 1e-6
D_MODEL = 1024
CONV_WIDTH = 512
SGU_WIDTH = 512
SGU_HEADS = 8
HEAD_PAIRS = SGU_HEADS // 2
CHUNK = 128
N_GROUPS = 4
EXPERTS_PER_GROUP = 8
N_EXPERTS = N_GROUPS * EXPERTS_PER_GROUP
TOP_K = 2
D_EXPERT = 512
PLE_DIM = 256
LANES = 128
SUBLANES = 8
BLOCK = 2 * SUBLANES
ROUTE_LANE0 = N_GROUPS
ROUTE_ROWS = 40

TM = 512
TR = 512
BLOCKS_PER_ROW_TILE = TR // BLOCK
KOUT = TOP_K * TM + N_EXPERTS * BLOCK
TAIL = 128
SORT_MAIN = KOUT - 2 * TAIL
META_W = 4
ROW_STEP = 128
N_SLOTS = 3
DUMP_BLOCKS = N_SLOTS * BLOCKS_PER_ROW_TILE
assert DUMP_BLOCKS * BLOCK <= KOUT
WEIGHT_DMA_PRIORITY = 1
VMEM_LIMIT = 56 * 1024 * 1024


def _rms(x, g):
    return x * lax.rsqrt(jnp.mean(x * x, axis=-1, keepdims=True) + EPS) * g


def _build_tables(n_s, nt, meta_ref, src_ref, dst_ref, used_ref):
    tp = meta_ref.shape[0]
    bpt = BLOCKS_PER_ROW_TILE
    kb = KOUT // BLOCK
    n = n_s[0:nt, :]
    lane_n = lax.broadcasted_iota(I32, (nt, LANES), 1)
    is_e = (lane_n >= ROUTE_LANE0) & (lane_n < ROUTE_LANE0 + N_EXPERTS)
    nblk = jnp.where(is_e, jnp.floor((n + (BLOCK - 1)) * (1.0 / BLOCK)), 0.0)
    er = lax.broadcasted_iota(I32, (LANES, LANES), 0)
    ec = lax.broadcasted_iota(I32, (LANES, LANES), 1)
    loc = jnp.dot(nblk.astype(BF16), (er < ec).astype(BF16), preferred_element_type=F32)
    ti = lax.broadcasted_iota(I32, (nt, nt), 0)
    tj = lax.broadcasted_iota(I32, (nt, nt), 1)
    carry = jnp.dot((tj < ti).astype(BF16), nblk.astype(BF16), preferred_element_type=F32)
    total = jnp.sum(nblk, axis=0, keepdims=True)
    tiles_e = jnp.floor((total + (bpt - 1)) * (1.0 / bpt))
    tile_end = jnp.dot(jnp.broadcast_to(tiles_e, (SUBLANES, LANES)).astype(BF16),
                       (er <= ec).astype(BF16), preferred_element_type=F32)[0:1, :]
    tile0 = tile_end - tiles_e
    n_used = jnp.max(tile_end, axis=1, keepdims=True)

    tau = lax.broadcasted_iota(I32, (tp, LANES), 0).astype(F32)
    lane_i = lax.broadcasted_iota(I32, (tp, LANES), 1)
    lane_t = lane_i.astype(F32)
    is_e_t = (lane_i >= ROUTE_LANE0) & (lane_i < ROUTE_LANE0 + N_EXPERTS)
    te = jnp.sum(jnp.where(is_e_t & (tile_end <= jnp.minimum(tau, n_used - 1.0)), 1.0, 0.0),
                 axis=1, keepdims=True)
    oh = lane_t == te + ROUTE_LANE0
    pick = lambda row: jnp.sum(jnp.where(oh, row, 0.0), axis=1, keepdims=True)
    seg0_t = pick(tile0) * bpt
    e_lane = lane_t - ROUTE_LANE0
    cand = jnp.where(is_e_t & (e_lane > te) & (tiles_e > 0.0), e_lane, 99.0)
    nxt = jnp.min(cand, axis=1, keepdims=True)
    nxt = jnp.where(nxt == 99.0, -1.0, nxt)
    rows_t = jnp.clip(pick(total) * BLOCK - (tau[:, 0:1] - pick(tile0)) * TR, ROW_STEP, TR)
    rows_t = jnp.ceil(rows_t * (1.0 / ROW_STEP)) * ROW_STEP
    meta = jnp.where(lane_i == 0, te, 0.0)
    meta = jnp.where(lane_i == 1, nxt, meta)
    meta = jnp.where(lane_i == 2, rows_t, meta)
    meta = jnp.where(lane_i == 3, n_used, meta)
    meta_ref[...] = meta.astype(I32)

    bidx = tau * bpt + lane_t
    off = bidx - seg0_t
    ohb = jnp.where(oh, 1.0, 0.0).astype(BF16)

    def per_tile(tab):
        hi = jnp.floor(tab * (1.0 / 32))
        lo = tab - 32.0 * hi
        nt_dot = lambda a: lax.dot_general(ohb, a.astype(BF16), (((1,), (1,)), ((), ())),
                                           preferred_element_type=F32)
        return 32.0 * nt_dot(hi) + nt_dot(lo)

    run_end = per_tile(carry + nblk)
    run_beg = per_tile(carry)
    run_loc = per_tile(loc)
    tile_of = jnp.zeros((tp, LANES), F32)
    for it in range(nt):
        tile_of = tile_of + jnp.where(run_end[:, it:it + 1] <= off, 1.0, 0.0)
    srcv = off
    for it in range(nt):
        srcv = srcv + jnp.where(tile_of == it,
                                run_loc[:, it:it + 1] - run_beg[:, it:it + 1] + it * kb, 0.0)
    valid = (tile_of < nt) & (bidx < n_used * bpt)
    slot3 = tau - N_SLOTS * jnp.floor((tau + 0.5) * (1.0 / N_SLOTS))
    dump = nt * kb + slot3 * bpt + lane_t
    src_ref[...] = jnp.where(valid, srcv, kb - 1.0).astype(I32)
    dst_ref[0:tp, :] = jnp.where(valid, srcv, dump).astype(I32)
    dst_ref[tp:tp + SUBLANES, :] = (nt * kb + DUMP_BLOCKS - bpt
                                    + lax.broadcasted_iota(I32, (SUBLANES, LANES), 1))
    used_ref[...] = jnp.broadcast_to(jnp.sum(nblk, axis=1, keepdims=True) * BLOCK,
                                     (nt, LANES)).astype(I32)


def _mixer_router_kernel(tiles_per_seq, nt,
                         x_ref, gmix_ref, win_f32, wconv_ref, gsgu_ref, wsp_ref, bsp_ref,
                         wout_f32, gffn_ref, wrt_f32, brt_ref,
                         x1_ref, route_ref, stage_ref, meta_ref, src_ref, dst_ref, used_ref,
                         wcat_s, tri_s, halo_s, mix_s, h2_s, win_ref, wout_ref, wrt_ref, n_s):
    i = pl.program_id(0)
    nch = TM // CHUNK

    @pl.when(i == 0)
    def _init():
        r = lax.broadcasted_iota(I32, (CHUNK, CHUNK), 0)
        c = lax.broadcasted_iota(I32, (CHUNK, CHUNK), 1)
        causal = c <= r
        for j in range(HEAD_PAIRS):
            wa = jnp.where(causal, wsp_ref[2 * j], 0.0)
            wb = jnp.where(causal, wsp_ref[2 * j + 1], 0.0)
            wcat_s[j] = jnp.concatenate([wa, wb], axis=1).astype(BF16)
        rr = lax.broadcasted_iota(I32, (TM, TM), 0)
        cc = lax.broadcasted_iota(I32, (TM, TM), 1)
        tri_s[...] = (rr < cc).astype(BF16)
        h2_s[...] = jnp.zeros_like(h2_s)
        win_ref[...] = win_f32[...].astype(BF16)
        wout_ref[...] = wout_f32[...].astype(BF16)
        wrt_ref[...] = wrt_f32[...].astype(BF16)

    @pl.when(i % tiles_per_seq == 0)
    def _seq_start():
        halo_s[...] = jnp.zeros_like(halo_s)

    x = x_ref[...]
    hb = _rms(x, gmix_ref[...]).astype(BF16)
    h2b = h2_s[(i + 1) % 2]

    def proj(k):
        return jnp.dot(hb, win_ref[:, k * 512:(k + 1) * 512], preferred_element_type=F32)

    logits = lax.dot_general(wrt_ref[...], h2b, (((1,), (1,)), ((), ())),
                             preferred_element_type=F32)[0:ROUTE_ROWS, :] + brt_ref[0:ROUTE_ROWS, :]
    pu = proj(3)
    pv = proj(4)

    ridx = lax.broadcasted_iota(I32, (ROUTE_ROWS, TM), 0).astype(F32)
    neg = jnp.float32(-jnp.inf)
    big = jnp.float32(1e9)
    is_g = ridx < N_GROUPS
    gl = jnp.where(is_g, logits, neg)
    gmax = jnp.max(gl, axis=0, keepdims=True)
    gsum = jnp.sum(jnp.where(is_g, jnp.exp(gl - gmax), 0.0), axis=0, keepdims=True)
    g_w = 1.0 / gsum
    g_idx = jnp.min(jnp.where(gl == gmax, ridx, big), axis=0, keepdims=True)
    lo_row = ROUTE_LANE0 + EXPERTS_PER_GROUP * g_idx
    in_grp = (ridx >= lo_row) & (ridx < lo_row + EXPERTS_PER_GROUP)
    el = jnp.where(in_grp, logits, neg)
    v1 = jnp.max(el, axis=0, keepdims=True)
    i1 = jnp.min(jnp.where(el == v1, ridx, big), axis=0, keepdims=True)
    el2 = jnp.where(ridx == i1, neg, el)
    v2 = jnp.max(el2, axis=0, keepdims=True)
    i2 = jnp.min(jnp.where(el2 == v2, ridx, big), axis=0, keepdims=True)
    e21 = jnp.exp(v2 - v1)
    w1 = g_w / (1.0 + e21)
    w2 = g_w * e21 / (1.0 + e21)

    u = jax.nn.gelu(pu)
    zc = proj(1) * proj(2)
    v = jax.nn.gelu(pv)
    vc = v - jnp.mean(v, axis=-1, keepdims=True)
    vn = vc * lax.rsqrt(jnp.mean(vc * vc, axis=-1, keepdims=True) + EPS) * gsgu_ref[...]
    vnb = vn.astype(BF16)

    row = lax.broadcasted_iota(I32, (TM, CONV_WIDTH), 0)
    h6 = halo_s[6:7, :]
    h7 = halo_s[7:8, :]
    z1 = jnp.where(row == 0, h7, pltpu.roll(zc, 1, 0))
    z2 = jnp.where(row == 0, h6, jnp.where(row == 1, h7, pltpu.roll(zc, 2, 0)))
    conv = z2 * wconv_ref[0:1, :] + z1 * wconv_ref[1:2, :] + zc * wconv_ref[2:3, :]
    halo_s[...] = zc[TM - 8:TM, :]
    mix_s[:, 0:CONV_WIDTH] = (proj(0) * conv).astype(BF16)

    sel1 = ridx == i1
    sel2 = ridx == i2
    onehot = jnp.where(sel1 | sel2, 1.0, 0.0)
    counts = jnp.sum(onehot, axis=1, keepdims=True)
    pad_rows = jnp.zeros((LANES - ROUTE_ROWS, LANES), F32)
    counts_sq = jnp.concatenate([jnp.broadcast_to(counts, (ROUTE_ROWS, LANES)), pad_rows], axis=0)
    blocks_sq = jnp.ceil(counts_sq * (1.0 / BLOCK))
    er = lax.broadcasted_iota(I32, (LANES, LANES), 0)
    ec = lax.broadcasted_iota(I32, (LANES, LANES), 1)
    before = (ec < er).astype(BF16)
    run_start = BLOCK * jnp.dot(before, blocks_sq.astype(BF16),
                                preferred_element_type=F32)[0:ROUTE_ROWS, 0:1]
    rank = jnp.dot(onehot.astype(BF16), tri_s[...], preferred_element_type=F32) + run_start
    pos1_row = jnp.sum(jnp.where(sel1, rank, 0.0), axis=0, keepdims=True)
    pos2_row = jnp.sum(jnp.where(sel2, rank, 0.0), axis=0, keepdims=True)
    n_s[pl.ds(jnp.where(i == 0, nt, i - 1), 1), :] = counts_sq.T[0:1, :]

    r8 = lax.broadcasted_iota(I32, (SUBLANES, TM), 0)
    rows8 = jnp.where(r8 == 0, i1 - ROUTE_LANE0, 0.0)
    rows8 = jnp.where(r8 == 1, i2 - ROUTE_LANE0, rows8)
    rows8 = jnp.where(r8 == 2, w1, rows8)
    rows8 = jnp.where(r8 == 3, w2, rows8)
    rows8 = jnp.where(r8 == 4, pos1_row, rows8)
    rows8 = jnp.where(r8 == 5, pos2_row, rows8)
    route_ref[...] = jnp.concatenate(
        [rows8, jnp.zeros((LANES - SUBLANES, TM), F32)], axis=0).T

    def sort_rows(lo, hi):
        out_row = (lax.broadcasted_iota(I32, (hi - lo, TM), 0) + lo).astype(F32)
        sort_mat = jnp.where((out_row == pos1_row) | (out_row == pos2_row), 1.0, 0.0).astype(BF16)
        stage_ref[lo:hi, :] = jnp.dot(sort_mat, h2b, preferred_element_type=F32).astype(BF16)

    sort_rows(0, SORT_MAIN // 2)
    left = lax.broadcasted_iota(I32, (CHUNK, LANES), 1) < (LANES // 2)
    zero = jnp.zeros((CHUNK, LANES), BF16)
    for j in range(HEAD_PAIRS):
        cols = []
        for c in range(nch):
            s = vnb[c * CHUNK:(c + 1) * CHUNK, j * LANES:(j + 1) * LANES]
            cols.append(jnp.concatenate([jnp.where(left, s, zero), jnp.where(left, zero, s)], axis=0))
        rhs = jnp.concatenate(cols, axis=1)
        res = jnp.dot(wcat_s[j], rhs, preferred_element_type=F32)
        bias = bsp_ref[:, j * LANES:(j + 1) * LANES]
        for c in range(nch):
            mixed = res[:, c * LANES:(c + 1) * LANES] + bias
            uu = u[c * CHUNK:(c + 1) * CHUNK, j * LANES:(j + 1) * LANES]
            mix_s[c * CHUNK:(c + 1) * CHUNK,
                  CONV_WIDTH + j * LANES:CONV_WIDTH + (j + 1) * LANES] = (uu * mixed).astype(BF16)

    sort_rows(SORT_MAIN // 2, SORT_MAIN)
    x1 = x + jnp.dot(mix_s[...], wout_ref[...], preferred_element_type=F32)
    x1_ref[...] = x1
    h2_s[i % 2] = _rms(x1, gffn_ref[...]).astype(BF16)

    used_rows = BLOCK * jnp.sum(blocks_sq[:, 0:1])
    for lo in range(SORT_MAIN, KOUT, TAIL):
        @pl.when(used_rows > lo)
        def _tail(lo=lo):
            sort_rows(lo, lo + TAIL)

        @pl.when(used_rows <= lo)
        def _empty_tail(lo=lo):
            stage_ref[lo:lo + TAIL, :] = jnp.zeros((TAIL, D_MODEL), BF16)

    @pl.when(i == nt)
    def _tables():
        _build_tables(n_s, nt, meta_ref, src_ref, dst_ref, used_ref)


def _mixer_router(x2d, g_mix, w_in, w_conv, g_sgu, w_sp, b_sp, w_out, g_ffn, w_rt, b_rt, seq,
                  n_tiles):
    t = x2d.shape[0]
    nt = t // TM
    tp = -(-n_tiles // SUBLANES) * SUBLANES
    whole = lambda r: pl.BlockSpec((r, LANES), lambda i: (0, 0))
    full = lambda a: pl.BlockSpec(a.shape, lambda i: (0,) * a.ndim)
    ins = (g_mix, w_in, w_conv, g_sgu, w_sp, b_sp, w_out, g_ffn, w_rt, b_rt)
    prev = lambda i: (jnp.where(i == 0, nt, i - 1), 0)
    return pl.pallas_call(
        functools.partial(_mixer_router_kernel, seq // TM, nt),
        grid=(nt + 1,),
        in_specs=[pl.BlockSpec((TM, D_MODEL), lambda i: (jnp.minimum(i, nt - 1), 0))]
        + [full(a) for a in ins],
        out_specs=[pl.BlockSpec((TM, D_MODEL), lambda i: (i, 0)),
                   pl.BlockSpec((TM, LANES), prev),
                   pl.BlockSpec((KOUT, D_MODEL), prev),
                   whole(tp), whole(tp), whole(tp + SUBLANES), whole(nt)],
        out_shape=[jax.ShapeDtypeStruct(((nt + 1) * TM, D_MODEL), F32),
                   jax.ShapeDtypeStruct(((nt + 1) * TM, LANES), F32),
                   jax.ShapeDtypeStruct(((nt + 1) * KOUT, D_MODEL), BF16),
                   jax.ShapeDtypeStruct((tp, LANES), I32),
                   jax.ShapeDtypeStruct((tp, LANES), I32),
                   jax.ShapeDtypeStruct((tp + SUBLANES, LANES), I32),
                   jax.ShapeDtypeStruct((nt, LANES), I32)],
        scratch_shapes=[pltpu.VMEM((HEAD_PAIRS, CHUNK, 2 * CHUNK), BF16),
                        pltpu.VMEM((TM, TM), BF16),
                        pltpu.VMEM((8, CONV_WIDTH), F32),
                        pltpu.VMEM((TM, D_MODEL), BF16),
                        pltpu.VMEM((2, TM, D_MODEL), BF16),
                        pltpu.VMEM(w_in.shape, BF16),
                        pltpu.VMEM(w_out.shape, BF16),
                        pltpu.VMEM(w_rt.shape, BF16),
                        pltpu.VMEM((nt + SUBLANES, LANES), F32)],
        compiler_params=pltpu.CompilerParams(dimension_semantics=("arbitrary",),
                                             vmem_limit_bytes=VMEM_LIMIT),
        name="mixer_router",
    )(x2d, *ins)


def _expert_kernel(meta_ref, src_ref, dst_ref, stage_ref,
                   wg_hbm, wu_hbm, wd_hbm, out_ref,
                   xbuf, obuf, wg_f, wu_f, wd_f, wg_s, wu_s, wd_s, sem, osem, wsem):
    i = pl.program_id(0)
    n_used = meta_ref[3]
    slot = i % N_SLOTS
    bpt = BLOCKS_PER_ROW_TILE
    dump_row = dst_ref.shape[0] // bpt - SUBLANES
    expert_of = lambda tile: meta_ref[tile * META_W]

    def gather_copy(seq, j):
        tile = jnp.minimum(seq, n_used - 1)
        src = pl.multiple_of(src_ref[tile * bpt + j] * BLOCK, BLOCK)
        slot_ = seq % N_SLOTS
        return pltpu.make_async_copy(stage_ref.at[pl.ds(src, BLOCK)],
                                     xbuf.at[slot_, pl.ds(j * BLOCK, BLOCK)], sem.at[slot_])

    def scatter_copy(tile, j):
        row = jnp.where(tile < 0, dump_row, tile)
        dst = pl.multiple_of(dst_ref[row * bpt + j] * BLOCK, BLOCK)
        slot_ = (tile + N_SLOTS) % N_SLOTS
        return pltpu.make_async_copy(obuf.at[slot_, pl.ds(j * BLOCK, BLOCK)],
                                     out_ref.at[pl.ds(dst, BLOCK)], osem.at[slot_])

    def each_block(fn):
        for j in range(BLOCKS_PER_ROW_TILE):
            fn(j)

    def weight_copies(e):
        return (pltpu.make_async_copy(wg_hbm.at[e], wg_f, wsem.at[0]),
                pltpu.make_async_copy(wu_hbm.at[e], wu_f, wsem.at[1]),
                pltpu.make_async_copy(wd_hbm.at[e], wd_f, wsem.at[2]))

    @pl.when(i == 0)
    def _first():
        each_block(lambda j: gather_copy(0, j).start())
        each_block(lambda j: gather_copy(1, j).start())
        for cp in weight_copies(expert_of(0)):
            cp.start(priority=WEIGHT_DMA_PRIORITY)
        obuf[...] = jnp.zeros_like(obuf)

    @pl.when((i >= 2) & (i - 3 < n_used))
    def _free_out_slot():
        each_block(lambda j: scatter_copy(i - 3, j).wait())

    @pl.when(i == n_used)
    def _after_last_tile():
        each_block(lambda j: gather_copy(n_used, j).wait())
        each_block(lambda j: gather_copy(n_used + 1, j).wait())
        each_block(lambda j: scatter_copy(n_used - 1, j).start())

    @pl.when(i < n_used)
    def _tile():
        e = expert_of(i)
        prev = expert_of(jnp.maximum(i - 1, 0))

        @pl.when((i == 0) | (e != prev))
        def _new_expert():
            for cp in weight_copies(e):
                cp.wait()
            wg_s[...] = wg_f[...].astype(BF16)
            wu_s[...] = wu_f[...].astype(BF16)
            wd_s[...] = wd_f[...].astype(BF16)
            nxt = meta_ref[i * META_W + 1]

            @pl.when(nxt >= 0)
            def _next_weights():
                for cp in weight_copies(nxt):
                    cp.start(priority=WEIGHT_DMA_PRIORITY)

        each_block(lambda j: gather_copy(i, j).wait())

        def mlp(rows):
            hb = xbuf[slot, 0:rows, :]
            a = jnp.dot(hb, wg_s[...], preferred_element_type=F32)
            b = jnp.dot(hb, wu_s[...], preferred_element_type=F32)
            each_block(lambda j: gather_copy(i + 2, j).start())
            each_block(lambda j: scatter_copy(i - 1, j).start())
            hid = (a * jax.nn.sigmoid(a) * b).astype(BF16)
            obuf[slot, 0:rows, :] = jnp.dot(hid, wd_s[...], preferred_element_type=F32).astype(BF16)

        rows_needed = meta_ref[i * META_W + 2]
        for rows in range(ROW_STEP, TR + 1, ROW_STEP):
            pl.when(rows_needed == rows)(functools.partial(mlp, rows))


def _experts(meta, src_block, dst_block, stage, w_gate, w_up, w_down, n_tiles):
    any_spec = pl.BlockSpec(memory_space=pl.ANY)
    return pl.pallas_call(
        _expert_kernel,
        grid_spec=pltpu.PrefetchScalarGridSpec(
            num_scalar_prefetch=3,
            grid=(n_tiles,),
            in_specs=[any_spec, any_spec, any_spec, any_spec],
            out_specs=any_spec,
            scratch_shapes=[pltpu.VMEM((N_SLOTS, TR, D_MODEL), BF16),
                            pltpu.VMEM((N_SLOTS, TR, D_MODEL), BF16),
                            pltpu.VMEM((D_MODEL, D_EXPERT), F32),
                            pltpu.VMEM((D_MODEL, D_EXPERT), F32),
                            pltpu.VMEM((D_EXPERT, D_MODEL), F32),
                            pltpu.VMEM((D_MODEL, D_EXPERT), BF16),
                            pltpu.VMEM((D_MODEL, D_EXPERT), BF16),
                            pltpu.VMEM((D_EXPERT, D_MODEL), BF16),
                            pltpu.SemaphoreType.DMA((N_SLOTS,)),
                            pltpu.SemaphoreType.DMA((N_SLOTS,)),
                            pltpu.SemaphoreType.DMA((3,))]),
        out_shape=jax.ShapeDtypeStruct(stage.shape, stage.dtype),
        input_output_aliases={3: 0},
        compiler_params=pltpu.CompilerParams(dimension_semantics=("arbitrary",),
                                             vmem_limit_bytes=VMEM_LIMIT),
        name="experts",
    )(meta, src_block, dst_block, stage, w_gate, w_up, w_down)


def _combine_kernel(used_ref, x1_ref, route_ref, p_ref, gple_ref, wpg_f32, wpp_f32, gfin_ref,
                    srt_ref, out_ref, y_s, wpg_ref, wpp_ref):
    i = pl.program_id(0)

    @pl.when(i == 0)
    def _init():
        wpg_ref[...] = wpg_f32[...].astype(BF16)
        wpp_ref[...] = wpp_f32[...].astype(BF16)

    route = route_ref[...]
    w1, w2, pos1, pos2 = (route[:, k:k + 1] for k in (2, 3, 4, 5))

    def unsort(lo, hi):
        col = (lax.broadcasted_iota(I32, (TM, hi - lo), 1) + lo).astype(F32)
        mat = jnp.where(col == pos1, w1, jnp.where(col == pos2, w2, 0.0)).astype(BF16)
        return jnp.dot(mat, srt_ref[lo:hi, :], preferred_element_type=F32)

    pe = jnp.dot(p_ref[...].astype(BF16), wpp_ref[...], preferred_element_type=F32)
    y_s[...] = unsort(0, SORT_MAIN)
    for lo in range(SORT_MAIN, KOUT, TAIL):
        @pl.when(used_ref[i] > lo)
        def _tail(lo=lo):
            y_s[...] += unsort(lo, lo + TAIL)

    x2 = x1_ref[...] + y_s[...]
    hg = _rms(x2, gple_ref[...]).astype(BF16)
    gate = jax.nn.sigmoid(jnp.dot(hg, wpg_ref[...], preferred_element_type=F32))
    x3 = x2 + gate * pe
    out_ref[...] = _rms(x3, gfin_ref[...])


def _combine(used, x1, route, p2d, g_ple, w_pg, w_pp, g_fin, sorted_out):
    t = p2d.shape[0]
    full = lambda a: pl.BlockSpec(a.shape, lambda i, *_: (0,) * a.ndim)
    return pl.pallas_call(
        _combine_kernel,
        grid_spec=pltpu.PrefetchScalarGridSpec(
            num_scalar_prefetch=1,
            grid=(t // TM,),
            in_specs=[pl.BlockSpec((TM, D_MODEL), lambda i, *_: (i, 0)),
                      pl.BlockSpec((TM, LANES), lambda i, *_: (i, 0)),
                      pl.BlockSpec((TM, PLE_DIM), lambda i, *_: (i, 0)),
                      full(g_ple), full(w_pg), full(w_pp), full(g_fin),
                      pl.BlockSpec((KOUT, D_MODEL), lambda i, *_: (i, 0))],
            out_specs=pl.BlockSpec((TM, D_MODEL), lambda i, *_: (i, 0)),
            scratch_shapes=[pltpu.VMEM((TM, D_MODEL), F32),
                            pltpu.VMEM(w_pg.shape, BF16),
                            pltpu.VMEM(w_pp.shape, BF16)]),
        out_shape=jax.ShapeDtypeStruct((t, D_MODEL), F32),
        compiler_params=pltpu.CompilerParams(dimension_semantics=("arbitrary",),
                                             vmem_limit_bytes=VMEM_LIMIT),
        name="combine_ple",
    )(used, x1, route, p2d, g_ple, w_pg, w_pp, g_fin, sorted_out)


def kernel(x, p, g_mix, w_in, w_conv, g_sgu, w_spatial, b_spatial, w_out, g_ffn, w_group, b_group,
           w_router, b_router, w_gate, w_up, w_down, g_ple, w_ple_gate, w_ple_proj, g_final):
    bsz, seq, d = x.shape
    t = bsz * seq
    assert w_in.shape[0] == 1, "single-layer block"
    assert d == D_MODEL and seq % TM == 0 and TM % CHUNK == 0
    nt = t // TM
    max_rows = TOP_K * t + nt * N_EXPERTS * (BLOCK - 1)
    n_tiles = (max_rows + N_EXPERTS * (TR - 1)) // TR + N_SLOTS
    l = 0

    w_rt = jnp.concatenate(
        [w_group[l], jnp.transpose(w_router[l], (1, 0, 2)).reshape(d, N_EXPERTS)], axis=1)
    w_rt = jnp.pad(w_rt, ((0, 0), (0, LANES - w_rt.shape[1]))).T
    b_rt = jnp.pad(jnp.concatenate([b_group[l], b_router[l].reshape(-1)]),
                   (0, LANES - N_GROUPS - N_EXPERTS)).reshape(LANES, 1)
    b_sp = jnp.repeat(b_spatial[l].T, SGU_WIDTH // SGU_HEADS, axis=1)

    x1, route, stage, meta, src, dst, used = _mixer_router(
        x.reshape(t, d), g_mix[l].reshape(1, d), w_in[l], w_conv[l],
        g_sgu[l].reshape(1, -1), w_spatial[l], b_sp, w_out[l],
        g_ffn[l].reshape(1, d), w_rt, b_rt, seq, n_tiles)

    bpt = BLOCKS_PER_ROW_TILE
    meta, src, dst, used = (meta[:, :META_W].reshape(-1), src[:, :bpt].reshape(-1),
                            dst[:, :bpt].reshape(-1), used[:, 0])
    sorted_out = _experts(meta, src, dst, stage,
                          w_gate[l].reshape(N_EXPERTS, d, D_EXPERT),
                          w_up[l].reshape(N_EXPERTS, d, D_EXPERT),
                          w_down[l].reshape(N_EXPERTS, D_EXPERT, d), n_tiles)
    out = _combine(used, x1, route, p[l].reshape(t, PLE_DIM), g_ple[l].reshape(1, d),
                   w_ple_gate[l], w_ple_proj[l], g_final.reshape(1, d), sorted_out)
    return out.reshape(bsz, seq, d)
```

```python
import functools

import jax
import jax.numpy as jnp
from jax import lax
from jax.experimental import pallas as pl
from jax.experimental.pallas import tpu as pltpu

F32 = jnp.float32
BF16 = jnp.bfloat16
I32 = jnp.int32

EPS = 1e-6
D_MODEL = 1024
CONV_WIDTH = 512
SGU_WIDTH = 512
SGU_HEADS = 8
HEAD_PAIRS = SGU_HEADS // 2
CHUNK = 128
N_GROUPS = 4
EXPERTS_PER_GROUP = 8
N_EXPERTS = N_GROUPS * EXPERTS_PER_GROUP
TOP_K = 2
D_EXPERT = 512
PLE_DIM = 256
LANES = 128
SUBLANES = 8
BLOCK = 2 * SUBLANES
ROUTE_LANE0 = N_GROUPS
ROUTE_ROWS = 40

TM = 512
TR = 512
BLOCKS_PER_ROW_TILE = TR // BLOCK
KOUT = TOP_K * TM + N_EXPERTS * BLOCK
TAIL = 128
SORT_MAIN = KOUT - 2 * TAIL
META_W = 4
ROW_STEP = 128
N_SLOTS = 3
DUMP_BLOCKS = N_SLOTS * BLOCKS_PER_ROW_TILE
assert DUMP_BLOCKS * BLOCK <= KOUT
WEIGHT_DMA_PRIORITY = 1
VMEM_LIMIT = 56 * 1024 * 1024


def _rms(x, g):
    return x * lax.rsqrt(jnp.mean(x * x, axis=-1, keepdims=True) + EPS) * g


def _build_tables(n_s, nt, meta_ref, src_ref, dst_ref, used_ref):
    tp = meta_ref.shape[0]
    bpt = BLOCKS_PER_ROW_TILE
    kb = KOUT // BLOCK
    n = n_s[0:nt, :]
    lane_n = lax.broadcasted_iota(I32, (nt, LANES), 1)
    is_e = (lane_n >= ROUTE_LANE0) & (lane_n < ROUTE_LANE0 + N_EXPERTS)
    nblk = jnp.where(is_e, jnp.floor((n + (BLOCK - 1)) * (1.0 / BLOCK)), 0.0)
    er = lax.broadcasted_iota(I32, (LANES, LANES), 0)
    ec = lax.broadcasted_iota(I32, (LANES, LANES), 1)
    loc = jnp.dot(nblk.astype(BF16), (er < ec).astype(BF16), preferred_element_type=F32)
    ti = lax.broadcasted_iota(I32, (nt, nt), 0)
    tj = lax.broadcasted_iota(I32, (nt, nt), 1)
    carry = jnp.dot((tj < ti).astype(BF16), nblk.astype(BF16), preferred_element_type=F32)
    total = jnp.sum(nblk, axis=0, keepdims=True)
    tiles_e = jnp.floor((total + (bpt - 1)) * (1.0 / bpt))
    tile_end = jnp.dot(jnp.broadcast_to(tiles_e, (SUBLANES, LANES)).astype(BF16),
                       (er <= ec).astype(BF16), preferred_element_type=F32)[0:1, :]
    tile0 = tile_end - tiles_e
    n_used = jnp.max(tile_end, axis=1, keepdims=True)

    tau = lax.broadcasted_iota(I32, (tp, LANES), 0).astype(F32)
    lane_i = lax.broadcasted_iota(I32, (tp, LANES), 1)
    lane_t = lane_i.astype(F32)
    is_e_t = (lane_i >= ROUTE_LANE0) & (lane_i < ROUTE_LANE0 + N_EXPERTS)
    te = jnp.sum(jnp.where(is_e_t & (tile_end <= jnp.minimum(tau, n_used - 1.0)), 1.0, 0.0),
                 axis=1, keepdims=True)
    oh = lane_t == te + ROUTE_LANE0
    pick = lambda row: jnp.sum(jnp.where(oh, row, 0.0), axis=1, keepdims=True)
    seg0_t = pick(tile0) * bpt
    e_lane = lane_t - ROUTE_LANE0
    cand = jnp.where(is_e_t & (e_lane > te) & (tiles_e > 0.0), e_lane, 99.0)
    nxt = jnp.min(cand, axis=1, keepdims=True)
    nxt = jnp.where(nxt == 99.0, -1.0, nxt)
    rows_t = jnp.clip(pick(total) * BLOCK - (tau[:, 0:1] - pick(tile0)) * TR, ROW_STEP, TR)
    rows_t = jnp.ceil(rows_t * (1.0 / ROW_STEP)) * ROW_STEP
    meta = jnp.where(lane_i == 0, te, 0.0)
    meta = jnp.where(lane_i == 1, nxt, meta)
    meta = jnp.where(lane_i == 2, rows_t, meta)
    meta = jnp.where(lane_i == 3, n_used, meta)
    meta_ref[...] = meta.astype(I32)

    bidx = tau * bpt + lane_t
    off = bidx - seg0_t
    ohb = jnp.where(oh, 1.0, 0.0).astype(BF16)

    def per_tile(tab):
        hi = jnp.floor(tab * (1.0 / 32))
        lo = tab - 32.0 * hi
        nt_dot = lambda a: lax.dot_general(ohb, a.astype(BF16), (((1,), (1,)), ((), ())),
                                           preferred_element_type=F32)
        return 32.0 * nt_dot(hi) + nt_dot(lo)

    run_end = per_tile(carry + nblk)
    run_beg = per_tile(carry)
    run_loc = per_tile(loc)
    tile_of = jnp.zeros((tp, LANES), F32)
    for it in range(nt):
        tile_of = tile_of + jnp.where(run_end[:, it:it + 1] <= off, 1.0, 0.0)
    srcv = off
    for it in range(nt):
        srcv = srcv + jnp.where(tile_of == it,
                                run_loc[:, it:it + 1] - run_beg[:, it:it + 1] + it * kb, 0.0)
    valid = (tile_of < nt) & (bidx < n_used * bpt)
    slot3 = tau - N_SLOTS * jnp.floor((tau + 0.5) * (1.0 / N_SLOTS))
    dump = nt * kb + slot3 * bpt + lane_t
    src_ref[...] = jnp.where(valid, srcv, kb - 1.0).astype(I32)
    dst_ref[0:tp, :] = jnp.where(valid, srcv, dump).astype(I32)
    dst_ref[tp:tp + SUBLANES, :] = (nt * kb + DUMP_BLOCKS - bpt
                                    + lax.broadcasted_iota(I32, (SUBLANES, LANES), 1))
    used_ref[...] = jnp.broadcast_to(jnp.sum(nblk, axis=1, keepdims=True) * BLOCK,
                                     (nt, LANES)).astype(I32)


def _mixer_router_kernel(tiles_per_seq, nt,
                         x_ref, gmix_ref, win_f32, wconv_ref, gsgu_ref, wsp_ref, bsp_ref,
                         wout_f32, gffn_ref, wrt_f32, brt_ref,
                         x1_ref, route_ref, stage_ref, meta_ref, src_ref, dst_ref, used_ref,
                         wcat_s, tri_s, halo_s, mix_s, h2_s, win_ref, wout_ref, wrt_ref, n_s):
    i = pl.program_id(0)
    nch = TM // CHUNK

    @pl.when(i == 0)
    def _init():
        r = lax.broadcasted_iota(I32, (CHUNK, CHUNK), 0)
        c = lax.broadcasted_iota(I32, (CHUNK, CHUNK), 1)
        causal = c <= r
        for j in range(HEAD_PAIRS):
            wa = jnp.where(causal, wsp_ref[2 * j], 0.0)
            wb = jnp.where(causal, wsp_ref[2 * j + 1], 0.0)
            wcat_s[j] = jnp.concatenate([wa, wb], axis=1).astype(BF16)
        rr = lax.broadcasted_iota(I32, (TM, TM), 0)
        cc = lax.broadcasted_iota(I32, (TM, TM), 1)
        tri_s[...] = (rr < cc).astype(BF16)
        h2_s[...] = jnp.zeros_like(h2_s)
        win_ref[...] = win_f32[...].astype(BF16)
        wout_ref[...] = wout_f32[...].astype(BF16)
        wrt_ref[...] = wrt_f32[...].astype(BF16)

    @pl.when(i % tiles_per_seq == 0)
    def _seq_start():
        halo_s[...] = jnp.zeros_like(halo_s)

    x = x_ref[...]
    hb = _rms(x, gmix_ref[...]).astype(BF16)
    h2b = h2_s[(i + 1) % 2]

    def proj(k):
        return jnp.dot(hb, win_ref[:, k * 512:(k + 1) * 512], preferred_element_type=F32)

    logits = lax.dot_general(wrt_ref[...], h2b, (((1,), (1,)), ((), ())),
                             preferred_element_type=F32)[0:ROUTE_ROWS, :] + brt_ref[0:ROUTE_ROWS, :]
    pu = proj(3)
    pv = proj(4)

    ridx = lax.broadcasted_iota(I32, (ROUTE_ROWS, TM), 0).astype(F32)
    neg = jnp.float32(-jnp.inf)
    big = jnp.float32(1e9)
    is_g = ridx < N_GROUPS
    gl = jnp.where(is_g, logits, neg)
    gmax = jnp.max(gl, axis=0, keepdims=True)
    gsum = jnp.sum(jnp.where(is_g, jnp.exp(gl - gmax), 0.0), axis=0, keepdims=True)
    g_w = 1.0 / gsum
    g_idx = jnp.min(jnp.where(gl == gmax, ridx, big), axis=0, keepdims=True)
    lo_row = ROUTE_LANE0 + EXPERTS_PER_GROUP * g_idx
    in_grp = (ridx >= lo_row) & (ridx < lo_row + EXPERTS_PER_GROUP)
    el = jnp.where(in_grp, logits, neg)
    v1 = jnp.max(el, axis=0, keepdims=True)
    i1 = jnp.min(jnp.where(el == v1, ridx, big), axis=0, keepdims=True)
    el2 = jnp.where(ridx == i1, neg, el)
    v2 = jnp.max(el2, axis=0, keepdims=True)
    i2 = jnp.min(jnp.where(el2 == v2, ridx, big), axis=0, keepdims=True)
    e21 = jnp.exp(v2 - v1)
    w1 = g_w / (1.0 + e21)
    w2 = g_w * e21 / (1.0 + e21)

    u = jax.nn.gelu(pu)
    zc = proj(1) * proj(2)
    v = jax.nn.gelu(pv)
    vc = v - jnp.mean(v, axis=-1, keepdims=True)
    vn = vc * lax.rsqrt(jnp.mean(vc * vc, axis=-1, keepdims=True) + EPS) * gsgu_ref[...]
    vnb = vn.astype(BF16)

    row = lax.broadcasted_iota(I32, (TM, CONV_WIDTH), 0)
    h6 = halo_s[6:7, :]
    h7 = halo_s[7:8, :]
    z1 = jnp.where(row == 0, h7, pltpu.roll(zc, 1, 0))
    z2 = jnp.where(row == 0, h6, jnp.where(row == 1, h7, pltpu.roll(zc, 2, 0)))
    conv = z2 * wconv_ref[0:1, :] + z1 * wconv_ref[1:2, :] + zc * wconv_ref[2:3, :]
    halo_s[...] = zc[TM - 8:TM, :]
    mix_s[:, 0:CONV_WIDTH] = (proj(0) * conv).astype(BF16)

    sel1 = ridx == i1
    sel2 = ridx == i2
    onehot = jnp.where(sel1 | sel2, 1.0, 0.0)
    counts = jnp.sum(onehot, axis=1, keepdims=True)
    pad_rows = jnp.zeros((LANES - ROUTE_ROWS, LANES), F32)
    counts_sq = jnp.concatenate([jnp.broadcast_to(counts, (ROUTE_ROWS, LANES)), pad_rows], axis=0)
    blocks_sq = jnp.ceil(counts_sq * (1.0 / BLOCK))
    er = lax.broadcasted_iota(I32, (LANES, LANES), 0)
    ec = lax.broadcasted_iota(I32, (LANES, LANES), 1)
    before = (ec < er).astype(BF16)
    run_start = BLOCK * jnp.dot(before, blocks_sq.astype(BF16),
                                preferred_element_type=F32)[0:ROUTE_ROWS, 0:1]
    rank = jnp.dot(onehot.astype(BF16), tri_s[...], preferred_element_type=F32) + run_start
    pos1_row = jnp.sum(jnp.where(sel1, rank, 0.0), axis=0, keepdims=True)
    pos2_row = jnp.sum(jnp.where(sel2, rank, 0.0), axis=0, keepdims=True)
    n_s[pl.ds(jnp.where(i == 0, nt, i - 1), 1), :] = counts_sq.T[0:1, :]

    r8 = lax.broadcasted_iota(I32, (SUBLANES, TM), 0)
    rows8 = jnp.where(r8 == 0, i1 - ROUTE_LANE0, 0.0)
    rows8 = jnp.where(r8 == 1, i2 - ROUTE_LANE0, rows8)
    rows8 = jnp.where(r8 == 2, w1, rows8)
    rows8 = jnp.where(r8 == 3, w2, rows8)
    rows8 = jnp.where(r8 == 4, pos1_row, rows8)
    rows8 = jnp.where(r8 == 5, pos2_row, rows8)
    route_ref[...] = jnp.concatenate(
        [rows8, jnp.zeros((LANES - SUBLANES, TM), F32)], axis=0).T

    def sort_rows(lo, hi):
        out_row = (lax.broadcasted_iota(I32, (hi - lo, TM), 0) + lo).astype(F32)
        sort_mat = jnp.where((out_row == pos1_row) | (out_row == pos2_row), 1.0, 0.0).astype(BF16)
        stage_ref[lo:hi, :] = jnp.dot(sort_mat, h2b, preferred_element_type=F32).astype(BF16)

    sort_rows(0, SORT_MAIN // 2)
    left = lax.broadcasted_iota(I32, (CHUNK, LANES), 1) < (LANES // 2)
    zero = jnp.zeros((CHUNK, LANES), BF16)
    for j in range(HEAD_PAIRS):
        cols = []
        for c in range(nch):
            s = vnb[c * CHUNK:(c + 1) * CHUNK, j * LANES:(j + 1) * LANES]
            cols.append(jnp.concatenate([jnp.where(left, s, zero), jnp.where(left, zero, s)], axis=0))
        rhs = jnp.concatenate(cols, axis=1)
        res = jnp.dot(wcat_s[j], rhs, preferred_element_type=F32)
        bias = bsp_ref[:, j * LANES:(j + 1) * LANES]
        for c in range(nch):
            mixed = res[:, c * LANES:(c + 1) * LANES] + bias
            uu = u[c * CHUNK:(c + 1) * CHUNK, j * LANES:(j + 1) * LANES]
            mix_s[c * CHUNK:(c + 1) * CHUNK,
                  CONV_WIDTH + j * LANES:CONV_WIDTH + (j + 1) * LANES] = (uu * mixed).astype(BF16)

    sort_rows(SORT_MAIN // 2, SORT_MAIN)
    x1 = x + jnp.dot(mix_s[...], wout_ref[...], preferred_element_type=F32)
    x1_ref[...] = x1
    h2_s[i % 2] = _rms(x1, gffn_ref[...]).astype(BF16)

    used_rows = BLOCK * jnp.sum(blocks_sq[:, 0:1])
    for lo in range(SORT_MAIN, KOUT, TAIL):
        @pl.when(used_rows > lo)
        def _tail(lo=lo):
            sort_rows(lo, lo + TAIL)

        @pl.when(used_rows <= lo)
        def _empty_tail(lo=lo):
            stage_ref[lo:lo + TAIL, :] = jnp.zeros((TAIL, D_MODEL), BF16)

    @pl.when(i == nt)
    def _tables():
        _build_tables(n_s, nt, meta_ref, src_ref, dst_ref, used_ref)


def _mixer_router(x2d, g_mix, w_in, w_conv, g_sgu, w_sp, b_sp, w_out, g_ffn, w_rt, b_rt, seq,
                  n_tiles):
    t = x2d.shape[0]
    nt = t // TM
    tp = -(-n_tiles // SUBLANES) * SUBLANES
    whole = lambda r: pl.BlockSpec((r, LANES), lambda i: (0, 0))
    full = lambda a: pl.BlockSpec(a.shape, lambda i: (0,) * a.ndim)
    ins = (g_mix, w_in, w_conv, g_sgu, w_sp, b_sp, w_out, g_ffn, w_rt, b_rt)
    prev = lambda i: (jnp.where(i == 0, nt, i - 1), 0)
    return pl.pallas_call(
        functools.partial(_mixer_router_kernel, seq // TM, nt),
        grid=(nt + 1,),
        in_specs=[pl.BlockSpec((TM, D_MODEL), lambda i: (jnp.minimum(i, nt - 1), 0))]
        + [full(a) for a in ins],
        out_specs=[pl.BlockSpec((TM, D_MODEL), lambda i: (i, 0)),
                   pl.BlockSpec((TM, LANES), prev),
                   pl.BlockSpec((KOUT, D_MODEL), prev),
                   whole(tp), whole(tp), whole(tp + SUBLANES), whole(nt)],
        out_shape=[jax.ShapeDtypeStruct(((nt + 1) * TM, D_MODEL), F32),
                   jax.ShapeDtypeStruct(((nt + 1) * TM, LANES), F32),
                   jax.ShapeDtypeStruct(((nt + 1) * KOUT, D_MODEL), BF16),
                   jax.ShapeDtypeStruct((tp, LANES), I32),
                   jax.ShapeDtypeStruct((tp, LANES), I32),
                   jax.ShapeDtypeStruct((tp + SUBLANES, LANES), I32),
                   jax.ShapeDtypeStruct((nt, LANES), I32)],
        scratch_shapes=[pltpu.VMEM((HEAD_PAIRS, CHUNK, 2 * CHUNK), BF16),
                        pltpu.VMEM((TM, TM), BF16),
                        pltpu.VMEM((8, CONV_WIDTH), F32),
                        pltpu.VMEM((TM, D_MODEL), BF16),
                        pltpu.VMEM((2, TM, D_MODEL), BF16),
                        pltpu.VMEM(w_in.shape, BF16),
                        pltpu.VMEM(w_out.shape, BF16),
                        pltpu.VMEM(w_rt.shape, BF16),
                        pltpu.VMEM((nt + SUBLANES, LANES), F32)],
        compiler_params=pltpu.CompilerParams(dimension_semantics=("arbitrary",),
                                             vmem_limit_bytes=VMEM_LIMIT),
        name="mixer_router",
    )(x2d, *ins)


def _expert_kernel(meta_ref, src_ref, dst_ref, stage_ref,
                   wg_hbm, wu_hbm, wd_hbm, out_ref,
                   xbuf, obuf, wg_f, wu_f, wd_f, wg_s, wu_s, wd_s, sem, osem, wsem):
    i = pl.program_id(0)
    n_used = meta_ref[3]
    slot = i % N_SLOTS
    bpt = BLOCKS_PER_ROW_TILE
    dump_row = dst_ref.shape[0] // bpt - SUBLANES
    expert_of = lambda tile: meta_ref[tile * META_W]

    def gather_copy(seq, j):
        tile = jnp.minimum(seq, n_used - 1)
        src = pl.multiple_of(src_ref[tile * bpt + j] * BLOCK, BLOCK)
        slot_ = seq % N_SLOTS
        return pltpu.make_async_copy(stage_ref.at[pl.ds(src, BLOCK)],
                                     xbuf.at[slot_, pl.ds(j * BLOCK, BLOCK)], sem.at[slot_])

    def scatter_copy(tile, j):
        row = jnp.where(tile < 0, dump_row, tile)
        dst = pl.multiple_of(dst_ref[row * bpt + j] * BLOCK, BLOCK)
        slot_ = (tile + N_SLOTS) % N_SLOTS
        return pltpu.make_async_copy(obuf.at[slot_, pl.ds(j * BLOCK, BLOCK)],
                                     out_ref.at[pl.ds(dst, BLOCK)], osem.at[slot_])

    def each_block(fn):
        for j in range(BLOCKS_PER_ROW_TILE):
            fn(j)

    def weight_copies(e):
        return (pltpu.make_async_copy(wg_hbm.at[e], wg_f, wsem.at[0]),
                pltpu.make_async_copy(wu_hbm.at[e], wu_f, wsem.at[1]),
                pltpu.make_async_copy(wd_hbm.at[e], wd_f, wsem.at[2]))

    @pl.when(i == 0)
    def _first():
        each_block(lambda j: gather_copy(0, j).start())
        each_block(lambda j: gather_copy(1, j).start())
        for cp in weight_copies(expert_of(0)):
            cp.start(priority=WEIGHT_DMA_PRIORITY)
        obuf[...] = jnp.zeros_like(obuf)

    @pl.when((i >= 2) & (i - 3 < n_used))
    def _free_out_slot():
        each_block(lambda j: scatter_copy(i - 3, j).wait())

    @pl.when(i == n_used)
    def _after_last_tile():
        each_block(lambda j: gather_copy(n_used, j).wait())
        each_block(lambda j: gather_copy(n_used + 1, j).wait())
        each_block(lambda j: scatter_copy(n_used - 1, j).start())

    @pl.when(i < n_used)
    def _tile():
        e = expert_of(i)
        prev = expert_of(jnp.maximum(i - 1, 0))

        @pl.when((i == 0) | (e != prev))
        def _new_expert():
            for cp in weight_copies(e):
                cp.wait()
            wg_s[...] = wg_f[...].astype(BF16)
            wu_s[...] = wu_f[...].astype(BF16)
            wd_s[...] = wd_f[...].astype(BF16)
            nxt = meta_ref[i * META_W + 1]

            @pl.when(nxt >= 0)
            def _next_weights():
                for cp in weight_copies(nxt):
                    cp.start(priority=WEIGHT_DMA_PRIORITY)

        each_block(lambda j: gather_copy(i, j).wait())

        def mlp(rows):
            hb = xbuf[slot, 0:rows, :]
            a = jnp.dot(hb, wg_s[...], preferred_element_type=F32)
            b = jnp.dot(hb, wu_s[...], preferred_element_type=F32)
            each_block(lambda j: gather_copy(i + 2, j).start())
            each_block(lambda j: scatter_copy(i - 1, j).start())
            hid = (a * jax.nn.sigmoid(a) * b).astype(BF16)
            obuf[slot, 0:rows, :] = jnp.dot(hid, wd_s[...], preferred_element_type=F32).astype(BF16)

        rows_needed = meta_ref[i * META_W + 2]
        for rows in range(ROW_STEP, TR + 1, ROW_STEP):
            pl.when(rows_needed == rows)(functools.partial(mlp, rows))


def _experts(meta, src_block, dst_block, stage, w_gate, w_up, w_down, n_tiles):
    any_spec = pl.BlockSpec(memory_space=pl.ANY)
    return pl.pallas_call(
        _expert_kernel,
        grid_spec=pltpu.PrefetchScalarGridSpec(
            num_scalar_prefetch=3,
            grid=(n_tiles,),
            in_specs=[any_spec, any_spec, any_spec, any_spec],
            out_specs=any_spec,
            scratch_shapes=[pltpu.VMEM((N_SLOTS, TR, D_MODEL), BF16),
                            pltpu.VMEM((N_SLOTS, TR, D_MODEL), BF16),
                            pltpu.VMEM((D_MODEL, D_EXPERT), F32),
                            pltpu.VMEM((D_MODEL, D_EXPERT), F32),
                            pltpu.VMEM((D_EXPERT, D_MODEL), F32),
                            pltpu.VMEM((D_MODEL, D_EXPERT), BF16),
                            pltpu.VMEM((D_MODEL, D_EXPERT), BF16),
                            pltpu.VMEM((D_EXPERT, D_MODEL), BF16),
                            pltpu.SemaphoreType.DMA((N_SLOTS,)),
                            pltpu.SemaphoreType.DMA((N_SLOTS,)),
                            pltpu.SemaphoreType.DMA((3,))]),
        out_shape=jax.ShapeDtypeStruct(stage.shape, stage.dtype),
        input_output_aliases={3: 0},
        compiler_params=pltpu.CompilerParams(dimension_semantics=("arbitrary",),
                                             vmem_limit_bytes=VMEM_LIMIT),
        name="experts",
    )(meta, src_block, dst_block, stage, w_gate, w_up, w_down)


def _combine_kernel(used_ref, x1_ref, route_ref, p_ref, gple_ref, wpg_f32, wpp_f32, gfin_ref,
                    srt_ref, out_ref, y_s, wpg_ref, wpp_ref):
    i = pl.program_id(0)

    @pl.when(i == 0)
    def _init():
        wpg_ref[...] = wpg_f32[...].astype(BF16)
        wpp_ref[...] = wpp_f32[...].astype(BF16)

    route = route_ref[...]
    w1, w2, pos1, pos2 = (route[:, k:k + 1] for k in (2, 3, 4, 5))

    def unsort(lo, hi):
        col = (lax.broadcasted_iota(I32, (TM, hi - lo), 1) + lo).astype(F32)
        mat = jnp.where(col == pos1, w1, jnp.where(col == pos2, w2, 0.0)).astype(BF16)
        return jnp.dot(mat, srt_ref[lo:hi, :], preferred_element_type=F32)

    def finish(y):
        pe = jnp.dot(p_ref[...].astype(BF16), wpp_ref[...], preferred_element_type=F32)
        hrows = TM // 2
        for h in range(2):
            rs = slice(h * hrows, (h + 1) * hrows)
            x2 = x1_ref[rs, :] + y[rs, :]
            hg = _rms(x2, gple_ref[...]).astype(BF16)
            gate = jax.nn.sigmoid(jnp.dot(hg, wpg_ref[...], preferred_element_type=F32))
            x3 = x2 + gate * pe[rs, :]
            out_ref[rs, :] = _rms(x3, gfin_ref[...])

    @pl.when(used_ref[i] <= SORT_MAIN)
    def _common():
        finish(unsort(0, SORT_MAIN))

    @pl.when(used_ref[i] > SORT_MAIN)
    def _with_tails():
        y_s[...] = unsort(0, SORT_MAIN)
        for lo in range(SORT_MAIN, KOUT, TAIL):
            @pl.when(used_ref[i] > lo)
            def _tail(lo=lo):
                y_s[...] += unsort(lo, lo + TAIL)
        finish(y_s[...])


def _combine(used, x1, route, p2d, g_ple, w_pg, w_pp, g_fin, sorted_out):
    t = p2d.shape[0]
    full = lambda a: pl.BlockSpec(a.shape, lambda i, *_: (0,) * a.ndim)
    return pl.pallas_call(
        _combine_kernel,
        grid_spec=pltpu.PrefetchScalarGridSpec(
            num_scalar_prefetch=1,
            grid=(t // TM,),
            in_specs=[pl.BlockSpec((TM, D_MODEL), lambda i, *_: (i, 0)),
                      pl.BlockSpec((TM, LANES), lambda i, *_: (i, 0)),
                      pl.BlockSpec((TM, PLE_DIM), lambda i, *_: (i, 0)),
                      full(g_ple), full(w_pg), full(w_pp), full(g_fin),
                      pl.BlockSpec((KOUT, D_MODEL), lambda i, *_: (i, 0))],
            out_specs=pl.BlockSpec((TM, D_MODEL), lambda i, *_: (i, 0)),
            scratch_shapes=[pltpu.VMEM((TM, D_MODEL), F32),
                            pltpu.VMEM(w_pg.shape, BF16),
                            pltpu.VMEM(w_pp.shape, BF16)]),
        out_shape=jax.ShapeDtypeStruct((t, D_MODEL), F32),
        compiler_params=pltpu.CompilerParams(dimension_semantics=("arbitrary",),
                                             vmem_limit_bytes=VMEM_LIMIT),
        name="combine_ple",
    )(used, x1, route, p2d, g_ple, w_pg, w_pp, g_fin, sorted_out)


def kernel(x, p, g_mix, w_in, w_conv, g_sgu, w_spatial, b_spatial, w_out, g_ffn, w_group, b_group,
           w_router, b_router, w_gate, w_up, w_down, g_ple, w_ple_gate, w_ple_proj, g_final):
    bsz, seq, d = x.shape
    t = bsz * seq
    assert w_in.shape[0] == 1, "single-layer block"
    assert d == D_MODEL and seq % TM == 0 and TM % CHUNK == 0
    nt = t // TM
    max_rows = TOP_K * t + nt * N_EXPERTS * (BLOCK - 1)
    n_tiles = (max_rows + N_EXPERTS * (TR - 1)) // TR + N_SLOTS
    l = 0

    w_rt = jnp.concatenate(
        [w_group[l], jnp.transpose(w_router[l], (1, 0, 2)).reshape(d, N_EXPERTS)], axis=1)
    w_rt = jnp.pad(w_rt, ((0, 0), (0, LANES - w_rt.shape[1]))).T
    b_rt = jnp.pad(jnp.concatenate([b_group[l], b_router[l].reshape(-1)]),
                   (0, LANES - N_GROUPS - N_EXPERTS)).reshape(LANES, 1)
    b_sp = jnp.repeat(b_spatial[l].T, SGU_WIDTH // SGU_HEADS, axis=1)

    x1, route, stage, meta, src, dst, used = _mixer_router(
        x.reshape(t, d), g_mix[l].reshape(1, d), w_in[l], w_conv[l],
        g_sgu[l].reshape(1, -1), w_spatial[l], b_sp, w_out[l],
        g_ffn[l].reshape(1, d), w_rt, b_rt, seq, n_tiles)

    bpt = BLOCKS_PER_ROW_TILE
    meta, src, dst, used = (meta[:, :META_W].reshape(-1), src[:, :bpt].reshape(-1),
                            dst[:, :bpt].reshape(-1), used[:, 0])
    sorted_out = _experts(meta, src, dst, stage,
                          w_gate[l].reshape(N_EXPERTS, d, D_EXPERT),
                          w_up[l].reshape(N_EXPERTS, d, D_EXPERT),
                          w_down[l].reshape(N_EXPERTS, D_EXPERT, d), n_tiles)
    out = _combine(used, x1, route, p[l].reshape(t, PLE_DIM), g_ple[l].reshape(1, d),
                   w_ple_gate[l], w_ple_proj[l], g_final.reshape(1, d), sorted_out)
    return out.reshape(bsz, seq, d)
```

```python
import functools

import jax
import jax.numpy as jnp
from jax import lax
from jax.experimental import pallas as pl
from jax.experimental.pallas import tpu as pltpu

F32 = jnp.float32
BF16 = jnp.bfloat16
I32 = jnp.int32

EPS = 1e-6
D_MODEL = 1024
CONV_WIDTH = 512
SGU_WIDTH = 512
SGU_HEADS = 8
HEAD_PAIRS = SGU_HEADS // 2
CHUNK = 128
N_GROUPS = 4
EXPERTS_PER_GROUP = 8
N_EXPERTS = N_GROUPS * EXPERTS_PER_GROUP
TOP_K = 2
D_EXPERT = 512
PLE_DIM = 256
LANES = 128
SUBLANES = 8
BLOCK = 2 * SUBLANES
ROUTE_LANE0 = N_GROUPS
ROUTE_ROWS = 40

TM = 512
TR = 512
BLOCKS_PER_ROW_TILE = TR // BLOCK
KOUT = TOP_K * TM + N_EXPERTS * BLOCK
TAIL = 128
SORT_MAIN = KOUT - 2 * TAIL
META_W = 4
ROW_STEP = 128
N_SLOTS = 3
DUMP_BLOCKS = N_SLOTS * BLOCKS_PER_ROW_TILE
assert DUMP_BLOCKS * BLOCK <= KOUT
WEIGHT_DMA_PRIORITY = 1
VMEM_LIMIT = 56 * 1024 * 1024


def _rms(x, g):
    return x * lax.rsqrt(jnp.mean(x * x, axis=-1, keepdims=True) + EPS) * g


def _build_tables(n_s, nt, meta_ref, src_ref, dst_ref, used_ref):
    tp = meta_ref.shape[0]
    bpt = BLOCKS_PER_ROW_TILE
    kb = KOUT // BLOCK
    n = n_s[0:nt, :]
    lane_n = lax.broadcasted_iota(I32, (nt, LANES), 1)
    is_e = (lane_n >= ROUTE_LANE0) & (lane_n < ROUTE_LANE0 + N_EXPERTS)
    nblk = jnp.where(is_e, jnp.floor((n + (BLOCK - 1)) * (1.0 / BLOCK)), 0.0)
    er = lax.broadcasted_iota(I32, (LANES, LANES), 0)
    ec = lax.broadcasted_iota(I32, (LANES, LANES), 1)
    loc = jnp.dot(nblk.astype(BF16), (er < ec).astype(BF16), preferred_element_type=F32)
    ti = lax.broadcasted_iota(I32, (nt, nt), 0)
    tj = lax.broadcasted_iota(I32, (nt, nt), 1)
    carry = jnp.dot((tj < ti).astype(BF16), nblk.astype(BF16), preferred_element_type=F32)
    total = jnp.sum(nblk, axis=0, keepdims=True)
    tiles_e = jnp.floor((total + (bpt - 1)) * (1.0 / bpt))
    tile_end = jnp.dot(jnp.broadcast_to(tiles_e, (SUBLANES, LANES)).astype(BF16),
                       (er <= ec).astype(BF16), preferred_element_type=F32)[0:1, :]
    tile0 = tile_end - tiles_e
    n_used = jnp.max(tile_end, axis=1, keepdims=True)

    tau = lax.broadcasted_iota(I32, (tp, LANES), 0).astype(F32)
    lane_i = lax.broadcasted_iota(I32, (tp, LANES), 1)
    lane_t = lane_i.astype(F32)
    is_e_t = (lane_i >= ROUTE_LANE0) & (lane_i < ROUTE_LANE0 + N_EXPERTS)
    te = jnp.sum(jnp.where(is_e_t & (tile_end <= jnp.minimum(tau, n_used - 1.0)), 1.0, 0.0),
                 axis=1, keepdims=True)
    oh = lane_t == te + ROUTE_LANE0
    pick = lambda row: jnp.sum(jnp.where(oh, row, 0.0), axis=1, keepdims=True)
    seg0_t = pick(tile0) * bpt
    e_lane = lane_t - ROUTE_LANE0
    cand = jnp.where(is_e_t & (e_lane > te) & (tiles_e > 0.0), e_lane, 99.0)
    nxt = jnp.min(cand, axis=1, keepdims=True)
    nxt = jnp.where(nxt == 99.0, -1.0, nxt)
    rows_t = jnp.clip(pick(total) * BLOCK - (tau[:, 0:1] - pick(tile0)) * TR, ROW_STEP, TR)
    rows_t = jnp.ceil(rows_t * (1.0 / ROW_STEP)) * ROW_STEP
    meta = jnp.where(lane_i == 0, te, 0.0)
    meta = jnp.where(lane_i == 1, nxt, meta)
    meta = jnp.where(lane_i == 2, rows_t, meta)
    meta = jnp.where(lane_i == 3, n_used, meta)
    meta_ref[...] = meta.astype(I32)

    bidx = tau * bpt + lane_t
    off = bidx - seg0_t
    ohb = jnp.where(oh, 1.0, 0.0).astype(BF16)

    def per_tile(tab):
        hi = jnp.floor(tab * (1.0 / 32))
        lo = tab - 32.0 * hi
        nt_dot = lambda a: lax.dot_general(ohb, a.astype(BF16), (((1,), (1,)), ((), ())),
                                           preferred_element_type=F32)
        return 32.0 * nt_dot(hi) + nt_dot(lo)

    run_end = per_tile(carry + nblk)
    run_beg = per_tile(carry)
    run_loc = per_tile(loc)
    tile_of = jnp.zeros((tp, LANES), F32)
    for it in range(nt):
        tile_of = tile_of + jnp.where(run_end[:, it:it + 1] <= off, 1.0, 0.0)
    srcv = off
    for it in range(nt):
        srcv = srcv + jnp.where(tile_of == it,
                                run_loc[:, it:it + 1] - run_beg[:, it:it + 1] + it * kb, 0.0)
    valid = (tile_of < nt) & (bidx < n_used * bpt)
    slot3 = tau - N_SLOTS * jnp.floor((tau + 0.5) * (1.0 / N_SLOTS))
    dump = nt * kb + slot3 * bpt + lane_t
    src_ref[...] = jnp.where(valid, srcv, kb - 1.0).astype(I32)
    dst_ref[0:tp, :] = jnp.where(valid, srcv, dump).astype(I32)
    dst_ref[tp:tp + SUBLANES, :] = (nt * kb + DUMP_BLOCKS - bpt
                                    + lax.broadcasted_iota(I32, (SUBLANES, LANES), 1))
    used_ref[...] = jnp.broadcast_to(jnp.sum(nblk, axis=1, keepdims=True) * BLOCK,
                                     (nt, LANES)).astype(I32)


def _mixer_router_kernel(tiles_per_seq, nt,
                         x_ref, gmix_ref, win_f32, wconv_ref, gsgu_ref, wsp_ref, bsp_ref,
                         wout_f32, gffn_ref, wrt_f32, brt_ref,
                         x1_ref, route_ref, stage_ref, meta_ref, src_ref, dst_ref, used_ref,
                         wcat_s, tri_s, halo_s, mix_s, h2_s, win_ref, wout_ref, wrt_ref, n_s):
    i = pl.program_id(0)
    nch = TM // CHUNK

    @pl.when(i == 0)
    def _init():
        r = lax.broadcasted_iota(I32, (CHUNK, CHUNK), 0)
        c = lax.broadcasted_iota(I32, (CHUNK, CHUNK), 1)
        causal = c <= r
        for j in range(HEAD_PAIRS):
            wa = jnp.where(causal, wsp_ref[2 * j], 0.0)
            wb = jnp.where(causal, wsp_ref[2 * j + 1], 0.0)
            wcat_s[j] = jnp.concatenate([wa, wb], axis=1).astype(BF16)
        rr = lax.broadcasted_iota(I32, (TM, TM), 0)
        cc = lax.broadcasted_iota(I32, (TM, TM), 1)
        tri_s[...] = (rr < cc).astype(BF16)
        h2_s[...] = jnp.zeros_like(h2_s)
        win_ref[...] = win_f32[...].astype(BF16)
        wout_ref[...] = wout_f32[...].astype(BF16)
        wrt_ref[...] = wrt_f32[...].astype(BF16)

    @pl.when(i % tiles_per_seq == 0)
    def _seq_start():
        halo_s[...] = jnp.zeros_like(halo_s)

    def step(mix):
        h2b = h2_s[(i + 1) % 2]
        if mix:
            x = x_ref[...]
            hb = _rms(x, gmix_ref[...]).astype(BF16)

            def proj(k):
                return jnp.dot(hb, win_ref[:, k * 512:(k + 1) * 512], preferred_element_type=F32)

        logits = lax.dot_general(wrt_ref[...], h2b, (((1,), (1,)), ((), ())),
                                 preferred_element_type=F32)[0:ROUTE_ROWS, :] + brt_ref[0:ROUTE_ROWS, :]
        if mix:
            pu = proj(3)
            pv = proj(4)

        ridx = lax.broadcasted_iota(I32, (ROUTE_ROWS, TM), 0).astype(F32)
        neg = jnp.float32(-jnp.inf)
        big = jnp.float32(1e9)
        is_g = ridx < N_GROUPS
        gl = jnp.where(is_g, logits, neg)
        gmax = jnp.max(gl, axis=0, keepdims=True)
        gsum = jnp.sum(jnp.where(is_g, jnp.exp(gl - gmax), 0.0), axis=0, keepdims=True)
        g_w = 1.0 / gsum
        g_idx = jnp.min(jnp.where(gl == gmax, ridx, big), axis=0, keepdims=True)
        lo_row = ROUTE_LANE0 + EXPERTS_PER_GROUP * g_idx
        in_grp = (ridx >= lo_row) & (ridx < lo_row + EXPERTS_PER_GROUP)
        el = jnp.where(in_grp, logits, neg)
        v1 = jnp.max(el, axis=0, keepdims=True)
        i1 = jnp.min(jnp.where(el == v1, ridx, big), axis=0, keepdims=True)
        el2 = jnp.where(ridx == i1, neg, el)
        v2 = jnp.max(el2, axis=0, keepdims=True)
        i2 = jnp.min(jnp.where(el2 == v2, ridx, big), axis=0, keepdims=True)
        e21 = jnp.exp(v2 - v1)
        w1 = g_w / (1.0 + e21)
        w2 = g_w * e21 / (1.0 + e21)

        if mix:
            u = jax.nn.gelu(pu)
            zc = proj(1) * proj(2)
            v = jax.nn.gelu(pv)
            vc = v - jnp.mean(v, axis=-1, keepdims=True)
            vn = vc * lax.rsqrt(jnp.mean(vc * vc, axis=-1, keepdims=True) + EPS) * gsgu_ref[...]
            vnb = vn.astype(BF16)

            row = lax.broadcasted_iota(I32, (TM, CONV_WIDTH), 0)
            h6 = halo_s[6:7, :]
            h7 = halo_s[7:8, :]
            z1 = jnp.where(row == 0, h7, pltpu.roll(zc, 1, 0))
            z2 = jnp.where(row == 0, h6, jnp.where(row == 1, h7, pltpu.roll(zc, 2, 0)))
            conv = z2 * wconv_ref[0:1, :] + z1 * wconv_ref[1:2, :] + zc * wconv_ref[2:3, :]
            halo_s[...] = zc[TM - 8:TM, :]
            mix_s[:, 0:CONV_WIDTH] = (proj(0) * conv).astype(BF16)

        sel1 = ridx == i1
        sel2 = ridx == i2
        onehot = jnp.where(sel1 | sel2, 1.0, 0.0)
        counts = jnp.sum(onehot, axis=1, keepdims=True)
        pad_rows = jnp.zeros((LANES - ROUTE_ROWS, LANES), F32)
        counts_sq = jnp.concatenate([jnp.broadcast_to(counts, (ROUTE_ROWS, LANES)), pad_rows], axis=0)
        blocks_sq = jnp.ceil(counts_sq * (1.0 / BLOCK))
        er = lax.broadcasted_iota(I32, (LANES, LANES), 0)
        ec = lax.broadcasted_iota(I32, (LANES, LANES), 1)
        before = (ec < er).astype(BF16)
        run_start = BLOCK * jnp.dot(before, blocks_sq.astype(BF16),
                                    preferred_element_type=F32)[0:ROUTE_ROWS, 0:1]
        rank = jnp.dot(onehot.astype(BF16), tri_s[...], preferred_element_type=F32) + run_start
        pos1_row = jnp.sum(jnp.where(sel1, rank, 0.0), axis=0, keepdims=True)
        pos2_row = jnp.sum(jnp.where(sel2, rank, 0.0), axis=0, keepdims=True)
        n_s[pl.ds(jnp.where(i == 0, nt, i - 1), 1), :] = counts_sq.T[0:1, :]

        r8 = lax.broadcasted_iota(I32, (SUBLANES, TM), 0)
        rows8 = jnp.where(r8 == 0, i1 - ROUTE_LANE0, 0.0)
        rows8 = jnp.where(r8 == 1, i2 - ROUTE_LANE0, rows8)
        rows8 = jnp.where(r8 == 2, w1, rows8)
        rows8 = jnp.where(r8 == 3, w2, rows8)
        rows8 = jnp.where(r8 == 4, pos1_row, rows8)
        rows8 = jnp.where(r8 == 5, pos2_row, rows8)
        route_ref[...] = jnp.concatenate(
            [rows8, jnp.zeros((LANES - SUBLANES, TM), F32)], axis=0).T

        def sort_rows(lo, hi):
            out_row = (lax.broadcasted_iota(I32, (hi - lo, TM), 0) + lo).astype(F32)
            sort_mat = jnp.where((out_row == pos1_row) | (out_row == pos2_row), 1.0, 0.0).astype(BF16)
            stage_ref[lo:hi, :] = jnp.dot(sort_mat, h2b, preferred_element_type=F32).astype(BF16)

        sort_rows(0, SORT_MAIN // 2)
        if mix:
            left = lax.broadcasted_iota(I32, (CHUNK, LANES), 1) < (LANES // 2)
            zero = jnp.zeros((CHUNK, LANES), BF16)
            for j in range(HEAD_PAIRS):
                cols = []
                for c in range(nch):
                    s = vnb[c * CHUNK:(c + 1) * CHUNK, j * LANES:(j + 1) * LANES]
                    cols.append(jnp.concatenate([jnp.where(left, s, zero), jnp.where(left, zero, s)],
                                                axis=0))
                rhs = jnp.concatenate(cols, axis=1)
                res = jnp.dot(wcat_s[j], rhs, preferred_element_type=F32)
                bias = bsp_ref[:, j * LANES:(j + 1) * LANES]
                for c in range(nch):
                    mixed = res[:, c * LANES:(c + 1) * LANES] + bias
                    uu = u[c * CHUNK:(c + 1) * CHUNK, j * LANES:(j + 1) * LANES]
                    mix_s[c * CHUNK:(c + 1) * CHUNK,
                          CONV_WIDTH + j * LANES:CONV_WIDTH + (j + 1) * LANES] = (uu * mixed).astype(BF16)

        sort_rows(SORT_MAIN // 2, SORT_MAIN)
        if mix:
            x1 = x + jnp.dot(mix_s[...], wout_ref[...], preferred_element_type=F32)
            x1_ref[...] = x1
            h2_s[i % 2] = _rms(x1, gffn_ref[...]).astype(BF16)

        used_rows = BLOCK * jnp.sum(blocks_sq[:, 0:1])
        for lo in range(SORT_MAIN, KOUT, TAIL):
            @pl.when(used_rows > lo)
            def _tail(lo=lo):
                sort_rows(lo, lo + TAIL)

            @pl.when(used_rows <= lo)
            def _empty_tail(lo=lo):
                stage_ref[lo:lo + TAIL, :] = jnp.zeros((TAIL, D_MODEL), BF16)

    pl.when(i < nt)(functools.partial(step, True))

    @pl.when(i == nt)
    def _last():
        step(False)
        _build_tables(n_s, nt, meta_ref, src_ref, dst_ref, used_ref)


def _mixer_router(x2d, g_mix, w_in, w_conv, g_sgu, w_sp, b_sp, w_out, g_ffn, w_rt, b_rt, seq,
                  n_tiles):
    t = x2d.shape[0]
    nt = t // TM
    tp = -(-n_tiles // SUBLANES) * SUBLANES
    whole = lambda r: pl.BlockSpec((r, LANES), lambda i: (0, 0))
    full = lambda a: pl.BlockSpec(a.shape, lambda i: (0,) * a.ndim)
    ins = (g_mix, w_in, w_conv, g_sgu, w_sp, b_sp, w_out, g_ffn, w_rt, b_rt)
    prev = lambda i: (jnp.where(i == 0, nt, i - 1), 0)
    return pl.pallas_call(
        functools.partial(_mixer_router_kernel, seq // TM, nt),
        grid=(nt + 1,),
        in_specs=[pl.BlockSpec((TM, D_MODEL), lambda i: (jnp.minimum(i, nt - 1), 0))]
        + [full(a) for a in ins],
        out_specs=[pl.BlockSpec((TM, D_MODEL), lambda i: (jnp.minimum(i, nt - 1), 0)),
                   pl.BlockSpec((TM, LANES), prev),
                   pl.BlockSpec((KOUT, D_MODEL), prev),
                   whole(tp), whole(tp), whole(tp + SUBLANES), whole(nt)],
        out_shape=[jax.ShapeDtypeStruct((t, D_MODEL), F32),
                   jax.ShapeDtypeStruct(((nt + 1) * TM, LANES), F32),
                   jax.ShapeDtypeStruct(((nt + 1) * KOUT, D_MODEL), BF16),
                   jax.ShapeDtypeStruct((tp, LANES), I32),
                   jax.ShapeDtypeStruct((tp, LANES), I32),
                   jax.ShapeDtypeStruct((tp + SUBLANES, LANES), I32),
                   jax.ShapeDtypeStruct((nt, LANES), I32)],
        scratch_shapes=[pltpu.VMEM((HEAD_PAIRS, CHUNK, 2 * CHUNK), BF16),
                        pltpu.VMEM((TM, TM), BF16),
                        pltpu.VMEM((8, CONV_WIDTH), F32),
                        pltpu.VMEM((TM, D_MODEL), BF16),
                        pltpu.VMEM((2, TM, D_MODEL), BF16),
                        pltpu.VMEM(w_in.shape, BF16),
                        pltpu.VMEM(w_out.shape, BF16),
                        pltpu.VMEM(w_rt.shape, BF16),
                        pltpu.VMEM((nt + SUBLANES, LANES), F32)],
        compiler_params=pltpu.CompilerParams(dimension_semantics=("arbitrary",),
                                             vmem_limit_bytes=VMEM_LIMIT),
        name="mixer_router",
    )(x2d, *ins)


def _expert_kernel(meta_ref, src_ref, dst_ref, stage_ref,
                   wg_hbm, wu_hbm, wd_hbm, out_ref,
                   xbuf, obuf, wg_f, wu_f, wd_f, wg_s, wu_s, wd_s, sem, osem, wsem):
    i = pl.program_id(0)
    n_used = meta_ref[3]
    slot = i % N_SLOTS
    bpt = BLOCKS_PER_ROW_TILE
    dump_row = dst_ref.shape[0] // bpt - SUBLANES
    expert_of = lambda tile: meta_ref[tile * META_W]

    def gather_copy(seq, j):
        tile = jnp.minimum(seq, n_used - 1)
        src = pl.multiple_of(src_ref[tile * bpt + j] * BLOCK, BLOCK)
        slot_ = seq % N_SLOTS
        return pltpu.make_async_copy(stage_ref.at[pl.ds(src, BLOCK)],
                                     xbuf.at[slot_, pl.ds(j * BLOCK, BLOCK)], sem.at[slot_])

    def scatter_copy(tile, j):
        row = jnp.where(tile < 0, dump_row, tile)
        dst = pl.multiple_of(dst_ref[row * bpt + j] * BLOCK, BLOCK)
        slot_ = (tile + N_SLOTS) % N_SLOTS
        return pltpu.make_async_copy(obuf.at[slot_, pl.ds(j * BLOCK, BLOCK)],
                                     out_ref.at[pl.ds(dst, BLOCK)], osem.at[slot_])

    def each_block(fn):
        for j in range(BLOCKS_PER_ROW_TILE):
            fn(j)

    def weight_copies(e):
        return (pltpu.make_async_copy(wg_hbm.at[e], wg_f, wsem.at[0]),
                pltpu.make_async_copy(wu_hbm.at[e], wu_f, wsem.at[1]),
                pltpu.make_async_copy(wd_hbm.at[e], wd_f, wsem.at[2]))

    @pl.when(i == 0)
    def _first():
        each_block(lambda j: gather_copy(0, j).start())
        each_block(lambda j: gather_copy(1, j).start())
        for cp in weight_copies(expert_of(0)):
            cp.start(priority=WEIGHT_DMA_PRIORITY)
        obuf[...] = jnp.zeros_like(obuf)

    @pl.when((i >= 2) & (i - 3 < n_used))
    def _free_out_slot():
        each_block(lambda j: scatter_copy(i - 3, j).wait())

    @pl.when(i == n_used)
    def _after_last_tile():
        each_block(lambda j: gather_copy(n_used, j).wait())
        each_block(lambda j: gather_copy(n_used + 1, j).wait())
        each_block(lambda j: scatter_copy(n_used - 1, j).start())

    @pl.when(i < n_used)
    def _tile():
        e = expert_of(i)
        prev = expert_of(jnp.maximum(i - 1, 0))

        @pl.when((i == 0) | (e != prev))
        def _new_expert():
            for cp in weight_copies(e):
                cp.wait()
            wg_s[...] = wg_f[...].astype(BF16)
            wu_s[...] = wu_f[...].astype(BF16)
            wd_s[...] = wd_f[...].astype(BF16)
            nxt = meta_ref[i * META_W + 1]

            @pl.when(nxt >= 0)
            def _next_weights():
                for cp in weight_copies(nxt):
                    cp.start(priority=WEIGHT_DMA_PRIORITY)

        each_block(lambda j: gather_copy(i, j).wait())

        def mlp(rows):
            hb = xbuf[slot, 0:rows, :]
            a = jnp.dot(hb, wg_s[...], preferred_element_type=F32)
            b = jnp.dot(hb, wu_s[...], preferred_element_type=F32)
            each_block(lambda j: gather_copy(i + 2, j).start())
            each_block(lambda j: scatter_copy(i - 1, j).start())
            hid = (a * jax.nn.sigmoid(a) * b).astype(BF16)
            obuf[slot, 0:rows, :] = jnp.dot(hid, wd_s[...], preferred_element_type=F32).astype(BF16)

        rows_needed = meta_ref[i * META_W + 2]
        for rows in range(ROW_STEP, TR + 1, ROW_STEP):
            pl.when(rows_needed == rows)(functools.partial(mlp, rows))


def _experts(meta, src_block, dst_block, stage, w_gate, w_up, w_down, n_tiles):
    any_spec = pl.BlockSpec(memory_space=pl.ANY)
    return pl.pallas_call(
        _expert_kernel,
        grid_spec=pltpu.PrefetchScalarGridSpec(
            num_scalar_prefetch=3,
            grid=(n_tiles,),
            in_specs=[any_spec, any_spec, any_spec, any_spec],
            out_specs=any_spec,
            scratch_shapes=[pltpu.VMEM((N_SLOTS, TR, D_MODEL), BF16),
                            pltpu.VMEM((N_SLOTS, TR, D_MODEL), BF16),
                            pltpu.VMEM((D_MODEL, D_EXPERT), F32),
                            pltpu.VMEM((D_MODEL, D_EXPERT), F32),
                            pltpu.VMEM((D_EXPERT, D_MODEL), F32),
                            pltpu.VMEM((D_MODEL, D_EXPERT), BF16),
                            pltpu.VMEM((D_MODEL, D_EXPERT), BF16),
                            pltpu.VMEM((D_EXPERT, D_MODEL), BF16),
                            pltpu.SemaphoreType.DMA((N_SLOTS,)),
                            pltpu.SemaphoreType.DMA((N_SLOTS,)),
                            pltpu.SemaphoreType.DMA((3,))]),
        out_shape=jax.ShapeDtypeStruct(stage.shape, stage.dtype),
        input_output_aliases={3: 0},
        compiler_params=pltpu.CompilerParams(dimension_semantics=("arbitrary",),
                                             vmem_limit_bytes=VMEM_LIMIT),
        name="experts",
    )(meta, src_block, dst_block, stage, w_gate, w_up, w_down)


def _combine_kernel(used_ref, x1_ref, route_ref, p_ref, gple_ref, wpg_f32, wpp_f32, gfin_ref,
                    srt_ref, out_ref, y_s, wpg_ref, wpp_ref):
    i = pl.program_id(0)

    @pl.when(i == 0)
    def _init():
        wpg_ref[...] = wpg_f32[...].astype(BF16)
        wpp_ref[...] = wpp_f32[...].astype(BF16)

    route = route_ref[...]
    w1, w2, pos1, pos2 = (route[:, k:k + 1] for k in (2, 3, 4, 5))

    def unsort(lo, hi):
        col = (lax.broadcasted_iota(I32, (TM, hi - lo), 1) + lo).astype(F32)
        mat = jnp.where(col == pos1, w1, jnp.where(col == pos2, w2, 0.0)).astype(BF16)
        return jnp.dot(mat, srt_ref[lo:hi, :], preferred_element_type=F32)

    def finish(y):
        pe = jnp.dot(p_ref[...].astype(BF16), wpp_ref[...], preferred_element_type=F32)
        hrows = TM // 2
        for h in range(2):
            rs = slice(h * hrows, (h + 1) * hrows)
            x2 = x1_ref[rs, :] + y[rs, :]
            hg = _rms(x2, gple_ref[...]).astype(BF16)
            gate = jax.nn.sigmoid(jnp.dot(hg, wpg_ref[...], preferred_element_type=F32))
            x3 = x2 + gate * pe[rs, :]
            out_ref[rs, :] = _rms(x3, gfin_ref[...])

    @pl.when(used_ref[i] <= SORT_MAIN)
    def _common():
        finish(unsort(0, SORT_MAIN))

    @pl.when(used_ref[i] > SORT_MAIN)
    def _with_tails():
        y_s[...] = unsort(0, SORT_MAIN)
        for lo in range(SORT_MAIN, KOUT, TAIL):
            @pl.when(used_ref[i] > lo)
            def _tail(lo=lo):
                y_s[...] += unsort(lo, lo + TAIL)
        finish(y_s[...])


def _combine(used, x1, route, p2d, g_ple, w_pg, w_pp, g_fin, sorted_out):
    t = p2d.shape[0]
    full = lambda a: pl.BlockSpec(a.shape, lambda i, *_: (0,) * a.ndim)
    return pl.pallas_call(
        _combine_kernel,
        grid_spec=pltpu.PrefetchScalarGridSpec(
            num_scalar_prefetch=1,
            grid=(t // TM,),
            in_specs=[pl.BlockSpec((TM, D_MODEL), lambda i, *_: (i, 0)),
                      pl.BlockSpec((TM, LANES), lambda i, *_: (i, 0)),
                      pl.BlockSpec((TM, PLE_DIM), lambda i, *_: (i, 0)),
                      full(g_ple), full(w_pg), full(w_pp), full(g_fin),
                      pl.BlockSpec((KOUT, D_MODEL), lambda i, *_: (i, 0))],
            out_specs=pl.BlockSpec((TM, D_MODEL), lambda i, *_: (i, 0)),
            scratch_shapes=[pltpu.VMEM((TM, D_MODEL), F32),
                            pltpu.VMEM(w_pg.shape, BF16),
                            pltpu.VMEM(w_pp.shape, BF16)]),
        out_shape=jax.ShapeDtypeStruct((t, D_MODEL), F32),
        compiler_params=pltpu.CompilerParams(dimension_semantics=("arbitrary",),
                                             vmem_limit_bytes=VMEM_LIMIT),
        name="combine_ple",
    )(used, x1, route, p2d, g_ple, w_pg, w_pp, g_fin, sorted_out)


def kernel(x, p, g_mix, w_in, w_conv, g_sgu, w_spatial, b_spatial, w_out, g_ffn, w_group, b_group,
           w_router, b_router, w_gate, w_up, w_down, g_ple, w_ple_gate, w_ple_proj, g_final):
    bsz, seq, d = x.shape
    t = bsz * seq
    assert w_in.shape[0] == 1, "single-layer block"
    assert d == D_MODEL and seq % TM == 0 and TM % CHUNK == 0
    nt = t // TM
    max_rows = TOP_K * t + nt * N_EXPERTS * (BLOCK - 1)
    n_tiles = (max_rows + N_EXPERTS * (TR - 1)) // TR + N_SLOTS
    l = 0

    w_rt = jnp.concatenate(
        [w_group[l], jnp.transpose(w_router[l], (1, 0, 2)).reshape(d, N_EXPERTS)], axis=1)
    w_rt = jnp.pad(w_rt, ((0, 0), (0, LANES - w_rt.shape[1]))).T
    b_rt = jnp.pad(jnp.concatenate([b_group[l], b_router[l].reshape(-1)]),
                   (0, LANES - N_GROUPS - N_EXPERTS)).reshape(LANES, 1)
    b_sp = jnp.repeat(b_spatial[l].T, SGU_WIDTH // SGU_HEADS, axis=1)

    x1, route, stage, meta, src, dst, used = _mixer_router(
        x.reshape(t, d), g_mix[l].reshape(1, d), w_in[l], w_conv[l],
        g_sgu[l].reshape(1, -1), w_spatial[l], b_sp, w_out[l],
        g_ffn[l].reshape(1, d), w_rt, b_rt, seq, n_tiles)

    bpt = BLOCKS_PER_ROW_TILE
    meta, src, dst, used = (meta[:, :META_W].reshape(-1), src[:, :bpt].reshape(-1),
                            dst[:, :bpt].reshape(-1), used[:, 0])
    sorted_out = _experts(meta, src, dst, stage,
                          w_gate[l].reshape(N_EXPERTS, d, D_EXPERT),
                          w_up[l].reshape(N_EXPERTS, d, D_EXPERT),
                          w_down[l].reshape(N_EXPERTS, D_EXPERT, d), n_tiles)
    out = _combine(used, x1, route, p[l].reshape(t, PLE_DIM), g_ple[l].reshape(1, d),
                   w_ple_gate[l], w_ple_proj[l], g_final.reshape(1, d), sorted_out)
    return out.reshape(bsz, seq, d)
```

```python
import functools

import jax
import jax.numpy as jnp
from jax import lax
from jax.experimental import pallas as pl
from jax.experimental.pallas import tpu as pltpu

F32 = jnp.float32
BF16 = jnp.bfloat16
I32 = jnp.int32

EPS = 1e-6
D_MODEL = 1024
CONV_WIDTH = 512
SGU_WIDTH = 512
SGU_HEADS = 8
HEAD_PAIRS = SGU_HEADS // 2
CHUNK = 128
N_GROUPS = 4
EXPERTS_PER_GROUP = 8
N_EXPERTS = N_GROUPS * EXPERTS_PER_GROUP
TOP_K = 2
D_EXPERT = 512
PLE_DIM = 256
LANES = 128
SUBLANES = 8
BLOCK = 2 * SUBLANES
ROUTE_LANE0 = N_GROUPS
ROUTE_ROWS = 40

TM = 512
TR = 512
BLOCKS_PER_ROW_TILE = TR // BLOCK
KOUT = TOP_K * TM + N_EXPERTS * BLOCK
TAIL = 128
SORT_MAIN = KOUT - 2 * TAIL
META_W = 4
ROW_STEP = 64
N_SLOTS = 3
DUMP_BLOCKS = N_SLOTS * BLOCKS_PER_ROW_TILE
assert DUMP_BLOCKS * BLOCK <= KOUT
WEIGHT_DMA_PRIORITY = 1
VMEM_LIMIT = 56 * 1024 * 1024


def _rms(x, g):
    return x * lax.rsqrt(jnp.mean(x * x, axis=-1, keepdims=True) + EPS) * g


def _build_tables(n_s, nt, meta_ref, src_ref, dst_ref, used_ref):
    tp = meta_ref.shape[0]
    bpt = BLOCKS_PER_ROW_TILE
    kb = KOUT // BLOCK
    n = n_s[0:nt, :]
    lane_n = lax.broadcasted_iota(I32, (nt, LANES), 1)
    is_e = (lane_n >= ROUTE_LANE0) & (lane_n < ROUTE_LANE0 + N_EXPERTS)
    nblk = jnp.where(is_e, jnp.floor((n + (BLOCK - 1)) * (1.0 / BLOCK)), 0.0)
    er = lax.broadcasted_iota(I32, (LANES, LANES), 0)
    ec = lax.broadcasted_iota(I32, (LANES, LANES), 1)
    loc = jnp.dot(nblk.astype(BF16), (er < ec).astype(BF16), preferred_element_type=F32)
    ti = lax.broadcasted_iota(I32, (nt, nt), 0)
    tj = lax.broadcasted_iota(I32, (nt, nt), 1)
    carry = jnp.dot((tj < ti).astype(BF16), nblk.astype(BF16), preferred_element_type=F32)
    total = jnp.sum(nblk, axis=0, keepdims=True)
    tiles_e = jnp.floor((total + (bpt - 1)) * (1.0 / bpt))
    tile_end = jnp.dot(jnp.broadcast_to(tiles_e, (SUBLANES, LANES)).astype(BF16),
                       (er <= ec).astype(BF16), preferred_element_type=F32)[0:1, :]
    tile0 = tile_end - tiles_e
    n_used = jnp.max(tile_end, axis=1, keepdims=True)

    tau = lax.broadcasted_iota(I32, (tp, LANES), 0).astype(F32)
    lane_i = lax.broadcasted_iota(I32, (tp, LANES), 1)
    lane_t = lane_i.astype(F32)
    is_e_t = (lane_i >= ROUTE_LANE0) & (lane_i < ROUTE_LANE0 + N_EXPERTS)
    te = jnp.sum(jnp.where(is_e_t & (tile_end <= jnp.minimum(tau, n_used - 1.0)), 1.0, 0.0),
                 axis=1, keepdims=True)
    oh = lane_t == te + ROUTE_LANE0
    pick = lambda row: jnp.sum(jnp.where(oh, row, 0.0), axis=1, keepdims=True)
    seg0_t = pick(tile0) * bpt
    e_lane = lane_t - ROUTE_LANE0
    cand = jnp.where(is_e_t & (e_lane > te) & (tiles_e > 0.0), e_lane, 99.0)
    nxt = jnp.min(cand, axis=1, keepdims=True)
    nxt = jnp.where(nxt == 99.0, -1.0, nxt)
    rows_t = jnp.clip(pick(total) * BLOCK - (tau[:, 0:1] - pick(tile0)) * TR, ROW_STEP, TR)
    rows_t = jnp.ceil(rows_t * (1.0 / ROW_STEP)) * ROW_STEP
    meta = jnp.where(lane_i == 0, te, 0.0)
    meta = jnp.where(lane_i == 1, nxt, meta)
    meta = jnp.where(lane_i == 2, rows_t, meta)
    meta = jnp.where(lane_i == 3, n_used, meta)
    meta_ref[...] = meta.astype(I32)

    bidx = tau * bpt + lane_t
    off = bidx - seg0_t
    ohb = jnp.where(oh, 1.0, 0.0).astype(BF16)

    def per_tile(tab):
        hi = jnp.floor(tab * (1.0 / 32))
        lo = tab - 32.0 * hi
        nt_dot = lambda a: lax.dot_general(ohb, a.astype(BF16), (((1,), (1,)), ((), ())),
                                           preferred_element_type=F32)
        return 32.0 * nt_dot(hi) + nt_dot(lo)

    run_end = per_tile(carry + nblk)
    run_beg = per_tile(carry)
    run_loc = per_tile(loc)
    tile_of = jnp.zeros((tp, LANES), F32)
    for it in range(nt):
        tile_of = tile_of + jnp.where(run_end[:, it:it + 1] <= off, 1.0, 0.0)
    srcv = off
    for it in range(nt):
        srcv = srcv + jnp.where(tile_of == it,
                                run_loc[:, it:it + 1] - run_beg[:, it:it + 1] + it * kb, 0.0)
    valid = (tile_of < nt) & (bidx < n_used * bpt)
    slot3 = tau - N_SLOTS * jnp.floor((tau + 0.5) * (1.0 / N_SLOTS))
    dump = nt * kb + slot3 * bpt + lane_t
    src_ref[...] = jnp.where(valid, srcv, kb - 1.0).astype(I32)
    dst_ref[0:tp, :] = jnp.where(valid, srcv, dump).astype(I32)
    dst_ref[tp:tp + SUBLANES, :] = (nt * kb + DUMP_BLOCKS - bpt
                                    + lax.broadcasted_iota(I32, (SUBLANES, LANES), 1))
    used_ref[...] = jnp.broadcast_to(jnp.sum(nblk, axis=1, keepdims=True) * BLOCK,
                                     (nt, LANES)).astype(I32)


def _mixer_router_kernel(tiles_per_seq, nt,
                         x_ref, gmix_ref, win_f32, wconv_ref, gsgu_ref, wsp_ref, bsp_ref,
                         wout_f32, gffn_ref, wrt_f32, brt_ref,
                         x1_ref, route_ref, stage_ref, meta_ref, src_ref, dst_ref, used_ref,
                         wcat_s, tri_s, halo_s, mix_s, h2_s, win_ref, wout_ref, wrt_ref, n_s):
    i = pl.program_id(0)
    nch = TM // CHUNK

    @pl.when(i == 0)
    def _init():
        r = lax.broadcasted_iota(I32, (CHUNK, CHUNK), 0)
        c = lax.broadcasted_iota(I32, (CHUNK, CHUNK), 1)
        causal = c <= r
        for j in range(HEAD_PAIRS):
            wa = jnp.where(causal, wsp_ref[2 * j], 0.0)
            wb = jnp.where(causal, wsp_ref[2 * j + 1], 0.0)
            wcat_s[j] = jnp.concatenate([wa, wb], axis=1).astype(BF16)
        rr = lax.broadcasted_iota(I32, (TM, TM), 0)
        cc = lax.broadcasted_iota(I32, (TM, TM), 1)
        tri_s[...] = (rr < cc).astype(BF16)
        h2_s[...] = jnp.zeros_like(h2_s)
        win_ref[...] = win_f32[...].astype(BF16)
        wout_ref[...] = wout_f32[...].astype(BF16)
        wrt_ref[...] = wrt_f32[...].astype(BF16)

    @pl.when(i % tiles_per_seq == 0)
    def _seq_start():
        halo_s[...] = jnp.zeros_like(halo_s)

    def step(mix):
        h2b = h2_s[(i + 1) % 2]
        if mix:
            x = x_ref[...]
            hb = _rms(x, gmix_ref[...]).astype(BF16)

            def proj(k):
                return jnp.dot(hb, win_ref[:, k * 512:(k + 1) * 512], preferred_element_type=F32)

        logits = lax.dot_general(wrt_ref[...], h2b, (((1,), (1,)), ((), ())),
                                 preferred_element_type=F32)[0:ROUTE_ROWS, :] + brt_ref[0:ROUTE_ROWS, :]
        if mix:
            pu = proj(3)
            pv = proj(4)

        ridx = lax.broadcasted_iota(I32, (ROUTE_ROWS, TM), 0).astype(F32)
        neg = jnp.float32(-jnp.inf)
        big = jnp.float32(1e9)
        is_g = ridx < N_GROUPS
        gl = jnp.where(is_g, logits, neg)
        gmax = jnp.max(gl, axis=0, keepdims=True)
        gsum = jnp.sum(jnp.where(is_g, jnp.exp(gl - gmax), 0.0), axis=0, keepdims=True)
        g_w = 1.0 / gsum
        g_idx = jnp.min(jnp.where(gl == gmax, ridx, big), axis=0, keepdims=True)
        lo_row = ROUTE_LANE0 + EXPERTS_PER_GROUP * g_idx
        in_grp = (ridx >= lo_row) & (ridx < lo_row + EXPERTS_PER_GROUP)
        el = jnp.where(in_grp, logits, neg)
        v1 = jnp.max(el, axis=0, keepdims=True)
        i1 = jnp.min(jnp.where(el == v1, ridx, big), axis=0, keepdims=True)
        el2 = jnp.where(ridx == i1, neg, el)
        v2 = jnp.max(el2, axis=0, keepdims=True)
        i2 = jnp.min(jnp.where(el2 == v2, ridx, big), axis=0, keepdims=True)
        e21 = jnp.exp(v2 - v1)
        w1 = g_w / (1.0 + e21)
        w2 = g_w * e21 / (1.0 + e21)

        if mix:
            u = jax.nn.gelu(pu)
            zc = proj(1) * proj(2)
            v = jax.nn.gelu(pv)
            vc = v - jnp.mean(v, axis=-1, keepdims=True)
            vn = vc * lax.rsqrt(jnp.mean(vc * vc, axis=-1, keepdims=True) + EPS) * gsgu_ref[...]
            vnb = vn.astype(BF16)

            row = lax.broadcasted_iota(I32, (TM, CONV_WIDTH), 0)
            h6 = halo_s[6:7, :]
            h7 = halo_s[7:8, :]
            z1 = jnp.where(row == 0, h7, pltpu.roll(zc, 1, 0))
            z2 = jnp.where(row == 0, h6, jnp.where(row == 1, h7, pltpu.roll(zc, 2, 0)))
            conv = z2 * wconv_ref[0:1, :] + z1 * wconv_ref[1:2, :] + zc * wconv_ref[2:3, :]
            halo_s[...] = zc[TM - 8:TM, :]
            mix_s[:, 0:CONV_WIDTH] = (proj(0) * conv).astype(BF16)

        sel1 = ridx == i1
        sel2 = ridx == i2
        onehot = jnp.where(sel1 | sel2, 1.0, 0.0)
        counts = jnp.sum(onehot, axis=1, keepdims=True)
        pad_rows = jnp.zeros((LANES - ROUTE_ROWS, LANES), F32)
        counts_sq = jnp.concatenate([jnp.broadcast_to(counts, (ROUTE_ROWS, LANES)), pad_rows], axis=0)
        blocks_sq = jnp.ceil(counts_sq * (1.0 / BLOCK))
        er = lax.broadcasted_iota(I32, (LANES, LANES), 0)
        ec = lax.broadcasted_iota(I32, (LANES, LANES), 1)
        before = (ec < er).astype(BF16)
        run_start = BLOCK * jnp.dot(before, blocks_sq.astype(BF16),
                                    preferred_element_type=F32)[0:ROUTE_ROWS, 0:1]
        rank = jnp.dot(onehot.astype(BF16), tri_s[...], preferred_element_type=F32) + run_start
        pos1_row = jnp.sum(jnp.where(sel1, rank, 0.0), axis=0, keepdims=True)
        pos2_row = jnp.sum(jnp.where(sel2, rank, 0.0), axis=0, keepdims=True)
        n_s[pl.ds(jnp.where(i == 0, nt, i - 1), 1), :] = counts_sq.T[0:1, :]

        r8 = lax.broadcasted_iota(I32, (SUBLANES, TM), 0)
        rows8 = jnp.where(r8 == 0, i1 - ROUTE_LANE0, 0.0)
        rows8 = jnp.where(r8 == 1, i2 - ROUTE_LANE0, rows8)
        rows8 = jnp.where(r8 == 2, w1, rows8)
        rows8 = jnp.where(r8 == 3, w2, rows8)
        rows8 = jnp.where(r8 == 4, pos1_row, rows8)
        rows8 = jnp.where(r8 == 5, pos2_row, rows8)
        route_ref[...] = jnp.concatenate(
            [rows8, jnp.zeros((LANES - SUBLANES, TM), F32)], axis=0).T

        def sort_rows(lo, hi):
            out_row = (lax.broadcasted_iota(I32, (hi - lo, TM), 0) + lo).astype(F32)
            sort_mat = jnp.where((out_row == pos1_row) | (out_row == pos2_row), 1.0, 0.0).astype(BF16)
            stage_ref[lo:hi, :] = jnp.dot(sort_mat, h2b, preferred_element_type=F32).astype(BF16)

        sort_rows(0, SORT_MAIN // 2)
        if mix:
            left = lax.broadcasted_iota(I32, (CHUNK, LANES), 1) < (LANES // 2)
            zero = jnp.zeros((CHUNK, LANES), BF16)
            for j in range(HEAD_PAIRS):
                cols = []
                for c in range(nch):
                    s = vnb[c * CHUNK:(c + 1) * CHUNK, j * LANES:(j + 1) * LANES]
                    cols.append(jnp.concatenate([jnp.where(left, s, zero), jnp.where(left, zero, s)],
                                                axis=0))
                rhs = jnp.concatenate(cols, axis=1)
                res = jnp.dot(wcat_s[j], rhs, preferred_element_type=F32)
                bias = bsp_ref[:, j * LANES:(j + 1) * LANES]
                for c in range(nch):
                    mixed = res[:, c * LANES:(c + 1) * LANES] + bias
                    uu = u[c * CHUNK:(c + 1) * CHUNK, j * LANES:(j + 1) * LANES]
                    mix_s[c * CHUNK:(c + 1) * CHUNK,
                          CONV_WIDTH + j * LANES:CONV_WIDTH + (j + 1) * LANES] = (uu * mixed).astype(BF16)

        sort_rows(SORT_MAIN // 2, SORT_MAIN)
        if mix:
            x1 = x + jnp.dot(mix_s[...], wout_ref[...], preferred_element_type=F32)
            x1_ref[...] = x1
            h2_s[i % 2] = _rms(x1, gffn_ref[...]).astype(BF16)

        used_rows = BLOCK * jnp.sum(blocks_sq[:, 0:1])
        for lo in range(SORT_MAIN, KOUT, TAIL):
            @pl.when(used_rows > lo)
            def _tail(lo=lo):
                sort_rows(lo, lo + TAIL)

            @pl.when(used_rows <= lo)
            def _empty_tail(lo=lo):
                stage_ref[lo:lo + TAIL, :] = jnp.zeros((TAIL, D_MODEL), BF16)

    pl.when(i < nt)(functools.partial(step, True))

    @pl.when(i == nt)
    def _last():
        step(False)
        _build_tables(n_s, nt, meta_ref, src_ref, dst_ref, used_ref)


def _mixer_router(x2d, g_mix, w_in, w_conv, g_sgu, w_sp, b_sp, w_out, g_ffn, w_rt, b_rt, seq,
                  n_tiles):
    t = x2d.shape[0]
    nt = t // TM
    tp = -(-n_tiles // SUBLANES) * SUBLANES
    whole = lambda r: pl.BlockSpec((r, LANES), lambda i: (0, 0))
    full = lambda a: pl.BlockSpec(a.shape, lambda i: (0,) * a.ndim)
    ins = (g_mix, w_in, w_conv, g_sgu, w_sp, b_sp, w_out, g_ffn, w_rt, b_rt)
    prev = lambda i: (jnp.where(i == 0, nt, i - 1), 0)
    return pl.pallas_call(
        functools.partial(_mixer_router_kernel, seq // TM, nt),
        grid=(nt + 1,),
        in_specs=[pl.BlockSpec((TM, D_MODEL), lambda i: (jnp.minimum(i, nt - 1), 0))]
        + [full(a) for a in ins],
        out_specs=[pl.BlockSpec((TM, D_MODEL), lambda i: (jnp.minimum(i, nt - 1), 0)),
                   pl.BlockSpec((TM, LANES), prev),
                   pl.BlockSpec((KOUT, D_MODEL), prev),
                   whole(tp), whole(tp), whole(tp + SUBLANES), whole(nt)],
        out_shape=[jax.ShapeDtypeStruct((t, D_MODEL), F32),
                   jax.ShapeDtypeStruct(((nt + 1) * TM, LANES), F32),
                   jax.ShapeDtypeStruct(((nt + 1) * KOUT, D_MODEL), BF16),
                   jax.ShapeDtypeStruct((tp, LANES), I32),
                   jax.ShapeDtypeStruct((tp, LANES), I32),
                   jax.ShapeDtypeStruct((tp + SUBLANES, LANES), I32),
                   jax.ShapeDtypeStruct((nt, LANES), I32)],
        scratch_shapes=[pltpu.VMEM((HEAD_PAIRS, CHUNK, 2 * CHUNK), BF16),
                        pltpu.VMEM((TM, TM), BF16),
                        pltpu.VMEM((8, CONV_WIDTH), F32),
                        pltpu.VMEM((TM, D_MODEL), BF16),
                        pltpu.VMEM((2, TM, D_MODEL), BF16),
                        pltpu.VMEM(w_in.shape, BF16),
                        pltpu.VMEM(w_out.shape, BF16),
                        pltpu.VMEM(w_rt.shape, BF16),
                        pltpu.VMEM((nt + SUBLANES, LANES), F32)],
        compiler_params=pltpu.CompilerParams(dimension_semantics=("arbitrary",),
                                             vmem_limit_bytes=VMEM_LIMIT),
        name="mixer_router",
    )(x2d, *ins)


def _expert_kernel(meta_ref, src_ref, dst_ref, stage_ref,
                   wg_hbm, wu_hbm, wd_hbm, out_ref,
                   xbuf, obuf, wg_f, wu_f, wd_f, wg_s, wu_s, wd_s, sem, osem, wsem):
    i = pl.program_id(0)
    n_used = meta_ref[3]
    slot = i % N_SLOTS
    bpt = BLOCKS_PER_ROW_TILE
    dump_row = dst_ref.shape[0] // bpt - SUBLANES
    expert_of = lambda tile: meta_ref[tile * META_W]

    def gather_copy(seq, j):
        tile = jnp.minimum(seq, n_used - 1)
        src = pl.multiple_of(src_ref[tile * bpt + j] * BLOCK, BLOCK)
        slot_ = seq % N_SLOTS
        return pltpu.make_async_copy(stage_ref.at[pl.ds(src, BLOCK)],
                                     xbuf.at[slot_, pl.ds(j * BLOCK, BLOCK)], sem.at[slot_])

    def scatter_copy(tile, j):
        row = jnp.where(tile < 0, dump_row, tile)
        dst = pl.multiple_of(dst_ref[row * bpt + j] * BLOCK, BLOCK)
        slot_ = (tile + N_SLOTS) % N_SLOTS
        return pltpu.make_async_copy(obuf.at[slot_, pl.ds(j * BLOCK, BLOCK)],
                                     out_ref.at[pl.ds(dst, BLOCK)], osem.at[slot_])

    def each_block(fn):
        for j in range(BLOCKS_PER_ROW_TILE):
            fn(j)

    def weight_copies(e):
        return (pltpu.make_async_copy(wg_hbm.at[e], wg_f, wsem.at[0]),
                pltpu.make_async_copy(wu_hbm.at[e], wu_f, wsem.at[1]),
                pltpu.make_async_copy(wd_hbm.at[e], wd_f, wsem.at[2]))

    @pl.when(i == 0)
    def _first():
        each_block(lambda j: gather_copy(0, j).start())
        each_block(lambda j: gather_copy(1, j).start())
        for cp in weight_copies(expert_of(0)):
            cp.start(priority=WEIGHT_DMA_PRIORITY)
        obuf[...] = jnp.zeros_like(obuf)

    @pl.when((i >= 2) & (i - 3 < n_used))
    def _free_out_slot():
        each_block(lambda j: scatter_copy(i - 3, j).wait())

    @pl.when(i == n_used)
    def _after_last_tile():
        each_block(lambda j: gather_copy(n_used, j).wait())
        each_block(lambda j: gather_copy(n_used + 1, j).wait())
        each_block(lambda j: scatter_copy(n_used - 1, j).start())

    @pl.when(i < n_used)
    def _tile():
        e = expert_of(i)
        prev = expert_of(jnp.maximum(i - 1, 0))

        @pl.when((i == 0) | (e != prev))
        def _new_expert():
            for cp in weight_copies(e):
                cp.wait()
            wg_s[...] = wg_f[...].astype(BF16)
            wu_s[...] = wu_f[...].astype(BF16)
            wd_s[...] = wd_f[...].astype(BF16)
            nxt = meta_ref[i * META_W + 1]

            @pl.when(nxt >= 0)
            def _next_weights():
                for cp in weight_copies(nxt):
                    cp.start(priority=WEIGHT_DMA_PRIORITY)

        each_block(lambda j: gather_copy(i, j).wait())

        def mlp(rows):
            hb = xbuf[slot, 0:rows, :]
            a = jnp.dot(hb, wg_s[...], preferred_element_type=F32)
            b = jnp.dot(hb, wu_s[...], preferred_element_type=F32)
            each_block(lambda j: gather_copy(i + 2, j).start())
            each_block(lambda j: scatter_copy(i - 1, j).start())
            hid = (a * jax.nn.sigmoid(a) * b).astype(BF16)
            obuf[slot, 0:rows, :] = jnp.dot(hid, wd_s[...], preferred_element_type=F32).astype(BF16)

        rows_needed = meta_ref[i * META_W + 2]
        for rows in range(ROW_STEP, TR + 1, ROW_STEP):
            pl.when(rows_needed == rows)(functools.partial(mlp, rows))


def _experts(meta, src_block, dst_block, stage, w_gate, w_up, w_down, n_tiles):
    any_spec = pl.BlockSpec(memory_space=pl.ANY)
    return pl.pallas_call(
        _expert_kernel,
        grid_spec=pltpu.PrefetchScalarGridSpec(
            num_scalar_prefetch=3,
            grid=(n_tiles,),
            in_specs=[any_spec, any_spec, any_spec, any_spec],
            out_specs=any_spec,
            scratch_shapes=[pltpu.VMEM((N_SLOTS, TR, D_MODEL), BF16),
                            pltpu.VMEM((N_SLOTS, TR, D_MODEL), BF16),
                            pltpu.VMEM((D_MODEL, D_EXPERT), F32),
                            pltpu.VMEM((D_MODEL, D_EXPERT), F32),
                            pltpu.VMEM((D_EXPERT, D_MODEL), F32),
                            pltpu.VMEM((D_MODEL, D_EXPERT), BF16),
                            pltpu.VMEM((D_MODEL, D_EXPERT), BF16),
                            pltpu.VMEM((D_EXPERT, D_MODEL), BF16),
                            pltpu.SemaphoreType.DMA((N_SLOTS,)),
                            pltpu.SemaphoreType.DMA((N_SLOTS,)),
                            pltpu.SemaphoreType.DMA((3,))]),
        out_shape=jax.ShapeDtypeStruct(stage.shape, stage.dtype),
        input_output_aliases={3: 0},
        compiler_params=pltpu.CompilerParams(dimension_semantics=("arbitrary",),
                                             vmem_limit_bytes=VMEM_LIMIT),
        name="experts",
    )(meta, src_block, dst_block, stage, w_gate, w_up, w_down)


def _combine_kernel(used_ref, x1_ref, route_ref, p_ref, gple_ref, wpg_f32, wpp_f32, gfin_ref,
                    srt_ref, out_ref, y_s, wpg_ref, wpp_ref):
    i = pl.program_id(0)

    @pl.when(i == 0)
    def _init():
        wpg_ref[...] = wpg_f32[...].astype(BF16)
        wpp_ref[...] = wpp_f32[...].astype(BF16)

    route = route_ref[...]
    w1, w2, pos1, pos2 = (route[:, k:k + 1] for k in (2, 3, 4, 5))

    def unsort(lo, hi):
        col = (lax.broadcasted_iota(I32, (TM, hi - lo), 1) + lo).astype(F32)
        mat = jnp.where(col == pos1, w1, jnp.where(col == pos2, w2, 0.0)).astype(BF16)
        return jnp.dot(mat, srt_ref[lo:hi, :], preferred_element_type=F32)

    def finish(y):
        pe = jnp.dot(p_ref[...].astype(BF16), wpp_ref[...], preferred_element_type=F32)
        hrows = TM // 2
        for h in range(2):
            rs = slice(h * hrows, (h + 1) * hrows)
            x2 = x1_ref[rs, :] + y[rs, :]
            hg = _rms(x2, gple_ref[...]).astype(BF16)
            gate = jax.nn.sigmoid(jnp.dot(hg, wpg_ref[...], preferred_element_type=F32))
            x3 = x2 + gate * pe[rs, :]
            out_ref[rs, :] = _rms(x3, gfin_ref[...])

    used_rows = used_ref[i, 0]

    @pl.when(used_rows <= SORT_MAIN)
    def _common():
        finish(unsort(0, SORT_MAIN))

    @pl.when(used_rows > SORT_MAIN)
    def _with_tails():
        y_s[...] = unsort(0, SORT_MAIN)
        for lo in range(SORT_MAIN, KOUT, TAIL):
            @pl.when(used_rows > lo)
            def _tail(lo=lo):
                y_s[...] += unsort(lo, lo + TAIL)
        finish(y_s[...])


def _combine(used, x1, route, p2d, g_ple, w_pg, w_pp, g_fin, sorted_out):
    t = p2d.shape[0]
    full = lambda a: pl.BlockSpec(a.shape, lambda i, *_: (0,) * a.ndim)
    return pl.pallas_call(
        _combine_kernel,
        grid_spec=pltpu.PrefetchScalarGridSpec(
            num_scalar_prefetch=1,
            grid=(t // TM,),
            in_specs=[pl.BlockSpec((TM, D_MODEL), lambda i, *_: (i, 0)),
                      pl.BlockSpec((TM, LANES), lambda i, *_: (i, 0)),
                      pl.BlockSpec((TM, PLE_DIM), lambda i, *_: (i, 0)),
                      full(g_ple), full(w_pg), full(w_pp), full(g_fin),
                      pl.BlockSpec((KOUT, D_MODEL), lambda i, *_: (i, 0))],
            out_specs=pl.BlockSpec((TM, D_MODEL), lambda i, *_: (i, 0)),
            scratch_shapes=[pltpu.VMEM((TM, D_MODEL), F32),
                            pltpu.VMEM(w_pg.shape, BF16),
                            pltpu.VMEM(w_pp.shape, BF16)]),
        out_shape=jax.ShapeDtypeStruct((t, D_MODEL), F32),
        compiler_params=pltpu.CompilerParams(dimension_semantics=("arbitrary",),
                                             vmem_limit_bytes=VMEM_LIMIT),
        name="combine_ple",
    )(used, x1, route, p2d, g_ple, w_pg, w_pp, g_fin, sorted_out)


def kernel(x, p, g_mix, w_in, w_conv, g_sgu, w_spatial, b_spatial, w_out, g_ffn, w_group, b_group,
           w_router, b_router, w_gate, w_up, w_down, g_ple, w_ple_gate, w_ple_proj, g_final):
    bsz, seq, d = x.shape
    t = bsz * seq
    assert w_in.shape[0] == 1, "single-layer block"
    assert d == D_MODEL and seq % TM == 0 and TM % CHUNK == 0
    nt = t // TM
    max_rows = TOP_K * t + nt * N_EXPERTS * (BLOCK - 1)
    n_tiles = (max_rows + N_EXPERTS * (TR - 1)) // TR + N_SLOTS
    l = 0

    w_rt = jnp.concatenate(
        [w_group[l], jnp.transpose(w_router[l], (1, 0, 2)).reshape(d, N_EXPERTS)], axis=1)
    w_rt = jnp.pad(w_rt, ((0, 0), (0, LANES - w_rt.shape[1]))).T
    b_rt = jnp.pad(jnp.concatenate([b_group[l], b_router[l].reshape(-1)]),
                   (0, LANES - N_GROUPS - N_EXPERTS)).reshape(LANES, 1)
    b_sp = jnp.repeat(b_spatial[l].T, SGU_WIDTH // SGU_HEADS, axis=1)

    x1, route, stage, meta, src, dst, used = _mixer_router(
        x.reshape(t, d), g_mix[l].reshape(1, d), w_in[l], w_conv[l],
        g_sgu[l].reshape(1, -1), w_spatial[l], b_sp, w_out[l],
        g_ffn[l].reshape(1, d), w_rt, b_rt, seq, n_tiles)

    bpt = BLOCKS_PER_ROW_TILE
    meta, src, dst = (meta[:, :META_W].reshape(-1), src[:, :bpt].reshape(-1),
                      dst[:, :bpt].reshape(-1))
    sorted_out = _experts(meta, src, dst, stage,
                          w_gate[l].reshape(N_EXPERTS, d, D_EXPERT),
                          w_up[l].reshape(N_EXPERTS, d, D_EXPERT),
                          w_down[l].reshape(N_EXPERTS, D_EXPERT, d), n_tiles)
    out = _combine(used, x1, route, p[l].reshape(t, PLE_DIM), g_ple[l].reshape(1, d),
                   w_ple_gate[l], w_ple_proj[l], g_final.reshape(1, d), sorted_out)
    return out.reshape(bsz, seq, d)
```

```python
import functools

import jax
import jax.numpy as jnp
from jax import lax
from jax.experimental import pallas as pl
from jax.experimental.pallas import tpu as pltpu

F32 = jnp.float32
BF16 = jnp.bfloat16
I32 = jnp.int32

EPS = 1e-6
D_MODEL = 1024
CONV_WIDTH = 512
SGU_WIDTH = 512
SGU_HEADS = 8
HEAD_PAIRS = SGU_HEADS // 2
CHUNK = 128
N_GROUPS = 4
EXPERTS_PER_GROUP = 8
N_EXPERTS = N_GROUPS * EXPERTS_PER_GROUP
TOP_K = 2
D_EXPERT = 512
PLE_DIM = 256
LANES = 128
SUBLANES = 8
BLOCK = 2 * SUBLANES
ROUTE_LANE0 = N_GROUPS
ROUTE_ROWS = 40

TM = 512
TR = 512
BLOCKS_PER_ROW_TILE = TR // BLOCK
KOUT = TOP_K * TM + N_EXPERTS * BLOCK
TAIL = 128
SORT_MAIN = KOUT - 2 * TAIL
META_W = 4
ROW_STEP = 128
N_SLOTS = 3
DUMP_BLOCKS = N_SLOTS * BLOCKS_PER_ROW_TILE
assert DUMP_BLOCKS * BLOCK <= KOUT
WEIGHT_DMA_PRIORITY = 1
VMEM_LIMIT = 56 * 1024 * 1024


def _rms(x, g):
    return x * lax.rsqrt(jnp.mean(x * x, axis=-1, keepdims=True) + EPS) * g


def _build_tables(n_s, nt, meta_ref, src_ref, dst_ref, used_ref):
    tp = meta_ref.shape[0]
    bpt = BLOCKS_PER_ROW_TILE
    kb = KOUT // BLOCK
    n = n_s[0:nt, :]
    lane_n = lax.broadcasted_iota(I32, (nt, LANES), 1)
    is_e = (lane_n >= ROUTE_LANE0) & (lane_n < ROUTE_LANE0 + N_EXPERTS)
    nblk = jnp.where(is_e, jnp.floor((n + (BLOCK - 1)) * (1.0 / BLOCK)), 0.0)
    er = lax.broadcasted_iota(I32, (LANES, LANES), 0)
    ec = lax.broadcasted_iota(I32, (LANES, LANES), 1)
    loc = jnp.dot(nblk.astype(BF16), (er < ec).astype(BF16), preferred_element_type=F32)
    ti = lax.broadcasted_iota(I32, (nt, nt), 0)
    tj = lax.broadcasted_iota(I32, (nt, nt), 1)
    carry = jnp.dot((tj < ti).astype(BF16), nblk.astype(BF16), preferred_element_type=F32)
    total = jnp.sum(nblk, axis=0, keepdims=True)
    tiles_e = jnp.floor((total + (bpt - 1)) * (1.0 / bpt))
    tile_end = jnp.dot(jnp.broadcast_to(tiles_e, (SUBLANES, LANES)).astype(BF16),
                       (er <= ec).astype(BF16), preferred_element_type=F32)[0:1, :]
    tile0 = tile_end - tiles_e
    n_used = jnp.max(tile_end, axis=1, keepdims=True)

    tau = lax.broadcasted_iota(I32, (tp, LANES), 0).astype(F32)
    lane_i = lax.broadcasted_iota(I32, (tp, LANES), 1)
    lane_t = lane_i.astype(F32)
    is_e_t = (lane_i >= ROUTE_LANE0) & (lane_i < ROUTE_LANE0 + N_EXPERTS)
    te = jnp.sum(jnp.where(is_e_t & (tile_end <= jnp.minimum(tau, n_used - 1.0)), 1.0, 0.0),
                 axis=1, keepdims=True)
    oh = lane_t == te + ROUTE_LANE0
    pick = lambda row: jnp.sum(jnp.where(oh, row, 0.0), axis=1, keepdims=True)
    seg0_t = pick(tile0) * bpt
    e_lane = lane_t - ROUTE_LANE0
    cand = jnp.where(is_e_t & (e_lane > te) & (tiles_e > 0.0), e_lane, 99.0)
    nxt = jnp.min(cand, axis=1, keepdims=True)
    nxt = jnp.where(nxt == 99.0, -1.0, nxt)
    rows_t = jnp.clip(pick(total) * BLOCK - (tau[:, 0:1] - pick(tile0)) * TR, ROW_STEP, TR)
    rows_t = jnp.ceil(rows_t * (1.0 / ROW_STEP)) * ROW_STEP
    meta = jnp.where(lane_i == 0, te, 0.0)
    meta = jnp.where(lane_i == 1, nxt, meta)
    meta = jnp.where(lane_i == 2, rows_t, meta)
    meta = jnp.where(lane_i == 3, n_used, meta)
    meta_ref[...] = meta.astype(I32)

    bidx = tau * bpt + lane_t
    off = bidx - seg0_t
    ohb = jnp.where(oh, 1.0, 0.0).astype(BF16)

    def per_tile(tab):
        hi = jnp.floor(tab * (1.0 / 32))
        lo = tab - 32.0 * hi
        nt_dot = lambda a: lax.dot_general(ohb, a.astype(BF16), (((1,), (1,)), ((), ())),
                                           preferred_element_type=F32)
        return 32.0 * nt_dot(hi) + nt_dot(lo)

    run_end = per_tile(carry + nblk)
    run_beg = per_tile(carry)
    run_loc = per_tile(loc)
    tile_of = jnp.zeros((tp, LANES), F32)
    for it in range(nt):
        tile_of = tile_of + jnp.where(run_end[:, it:it + 1] <= off, 1.0, 0.0)
    srcv = off
    for it in range(nt):
        srcv = srcv + jnp.where(tile_of == it,
                                run_loc[:, it:it + 1] - run_beg[:, it:it + 1] + it * kb, 0.0)
    valid = (tile_of < nt) & (bidx < n_used * bpt)
    slot3 = tau - N_SLOTS * jnp.floor((tau + 0.5) * (1.0 / N_SLOTS))
    dump = nt * kb + slot3 * bpt + lane_t
    src_ref[...] = jnp.where(valid, srcv, kb - 1.0).astype(I32)
    dst_ref[0:tp, :] = jnp.where(valid, srcv, dump).astype(I32)
    dst_ref[tp:tp + SUBLANES, :] = (nt * kb + DUMP_BLOCKS - bpt
                                    + lax.broadcasted_iota(I32, (SUBLANES, LANES), 1))
    used_ref[...] = jnp.broadcast_to(jnp.sum(nblk, axis=1, keepdims=True) * BLOCK,
                                     (nt, LANES)).astype(I32)


def _mixer_router_kernel(tiles_per_seq, nt,
                         x_ref, gmix_ref, win_f32, wconv_ref, gsgu_ref, wsp_ref, bsp_ref,
                         wout_f32, gffn_ref, wrt_f32, brt_ref,
                         x1_ref, route_ref, stage_ref, meta_ref, src_ref, dst_ref, used_ref,
                         wcat_s, tri_s, halo_s, mix_s, h2_s, win_ref, wout_ref, wrt_ref, n_s):
    i = pl.program_id(0)
    nch = TM // CHUNK

    @pl.when(i == 0)
    def _init():
        r = lax.broadcasted_iota(I32, (CHUNK, CHUNK), 0)
        c = lax.broadcasted_iota(I32, (CHUNK, CHUNK), 1)
        causal = c <= r
        for j in range(HEAD_PAIRS):
            wa = jnp.where(causal, wsp_ref[2 * j], 0.0)
            wb = jnp.where(causal, wsp_ref[2 * j + 1], 0.0)
            wcat_s[j] = jnp.concatenate([wa, wb], axis=1).astype(BF16)
        rr = lax.broadcasted_iota(I32, (TM, TM), 0)
        cc = lax.broadcasted_iota(I32, (TM, TM), 1)
        tri_s[...] = (rr < cc).astype(BF16)
        h2_s[...] = jnp.zeros_like(h2_s)
        win_ref[...] = win_f32[...].astype(BF16)
        wout_ref[...] = wout_f32[...].astype(BF16)
        wrt_ref[...] = wrt_f32[...].astype(BF16)

    @pl.when(i % tiles_per_seq == 0)
    def _seq_start():
        halo_s[...] = jnp.zeros_like(halo_s)

    def step(mix):
        h2b = h2_s[(i + 1) % 2]
        if mix:
            x = x_ref[...]
            hb = _rms(x, gmix_ref[...]).astype(BF16)

            def proj(k):
                return jnp.dot(hb, win_ref[:, k * 512:(k + 1) * 512], preferred_element_type=F32)

        logits = lax.dot_general(wrt_ref[...], h2b, (((1,), (1,)), ((), ())),
                                 preferred_element_type=F32)[0:ROUTE_ROWS, :] + brt_ref[0:ROUTE_ROWS, :]
        if mix:
            pu = proj(3)
            pv = proj(4)

        ridx = lax.broadcasted_iota(I32, (ROUTE_ROWS, TM), 0).astype(F32)
        neg = jnp.float32(-jnp.inf)
        big = jnp.float32(1e9)
        is_g = ridx < N_GROUPS
        gl = jnp.where(is_g, logits, neg)
        gmax = jnp.max(gl, axis=0, keepdims=True)
        gsum = jnp.sum(jnp.where(is_g, jnp.exp(gl - gmax), 0.0), axis=0, keepdims=True)
        g_w = 1.0 / gsum
        g_idx = jnp.min(jnp.where(gl == gmax, ridx, big), axis=0, keepdims=True)
        lo_row = ROUTE_LANE0 + EXPERTS_PER_GROUP * g_idx
        in_grp = (ridx >= lo_row) & (ridx < lo_row + EXPERTS_PER_GROUP)
        el = jnp.where(in_grp, logits, neg)
        v1 = jnp.max(el, axis=0, keepdims=True)
        i1 = jnp.min(jnp.where(el == v1, ridx, big), axis=0, keepdims=True)
        el2 = jnp.where(ridx == i1, neg, el)
        v2 = jnp.max(el2, axis=0, keepdims=True)
        i2 = jnp.min(jnp.where(el2 == v2, ridx, big), axis=0, keepdims=True)
        e21 = jnp.exp(v2 - v1)
        w1 = g_w / (1.0 + e21)
        w2 = g_w * e21 / (1.0 + e21)

        if mix:
            u = jax.nn.gelu(pu)
            zc = proj(1) * proj(2)
            v = jax.nn.gelu(pv)
            vc = v - jnp.mean(v, axis=-1, keepdims=True)
            vn = vc * lax.rsqrt(jnp.mean(vc * vc, axis=-1, keepdims=True) + EPS) * gsgu_ref[...]
            vnb = vn.astype(BF16)

            row = lax.broadcasted_iota(I32, (TM, CONV_WIDTH), 0)
            h6 = halo_s[6:7, :]
            h7 = halo_s[7:8, :]
            z1 = jnp.where(row == 0, h7, pltpu.roll(zc, 1, 0))
            z2 = jnp.where(row == 0, h6, jnp.where(row == 1, h7, pltpu.roll(zc, 2, 0)))
            conv = z2 * wconv_ref[0:1, :] + z1 * wconv_ref[1:2, :] + zc * wconv_ref[2:3, :]
            halo_s[...] = zc[TM - 8:TM, :]
            mix_s[:, 0:CONV_WIDTH] = (proj(0) * conv).astype(BF16)

        sel1 = ridx == i1
        sel2 = ridx == i2
        onehot = jnp.where(sel1 | sel2, 1.0, 0.0)
        counts = jnp.sum(onehot, axis=1, keepdims=True)
        pad_rows = jnp.zeros((LANES - ROUTE_ROWS, LANES), F32)
        counts_sq = jnp.concatenate([jnp.broadcast_to(counts, (ROUTE_ROWS, LANES)), pad_rows], axis=0)
        blocks_sq = jnp.ceil(counts_sq * (1.0 / BLOCK))
        er = lax.broadcasted_iota(I32, (LANES, LANES), 0)
        ec = lax.broadcasted_iota(I32, (LANES, LANES), 1)
        before = (ec < er).astype(BF16)
        run_start = BLOCK * jnp.dot(before, blocks_sq.astype(BF16),
                                    preferred_element_type=F32)[0:ROUTE_ROWS, 0:1]
        rank = jnp.dot(onehot.astype(BF16), tri_s[...], preferred_element_type=F32) + run_start
        pos1_row = jnp.sum(jnp.where(sel1, rank, 0.0), axis=0, keepdims=True)
        pos2_row = jnp.sum(jnp.where(sel2, rank, 0.0), axis=0, keepdims=True)
        n_s[pl.ds(jnp.where(i == 0, nt, i - 1), 1), :] = counts_sq.T[0:1, :]

        r8 = lax.broadcasted_iota(I32, (SUBLANES, TM), 0)
        rows8 = jnp.where(r8 == 0, i1 - ROUTE_LANE0, 0.0)
        rows8 = jnp.where(r8 == 1, i2 - ROUTE_LANE0, rows8)
        rows8 = jnp.where(r8 == 2, w1, rows8)
        rows8 = jnp.where(r8 == 3, w2, rows8)
        rows8 = jnp.where(r8 == 4, pos1_row, rows8)
        rows8 = jnp.where(r8 == 5, pos2_row, rows8)
        route_ref[...] = jnp.concatenate(
            [rows8, jnp.zeros((LANES - SUBLANES, TM), F32)], axis=0).T

        def sort_rows(lo, hi):
            out_row = (lax.broadcasted_iota(I32, (hi - lo, TM), 0) + lo).astype(F32)
            sort_mat = jnp.where((out_row == pos1_row) | (out_row == pos2_row), 1.0, 0.0).astype(BF16)
            stage_ref[lo:hi, :] = jnp.dot(sort_mat, h2b, preferred_element_type=F32).astype(BF16)

        sort_rows(0, SORT_MAIN // 2)
        if mix:
            left = lax.broadcasted_iota(I32, (CHUNK, LANES), 1) < (LANES // 2)
            zero = jnp.zeros((CHUNK, LANES), BF16)
            for j in range(HEAD_PAIRS):
                cols = []
                for c in range(nch):
                    s = vnb[c * CHUNK:(c + 1) * CHUNK, j * LANES:(j + 1) * LANES]
                    cols.append(jnp.concatenate([jnp.where(left, s, zero), jnp.where(left, zero, s)],
                                                axis=0))
                rhs = jnp.concatenate(cols, axis=1)
                res = jnp.dot(wcat_s[j], rhs, preferred_element_type=F32)
                bias = bsp_ref[:, j * LANES:(j + 1) * LANES]
                for c in range(nch):
                    mixed = res[:, c * LANES:(c + 1) * LANES] + bias
                    uu = u[c * CHUNK:(c + 1) * CHUNK, j * LANES:(j + 1) * LANES]
                    mix_s[c * CHUNK:(c + 1) * CHUNK,
                          CONV_WIDTH + j * LANES:CONV_WIDTH + (j + 1) * LANES] = (uu * mixed).astype(BF16)

        sort_rows(SORT_MAIN // 2, SORT_MAIN)
        if mix:
            x1 = x + jnp.dot(mix_s[...], wout_ref[...], preferred_element_type=F32)
            x1_ref[...] = x1
            h2_s[i % 2] = _rms(x1, gffn_ref[...]).astype(BF16)

        used_rows = BLOCK * jnp.sum(blocks_sq[:, 0:1])
        for lo in range(SORT_MAIN, KOUT, TAIL):
            @pl.when(used_rows > lo)
            def _tail(lo=lo):
                sort_rows(lo, lo + TAIL)

            @pl.when(used_rows <= lo)
            def _empty_tail(lo=lo):
                stage_ref[lo:lo + TAIL, :] = jnp.zeros((TAIL, D_MODEL), BF16)

    pl.when(i < nt)(functools.partial(step, True))

    @pl.when(i == nt)
    def _last():
        step(False)
        _build_tables(n_s, nt, meta_ref, src_ref, dst_ref, used_ref)


def _mixer_router(x2d, g_mix, w_in, w_conv, g_sgu, w_sp, b_sp, w_out, g_ffn, w_rt, b_rt, seq,
                  n_tiles):
    t = x2d.shape[0]
    nt = t // TM
    tp = -(-n_tiles // SUBLANES) * SUBLANES
    whole = lambda r: pl.BlockSpec((r, LANES), lambda i: (0, 0))
    full = lambda a: pl.BlockSpec(a.shape, lambda i: (0,) * a.ndim)
    ins = (g_mix, w_in, w_conv, g_sgu, w_sp, b_sp, w_out, g_ffn, w_rt, b_rt)
    prev = lambda i: (jnp.where(i == 0, nt, i - 1), 0)
    return pl.pallas_call(
        functools.partial(_mixer_router_kernel, seq // TM, nt),
        grid=(nt + 1,),
        in_specs=[pl.BlockSpec((TM, D_MODEL), lambda i: (jnp.minimum(i, nt - 1), 0))]
        + [full(a) for a in ins],
        out_specs=[pl.BlockSpec((TM, D_MODEL), lambda i: (jnp.minimum(i, nt - 1), 0)),
                   pl.BlockSpec((TM, LANES), prev),
                   pl.BlockSpec((KOUT, D_MODEL), prev),
                   whole(tp), whole(tp), whole(tp + SUBLANES), whole(nt)],
        out_shape=[jax.ShapeDtypeStruct((t, D_MODEL), F32),
                   jax.ShapeDtypeStruct(((nt + 1) * TM, LANES), F32),
                   jax.ShapeDtypeStruct(((nt + 1) * KOUT, D_MODEL), BF16),
                   jax.ShapeDtypeStruct((tp, LANES), I32),
                   jax.ShapeDtypeStruct((tp, LANES), I32),
                   jax.ShapeDtypeStruct((tp + SUBLANES, LANES), I32),
                   jax.ShapeDtypeStruct((nt, LANES), I32)],
        scratch_shapes=[pltpu.VMEM((HEAD_PAIRS, CHUNK, 2 * CHUNK), BF16),
                        pltpu.VMEM((TM, TM), BF16),
                        pltpu.VMEM((8, CONV_WIDTH), F32),
                        pltpu.VMEM((TM, D_MODEL), BF16),
                        pltpu.VMEM((2, TM, D_MODEL), BF16),
                        pltpu.VMEM(w_in.shape, BF16),
                        pltpu.VMEM(w_out.shape, BF16),
                        pltpu.VMEM(w_rt.shape, BF16),
                        pltpu.VMEM((nt + SUBLANES, LANES), F32)],
        compiler_params=pltpu.CompilerParams(dimension_semantics=("arbitrary",),
                                             vmem_limit_bytes=VMEM_LIMIT),
        name="mixer_router",
    )(x2d, *ins)


def _expert_kernel(meta_ref, src_ref, dst_ref, stage_ref,
                   wg_hbm, wu_hbm, wd_hbm, out_ref,
                   xbuf, obuf, wg_f, wu_f, wd_f, wg_s, wu_s, wd_s, sem, osem, wsem):
    i = pl.program_id(0)
    n_used = meta_ref[3]
    slot = i % N_SLOTS
    bpt = BLOCKS_PER_ROW_TILE
    dump_row = dst_ref.shape[0] // bpt - SUBLANES
    expert_of = lambda tile: meta_ref[tile * META_W]

    def gather_copy(seq, j):
        tile = jnp.minimum(seq, n_used - 1)
        src = pl.multiple_of(src_ref[tile * bpt + j] * BLOCK, BLOCK)
        slot_ = seq % N_SLOTS
        return pltpu.make_async_copy(stage_ref.at[pl.ds(src, BLOCK)],
                                     xbuf.at[slot_, pl.ds(j * BLOCK, BLOCK)], sem.at[slot_])

    def scatter_copy(tile, j):
        row = jnp.where(tile < 0, dump_row, tile)
        dst = pl.multiple_of(dst_ref[row * bpt + j] * BLOCK, BLOCK)
        slot_ = (tile + N_SLOTS) % N_SLOTS
        return pltpu.make_async_copy(obuf.at[slot_, pl.ds(j * BLOCK, BLOCK)],
                                     out_ref.at[pl.ds(dst, BLOCK)], osem.at[slot_])

    def each_block(fn):
        for j in range(BLOCKS_PER_ROW_TILE):
            fn(j)

    def weight_copies(e):
        return (pltpu.make_async_copy(wg_hbm.at[e], wg_f, wsem.at[0]),
                pltpu.make_async_copy(wu_hbm.at[e], wu_f, wsem.at[1]),
                pltpu.make_async_copy(wd_hbm.at[e], wd_f, wsem.at[2]))

    @pl.when(i == 0)
    def _first():
        each_block(lambda j: gather_copy(0, j).start())
        each_block(lambda j: gather_copy(1, j).start())
        for cp in weight_copies(expert_of(0)):
            cp.start(priority=WEIGHT_DMA_PRIORITY)
        obuf[...] = jnp.zeros_like(obuf)

    @pl.when((i >= 2) & (i - 3 < n_used))
    def _free_out_slot():
        each_block(lambda j: scatter_copy(i - 3, j).wait())

    @pl.when(i == n_used)
    def _after_last_tile():
        each_block(lambda j: gather_copy(n_used, j).wait())
        each_block(lambda j: gather_copy(n_used + 1, j).wait())
        each_block(lambda j: scatter_copy(n_used - 1, j).start())

    @pl.when(i < n_used)
    def _tile():
        e = expert_of(i)
        prev = expert_of(jnp.maximum(i - 1, 0))

        @pl.when((i == 0) | (e != prev))
        def _new_expert():
            for cp in weight_copies(e):
                cp.wait()
            wg_s[...] = wg_f[...].astype(BF16)
            wu_s[...] = wu_f[...].astype(BF16)
            wd_s[...] = wd_f[...].astype(BF16)
            nxt = meta_ref[i * META_W + 1]

            @pl.when(nxt >= 0)
            def _next_weights():
                for cp in weight_copies(nxt):
                    cp.start(priority=WEIGHT_DMA_PRIORITY)

        each_block(lambda j: gather_copy(i, j).wait())

        def mlp(rows):
            hb = xbuf[slot, 0:rows, :]
            a = jnp.dot(hb, wg_s[...], preferred_element_type=F32)
            b = jnp.dot(hb, wu_s[...], preferred_element_type=F32)
            each_block(lambda j: gather_copy(i + 2, j).start())
            each_block(lambda j: scatter_copy(i - 1, j).start())
            hid = (a * jax.nn.sigmoid(a) * b).astype(BF16)
            obuf[slot, 0:rows, :] = jnp.dot(hid, wd_s[...], preferred_element_type=F32).astype(BF16)

        rows_needed = meta_ref[i * META_W + 2]
        for rows in range(ROW_STEP, TR + 1, ROW_STEP):
            pl.when(rows_needed == rows)(functools.partial(mlp, rows))


def _experts(meta, src_block, dst_block, stage, w_gate, w_up, w_down, n_tiles):
    any_spec = pl.BlockSpec(memory_space=pl.ANY)
    return pl.pallas_call(
        _expert_kernel,
        grid_spec=pltpu.PrefetchScalarGridSpec(
            num_scalar_prefetch=3,
            grid=(n_tiles,),
            in_specs=[any_spec, any_spec, any_spec, any_spec],
            out_specs=any_spec,
            scratch_shapes=[pltpu.VMEM((N_SLOTS, TR, D_MODEL), BF16),
                            pltpu.VMEM((N_SLOTS, TR, D_MODEL), BF16),
                            pltpu.VMEM((D_MODEL, D_EXPERT), F32),
                            pltpu.VMEM((D_MODEL, D_EXPERT), F32),
                            pltpu.VMEM((D_EXPERT, D_MODEL), F32),
                            pltpu.VMEM((D_MODEL, D_EXPERT), BF16),
                            pltpu.VMEM((D_MODEL, D_EXPERT), BF16),
                            pltpu.VMEM((D_EXPERT, D_MODEL), BF16),
                            pltpu.SemaphoreType.DMA((N_SLOTS,)),
                            pltpu.SemaphoreType.DMA((N_SLOTS,)),
                            pltpu.SemaphoreType.DMA((3,))]),
        out_shape=jax.ShapeDtypeStruct(stage.shape, stage.dtype),
        input_output_aliases={3: 0},
        compiler_params=pltpu.CompilerParams(dimension_semantics=("arbitrary",),
                                             vmem_limit_bytes=VMEM_LIMIT),
        name="experts",
    )(meta, src_block, dst_block, stage, w_gate, w_up, w_down)


def _combine_kernel(used_ref, x1_ref, route_ref, p_ref, gple_ref, wpg_f32, wpp_f32, gfin_ref,
                    srt_ref, out_ref, y_s, wpg_ref, wpp_ref):
    i = pl.program_id(0)

    @pl.when(i == 0)
    def _init():
        wpg_ref[...] = wpg_f32[...].astype(BF16)
        wpp_ref[...] = wpp_f32[...].astype(BF16)

    route = route_ref[...]
    w1, w2, pos1, pos2 = (route[:, k:k + 1] for k in (2, 3, 4, 5))

    def unsort(lo, hi):
        col = (lax.broadcasted_iota(I32, (TM, hi - lo), 1) + lo).astype(F32)
        mat = jnp.where(col == pos1, w1, jnp.where(col == pos2, w2, 0.0)).astype(BF16)
        return jnp.dot(mat, srt_ref[lo:hi, :], preferred_element_type=F32)

    def finish(y):
        pe = jnp.dot(p_ref[...].astype(BF16), wpp_ref[...], preferred_element_type=F32)
        hrows = TM // 2
        for h in range(2):
            rs = slice(h * hrows, (h + 1) * hrows)
            x2 = x1_ref[rs, :] + y[rs, :]
            hg = _rms(x2, gple_ref[...]).astype(BF16)
            gate = jax.nn.sigmoid(jnp.dot(hg, wpg_ref[...], preferred_element_type=F32))
            x3 = x2 + gate * pe[rs, :]
            out_ref[rs, :] = _rms(x3, gfin_ref[...])

    used_rows = used_ref[i, 0]

    @pl.when(used_rows <= SORT_MAIN)
    def _common():
        finish(unsort(0, SORT_MAIN))

    @pl.when(used_rows > SORT_MAIN)
    def _with_tails():
        y_s[...] = unsort(0, SORT_MAIN)
        for lo in range(SORT_MAIN, KOUT, TAIL):
            @pl.when(used_rows > lo)
            def _tail(lo=lo):
                y_s[...] += unsort(lo, lo + TAIL)
        finish(y_s[...])


def _combine(used, x1, route, p2d, g_ple, w_pg, w_pp, g_fin, sorted_out):
    t = p2d.shape[0]
    full = lambda a: pl.BlockSpec(a.shape, lambda i, *_: (0,) * a.ndim)
    return pl.pallas_call(
        _combine_kernel,
        grid_spec=pltpu.PrefetchScalarGridSpec(
            num_scalar_prefetch=1,
            grid=(t // TM,),
            in_specs=[pl.BlockSpec((TM, D_MODEL), lambda i, *_: (i, 0)),
                      pl.BlockSpec((TM, LANES), lambda i, *_: (i, 0)),
                      pl.BlockSpec((TM, PLE_DIM), lambda i, *_: (i, 0)),
                      full(g_ple), full(w_pg), full(w_pp), full(g_fin),
                      pl.BlockSpec((KOUT, D_MODEL), lambda i, *_: (i, 0))],
            out_specs=pl.BlockSpec((TM, D_MODEL), lambda i, *_: (i, 0)),
            scratch_shapes=[pltpu.VMEM((TM, D_MODEL), F32),
                            pltpu.VMEM(w_pg.shape, BF16),
                            pltpu.VMEM(w_pp.shape, BF16)]),
        out_shape=jax.ShapeDtypeStruct((t, D_MODEL), F32),
        compiler_params=pltpu.CompilerParams(dimension_semantics=("arbitrary",),
                                             vmem_limit_bytes=VMEM_LIMIT),
        name="combine_ple",
    )(used, x1, route, p2d, g_ple, w_pg, w_pp, g_fin, sorted_out)


def kernel(x, p, g_mix, w_in, w_conv, g_sgu, w_spatial, b_spatial, w_out, g_ffn, w_group, b_group,
           w_router, b_router, w_gate, w_up, w_down, g_ple, w_ple_gate, w_ple_proj, g_final):
    bsz, seq, d = x.shape
    t = bsz * seq
    assert w_in.shape[0] == 1, "single-layer block"
    assert d == D_MODEL and seq % TM == 0 and TM % CHUNK == 0
    nt = t // TM
    max_rows = TOP_K * t + nt * N_EXPERTS * (BLOCK - 1)
    n_tiles = (max_rows + N_EXPERTS * (TR - 1)) // TR + N_SLOTS
    l = 0

    w_rt = jnp.concatenate(
        [w_group[l], jnp.transpose(w_router[l], (1, 0, 2)).reshape(d, N_EXPERTS)], axis=1)
    w_rt = jnp.pad(w_rt, ((0, 0), (0, LANES - w_rt.shape[1]))).T
    b_rt = jnp.pad(jnp.concatenate([b_group[l], b_router[l].reshape(-1)]),
                   (0, LANES - N_GROUPS - N_EXPERTS)).reshape(LANES, 1)
    b_sp = jnp.repeat(b_spatial[l].T, SGU_WIDTH // SGU_HEADS, axis=1)

    x1, route, stage, meta, src, dst, used = _mixer_router(
        x.reshape(t, d), g_mix[l].reshape(1, d), w_in[l], w_conv[l],
        g_sgu[l].reshape(1, -1), w_spatial[l], b_sp, w_out[l],
        g_ffn[l].reshape(1, d), w_rt, b_rt, seq, n_tiles)

    bpt = BLOCKS_PER_ROW_TILE
    meta, src, dst = (meta[:, :META_W].reshape(-1), src[:, :bpt].reshape(-1),
                      dst[:, :bpt].reshape(-1))
    sorted_out = _experts(meta, src, dst, stage,
                          w_gate[l].reshape(N_EXPERTS, d, D_EXPERT),
                          w_up[l].reshape(N_EXPERTS, d, D_EXPERT),
                          w_down[l].reshape(N_EXPERTS, D_EXPERT, d), n_tiles)
    out = _combine(used, x1, route, p[l].reshape(t, PLE_DIM), g_ple[l].reshape(1, d),
                   w_ple_gate[l], w_ple_proj[l], g_final.reshape(1, d), sorted_out)
    return out.reshape(bsz, seq, d)
```

```python
import functools

import jax
import jax.numpy as jnp
from jax import lax
from jax.experimental import pallas as pl
from jax.experimental.pallas import tpu as pltpu

F32 = jnp.float32
BF16 = jnp.bfloat16
I32 = jnp.int32

EPS = 1e-6
D_MODEL = 1024
CONV_WIDTH = 512
SGU_WIDTH = 512
SGU_HEADS = 8
HEAD_PAIRS = SGU_HEADS // 2
CHUNK = 128
N_GROUPS = 4
EXPERTS_PER_GROUP = 8
N_EXPERTS = N_GROUPS * EXPERTS_PER_GROUP
TOP_K = 2
D_EXPERT = 512
PLE_DIM = 256
LANES = 128
SUBLANES = 8
BLOCK = 2 * SUBLANES
ROUTE_LANE0 = N_GROUPS
ROUTE_ROWS = 40

TM = 512
TR = 512
BLOCKS_PER_ROW_TILE = TR // BLOCK
KOUT = TOP_K * TM + N_EXPERTS * BLOCK
TAIL = 128
SORT_MAIN = KOUT - 2 * TAIL
META_W = 4
ROW_STEP = 128
N_SLOTS = 3
DUMP_BLOCKS = N_SLOTS * BLOCKS_PER_ROW_TILE
assert DUMP_BLOCKS * BLOCK <= KOUT
WEIGHT_DMA_PRIORITY = 1
OUTPUT_DMA_PRIORITY = 1
VMEM_LIMIT = 56 * 1024 * 1024


def _rms(x, g):
    return x * lax.rsqrt(jnp.mean(x * x, axis=-1, keepdims=True) + EPS) * g


def _build_tables(n_s, nt, meta_ref, src_ref, dst_ref, used_ref):
    tp = meta_ref.shape[0]
    bpt = BLOCKS_PER_ROW_TILE
    kb = KOUT // BLOCK
    n = n_s[0:nt, :]
    lane_n = lax.broadcasted_iota(I32, (nt, LANES), 1)
    is_e = (lane_n >= ROUTE_LANE0) & (lane_n < ROUTE_LANE0 + N_EXPERTS)
    nblk = jnp.where(is_e, jnp.floor((n + (BLOCK - 1)) * (1.0 / BLOCK)), 0.0)
    er = lax.broadcasted_iota(I32, (LANES, LANES), 0)
    ec = lax.broadcasted_iota(I32, (LANES, LANES), 1)
    loc = jnp.dot(nblk.astype(BF16), (er < ec).astype(BF16), preferred_element_type=F32)
    ti = lax.broadcasted_iota(I32, (nt, nt), 0)
    tj = lax.broadcasted_iota(I32, (nt, nt), 1)
    carry = jnp.dot((tj < ti).astype(BF16), nblk.astype(BF16), preferred_element_type=F32)
    total = jnp.sum(nblk, axis=0, keepdims=True)
    tiles_e = jnp.floor((total + (bpt - 1)) * (1.0 / bpt))
    tile_end = jnp.dot(jnp.broadcast_to(tiles_e, (SUBLANES, LANES)).astype(BF16),
                       (er <= ec).astype(BF16), preferred_element_type=F32)[0:1, :]
    tile0 = tile_end - tiles_e
    n_used = jnp.max(tile_end, axis=1, keepdims=True)

    tau = lax.broadcasted_iota(I32, (tp, LANES), 0).astype(F32)
    lane_i = lax.broadcasted_iota(I32, (tp, LANES), 1)
    lane_t = lane_i.astype(F32)
    is_e_t = (lane_i >= ROUTE_LANE0) & (lane_i < ROUTE_LANE0 + N_EXPERTS)
    te = jnp.sum(jnp.where(is_e_t & (tile_end <= jnp.minimum(tau, n_used - 1.0)), 1.0, 0.0),
                 axis=1, keepdims=True)
    oh = lane_t == te + ROUTE_LANE0
    pick = lambda row: jnp.sum(jnp.where(oh, row, 0.0), axis=1, keepdims=True)
    seg0_t = pick(tile0) * bpt
    e_lane = lane_t - ROUTE_LANE0
    cand = jnp.where(is_e_t & (e_lane > te) & (tiles_e > 0.0), e_lane, 99.0)
    nxt = jnp.min(cand, axis=1, keepdims=True)
    nxt = jnp.where(nxt == 99.0, -1.0, nxt)
    rows_t = jnp.clip(pick(total) * BLOCK - (tau[:, 0:1] - pick(tile0)) * TR, ROW_STEP, TR)
    rows_t = jnp.ceil(rows_t * (1.0 / ROW_STEP)) * ROW_STEP
    meta = jnp.where(lane_i == 0, te, 0.0)
    meta = jnp.where(lane_i == 1, nxt, meta)
    meta = jnp.where(lane_i == 2, rows_t, meta)
    meta = jnp.where(lane_i == 3, n_used, meta)
    meta_ref[...] = meta.astype(I32)

    bidx = tau * bpt + lane_t
    off = bidx - seg0_t
    ohb = jnp.where(oh, 1.0, 0.0).astype(BF16)

    def per_tile(tab):
        hi = jnp.floor(tab * (1.0 / 32))
        lo = tab - 32.0 * hi
        nt_dot = lambda a: lax.dot_general(ohb, a.astype(BF16), (((1,), (1,)), ((), ())),
                                           preferred_element_type=F32)
        return 32.0 * nt_dot(hi) + nt_dot(lo)

    run_end = per_tile(carry + nblk)
    run_beg = per_tile(carry)
    run_loc = per_tile(loc)
    tile_of = jnp.zeros((tp, LANES), F32)
    for it in range(nt):
        tile_of = tile_of + jnp.where(run_end[:, it:it + 1] <= off, 1.0, 0.0)
    srcv = off
    for it in range(nt):
        srcv = srcv + jnp.where(tile_of == it,
                                run_loc[:, it:it + 1] - run_beg[:, it:it + 1] + it * kb, 0.0)
    valid = (tile_of < nt) & (bidx < n_used * bpt)
    slot3 = tau - N_SLOTS * jnp.floor((tau + 0.5) * (1.0 / N_SLOTS))
    dump = nt * kb + slot3 * bpt + lane_t
    src_ref[...] = jnp.where(valid, srcv, kb - 1.0).astype(I32)
    dst_ref[0:tp, :] = jnp.where(valid, srcv, dump).astype(I32)
    dst_ref[tp:tp + SUBLANES, :] = (nt * kb + DUMP_BLOCKS - bpt
                                    + lax.broadcasted_iota(I32, (SUBLANES, LANES), 1))
    used_ref[...] = jnp.broadcast_to(jnp.sum(nblk, axis=1, keepdims=True) * BLOCK,
                                     (nt, LANES)).astype(I32)


def _mixer_router_kernel(tiles_per_seq, nt,
                         x_ref, gmix_ref, win_f32, wconv_ref, gsgu_ref, wsp_ref, bsp_ref,
                         wout_f32, gffn_ref, wrt_f32, brt_ref,
                         x1_ref, route_ref, stage_ref, meta_ref, src_ref, dst_ref, used_ref,
                         wcat_s, tri_s, halo_s, mix_s, h2_s, win_ref, wout_ref, wrt_ref, n_s):
    i = pl.program_id(0)
    nch = TM // CHUNK

    @pl.when(i == 0)
    def _init():
        r = lax.broadcasted_iota(I32, (CHUNK, CHUNK), 0)
        c = lax.broadcasted_iota(I32, (CHUNK, CHUNK), 1)
        causal = c <= r
        for j in range(HEAD_PAIRS):
            wa = jnp.where(causal, wsp_ref[2 * j], 0.0)
            wb = jnp.where(causal, wsp_ref[2 * j + 1], 0.0)
            wcat_s[j] = jnp.concatenate([wa, wb], axis=1).astype(BF16)
        rr = lax.broadcasted_iota(I32, (TM, TM), 0)
        cc = lax.broadcasted_iota(I32, (TM, TM), 1)
        tri_s[...] = (rr < cc).astype(BF16)
        h2_s[...] = jnp.zeros_like(h2_s)
        win_ref[...] = win_f32[...].astype(BF16)
        wout_ref[...] = wout_f32[...].astype(BF16)
        wrt_ref[...] = wrt_f32[...].astype(BF16)

    @pl.when(i % tiles_per_seq == 0)
    def _seq_start():
        halo_s[...] = jnp.zeros_like(halo_s)

    def step(mix):
        h2b = h2_s[(i + 1) % 2]
        if mix:
            x = x_ref[...]
            hb = _rms(x, gmix_ref[...]).astype(BF16)

            def proj(k):
                return jnp.dot(hb, win_ref[:, k * 512:(k + 1) * 512], preferred_element_type=F32)

        logits = lax.dot_general(wrt_ref[...], h2b, (((1,), (1,)), ((), ())),
                                 preferred_element_type=F32)[0:ROUTE_ROWS, :] + brt_ref[0:ROUTE_ROWS, :]
        if mix:
            pu = proj(3)
            pv = proj(4)

        ridx = lax.broadcasted_iota(I32, (ROUTE_ROWS, TM), 0).astype(F32)
        neg = jnp.float32(-jnp.inf)
        big = jnp.float32(1e9)
        is_g = ridx < N_GROUPS
        gl = jnp.where(is_g, logits, neg)
        gmax = jnp.max(gl, axis=0, keepdims=True)
        gsum = jnp.sum(jnp.where(is_g, jnp.exp(gl - gmax), 0.0), axis=0, keepdims=True)
        g_w = 1.0 / gsum
        g_idx = jnp.min(jnp.where(gl == gmax, ridx, big), axis=0, keepdims=True)
        lo_row = ROUTE_LANE0 + EXPERTS_PER_GROUP * g_idx
        in_grp = (ridx >= lo_row) & (ridx < lo_row + EXPERTS_PER_GROUP)
        el = jnp.where(in_grp, logits, neg)
        v1 = jnp.max(el, axis=0, keepdims=True)
        i1 = jnp.min(jnp.where(el == v1, ridx, big), axis=0, keepdims=True)
        el2 = jnp.where(ridx == i1, neg, el)
        v2 = jnp.max(el2, axis=0, keepdims=True)
        i2 = jnp.min(jnp.where(el2 == v2, ridx, big), axis=0, keepdims=True)
        e21 = jnp.exp(v2 - v1)
        w1 = g_w / (1.0 + e21)
        w2 = g_w * e21 / (1.0 + e21)

        if mix:
            u = jax.nn.gelu(pu)
            zc = proj(1) * proj(2)
            v = jax.nn.gelu(pv)
            vc = v - jnp.mean(v, axis=-1, keepdims=True)
            vn = vc * lax.rsqrt(jnp.mean(vc * vc, axis=-1, keepdims=True) + EPS) * gsgu_ref[...]
            vnb = vn.astype(BF16)

            row = lax.broadcasted_iota(I32, (TM, CONV_WIDTH), 0)
            h6 = halo_s[6:7, :]
            h7 = halo_s[7:8, :]
            z1 = jnp.where(row == 0, h7, pltpu.roll(zc, 1, 0))
            z2 = jnp.where(row == 0, h6, jnp.where(row == 1, h7, pltpu.roll(zc, 2, 0)))
            conv = z2 * wconv_ref[0:1, :] + z1 * wconv_ref[1:2, :] + zc * wconv_ref[2:3, :]
            halo_s[...] = zc[TM - 8:TM, :]
            mix_s[:, 0:CONV_WIDTH] = (proj(0) * conv).astype(BF16)

        sel1 = ridx == i1
        sel2 = ridx == i2
        onehot = jnp.where(sel1 | sel2, 1.0, 0.0)
        counts = jnp.sum(onehot, axis=1, keepdims=True)
        pad_rows = jnp.zeros((LANES - ROUTE_ROWS, LANES), F32)
        counts_sq = jnp.concatenate([jnp.broadcast_to(counts, (ROUTE_ROWS, LANES)), pad_rows], axis=0)
        blocks_sq = jnp.ceil(counts_sq * (1.0 / BLOCK))
        er = lax.broadcasted_iota(I32, (LANES, LANES), 0)
        ec = lax.broadcasted_iota(I32, (LANES, LANES), 1)
        before = (ec < er).astype(BF16)
        run_start = BLOCK * jnp.dot(before, blocks_sq.astype(BF16),
                                    preferred_element_type=F32)[0:ROUTE_ROWS, 0:1]
        rank = jnp.dot(onehot.astype(BF16), tri_s[...], preferred_element_type=F32) + run_start
        pos1_row = jnp.sum(jnp.where(sel1, rank, 0.0), axis=0, keepdims=True)
        pos2_row = jnp.sum(jnp.where(sel2, rank, 0.0), axis=0, keepdims=True)
        n_s[pl.ds(jnp.where(i == 0, nt, i - 1), 1), :] = counts_sq.T[0:1, :]

        r8 = lax.broadcasted_iota(I32, (SUBLANES, TM), 0)
        rows8 = jnp.where(r8 == 0, i1 - ROUTE_LANE0, 0.0)
        rows8 = jnp.where(r8 == 1, i2 - ROUTE_LANE0, rows8)
        rows8 = jnp.where(r8 == 2, w1, rows8)
        rows8 = jnp.where(r8 == 3, w2, rows8)
        rows8 = jnp.where(r8 == 4, pos1_row, rows8)
        rows8 = jnp.where(r8 == 5, pos2_row, rows8)
        route_ref[...] = jnp.concatenate(
            [rows8, jnp.zeros((LANES - SUBLANES, TM), F32)], axis=0).T

        def sort_rows(lo, hi):
            out_row = (lax.broadcasted_iota(I32, (hi - lo, TM), 0) + lo).astype(F32)
            sort_mat = jnp.where((out_row == pos1_row) | (out_row == pos2_row), 1.0, 0.0).astype(BF16)
            stage_ref[lo:hi, :] = jnp.dot(sort_mat, h2b, preferred_element_type=F32).astype(BF16)

        sort_rows(0, SORT_MAIN // 2)
        if mix:
            left = lax.broadcasted_iota(I32, (CHUNK, LANES), 1) < (LANES // 2)
            zero = jnp.zeros((CHUNK, LANES), BF16)
            for j in range(HEAD_PAIRS):
                cols = []
                for c in range(nch):
                    s = vnb[c * CHUNK:(c + 1) * CHUNK, j * LANES:(j + 1) * LANES]
                    cols.append(jnp.concatenate([jnp.where(left, s, zero), jnp.where(left, zero, s)],
                                                axis=0))
                rhs = jnp.concatenate(cols, axis=1)
                res = jnp.dot(wcat_s[j], rhs, preferred_element_type=F32)
                bias = bsp_ref[:, j * LANES:(j + 1) * LANES]
                for c in range(nch):
                    mixed = res[:, c * LANES:(c + 1) * LANES] + bias
                    uu = u[c * CHUNK:(c + 1) * CHUNK, j * LANES:(j + 1) * LANES]
                    mix_s[c * CHUNK:(c + 1) * CHUNK,
                          CONV_WIDTH + j * LANES:CONV_WIDTH + (j + 1) * LANES] = (uu * mixed).astype(BF16)

        sort_rows(SORT_MAIN // 2, SORT_MAIN)
        if mix:
            x1 = x + jnp.dot(mix_s[...], wout_ref[...], preferred_element_type=F32)
            x1_ref[...] = x1
            h2_s[i % 2] = _rms(x1, gffn_ref[...]).astype(BF16)

        used_rows = BLOCK * jnp.sum(blocks_sq[:, 0:1])
        for lo in range(SORT_MAIN, KOUT, TAIL):
            @pl.when(used_rows > lo)
            def _tail(lo=lo):
                sort_rows(lo, lo + TAIL)

            @pl.when(used_rows <= lo)
            def _empty_tail(lo=lo):
                stage_ref[lo:lo + TAIL, :] = jnp.zeros((TAIL, D_MODEL), BF16)

    pl.when(i < nt)(functools.partial(step, True))

    @pl.when(i == nt)
    def _last():
        step(False)
        _build_tables(n_s, nt, meta_ref, src_ref, dst_ref, used_ref)


def _mixer_router(x2d, g_mix, w_in, w_conv, g_sgu, w_sp, b_sp, w_out, g_ffn, w_rt, b_rt, seq,
                  n_tiles):
    t = x2d.shape[0]
    nt = t // TM
    tp = -(-n_tiles // SUBLANES) * SUBLANES
    whole = lambda r: pl.BlockSpec((r, LANES), lambda i: (0, 0))
    full = lambda a: pl.BlockSpec(a.shape, lambda i: (0,) * a.ndim)
    ins = (g_mix, w_in, w_conv, g_sgu, w_sp, b_sp, w_out, g_ffn, w_rt, b_rt)
    prev = lambda i: (jnp.where(i == 0, nt, i - 1), 0)
    return pl.pallas_call(
        functools.partial(_mixer_router_kernel, seq // TM, nt),
        grid=(nt + 1,),
        in_specs=[pl.BlockSpec((TM, D_MODEL), lambda i: (jnp.minimum(i, nt - 1), 0))]
        + [full(a) for a in ins],
        out_specs=[pl.BlockSpec((TM, D_MODEL), lambda i: (jnp.minimum(i, nt - 1), 0)),
                   pl.BlockSpec((TM, LANES), prev),
                   pl.BlockSpec((KOUT, D_MODEL), prev),
                   whole(tp), whole(tp), whole(tp + SUBLANES), whole(nt)],
        out_shape=[jax.ShapeDtypeStruct((t, D_MODEL), F32),
                   jax.ShapeDtypeStruct(((nt + 1) * TM, LANES), F32),
                   jax.ShapeDtypeStruct(((nt + 1) * KOUT, D_MODEL), BF16),
                   jax.ShapeDtypeStruct((tp, LANES), I32),
                   jax.ShapeDtypeStruct((tp, LANES), I32),
                   jax.ShapeDtypeStruct((tp + SUBLANES, LANES), I32),
                   jax.ShapeDtypeStruct((nt, LANES), I32)],
        scratch_shapes=[pltpu.VMEM((HEAD_PAIRS, CHUNK, 2 * CHUNK), BF16),
                        pltpu.VMEM((TM, TM), BF16),
                        pltpu.VMEM((8, CONV_WIDTH), F32),
                        pltpu.VMEM((TM, D_MODEL), BF16),
                        pltpu.VMEM((2, TM, D_MODEL), BF16),
                        pltpu.VMEM(w_in.shape, BF16),
                        pltpu.VMEM(w_out.shape, BF16),
                        pltpu.VMEM(w_rt.shape, BF16),
                        pltpu.VMEM((nt + SUBLANES, LANES), F32)],
        compiler_params=pltpu.CompilerParams(dimension_semantics=("arbitrary",),
                                             vmem_limit_bytes=VMEM_LIMIT),
        name="mixer_router",
    )(x2d, *ins)


def _expert_kernel(meta_ref, src_ref, dst_ref, stage_ref,
                   wg_hbm, wu_hbm, wd_hbm, out_ref,
                   xbuf, obuf, wg_f, wu_f, wd_f, wg_s, wu_s, wd_s, sem, osem, wsem):
    i = pl.program_id(0)
    n_used = meta_ref[3]
    slot = i % N_SLOTS
    bpt = BLOCKS_PER_ROW_TILE
    dump_row = dst_ref.shape[0] // bpt - SUBLANES
    expert_of = lambda tile: meta_ref[tile * META_W]

    def gather_copy(seq, j):
        tile = jnp.minimum(seq, n_used - 1)
        src = pl.multiple_of(src_ref[tile * bpt + j] * BLOCK, BLOCK)
        slot_ = seq % N_SLOTS
        return pltpu.make_async_copy(stage_ref.at[pl.ds(src, BLOCK)],
                                     xbuf.at[slot_, pl.ds(j * BLOCK, BLOCK)], sem.at[slot_])

    def scatter_copy(tile, j):
        row = jnp.where(tile < 0, dump_row, tile)
        dst = pl.multiple_of(dst_ref[row * bpt + j] * BLOCK, BLOCK)
        slot_ = (tile + N_SLOTS) % N_SLOTS
        return pltpu.make_async_copy(obuf.at[slot_, pl.ds(j * BLOCK, BLOCK)],
                                     out_ref.at[pl.ds(dst, BLOCK)], osem.at[slot_])

    def each_block(fn):
        for j in range(BLOCKS_PER_ROW_TILE):
            fn(j)

    def weight_copies(e):
        return (pltpu.make_async_copy(wg_hbm.at[e], wg_f, wsem.at[0]),
                pltpu.make_async_copy(wu_hbm.at[e], wu_f, wsem.at[1]),
                pltpu.make_async_copy(wd_hbm.at[e], wd_f, wsem.at[2]))

    @pl.when(i == 0)
    def _first():
        each_block(lambda j: gather_copy(0, j).start())
        each_block(lambda j: gather_copy(1, j).start())
        for cp in weight_copies(expert_of(0)):
            cp.start(priority=WEIGHT_DMA_PRIORITY)
        obuf[...] = jnp.zeros_like(obuf)

    @pl.when((i >= 2) & (i - 3 < n_used))
    def _free_out_slot():
        each_block(lambda j: scatter_copy(i - 3, j).wait())

    @pl.when(i == n_used)
    def _after_last_tile():
        each_block(lambda j: gather_copy(n_used, j).wait())
        each_block(lambda j: gather_copy(n_used + 1, j).wait())
        each_block(lambda j: scatter_copy(n_used - 1, j).start(priority=OUTPUT_DMA_PRIORITY))

    @pl.when(i < n_used)
    def _tile():
        e = expert_of(i)
        prev = expert_of(jnp.maximum(i - 1, 0))

        @pl.when((i == 0) | (e != prev))
        def _new_expert():
            for cp in weight_copies(e):
                cp.wait()
            wg_s[...] = wg_f[...].astype(BF16)
            wu_s[...] = wu_f[...].astype(BF16)
            wd_s[...] = wd_f[...].astype(BF16)
            nxt = meta_ref[i * META_W + 1]

            @pl.when(nxt >= 0)
            def _next_weights():
                for cp in weight_copies(nxt):
                    cp.start(priority=WEIGHT_DMA_PRIORITY)

        each_block(lambda j: gather_copy(i, j).wait())

        def mlp(rows):
            hb = xbuf[slot, 0:rows, :]
            a = jnp.dot(hb, wg_s[...], preferred_element_type=F32)
            b = jnp.dot(hb, wu_s[...], preferred_element_type=F32)
            each_block(lambda j: gather_copy(i + 2, j).start())
            each_block(lambda j: scatter_copy(i - 1, j).start(priority=OUTPUT_DMA_PRIORITY))
            hid = (a * jax.nn.sigmoid(a) * b).astype(BF16)
            obuf[slot, 0:rows, :] = jnp.dot(hid, wd_s[...], preferred_element_type=F32).astype(BF16)

        rows_needed = meta_ref[i * META_W + 2]
        for rows in range(ROW_STEP, TR + 1, ROW_STEP):
            pl.when(rows_needed == rows)(functools.partial(mlp, rows))


def _experts(meta, src_block, dst_block, stage, w_gate, w_up, w_down, n_tiles):
    any_spec = pl.BlockSpec(memory_space=pl.ANY)
    return pl.pallas_call(
        _expert_kernel,
        grid_spec=pltpu.PrefetchScalarGridSpec(
            num_scalar_prefetch=3,
            grid=(n_tiles,),
            in_specs=[any_spec, any_spec, any_spec, any_spec],
            out_specs=any_spec,
            scratch_shapes=[pltpu.VMEM((N_SLOTS, TR, D_MODEL), BF16),
                            pltpu.VMEM((N_SLOTS, TR, D_MODEL), BF16),
                            pltpu.VMEM((D_MODEL, D_EXPERT), F32),
                            pltpu.VMEM((D_MODEL, D_EXPERT), F32),
                            pltpu.VMEM((D_EXPERT, D_MODEL), F32),
                            pltpu.VMEM((D_MODEL, D_EXPERT), BF16),
                            pltpu.VMEM((D_MODEL, D_EXPERT), BF16),
                            pltpu.VMEM((D_EXPERT, D_MODEL), BF16),
                            pltpu.SemaphoreType.DMA((N_SLOTS,)),
                            pltpu.SemaphoreType.DMA((N_SLOTS,)),
                            pltpu.SemaphoreType.DMA((3,))]),
        out_shape=jax.ShapeDtypeStruct(stage.shape, stage.dtype),
        input_output_aliases={3: 0},
        compiler_params=pltpu.CompilerParams(dimension_semantics=("arbitrary",),
                                             vmem_limit_bytes=VMEM_LIMIT),
        name="experts",
    )(meta, src_block, dst_block, stage, w_gate, w_up, w_down)


def _combine_kernel(used_ref, x1_ref, route_ref, p_ref, gple_ref, wpg_f32, wpp_f32, gfin_ref,
                    srt_ref, out_ref, y_s, wpg_ref, wpp_ref):
    i = pl.program_id(0)

    @pl.when(i == 0)
    def _init():
        wpg_ref[...] = wpg_f32[...].astype(BF16)
        wpp_ref[...] = wpp_f32[...].astype(BF16)

    route = route_ref[...]
    w1, w2, pos1, pos2 = (route[:, k:k + 1] for k in (2, 3, 4, 5))

    def unsort(lo, hi):
        col = (lax.broadcasted_iota(I32, (TM, hi - lo), 1) + lo).astype(F32)
        mat = jnp.where(col == pos1, w1, jnp.where(col == pos2, w2, 0.0)).astype(BF16)
        return jnp.dot(mat, srt_ref[lo:hi, :], preferred_element_type=F32)

    def finish(y):
        pe = jnp.dot(p_ref[...].astype(BF16), wpp_ref[...], preferred_element_type=F32)
        hrows = TM // 2
        for h in range(2):
            rs = slice(h * hrows, (h + 1) * hrows)
            x2 = x1_ref[rs, :] + y[rs, :]
            hg = _rms(x2, gple_ref[...]).astype(BF16)
            gate = jax.nn.sigmoid(jnp.dot(hg, wpg_ref[...], preferred_element_type=F32))
            x3 = x2 + gate * pe[rs, :]
            out_ref[rs, :] = _rms(x3, gfin_ref[...])

    @pl.when(used_ref[i] <= SORT_MAIN)
    def _common():
        finish(unsort(0, SORT_MAIN))

    @pl.when(used_ref[i] > SORT_MAIN)
    def _with_tails():
        y_s[...] = unsort(0, SORT_MAIN)
        for lo in range(SORT_MAIN, KOUT, TAIL):
            @pl.when(used_ref[i] > lo)
            def _tail(lo=lo):
                y_s[...] += unsort(lo, lo + TAIL)
        finish(y_s[...])


def _combine(used, x1, route, p2d, g_ple, w_pg, w_pp, g_fin, sorted_out):
    t = p2d.shape[0]
    full = lambda a: pl.BlockSpec(a.shape, lambda i, *_: (0,) * a.ndim)
    return pl.pallas_call(
        _combine_kernel,
        grid_spec=pltpu.PrefetchScalarGridSpec(
            num_scalar_prefetch=1,
            grid=(t // TM,),
            in_specs=[pl.BlockSpec((TM, D_MODEL), lambda i, *_: (i, 0)),
                      pl.BlockSpec((TM, LANES), lambda i, *_: (i, 0)),
                      pl.BlockSpec((TM, PLE_DIM), lambda i, *_: (i, 0)),
                      full(g_ple), full(w_pg), full(w_pp), full(g_fin),
                      pl.BlockSpec((KOUT, D_MODEL), lambda i, *_: (i, 0))],
            out_specs=pl.BlockSpec((TM, D_MODEL), lambda i, *_: (i, 0)),
            scratch_shapes=[pltpu.VMEM((TM, D_MODEL), F32),
                            pltpu.VMEM(w_pg.shape, BF16),
                            pltpu.VMEM(w_pp.shape, BF16)]),
        out_shape=jax.ShapeDtypeStruct((t, D_MODEL), F32),
        compiler_params=pltpu.CompilerParams(dimension_semantics=("arbitrary",),
                                             vmem_limit_bytes=VMEM_LIMIT),
        name="combine_ple",
    )(used, x1, route, p2d, g_ple, w_pg, w_pp, g_fin, sorted_out)


def kernel(x, p, g_mix, w_in, w_conv, g_sgu, w_spatial, b_spatial, w_out, g_ffn, w_group, b_group,
           w_router, b_router, w_gate, w_up, w_down, g_ple, w_ple_gate, w_ple_proj, g_final):
    bsz, seq, d = x.shape
    t = bsz * seq
    assert w_in.shape[0] == 1, "single-layer block"
    assert d == D_MODEL and seq % TM == 0 and TM % CHUNK == 0
    nt = t // TM
    max_rows = TOP_K * t + nt * N_EXPERTS * (BLOCK - 1)
    n_tiles = (max_rows + N_EXPERTS * (TR - 1)) // TR + N_SLOTS
    l = 0

    w_rt = jnp.concatenate(
        [w_group[l], jnp.transpose(w_router[l], (1, 0, 2)).reshape(d, N_EXPERTS)], axis=1)
    w_rt = jnp.pad(w_rt, ((0, 0), (0, LANES - w_rt.shape[1]))).T
    b_rt = jnp.pad(jnp.concatenate([b_group[l], b_router[l].reshape(-1)]),
                   (0, LANES - N_GROUPS - N_EXPERTS)).reshape(LANES, 1)
    b_sp = jnp.repeat(b_spatial[l].T, SGU_WIDTH // SGU_HEADS, axis=1)

    x1, route, stage, meta, src, dst, used = _mixer_router(
        x.reshape(t, d), g_mix[l].reshape(1, d), w_in[l], w_conv[l],
        g_sgu[l].reshape(1, -1), w_spatial[l], b_sp, w_out[l],
        g_ffn[l].reshape(1, d), w_rt, b_rt, seq, n_tiles)

    bpt = BLOCKS_PER_ROW_TILE
    meta, src, dst, used = (meta[:, :META_W].reshape(-1), src[:, :bpt].reshape(-1),
                            dst[:, :bpt].reshape(-1), used[:, 0])
    sorted_out = _experts(meta, src, dst, stage,
                          w_gate[l].reshape(N_EXPERTS, d, D_EXPERT),
                          w_up[l].reshape(N_EXPERTS, d, D_EXPERT),
                          w_down[l].reshape(N_EXPERTS, D_EXPERT, d), n_tiles)
    out = _combine(used, x1, route, p[l].reshape(t, PLE_DIM), g_ple[l].reshape(1, d),
                   w_ple_gate[l], w_ple_proj[l], g_final.reshape(1, d), sorted_out)
    return out.reshape(bsz, seq, d)
```

```python
import functools

import jax
import jax.numpy as jnp
from jax import lax
from jax.experimental import pallas as pl
from jax.experimental.pallas import tpu as pltpu

F32 = jnp.float32
BF16 = jnp.bfloat16
I32 = jnp.int32

EPS = 1e-6
D_MODEL = 1024
CONV_WIDTH = 512
SGU_WIDTH = 512
SGU_HEADS = 8
HEAD_PAIRS = SGU_HEADS // 2
CHUNK = 128
N_GROUPS = 4
EXPERTS_PER_GROUP = 8
N_EXPERTS = N_GROUPS * EXPERTS_PER_GROUP
TOP_K = 2
D_EXPERT = 512
PLE_DIM = 256
LANES = 128
SUBLANES = 8
BLOCK = 2 * SUBLANES
ROUTE_LANE0 = N_GROUPS
ROUTE_ROWS = 40

TM = 512
TR = 512
BLOCKS_PER_ROW_TILE = TR // BLOCK
KOUT = TOP_K * TM + N_EXPERTS * BLOCK
TAIL = 128
SORT_MAIN = KOUT - 2 * TAIL
COMBINE_TILES = 2
META_W = 4
ROW_STEP = 128
N_SLOTS = 3
DUMP_BLOCKS = N_SLOTS * BLOCKS_PER_ROW_TILE
assert DUMP_BLOCKS * BLOCK <= KOUT
WEIGHT_DMA_PRIORITY = 1
VMEM_LIMIT = 56 * 1024 * 1024


def _rms(x, g):
    return x * lax.rsqrt(jnp.mean(x * x, axis=-1, keepdims=True) + EPS) * g


def _build_tables(n_s, nt, meta_ref, src_ref, dst_ref, used_ref):
    tp = meta_ref.shape[0]
    bpt = BLOCKS_PER_ROW_TILE
    kb = KOUT // BLOCK
    n = n_s[0:nt, :]
    lane_n = lax.broadcasted_iota(I32, (nt, LANES), 1)
    is_e = (lane_n >= ROUTE_LANE0) & (lane_n < ROUTE_LANE0 + N_EXPERTS)
    nblk = jnp.where(is_e, jnp.floor((n + (BLOCK - 1)) * (1.0 / BLOCK)), 0.0)
    er = lax.broadcasted_iota(I32, (LANES, LANES), 0)
    ec = lax.broadcasted_iota(I32, (LANES, LANES), 1)
    loc = jnp.dot(nblk.astype(BF16), (er < ec).astype(BF16), preferred_element_type=F32)
    ti = lax.broadcasted_iota(I32, (nt, nt), 0)
    tj = lax.broadcasted_iota(I32, (nt, nt), 1)
    carry = jnp.dot((tj < ti).astype(BF16), nblk.astype(BF16), preferred_element_type=F32)
    total = jnp.sum(nblk, axis=0, keepdims=True)
    tiles_e = jnp.floor((total + (bpt - 1)) * (1.0 / bpt))
    tile_end = jnp.dot(jnp.broadcast_to(tiles_e, (SUBLANES, LANES)).astype(BF16),
                       (er <= ec).astype(BF16), preferred_element_type=F32)[0:1, :]
    tile0 = tile_end - tiles_e
    n_used = jnp.max(tile_end, axis=1, keepdims=True)

    tau = lax.broadcasted_iota(I32, (tp, LANES), 0).astype(F32)
    lane_i = lax.broadcasted_iota(I32, (tp, LANES), 1)
    lane_t = lane_i.astype(F32)
    is_e_t = (lane_i >= ROUTE_LANE0) & (lane_i < ROUTE_LANE0 + N_EXPERTS)
    te = jnp.sum(jnp.where(is_e_t & (tile_end <= jnp.minimum(tau, n_used - 1.0)), 1.0, 0.0),
                 axis=1, keepdims=True)
    oh = lane_t == te + ROUTE_LANE0
    pick = lambda row: jnp.sum(jnp.where(oh, row, 0.0), axis=1, keepdims=True)
    seg0_t = pick(tile0) * bpt
    e_lane = lane_t - ROUTE_LANE0
    cand = jnp.where(is_e_t & (e_lane > te) & (tiles_e > 0.0), e_lane, 99.0)
    nxt = jnp.min(cand, axis=1, keepdims=True)
    nxt = jnp.where(nxt == 99.0, -1.0, nxt)
    rows_t = jnp.clip(pick(total) * BLOCK - (tau[:, 0:1] - pick(tile0)) * TR, ROW_STEP, TR)
    rows_t = jnp.ceil(rows_t * (1.0 / ROW_STEP)) * ROW_STEP
    meta = jnp.where(lane_i == 0, te, 0.0)
    meta = jnp.where(lane_i == 1, nxt, meta)
    meta = jnp.where(lane_i == 2, rows_t, meta)
    meta = jnp.where(lane_i == 3, n_used, meta)
    meta_ref[...] = meta.astype(I32)

    bidx = tau * bpt + lane_t
    off = bidx - seg0_t
    ohb = jnp.where(oh, 1.0, 0.0).astype(BF16)

    def per_tile(tab):
        hi = jnp.floor(tab * (1.0 / 32))
        lo = tab - 32.0 * hi
        nt_dot = lambda a: lax.dot_general(ohb, a.astype(BF16), (((1,), (1,)), ((), ())),
                                           preferred_element_type=F32)
        return 32.0 * nt_dot(hi) + nt_dot(lo)

    run_end = per_tile(carry + nblk)
    run_beg = per_tile(carry)
    run_loc = per_tile(loc)
    tile_of = jnp.zeros((tp, LANES), F32)
    for it in range(nt):
        tile_of = tile_of + jnp.where(run_end[:, it:it + 1] <= off, 1.0, 0.0)
    srcv = off
    for it in range(nt):
        srcv = srcv + jnp.where(tile_of == it,
                                run_loc[:, it:it + 1] - run_beg[:, it:it + 1] + it * kb, 0.0)
    valid = (tile_of < nt) & (bidx < n_used * bpt)
    slot3 = tau - N_SLOTS * jnp.floor((tau + 0.5) * (1.0 / N_SLOTS))
    dump = nt * kb + slot3 * bpt + lane_t
    src_ref[...] = jnp.where(valid, srcv, kb - 1.0).astype(I32)
    dst_ref[0:tp, :] = jnp.where(valid, srcv, dump).astype(I32)
    dst_ref[tp:tp + SUBLANES, :] = (nt * kb + DUMP_BLOCKS - bpt
                                    + lax.broadcasted_iota(I32, (SUBLANES, LANES), 1))
    used_ref[...] = jnp.broadcast_to(jnp.sum(nblk, axis=1, keepdims=True) * BLOCK,
                                     (nt, LANES)).astype(I32)


def _mixer_router_kernel(tiles_per_seq, nt,
                         x_ref, gmix_ref, win_f32, wconv_ref, gsgu_ref, wsp_ref, bsp_ref,
                         wout_f32, gffn_ref, wrt_f32, brt_ref,
                         x1_ref, route_ref, stage_ref, meta_ref, src_ref, dst_ref, used_ref,
                         wcat_s, tri_s, halo_s, mix_s, h2_s, win_ref, wout_ref, wrt_ref, n_s):
    i = pl.program_id(0)
    nch = TM // CHUNK

    @pl.when(i == 0)
    def _init():
        r = lax.broadcasted_iota(I32, (CHUNK, CHUNK), 0)
        c = lax.broadcasted_iota(I32, (CHUNK, CHUNK), 1)
        causal = c <= r
        for j in range(HEAD_PAIRS):
            wa = jnp.where(causal, wsp_ref[2 * j], 0.0)
            wb = jnp.where(causal, wsp_ref[2 * j + 1], 0.0)
            wcat_s[j] = jnp.concatenate([wa, wb], axis=1).astype(BF16)
        rr = lax.broadcasted_iota(I32, (TM, TM), 0)
        cc = lax.broadcasted_iota(I32, (TM, TM), 1)
        tri_s[...] = (rr < cc).astype(BF16)
        h2_s[...] = jnp.zeros_like(h2_s)
        win_ref[...] = win_f32[...].astype(BF16)
        wout_ref[...] = wout_f32[...].astype(BF16)
        wrt_ref[...] = wrt_f32[...].astype(BF16)

    @pl.when(i % tiles_per_seq == 0)
    def _seq_start():
        halo_s[...] = jnp.zeros_like(halo_s)

    def step(mix):
        h2b = h2_s[(i + 1) % 2]
        if mix:
            x = x_ref[...]
            hb = _rms(x, gmix_ref[...]).astype(BF16)

            def proj(k):
                return jnp.dot(hb, win_ref[:, k * 512:(k + 1) * 512], preferred_element_type=F32)

        logits = lax.dot_general(wrt_ref[...], h2b, (((1,), (1,)), ((), ())),
                                 preferred_element_type=F32)[0:ROUTE_ROWS, :] + brt_ref[0:ROUTE_ROWS, :]
        if mix:
            pu = proj(3)
            pv = proj(4)

        ridx = lax.broadcasted_iota(I32, (ROUTE_ROWS, TM), 0).astype(F32)
        neg = jnp.float32(-jnp.inf)
        big = jnp.float32(1e9)
        is_g = ridx < N_GROUPS
        gl = jnp.where(is_g, logits, neg)
        gmax = jnp.max(gl, axis=0, keepdims=True)
        gsum = jnp.sum(jnp.where(is_g, jnp.exp(gl - gmax), 0.0), axis=0, keepdims=True)
        g_w = 1.0 / gsum
        g_idx = jnp.min(jnp.where(gl == gmax, ridx, big), axis=0, keepdims=True)
        lo_row = ROUTE_LANE0 + EXPERTS_PER_GROUP * g_idx
        in_grp = (ridx >= lo_row) & (ridx < lo_row + EXPERTS_PER_GROUP)
        el = jnp.where(in_grp, logits, neg)
        v1 = jnp.max(el, axis=0, keepdims=True)
        i1 = jnp.min(jnp.where(el == v1, ridx, big), axis=0, keepdims=True)
        el2 = jnp.where(ridx == i1, neg, el)
        v2 = jnp.max(el2, axis=0, keepdims=True)
        i2 = jnp.min(jnp.where(el2 == v2, ridx, big), axis=0, keepdims=True)
        e21 = jnp.exp(v2 - v1)
        w1 = g_w / (1.0 + e21)
        w2 = g_w * e21 / (1.0 + e21)

        if mix:
            u = jax.nn.gelu(pu)
            zc = proj(1) * proj(2)
            v = jax.nn.gelu(pv)
            vc = v - jnp.mean(v, axis=-1, keepdims=True)
            vn = vc * lax.rsqrt(jnp.mean(vc * vc, axis=-1, keepdims=True) + EPS) * gsgu_ref[...]
            vnb = vn.astype(BF16)

            row = lax.broadcasted_iota(I32, (TM, CONV_WIDTH), 0)
            h6 = halo_s[6:7, :]
            h7 = halo_s[7:8, :]
            z1 = jnp.where(row == 0, h7, pltpu.roll(zc, 1, 0))
            z2 = jnp.where(row == 0, h6, jnp.where(row == 1, h7, pltpu.roll(zc, 2, 0)))
            conv = z2 * wconv_ref[0:1, :] + z1 * wconv_ref[1:2, :] + zc * wconv_ref[2:3, :]
            halo_s[...] = zc[TM - 8:TM, :]
            mix_s[:, 0:CONV_WIDTH] = (proj(0) * conv).astype(BF16)

        sel1 = ridx == i1
        sel2 = ridx == i2
        onehot = jnp.where(sel1 | sel2, 1.0, 0.0)
        counts = jnp.sum(onehot, axis=1, keepdims=True)
        pad_rows = jnp.zeros((LANES - ROUTE_ROWS, LANES), F32)
        counts_sq = jnp.concatenate([jnp.broadcast_to(counts, (ROUTE_ROWS, LANES)), pad_rows], axis=0)
        blocks_sq = jnp.ceil(counts_sq * (1.0 / BLOCK))
        er = lax.broadcasted_iota(I32, (LANES, LANES), 0)
        ec = lax.broadcasted_iota(I32, (LANES, LANES), 1)
        before = (ec < er).astype(BF16)
        run_start = BLOCK * jnp.dot(before, blocks_sq.astype(BF16),
                                    preferred_element_type=F32)[0:ROUTE_ROWS, 0:1]
        rank = jnp.dot(onehot.astype(BF16), tri_s[...], preferred_element_type=F32) + run_start
        pos1_row = jnp.sum(jnp.where(sel1, rank, 0.0), axis=0, keepdims=True)
        pos2_row = jnp.sum(jnp.where(sel2, rank, 0.0), axis=0, keepdims=True)
        n_s[pl.ds(jnp.where(i == 0, nt, i - 1), 1), :] = counts_sq.T[0:1, :]

        r8 = lax.broadcasted_iota(I32, (SUBLANES, TM), 0)
        rows8 = jnp.where(r8 == 0, i1 - ROUTE_LANE0, 0.0)
        rows8 = jnp.where(r8 == 1, i2 - ROUTE_LANE0, rows8)
        rows8 = jnp.where(r8 == 2, w1, rows8)
        rows8 = jnp.where(r8 == 3, w2, rows8)
        rows8 = jnp.where(r8 == 4, pos1_row, rows8)
        rows8 = jnp.where(r8 == 5, pos2_row, rows8)
        route_ref[...] = jnp.concatenate(
            [rows8, jnp.zeros((LANES - SUBLANES, TM), F32)], axis=0).T

        def sort_rows(lo, hi):
            out_row = (lax.broadcasted_iota(I32, (hi - lo, TM), 0) + lo).astype(F32)
            sort_mat = jnp.where((out_row == pos1_row) | (out_row == pos2_row), 1.0, 0.0).astype(BF16)
            stage_ref[lo:hi, :] = jnp.dot(sort_mat, h2b, preferred_element_type=F32).astype(BF16)

        sort_rows(0, SORT_MAIN // 2)
        if mix:
            left = lax.broadcasted_iota(I32, (CHUNK, LANES), 1) < (LANES // 2)
            zero = jnp.zeros((CHUNK, LANES), BF16)
            for j in range(HEAD_PAIRS):
                cols = []
                for c in range(nch):
                    s = vnb[c * CHUNK:(c + 1) * CHUNK, j * LANES:(j + 1) * LANES]
                    cols.append(jnp.concatenate([jnp.where(left, s, zero), jnp.where(left, zero, s)],
                                                axis=0))
                rhs = jnp.concatenate(cols, axis=1)
                res = jnp.dot(wcat_s[j], rhs, preferred_element_type=F32)
                bias = bsp_ref[:, j * LANES:(j + 1) * LANES]
                for c in range(nch):
                    mixed = res[:, c * LANES:(c + 1) * LANES] + bias
                    uu = u[c * CHUNK:(c + 1) * CHUNK, j * LANES:(j + 1) * LANES]
                    mix_s[c * CHUNK:(c + 1) * CHUNK,
                          CONV_WIDTH + j * LANES:CONV_WIDTH + (j + 1) * LANES] = (uu * mixed).astype(BF16)

        sort_rows(SORT_MAIN // 2, SORT_MAIN)
        if mix:
            x1 = x + jnp.dot(mix_s[...], wout_ref[...], preferred_element_type=F32)
            x1_ref[...] = x1
            h2_s[i % 2] = _rms(x1, gffn_ref[...]).astype(BF16)

        used_rows = BLOCK * jnp.sum(blocks_sq[:, 0:1])
        for lo in range(SORT_MAIN, KOUT, TAIL):
            @pl.when(used_rows > lo)
            def _tail(lo=lo):
                sort_rows(lo, lo + TAIL)

            @pl.when(used_rows <= lo)
            def _empty_tail(lo=lo):
                stage_ref[lo:lo + TAIL, :] = jnp.zeros((TAIL, D_MODEL), BF16)

    pl.when(i < nt)(functools.partial(step, True))

    @pl.when(i == nt)
    def _last():
        step(False)
        _build_tables(n_s, nt, meta_ref, src_ref, dst_ref, used_ref)


def _mixer_router(x2d, g_mix, w_in, w_conv, g_sgu, w_sp, b_sp, w_out, g_ffn, w_rt, b_rt, seq,
                  n_tiles):
    t = x2d.shape[0]
    nt = t // TM
    tp = -(-n_tiles // SUBLANES) * SUBLANES
    whole = lambda r: pl.BlockSpec((r, LANES), lambda i: (0, 0))
    full = lambda a: pl.BlockSpec(a.shape, lambda i: (0,) * a.ndim)
    ins = (g_mix, w_in, w_conv, g_sgu, w_sp, b_sp, w_out, g_ffn, w_rt, b_rt)
    prev = lambda i: (jnp.where(i == 0, nt, i - 1), 0)
    return pl.pallas_call(
        functools.partial(_mixer_router_kernel, seq // TM, nt),
        grid=(nt + 1,),
        in_specs=[pl.BlockSpec((TM, D_MODEL), lambda i: (jnp.minimum(i, nt - 1), 0))]
        + [full(a) for a in ins],
        out_specs=[pl.BlockSpec((TM, D_MODEL), lambda i: (jnp.minimum(i, nt - 1), 0)),
                   pl.BlockSpec((TM, LANES), prev),
                   pl.BlockSpec((KOUT, D_MODEL), prev),
                   whole(tp), whole(tp), whole(tp + SUBLANES), whole(nt)],
        out_shape=[jax.ShapeDtypeStruct((t, D_MODEL), F32),
                   jax.ShapeDtypeStruct(((nt + 1) * TM, LANES), F32),
                   jax.ShapeDtypeStruct(((nt + 1) * KOUT, D_MODEL), BF16),
                   jax.ShapeDtypeStruct((tp, LANES), I32),
                   jax.ShapeDtypeStruct((tp, LANES), I32),
                   jax.ShapeDtypeStruct((tp + SUBLANES, LANES), I32),
                   jax.ShapeDtypeStruct((nt, LANES), I32)],
        scratch_shapes=[pltpu.VMEM((HEAD_PAIRS, CHUNK, 2 * CHUNK), BF16),
                        pltpu.VMEM((TM, TM), BF16),
                        pltpu.VMEM((8, CONV_WIDTH), F32),
                        pltpu.VMEM((TM, D_MODEL), BF16),
                        pltpu.VMEM((2, TM, D_MODEL), BF16),
                        pltpu.VMEM(w_in.shape, BF16),
                        pltpu.VMEM(w_out.shape, BF16),
                        pltpu.VMEM(w_rt.shape, BF16),
                        pltpu.VMEM((nt + SUBLANES, LANES), F32)],
        compiler_params=pltpu.CompilerParams(dimension_semantics=("arbitrary",),
                                             vmem_limit_bytes=VMEM_LIMIT),
        name="mixer_router",
    )(x2d, *ins)


def _expert_kernel(meta_ref, src_ref, dst_ref, stage_ref,
                   wg_hbm, wu_hbm, wd_hbm, out_ref,
                   xbuf, obuf, wg_f, wu_f, wd_f, wg_s, wu_s, wd_s, sem, osem, wsem):
    i = pl.program_id(0)
    n_used = meta_ref[3]
    slot = i % N_SLOTS
    bpt = BLOCKS_PER_ROW_TILE
    dump_row = dst_ref.shape[0] // bpt - SUBLANES
    expert_of = lambda tile: meta_ref[tile * META_W]

    def gather_copy(seq, j):
        tile = jnp.minimum(seq, n_used - 1)
        src = pl.multiple_of(src_ref[tile * bpt + j] * BLOCK, BLOCK)
        slot_ = seq % N_SLOTS
        return pltpu.make_async_copy(stage_ref.at[pl.ds(src, BLOCK)],
                                     xbuf.at[slot_, pl.ds(j * BLOCK, BLOCK)], sem.at[slot_])

    def scatter_copy(tile, j):
        row = jnp.where(tile < 0, dump_row, tile)
        dst = pl.multiple_of(dst_ref[row * bpt + j] * BLOCK, BLOCK)
        slot_ = (tile + N_SLOTS) % N_SLOTS
        return pltpu.make_async_copy(obuf.at[slot_, pl.ds(j * BLOCK, BLOCK)],
                                     out_ref.at[pl.ds(dst, BLOCK)], osem.at[slot_])

    def each_block(fn):
        for j in range(BLOCKS_PER_ROW_TILE):
            fn(j)

    def weight_copies(e):
        return (pltpu.make_async_copy(wg_hbm.at[e], wg_f, wsem.at[0]),
                pltpu.make_async_copy(wu_hbm.at[e], wu_f, wsem.at[1]),
                pltpu.make_async_copy(wd_hbm.at[e], wd_f, wsem.at[2]))

    @pl.when(i == 0)
    def _first():
        each_block(lambda j: gather_copy(0, j).start())
        each_block(lambda j: gather_copy(1, j).start())
        for cp in weight_copies(expert_of(0)):
            cp.start(priority=WEIGHT_DMA_PRIORITY)
        obuf[...] = jnp.zeros_like(obuf)

    @pl.when((i >= 2) & (i - 3 < n_used))
    def _free_out_slot():
        each_block(lambda j: scatter_copy(i - 3, j).wait())

    @pl.when(i == n_used)
    def _after_last_tile():
        each_block(lambda j: gather_copy(n_used, j).wait())
        each_block(lambda j: gather_copy(n_used + 1, j).wait())
        each_block(lambda j: scatter_copy(n_used - 1, j).start())

    @pl.when(i < n_used)
    def _tile():
        e = expert_of(i)
        prev = expert_of(jnp.maximum(i - 1, 0))

        @pl.when((i == 0) | (e != prev))
        def _new_expert():
            for cp in weight_copies(e):
                cp.wait()
            wg_s[...] = wg_f[...].astype(BF16)
            wu_s[...] = wu_f[...].astype(BF16)
            wd_s[...] = wd_f[...].astype(BF16)
            nxt = meta_ref[i * META_W + 1]

            @pl.when(nxt >= 0)
            def _next_weights():
                for cp in weight_copies(nxt):
                    cp.start(priority=WEIGHT_DMA_PRIORITY)

        each_block(lambda j: gather_copy(i, j).wait())

        def mlp(rows):
            hb = xbuf[slot, 0:rows, :]
            a = jnp.dot(hb, wg_s[...], preferred_element_type=F32)
            b = jnp.dot(hb, wu_s[...], preferred_element_type=F32)
            each_block(lambda j: gather_copy(i + 2, j).start())
            each_block(lambda j: scatter_copy(i - 1, j).start())
            hid = (a * jax.nn.sigmoid(a) * b).astype(BF16)
            obuf[slot, 0:rows, :] = jnp.dot(hid, wd_s[...], preferred_element_type=F32).astype(BF16)

        rows_needed = meta_ref[i * META_W + 2]
        for rows in range(ROW_STEP, TR + 1, ROW_STEP):
            pl.when(rows_needed == rows)(functools.partial(mlp, rows))


def _experts(meta, src_block, dst_block, stage, w_gate, w_up, w_down, n_tiles):
    any_spec = pl.BlockSpec(memory_space=pl.ANY)
    return pl.pallas_call(
        _expert_kernel,
        grid_spec=pltpu.PrefetchScalarGridSpec(
            num_scalar_prefetch=3,
            grid=(n_tiles,),
            in_specs=[any_spec, any_spec, any_spec, any_spec],
            out_specs=any_spec,
            scratch_shapes=[pltpu.VMEM((N_SLOTS, TR, D_MODEL), BF16),
                            pltpu.VMEM((N_SLOTS, TR, D_MODEL), BF16),
                            pltpu.VMEM((D_MODEL, D_EXPERT), F32),
                            pltpu.VMEM((D_MODEL, D_EXPERT), F32),
                            pltpu.VMEM((D_EXPERT, D_MODEL), F32),
                            pltpu.VMEM((D_MODEL, D_EXPERT), BF16),
                            pltpu.VMEM((D_MODEL, D_EXPERT), BF16),
                            pltpu.VMEM((D_EXPERT, D_MODEL), BF16),
                            pltpu.SemaphoreType.DMA((N_SLOTS,)),
                            pltpu.SemaphoreType.DMA((N_SLOTS,)),
                            pltpu.SemaphoreType.DMA((3,))]),
        out_shape=jax.ShapeDtypeStruct(stage.shape, stage.dtype),
        input_output_aliases={3: 0},
        compiler_params=pltpu.CompilerParams(dimension_semantics=("arbitrary",),
                                             vmem_limit_bytes=VMEM_LIMIT),
        name="experts",
    )(meta, src_block, dst_block, stage, w_gate, w_up, w_down)


def _combine_kernel(used_ref, x1_ref, route_ref, p_ref, gple_ref, wpg_f32, wpp_f32, gfin_ref,
                    srt_ref, out_ref, y_s, wpg_ref, wpp_ref):
    i = pl.program_id(0)

    @pl.when(i == 0)
    def _init():
        wpg_ref[...] = wpg_f32[...].astype(BF16)
        wpp_ref[...] = wpp_f32[...].astype(BF16)

    def one_tile(k):
        t0 = k * TM
        s0 = k * KOUT
        used = used_ref[i * COMBINE_TILES + k]
        route = route_ref[t0:t0 + TM, :]
        w1, w2, pos1, pos2 = (route[:, c:c + 1] for c in (2, 3, 4, 5))

        def unsort(lo, hi):
            col = (lax.broadcasted_iota(I32, (TM, hi - lo), 1) + lo).astype(F32)
            mat = jnp.where(col == pos1, w1, jnp.where(col == pos2, w2, 0.0)).astype(BF16)
            return jnp.dot(mat, srt_ref[s0 + lo:s0 + hi, :], preferred_element_type=F32)

        def finish(y):
            pe = jnp.dot(p_ref[t0:t0 + TM, :].astype(BF16), wpp_ref[...],
                         preferred_element_type=F32)
            hrows = TM // 2
            for h in range(2):
                rs = slice(h * hrows, (h + 1) * hrows)
                ts = slice(t0 + h * hrows, t0 + (h + 1) * hrows)
                x2 = x1_ref[ts, :] + y[rs, :]
                hg = _rms(x2, gple_ref[...]).astype(BF16)
                gate = jax.nn.sigmoid(jnp.dot(hg, wpg_ref[...], preferred_element_type=F32))
                x3 = x2 + gate * pe[rs, :]
                out_ref[ts, :] = _rms(x3, gfin_ref[...])

        @pl.when(used <= SORT_MAIN)
        def _common():
            finish(unsort(0, SORT_MAIN))

        @pl.when(used > SORT_MAIN)
        def _with_tails():
            y_s[...] = unsort(0, SORT_MAIN)
            for lo in range(SORT_MAIN, KOUT, TAIL):
                @pl.when(used > lo)
                def _tail(lo=lo):
                    y_s[...] += unsort(lo, lo + TAIL)
            finish(y_s[...])

    for k in range(COMBINE_TILES):
        one_tile(k)


def _combine(used, x1, route, p2d, g_ple, w_pg, w_pp, g_fin, sorted_out):
    t = p2d.shape[0]
    tc = COMBINE_TILES * TM
    full = lambda a: pl.BlockSpec(a.shape, lambda i, *_: (0,) * a.ndim)
    return pl.pallas_call(
        _combine_kernel,
        grid_spec=pltpu.PrefetchScalarGridSpec(
            num_scalar_prefetch=1,
            grid=(t // tc,),
            in_specs=[pl.BlockSpec((tc, D_MODEL), lambda i, *_: (i, 0)),
                      pl.BlockSpec((tc, LANES), lambda i, *_: (i, 0)),
                      pl.BlockSpec((tc, PLE_DIM), lambda i, *_: (i, 0)),
                      full(g_ple), full(w_pg), full(w_pp), full(g_fin),
                      pl.BlockSpec((COMBINE_TILES * KOUT, D_MODEL), lambda i, *_: (i, 0))],
            out_specs=pl.BlockSpec((tc, D_MODEL), lambda i, *_: (i, 0)),
            scratch_shapes=[pltpu.VMEM((TM, D_MODEL), F32),
                            pltpu.VMEM(w_pg.shape, BF16),
                            pltpu.VMEM(w_pp.shape, BF16)]),
        out_shape=jax.ShapeDtypeStruct((t, D_MODEL), F32),
        compiler_params=pltpu.CompilerParams(dimension_semantics=("arbitrary",),
                                             vmem_limit_bytes=VMEM_LIMIT),
        name="combine_ple",
    )(used, x1, route, p2d, g_ple, w_pg, w_pp, g_fin, sorted_out)


def kernel(x, p, g_mix, w_in, w_conv, g_sgu, w_spatial, b_spatial, w_out, g_ffn, w_group, b_group,
           w_router, b_router, w_gate, w_up, w_down, g_ple, w_ple_gate, w_ple_proj, g_final):
    bsz, seq, d = x.shape
    t = bsz * seq
    assert w_in.shape[0] == 1, "single-layer block"
    assert d == D_MODEL and seq % TM == 0 and TM % CHUNK == 0
    nt = t // TM
    max_rows = TOP_K * t + nt * N_EXPERTS * (BLOCK - 1)
    n_tiles = (max_rows + N_EXPERTS * (TR - 1)) // TR + N_SLOTS
    l = 0

    w_rt = jnp.concatenate(
        [w_group[l], jnp.transpose(w_router[l], (1, 0, 2)).reshape(d, N_EXPERTS)], axis=1)
    w_rt = jnp.pad(w_rt, ((0, 0), (0, LANES - w_rt.shape[1]))).T
    b_rt = jnp.pad(jnp.concatenate([b_group[l], b_router[l].reshape(-1)]),
                   (0, LANES - N_GROUPS - N_EXPERTS)).reshape(LANES, 1)
    b_sp = jnp.repeat(b_spatial[l].T, SGU_WIDTH // SGU_HEADS, axis=1)

    x1, route, stage, meta, src, dst, used = _mixer_router(
        x.reshape(t, d), g_mix[l].reshape(1, d), w_in[l], w_conv[l],
        g_sgu[l].reshape(1, -1), w_spatial[l], b_sp, w_out[l],
        g_ffn[l].reshape(1, d), w_rt, b_rt, seq, n_tiles)

    bpt = BLOCKS_PER_ROW_TILE
    meta, src, dst, used = (meta[:, :META_W].reshape(-1), src[:, :bpt].reshape(-1),
                            dst[:, :bpt].reshape(-1), used[:, 0])
    sorted_out = _experts(meta, src, dst, stage,
                          w_gate[l].reshape(N_EXPERTS, d, D_EXPERT),
                          w_up[l].reshape(N_EXPERTS, d, D_EXPERT),
                          w_down[l].reshape(N_EXPERTS, D_EXPERT, d), n_tiles)
    out = _combine(used, x1, route, p[l].reshape(t, PLE_DIM), g_ple[l].reshape(1, d),
                   w_ple_gate[l], w_ple_proj[l], g_final.reshape(1, d), sorted_out)
    return out.reshape(bsz, seq, d)
```

```python
import functools

import jax
import jax.numpy as jnp
from jax import lax
from jax.experimental import pallas as pl
from jax.experimental.pallas import tpu as pltpu

F32 = jnp.float32
BF16 = jnp.bfloat16
I32 = jnp.int32

EPS = 1e-6
D_MODEL = 1024
CONV_WIDTH = 512
SGU_WIDTH = 512
SGU_HEADS = 8
HEAD_PAIRS = SGU_HEADS // 2
CHUNK = 128
N_GROUPS = 4
EXPERTS_PER_GROUP = 8
N_EXPERTS = N_GROUPS * EXPERTS_PER_GROUP
TOP_K = 2
D_EXPERT = 512
PLE_DIM = 256
LANES = 128
SUBLANES = 8
BLOCK = 2 * SUBLANES
ROUTE_LANE0 = N_GROUPS
ROUTE_ROWS = 40

TM = 512
TR = 512
BLOCKS_PER_ROW_TILE = TR // BLOCK
KOUT = TOP_K * TM + N_EXPERTS * BLOCK
TAIL = 128
SORT_MAIN = KOUT - 2 * TAIL
META_W = 4
ROW_STEP = 128
N_SLOTS = 3
DUMP_BLOCKS = N_SLOTS * BLOCKS_PER_ROW_TILE
assert DUMP_BLOCKS * BLOCK <= KOUT
WEIGHT_DMA_PRIORITY = 1
VMEM_LIMIT = 56 * 1024 * 1024


def _rms(x, g):
    return x * lax.rsqrt(jnp.mean(x * x, axis=-1, keepdims=True) + EPS) * g


def _build_tables(n_s, nt, meta_ref, src_ref, dst_ref, used_ref):
    tp = meta_ref.shape[0]
    bpt = BLOCKS_PER_ROW_TILE
    kb = KOUT // BLOCK
    n = n_s[0:nt, :]
    lane_n = lax.broadcasted_iota(I32, (nt, LANES), 1)
    is_e = (lane_n >= ROUTE_LANE0) & (lane_n < ROUTE_LANE0 + N_EXPERTS)
    nblk = jnp.where(is_e, jnp.floor((n + (BLOCK - 1)) * (1.0 / BLOCK)), 0.0)
    er = lax.broadcasted_iota(I32, (LANES, LANES), 0)
    ec = lax.broadcasted_iota(I32, (LANES, LANES), 1)
    loc = jnp.dot(nblk.astype(BF16), (er < ec).astype(BF16), preferred_element_type=F32)
    ti = lax.broadcasted_iota(I32, (nt, nt), 0)
    tj = lax.broadcasted_iota(I32, (nt, nt), 1)
    carry = jnp.dot((tj < ti).astype(BF16), nblk.astype(BF16), preferred_element_type=F32)
    total = jnp.sum(nblk, axis=0, keepdims=True)
    tiles_e = jnp.floor((total + (bpt - 1)) * (1.0 / bpt))
    tile_end = jnp.dot(jnp.broadcast_to(tiles_e, (SUBLANES, LANES)).astype(BF16),
                       (er <= ec).astype(BF16), preferred_element_type=F32)[0:1, :]
    tile0 = tile_end - tiles_e
    n_used = jnp.max(tile_end, axis=1, keepdims=True)

    tau = lax.broadcasted_iota(I32, (tp, LANES), 0).astype(F32)
    lane_i = lax.broadcasted_iota(I32, (tp, LANES), 1)
    lane_t = lane_i.astype(F32)
    is_e_t = (lane_i >= ROUTE_LANE0) & (lane_i < ROUTE_LANE0 + N_EXPERTS)
    te = jnp.sum(jnp.where(is_e_t & (tile_end <= jnp.minimum(tau, n_used - 1.0)), 1.0, 0.0),
                 axis=1, keepdims=True)
    oh = lane_t == te + ROUTE_LANE0
    pick = lambda row: jnp.sum(jnp.where(oh, row, 0.0), axis=1, keepdims=True)
    seg0_t = pick(tile0) * bpt
    e_lane = lane_t - ROUTE_LANE0
    cand = jnp.where(is_e_t & (e_lane > te) & (tiles_e > 0.0), e_lane, 99.0)
    nxt = jnp.min(cand, axis=1, keepdims=True)
    nxt = jnp.where(nxt == 99.0, -1.0, nxt)
    rows_t = jnp.clip(pick(total) * BLOCK - (tau[:, 0:1] - pick(tile0)) * TR, ROW_STEP, TR)
    rows_t = jnp.ceil(rows_t * (1.0 / ROW_STEP)) * ROW_STEP
    meta = jnp.where(lane_i == 0, te, 0.0)
    meta = jnp.where(lane_i == 1, nxt, meta)
    meta = jnp.where(lane_i == 2, rows_t, meta)
    meta = jnp.where(lane_i == 3, n_used, meta)
    meta_ref[...] = meta.astype(I32)

    bidx = tau * bpt + lane_t
    off = bidx - seg0_t
    ohb = jnp.where(oh, 1.0, 0.0).astype(BF16)

    def per_tile(tab):
        hi = jnp.floor(tab * (1.0 / 32))
        lo = tab - 32.0 * hi
        nt_dot = lambda a: lax.dot_general(ohb, a.astype(BF16), (((1,), (1,)), ((), ())),
                                           preferred_element_type=F32)
        return 32.0 * nt_dot(hi) + nt_dot(lo)

    run_end = per_tile(carry + nblk)
    run_beg = per_tile(carry)
    run_loc = per_tile(loc)
    tile_of = jnp.zeros((tp, LANES), F32)
    for it in range(nt):
        tile_of = tile_of + jnp.where(run_end[:, it:it + 1] <= off, 1.0, 0.0)
    srcv = off
    for it in range(nt):
        srcv = srcv + jnp.where(tile_of == it,
                                run_loc[:, it:it + 1] - run_beg[:, it:it + 1] + it * kb, 0.0)
    valid = (tile_of < nt) & (bidx < n_used * bpt)
    slot3 = tau - N_SLOTS * jnp.floor((tau + 0.5) * (1.0 / N_SLOTS))
    dump = nt * kb + slot3 * bpt + lane_t
    src_ref[...] = jnp.where(valid, srcv, kb - 1.0).astype(I32)
    dst_ref[0:tp, :] = jnp.where(valid, srcv, dump).astype(I32)
    dst_ref[tp:tp + SUBLANES, :] = (nt * kb + DUMP_BLOCKS - bpt
                                    + lax.broadcasted_iota(I32, (SUBLANES, LANES), 1))
    used_ref[...] = jnp.broadcast_to(jnp.sum(nblk, axis=1, keepdims=True) * BLOCK,
                                     (nt, LANES)).astype(I32)


def _mixer_router_kernel(tiles_per_seq, nt,
                         x_ref, gmix_ref, win_f32, wconv_ref, gsgu_ref, wsp_ref, bsp_ref,
                         wout_f32, gffn_ref, wrt_f32, brt_ref,
                         x1_ref, route_ref, stage_ref, meta_ref, src_ref, dst_ref, used_ref,
                         wcat_s, tri_s, halo_s, mix_s, h2_s, win_ref, wout_ref, wrt_ref, n_s):
    i = pl.program_id(0)
    nch = TM // CHUNK

    @pl.when(i == 0)
    def _init():
        r = lax.broadcasted_iota(I32, (CHUNK, CHUNK), 0)
        c = lax.broadcasted_iota(I32, (CHUNK, CHUNK), 1)
        causal = c <= r
        for j in range(HEAD_PAIRS):
            wa = jnp.where(causal, wsp_ref[2 * j], 0.0)
            wb = jnp.where(causal, wsp_ref[2 * j + 1], 0.0)
            wcat_s[j] = jnp.concatenate([wa, wb], axis=1).astype(BF16)
        rr = lax.broadcasted_iota(I32, (TM, TM), 0)
        cc = lax.broadcasted_iota(I32, (TM, TM), 1)
        tri_s[...] = (rr < cc).astype(BF16)
        h2_s[...] = jnp.zeros_like(h2_s)
        win_ref[...] = win_f32[...].astype(BF16)
        wout_ref[...] = wout_f32[...].astype(BF16)
        wrt_ref[...] = wrt_f32[...].astype(BF16)

    @pl.when(i % tiles_per_seq == 0)
    def _seq_start():
        halo_s[...] = jnp.zeros_like(halo_s)

    def step(mix):
        h2b = h2_s[(i + 1) % 2]
        if mix:
            x = x_ref[...]
            hb = _rms(x, gmix_ref[...]).astype(BF16)

            def proj(k):
                return jnp.dot(hb, win_ref[:, k * 512:(k + 1) * 512], preferred_element_type=F32)

        logits = lax.dot_general(wrt_ref[...], h2b, (((1,), (1,)), ((), ())),
                                 preferred_element_type=F32)[0:ROUTE_ROWS, :] + brt_ref[0:ROUTE_ROWS, :]
        if mix:
            pu = proj(3)
            pv = proj(4)

        ridx = lax.broadcasted_iota(I32, (ROUTE_ROWS, TM), 0).astype(F32)
        neg = jnp.float32(-jnp.inf)
        big = jnp.float32(1e9)
        is_g = ridx < N_GROUPS
        gl = jnp.where(is_g, logits, neg)
        gmax = jnp.max(gl, axis=0, keepdims=True)
        gsum = jnp.sum(jnp.where(is_g, jnp.exp(gl - gmax), 0.0), axis=0, keepdims=True)
        g_w = 1.0 / gsum
        g_idx = jnp.min(jnp.where(gl == gmax, ridx, big), axis=0, keepdims=True)
        lo_row = ROUTE_LANE0 + EXPERTS_PER_GROUP * g_idx
        in_grp = (ridx >= lo_row) & (ridx < lo_row + EXPERTS_PER_GROUP)
        el = jnp.where(in_grp, logits, neg)
        v1 = jnp.max(el, axis=0, keepdims=True)
        i1 = jnp.min(jnp.where(el == v1, ridx, big), axis=0, keepdims=True)
        el2 = jnp.where(ridx == i1, neg, el)
        v2 = jnp.max(el2, axis=0, keepdims=True)
        i2 = jnp.min(jnp.where(el2 == v2, ridx, big), axis=0, keepdims=True)
        e21 = jnp.exp(v2 - v1)
        w1 = g_w / (1.0 + e21)
        w2 = g_w * e21 / (1.0 + e21)

        if mix:
            u = jax.nn.gelu(pu)
            zc = proj(1) * proj(2)
            v = jax.nn.gelu(pv)
            vc = v - jnp.mean(v, axis=-1, keepdims=True)
            vn = vc * lax.rsqrt(jnp.mean(vc * vc, axis=-1, keepdims=True) + EPS) * gsgu_ref[...]
            vnb = vn.astype(BF16)

            row = lax.broadcasted_iota(I32, (TM, CONV_WIDTH), 0)
            h6 = halo_s[6:7, :]
            h7 = halo_s[7:8, :]
            z1 = jnp.where(row == 0, h7, pltpu.roll(zc, 1, 0))
            z2 = jnp.where(row == 0, h6, jnp.where(row == 1, h7, pltpu.roll(zc, 2, 0)))
            conv = z2 * wconv_ref[0:1, :] + z1 * wconv_ref[1:2, :] + zc * wconv_ref[2:3, :]
            halo_s[...] = zc[TM - 8:TM, :]
            mix_s[:, 0:CONV_WIDTH] = (proj(0) * conv).astype(BF16)

        sel1 = ridx == i1
        sel2 = ridx == i2
        onehot = jnp.where(sel1 | sel2, 1.0, 0.0)
        counts = jnp.sum(onehot, axis=1, keepdims=True)
        pad_rows = jnp.zeros((LANES - ROUTE_ROWS, LANES), F32)
        counts_sq = jnp.concatenate([jnp.broadcast_to(counts, (ROUTE_ROWS, LANES)), pad_rows], axis=0)
        blocks_sq = jnp.ceil(counts_sq * (1.0 / BLOCK))
        er = lax.broadcasted_iota(I32, (LANES, LANES), 0)
        ec = lax.broadcasted_iota(I32, (LANES, LANES), 1)
        before = (ec < er).astype(BF16)
        run_start = BLOCK * jnp.dot(before, blocks_sq.astype(BF16),
                                    preferred_element_type=F32)[0:ROUTE_ROWS, 0:1]
        rank = jnp.dot(onehot.astype(BF16), tri_s[...], preferred_element_type=F32) + run_start
        pos1_row = jnp.sum(jnp.where(sel1, rank, 0.0), axis=0, keepdims=True)
        pos2_row = jnp.sum(jnp.where(sel2, rank, 0.0), axis=0, keepdims=True)
        n_s[pl.ds(jnp.where(i == 0, nt, i - 1), 1), :] = counts_sq.T[0:1, :]

        r8 = lax.broadcasted_iota(I32, (SUBLANES, TM), 0)
        rows8 = jnp.where(r8 == 0, i1 - ROUTE_LANE0, 0.0)
        rows8 = jnp.where(r8 == 1, i2 - ROUTE_LANE0, rows8)
        rows8 = jnp.where(r8 == 2, w1, rows8)
        rows8 = jnp.where(r8 == 3, w2, rows8)
        rows8 = jnp.where(r8 == 4, pos1_row, rows8)
        rows8 = jnp.where(r8 == 5, pos2_row, rows8)
        route_ref[...] = jnp.concatenate(
            [rows8, jnp.zeros((LANES - SUBLANES, TM), F32)], axis=0).T

        def sort_rows(lo, hi):
            out_row = (lax.broadcasted_iota(I32, (hi - lo, TM), 0) + lo).astype(F32)
            sort_mat = jnp.where((out_row == pos1_row) | (out_row == pos2_row), 1.0, 0.0).astype(BF16)
            stage_ref[lo:hi, :] = jnp.dot(sort_mat, h2b, preferred_element_type=F32).astype(BF16)

        sort_rows(0, SORT_MAIN // 2)
        if mix:
            left = lax.broadcasted_iota(I32, (CHUNK, LANES), 1) < (LANES // 2)
            zero = jnp.zeros((CHUNK, LANES), BF16)
            for j in range(HEAD_PAIRS):
                cols = []
                for c in range(nch):
                    s = vnb[c * CHUNK:(c + 1) * CHUNK, j * LANES:(j + 1) * LANES]
                    cols.append(jnp.concatenate([jnp.where(left, s, zero), jnp.where(left, zero, s)],
                                                axis=0))
                rhs = jnp.concatenate(cols, axis=1)
                res = jnp.dot(wcat_s[j], rhs, preferred_element_type=F32)
                bias = bsp_ref[:, j * LANES:(j + 1) * LANES]
                for c in range(nch):
                    mixed = res[:, c * LANES:(c + 1) * LANES] + bias
                    uu = u[c * CHUNK:(c + 1) * CHUNK, j * LANES:(j + 1) * LANES]
                    mix_s[c * CHUNK:(c + 1) * CHUNK,
                          CONV_WIDTH + j * LANES:CONV_WIDTH + (j + 1) * LANES] = (uu * mixed).astype(BF16)

        sort_rows(SORT_MAIN // 2, SORT_MAIN)
        if mix:
            x1 = x + jnp.dot(mix_s[...], wout_ref[...], preferred_element_type=F32)
            x1_ref[...] = x1
            h2_s[i % 2] = _rms(x1, gffn_ref[...]).astype(BF16)

        used_rows = BLOCK * jnp.sum(blocks_sq[:, 0:1])
        for lo in range(SORT_MAIN, KOUT, TAIL):
            @pl.when(used_rows > lo)
            def _tail(lo=lo):
                sort_rows(lo, lo + TAIL)

            @pl.when(used_rows <= lo)
            def _empty_tail(lo=lo):
                stage_ref[lo:lo + TAIL, :] = jnp.zeros((TAIL, D_MODEL), BF16)

    pl.when(i < nt)(functools.partial(step, True))

    @pl.when(i == nt)
    def _last():
        step(False)
        _build_tables(n_s, nt, meta_ref, src_ref, dst_ref, used_ref)


def _mixer_router(x2d, g_mix, w_in, w_conv, g_sgu, w_sp, b_sp, w_out, g_ffn, w_rt, b_rt, seq,
                  n_tiles):
    t = x2d.shape[0]
    nt = t // TM
    tp = -(-n_tiles // SUBLANES) * SUBLANES
    whole = lambda r: pl.BlockSpec((r, LANES), lambda i: (0, 0))
    full = lambda a: pl.BlockSpec(a.shape, lambda i: (0,) * a.ndim)
    ins = (g_mix, w_in, w_conv, g_sgu, w_sp, b_sp, w_out, g_ffn, w_rt, b_rt)
    prev = lambda i: (jnp.where(i == 0, nt, i - 1), 0)
    return pl.pallas_call(
        functools.partial(_mixer_router_kernel, seq // TM, nt),
        grid=(nt + 1,),
        in_specs=[pl.BlockSpec((TM, D_MODEL), lambda i: (jnp.minimum(i, nt - 1), 0))]
        + [full(a) for a in ins],
        out_specs=[pl.BlockSpec((TM, D_MODEL), lambda i: (jnp.minimum(i, nt - 1), 0)),
                   pl.BlockSpec((TM, LANES), prev),
                   pl.BlockSpec((KOUT, D_MODEL), prev),
                   whole(tp), whole(tp), whole(tp + SUBLANES), whole(nt)],
        out_shape=[jax.ShapeDtypeStruct((t, D_MODEL), F32),
                   jax.ShapeDtypeStruct(((nt + 1) * TM, LANES), F32),
                   jax.ShapeDtypeStruct(((nt + 1) * KOUT, D_MODEL), BF16),
                   jax.ShapeDtypeStruct((tp, LANES), I32),
                   jax.ShapeDtypeStruct((tp, LANES), I32),
                   jax.ShapeDtypeStruct((tp + SUBLANES, LANES), I32),
                   jax.ShapeDtypeStruct((nt, LANES), I32)],
        scratch_shapes=[pltpu.VMEM((HEAD_PAIRS, CHUNK, 2 * CHUNK), BF16),
                        pltpu.VMEM((TM, TM), BF16),
                        pltpu.VMEM((8, CONV_WIDTH), F32),
                        pltpu.VMEM((TM, D_MODEL), BF16),
                        pltpu.VMEM((2, TM, D_MODEL), BF16),
                        pltpu.VMEM(w_in.shape, BF16),
                        pltpu.VMEM(w_out.shape, BF16),
                        pltpu.VMEM(w_rt.shape, BF16),
                        pltpu.VMEM((nt + SUBLANES, LANES), F32)],
        compiler_params=pltpu.CompilerParams(dimension_semantics=("arbitrary",),
                                             vmem_limit_bytes=VMEM_LIMIT),
        name="mixer_router",
    )(x2d, *ins)


def _expert_kernel(meta_ref, src_ref, dst_ref, stage_ref,
                   wg_hbm, wu_hbm, wd_hbm, out_ref,
                   xbuf, obuf, wg_f, wu_f, wd_f, wg_s, wu_s, wd_s, sem, osem, wsem):
    i = pl.program_id(0)
    n_used = meta_ref[3]
    slot = i % N_SLOTS
    bpt = BLOCKS_PER_ROW_TILE
    dump_row = dst_ref.shape[0] // bpt - SUBLANES
    expert_of = lambda tile: meta_ref[tile * META_W]

    def gather_copy(seq, j):
        tile = jnp.minimum(seq, n_used - 1)
        src = pl.multiple_of(src_ref[tile * bpt + j] * BLOCK, BLOCK)
        slot_ = seq % N_SLOTS
        return pltpu.make_async_copy(stage_ref.at[pl.ds(src, BLOCK)],
                                     xbuf.at[slot_, pl.ds(j * BLOCK, BLOCK)], sem.at[slot_])

    def scatter_copy(tile, j):
        row = jnp.where(tile < 0, dump_row, tile)
        dst = pl.multiple_of(dst_ref[row * bpt + j] * BLOCK, BLOCK)
        slot_ = (tile + N_SLOTS) % N_SLOTS
        return pltpu.make_async_copy(obuf.at[slot_, pl.ds(j * BLOCK, BLOCK)],
                                     out_ref.at[pl.ds(dst, BLOCK)], osem.at[slot_])

    def each_block(fn):
        for j in range(BLOCKS_PER_ROW_TILE):
            fn(j)

    def weight_copies(e):
        return (pltpu.make_async_copy(wg_hbm.at[e], wg_f, wsem.at[0]),
                pltpu.make_async_copy(wu_hbm.at[e], wu_f, wsem.at[1]),
                pltpu.make_async_copy(wd_hbm.at[e], wd_f, wsem.at[2]))

    @pl.when(i == 0)
    def _first():
        each_block(lambda j: gather_copy(0, j).start())
        each_block(lambda j: gather_copy(1, j).start())
        for cp in weight_copies(expert_of(0)):
            cp.start(priority=WEIGHT_DMA_PRIORITY)
        obuf[...] = jnp.zeros_like(obuf)

    @pl.when((i >= 2) & (i - 3 < n_used))
    def _free_out_slot():
        each_block(lambda j: scatter_copy(i - 3, j).wait())

    @pl.when(i == n_used)
    def _after_last_tile():
        each_block(lambda j: gather_copy(n_used, j).wait())
        each_block(lambda j: gather_copy(n_used + 1, j).wait())
        each_block(lambda j: scatter_copy(n_used - 1, j).start())

    @pl.when(i < n_used)
    def _tile():
        e = expert_of(i)
        prev = expert_of(jnp.maximum(i - 1, 0))

        @pl.when((i == 0) | (e != prev))
        def _new_expert():
            for cp in weight_copies(e):
                cp.wait()
            wg_s[...] = wg_f[...].astype(BF16)
            wu_s[...] = wu_f[...].astype(BF16)
            wd_s[...] = wd_f[...].astype(BF16)
            nxt = meta_ref[i * META_W + 1]

            @pl.when(nxt >= 0)
            def _next_weights():
                for cp in weight_copies(nxt):
                    cp.start(priority=WEIGHT_DMA_PRIORITY)

        each_block(lambda j: gather_copy(i, j).wait())

        def mlp(rows):
            hb = xbuf[slot, 0:rows, :]
            a = jnp.dot(hb, wg_s[...], preferred_element_type=F32)
            b = jnp.dot(hb, wu_s[...], preferred_element_type=F32)
            each_block(lambda j: gather_copy(i + 2, j).start())
            each_block(lambda j: scatter_copy(i - 1, j).start())
            hid = (a * jax.nn.sigmoid(a) * b).astype(BF16)
            obuf[slot, 0:rows, :] = jnp.dot(hid, wd_s[...], preferred_element_type=F32).astype(BF16)

        rows_needed = meta_ref[i * META_W + 2]
        for rows in range(ROW_STEP, TR + 1, ROW_STEP):
            pl.when(rows_needed == rows)(functools.partial(mlp, rows))


def _experts(meta, src_block, dst_block, stage, w_gate, w_up, w_down, n_tiles):
    any_spec = pl.BlockSpec(memory_space=pl.ANY)
    return pl.pallas_call(
        _expert_kernel,
        grid_spec=pltpu.PrefetchScalarGridSpec(
            num_scalar_prefetch=3,
            grid=(n_tiles,),
            in_specs=[any_spec, any_spec, any_spec, any_spec],
            out_specs=any_spec,
            scratch_shapes=[pltpu.VMEM((N_SLOTS, TR, D_MODEL), BF16),
                            pltpu.VMEM((N_SLOTS, TR, D_MODEL), BF16),
                            pltpu.VMEM((D_MODEL, D_EXPERT), F32),
                            pltpu.VMEM((D_MODEL, D_EXPERT), F32),
                            pltpu.VMEM((D_EXPERT, D_MODEL), F32),
                            pltpu.VMEM((D_MODEL, D_EXPERT), BF16),
                            pltpu.VMEM((D_MODEL, D_EXPERT), BF16),
                            pltpu.VMEM((D_EXPERT, D_MODEL), BF16),
                            pltpu.SemaphoreType.DMA((N_SLOTS,)),
                            pltpu.SemaphoreType.DMA((N_SLOTS,)),
                            pltpu.SemaphoreType.DMA((3,))]),
        out_shape=jax.ShapeDtypeStruct(stage.shape, stage.dtype),
        input_output_aliases={3: 0},
        compiler_params=pltpu.CompilerParams(dimension_semantics=("arbitrary",),
                                             vmem_limit_bytes=VMEM_LIMIT),
        name="experts",
    )(meta, src_block, dst_block, stage, w_gate, w_up, w_down)


def _combine_kernel(used_ref, x1_hbm, route_ref, p_ref, gple_ref, wpg_f32, wpp_f32, gfin_ref,
                    srt_hbm, out_ref, y_s, wpg_ref, wpp_ref, x1buf, sbuf, xsem, ssem):
    i = pl.program_id(0)
    n = pl.num_programs(0)
    slot = i % N_SLOTS

    def x1_copy(step):
        s = step % N_SLOTS
        return pltpu.make_async_copy(x1_hbm.at[pl.ds(pl.multiple_of(step * TM, TM), TM)],
                                     x1buf.at[s], xsem.at[s])

    def stage_copy(step, rows):
        s = step % N_SLOTS
        return pltpu.make_async_copy(srt_hbm.at[pl.ds(pl.multiple_of(step * KOUT, KOUT), rows)],
                                     sbuf.at[s, pl.ds(0, rows)], ssem.at[s])

    def start_inputs(step):
        x1_copy(step).start()

        @pl.when(used_ref[step] <= SORT_MAIN)
        def _main_rows():
            stage_copy(step, SORT_MAIN).start()

        @pl.when(used_ref[step] > SORT_MAIN)
        def _all_rows():
            stage_copy(step, KOUT).start()

    @pl.when(i == 0)
    def _init():
        start_inputs(i)

        @pl.when(n > 1)
        def _second():
            start_inputs(i + 1)

        wpg_ref[...] = wpg_f32[...].astype(BF16)
        wpp_ref[...] = wpp_f32[...].astype(BF16)

    @pl.when(i + 2 < n)
    def _prefetch():
        start_inputs(i + 2)

    route = route_ref[...]
    w1, w2, pos1, pos2 = (route[:, k:k + 1] for k in (2, 3, 4, 5))

    def unsort(lo, hi):
        col = (lax.broadcasted_iota(I32, (TM, hi - lo), 1) + lo).astype(F32)
        mat = jnp.where(col == pos1, w1, jnp.where(col == pos2, w2, 0.0)).astype(BF16)
        return jnp.dot(mat, sbuf[slot, lo:hi, :], preferred_element_type=F32)

    def finish(y):
        pe = jnp.dot(p_ref[...].astype(BF16), wpp_ref[...], preferred_element_type=F32)
        x1_copy(i).wait()
        hrows = TM // 2
        for h in range(2):
            rs = slice(h * hrows, (h + 1) * hrows)
            x2 = x1buf[slot, rs, :] + y[rs, :]
            hg = _rms(x2, gple_ref[...]).astype(BF16)
            gate = jax.nn.sigmoid(jnp.dot(hg, wpg_ref[...], preferred_element_type=F32))
            x3 = x2 + gate * pe[rs, :]
            out_ref[rs, :] = _rms(x3, gfin_ref[...])

    @pl.when(used_ref[i] <= SORT_MAIN)
    def _common():
        stage_copy(i, SORT_MAIN).wait()
        finish(unsort(0, SORT_MAIN))

    @pl.when(used_ref[i] > SORT_MAIN)
    def _with_tails():
        stage_copy(i, KOUT).wait()
        y_s[...] = unsort(0, SORT_MAIN)
        for lo in range(SORT_MAIN, KOUT, TAIL):
            @pl.when(used_ref[i] > lo)
            def _tail(lo=lo):
                y_s[...] += unsort(lo, lo + TAIL)
        finish(y_s[...])


def _combine(used, x1, route, p2d, g_ple, w_pg, w_pp, g_fin, sorted_out):
    t = p2d.shape[0]
    full = lambda a: pl.BlockSpec(a.shape, lambda i, *_: (0,) * a.ndim)
    return pl.pallas_call(
        _combine_kernel,
        grid_spec=pltpu.PrefetchScalarGridSpec(
            num_scalar_prefetch=1,
            grid=(t // TM,),
            in_specs=[pl.BlockSpec(memory_space=pl.ANY),
                      pl.BlockSpec((TM, LANES), lambda i, *_: (i, 0)),
                      pl.BlockSpec((TM, PLE_DIM), lambda i, *_: (i, 0)),
                      full(g_ple), full(w_pg), full(w_pp), full(g_fin),
                      pl.BlockSpec(memory_space=pl.ANY)],
            out_specs=pl.BlockSpec((TM, D_MODEL), lambda i, *_: (i, 0)),
            scratch_shapes=[pltpu.VMEM((TM, D_MODEL), F32),
                            pltpu.VMEM(w_pg.shape, BF16),
                            pltpu.VMEM(w_pp.shape, BF16),
                            pltpu.VMEM((N_SLOTS, TM, D_MODEL), F32),
                            pltpu.VMEM((N_SLOTS, KOUT, D_MODEL), BF16),
                            pltpu.SemaphoreType.DMA((N_SLOTS,)),
                            pltpu.SemaphoreType.DMA((N_SLOTS,))]),
        out_shape=jax.ShapeDtypeStruct((t, D_MODEL), F32),
        compiler_params=pltpu.CompilerParams(dimension_semantics=("arbitrary",),
                                             vmem_limit_bytes=VMEM_LIMIT),
        name="combine_ple",
    )(used, x1, route, p2d, g_ple, w_pg, w_pp, g_fin, sorted_out)


def kernel(x, p, g_mix, w_in, w_conv, g_sgu, w_spatial, b_spatial, w_out, g_ffn, w_group, b_group,
           w_router, b_router, w_gate, w_up, w_down, g_ple, w_ple_gate, w_ple_proj, g_final):
    bsz, seq, d = x.shape
    t = bsz * seq
    assert w_in.shape[0] == 1, "single-layer block"
    assert d == D_MODEL and seq % TM == 0 and TM % CHUNK == 0
    nt = t // TM
    max_rows = TOP_K * t + nt * N_EXPERTS * (BLOCK - 1)
    n_tiles = (max_rows + N_EXPERTS * (TR - 1)) // TR + N_SLOTS
    l = 0

    w_rt = jnp.concatenate(
        [w_group[l], jnp.transpose(w_router[l], (1, 0, 2)).reshape(d, N_EXPERTS)], axis=1)
    w_rt = jnp.pad(w_rt, ((0, 0), (0, LANES - w_rt.shape[1]))).T
    b_rt = jnp.pad(jnp.concatenate([b_group[l], b_router[l].reshape(-1)]),
                   (0, LANES - N_GROUPS - N_EXPERTS)).reshape(LANES, 1)
    b_sp = jnp.repeat(b_spatial[l].T, SGU_WIDTH // SGU_HEADS, axis=1)

    x1, route, stage, meta, src, dst, used = _mixer_router(
        x.reshape(t, d), g_mix[l].reshape(1, d), w_in[l], w_conv[l],
        g_sgu[l].reshape(1, -1), w_spatial[l], b_sp, w_out[l],
        g_ffn[l].reshape(1, d), w_rt, b_rt, seq, n_tiles)

    bpt = BLOCKS_PER_ROW_TILE
    meta, src, dst, used = (meta[:, :META_W].reshape(-1), src[:, :bpt].reshape(-1),
                            dst[:, :bpt].reshape(-1), used[:, 0])
    sorted_out = _experts(meta, src, dst, stage,
                          w_gate[l].reshape(N_EXPERTS, d, D_EXPERT),
                          w_up[l].reshape(N_EXPERTS, d, D_EXPERT),
                          w_down[l].reshape(N_EXPERTS, D_EXPERT, d), n_tiles)
    out = _combine(used, x1, route, p[l].reshape(t, PLE_DIM), g_ple[l].reshape(1, d),
                   w_ple_gate[l], w_ple_proj[l], g_final.reshape(1, d), sorted_out)
    return out.reshape(bsz, seq, d)
```

```python
import functools

import jax
import jax.numpy as jnp
from jax import lax
from jax.experimental import pallas as pl
from jax.experimental.pallas import tpu as pltpu

F32 = jnp.float32
BF16 = jnp.bfloat16
I32 = jnp.int32

EPS = 1e-6
D_MODEL = 1024
CONV_WIDTH = 512
SGU_WIDTH = 512
SGU_HEADS = 8
HEAD_PAIRS = SGU_HEADS // 2
CHUNK = 128
N_GROUPS = 4
EXPERTS_PER_GROUP = 8
N_EXPERTS = N_GROUPS * EXPERTS_PER_GROUP
TOP_K = 2
D_EXPERT = 512
PLE_DIM = 256
LANES = 128
SUBLANES = 8
BLOCK = 2 * SUBLANES
ROUTE_LANE0 = N_GROUPS
ROUTE_ROWS = 40

TM = 512
TR = 512
BLOCKS_PER_ROW_TILE = TR // BLOCK
KOUT = TOP_K * TM + N_EXPERTS * BLOCK
TAIL = 128
SORT_MAIN = KOUT - 2 * TAIL
META_W = 4
ROW_STEP = 128
N_SLOTS = 3
DUMP_BLOCKS = N_SLOTS * BLOCKS_PER_ROW_TILE
assert DUMP_BLOCKS * BLOCK <= KOUT
WEIGHT_DMA_PRIORITY = 1
VMEM_LIMIT = 56 * 1024 * 1024


def _rms(x, g):
    return x * lax.rsqrt(jnp.mean(x * x, axis=-1, keepdims=True) + EPS) * g


def _build_tables(n_s, nt, meta_ref, src_ref, dst_ref, used_ref):
    tp = meta_ref.shape[0]
    bpt = BLOCKS_PER_ROW_TILE
    kb = KOUT // BLOCK
    n = n_s[0:nt, :]
    lane_n = lax.broadcasted_iota(I32, (nt, LANES), 1)
    is_e = (lane_n >= ROUTE_LANE0) & (lane_n < ROUTE_LANE0 + N_EXPERTS)
    nblk = jnp.where(is_e, jnp.floor((n + (BLOCK - 1)) * (1.0 / BLOCK)), 0.0)
    er = lax.broadcasted_iota(I32, (LANES, LANES), 0)
    ec = lax.broadcasted_iota(I32, (LANES, LANES), 1)
    loc = jnp.dot(nblk.astype(BF16), (er < ec).astype(BF16), preferred_element_type=F32)
    ti = lax.broadcasted_iota(I32, (nt, nt), 0)
    tj = lax.broadcasted_iota(I32, (nt, nt), 1)
    carry = jnp.dot((tj < ti).astype(BF16), nblk.astype(BF16), preferred_element_type=F32)
    total = jnp.sum(nblk, axis=0, keepdims=True)
    tiles_e = jnp.floor((total + (bpt - 1)) * (1.0 / bpt))
    tile_end = jnp.dot(jnp.broadcast_to(tiles_e, (SUBLANES, LANES)).astype(BF16),
                       (er <= ec).astype(BF16), preferred_element_type=F32)[0:1, :]
    tile0 = tile_end - tiles_e
    n_used = jnp.max(tile_end, axis=1, keepdims=True)

    tau = lax.broadcasted_iota(I32, (tp, LANES), 0).astype(F32)
    lane_i = lax.broadcasted_iota(I32, (tp, LANES), 1)
    lane_t = lane_i.astype(F32)
    is_e_t = (lane_i >= ROUTE_LANE0) & (lane_i < ROUTE_LANE0 + N_EXPERTS)
    te = jnp.sum(jnp.where(is_e_t & (tile_end <= jnp.minimum(tau, n_used - 1.0)), 1.0, 0.0),
                 axis=1, keepdims=True)
    oh = lane_t == te + ROUTE_LANE0
    pick = lambda row: jnp.sum(jnp.where(oh, row, 0.0), axis=1, keepdims=True)
    seg0_t = pick(tile0) * bpt
    e_lane = lane_t - ROUTE_LANE0
    cand = jnp.where(is_e_t & (e_lane > te) & (tiles_e > 0.0), e_lane, 99.0)
    nxt = jnp.min(cand, axis=1, keepdims=True)
    nxt = jnp.where(nxt == 99.0, -1.0, nxt)
    rows_t = jnp.clip(pick(total) * BLOCK - (tau[:, 0:1] - pick(tile0)) * TR, ROW_STEP, TR)
    rows_t = jnp.ceil(rows_t * (1.0 / ROW_STEP)) * ROW_STEP
    meta = jnp.where(lane_i == 0, te, 0.0)
    meta = jnp.where(lane_i == 1, nxt, meta)
    meta = jnp.where(lane_i == 2, rows_t, meta)
    meta = jnp.where(lane_i == 3, n_used, meta)
    meta_ref[...] = meta.astype(I32)

    bidx = tau * bpt + lane_t
    off = bidx - seg0_t
    ohb = jnp.where(oh, 1.0, 0.0).astype(BF16)

    def per_tile(tab):
        hi = jnp.floor(tab * (1.0 / 32))
        lo = tab - 32.0 * hi
        nt_dot = lambda a: lax.dot_general(ohb, a.astype(BF16), (((1,), (1,)), ((), ())),
                                           preferred_element_type=F32)
        return 32.0 * nt_dot(hi) + nt_dot(lo)

    run_end = per_tile(carry + nblk)
    run_beg = per_tile(carry)
    run_loc = per_tile(loc)
    tile_of = jnp.zeros((tp, LANES), F32)
    for it in range(nt):
        tile_of = tile_of + jnp.where(run_end[:, it:it + 1] <= off, 1.0, 0.0)
    srcv = off
    for it in range(nt):
        srcv = srcv + jnp.where(tile_of == it,
                                run_loc[:, it:it + 1] - run_beg[:, it:it + 1] + it * kb, 0.0)
    valid = (tile_of < nt) & (bidx < n_used * bpt)
    slot3 = tau - N_SLOTS * jnp.floor((tau + 0.5) * (1.0 / N_SLOTS))
    dump = nt * kb + slot3 * bpt + lane_t
    src_ref[...] = jnp.where(valid, srcv, kb - 1.0).astype(I32)
    dst_ref[0:tp, :] = jnp.where(valid, srcv, dump).astype(I32)
    dst_ref[tp:tp + SUBLANES, :] = (nt * kb + DUMP_BLOCKS - bpt
                                    + lax.broadcasted_iota(I32, (SUBLANES, LANES), 1))
    used_ref[...] = jnp.broadcast_to(jnp.sum(nblk, axis=1, keepdims=True) * BLOCK,
                                     (nt, LANES)).astype(I32)


def _mixer_router_kernel(tiles_per_seq, nt,
                         x_ref, gmix_ref, win_f32, wconv_ref, gsgu_ref, wsp_ref, bsp_ref,
                         wout_f32, gffn_ref, wrt_f32, brt_ref,
                         x1_ref, route_ref, stage_ref, meta_ref, src_ref, dst_ref, used_ref,
                         wcat_s, tri_s, halo_s, mix_s, h2_s, win_ref, wout_ref, wrt_ref, n_s):
    i = pl.program_id(0)
    nch = TM // CHUNK

    @pl.when(i == 0)
    def _init():
        r = lax.broadcasted_iota(I32, (CHUNK, CHUNK), 0)
        c = lax.broadcasted_iota(I32, (CHUNK, CHUNK), 1)
        causal = c <= r
        for j in range(HEAD_PAIRS):
            wa = jnp.where(causal, wsp_ref[2 * j], 0.0)
            wb = jnp.where(causal, wsp_ref[2 * j + 1], 0.0)
            wcat_s[j] = jnp.concatenate([wa, wb], axis=1).astype(BF16)
        rr = lax.broadcasted_iota(I32, (TM, TM), 0)
        cc = lax.broadcasted_iota(I32, (TM, TM), 1)
        tri_s[...] = (rr < cc).astype(BF16)
        h2_s[...] = jnp.zeros_like(h2_s)
        win_ref[...] = win_f32[...].astype(BF16)
        wout_ref[...] = wout_f32[...].astype(BF16)
        wrt_ref[...] = wrt_f32[...].astype(BF16)

    @pl.when(i % tiles_per_seq == 0)
    def _seq_start():
        halo_s[...] = jnp.zeros_like(halo_s)

    def step(mix):
        h2b = h2_s[(i + 1) % 2]
        if mix:
            x = x_ref[...]
            hb = _rms(x, gmix_ref[...]).astype(BF16)

            def proj(k):
                return jnp.dot(hb, win_ref[:, k * 512:(k + 1) * 512], preferred_element_type=F32)

        logits = lax.dot_general(wrt_ref[...], h2b, (((1,), (1,)), ((), ())),
                                 preferred_element_type=F32)[0:ROUTE_ROWS, :] + brt_ref[0:ROUTE_ROWS, :]
        if mix:
            pu = proj(3)
            pv = proj(4)

        ridx = lax.broadcasted_iota(I32, (ROUTE_ROWS, TM), 0).astype(F32)
        neg = jnp.float32(-jnp.inf)
        big = jnp.float32(1e9)
        is_g = ridx < N_GROUPS
        gl = jnp.where(is_g, logits, neg)
        gmax = jnp.max(gl, axis=0, keepdims=True)
        gsum = jnp.sum(jnp.where(is_g, jnp.exp(gl - gmax), 0.0), axis=0, keepdims=True)
        g_w = 1.0 / gsum
        g_idx = jnp.min(jnp.where(gl == gmax, ridx, big), axis=0, keepdims=True)
        lo_row = ROUTE_LANE0 + EXPERTS_PER_GROUP * g_idx
        in_grp = (ridx >= lo_row) & (ridx < lo_row + EXPERTS_PER_GROUP)
        el = jnp.where(in_grp, logits, neg)
        v1 = jnp.max(el, axis=0, keepdims=True)
        i1 = jnp.min(jnp.where(el == v1, ridx, big), axis=0, keepdims=True)
        el2 = jnp.where(ridx == i1, neg, el)
        v2 = jnp.max(el2, axis=0, keepdims=True)
        i2 = jnp.min(jnp.where(el2 == v2, ridx, big), axis=0, keepdims=True)
        e21 = jnp.exp(v2 - v1)
        w1 = g_w / (1.0 + e21)
        w2 = g_w * e21 / (1.0 + e21)

        if mix:
            u = jax.nn.gelu(pu)
            zc = proj(1) * proj(2)
            v = jax.nn.gelu(pv)
            vc = v - jnp.mean(v, axis=-1, keepdims=True)
            vn = vc * lax.rsqrt(jnp.mean(vc * vc, axis=-1, keepdims=True) + EPS) * gsgu_ref[...]
            vnb = vn.astype(BF16)

            row = lax.broadcasted_iota(I32, (TM, CONV_WIDTH), 0)
            h6 = halo_s[6:7, :]
            h7 = halo_s[7:8, :]
            z1 = jnp.where(row == 0, h7, pltpu.roll(zc, 1, 0))
            z2 = jnp.where(row == 0, h6, jnp.where(row == 1, h7, pltpu.roll(zc, 2, 0)))
            conv = z2 * wconv_ref[0:1, :] + z1 * wconv_ref[1:2, :] + zc * wconv_ref[2:3, :]
            halo_s[...] = zc[TM - 8:TM, :]
            mix_s[:, 0:CONV_WIDTH] = (proj(0) * conv).astype(BF16)

        sel1 = ridx == i1
        sel2 = ridx == i2
        onehot = jnp.where(sel1 | sel2, 1.0, 0.0)
        counts = jnp.sum(onehot, axis=1, keepdims=True)
        pad_rows = jnp.zeros((LANES - ROUTE_ROWS, LANES), F32)
        counts_sq = jnp.concatenate([jnp.broadcast_to(counts, (ROUTE_ROWS, LANES)), pad_rows], axis=0)
        blocks_sq = jnp.ceil(counts_sq * (1.0 / BLOCK))
        er = lax.broadcasted_iota(I32, (LANES, LANES), 0)
        ec = lax.broadcasted_iota(I32, (LANES, LANES), 1)
        before = (ec < er).astype(BF16)
        run_start = BLOCK * jnp.dot(before, blocks_sq.astype(BF16),
                                    preferred_element_type=F32)[0:ROUTE_ROWS, 0:1]
        rank = jnp.dot(onehot.astype(BF16), tri_s[...], preferred_element_type=F32) + run_start
        pos1_row = jnp.sum(jnp.where(sel1, rank, 0.0), axis=0, keepdims=True)
        pos2_row = jnp.sum(jnp.where(sel2, rank, 0.0), axis=0, keepdims=True)
        n_s[pl.ds(jnp.where(i == 0, nt, i - 1), 1), :] = counts_sq.T[0:1, :]

        r8 = lax.broadcasted_iota(I32, (SUBLANES, TM), 0)
        rows8 = jnp.where(r8 == 0, i1 - ROUTE_LANE0, 0.0)
        rows8 = jnp.where(r8 == 1, i2 - ROUTE_LANE0, rows8)
        rows8 = jnp.where(r8 == 2, w1, rows8)
        rows8 = jnp.where(r8 == 3, w2, rows8)
        rows8 = jnp.where(r8 == 4, pos1_row, rows8)
        rows8 = jnp.where(r8 == 5, pos2_row, rows8)
        route_ref[...] = jnp.concatenate(
            [rows8, jnp.zeros((LANES - SUBLANES, TM), F32)], axis=0).T

        def sort_rows(lo, hi):
            out_row = (lax.broadcasted_iota(I32, (hi - lo, TM), 0) + lo).astype(F32)
            sort_mat = jnp.where((out_row == pos1_row) | (out_row == pos2_row), 1.0, 0.0).astype(BF16)
            stage_ref[lo:hi, :] = jnp.dot(sort_mat, h2b, preferred_element_type=F32).astype(BF16)

        sort_rows(0, SORT_MAIN // 2)
        if mix:
            left = lax.broadcasted_iota(I32, (CHUNK, LANES), 1) < (LANES // 2)
            zero = jnp.zeros((CHUNK, LANES), BF16)
            for j in range(HEAD_PAIRS):
                cols = []
                for c in range(nch):
                    s = vnb[c * CHUNK:(c + 1) * CHUNK, j * LANES:(j + 1) * LANES]
                    cols.append(jnp.concatenate([jnp.where(left, s, zero), jnp.where(left, zero, s)],
                                                axis=0))
                rhs = jnp.concatenate(cols, axis=1)
                res = jnp.dot(wcat_s[j], rhs, preferred_element_type=F32)
                bias = bsp_ref[:, j * LANES:(j + 1) * LANES]
                for c in range(nch):
                    mixed = res[:, c * LANES:(c + 1) * LANES] + bias
                    uu = u[c * CHUNK:(c + 1) * CHUNK, j * LANES:(j + 1) * LANES]
                    mix_s[c * CHUNK:(c + 1) * CHUNK,
                          CONV_WIDTH + j * LANES:CONV_WIDTH + (j + 1) * LANES] = (uu * mixed).astype(BF16)

        sort_rows(SORT_MAIN // 2, SORT_MAIN)
        if mix:
            x1 = x + jnp.dot(mix_s[...], wout_ref[...], preferred_element_type=F32)
            x1_ref[...] = x1
            h2_s[i % 2] = _rms(x1, gffn_ref[...]).astype(BF16)

        used_rows = BLOCK * jnp.sum(blocks_sq[:, 0:1])
        for lo in range(SORT_MAIN, KOUT, TAIL):
            @pl.when(used_rows > lo)
            def _tail(lo=lo):
                sort_rows(lo, lo + TAIL)

            @pl.when(used_rows <= lo)
            def _empty_tail(lo=lo):
                stage_ref[lo:lo + TAIL, :] = jnp.zeros((TAIL, D_MODEL), BF16)

    pl.when(i < nt)(functools.partial(step, True))

    @pl.when(i == nt)
    def _last():
        step(False)
        _build_tables(n_s, nt, meta_ref, src_ref, dst_ref, used_ref)


def _mixer_router(x2d, g_mix, w_in, w_conv, g_sgu, w_sp, b_sp, w_out, g_ffn, w_rt, b_rt, seq,
                  n_tiles):
    t = x2d.shape[0]
    nt = t // TM
    tp = -(-n_tiles // SUBLANES) * SUBLANES
    whole = lambda r: pl.BlockSpec((r, LANES), lambda i: (0, 0))
    full = lambda a: pl.BlockSpec(a.shape, lambda i: (0,) * a.ndim)
    ins = (g_mix, w_in, w_conv, g_sgu, w_sp, b_sp, w_out, g_ffn, w_rt, b_rt)
    prev = lambda i: (jnp.where(i == 0, nt, i - 1), 0)
    return pl.pallas_call(
        functools.partial(_mixer_router_kernel, seq // TM, nt),
        grid=(nt + 1,),
        in_specs=[pl.BlockSpec((TM, D_MODEL), lambda i: (jnp.minimum(i, nt - 1), 0))]
        + [full(a) for a in ins],
        out_specs=[pl.BlockSpec((TM, D_MODEL), lambda i: (jnp.minimum(i, nt - 1), 0)),
                   pl.BlockSpec((TM, LANES), prev),
                   pl.BlockSpec((KOUT, D_MODEL), prev),
                   whole(tp), whole(tp), whole(tp + SUBLANES), whole(nt)],
        out_shape=[jax.ShapeDtypeStruct((t, D_MODEL), F32),
                   jax.ShapeDtypeStruct(((nt + 1) * TM, LANES), F32),
                   jax.ShapeDtypeStruct(((nt + 1) * KOUT, D_MODEL), BF16),
                   jax.ShapeDtypeStruct((tp, LANES), I32),
                   jax.ShapeDtypeStruct((tp, LANES), I32),
                   jax.ShapeDtypeStruct((tp + SUBLANES, LANES), I32),
                   jax.ShapeDtypeStruct((nt, LANES), I32)],
        scratch_shapes=[pltpu.VMEM((HEAD_PAIRS, CHUNK, 2 * CHUNK), BF16),
                        pltpu.VMEM((TM, TM), BF16),
                        pltpu.VMEM((8, CONV_WIDTH), F32),
                        pltpu.VMEM((TM, D_MODEL), BF16),
                        pltpu.VMEM((2, TM, D_MODEL), BF16),
                        pltpu.VMEM(w_in.shape, BF16),
                        pltpu.VMEM(w_out.shape, BF16),
                        pltpu.VMEM(w_rt.shape, BF16),
                        pltpu.VMEM((nt + SUBLANES, LANES), F32)],
        compiler_params=pltpu.CompilerParams(dimension_semantics=("arbitrary",),
                                             vmem_limit_bytes=VMEM_LIMIT),
        name="mixer_router",
    )(x2d, *ins)


def _expert_kernel(meta_ref, src_ref, dst_ref, stage_ref,
                   wg_hbm, wu_hbm, wd_hbm, out_ref,
                   xbuf, obuf, wg_f, wu_f, wd_f, wg_s, wu_s, wd_s, sem, osem, wsem):
    i = pl.program_id(0)
    n_used = meta_ref[3]
    slot = i % N_SLOTS
    bpt = BLOCKS_PER_ROW_TILE
    dump_row = dst_ref.shape[0] // bpt - SUBLANES
    expert_of = lambda tile: meta_ref[tile * META_W]

    def gather_copy(seq, j):
        tile = jnp.minimum(seq, n_used - 1)
        src = pl.multiple_of(src_ref[tile * bpt + j] * BLOCK, BLOCK)
        slot_ = seq % N_SLOTS
        return pltpu.make_async_copy(stage_ref.at[pl.ds(src, BLOCK)],
                                     xbuf.at[slot_, pl.ds(j * BLOCK, BLOCK)], sem.at[slot_])

    def scatter_copy(tile, j):
        row = jnp.where(tile < 0, dump_row, tile)
        dst = pl.multiple_of(dst_ref[row * bpt + j] * BLOCK, BLOCK)
        slot_ = (tile + N_SLOTS) % N_SLOTS
        return pltpu.make_async_copy(obuf.at[slot_, pl.ds(j * BLOCK, BLOCK)],
                                     out_ref.at[pl.ds(dst, BLOCK)], osem.at[slot_])

    def each_block(fn):
        for j in range(BLOCKS_PER_ROW_TILE):
            fn(j)

    def weight_copies(e):
        return (pltpu.make_async_copy(wg_hbm.at[e], wg_f, wsem.at[0]),
                pltpu.make_async_copy(wu_hbm.at[e], wu_f, wsem.at[1]),
                pltpu.make_async_copy(wd_hbm.at[e], wd_f, wsem.at[2]))

    @pl.when(i == 0)
    def _first():
        each_block(lambda j: gather_copy(0, j).start())
        each_block(lambda j: gather_copy(1, j).start())
        for cp in weight_copies(expert_of(0)):
            cp.start(priority=WEIGHT_DMA_PRIORITY)
        obuf[...] = jnp.zeros_like(obuf)

    @pl.when((i >= 2) & (i - 3 < n_used))
    def _free_out_slot():
        each_block(lambda j: scatter_copy(i - 3, j).wait())

    @pl.when(i == n_used)
    def _after_last_tile():
        each_block(lambda j: gather_copy(n_used, j).wait())
        each_block(lambda j: gather_copy(n_used + 1, j).wait())
        each_block(lambda j: scatter_copy(n_used - 1, j).start())

    @pl.when(i < n_used)
    def _tile():
        e = expert_of(i)
        prev = expert_of(jnp.maximum(i - 1, 0))

        @pl.when((i == 0) | (e != prev))
        def _new_expert():
            for cp in weight_copies(e):
                cp.wait()
            wg_s[...] = wg_f[...].astype(BF16)
            wu_s[...] = wu_f[...].astype(BF16)
            wd_s[...] = wd_f[...].astype(BF16)
            nxt = meta_ref[i * META_W + 1]

            @pl.when(nxt >= 0)
            def _next_weights():
                for cp in weight_copies(nxt):
                    cp.start(priority=WEIGHT_DMA_PRIORITY)

        each_block(lambda j: gather_copy(i, j).wait())

        def mlp(rows):
            hb = xbuf[slot, 0:rows, :]
            a = jnp.dot(hb, wg_s[...], preferred_element_type=F32)
            b = jnp.dot(hb, wu_s[...], preferred_element_type=F32)
            each_block(lambda j: gather_copy(i + 2, j).start())
            each_block(lambda j: scatter_copy(i - 1, j).start())
            hid = (a * jax.nn.sigmoid(a) * b).astype(BF16)
            obuf[slot, 0:rows, :] = jnp.dot(hid, wd_s[...], preferred_element_type=F32).astype(BF16)

        rows_needed = meta_ref[i * META_W + 2]
        for rows in range(ROW_STEP, TR + 1, ROW_STEP):
            pl.when(rows_needed == rows)(functools.partial(mlp, rows))


def _experts(meta, src_block, dst_block, stage, w_gate, w_up, w_down, n_tiles):
    any_spec = pl.BlockSpec(memory_space=pl.ANY)
    return pl.pallas_call(
        _expert_kernel,
        grid_spec=pltpu.PrefetchScalarGridSpec(
            num_scalar_prefetch=3,
            grid=(n_tiles,),
            in_specs=[any_spec, any_spec, any_spec, any_spec],
            out_specs=any_spec,
            scratch_shapes=[pltpu.VMEM((N_SLOTS, TR, D_MODEL), BF16),
                            pltpu.VMEM((N_SLOTS, TR, D_MODEL), BF16),
                            pltpu.VMEM((D_MODEL, D_EXPERT), F32),
                            pltpu.VMEM((D_MODEL, D_EXPERT), F32),
                            pltpu.VMEM((D_EXPERT, D_MODEL), F32),
                            pltpu.VMEM((D_MODEL, D_EXPERT), BF16),
                            pltpu.VMEM((D_MODEL, D_EXPERT), BF16),
                            pltpu.VMEM((D_EXPERT, D_MODEL), BF16),
                            pltpu.SemaphoreType.DMA((N_SLOTS,)),
                            pltpu.SemaphoreType.DMA((N_SLOTS,)),
                            pltpu.SemaphoreType.DMA((3,))]),
        out_shape=jax.ShapeDtypeStruct(stage.shape, stage.dtype),
        input_output_aliases={3: 0},
        compiler_params=pltpu.CompilerParams(dimension_semantics=("arbitrary",),
                                             vmem_limit_bytes=VMEM_LIMIT),
        name="experts",
    )(meta, src_block, dst_block, stage, w_gate, w_up, w_down)


def _combine_kernel(used_ref, x1_hbm, route_ref, p_ref, gple_ref, wpg_f32, wpp_f32, gfin_ref,
                    srt_hbm, out_ref, y_s, wpg_ref, wpp_ref, x1buf, sbuf, xsem, ssem):
    i = pl.program_id(0)
    n = pl.num_programs(0)
    slot = i % N_SLOTS

    def x1_copy(step):
        s = step % N_SLOTS
        return pltpu.make_async_copy(x1_hbm.at[pl.ds(pl.multiple_of(step * TM, TM), TM)],
                                     x1buf.at[s], xsem.at[s])

    def stage_copy(step, rows):
        s = step % N_SLOTS
        return pltpu.make_async_copy(srt_hbm.at[pl.ds(pl.multiple_of(step * KOUT, KOUT), rows)],
                                     sbuf.at[s, pl.ds(0, rows)], ssem.at[s])

    def start_inputs(step):
        x1_copy(step).start()

        @pl.when(used_ref[step] <= SORT_MAIN)
        def _main_rows():
            stage_copy(step, SORT_MAIN).start()

        @pl.when(used_ref[step] > SORT_MAIN)
        def _all_rows():
            stage_copy(step, KOUT).start()

    @pl.when(i == 0)
    def _init():
        start_inputs(i)

        @pl.when(n > 1)
        def _second():
            start_inputs(i + 1)

        wpg_ref[...] = wpg_f32[...].astype(BF16)
        wpp_ref[...] = wpp_f32[...].astype(BF16)

    @pl.when(i + 2 < n)
    def _prefetch():
        start_inputs(i + 2)

    def body(k):
        route = route_ref[...]
        w1, w2, pos1, pos2 = (route[:, c:c + 1] for c in (2, 3, 4, 5))

        def unsort(lo, hi):
            col = (lax.broadcasted_iota(I32, (TM, hi - lo), 1) + lo).astype(F32)
            mat = jnp.where(col == pos1, w1, jnp.where(col == pos2, w2, 0.0)).astype(BF16)
            return jnp.dot(mat, sbuf[k, lo:hi, :], preferred_element_type=F32)

        def finish(y):
            pe = jnp.dot(p_ref[...].astype(BF16), wpp_ref[...], preferred_element_type=F32)
            x1_copy(i).wait()
            hrows = TM // 2
            for h in range(2):
                rs = slice(h * hrows, (h + 1) * hrows)
                x2 = x1buf[k, rs, :] + y[rs, :]
                hg = _rms(x2, gple_ref[...]).astype(BF16)
                gate = jax.nn.sigmoid(jnp.dot(hg, wpg_ref[...], preferred_element_type=F32))
                x3 = x2 + gate * pe[rs, :]
                out_ref[rs, :] = _rms(x3, gfin_ref[...])

        @pl.when(used_ref[i] <= SORT_MAIN)
        def _common():
            stage_copy(i, SORT_MAIN).wait()
            finish(unsort(0, SORT_MAIN))

        @pl.when(used_ref[i] > SORT_MAIN)
        def _with_tails():
            stage_copy(i, KOUT).wait()
            y_s[...] = unsort(0, SORT_MAIN)
            for lo in range(SORT_MAIN, KOUT, TAIL):
                @pl.when(used_ref[i] > lo)
                def _tail(lo=lo):
                    y_s[...] += unsort(lo, lo + TAIL)
            finish(y_s[...])

    for k in range(N_SLOTS):
        pl.when(slot == k)(functools.partial(body, k))


def _combine(used, x1, route, p2d, g_ple, w_pg, w_pp, g_fin, sorted_out):
    t = p2d.shape[0]
    full = lambda a: pl.BlockSpec(a.shape, lambda i, *_: (0,) * a.ndim)
    return pl.pallas_call(
        _combine_kernel,
        grid_spec=pltpu.PrefetchScalarGridSpec(
            num_scalar_prefetch=1,
            grid=(t // TM,),
            in_specs=[pl.BlockSpec(memory_space=pl.ANY),
                      pl.BlockSpec((TM, LANES), lambda i, *_: (i, 0)),
                      pl.BlockSpec((TM, PLE_DIM), lambda i, *_: (i, 0)),
                      full(g_ple), full(w_pg), full(w_pp), full(g_fin),
                      pl.BlockSpec(memory_space=pl.ANY)],
            out_specs=pl.BlockSpec((TM, D_MODEL), lambda i, *_: (i, 0)),
            scratch_shapes=[pltpu.VMEM((TM, D_MODEL), F32),
                            pltpu.VMEM(w_pg.shape, BF16),
                            pltpu.VMEM(w_pp.shape, BF16),
                            pltpu.VMEM((N_SLOTS, TM, D_MODEL), F32),
                            pltpu.VMEM((N_SLOTS, KOUT, D_MODEL), BF16),
                            pltpu.SemaphoreType.DMA((N_SLOTS,)),
                            pltpu.SemaphoreType.DMA((N_SLOTS,))]),
        out_shape=jax.ShapeDtypeStruct((t, D_MODEL), F32),
        compiler_params=pltpu.CompilerParams(dimension_semantics=("arbitrary",),
                                             vmem_limit_bytes=VMEM_LIMIT),
        name="combine_ple",
    )(used, x1, route, p2d, g_ple, w_pg, w_pp, g_fin, sorted_out)


def kernel(x, p, g_mix, w_in, w_conv, g_sgu, w_spatial, b_spatial, w_out, g_ffn, w_group, b_group,
           w_router, b_router, w_gate, w_up, w_down, g_ple, w_ple_gate, w_ple_proj, g_final):
    bsz, seq, d = x.shape
    t = bsz * seq
    assert w_in.shape[0] == 1, "single-layer block"
    assert d == D_MODEL and seq % TM == 0 and TM % CHUNK == 0
    nt = t // TM
    max_rows = TOP_K * t + nt * N_EXPERTS * (BLOCK - 1)
    n_tiles = (max_rows + N_EXPERTS * (TR - 1)) // TR + N_SLOTS
    l = 0

    w_rt = jnp.concatenate(
        [w_group[l], jnp.transpose(w_router[l], (1, 0, 2)).reshape(d, N_EXPERTS)], axis=1)
    w_rt = jnp.pad(w_rt, ((0, 0), (0, LANES - w_rt.shape[1]))).T
    b_rt = jnp.pad(jnp.concatenate([b_group[l], b_router[l].reshape(-1)]),
                   (0, LANES - N_GROUPS - N_EXPERTS)).reshape(LANES, 1)
    b_sp = jnp.repeat(b_spatial[l].T, SGU_WIDTH // SGU_HEADS, axis=1)

    x1, route, stage, meta, src, dst, used = _mixer_router(
        x.reshape(t, d), g_mix[l].reshape(1, d), w_in[l], w_conv[l],
        g_sgu[l].reshape(1, -1), w_spatial[l], b_sp, w_out[l],
        g_ffn[l].reshape(1, d), w_rt, b_rt, seq, n_tiles)

    bpt = BLOCKS_PER_ROW_TILE
    meta, src, dst, used = (meta[:, :META_W].reshape(-1), src[:, :bpt].reshape(-1),
                            dst[:, :bpt].reshape(-1), used[:, 0])
    sorted_out = _experts(meta, src, dst, stage,
                          w_gate[l].reshape(N_EXPERTS, d, D_EXPERT),
                          w_up[l].reshape(N_EXPERTS, d, D_EXPERT),
                          w_down[l].reshape(N_EXPERTS, D_EXPERT, d), n_tiles)
    out = _combine(used, x1, route, p[l].reshape(t, PLE_DIM), g_ple[l].reshape(1, d),
                   w_ple_gate[l], w_ple_proj[l], g_final.reshape(1, d), sorted_out)
    return out.reshape(bsz, seq, d)
```

```python
import functools

import jax
import jax.numpy as jnp
from jax import lax
from jax.experimental import pallas as pl
from jax.experimental.pallas import tpu as pltpu

F32 = jnp.float32
BF16 = jnp.bfloat16
I32 = jnp.int32

EPS = 1e-6
D_MODEL = 1024
CONV_WIDTH = 512
SGU_WIDTH = 512
SGU_HEADS = 8
HEAD_PAIRS = SGU_HEADS // 2
CHUNK = 128
N_GROUPS = 4
EXPERTS_PER_GROUP = 8
N_EXPERTS = N_GROUPS * EXPERTS_PER_GROUP
TOP_K = 2
D_EXPERT = 512
PLE_DIM = 256
LANES = 128
SUBLANES = 8
BLOCK = 2 * SUBLANES
ROUTE_LANE0 = N_GROUPS
ROUTE_ROWS = 40

TM = 512
TR = 512
BLOCKS_PER_ROW_TILE = TR // BLOCK
KOUT = TOP_K * TM + N_EXPERTS * BLOCK
TAIL = 128
SORT_MAIN = KOUT - 2 * TAIL
META_W = 4
ROW_STEP = 128
N_SLOTS = 3
DUMP_BLOCKS = N_SLOTS * BLOCKS_PER_ROW_TILE
assert DUMP_BLOCKS * BLOCK <= KOUT
WEIGHT_DMA_PRIORITY = 1
VMEM_LIMIT = 56 * 1024 * 1024


def _rms(x, g):
    return x * lax.rsqrt(jnp.mean(x * x, axis=-1, keepdims=True) + EPS) * g


def _build_tables(n_s, nt, meta_ref, src_ref, dst_ref, used_ref):
    tp = meta_ref.shape[0]
    bpt = BLOCKS_PER_ROW_TILE
    kb = KOUT // BLOCK
    n = n_s[0:nt, :]
    lane_n = lax.broadcasted_iota(I32, (nt, LANES), 1)
    is_e = (lane_n >= ROUTE_LANE0) & (lane_n < ROUTE_LANE0 + N_EXPERTS)
    nblk = jnp.where(is_e, jnp.floor((n + (BLOCK - 1)) * (1.0 / BLOCK)), 0.0)
    er = lax.broadcasted_iota(I32, (LANES, LANES), 0)
    ec = lax.broadcasted_iota(I32, (LANES, LANES), 1)
    loc = jnp.dot(nblk.astype(BF16), (er < ec).astype(BF16), preferred_element_type=F32)
    ti = lax.broadcasted_iota(I32, (nt, nt), 0)
    tj = lax.broadcasted_iota(I32, (nt, nt), 1)
    carry = jnp.dot((tj < ti).astype(BF16), nblk.astype(BF16), preferred_element_type=F32)
    total = jnp.sum(nblk, axis=0, keepdims=True)
    tiles_e = jnp.floor((total + (bpt - 1)) * (1.0 / bpt))
    tile_end = jnp.dot(jnp.broadcast_to(tiles_e, (SUBLANES, LANES)).astype(BF16),
                       (er <= ec).astype(BF16), preferred_element_type=F32)[0:1, :]
    tile0 = tile_end - tiles_e
    n_used = jnp.max(tile_end, axis=1, keepdims=True)

    tau = lax.broadcasted_iota(I32, (tp, LANES), 0).astype(F32)
    lane_i = lax.broadcasted_iota(I32, (tp, LANES), 1)
    lane_t = lane_i.astype(F32)
    is_e_t = (lane_i >= ROUTE_LANE0) & (lane_i < ROUTE_LANE0 + N_EXPERTS)
    te = jnp.sum(jnp.where(is_e_t & (tile_end <= jnp.minimum(tau, n_used - 1.0)), 1.0, 0.0),
                 axis=1, keepdims=True)
    oh = lane_t == te + ROUTE_LANE0
    pick = lambda row: jnp.sum(jnp.where(oh, row, 0.0), axis=1, keepdims=True)
    seg0_t = pick(tile0) * bpt
    e_lane = lane_t - ROUTE_LANE0
    cand = jnp.where(is_e_t & (e_lane > te) & (tiles_e > 0.0), e_lane, 99.0)
    nxt = jnp.min(cand, axis=1, keepdims=True)
    nxt = jnp.where(nxt == 99.0, -1.0, nxt)
    rows_t = jnp.clip(pick(total) * BLOCK - (tau[:, 0:1] - pick(tile0)) * TR, ROW_STEP, TR)
    rows_t = jnp.ceil(rows_t * (1.0 / ROW_STEP)) * ROW_STEP
    meta = jnp.where(lane_i == 0, te, 0.0)
    meta = jnp.where(lane_i == 1, nxt, meta)
    meta = jnp.where(lane_i == 2, rows_t, meta)
    meta = jnp.where(lane_i == 3, n_used, meta)
    meta_ref[...] = meta.astype(I32)

    bidx = tau * bpt + lane_t
    off = bidx - seg0_t
    ohb = jnp.where(oh, 1.0, 0.0).astype(BF16)

    def per_tile(tab):
        hi = jnp.floor(tab * (1.0 / 32))
        lo = tab - 32.0 * hi
        nt_dot = lambda a: lax.dot_general(ohb, a.astype(BF16), (((1,), (1,)), ((), ())),
                                           preferred_element_type=F32)
        return 32.0 * nt_dot(hi) + nt_dot(lo)

    run_end = per_tile(carry + nblk)
    run_beg = per_tile(carry)
    run_loc = per_tile(loc)
    tile_of = jnp.zeros((tp, LANES), F32)
    for it in range(nt):
        tile_of = tile_of + jnp.where(run_end[:, it:it + 1] <= off, 1.0, 0.0)
    srcv = off
    for it in range(nt):
        srcv = srcv + jnp.where(tile_of == it,
                                run_loc[:, it:it + 1] - run_beg[:, it:it + 1] + it * kb, 0.0)
    valid = (tile_of < nt) & (bidx < n_used * bpt)
    slot3 = tau - N_SLOTS * jnp.floor((tau + 0.5) * (1.0 / N_SLOTS))
    dump = nt * kb + slot3 * bpt + lane_t
    src_ref[...] = jnp.where(valid, srcv, kb - 1.0).astype(I32)
    dst_ref[0:tp, :] = jnp.where(valid, srcv, dump).astype(I32)
    dst_ref[tp:tp + SUBLANES, :] = (nt * kb + DUMP_BLOCKS - bpt
                                    + lax.broadcasted_iota(I32, (SUBLANES, LANES), 1))
    used_ref[...] = jnp.broadcast_to(jnp.sum(nblk, axis=1, keepdims=True) * BLOCK,
                                     (nt, LANES)).astype(I32)


def _mixer_router_kernel(tiles_per_seq, nt,
                         x_ref, gmix_ref, win_f32, wconv_ref, gsgu_ref, wsp_ref, bsp_ref,
                         wout_f32, gffn_ref, wrt_f32, brt_ref,
                         x1_ref, route_ref, stage_ref, meta_ref, src_ref, dst_ref, used_ref,
                         wcat_s, tri_s, halo_s, mix_s, h2_s, win_ref, wout_ref, wrt_ref, n_s):
    i = pl.program_id(0)
    nch = TM // CHUNK

    @pl.when(i == 0)
    def _init():
        r = lax.broadcasted_iota(I32, (CHUNK, CHUNK), 0)
        c = lax.broadcasted_iota(I32, (CHUNK, CHUNK), 1)
        causal = c <= r
        for j in range(HEAD_PAIRS):
            wa = jnp.where(causal, wsp_ref[2 * j], 0.0)
            wb = jnp.where(causal, wsp_ref[2 * j + 1], 0.0)
            wcat_s[j] = jnp.concatenate([wa, wb], axis=1).astype(BF16)
        rr = lax.broadcasted_iota(I32, (TM, TM), 0)
        cc = lax.broadcasted_iota(I32, (TM, TM), 1)
        tri_s[...] = (rr < cc).astype(BF16)
        h2_s[...] = jnp.zeros_like(h2_s)
        win_ref[...] = win_f32[...].astype(BF16)
        wout_ref[...] = wout_f32[...].astype(BF16)
        wrt_ref[...] = wrt_f32[...].astype(BF16)

    @pl.when(i % tiles_per_seq == 0)
    def _seq_start():
        halo_s[...] = jnp.zeros_like(halo_s)

    def step(mix):
        h2b = h2_s[(i + 1) % 2]
        if mix:
            x = x_ref[...]
            hb = _rms(x, gmix_ref[...]).astype(BF16)

            def proj(k):
                return jnp.dot(hb, win_ref[:, k * 512:(k + 1) * 512], preferred_element_type=F32)

        logits = lax.dot_general(wrt_ref[...], h2b, (((1,), (1,)), ((), ())),
                                 preferred_element_type=F32)[0:ROUTE_ROWS, :] + brt_ref[0:ROUTE_ROWS, :]
        if mix:
            pu = proj(3)
            pv = proj(4)

        ridx = lax.broadcasted_iota(I32, (ROUTE_ROWS, TM), 0).astype(F32)
        neg = jnp.float32(-jnp.inf)
        big = jnp.float32(1e9)
        is_g = ridx < N_GROUPS
        gl = jnp.where(is_g, logits, neg)
        gmax = jnp.max(gl, axis=0, keepdims=True)
        gsum = jnp.sum(jnp.where(is_g, jnp.exp(gl - gmax), 0.0), axis=0, keepdims=True)
        g_w = 1.0 / gsum
        g_idx = jnp.min(jnp.where(gl == gmax, ridx, big), axis=0, keepdims=True)
        lo_row = ROUTE_LANE0 + EXPERTS_PER_GROUP * g_idx
        in_grp = (ridx >= lo_row) & (ridx < lo_row + EXPERTS_PER_GROUP)
        el = jnp.where(in_grp, logits, neg)
        v1 = jnp.max(el, axis=0, keepdims=True)
        i1 = jnp.min(jnp.where(el == v1, ridx, big), axis=0, keepdims=True)
        el2 = jnp.where(ridx == i1, neg, el)
        v2 = jnp.max(el2, axis=0, keepdims=True)
        i2 = jnp.min(jnp.where(el2 == v2, ridx, big), axis=0, keepdims=True)
        e21 = jnp.exp(v2 - v1)
        w1 = g_w / (1.0 + e21)
        w2 = g_w * e21 / (1.0 + e21)

        if mix:
            u = jax.nn.gelu(pu)
            zc = proj(1) * proj(2)
            v = jax.nn.gelu(pv)
            vc = v - jnp.mean(v, axis=-1, keepdims=True)
            vn = vc * lax.rsqrt(jnp.mean(vc * vc, axis=-1, keepdims=True) + EPS) * gsgu_ref[...]
            vnb = vn.astype(BF16)

            row = lax.broadcasted_iota(I32, (TM, CONV_WIDTH), 0)
            h6 = halo_s[6:7, :]
            h7 = halo_s[7:8, :]
            z1 = jnp.where(row == 0, h7, pltpu.roll(zc, 1, 0))
            z2 = jnp.where(row == 0, h6, jnp.where(row == 1, h7, pltpu.roll(zc, 2, 0)))
            conv = z2 * wconv_ref[0:1, :] + z1 * wconv_ref[1:2, :] + zc * wconv_ref[2:3, :]
            halo_s[...] = zc[TM - 8:TM, :]
            mix_s[:, 0:CONV_WIDTH] = (proj(0) * conv).astype(BF16)

        sel1 = ridx == i1
        sel2 = ridx == i2
        onehot = jnp.where(sel1 | sel2, 1.0, 0.0)
        counts = jnp.sum(onehot, axis=1, keepdims=True)
        pad_rows = jnp.zeros((LANES - ROUTE_ROWS, LANES), F32)
        counts_sq = jnp.concatenate([jnp.broadcast_to(counts, (ROUTE_ROWS, LANES)), pad_rows], axis=0)
        blocks_sq = jnp.ceil(counts_sq * (1.0 / BLOCK))
        er = lax.broadcasted_iota(I32, (LANES, LANES), 0)
        ec = lax.broadcasted_iota(I32, (LANES, LANES), 1)
        before = (ec < er).astype(BF16)
        run_start = BLOCK * jnp.dot(before, blocks_sq.astype(BF16),
                                    preferred_element_type=F32)[0:ROUTE_ROWS, 0:1]
        rank = jnp.dot(onehot.astype(BF16), tri_s[...], preferred_element_type=F32) + run_start
        pos1_row = jnp.sum(jnp.where(sel1, rank, 0.0), axis=0, keepdims=True)
        pos2_row = jnp.sum(jnp.where(sel2, rank, 0.0), axis=0, keepdims=True)
        n_s[pl.ds(jnp.where(i == 0, nt, i - 1), 1), :] = counts_sq.T[0:1, :]

        r8 = lax.broadcasted_iota(I32, (SUBLANES, TM), 0)
        rows8 = jnp.where(r8 == 0, i1 - ROUTE_LANE0, 0.0)
        rows8 = jnp.where(r8 == 1, i2 - ROUTE_LANE0, rows8)
        rows8 = jnp.where(r8 == 2, w1, rows8)
        rows8 = jnp.where(r8 == 3, w2, rows8)
        rows8 = jnp.where(r8 == 4, pos1_row, rows8)
        rows8 = jnp.where(r8 == 5, pos2_row, rows8)
        route_ref[...] = jnp.concatenate(
            [rows8, jnp.zeros((LANES - SUBLANES, TM), F32)], axis=0).T

        def sort_rows(lo, hi):
            out_row = (lax.broadcasted_iota(I32, (hi - lo, TM), 0) + lo).astype(F32)
            sort_mat = jnp.where((out_row == pos1_row) | (out_row == pos2_row), 1.0, 0.0).astype(BF16)
            stage_ref[lo:hi, :] = jnp.dot(sort_mat, h2b, preferred_element_type=F32).astype(BF16)

        sort_rows(0, SORT_MAIN // 2)
        if mix:
            left = lax.broadcasted_iota(I32, (CHUNK, LANES), 1) < (LANES // 2)
            zero = jnp.zeros((CHUNK, LANES), BF16)
            for j in range(HEAD_PAIRS):
                cols = []
                for c in range(nch):
                    s = vnb[c * CHUNK:(c + 1) * CHUNK, j * LANES:(j + 1) * LANES]
                    cols.append(jnp.concatenate([jnp.where(left, s, zero), jnp.where(left, zero, s)],
                                                axis=0))
                rhs = jnp.concatenate(cols, axis=1)
                res = jnp.dot(wcat_s[j], rhs, preferred_element_type=F32)
                bias = bsp_ref[:, j * LANES:(j + 1) * LANES]
                for c in range(nch):
                    mixed = res[:, c * LANES:(c + 1) * LANES] + bias
                    uu = u[c * CHUNK:(c + 1) * CHUNK, j * LANES:(j + 1) * LANES]
                    mix_s[c * CHUNK:(c + 1) * CHUNK,
                          CONV_WIDTH + j * LANES:CONV_WIDTH + (j + 1) * LANES] = (uu * mixed).astype(BF16)

        sort_rows(SORT_MAIN // 2, SORT_MAIN)
        if mix:
            x1 = x + jnp.dot(mix_s[...], wout_ref[...], preferred_element_type=F32)
            x1_ref[...] = x1
            h2_s[i % 2] = _rms(x1, gffn_ref[...]).astype(BF16)

        used_rows = BLOCK * jnp.sum(blocks_sq[:, 0:1])
        for lo in range(SORT_MAIN, KOUT, TAIL):
            @pl.when(used_rows > lo)
            def _tail(lo=lo):
                sort_rows(lo, lo + TAIL)

            @pl.when(used_rows <= lo)
            def _empty_tail(lo=lo):
                stage_ref[lo:lo + TAIL, :] = jnp.zeros((TAIL, D_MODEL), BF16)

    pl.when(i < nt)(functools.partial(step, True))

    @pl.when(i == nt)
    def _last():
        step(False)
        _build_tables(n_s, nt, meta_ref, src_ref, dst_ref, used_ref)


def _mixer_router(x2d, g_mix, w_in, w_conv, g_sgu, w_sp, b_sp, w_out, g_ffn, w_rt, b_rt, seq,
                  n_tiles):
    t = x2d.shape[0]
    nt = t // TM
    tp = -(-n_tiles // SUBLANES) * SUBLANES
    whole = lambda r: pl.BlockSpec((r, LANES), lambda i: (0, 0))
    full = lambda a: pl.BlockSpec(a.shape, lambda i: (0,) * a.ndim)
    ins = (g_mix, w_in, w_conv, g_sgu, w_sp, b_sp, w_out, g_ffn, w_rt, b_rt)
    prev = lambda i: (jnp.where(i == 0, nt, i - 1), 0)
    return pl.pallas_call(
        functools.partial(_mixer_router_kernel, seq // TM, nt),
        grid=(nt + 1,),
        in_specs=[pl.BlockSpec((TM, D_MODEL), lambda i: (jnp.minimum(i, nt - 1), 0))]
        + [full(a) for a in ins],
        out_specs=[pl.BlockSpec((TM, D_MODEL), lambda i: (jnp.minimum(i, nt - 1), 0)),
                   pl.BlockSpec((TM, LANES), prev),
                   pl.BlockSpec((KOUT, D_MODEL), prev),
                   whole(tp), whole(tp), whole(tp + SUBLANES), whole(nt)],
        out_shape=[jax.ShapeDtypeStruct((t, D_MODEL), F32),
                   jax.ShapeDtypeStruct(((nt + 1) * TM, LANES), F32),
                   jax.ShapeDtypeStruct(((nt + 1) * KOUT, D_MODEL), BF16),
                   jax.ShapeDtypeStruct((tp, LANES), I32),
                   jax.ShapeDtypeStruct((tp, LANES), I32),
                   jax.ShapeDtypeStruct((tp + SUBLANES, LANES), I32),
                   jax.ShapeDtypeStruct((nt, LANES), I32)],
        scratch_shapes=[pltpu.VMEM((HEAD_PAIRS, CHUNK, 2 * CHUNK), BF16),
                        pltpu.VMEM((TM, TM), BF16),
                        pltpu.VMEM((8, CONV_WIDTH), F32),
                        pltpu.VMEM((TM, D_MODEL), BF16),
                        pltpu.VMEM((2, TM, D_MODEL), BF16),
                        pltpu.VMEM(w_in.shape, BF16),
                        pltpu.VMEM(w_out.shape, BF16),
                        pltpu.VMEM(w_rt.shape, BF16),
                        pltpu.VMEM((nt + SUBLANES, LANES), F32)],
        compiler_params=pltpu.CompilerParams(dimension_semantics=("arbitrary",),
                                             vmem_limit_bytes=VMEM_LIMIT),
        name="mixer_router",
    )(x2d, *ins)


def _expert_kernel(meta_ref, src_ref, dst_ref, stage_ref,
                   wg_hbm, wu_hbm, wd_hbm, out_ref,
                   xbuf, obuf, wg_f, wu_f, wd_f, wg_s, wu_s, wd_s, sem, osem, wsem):
    i = pl.program_id(0)
    n_used = meta_ref[3]
    slot = i % N_SLOTS
    bpt = BLOCKS_PER_ROW_TILE
    dump_row = dst_ref.shape[0] // bpt - SUBLANES
    expert_of = lambda tile: meta_ref[tile * META_W]

    def gather_copy(seq, j):
        tile = jnp.minimum(seq, n_used - 1)
        src = pl.multiple_of(src_ref[tile * bpt + j] * BLOCK, BLOCK)
        slot_ = seq % N_SLOTS
        return pltpu.make_async_copy(stage_ref.at[pl.ds(src, BLOCK)],
                                     xbuf.at[slot_, pl.ds(j * BLOCK, BLOCK)], sem.at[slot_])

    def scatter_copy(tile, j):
        row = jnp.where(tile < 0, dump_row, tile)
        dst = pl.multiple_of(dst_ref[row * bpt + j] * BLOCK, BLOCK)
        slot_ = (tile + N_SLOTS) % N_SLOTS
        return pltpu.make_async_copy(obuf.at[slot_, pl.ds(j * BLOCK, BLOCK)],
                                     out_ref.at[pl.ds(dst, BLOCK)], osem.at[slot_])

    def each_block(fn):
        for j in range(BLOCKS_PER_ROW_TILE):
            fn(j)

    def weight_copies(e):
        return (pltpu.make_async_copy(wg_hbm.at[e], wg_f, wsem.at[0]),
                pltpu.make_async_copy(wu_hbm.at[e], wu_f, wsem.at[1]),
                pltpu.make_async_copy(wd_hbm.at[e], wd_f, wsem.at[2]))

    @pl.when(i == 0)
    def _first():
        each_block(lambda j: gather_copy(0, j).start())
        each_block(lambda j: gather_copy(1, j).start())
        for cp in weight_copies(expert_of(0)):
            cp.start(priority=WEIGHT_DMA_PRIORITY)
        obuf[...] = jnp.zeros_like(obuf)

    @pl.when((i >= 2) & (i - 3 < n_used))
    def _free_out_slot():
        each_block(lambda j: scatter_copy(i - 3, j).wait())

    @pl.when(i == n_used)
    def _after_last_tile():
        each_block(lambda j: gather_copy(n_used, j).wait())
        each_block(lambda j: gather_copy(n_used + 1, j).wait())
        each_block(lambda j: scatter_copy(n_used - 1, j).start())

    @pl.when(i < n_used)
    def _tile():
        e = expert_of(i)
        prev = expert_of(jnp.maximum(i - 1, 0))

        @pl.when((i == 0) | (e != prev))
        def _new_expert():
            for cp in weight_copies(e):
                cp.wait()
            wg_s[...] = wg_f[...].astype(BF16)
            wu_s[...] = wu_f[...].astype(BF16)
            wd_s[...] = wd_f[...].astype(BF16)
            nxt = meta_ref[i * META_W + 1]

            @pl.when(nxt >= 0)
            def _next_weights():
                for cp in weight_copies(nxt):
                    cp.start(priority=WEIGHT_DMA_PRIORITY)

        each_block(lambda j: gather_copy(i, j).wait())

        def mlp(rows):
            hb = xbuf[slot, 0:rows, :]
            a = jnp.dot(hb, wg_s[...], preferred_element_type=F32)
            b = jnp.dot(hb, wu_s[...], preferred_element_type=F32)
            each_block(lambda j: gather_copy(i + 2, j).start())
            each_block(lambda j: scatter_copy(i - 1, j).start())
            hid = (a * jax.nn.sigmoid(a) * b).astype(BF16)
            obuf[slot, 0:rows, :] = jnp.dot(hid, wd_s[...], preferred_element_type=F32).astype(BF16)

        rows_needed = meta_ref[i * META_W + 2]
        for rows in range(ROW_STEP, TR + 1, ROW_STEP):
            pl.when(rows_needed == rows)(functools.partial(mlp, rows))


def _experts(meta, src_block, dst_block, stage, w_gate, w_up, w_down, n_tiles):
    any_spec = pl.BlockSpec(memory_space=pl.ANY)
    return pl.pallas_call(
        _expert_kernel,
        grid_spec=pltpu.PrefetchScalarGridSpec(
            num_scalar_prefetch=3,
            grid=(n_tiles,),
            in_specs=[any_spec, any_spec, any_spec, any_spec],
            out_specs=any_spec,
            scratch_shapes=[pltpu.VMEM((N_SLOTS, TR, D_MODEL), BF16),
                            pltpu.VMEM((N_SLOTS, TR, D_MODEL), BF16),
                            pltpu.VMEM((D_MODEL, D_EXPERT), F32),
                            pltpu.VMEM((D_MODEL, D_EXPERT), F32),
                            pltpu.VMEM((D_EXPERT, D_MODEL), F32),
                            pltpu.VMEM((D_MODEL, D_EXPERT), BF16),
                            pltpu.VMEM((D_MODEL, D_EXPERT), BF16),
                            pltpu.VMEM((D_EXPERT, D_MODEL), BF16),
                            pltpu.SemaphoreType.DMA((N_SLOTS,)),
                            pltpu.SemaphoreType.DMA((N_SLOTS,)),
                            pltpu.SemaphoreType.DMA((3,))]),
        out_shape=jax.ShapeDtypeStruct(stage.shape, stage.dtype),
        input_output_aliases={3: 0},
        compiler_params=pltpu.CompilerParams(dimension_semantics=("arbitrary",),
                                             vmem_limit_bytes=VMEM_LIMIT),
        name="experts",
    )(meta, src_block, dst_block, stage, w_gate, w_up, w_down)


def _combine_kernel(used_ref, x1_ref, route_ref, p_ref, gple_ref, wpg_f32, wpp_f32, gfin_ref,
                    srt_ref, out_ref, y_s, wpg_ref, wpp_ref):
    i = pl.program_id(0)

    @pl.when(i == 0)
    def _init():
        wpg_ref[...] = wpg_f32[...].astype(BF16)
        wpp_ref[...] = wpp_f32[...].astype(BF16)

    route = route_ref[...]
    w1, w2, pos1, pos2 = (route[:, k:k + 1] for k in (2, 3, 4, 5))

    def unsort(lo, hi):
        col = (lax.broadcasted_iota(I32, (TM, hi - lo), 1) + lo).astype(F32)
        mat = jnp.where(col == pos1, w1, jnp.where(col == pos2, w2, 0.0)).astype(BF16)
        return jnp.dot(mat, srt_ref[lo:hi, :], preferred_element_type=F32)

    def finish(y):
        pe = jnp.dot(p_ref[...].astype(BF16), wpp_ref[...], preferred_element_type=F32)
        hrows = TM // 2
        for h in range(2):
            rs = slice(h * hrows, (h + 1) * hrows)
            x2 = x1_ref[rs, :] + y[rs, :]
            hg = _rms(x2, gple_ref[...]).astype(BF16)
            gate = jax.nn.sigmoid(jnp.dot(hg, wpg_ref[...], preferred_element_type=F32))
            x3 = x2 + gate * pe[rs, :]
            out_ref[rs, :] = _rms(x3, gfin_ref[...])

    y_s[...] = unsort(0, SORT_MAIN)
    for lo in range(SORT_MAIN, KOUT, TAIL):
        @pl.when(used_ref[i] > lo)
        def _tail(lo=lo):
            y_s[...] += unsort(lo, lo + TAIL)
    finish(y_s)


def _combine(used, x1, route, p2d, g_ple, w_pg, w_pp, g_fin, sorted_out):
    t = p2d.shape[0]
    full = lambda a: pl.BlockSpec(a.shape, lambda i, *_: (0,) * a.ndim)
    return pl.pallas_call(
        _combine_kernel,
        grid_spec=pltpu.PrefetchScalarGridSpec(
            num_scalar_prefetch=1,
            grid=(t // TM,),
            in_specs=[pl.BlockSpec((TM, D_MODEL), lambda i, *_: (i, 0)),
                      pl.BlockSpec((TM, LANES), lambda i, *_: (i, 0)),
                      pl.BlockSpec((TM, PLE_DIM), lambda i, *_: (i, 0)),
                      full(g_ple), full(w_pg), full(w_pp), full(g_fin),
                      pl.BlockSpec((KOUT, D_MODEL), lambda i, *_: (i, 0))],
            out_specs=pl.BlockSpec((TM, D_MODEL), lambda i, *_: (i, 0)),
            scratch_shapes=[pltpu.VMEM((TM, D_MODEL), F32),
                            pltpu.VMEM(w_pg.shape, BF16),
                            pltpu.VMEM(w_pp.shape, BF16)]),
        out_shape=jax.ShapeDtypeStruct((t, D_MODEL), F32),
        compiler_params=pltpu.CompilerParams(dimension_semantics=("arbitrary",),
                                             vmem_limit_bytes=VMEM_LIMIT),
        name="combine_ple",
    )(used, x1, route, p2d, g_ple, w_pg, w_pp, g_fin, sorted_out)


def kernel(x, p, g_mix, w_in, w_conv, g_sgu, w_spatial, b_spatial, w_out, g_ffn, w_group, b_group,
           w_router, b_router, w_gate, w_up, w_down, g_ple, w_ple_gate, w_ple_proj, g_final):
    bsz, seq, d = x.shape
    t = bsz * seq
    assert w_in.shape[0] == 1, "single-layer block"
    assert d == D_MODEL and seq % TM == 0 and TM % CHUNK == 0
    nt = t // TM
    max_rows = TOP_K * t + nt * N_EXPERTS * (BLOCK - 1)
    n_tiles = (max_rows + N_EXPERTS * (TR - 1)) // TR + N_SLOTS
    l = 0

    w_rt = jnp.concatenate(
        [w_group[l], jnp.transpose(w_router[l], (1, 0, 2)).reshape(d, N_EXPERTS)], axis=1)
    w_rt = jnp.pad(w_rt, ((0, 0), (0, LANES - w_rt.shape[1]))).T
    b_rt = jnp.pad(jnp.concatenate([b_group[l], b_router[l].reshape(-1)]),
                   (0, LANES - N_GROUPS - N_EXPERTS)).reshape(LANES, 1)
    b_sp = jnp.repeat(b_spatial[l].T, SGU_WIDTH // SGU_HEADS, axis=1)

    x1, route, stage, meta, src, dst, used = _mixer_router(
        x.reshape(t, d), g_mix[l].reshape(1, d), w_in[l], w_conv[l],
        g_sgu[l].reshape(1, -1), w_spatial[l], b_sp, w_out[l],
        g_ffn[l].reshape(1, d), w_rt, b_rt, seq, n_tiles)

    bpt = BLOCKS_PER_ROW_TILE
    meta, src, dst, used = (meta[:, :META_W].reshape(-1), src[:, :bpt].reshape(-1),
                            dst[:, :bpt].reshape(-1), used[:, 0])
    sorted_out = _experts(meta, src, dst, stage,
                          w_gate[l].reshape(N_EXPERTS, d, D_EXPERT),
                          w_up[l].reshape(N_EXPERTS, d, D_EXPERT),
                          w_down[l].reshape(N_EXPERTS, D_EXPERT, d), n_tiles)
    out = _combine(used, x1, route, p[l].reshape(t, PLE_DIM), g_ple[l].reshape(1, d),
                   w_ple_gate[l], w_ple_proj[l], g_final.reshape(1, d), sorted_out)
    return out.reshape(bsz, seq, d)
```

```python
import functools

import jax
import jax.numpy as jnp
from jax import lax
from jax.experimental import pallas as pl
from jax.experimental.pallas import tpu as pltpu

F32 = jnp.float32
BF16 = jnp.bfloat16
I32 = jnp.int32

EPS = 1e-6
D_MODEL = 1024
CONV_WIDTH = 512
SGU_WIDTH = 512
SGU_HEADS = 8
HEAD_PAIRS = SGU_HEADS // 2
CHUNK = 128
N_GROUPS = 4
EXPERTS_PER_GROUP = 8
N_EXPERTS = N_GROUPS * EXPERTS_PER_GROUP
TOP_K = 2
D_EXPERT = 512
PLE_DIM = 256
LANES = 128
SUBLANES = 8
BLOCK = 2 * SUBLANES
ROUTE_LANE0 = N_GROUPS
ROUTE_ROWS = 40

TM = 512
TR = 512
BLOCKS_PER_ROW_TILE = TR // BLOCK
KOUT = TOP_K * TM + N_EXPERTS * BLOCK
TAIL = 128
SORT_MAIN = KOUT - 2 * TAIL
META_W = 4
ROW_STEP = 128
N_SLOTS = 3
DUMP_BLOCKS = N_SLOTS * BLOCKS_PER_ROW_TILE
assert DUMP_BLOCKS * BLOCK <= KOUT
WEIGHT_DMA_PRIORITY = 1
VMEM_LIMIT = 56 * 1024 * 1024


def _rms(x, g):
    return x * lax.rsqrt(jnp.mean(x * x, axis=-1, keepdims=True) + EPS) * g


def _build_tables(n_s, nt, meta_ref, src_ref, dst_ref, used_ref):
    tp = meta_ref.shape[0]
    bpt = BLOCKS_PER_ROW_TILE
    kb = KOUT // BLOCK
    n = n_s[0:nt, :]
    lane_n = lax.broadcasted_iota(I32, (nt, LANES), 1)
    is_e = (lane_n >= ROUTE_LANE0) & (lane_n < ROUTE_LANE0 + N_EXPERTS)
    nblk = jnp.where(is_e, jnp.floor((n + (BLOCK - 1)) * (1.0 / BLOCK)), 0.0)
    er = lax.broadcasted_iota(I32, (LANES, LANES), 0)
    ec = lax.broadcasted_iota(I32, (LANES, LANES), 1)
    loc = jnp.dot(nblk.astype(BF16), (er < ec).astype(BF16), preferred_element_type=F32)
    ti = lax.broadcasted_iota(I32, (nt, nt), 0)
    tj = lax.broadcasted_iota(I32, (nt, nt), 1)
    carry = jnp.dot((tj < ti).astype(BF16), nblk.astype(BF16), preferred_element_type=F32)
    total = jnp.sum(nblk, axis=0, keepdims=True)
    tiles_e = jnp.floor((total + (bpt - 1)) * (1.0 / bpt))
    tile_end = jnp.dot(jnp.broadcast_to(tiles_e, (SUBLANES, LANES)).astype(BF16),
                       (er <= ec).astype(BF16), preferred_element_type=F32)[0:1, :]
    tile0 = tile_end - tiles_e
    n_used = jnp.max(tile_end, axis=1, keepdims=True)

    tau = lax.broadcasted_iota(I32, (tp, LANES), 0).astype(F32)
    lane_i = lax.broadcasted_iota(I32, (tp, LANES), 1)
    lane_t = lane_i.astype(F32)
    is_e_t = (lane_i >= ROUTE_LANE0) & (lane_i < ROUTE_LANE0 + N_EXPERTS)
    te = jnp.sum(jnp.where(is_e_t & (tile_end <= jnp.minimum(tau, n_used - 1.0)), 1.0, 0.0),
                 axis=1, keepdims=True)
    oh = lane_t == te + ROUTE_LANE0
    pick = lambda row: jnp.sum(jnp.where(oh, row, 0.0), axis=1, keepdims=True)
    seg0_t = pick(tile0) * bpt
    e_lane = lane_t - ROUTE_LANE0
    cand = jnp.where(is_e_t & (e_lane > te) & (tiles_e > 0.0), e_lane, 99.0)
    nxt = jnp.min(cand, axis=1, keepdims=True)
    nxt = jnp.where(nxt == 99.0, -1.0, nxt)
    rows_t = jnp.clip(pick(total) * BLOCK - (tau[:, 0:1] - pick(tile0)) * TR, ROW_STEP, TR)
    rows_t = jnp.ceil(rows_t * (1.0 / ROW_STEP)) * ROW_STEP
    meta = jnp.where(lane_i == 0, te, 0.0)
    meta = jnp.where(lane_i == 1, nxt, meta)
    meta = jnp.where(lane_i == 2, rows_t, meta)
    meta = jnp.where(lane_i == 3, n_used, meta)
    meta_ref[...] = meta.astype(I32)

    bidx = tau * bpt + lane_t
    off = bidx - seg0_t
    ohb = jnp.where(oh, 1.0, 0.0).astype(BF16)

    def per_tile(tab):
        hi = jnp.floor(tab * (1.0 / 32))
        lo = tab - 32.0 * hi
        nt_dot = lambda a: lax.dot_general(ohb, a.astype(BF16), (((1,), (1,)), ((), ())),
                                           preferred_element_type=F32)
        return 32.0 * nt_dot(hi) + nt_dot(lo)

    run_end = per_tile(carry + nblk)
    run_beg = per_tile(carry)
    run_loc = per_tile(loc)
    tile_of = jnp.zeros((tp, LANES), F32)
    for it in range(nt):
        tile_of = tile_of + jnp.where(run_end[:, it:it + 1] <= off, 1.0, 0.0)
    srcv = off
    for it in range(nt):
        srcv = srcv + jnp.where(tile_of == it,
                                run_loc[:, it:it + 1] - run_beg[:, it:it + 1] + it * kb, 0.0)
    valid = (tile_of < nt) & (bidx < n_used * bpt)
    slot3 = tau - N_SLOTS * jnp.floor((tau + 0.5) * (1.0 / N_SLOTS))
    dump = nt * kb + slot3 * bpt + lane_t
    src_ref[...] = jnp.where(valid, srcv, kb - 1.0).astype(I32)
    dst_ref[0:tp, :] = jnp.where(valid, srcv, dump).astype(I32)
    dst_ref[tp:tp + SUBLANES, :] = (nt * kb + DUMP_BLOCKS - bpt
                                    + lax.broadcasted_iota(I32, (SUBLANES, LANES), 1))
    used_ref[...] = jnp.broadcast_to(jnp.sum(nblk, axis=1, keepdims=True) * BLOCK,
                                     (nt, LANES)).astype(I32)


def _mixer_router_kernel(tiles_per_seq, nt,
                         x_ref, gmix_ref, win_f32, wconv_ref, gsgu_ref, wsp_ref, bsp_ref,
                         wout_f32, gffn_ref, wrt_f32, brt_ref,
                         x1_ref, route_ref, stage_ref, meta_ref, src_ref, dst_ref, used_ref,
                         wcat_s, tri_s, halo_s, mix_s, h2_s, win_ref, wout_ref, wrt_ref, n_s):
    i = pl.program_id(0)
    nch = TM // CHUNK

    @pl.when(i == 0)
    def _init():
        r = lax.broadcasted_iota(I32, (CHUNK, CHUNK), 0)
        c = lax.broadcasted_iota(I32, (CHUNK, CHUNK), 1)
        causal = c <= r
        for j in range(HEAD_PAIRS):
            wa = jnp.where(causal, wsp_ref[2 * j], 0.0)
            wb = jnp.where(causal, wsp_ref[2 * j + 1], 0.0)
            wcat_s[j] = jnp.concatenate([wa, wb], axis=1).astype(BF16)
        rr = lax.broadcasted_iota(I32, (TM, TM), 0)
        cc = lax.broadcasted_iota(I32, (TM, TM), 1)
        tri_s[...] = (rr < cc).astype(BF16)
        h2_s[...] = jnp.zeros_like(h2_s)
        win_ref[...] = win_f32[...].astype(BF16)
        wout_ref[...] = wout_f32[...].astype(BF16)
        wrt_ref[...] = wrt_f32[...].astype(BF16)

    @pl.when(i % tiles_per_seq == 0)
    def _seq_start():
        halo_s[...] = jnp.zeros_like(halo_s)

    def step(mix):
        h2b = h2_s[(i + 1) % 2]
        if mix:
            x = x_ref[...]
            hb = _rms(x, gmix_ref[...]).astype(BF16)

            def proj(k):
                return jnp.dot(hb, win_ref[:, k * 512:(k + 1) * 512], preferred_element_type=F32)

        logits = lax.dot_general(wrt_ref[...], h2b, (((1,), (1,)), ((), ())),
                                 preferred_element_type=F32)[0:ROUTE_ROWS, :] + brt_ref[0:ROUTE_ROWS, :]
        if mix:
            pu = proj(3)
            pv = proj(4)

        ridx = lax.broadcasted_iota(I32, (ROUTE_ROWS, TM), 0).astype(F32)
        neg = jnp.float32(-jnp.inf)
        big = jnp.float32(1e9)
        is_g = ridx < N_GROUPS
        gl = jnp.where(is_g, logits, neg)
        gmax = jnp.max(gl, axis=0, keepdims=True)
        gsum = jnp.sum(jnp.where(is_g, jnp.exp(gl - gmax), 0.0), axis=0, keepdims=True)
        g_w = 1.0 / gsum
        g_idx = jnp.min(jnp.where(gl == gmax, ridx, big), axis=0, keepdims=True)
        lo_row = ROUTE_LANE0 + EXPERTS_PER_GROUP * g_idx
        in_grp = (ridx >= lo_row) & (ridx < lo_row + EXPERTS_PER_GROUP)
        el = jnp.where(in_grp, logits, neg)
        v1 = jnp.max(el, axis=0, keepdims=True)
        i1 = jnp.min(jnp.where(el == v1, ridx, big), axis=0, keepdims=True)
        el2 = jnp.where(ridx == i1, neg, el)
        v2 = jnp.max(el2, axis=0, keepdims=True)
        i2 = jnp.min(jnp.where(el2 == v2, ridx, big), axis=0, keepdims=True)
        e21 = jnp.exp(v2 - v1)
        w1 = g_w / (1.0 + e21)
        w2 = g_w * e21 / (1.0 + e21)

        if mix:
            u = jax.nn.gelu(pu)
            zc = proj(1) * proj(2)
            v = jax.nn.gelu(pv)
            vc = v - jnp.mean(v, axis=-1, keepdims=True)
            vn = vc * lax.rsqrt(jnp.mean(vc * vc, axis=-1, keepdims=True) + EPS) * gsgu_ref[...]
            vnb = vn.astype(BF16)

            row = lax.broadcasted_iota(I32, (TM, CONV_WIDTH), 0)
            h6 = halo_s[6:7, :]
            h7 = halo_s[7:8, :]
            z1 = jnp.where(row == 0, h7, pltpu.roll(zc, 1, 0))
            z2 = jnp.where(row == 0, h6, jnp.where(row == 1, h7, pltpu.roll(zc, 2, 0)))
            conv = z2 * wconv_ref[0:1, :] + z1 * wconv_ref[1:2, :] + zc * wconv_ref[2:3, :]
            halo_s[...] = zc[TM - 8:TM, :]
            mix_s[:, 0:CONV_WIDTH] = (proj(0) * conv).astype(BF16)

        sel1 = ridx == i1
        sel2 = ridx == i2
        onehot = jnp.where(sel1 | sel2, 1.0, 0.0)
        counts = jnp.sum(onehot, axis=1, keepdims=True)
        pad_rows = jnp.zeros((LANES - ROUTE_ROWS, LANES), F32)
        counts_sq = jnp.concatenate([jnp.broadcast_to(counts, (ROUTE_ROWS, LANES)), pad_rows], axis=0)
        blocks_sq = jnp.ceil(counts_sq * (1.0 / BLOCK))
        er = lax.broadcasted_iota(I32, (LANES, LANES), 0)
        ec = lax.broadcasted_iota(I32, (LANES, LANES), 1)
        before = (ec < er).astype(BF16)
        run_start = BLOCK * jnp.dot(before, blocks_sq.astype(BF16),
                                    preferred_element_type=F32)[0:ROUTE_ROWS, 0:1]
        rank = jnp.dot(onehot.astype(BF16), tri_s[...], preferred_element_type=F32) + run_start
        pos1_row = jnp.sum(jnp.where(sel1, rank, 0.0), axis=0, keepdims=True)
        pos2_row = jnp.sum(jnp.where(sel2, rank, 0.0), axis=0, keepdims=True)
        n_s[pl.ds(jnp.where(i == 0, nt, i - 1), 1), :] = counts_sq.T[0:1, :]

        r8 = lax.broadcasted_iota(I32, (SUBLANES, TM), 0)
        rows8 = jnp.where(r8 == 0, i1 - ROUTE_LANE0, 0.0)
        rows8 = jnp.where(r8 == 1, i2 - ROUTE_LANE0, rows8)
        rows8 = jnp.where(r8 == 2, w1, rows8)
        rows8 = jnp.where(r8 == 3, w2, rows8)
        rows8 = jnp.where(r8 == 4, pos1_row, rows8)
        rows8 = jnp.where(r8 == 5, pos2_row, rows8)
        route_ref[...] = jnp.concatenate(
            [rows8, jnp.zeros((LANES - SUBLANES, TM), F32)], axis=0).T

        def sort_rows(lo, hi):
            out_row = (lax.broadcasted_iota(I32, (hi - lo, TM), 0) + lo).astype(F32)
            sort_mat = jnp.where((out_row == pos1_row) | (out_row == pos2_row), 1.0, 0.0).astype(BF16)
            stage_ref[lo:hi, :] = jnp.dot(sort_mat, h2b, preferred_element_type=F32).astype(BF16)

        sort_rows(0, SORT_MAIN // 2)
        if mix:
            left = lax.broadcasted_iota(I32, (CHUNK, LANES), 1) < (LANES // 2)
            zero = jnp.zeros((CHUNK, LANES), BF16)
            for j in range(HEAD_PAIRS):
                cols = []
                for c in range(nch):
                    s = vnb[c * CHUNK:(c + 1) * CHUNK, j * LANES:(j + 1) * LANES]
                    cols.append(jnp.concatenate([jnp.where(left, s, zero), jnp.where(left, zero, s)],
                                                axis=0))
                rhs = jnp.concatenate(cols, axis=1)
                res = jnp.dot(wcat_s[j], rhs, preferred_element_type=F32)
                bias = bsp_ref[:, j * LANES:(j + 1) * LANES]
                for c in range(nch):
                    mixed = res[:, c * LANES:(c + 1) * LANES] + bias
                    uu = u[c * CHUNK:(c + 1) * CHUNK, j * LANES:(j + 1) * LANES]
                    mix_s[c * CHUNK:(c + 1) * CHUNK,
                          CONV_WIDTH + j * LANES:CONV_WIDTH + (j + 1) * LANES] = (uu * mixed).astype(BF16)

        sort_rows(SORT_MAIN // 2, SORT_MAIN)
        if mix:
            x1 = x + jnp.dot(mix_s[...], wout_ref[...], preferred_element_type=F32)
            x1_ref[...] = x1
            h2_s[i % 2] = _rms(x1, gffn_ref[...]).astype(BF16)

        used_rows = BLOCK * jnp.sum(blocks_sq[:, 0:1])
        for lo in range(SORT_MAIN, KOUT, TAIL):
            @pl.when(used_rows > lo)
            def _tail(lo=lo):
                sort_rows(lo, lo + TAIL)

            @pl.when(used_rows <= lo)
            def _empty_tail(lo=lo):
                stage_ref[lo:lo + TAIL, :] = jnp.zeros((TAIL, D_MODEL), BF16)

    pl.when(i < nt)(functools.partial(step, True))

    @pl.when(i == nt)
    def _last():
        step(False)
        _build_tables(n_s, nt, meta_ref, src_ref, dst_ref, used_ref)


def _mixer_router(x2d, g_mix, w_in, w_conv, g_sgu, w_sp, b_sp, w_out, g_ffn, w_rt, b_rt, seq,
                  n_tiles):
    t = x2d.shape[0]
    nt = t // TM
    tp = -(-n_tiles // SUBLANES) * SUBLANES
    whole = lambda r: pl.BlockSpec((r, LANES), lambda i: (0, 0))
    full = lambda a: pl.BlockSpec(a.shape, lambda i: (0,) * a.ndim)
    ins = (g_mix, w_in, w_conv, g_sgu, w_sp, b_sp, w_out, g_ffn, w_rt, b_rt)
    prev = lambda i: (jnp.where(i == 0, nt, i - 1), 0)
    return pl.pallas_call(
        functools.partial(_mixer_router_kernel, seq // TM, nt),
        grid=(nt + 1,),
        in_specs=[pl.BlockSpec((TM, D_MODEL), lambda i: (jnp.minimum(i, nt - 1), 0))]
        + [full(a) for a in ins],
        out_specs=[pl.BlockSpec((TM, D_MODEL), lambda i: (jnp.minimum(i, nt - 1), 0)),
                   pl.BlockSpec((TM, LANES), prev),
                   pl.BlockSpec((KOUT, D_MODEL), prev),
                   whole(tp), whole(tp), whole(tp + SUBLANES), whole(nt)],
        out_shape=[jax.ShapeDtypeStruct((t, D_MODEL), F32),
                   jax.ShapeDtypeStruct(((nt + 1) * TM, LANES), F32),
                   jax.ShapeDtypeStruct(((nt + 1) * KOUT, D_MODEL), BF16),
                   jax.ShapeDtypeStruct((tp, LANES), I32),
                   jax.ShapeDtypeStruct((tp, LANES), I32),
                   jax.ShapeDtypeStruct((tp + SUBLANES, LANES), I32),
                   jax.ShapeDtypeStruct((nt, LANES), I32)],
        scratch_shapes=[pltpu.VMEM((HEAD_PAIRS, CHUNK, 2 * CHUNK), BF16),
                        pltpu.VMEM((TM, TM), BF16),
                        pltpu.VMEM((8, CONV_WIDTH), F32),
                        pltpu.VMEM((TM, D_MODEL), BF16),
                        pltpu.VMEM((2, TM, D_MODEL), BF16),
                        pltpu.VMEM(w_in.shape, BF16),
                        pltpu.VMEM(w_out.shape, BF16),
                        pltpu.VMEM(w_rt.shape, BF16),
                        pltpu.VMEM((nt + SUBLANES, LANES), F32)],
        compiler_params=pltpu.CompilerParams(dimension_semantics=("arbitrary",),
                                             vmem_limit_bytes=VMEM_LIMIT),
        name="mixer_router",
    )(x2d, *ins)


def _expert_kernel(meta_ref, src_ref, dst_ref, stage_ref,
                   wg_hbm, wu_hbm, wd_hbm, out_ref,
                   xbuf, obuf, wg_f, wu_f, wd_f, wg_s, wu_s, wd_s, sem, osem, wsem):
    i = pl.program_id(0)
    n_used = meta_ref[3]
    slot = i % N_SLOTS
    bpt = BLOCKS_PER_ROW_TILE
    dump_row = dst_ref.shape[0] // bpt - SUBLANES
    expert_of = lambda tile: meta_ref[tile * META_W]

    def gather_copy(seq, j):
        tile = jnp.minimum(seq, n_used - 1)
        src = pl.multiple_of(src_ref[tile * bpt + j] * BLOCK, BLOCK)
        slot_ = seq % N_SLOTS
        return pltpu.make_async_copy(stage_ref.at[pl.ds(src, BLOCK)],
                                     xbuf.at[slot_, pl.ds(j * BLOCK, BLOCK)], sem.at[slot_])

    def scatter_copy(tile, j):
        row = jnp.where(tile < 0, dump_row, tile)
        dst = pl.multiple_of(dst_ref[row * bpt + j] * BLOCK, BLOCK)
        slot_ = (tile + N_SLOTS) % N_SLOTS
        return pltpu.make_async_copy(obuf.at[slot_, pl.ds(j * BLOCK, BLOCK)],
                                     out_ref.at[pl.ds(dst, BLOCK)], osem.at[slot_])

    def each_block(fn):
        for j in range(BLOCKS_PER_ROW_TILE):
            fn(j)

    def weight_copies(e):
        return (pltpu.make_async_copy(wg_hbm.at[e], wg_f, wsem.at[0]),
                pltpu.make_async_copy(wu_hbm.at[e], wu_f, wsem.at[1]),
                pltpu.make_async_copy(wd_hbm.at[e], wd_f, wsem.at[2]))

    @pl.when(i == 0)
    def _first():
        each_block(lambda j: gather_copy(0, j).start())
        each_block(lambda j: gather_copy(1, j).start())
        for cp in weight_copies(expert_of(0)):
            cp.start(priority=WEIGHT_DMA_PRIORITY)
        obuf[...] = jnp.zeros_like(obuf)

    @pl.when((i >= 2) & (i - 3 < n_used))
    def _free_out_slot():
        each_block(lambda j: scatter_copy(i - 3, j).wait())

    @pl.when(i == n_used)
    def _after_last_tile():
        each_block(lambda j: gather_copy(n_used, j).wait())
        each_block(lambda j: gather_copy(n_used + 1, j).wait())
        each_block(lambda j: scatter_copy(n_used - 1, j).start())

    @pl.when(i < n_used)
    def _tile():
        e = expert_of(i)
        prev = expert_of(jnp.maximum(i - 1, 0))

        @pl.when((i == 0) | (e != prev))
        def _new_expert():
            for cp in weight_copies(e):
                cp.wait()
            wg_s[...] = wg_f[...].astype(BF16)
            wu_s[...] = wu_f[...].astype(BF16)
            wd_s[...] = wd_f[...].astype(BF16)
            nxt = meta_ref[i * META_W + 1]

            @pl.when(nxt >= 0)
            def _next_weights():
                for cp in weight_copies(nxt):
                    cp.start(priority=WEIGHT_DMA_PRIORITY)

        each_block(lambda j: gather_copy(i, j).wait())

        def mlp(rows):
            hb = xbuf[slot, 0:rows, :]
            a = jnp.dot(hb, wg_s[...], preferred_element_type=F32)
            b = jnp.dot(hb, wu_s[...], preferred_element_type=F32)
            each_block(lambda j: gather_copy(i + 2, j).start())
            each_block(lambda j: scatter_copy(i - 1, j).start())
            hid = (a * jax.nn.sigmoid(a) * b).astype(BF16)
            obuf[slot, 0:rows, :] = jnp.dot(hid, wd_s[...], preferred_element_type=F32).astype(BF16)

        rows_needed = meta_ref[i * META_W + 2]
        for rows in range(ROW_STEP, TR + 1, ROW_STEP):
            pl.when(rows_needed == rows)(functools.partial(mlp, rows))


def _experts(meta, src_block, dst_block, stage, w_gate, w_up, w_down, n_tiles):
    any_spec = pl.BlockSpec(memory_space=pl.ANY)
    return pl.pallas_call(
        _expert_kernel,
        grid_spec=pltpu.PrefetchScalarGridSpec(
            num_scalar_prefetch=3,
            grid=(n_tiles,),
            in_specs=[any_spec, any_spec, any_spec, any_spec],
            out_specs=any_spec,
            scratch_shapes=[pltpu.VMEM((N_SLOTS, TR, D_MODEL), BF16),
                            pltpu.VMEM((N_SLOTS, TR, D_MODEL), BF16),
                            pltpu.VMEM((D_MODEL, D_EXPERT), F32),
                            pltpu.VMEM((D_MODEL, D_EXPERT), F32),
                            pltpu.VMEM((D_EXPERT, D_MODEL), F32),
                            pltpu.VMEM((D_MODEL, D_EXPERT), BF16),
                            pltpu.VMEM((D_MODEL, D_EXPERT), BF16),
                            pltpu.VMEM((D_EXPERT, D_MODEL), BF16),
                            pltpu.SemaphoreType.DMA((N_SLOTS,)),
                            pltpu.SemaphoreType.DMA((N_SLOTS,)),
                            pltpu.SemaphoreType.DMA((3,))]),
        out_shape=jax.ShapeDtypeStruct(stage.shape, stage.dtype),
        input_output_aliases={3: 0},
        compiler_params=pltpu.CompilerParams(dimension_semantics=("arbitrary",),
                                             vmem_limit_bytes=VMEM_LIMIT),
        name="experts",
    )(meta, src_block, dst_block, stage, w_gate, w_up, w_down)


def _combine_kernel(used_ref, x1_ref, route_ref, p_ref, gple_ref, wpg_f32, wpp_f32, gfin_ref,
                    srt_ref, out_ref, y_s, wpg_ref, wpp_ref):
    i = pl.program_id(0)

    @pl.when(i == 0)
    def _init():
        wpg_ref[...] = wpg_f32[...].astype(BF16)
        wpp_ref[...] = wpp_f32[...].astype(BF16)

    hrows = TM // 2

    def unsort(lo, hi, rs):
        route = route_ref[rs, :]
        w1, w2, pos1, pos2 = (route[:, k:k + 1] for k in (2, 3, 4, 5))
        col = (lax.broadcasted_iota(I32, (hrows, hi - lo), 1) + lo).astype(F32)
        mat = jnp.where(col == pos1, w1, jnp.where(col == pos2, w2, 0.0)).astype(BF16)
        return jnp.dot(mat, srt_ref[lo:hi, :], preferred_element_type=F32)

    def finish(y, rs, pe):
        x2 = x1_ref[rs, :] + y
        hg = _rms(x2, gple_ref[...]).astype(BF16)
        gate = jax.nn.sigmoid(jnp.dot(hg, wpg_ref[...], preferred_element_type=F32))
        x3 = x2 + gate * pe[rs, :]
        out_ref[rs, :] = _rms(x3, gfin_ref[...])

    @pl.when(used_ref[i] <= SORT_MAIN)
    def _common():
        pe = jnp.dot(p_ref[...].astype(BF16), wpp_ref[...], preferred_element_type=F32)
        y0 = unsort(0, SORT_MAIN, slice(0, hrows))
        y1 = unsort(0, SORT_MAIN, slice(hrows, TM))
        finish(y0, slice(0, hrows), pe)
        finish(y1, slice(hrows, TM), pe)

    @pl.when(used_ref[i] > SORT_MAIN)
    def _with_tails():
        pe = jnp.dot(p_ref[...].astype(BF16), wpp_ref[...], preferred_element_type=F32)
        for h in range(2):
            rs = slice(h * hrows, (h + 1) * hrows)
            y_s[rs, :] = unsort(0, SORT_MAIN, rs)
            for lo in range(SORT_MAIN, KOUT, TAIL):
                @pl.when(used_ref[i] > lo)
                def _tail(lo=lo, rs=rs):
                    y_s[rs, :] += unsort(lo, lo + TAIL, rs)
            finish(y_s[rs, :], rs, pe)


def _combine(used, x1, route, p2d, g_ple, w_pg, w_pp, g_fin, sorted_out):
    t = p2d.shape[0]
    full = lambda a: pl.BlockSpec(a.shape, lambda i, *_: (0,) * a.ndim)
    return pl.pallas_call(
        _combine_kernel,
        grid_spec=pltpu.PrefetchScalarGridSpec(
            num_scalar_prefetch=1,
            grid=(t // TM,),
            in_specs=[pl.BlockSpec((TM, D_MODEL), lambda i, *_: (i, 0)),
                      pl.BlockSpec((TM, LANES), lambda i, *_: (i, 0)),
                      pl.BlockSpec((TM, PLE_DIM), lambda i, *_: (i, 0)),
                      full(g_ple), full(w_pg), full(w_pp), full(g_fin),
                      pl.BlockSpec((KOUT, D_MODEL), lambda i, *_: (i, 0))],
            out_specs=pl.BlockSpec((TM, D_MODEL), lambda i, *_: (i, 0)),
            scratch_shapes=[pltpu.VMEM((TM, D_MODEL), F32),
                            pltpu.VMEM(w_pg.shape, BF16),
                            pltpu.VMEM(w_pp.shape, BF16)]),
        out_shape=jax.ShapeDtypeStruct((t, D_MODEL), F32),
        compiler_params=pltpu.CompilerParams(dimension_semantics=("arbitrary",),
                                             vmem_limit_bytes=VMEM_LIMIT),
        name="combine_ple",
    )(used, x1, route, p2d, g_ple, w_pg, w_pp, g_fin, sorted_out)


def kernel(x, p, g_mix, w_in, w_conv, g_sgu, w_spatial, b_spatial, w_out, g_ffn, w_group, b_group,
           w_router, b_router, w_gate, w_up, w_down, g_ple, w_ple_gate, w_ple_proj, g_final):
    bsz, seq, d = x.shape
    t = bsz * seq
    assert w_in.shape[0] == 1, "single-layer block"
    assert d == D_MODEL and seq % TM == 0 and TM % CHUNK == 0
    nt = t // TM
    max_rows = TOP_K * t + nt * N_EXPERTS * (BLOCK - 1)
    n_tiles = (max_rows + N_EXPERTS * (TR - 1)) // TR + N_SLOTS
    l = 0

    w_rt = jnp.concatenate(
        [w_group[l], jnp.transpose(w_router[l], (1, 0, 2)).reshape(d, N_EXPERTS)], axis=1)
    w_rt = jnp.pad(w_rt, ((0, 0), (0, LANES - w_rt.shape[1]))).T
    b_rt = jnp.pad(jnp.concatenate([b_group[l], b_router[l].reshape(-1)]),
                   (0, LANES - N_GROUPS - N_EXPERTS)).reshape(LANES, 1)
    b_sp = jnp.repeat(b_spatial[l].T, SGU_WIDTH // SGU_HEADS, axis=1)

    x1, route, stage, meta, src, dst, used = _mixer_router(
        x.reshape(t, d), g_mix[l].reshape(1, d), w_in[l], w_conv[l],
        g_sgu[l].reshape(1, -1), w_spatial[l], b_sp, w_out[l],
        g_ffn[l].reshape(1, d), w_rt, b_rt, seq, n_tiles)

    bpt = BLOCKS_PER_ROW_TILE
    meta, src, dst, used = (meta[:, :META_W].reshape(-1), src[:, :bpt].reshape(-1),
                            dst[:, :bpt].reshape(-1), used[:, 0])
    sorted_out = _experts(meta, src, dst, stage,
                          w_gate[l].reshape(N_EXPERTS, d, D_EXPERT),
                          w_up[l].reshape(N_EXPERTS, d, D_EXPERT),
                          w_down[l].reshape(N_EXPERTS, D_EXPERT, d), n_tiles)
    out = _combine(used, x1, route, p[l].reshape(t, PLE_DIM), g_ple[l].reshape(1, d),
                   w_ple_gate[l], w_ple_proj[l], g_final.reshape(1, d), sorted_out)
    return out.reshape(bsz, seq, d)
```

```python
import functools

import jax
import jax.numpy as jnp
from jax import lax
from jax.experimental import pallas as pl
from jax.experimental.pallas import tpu as pltpu

F32 = jnp.float32
BF16 = jnp.bfloat16
I32 = jnp.int32

EPS = 1e-6
D_MODEL = 1024
CONV_WIDTH = 512
SGU_WIDTH = 512
SGU_HEADS = 8
HEAD_PAIRS = SGU_HEADS // 2
CHUNK = 128
N_GROUPS = 4
EXPERTS_PER_GROUP = 8
N_EXPERTS = N_GROUPS * EXPERTS_PER_GROUP
TOP_K = 2
D_EXPERT = 512
PLE_DIM = 256
LANES = 128
SUBLANES = 8
BLOCK = 2 * SUBLANES
ROUTE_LANE0 = N_GROUPS
ROUTE_ROWS = 40

TM = 512
TR = 512
BLOCKS_PER_ROW_TILE = TR // BLOCK
KOUT = TOP_K * TM + N_EXPERTS * BLOCK
TAIL = 128
SORT_MAIN = KOUT - 2 * TAIL
STAGE_CHUNK = KOUT - SORT_MAIN
META_W = 4
ROW_STEP = 128
N_SLOTS = 3
DUMP_BLOCKS = N_SLOTS * BLOCKS_PER_ROW_TILE
assert DUMP_BLOCKS * BLOCK <= KOUT
WEIGHT_DMA_PRIORITY = 1
VMEM_LIMIT = 56 * 1024 * 1024


def _rms(x, g):
    return x * lax.rsqrt(jnp.mean(x * x, axis=-1, keepdims=True) + EPS) * g


def _build_tables(n_s, nt, meta_ref, src_ref, dst_ref, used_ref):
    tp = meta_ref.shape[0]
    bpt = BLOCKS_PER_ROW_TILE
    kb = KOUT // BLOCK
    n = n_s[0:nt, :]
    lane_n = lax.broadcasted_iota(I32, (nt, LANES), 1)
    is_e = (lane_n >= ROUTE_LANE0) & (lane_n < ROUTE_LANE0 + N_EXPERTS)
    nblk = jnp.where(is_e, jnp.floor((n + (BLOCK - 1)) * (1.0 / BLOCK)), 0.0)
    er = lax.broadcasted_iota(I32, (LANES, LANES), 0)
    ec = lax.broadcasted_iota(I32, (LANES, LANES), 1)
    loc = jnp.dot(nblk.astype(BF16), (er < ec).astype(BF16), preferred_element_type=F32)
    ti = lax.broadcasted_iota(I32, (nt, nt), 0)
    tj = lax.broadcasted_iota(I32, (nt, nt), 1)
    carry = jnp.dot((tj < ti).astype(BF16), nblk.astype(BF16), preferred_element_type=F32)
    total = jnp.sum(nblk, axis=0, keepdims=True)
    tiles_e = jnp.floor((total + (bpt - 1)) * (1.0 / bpt))
    tile_end = jnp.dot(jnp.broadcast_to(tiles_e, (SUBLANES, LANES)).astype(BF16),
                       (er <= ec).astype(BF16), preferred_element_type=F32)[0:1, :]
    tile0 = tile_end - tiles_e
    n_used = jnp.max(tile_end, axis=1, keepdims=True)

    tau = lax.broadcasted_iota(I32, (tp, LANES), 0).astype(F32)
    lane_i = lax.broadcasted_iota(I32, (tp, LANES), 1)
    lane_t = lane_i.astype(F32)
    is_e_t = (lane_i >= ROUTE_LANE0) & (lane_i < ROUTE_LANE0 + N_EXPERTS)
    te = jnp.sum(jnp.where(is_e_t & (tile_end <= jnp.minimum(tau, n_used - 1.0)), 1.0, 0.0),
                 axis=1, keepdims=True)
    oh = lane_t == te + ROUTE_LANE0
    pick = lambda row: jnp.sum(jnp.where(oh, row, 0.0), axis=1, keepdims=True)
    seg0_t = pick(tile0) * bpt
    e_lane = lane_t - ROUTE_LANE0
    cand = jnp.where(is_e_t & (e_lane > te) & (tiles_e > 0.0), e_lane, 99.0)
    nxt = jnp.min(cand, axis=1, keepdims=True)
    nxt = jnp.where(nxt == 99.0, -1.0, nxt)
    rows_t = jnp.clip(pick(total) * BLOCK - (tau[:, 0:1] - pick(tile0)) * TR, ROW_STEP, TR)
    rows_t = jnp.ceil(rows_t * (1.0 / ROW_STEP)) * ROW_STEP
    meta = jnp.where(lane_i == 0, te, 0.0)
    meta = jnp.where(lane_i == 1, nxt, meta)
    meta = jnp.where(lane_i == 2, rows_t, meta)
    meta = jnp.where(lane_i == 3, n_used, meta)
    meta_ref[...] = meta.astype(I32)

    bidx = tau * bpt + lane_t
    off = bidx - seg0_t
    ohb = jnp.where(oh, 1.0, 0.0).astype(BF16)

    def per_tile(tab):
        hi = jnp.floor(tab * (1.0 / 32))
        lo = tab - 32.0 * hi
        nt_dot = lambda a: lax.dot_general(ohb, a.astype(BF16), (((1,), (1,)), ((), ())),
                                           preferred_element_type=F32)
        return 32.0 * nt_dot(hi) + nt_dot(lo)

    run_end = per_tile(carry + nblk)
    run_beg = per_tile(carry)
    run_loc = per_tile(loc)
    tile_of = jnp.zeros((tp, LANES), F32)
    for it in range(nt):
        tile_of = tile_of + jnp.where(run_end[:, it:it + 1] <= off, 1.0, 0.0)
    srcv = off
    for it in range(nt):
        srcv = srcv + jnp.where(tile_of == it,
                                run_loc[:, it:it + 1] - run_beg[:, it:it + 1] + it * kb, 0.0)
    valid = (tile_of < nt) & (bidx < n_used * bpt)
    slot3 = tau - N_SLOTS * jnp.floor((tau + 0.5) * (1.0 / N_SLOTS))
    dump = nt * kb + slot3 * bpt + lane_t
    src_ref[...] = jnp.where(valid, srcv, kb - 1.0).astype(I32)
    dst_ref[0:tp, :] = jnp.where(valid, srcv, dump).astype(I32)
    dst_ref[tp:tp + SUBLANES, :] = (nt * kb + DUMP_BLOCKS - bpt
                                    + lax.broadcasted_iota(I32, (SUBLANES, LANES), 1))
    used_ref[...] = jnp.broadcast_to(jnp.sum(nblk, axis=1, keepdims=True) * BLOCK,
                                     (nt, LANES)).astype(I32)


def _mixer_router_kernel(tiles_per_seq, nt,
                         x_ref, gmix_ref, win_f32, wconv_ref, gsgu_ref, wsp_ref, bsp_ref,
                         wout_f32, gffn_ref, wrt_f32, brt_ref,
                         x1_ref, route_ref, stage_ref, meta_ref, src_ref, dst_ref, used_ref,
                         wcat_s, tri_s, halo_s, mix_s, h2_s, win_ref, wout_ref, wrt_ref, n_s):
    i = pl.program_id(0)
    nch = TM // CHUNK

    @pl.when(i == 0)
    def _init():
        r = lax.broadcasted_iota(I32, (CHUNK, CHUNK), 0)
        c = lax.broadcasted_iota(I32, (CHUNK, CHUNK), 1)
        causal = c <= r
        for j in range(HEAD_PAIRS):
            wa = jnp.where(causal, wsp_ref[2 * j], 0.0)
            wb = jnp.where(causal, wsp_ref[2 * j + 1], 0.0)
            wcat_s[j] = jnp.concatenate([wa, wb], axis=1).astype(BF16)
        rr = lax.broadcasted_iota(I32, (TM, TM), 0)
        cc = lax.broadcasted_iota(I32, (TM, TM), 1)
        tri_s[...] = (rr < cc).astype(BF16)
        h2_s[...] = jnp.zeros_like(h2_s)
        win_ref[...] = win_f32[...].astype(BF16)
        wout_ref[...] = wout_f32[...].astype(BF16)
        wrt_ref[...] = wrt_f32[...].astype(BF16)

    @pl.when(i % tiles_per_seq == 0)
    def _seq_start():
        halo_s[...] = jnp.zeros_like(halo_s)

    def step(mix):
        h2b = h2_s[(i + 1) % 2]
        if mix:
            x = x_ref[...]
            hb = _rms(x, gmix_ref[...]).astype(BF16)

            def proj(k):
                return jnp.dot(hb, win_ref[:, k * 512:(k + 1) * 512], preferred_element_type=F32)

        logits = lax.dot_general(wrt_ref[...], h2b, (((1,), (1,)), ((), ())),
                                 preferred_element_type=F32)[0:ROUTE_ROWS, :] + brt_ref[0:ROUTE_ROWS, :]
        if mix:
            pu = proj(3)
            pv = proj(4)

        ridx = lax.broadcasted_iota(I32, (ROUTE_ROWS, TM), 0).astype(F32)
        neg = jnp.float32(-jnp.inf)
        big = jnp.float32(1e9)
        is_g = ridx < N_GROUPS
        gl = jnp.where(is_g, logits, neg)
        gmax = jnp.max(gl, axis=0, keepdims=True)
        gsum = jnp.sum(jnp.where(is_g, jnp.exp(gl - gmax), 0.0), axis=0, keepdims=True)
        g_w = 1.0 / gsum
        g_idx = jnp.min(jnp.where(gl == gmax, ridx, big), axis=0, keepdims=True)
        lo_row = ROUTE_LANE0 + EXPERTS_PER_GROUP * g_idx
        in_grp = (ridx >= lo_row) & (ridx < lo_row + EXPERTS_PER_GROUP)
        el = jnp.where(in_grp, logits, neg)
        v1 = jnp.max(el, axis=0, keepdims=True)
        i1 = jnp.min(jnp.where(el == v1, ridx, big), axis=0, keepdims=True)
        el2 = jnp.where(ridx == i1, neg, el)
        v2 = jnp.max(el2, axis=0, keepdims=True)
        i2 = jnp.min(jnp.where(el2 == v2, ridx, big), axis=0, keepdims=True)
        e21 = jnp.exp(v2 - v1)
        w1 = g_w / (1.0 + e21)
        w2 = g_w * e21 / (1.0 + e21)

        if mix:
            u = jax.nn.gelu(pu)
            zc = proj(1) * proj(2)
            v = jax.nn.gelu(pv)
            vc = v - jnp.mean(v, axis=-1, keepdims=True)
            vn = vc * lax.rsqrt(jnp.mean(vc * vc, axis=-1, keepdims=True) + EPS) * gsgu_ref[...]
            vnb = vn.astype(BF16)

            row = lax.broadcasted_iota(I32, (TM, CONV_WIDTH), 0)
            h6 = halo_s[6:7, :]
            h7 = halo_s[7:8, :]
            z1 = jnp.where(row == 0, h7, pltpu.roll(zc, 1, 0))
            z2 = jnp.where(row == 0, h6, jnp.where(row == 1, h7, pltpu.roll(zc, 2, 0)))
            conv = z2 * wconv_ref[0:1, :] + z1 * wconv_ref[1:2, :] + zc * wconv_ref[2:3, :]
            halo_s[...] = zc[TM - 8:TM, :]
            mix_s[:, 0:CONV_WIDTH] = (proj(0) * conv).astype(BF16)

        sel1 = ridx == i1
        sel2 = ridx == i2
        onehot = jnp.where(sel1 | sel2, 1.0, 0.0)
        counts = jnp.sum(onehot, axis=1, keepdims=True)
        pad_rows = jnp.zeros((LANES - ROUTE_ROWS, LANES), F32)
        counts_sq = jnp.concatenate([jnp.broadcast_to(counts, (ROUTE_ROWS, LANES)), pad_rows], axis=0)
        blocks_sq = jnp.ceil(counts_sq * (1.0 / BLOCK))
        er = lax.broadcasted_iota(I32, (LANES, LANES), 0)
        ec = lax.broadcasted_iota(I32, (LANES, LANES), 1)
        before = (ec < er).astype(BF16)
        run_start = BLOCK * jnp.dot(before, blocks_sq.astype(BF16),
                                    preferred_element_type=F32)[0:ROUTE_ROWS, 0:1]
        rank = jnp.dot(onehot.astype(BF16), tri_s[...], preferred_element_type=F32) + run_start
        pos1_row = jnp.sum(jnp.where(sel1, rank, 0.0), axis=0, keepdims=True)
        pos2_row = jnp.sum(jnp.where(sel2, rank, 0.0), axis=0, keepdims=True)
        n_s[pl.ds(jnp.where(i == 0, nt, i - 1), 1), :] = counts_sq.T[0:1, :]

        r8 = lax.broadcasted_iota(I32, (SUBLANES, TM), 0)
        rows8 = jnp.where(r8 == 0, i1 - ROUTE_LANE0, 0.0)
        rows8 = jnp.where(r8 == 1, i2 - ROUTE_LANE0, rows8)
        rows8 = jnp.where(r8 == 2, w1, rows8)
        rows8 = jnp.where(r8 == 3, w2, rows8)
        rows8 = jnp.where(r8 == 4, pos1_row, rows8)
        rows8 = jnp.where(r8 == 5, pos2_row, rows8)
        route_ref[...] = rows8

        def sort_rows(lo, hi):
            out_row = (lax.broadcasted_iota(I32, (hi - lo, TM), 0) + lo).astype(F32)
            sort_mat = jnp.where((out_row == pos1_row) | (out_row == pos2_row), 1.0, 0.0).astype(BF16)
            stage_ref[lo:hi, :] = jnp.dot(sort_mat, h2b, preferred_element_type=F32).astype(BF16)

        sort_rows(0, SORT_MAIN // 2)
        if mix:
            left = lax.broadcasted_iota(I32, (CHUNK, LANES), 1) < (LANES // 2)
            zero = jnp.zeros((CHUNK, LANES), BF16)
            for j in range(HEAD_PAIRS):
                cols = []
                for c in range(nch):
                    s = vnb[c * CHUNK:(c + 1) * CHUNK, j * LANES:(j + 1) * LANES]
                    cols.append(jnp.concatenate([jnp.where(left, s, zero), jnp.where(left, zero, s)],
                                                axis=0))
                rhs = jnp.concatenate(cols, axis=1)
                res = jnp.dot(wcat_s[j], rhs, preferred_element_type=F32)
                bias = bsp_ref[:, j * LANES:(j + 1) * LANES]
                for c in range(nch):
                    mixed = res[:, c * LANES:(c + 1) * LANES] + bias
                    uu = u[c * CHUNK:(c + 1) * CHUNK, j * LANES:(j + 1) * LANES]
                    mix_s[c * CHUNK:(c + 1) * CHUNK,
                          CONV_WIDTH + j * LANES:CONV_WIDTH + (j + 1) * LANES] = (uu * mixed).astype(BF16)

        sort_rows(SORT_MAIN // 2, SORT_MAIN)
        if mix:
            x1 = x + jnp.dot(mix_s[...], wout_ref[...], preferred_element_type=F32)
            x1_ref[...] = x1
            h2_s[i % 2] = _rms(x1, gffn_ref[...]).astype(BF16)

        used_rows = BLOCK * jnp.sum(blocks_sq[:, 0:1])
        for lo in range(SORT_MAIN, KOUT, TAIL):
            @pl.when(used_rows > lo)
            def _tail(lo=lo):
                sort_rows(lo, lo + TAIL)

            @pl.when(used_rows <= lo)
            def _empty_tail(lo=lo):
                stage_ref[lo:lo + TAIL, :] = jnp.zeros((TAIL, D_MODEL), BF16)

    pl.when(i < nt)(functools.partial(step, True))

    @pl.when(i == nt)
    def _last():
        step(False)
        _build_tables(n_s, nt, meta_ref, src_ref, dst_ref, used_ref)


def _mixer_router(x2d, g_mix, w_in, w_conv, g_sgu, w_sp, b_sp, w_out, g_ffn, w_rt, b_rt, seq,
                  n_tiles):
    t = x2d.shape[0]
    nt = t // TM
    tp = -(-n_tiles // SUBLANES) * SUBLANES
    whole = lambda r: pl.BlockSpec((r, LANES), lambda i: (0, 0))
    full = lambda a: pl.BlockSpec(a.shape, lambda i: (0,) * a.ndim)
    ins = (g_mix, w_in, w_conv, g_sgu, w_sp, b_sp, w_out, g_ffn, w_rt, b_rt)
    prev = lambda i: (jnp.where(i == 0, nt, i - 1), 0)
    return pl.pallas_call(
        functools.partial(_mixer_router_kernel, seq // TM, nt),
        grid=(nt + 1,),
        in_specs=[pl.BlockSpec((TM, D_MODEL), lambda i: (jnp.minimum(i, nt - 1), 0))]
        + [full(a) for a in ins],
        out_specs=[pl.BlockSpec((TM, D_MODEL), lambda i: (jnp.minimum(i, nt - 1), 0)),
                   pl.BlockSpec((SUBLANES, TM), prev),
                   pl.BlockSpec((KOUT, D_MODEL), prev),
                   whole(tp), whole(tp), whole(tp + SUBLANES), whole(nt)],
        out_shape=[jax.ShapeDtypeStruct((t, D_MODEL), F32),
                   jax.ShapeDtypeStruct(((nt + 1) * SUBLANES, TM), F32),
                   jax.ShapeDtypeStruct(((nt + 1) * KOUT, D_MODEL), BF16),
                   jax.ShapeDtypeStruct((tp, LANES), I32),
                   jax.ShapeDtypeStruct((tp, LANES), I32),
                   jax.ShapeDtypeStruct((tp + SUBLANES, LANES), I32),
                   jax.ShapeDtypeStruct((nt, LANES), I32)],
        scratch_shapes=[pltpu.VMEM((HEAD_PAIRS, CHUNK, 2 * CHUNK), BF16),
                        pltpu.VMEM((TM, TM), BF16),
                        pltpu.VMEM((8, CONV_WIDTH), F32),
                        pltpu.VMEM((TM, D_MODEL), BF16),
                        pltpu.VMEM((2, TM, D_MODEL), BF16),
                        pltpu.VMEM(w_in.shape, BF16),
                        pltpu.VMEM(w_out.shape, BF16),
                        pltpu.VMEM(w_rt.shape, BF16),
                        pltpu.VMEM((nt + SUBLANES, LANES), F32)],
        compiler_params=pltpu.CompilerParams(dimension_semantics=("arbitrary",),
                                             vmem_limit_bytes=VMEM_LIMIT),
        name="mixer_router",
    )(x2d, *ins)


def _expert_kernel(meta_ref, src_ref, dst_ref, stage_ref,
                   wg_hbm, wu_hbm, wd_hbm, out_ref,
                   xbuf, obuf, wg_f, wu_f, wd_f, wg_s, wu_s, wd_s, sem, osem, wsem):
    i = pl.program_id(0)
    n_used = meta_ref[3]
    slot = i % N_SLOTS
    bpt = BLOCKS_PER_ROW_TILE
    dump_row = dst_ref.shape[0] // bpt - SUBLANES
    expert_of = lambda tile: meta_ref[tile * META_W]

    def gather_copy(seq, j):
        tile = jnp.minimum(seq, n_used - 1)
        src = pl.multiple_of(src_ref[tile * bpt + j] * BLOCK, BLOCK)
        slot_ = seq % N_SLOTS
        return pltpu.make_async_copy(stage_ref.at[pl.ds(src, BLOCK)],
                                     xbuf.at[slot_, pl.ds(j * BLOCK, BLOCK)], sem.at[slot_])

    def scatter_copy(tile, j):
        row = jnp.where(tile < 0, dump_row, tile)
        dst = pl.multiple_of(dst_ref[row * bpt + j] * BLOCK, BLOCK)
        slot_ = (tile + N_SLOTS) % N_SLOTS
        return pltpu.make_async_copy(obuf.at[slot_, pl.ds(j * BLOCK, BLOCK)],
                                     out_ref.at[pl.ds(dst, BLOCK)], osem.at[slot_])

    def each_block(fn):
        for j in range(BLOCKS_PER_ROW_TILE):
            fn(j)

    def weight_copies(e):
        return (pltpu.make_async_copy(wg_hbm.at[e], wg_f, wsem.at[0]),
                pltpu.make_async_copy(wu_hbm.at[e], wu_f, wsem.at[1]),
                pltpu.make_async_copy(wd_hbm.at[e], wd_f, wsem.at[2]))

    @pl.when(i == 0)
    def _first():
        each_block(lambda j: gather_copy(0, j).start())
        each_block(lambda j: gather_copy(1, j).start())
        for cp in weight_copies(expert_of(0)):
            cp.start(priority=WEIGHT_DMA_PRIORITY)
        obuf[...] = jnp.zeros_like(obuf)

    @pl.when((i >= 2) & (i - 3 < n_used))
    def _free_out_slot():
        each_block(lambda j: scatter_copy(i - 3, j).wait())

    @pl.when(i == n_used)
    def _after_last_tile():
        each_block(lambda j: gather_copy(n_used, j).wait())
        each_block(lambda j: gather_copy(n_used + 1, j).wait())
        each_block(lambda j: scatter_copy(n_used - 1, j).start())

    @pl.when(i < n_used)
    def _tile():
        e = expert_of(i)
        prev = expert_of(jnp.maximum(i - 1, 0))

        @pl.when((i == 0) | (e != prev))
        def _new_expert():
            for cp in weight_copies(e):
                cp.wait()
            wg_s[...] = wg_f[...].astype(BF16)
            wu_s[...] = wu_f[...].astype(BF16)
            wd_s[...] = wd_f[...].astype(BF16)
            nxt = meta_ref[i * META_W + 1]

            @pl.when(nxt >= 0)
            def _next_weights():
                for cp in weight_copies(nxt):
                    cp.start(priority=WEIGHT_DMA_PRIORITY)

        each_block(lambda j: gather_copy(i, j).wait())

        def mlp(rows):
            hb = xbuf[slot, 0:rows, :]
            a = jnp.dot(hb, wg_s[...], preferred_element_type=F32)
            b = jnp.dot(hb, wu_s[...], preferred_element_type=F32)
            each_block(lambda j: gather_copy(i + 2, j).start())
            each_block(lambda j: scatter_copy(i - 1, j).start())
            hid = (a * jax.nn.sigmoid(a) * b).astype(BF16)
            obuf[slot, 0:rows, :] = jnp.dot(hid, wd_s[...], preferred_element_type=F32).astype(BF16)

        rows_needed = meta_ref[i * META_W + 2]
        for rows in range(ROW_STEP, TR + 1, ROW_STEP):
            pl.when(rows_needed == rows)(functools.partial(mlp, rows))


def _experts(meta, src_block, dst_block, stage, w_gate, w_up, w_down, n_tiles):
    any_spec = pl.BlockSpec(memory_space=pl.ANY)
    return pl.pallas_call(
        _expert_kernel,
        grid_spec=pltpu.PrefetchScalarGridSpec(
            num_scalar_prefetch=3,
            grid=(n_tiles,),
            in_specs=[any_spec, any_spec, any_spec, any_spec],
            out_specs=any_spec,
            scratch_shapes=[pltpu.VMEM((N_SLOTS, TR, D_MODEL), BF16),
                            pltpu.VMEM((N_SLOTS, TR, D_MODEL), BF16),
                            pltpu.VMEM((D_MODEL, D_EXPERT), F32),
                            pltpu.VMEM((D_MODEL, D_EXPERT), F32),
                            pltpu.VMEM((D_EXPERT, D_MODEL), F32),
                            pltpu.VMEM((D_MODEL, D_EXPERT), BF16),
                            pltpu.VMEM((D_MODEL, D_EXPERT), BF16),
                            pltpu.VMEM((D_EXPERT, D_MODEL), BF16),
                            pltpu.SemaphoreType.DMA((N_SLOTS,)),
                            pltpu.SemaphoreType.DMA((N_SLOTS,)),
                            pltpu.SemaphoreType.DMA((3,))]),
        out_shape=jax.ShapeDtypeStruct(stage.shape, stage.dtype),
        input_output_aliases={3: 0},
        compiler_params=pltpu.CompilerParams(dimension_semantics=("arbitrary",),
                                             vmem_limit_bytes=VMEM_LIMIT),
        name="experts",
    )(meta, src_block, dst_block, stage, w_gate, w_up, w_down)


def _combine_kernel(used_ref, x1_ref, route_ref, p_ref, gple_ref, wpg_f32, wpp_f32, gfin_ref,
                    srt_ref, srt_hbm, out_ref, y_s, wpg_ref, wpp_ref, tail_s, tsem):
    i = pl.program_id(0)

    @pl.when(i == 0)
    def _init():
        wpg_ref[...] = wpg_f32[...].astype(BF16)
        wpp_ref[...] = wpp_f32[...].astype(BF16)

    hrows = TM // 2
    route = jnp.concatenate(
        [route_ref[...], jnp.zeros((LANES - SUBLANES, TM), F32)], axis=0).T

    def unsort(lo, hi, rs, rows):
        w1, w2, pos1, pos2 = (route[rs, k:k + 1] for k in (2, 3, 4, 5))
        col = (lax.broadcasted_iota(I32, (hrows, hi - lo), 1) + lo).astype(F32)
        mat = jnp.where(col == pos1, w1, jnp.where(col == pos2, w2, 0.0)).astype(BF16)
        return jnp.dot(mat, rows, preferred_element_type=F32)

    def main_rows():
        return srt_ref[...].reshape(SORT_MAIN, D_MODEL)

    def finish(y, rs, pe):
        x2 = x1_ref[rs, :] + y
        hg = _rms(x2, gple_ref[...]).astype(BF16)
        gate = jax.nn.sigmoid(jnp.dot(hg, wpg_ref[...], preferred_element_type=F32))
        x3 = x2 + gate * pe[rs, :]
        out_ref[rs, :] = _rms(x3, gfin_ref[...])

    @pl.when(used_ref[i] <= SORT_MAIN)
    def _common():
        pe = jnp.dot(p_ref[...].astype(BF16), wpp_ref[...], preferred_element_type=F32)
        y0 = unsort(0, SORT_MAIN, slice(0, hrows), main_rows())
        y1 = unsort(0, SORT_MAIN, slice(hrows, TM), main_rows())
        finish(y0, slice(0, hrows), pe)
        finish(y1, slice(hrows, TM), pe)

    @pl.when(used_ref[i] > SORT_MAIN)
    def _with_tails():
        tail = pltpu.make_async_copy(srt_hbm.at[i, SORT_MAIN // STAGE_CHUNK], tail_s, tsem.at[0])
        tail.start()
        tail.wait()
        pe = jnp.dot(p_ref[...].astype(BF16), wpp_ref[...], preferred_element_type=F32)
        for h in range(2):
            rs = slice(h * hrows, (h + 1) * hrows)
            y_s[rs, :] = unsort(0, SORT_MAIN, rs, main_rows())
            for lo in range(SORT_MAIN, KOUT, TAIL):
                @pl.when(used_ref[i] > lo)
                def _tail(lo=lo, rs=rs):
                    y_s[rs, :] += unsort(lo, lo + TAIL, rs,
                                         tail_s[lo - SORT_MAIN:lo - SORT_MAIN + TAIL, :])
            finish(y_s[rs, :], rs, pe)


def _combine(used, x1, route, p2d, g_ple, w_pg, w_pp, g_fin, sorted_out):
    t = p2d.shape[0]
    full = lambda a: pl.BlockSpec(a.shape, lambda i, *_: (0,) * a.ndim)
    chunks = sorted_out.reshape(-1, KOUT // STAGE_CHUNK, STAGE_CHUNK, D_MODEL)
    return pl.pallas_call(
        _combine_kernel,
        grid_spec=pltpu.PrefetchScalarGridSpec(
            num_scalar_prefetch=1,
            grid=(t // TM,),
            in_specs=[pl.BlockSpec((TM, D_MODEL), lambda i, *_: (i, 0)),
                      pl.BlockSpec((SUBLANES, TM), lambda i, *_: (i, 0)),
                      pl.BlockSpec((TM, PLE_DIM), lambda i, *_: (i, 0)),
                      full(g_ple), full(w_pg), full(w_pp), full(g_fin),
                      pl.BlockSpec((None, SORT_MAIN // STAGE_CHUNK, STAGE_CHUNK, D_MODEL),
                                   lambda i, *_: (i, 0, 0, 0)),
                      pl.BlockSpec(memory_space=pl.ANY)],
            out_specs=pl.BlockSpec((TM, D_MODEL), lambda i, *_: (i, 0)),
            scratch_shapes=[pltpu.VMEM((TM, D_MODEL), F32),
                            pltpu.VMEM(w_pg.shape, BF16),
                            pltpu.VMEM(w_pp.shape, BF16),
                            pltpu.VMEM((STAGE_CHUNK, D_MODEL), BF16),
                            pltpu.SemaphoreType.DMA((1,))]),
        out_shape=jax.ShapeDtypeStruct((t, D_MODEL), F32),
        compiler_params=pltpu.CompilerParams(dimension_semantics=("arbitrary",),
                                             vmem_limit_bytes=VMEM_LIMIT),
        name="combine_ple",
    )(used, x1, route, p2d, g_ple, w_pg, w_pp, g_fin, chunks, chunks)


def kernel(x, p, g_mix, w_in, w_conv, g_sgu, w_spatial, b_spatial, w_out, g_ffn, w_group, b_group,
           w_router, b_router, w_gate, w_up, w_down, g_ple, w_ple_gate, w_ple_proj, g_final):
    bsz, seq, d = x.shape
    t = bsz * seq
    assert w_in.shape[0] == 1, "single-layer block"
    assert d == D_MODEL and seq % TM == 0 and TM % CHUNK == 0
    nt = t // TM
    max_rows = TOP_K * t + nt * N_EXPERTS * (BLOCK - 1)
    n_tiles = (max_rows + N_EXPERTS * (TR - 1)) // TR + N_SLOTS
    l = 0

    w_rt = jnp.concatenate(
        [w_group[l], jnp.transpose(w_router[l], (1, 0, 2)).reshape(d, N_EXPERTS)], axis=1)
    w_rt = jnp.pad(w_rt, ((0, 0), (0, LANES - w_rt.shape[1]))).T
    b_rt = jnp.pad(jnp.concatenate([b_group[l], b_router[l].reshape(-1)]),
                   (0, LANES - N_GROUPS - N_EXPERTS)).reshape(LANES, 1)
    b_sp = jnp.repeat(b_spatial[l].T, SGU_WIDTH // SGU_HEADS, axis=1)

    x1, route, stage, meta, src, dst, used = _mixer_router(
        x.reshape(t, d), g_mix[l].reshape(1, d), w_in[l], w_conv[l],
        g_sgu[l].reshape(1, -1), w_spatial[l], b_sp, w_out[l],
        g_ffn[l].reshape(1, d), w_rt, b_rt, seq, n_tiles)

    bpt = BLOCKS_PER_ROW_TILE
    meta, src, dst, used = (meta[:, :META_W].reshape(-1), src[:, :bpt].reshape(-1),
                            dst[:, :bpt].reshape(-1), used[:, 0])
    sorted_out = _experts(meta, src, dst, stage,
                          w_gate[l].reshape(N_EXPERTS, d, D_EXPERT),
                          w_up[l].reshape(N_EXPERTS, d, D_EXPERT),
                          w_down[l].reshape(N_EXPERTS, D_EXPERT, d), n_tiles)
    out = _combine(used, x1, route, p[l].reshape(t, PLE_DIM), g_ple[l].reshape(1, d),
                   w_ple_gate[l], w_ple_proj[l], g_final.reshape(1, d), sorted_out)
    return out.reshape(bsz, seq, d)
```

```python
import functools

import jax
import jax.numpy as jnp
from jax import lax
from jax.experimental import pallas as pl
from jax.experimental.pallas import tpu as pltpu

F32 = jnp.float32
BF16 = jnp.bfloat16
I32 = jnp.int32

EPS = 1e-6
D_MODEL = 1024
CONV_WIDTH = 512
SGU_WIDTH = 512
SGU_HEADS = 8
HEAD_PAIRS = SGU_HEADS // 2
CHUNK = 128
N_GROUPS = 4
EXPERTS_PER_GROUP = 8
N_EXPERTS = N_GROUPS * EXPERTS_PER_GROUP
TOP_K = 2
D_EXPERT = 512
PLE_DIM = 256
LANES = 128
SUBLANES = 8
BLOCK = 2 * SUBLANES
ROUTE_LANE0 = N_GROUPS
ROUTE_ROWS = 40

TM = 512
TR = 512
BLOCKS_PER_ROW_TILE = TR // BLOCK
KOUT = TOP_K * TM + N_EXPERTS * BLOCK
TAIL = 128
SORT_MAIN = KOUT - 2 * TAIL
STAGE_CHUNK = KOUT - SORT_MAIN
META_W = 4
ROW_STEP = 128
N_SLOTS = 3
DUMP_BLOCKS = N_SLOTS * BLOCKS_PER_ROW_TILE
assert DUMP_BLOCKS * BLOCK <= KOUT
WEIGHT_DMA_PRIORITY = 1
VMEM_LIMIT = 56 * 1024 * 1024


def _rms(x, g):
    return x * lax.rsqrt(jnp.mean(x * x, axis=-1, keepdims=True) + EPS) * g


def _build_tables(n_s, nt, meta_ref, src_ref, dst_ref, used_ref):
    tp = meta_ref.shape[0]
    bpt = BLOCKS_PER_ROW_TILE
    kb = KOUT // BLOCK
    n = n_s[0:nt, :]
    lane_n = lax.broadcasted_iota(I32, (nt, LANES), 1)
    is_e = (lane_n >= ROUTE_LANE0) & (lane_n < ROUTE_LANE0 + N_EXPERTS)
    nblk = jnp.where(is_e, jnp.floor((n + (BLOCK - 1)) * (1.0 / BLOCK)), 0.0)
    er = lax.broadcasted_iota(I32, (LANES, LANES), 0)
    ec = lax.broadcasted_iota(I32, (LANES, LANES), 1)
    loc = jnp.dot(nblk.astype(BF16), (er < ec).astype(BF16), preferred_element_type=F32)
    ti = lax.broadcasted_iota(I32, (nt, nt), 0)
    tj = lax.broadcasted_iota(I32, (nt, nt), 1)
    carry = jnp.dot((tj < ti).astype(BF16), nblk.astype(BF16), preferred_element_type=F32)
    total = jnp.sum(nblk, axis=0, keepdims=True)
    tiles_e = jnp.floor((total + (bpt - 1)) * (1.0 / bpt))
    tile_end = jnp.dot(jnp.broadcast_to(tiles_e, (SUBLANES, LANES)).astype(BF16),
                       (er <= ec).astype(BF16), preferred_element_type=F32)[0:1, :]
    tile0 = tile_end - tiles_e
    n_used = jnp.max(tile_end, axis=1, keepdims=True)

    tau = lax.broadcasted_iota(I32, (tp, LANES), 0).astype(F32)
    lane_i = lax.broadcasted_iota(I32, (tp, LANES), 1)
    lane_t = lane_i.astype(F32)
    is_e_t = (lane_i >= ROUTE_LANE0) & (lane_i < ROUTE_LANE0 + N_EXPERTS)
    te = jnp.sum(jnp.where(is_e_t & (tile_end <= jnp.minimum(tau, n_used - 1.0)), 1.0, 0.0),
                 axis=1, keepdims=True)
    oh = lane_t == te + ROUTE_LANE0
    pick = lambda row: jnp.sum(jnp.where(oh, row, 0.0), axis=1, keepdims=True)
    seg0_t = pick(tile0) * bpt
    e_lane = lane_t - ROUTE_LANE0
    cand = jnp.where(is_e_t & (e_lane > te) & (tiles_e > 0.0), e_lane, 99.0)
    nxt = jnp.min(cand, axis=1, keepdims=True)
    nxt = jnp.where(nxt == 99.0, -1.0, nxt)
    rows_t = jnp.clip(pick(total) * BLOCK - (tau[:, 0:1] - pick(tile0)) * TR, ROW_STEP, TR)
    rows_t = jnp.ceil(rows_t * (1.0 / ROW_STEP)) * ROW_STEP
    meta = jnp.where(lane_i == 0, te, 0.0)
    meta = jnp.where(lane_i == 1, nxt, meta)
    meta = jnp.where(lane_i == 2, rows_t, meta)
    meta = jnp.where(lane_i == 3, n_used, meta)
    meta_ref[...] = meta.astype(I32)

    bidx = tau * bpt + lane_t
    off = bidx - seg0_t
    ohb = jnp.where(oh, 1.0, 0.0).astype(BF16)

    def per_tile(tab):
        hi = jnp.floor(tab * (1.0 / 32))
        lo = tab - 32.0 * hi
        nt_dot = lambda a: lax.dot_general(ohb, a.astype(BF16), (((1,), (1,)), ((), ())),
                                           preferred_element_type=F32)
        return 32.0 * nt_dot(hi) + nt_dot(lo)

    run_end = per_tile(carry + nblk)
    run_beg = per_tile(carry)
    run_loc = per_tile(loc)
    tile_of = jnp.zeros((tp, LANES), F32)
    for it in range(nt):
        tile_of = tile_of + jnp.where(run_end[:, it:it + 1] <= off, 1.0, 0.0)
    srcv = off
    for it in range(nt):
        srcv = srcv + jnp.where(tile_of == it,
                                run_loc[:, it:it + 1] - run_beg[:, it:it + 1] + it * kb, 0.0)
    valid = (tile_of < nt) & (bidx < n_used * bpt)
    slot3 = tau - N_SLOTS * jnp.floor((tau + 0.5) * (1.0 / N_SLOTS))
    dump = nt * kb + slot3 * bpt + lane_t
    src_ref[...] = jnp.where(valid, srcv, kb - 1.0).astype(I32)
    dst_ref[0:tp, :] = jnp.where(valid, srcv, dump).astype(I32)
    dst_ref[tp:tp + SUBLANES, :] = (nt * kb + DUMP_BLOCKS - bpt
                                    + lax.broadcasted_iota(I32, (SUBLANES, LANES), 1))
    used_ref[...] = jnp.broadcast_to(jnp.sum(nblk, axis=1, keepdims=True) * BLOCK,
                                     (nt, LANES)).astype(I32)


def _mixer_router_kernel(tiles_per_seq, nt,
                         x_ref, gmix_ref, win_f32, wconv_ref, gsgu_ref, wsp_ref, bsp_ref,
                         wout_f32, gffn_ref, wrt_f32, brt_ref,
                         x1_ref, route_ref, stage_ref, meta_ref, src_ref, dst_ref, used_ref,
                         wcat_s, tri_s, halo_s, mix_s, h2_s, win_ref, wout_ref, wrt_ref, n_s):
    i = pl.program_id(0)
    nch = TM // CHUNK

    @pl.when(i == 0)
    def _init():
        r = lax.broadcasted_iota(I32, (CHUNK, CHUNK), 0)
        c = lax.broadcasted_iota(I32, (CHUNK, CHUNK), 1)
        causal = c <= r
        for j in range(HEAD_PAIRS):
            wa = jnp.where(causal, wsp_ref[2 * j], 0.0)
            wb = jnp.where(causal, wsp_ref[2 * j + 1], 0.0)
            wcat_s[j] = jnp.concatenate([wa, wb], axis=1).astype(BF16)
        rr = lax.broadcasted_iota(I32, (TM, TM), 0)
        cc = lax.broadcasted_iota(I32, (TM, TM), 1)
        tri_s[...] = (rr < cc).astype(BF16)
        h2_s[...] = jnp.zeros_like(h2_s)
        win_ref[...] = win_f32[...].astype(BF16)
        wout_ref[...] = wout_f32[...].astype(BF16)
        wrt_ref[...] = wrt_f32[...].astype(BF16)

    @pl.when(i % tiles_per_seq == 0)
    def _seq_start():
        halo_s[...] = jnp.zeros_like(halo_s)

    def step(mix):
        h2b = h2_s[(i + 1) % 2]
        if mix:
            x = x_ref[...]
            hb = _rms(x, gmix_ref[...]).astype(BF16)

            def proj(k):
                return jnp.dot(hb, win_ref[:, k * 512:(k + 1) * 512], preferred_element_type=F32)

        logits = lax.dot_general(wrt_ref[...], h2b, (((1,), (1,)), ((), ())),
                                 preferred_element_type=F32)[0:ROUTE_ROWS, :] + brt_ref[0:ROUTE_ROWS, :]
        if mix:
            pu = proj(3)
            pv = proj(4)

        ridx = lax.broadcasted_iota(I32, (ROUTE_ROWS, TM), 0).astype(F32)
        neg = jnp.float32(-jnp.inf)
        big = jnp.float32(1e9)
        is_g = ridx < N_GROUPS
        gl = jnp.where(is_g, logits, neg)
        gmax = jnp.max(gl, axis=0, keepdims=True)
        gsum = jnp.sum(jnp.where(is_g, jnp.exp(gl - gmax), 0.0), axis=0, keepdims=True)
        g_w = 1.0 / gsum
        g_idx = jnp.min(jnp.where(gl == gmax, ridx, big), axis=0, keepdims=True)
        lo_row = ROUTE_LANE0 + EXPERTS_PER_GROUP * g_idx
        in_grp = (ridx >= lo_row) & (ridx < lo_row + EXPERTS_PER_GROUP)
        el = jnp.where(in_grp, logits, neg)
        v1 = jnp.max(el, axis=0, keepdims=True)
        i1 = jnp.min(jnp.where(el == v1, ridx, big), axis=0, keepdims=True)
        el2 = jnp.where(ridx == i1, neg, el)
        v2 = jnp.max(el2, axis=0, keepdims=True)
        i2 = jnp.min(jnp.where(el2 == v2, ridx, big), axis=0, keepdims=True)
        e21 = jnp.exp(v2 - v1)
        w1 = g_w / (1.0 + e21)
        w2 = g_w * e21 / (1.0 + e21)

        if mix:
            u = jax.nn.gelu(pu)
            zc = proj(1) * proj(2)
            v = jax.nn.gelu(pv)
            vc = v - jnp.mean(v, axis=-1, keepdims=True)
            vn = vc * lax.rsqrt(jnp.mean(vc * vc, axis=-1, keepdims=True) + EPS) * gsgu_ref[...]
            vnb = vn.astype(BF16)

            row = lax.broadcasted_iota(I32, (TM, CONV_WIDTH), 0)
            h6 = halo_s[6:7, :]
            h7 = halo_s[7:8, :]
            z1 = jnp.where(row == 0, h7, pltpu.roll(zc, 1, 0))
            z2 = jnp.where(row == 0, h6, jnp.where(row == 1, h7, pltpu.roll(zc, 2, 0)))
            conv = z2 * wconv_ref[0:1, :] + z1 * wconv_ref[1:2, :] + zc * wconv_ref[2:3, :]
            halo_s[...] = zc[TM - 8:TM, :]
            mix_s[:, 0:CONV_WIDTH] = (proj(0) * conv).astype(BF16)

        sel1 = ridx == i1
        sel2 = ridx == i2
        onehot = jnp.where(sel1 | sel2, 1.0, 0.0)
        counts = jnp.sum(onehot, axis=1, keepdims=True)
        pad_rows = jnp.zeros((LANES - ROUTE_ROWS, LANES), F32)
        counts_sq = jnp.concatenate([jnp.broadcast_to(counts, (ROUTE_ROWS, LANES)), pad_rows], axis=0)
        blocks_sq = jnp.ceil(counts_sq * (1.0 / BLOCK))
        er = lax.broadcasted_iota(I32, (LANES, LANES), 0)
        ec = lax.broadcasted_iota(I32, (LANES, LANES), 1)
        before = (ec < er).astype(BF16)
        run_start = BLOCK * jnp.dot(before, blocks_sq.astype(BF16),
                                    preferred_element_type=F32)[0:ROUTE_ROWS, 0:1]
        rank = jnp.dot(onehot.astype(BF16), tri_s[...], preferred_element_type=F32) + run_start
        pos1_row = jnp.sum(jnp.where(sel1, rank, 0.0), axis=0, keepdims=True)
        pos2_row = jnp.sum(jnp.where(sel2, rank, 0.0), axis=0, keepdims=True)
        n_s[pl.ds(jnp.where(i == 0, nt, i - 1), 1), :] = counts_sq.T[0:1, :]

        r8 = lax.broadcasted_iota(I32, (SUBLANES, TM), 0)
        rows8 = jnp.where(r8 == 0, i1 - ROUTE_LANE0, 0.0)
        rows8 = jnp.where(r8 == 1, i2 - ROUTE_LANE0, rows8)
        rows8 = jnp.where(r8 == 2, w1, rows8)
        rows8 = jnp.where(r8 == 3, w2, rows8)
        rows8 = jnp.where(r8 == 4, pos1_row, rows8)
        rows8 = jnp.where(r8 == 5, pos2_row, rows8)
        route_ref[...] = jnp.concatenate(
            [rows8, jnp.zeros((LANES - SUBLANES, TM), F32)], axis=0).T

        def sort_rows(lo, hi):
            out_row = (lax.broadcasted_iota(I32, (hi - lo, TM), 0) + lo).astype(F32)
            sort_mat = jnp.where((out_row == pos1_row) | (out_row == pos2_row), 1.0, 0.0).astype(BF16)
            stage_ref[lo:hi, :] = jnp.dot(sort_mat, h2b, preferred_element_type=F32).astype(BF16)

        sort_rows(0, SORT_MAIN // 2)
        if mix:
            left = lax.broadcasted_iota(I32, (CHUNK, LANES), 1) < (LANES // 2)
            zero = jnp.zeros((CHUNK, LANES), BF16)
            for j in range(HEAD_PAIRS):
                cols = []
                for c in range(nch):
                    s = vnb[c * CHUNK:(c + 1) * CHUNK, j * LANES:(j + 1) * LANES]
                    cols.append(jnp.concatenate([jnp.where(left, s, zero), jnp.where(left, zero, s)],
                                                axis=0))
                rhs = jnp.concatenate(cols, axis=1)
                res = jnp.dot(wcat_s[j], rhs, preferred_element_type=F32)
                bias = bsp_ref[:, j * LANES:(j + 1) * LANES]
                for c in range(nch):
                    mixed = res[:, c * LANES:(c + 1) * LANES] + bias
                    uu = u[c * CHUNK:(c + 1) * CHUNK, j * LANES:(j + 1) * LANES]
                    mix_s[c * CHUNK:(c + 1) * CHUNK,
                          CONV_WIDTH + j * LANES:CONV_WIDTH + (j + 1) * LANES] = (uu * mixed).astype(BF16)

        sort_rows(SORT_MAIN // 2, SORT_MAIN)
        if mix:
            x1 = x + jnp.dot(mix_s[...], wout_ref[...], preferred_element_type=F32)
            x1_ref[...] = x1
            h2_s[i % 2] = _rms(x1, gffn_ref[...]).astype(BF16)

        used_rows = BLOCK * jnp.sum(blocks_sq[:, 0:1])
        for lo in range(SORT_MAIN, KOUT, TAIL):
            @pl.when(used_rows > lo)
            def _tail(lo=lo):
                sort_rows(lo, lo + TAIL)

            @pl.when(used_rows <= lo)
            def _empty_tail(lo=lo):
                stage_ref[lo:lo + TAIL, :] = jnp.zeros((TAIL, D_MODEL), BF16)

    pl.when(i < nt)(functools.partial(step, True))

    @pl.when(i == nt)
    def _last():
        step(False)
        _build_tables(n_s, nt, meta_ref, src_ref, dst_ref, used_ref)


def _mixer_router(x2d, g_mix, w_in, w_conv, g_sgu, w_sp, b_sp, w_out, g_ffn, w_rt, b_rt, seq,
                  n_tiles):
    t = x2d.shape[0]
    nt = t // TM
    tp = -(-n_tiles // SUBLANES) * SUBLANES
    whole = lambda r: pl.BlockSpec((r, LANES), lambda i: (0, 0))
    full = lambda a: pl.BlockSpec(a.shape, lambda i: (0,) * a.ndim)
    ins = (g_mix, w_in, w_conv, g_sgu, w_sp, b_sp, w_out, g_ffn, w_rt, b_rt)
    prev = lambda i: (jnp.where(i == 0, nt, i - 1), 0)
    return pl.pallas_call(
        functools.partial(_mixer_router_kernel, seq // TM, nt),
        grid=(nt + 1,),
        in_specs=[pl.BlockSpec((TM, D_MODEL), lambda i: (jnp.minimum(i, nt - 1), 0))]
        + [full(a) for a in ins],
        out_specs=[pl.BlockSpec((TM, D_MODEL), lambda i: (jnp.minimum(i, nt - 1), 0)),
                   pl.BlockSpec((TM, LANES), prev),
                   pl.BlockSpec((KOUT, D_MODEL), prev),
                   whole(tp), whole(tp), whole(tp + SUBLANES), whole(nt)],
        out_shape=[jax.ShapeDtypeStruct((t, D_MODEL), F32),
                   jax.ShapeDtypeStruct(((nt + 1) * TM, LANES), F32),
                   jax.ShapeDtypeStruct(((nt + 1) * KOUT, D_MODEL), BF16),
                   jax.ShapeDtypeStruct((tp, LANES), I32),
                   jax.ShapeDtypeStruct((tp, LANES), I32),
                   jax.ShapeDtypeStruct((tp + SUBLANES, LANES), I32),
                   jax.ShapeDtypeStruct((nt, LANES), I32)],
        scratch_shapes=[pltpu.VMEM((HEAD_PAIRS, CHUNK, 2 * CHUNK), BF16),
                        pltpu.VMEM((TM, TM), BF16),
                        pltpu.VMEM((8, CONV_WIDTH), F32),
                        pltpu.VMEM((TM, D_MODEL), BF16),
                        pltpu.VMEM((2, TM, D_MODEL), BF16),
                        pltpu.VMEM(w_in.shape, BF16),
                        pltpu.VMEM(w_out.shape, BF16),
                        pltpu.VMEM(w_rt.shape, BF16),
                        pltpu.VMEM((nt + SUBLANES, LANES), F32)],
        compiler_params=pltpu.CompilerParams(dimension_semantics=("arbitrary",),
                                             vmem_limit_bytes=VMEM_LIMIT),
        name="mixer_router",
    )(x2d, *ins)


def _expert_kernel(meta_ref, src_ref, dst_ref, stage_ref,
                   wg_hbm, wu_hbm, wd_hbm, out_ref,
                   xbuf, obuf, wg_f, wu_f, wd_f, wg_s, wu_s, wd_s, sem, osem, wsem):
    i = pl.program_id(0)
    n_used = meta_ref[3]
    slot = i % N_SLOTS
    bpt = BLOCKS_PER_ROW_TILE
    dump_row = dst_ref.shape[0] // bpt - SUBLANES
    expert_of = lambda tile: meta_ref[tile * META_W]

    def gather_copy(seq, j):
        tile = jnp.minimum(seq, n_used - 1)
        src = pl.multiple_of(src_ref[tile * bpt + j] * BLOCK, BLOCK)
        slot_ = seq % N_SLOTS
        return pltpu.make_async_copy(stage_ref.at[pl.ds(src, BLOCK)],
                                     xbuf.at[slot_, pl.ds(j * BLOCK, BLOCK)], sem.at[slot_])

    def scatter_copy(tile, j):
        row = jnp.where(tile < 0, dump_row, tile)
        dst = pl.multiple_of(dst_ref[row * bpt + j] * BLOCK, BLOCK)
        slot_ = (tile + N_SLOTS) % N_SLOTS
        return pltpu.make_async_copy(obuf.at[slot_, pl.ds(j * BLOCK, BLOCK)],
                                     out_ref.at[pl.ds(dst, BLOCK)], osem.at[slot_])

    def each_block(fn):
        for j in range(BLOCKS_PER_ROW_TILE):
            fn(j)

    def weight_copies(e):
        return (pltpu.make_async_copy(wg_hbm.at[e], wg_f, wsem.at[0]),
                pltpu.make_async_copy(wu_hbm.at[e], wu_f, wsem.at[1]),
                pltpu.make_async_copy(wd_hbm.at[e], wd_f, wsem.at[2]))

    @pl.when(i == 0)
    def _first():
        each_block(lambda j: gather_copy(0, j).start())
        each_block(lambda j: gather_copy(1, j).start())
        for cp in weight_copies(expert_of(0)):
            cp.start(priority=WEIGHT_DMA_PRIORITY)
        obuf[...] = jnp.zeros_like(obuf)

    @pl.when((i >= 2) & (i - 3 < n_used))
    def _free_out_slot():
        each_block(lambda j: scatter_copy(i - 3, j).wait())

    @pl.when(i == n_used)
    def _after_last_tile():
        each_block(lambda j: gather_copy(n_used, j).wait())
        each_block(lambda j: gather_copy(n_used + 1, j).wait())
        each_block(lambda j: scatter_copy(n_used - 1, j).start())

    @pl.when(i < n_used)
    def _tile():
        e = expert_of(i)
        prev = expert_of(jnp.maximum(i - 1, 0))

        @pl.when((i == 0) | (e != prev))
        def _new_expert():
            for cp in weight_copies(e):
                cp.wait()
            wg_s[...] = wg_f[...].astype(BF16)
            wu_s[...] = wu_f[...].astype(BF16)
            wd_s[...] = wd_f[...].astype(BF16)
            nxt = meta_ref[i * META_W + 1]

            @pl.when(nxt >= 0)
            def _next_weights():
                for cp in weight_copies(nxt):
                    cp.start(priority=WEIGHT_DMA_PRIORITY)

        each_block(lambda j: gather_copy(i, j).wait())

        def mlp(rows):
            hb = xbuf[slot, 0:rows, :]
            a = jnp.dot(hb, wg_s[...], preferred_element_type=F32)
            b = jnp.dot(hb, wu_s[...], preferred_element_type=F32)
            each_block(lambda j: gather_copy(i + 2, j).start())
            each_block(lambda j: scatter_copy(i - 1, j).start())
            hid = (a * jax.nn.sigmoid(a) * b).astype(BF16)
            obuf[slot, 0:rows, :] = jnp.dot(hid, wd_s[...], preferred_element_type=F32).astype(BF16)

        rows_needed = meta_ref[i * META_W + 2]
        for rows in range(ROW_STEP, TR + 1, ROW_STEP):
            pl.when(rows_needed == rows)(functools.partial(mlp, rows))


def _experts(meta, src_block, dst_block, stage, w_gate, w_up, w_down, n_tiles):
    any_spec = pl.BlockSpec(memory_space=pl.ANY)
    return pl.pallas_call(
        _expert_kernel,
        grid_spec=pltpu.PrefetchScalarGridSpec(
            num_scalar_prefetch=3,
            grid=(n_tiles,),
            in_specs=[any_spec, any_spec, any_spec, any_spec],
            out_specs=any_spec,
            scratch_shapes=[pltpu.VMEM((N_SLOTS, TR, D_MODEL), BF16),
                            pltpu.VMEM((N_SLOTS, TR, D_MODEL), BF16),
                            pltpu.VMEM((D_MODEL, D_EXPERT), F32),
                            pltpu.VMEM((D_MODEL, D_EXPERT), F32),
                            pltpu.VMEM((D_EXPERT, D_MODEL), F32),
                            pltpu.VMEM((D_MODEL, D_EXPERT), BF16),
                            pltpu.VMEM((D_MODEL, D_EXPERT), BF16),
                            pltpu.VMEM((D_EXPERT, D_MODEL), BF16),
                            pltpu.SemaphoreType.DMA((N_SLOTS,)),
                            pltpu.SemaphoreType.DMA((N_SLOTS,)),
                            pltpu.SemaphoreType.DMA((3,))]),
        out_shape=jax.ShapeDtypeStruct(stage.shape, stage.dtype),
        input_output_aliases={3: 0},
        compiler_params=pltpu.CompilerParams(dimension_semantics=("arbitrary",),
                                             vmem_limit_bytes=VMEM_LIMIT),
        name="experts",
    )(meta, src_block, dst_block, stage, w_gate, w_up, w_down)


def _combine_kernel(used_ref, x1_ref, route_ref, p_ref, gple_ref, wpg_f32, wpp_f32, gfin_ref,
                    srt_ref, srt_hbm, out_ref, y_s, wpg_ref, wpp_ref, tail_s, tsem):
    i = pl.program_id(0)

    @pl.when(i == 0)
    def _init():
        wpg_ref[...] = wpg_f32[...].astype(BF16)
        wpp_ref[...] = wpp_f32[...].astype(BF16)

    hrows = TM // 2

    def unsort(lo, hi, rs, rows):
        route = route_ref[rs, :]
        w1, w2, pos1, pos2 = (route[:, k:k + 1] for k in (2, 3, 4, 5))
        col = (lax.broadcasted_iota(I32, (hrows, hi - lo), 1) + lo).astype(F32)
        mat = jnp.where(col == pos1, w1, jnp.where(col == pos2, w2, 0.0)).astype(BF16)
        return jnp.dot(mat, rows, preferred_element_type=F32)

    def main_rows():
        return srt_ref[...].reshape(SORT_MAIN, D_MODEL)

    def finish(y, rs, pe):
        x2 = x1_ref[rs, :] + y
        hg = _rms(x2, gple_ref[...]).astype(BF16)
        gate = jax.nn.sigmoid(jnp.dot(hg, wpg_ref[...], preferred_element_type=F32))
        x3 = x2 + gate * pe[rs, :]
        out_ref[rs, :] = _rms(x3, gfin_ref[...])

    @pl.when(used_ref[i] <= SORT_MAIN)
    def _common():
        pe = jnp.dot(p_ref[...].astype(BF16), wpp_ref[...], preferred_element_type=F32)
        y0 = unsort(0, SORT_MAIN, slice(0, hrows), main_rows())
        y1 = unsort(0, SORT_MAIN, slice(hrows, TM), main_rows())
        finish(y0, slice(0, hrows), pe)
        finish(y1, slice(hrows, TM), pe)

    @pl.when(used_ref[i] > SORT_MAIN)
    def _with_tails():
        tail = pltpu.make_async_copy(srt_hbm.at[i, SORT_MAIN // STAGE_CHUNK], tail_s, tsem.at[0])
        tail.start()
        tail.wait()
        pe = jnp.dot(p_ref[...].astype(BF16), wpp_ref[...], preferred_element_type=F32)
        for h in range(2):
            rs = slice(h * hrows, (h + 1) * hrows)
            y_s[rs, :] = unsort(0, SORT_MAIN, rs, main_rows())
            for lo in range(SORT_MAIN, KOUT, TAIL):
                @pl.when(used_ref[i] > lo)
                def _tail(lo=lo, rs=rs):
                    y_s[rs, :] += unsort(lo, lo + TAIL, rs,
                                         tail_s[lo - SORT_MAIN:lo - SORT_MAIN + TAIL, :])
            finish(y_s[rs, :], rs, pe)


def _combine(used, x1, route, p2d, g_ple, w_pg, w_pp, g_fin, sorted_out):
    t = p2d.shape[0]
    full = lambda a: pl.BlockSpec(a.shape, lambda i, *_: (0,) * a.ndim)
    chunks = sorted_out.reshape(-1, KOUT // STAGE_CHUNK, STAGE_CHUNK, D_MODEL)
    return pl.pallas_call(
        _combine_kernel,
        grid_spec=pltpu.PrefetchScalarGridSpec(
            num_scalar_prefetch=1,
            grid=(t // TM,),
            in_specs=[pl.BlockSpec((TM, D_MODEL), lambda i, *_: (i, 0)),
                      pl.BlockSpec((TM, LANES), lambda i, *_: (i, 0)),
                      pl.BlockSpec((TM, PLE_DIM), lambda i, *_: (i, 0)),
                      full(g_ple), full(w_pg), full(w_pp), full(g_fin),
                      pl.BlockSpec((None, SORT_MAIN // STAGE_CHUNK, STAGE_CHUNK, D_MODEL),
                                   lambda i, *_: (i, 0, 0, 0)),
                      pl.BlockSpec(memory_space=pl.ANY)],
            out_specs=pl.BlockSpec((TM, D_MODEL), lambda i, *_: (i, 0)),
            scratch_shapes=[pltpu.VMEM((TM, D_MODEL), F32),
                            pltpu.VMEM(w_pg.shape, BF16),
                            pltpu.VMEM(w_pp.shape, BF16),
                            pltpu.VMEM((STAGE_CHUNK, D_MODEL), BF16),
                            pltpu.SemaphoreType.DMA((1,))]),
        out_shape=jax.ShapeDtypeStruct((t, D_MODEL), F32),
        compiler_params=pltpu.CompilerParams(dimension_semantics=("arbitrary",),
                                             vmem_limit_bytes=VMEM_LIMIT),
        name="combine_ple",
    )(used, x1, route, p2d, g_ple, w_pg, w_pp, g_fin, chunks, chunks)


def kernel(x, p, g_mix, w_in, w_conv, g_sgu, w_spatial, b_spatial, w_out, g_ffn, w_group, b_group,
           w_router, b_router, w_gate, w_up, w_down, g_ple, w_ple_gate, w_ple_proj, g_final):
    bsz, seq, d = x.shape
    t = bsz * seq
    assert w_in.shape[0] == 1, "single-layer block"
    assert d == D_MODEL and seq % TM == 0 and TM % CHUNK == 0
    nt = t // TM
    max_rows = TOP_K * t + nt * N_EXPERTS * (BLOCK - 1)
    n_tiles = (max_rows + N_EXPERTS * (TR - 1)) // TR + N_SLOTS
    l = 0

    w_rt = jnp.concatenate(
        [w_group[l], jnp.transpose(w_router[l], (1, 0, 2)).reshape(d, N_EXPERTS)], axis=1)
    w_rt = jnp.pad(w_rt, ((0, 0), (0, LANES - w_rt.shape[1]))).T
    b_rt = jnp.pad(jnp.concatenate([b_group[l], b_router[l].reshape(-1)]),
                   (0, LANES - N_GROUPS - N_EXPERTS)).reshape(LANES, 1)
    b_sp = jnp.repeat(b_spatial[l].T, SGU_WIDTH // SGU_HEADS, axis=1)

    x1, route, stage, meta, src, dst, used = _mixer_router(
        x.reshape(t, d), g_mix[l].reshape(1, d), w_in[l], w_conv[l],
        g_sgu[l].reshape(1, -1), w_spatial[l], b_sp, w_out[l],
        g_ffn[l].reshape(1, d), w_rt, b_rt, seq, n_tiles)

    bpt = BLOCKS_PER_ROW_TILE
    meta, src, dst, used = (meta[:, :META_W].reshape(-1), src[:, :bpt].reshape(-1),
                            dst[:, :bpt].reshape(-1), used[:, 0])
    sorted_out = _experts(meta, src, dst, stage,
                          w_gate[l].reshape(N_EXPERTS, d, D_EXPERT),
                          w_up[l].reshape(N_EXPERTS, d, D_EXPERT),
                          w_down[l].reshape(N_EXPERTS, D_EXPERT, d), n_tiles)
    out = _combine(used, x1, route, p[l].reshape(t, PLE_DIM), g_ple[l].reshape(1, d),
                   w_ple_gate[l], w_ple_proj[l], g_final.reshape(1, d), sorted_out)
    return out.reshape(bsz, seq, d)
```

```python
import functools

import jax
import jax.numpy as jnp
from jax import lax
from jax.experimental import pallas as pl
from jax.experimental.pallas import tpu as pltpu

F32 = jnp.float32
BF16 = jnp.bfloat16
I32 = jnp.int32

EPS = 1e-6
D_MODEL = 1024
CONV_WIDTH = 512
SGU_WIDTH = 512
SGU_HEADS = 8
HEAD_PAIRS = SGU_HEADS // 2
CHUNK = 128
N_GROUPS = 4
EXPERTS_PER_GROUP = 8
N_EXPERTS = N_GROUPS * EXPERTS_PER_GROUP
TOP_K = 2
D_EXPERT = 512
PLE_DIM = 256
LANES = 128
SUBLANES = 8
BLOCK = 2 * SUBLANES
ROUTE_LANE0 = N_GROUPS
ROUTE_ROWS = 40

TM = 512
TR = 512
BLOCKS_PER_ROW_TILE = TR // BLOCK
KOUT = TOP_K * TM + N_EXPERTS * BLOCK
TAIL = 128
SORT_MAIN = KOUT - 2 * TAIL
META_W = 4
ROW_STEP = 128
N_SLOTS = 3
DUMP_BLOCKS = N_SLOTS * BLOCKS_PER_ROW_TILE
assert DUMP_BLOCKS * BLOCK <= KOUT
WEIGHT_DMA_PRIORITY = 1
VMEM_LIMIT = 56 * 1024 * 1024


def _rms(x, g):
    return x * lax.rsqrt(jnp.mean(x * x, axis=-1, keepdims=True) + EPS) * g


def _build_tables(n_s, nt, meta_ref, src_ref, dst_ref, used_ref):
    tp = meta_ref.shape[0]
    bpt = BLOCKS_PER_ROW_TILE
    kb = KOUT // BLOCK
    n = n_s[0:nt, :]
    lane_n = lax.broadcasted_iota(I32, (nt, LANES), 1)
    is_e = (lane_n >= ROUTE_LANE0) & (lane_n < ROUTE_LANE0 + N_EXPERTS)
    nblk = jnp.where(is_e, jnp.floor((n + (BLOCK - 1)) * (1.0 / BLOCK)), 0.0)
    er = lax.broadcasted_iota(I32, (LANES, LANES), 0)
    ec = lax.broadcasted_iota(I32, (LANES, LANES), 1)
    loc = jnp.dot(nblk.astype(BF16), (er < ec).astype(BF16), preferred_element_type=F32)
    ti = lax.broadcasted_iota(I32, (nt, nt), 0)
    tj = lax.broadcasted_iota(I32, (nt, nt), 1)
    carry = jnp.dot((tj < ti).astype(BF16), nblk.astype(BF16), preferred_element_type=F32)
    total = jnp.sum(nblk, axis=0, keepdims=True)
    tiles_e = jnp.floor((total + (bpt - 1)) * (1.0 / bpt))
    tile_end = jnp.dot(jnp.broadcast_to(tiles_e, (SUBLANES, LANES)).astype(BF16),
                       (er <= ec).astype(BF16), preferred_element_type=F32)[0:1, :]
    tile0 = tile_end - tiles_e
    n_used = jnp.max(tile_end, axis=1, keepdims=True)

    tau = lax.broadcasted_iota(I32, (tp, LANES), 0).astype(F32)
    lane_i = lax.broadcasted_iota(I32, (tp, LANES), 1)
    lane_t = lane_i.astype(F32)
    is_e_t = (lane_i >= ROUTE_LANE0) & (lane_i < ROUTE_LANE0 + N_EXPERTS)
    te = jnp.sum(jnp.where(is_e_t & (tile_end <= jnp.minimum(tau, n_used - 1.0)), 1.0, 0.0),
                 axis=1, keepdims=True)
    oh = lane_t == te + ROUTE_LANE0
    pick = lambda row: jnp.sum(jnp.where(oh, row, 0.0), axis=1, keepdims=True)
    seg0_t = pick(tile0) * bpt
    e_lane = lane_t - ROUTE_LANE0
    cand = jnp.where(is_e_t & (e_lane > te) & (tiles_e > 0.0), e_lane, 99.0)
    nxt = jnp.min(cand, axis=1, keepdims=True)
    nxt = jnp.where(nxt == 99.0, -1.0, nxt)
    rows_t = jnp.clip(pick(total) * BLOCK - (tau[:, 0:1] - pick(tile0)) * TR, ROW_STEP, TR)
    rows_t = jnp.ceil(rows_t * (1.0 / ROW_STEP)) * ROW_STEP
    meta = jnp.where(lane_i == 0, te, 0.0)
    meta = jnp.where(lane_i == 1, nxt, meta)
    meta = jnp.where(lane_i == 2, rows_t, meta)
    meta = jnp.where(lane_i == 3, n_used, meta)
    meta_ref[...] = meta.astype(I32)

    bidx = tau * bpt + lane_t
    off = bidx - seg0_t
    ohb = jnp.where(oh, 1.0, 0.0).astype(BF16)

    def per_tile(tab):
        hi = jnp.floor(tab * (1.0 / 32))
        lo = tab - 32.0 * hi
        nt_dot = lambda a: lax.dot_general(ohb, a.astype(BF16), (((1,), (1,)), ((), ())),
                                           preferred_element_type=F32)
        return 32.0 * nt_dot(hi) + nt_dot(lo)

    run_end = per_tile(carry + nblk)
    run_beg = per_tile(carry)
    run_loc = per_tile(loc)
    tile_of = jnp.zeros((tp, LANES), F32)
    for it in range(nt):
        tile_of = tile_of + jnp.where(run_end[:, it:it + 1] <= off, 1.0, 0.0)
    srcv = off
    for it in range(nt):
        srcv = srcv + jnp.where(tile_of == it,
                                run_loc[:, it:it + 1] - run_beg[:, it:it + 1] + it * kb, 0.0)
    valid = (tile_of < nt) & (bidx < n_used * bpt)
    slot3 = tau - N_SLOTS * jnp.floor((tau + 0.5) * (1.0 / N_SLOTS))
    dump = nt * kb + slot3 * bpt + lane_t
    src_ref[...] = jnp.where(valid, srcv, kb - 1.0).astype(I32)
    dst_ref[0:tp, :] = jnp.where(valid, srcv, dump).astype(I32)
    dst_ref[tp:tp + SUBLANES, :] = (nt * kb + DUMP_BLOCKS - bpt
                                    + lax.broadcasted_iota(I32, (SUBLANES, LANES), 1))
    used_ref[...] = jnp.broadcast_to(jnp.sum(nblk, axis=1, keepdims=True) * BLOCK,
                                     (nt, LANES)).astype(I32)


def _mixer_router_kernel(tiles_per_seq, nt,
                         x_ref, gmix_ref, win_f32, wconv_ref, gsgu_ref, wsp_ref, bsp_ref,
                         wout_f32, gffn_ref, wrt_f32, brt_ref, wg_hbm, wu_hbm, wd_hbm,
                         x1_ref, route_ref, stage_ref, meta_ref, src_ref, dst_ref, used_ref,
                         wgb_hbm, wub_hbm, wdb_hbm,
                         wcat_s, tri_s, halo_s, mix_s, h2_s, win_ref, wout_ref, wrt_ref, n_s,
                         wg_f, wu_f, wd_f, wg_b, wu_b, wd_b, wisem, wosem):
    i = pl.program_id(0)
    nch = TM // CHUNK

    @pl.when(i == 0)
    def _init():
        r = lax.broadcasted_iota(I32, (CHUNK, CHUNK), 0)
        c = lax.broadcasted_iota(I32, (CHUNK, CHUNK), 1)
        causal = c <= r
        for j in range(HEAD_PAIRS):
            wa = jnp.where(causal, wsp_ref[2 * j], 0.0)
            wb = jnp.where(causal, wsp_ref[2 * j + 1], 0.0)
            wcat_s[j] = jnp.concatenate([wa, wb], axis=1).astype(BF16)
        rr = lax.broadcasted_iota(I32, (TM, TM), 0)
        cc = lax.broadcasted_iota(I32, (TM, TM), 1)
        tri_s[...] = (rr < cc).astype(BF16)
        h2_s[...] = jnp.zeros_like(h2_s)
        win_ref[...] = win_f32[...].astype(BF16)
        wout_ref[...] = wout_f32[...].astype(BF16)
        wrt_ref[...] = wrt_f32[...].astype(BF16)

    @pl.when(i % tiles_per_seq == 0)
    def _seq_start():
        halo_s[...] = jnp.zeros_like(halo_s)

    def weights_in(e):
        return (pltpu.make_async_copy(wg_hbm.at[e], wg_f, wisem.at[0]),
                pltpu.make_async_copy(wu_hbm.at[e], wu_f, wisem.at[1]),
                pltpu.make_async_copy(wd_hbm.at[e], wd_f, wisem.at[2]))

    def weights_out(e):
        return (pltpu.make_async_copy(wg_b, wgb_hbm.at[e], wosem.at[0]),
                pltpu.make_async_copy(wu_b, wub_hbm.at[e], wosem.at[1]),
                pltpu.make_async_copy(wd_b, wdb_hbm.at[e], wosem.at[2]))

    @pl.when(i == 0)
    def _first_weights():
        for cp in weights_in(0):
            cp.start(priority=WEIGHT_DMA_PRIORITY)

    experts_per_step = -(-N_EXPERTS // (nt + 1))
    for k in range(experts_per_step):
        e = i * experts_per_step + k

        @pl.when(e < N_EXPERTS)
        def _cast_expert(e=e):
            for cp in weights_in(e):
                cp.wait()

            @pl.when(e > 0)
            def _previous_written():
                for cp in weights_out(e - 1):
                    cp.wait()

            wg_b[...] = wg_f[...].astype(BF16)
            wu_b[...] = wu_f[...].astype(BF16)
            wd_b[...] = wd_f[...].astype(BF16)
            for cp in weights_out(e):
                cp.start()

            @pl.when(e + 1 < N_EXPERTS)
            def _next_in():
                for cp in weights_in(e + 1):
                    cp.start(priority=WEIGHT_DMA_PRIORITY)

    @pl.when(i == nt)
    def _last_written():
        for cp in weights_out(N_EXPERTS - 1):
            cp.wait()

    def step(mix):
        h2b = h2_s[(i + 1) % 2]
        if mix:
            x = x_ref[...]
            hb = _rms(x, gmix_ref[...]).astype(BF16)

            def proj(k):
                return jnp.dot(hb, win_ref[:, k * 512:(k + 1) * 512], preferred_element_type=F32)

        logits = lax.dot_general(wrt_ref[...], h2b, (((1,), (1,)), ((), ())),
                                 preferred_element_type=F32)[0:ROUTE_ROWS, :] + brt_ref[0:ROUTE_ROWS, :]
        if mix:
            pu = proj(3)
            pv = proj(4)

        ridx = lax.broadcasted_iota(I32, (ROUTE_ROWS, TM), 0).astype(F32)
        neg = jnp.float32(-jnp.inf)
        big = jnp.float32(1e9)
        is_g = ridx < N_GROUPS
        gl = jnp.where(is_g, logits, neg)
        gmax = jnp.max(gl, axis=0, keepdims=True)
        gsum = jnp.sum(jnp.where(is_g, jnp.exp(gl - gmax), 0.0), axis=0, keepdims=True)
        g_w = 1.0 / gsum
        g_idx = jnp.min(jnp.where(gl == gmax, ridx, big), axis=0, keepdims=True)
        lo_row = ROUTE_LANE0 + EXPERTS_PER_GROUP * g_idx
        in_grp = (ridx >= lo_row) & (ridx < lo_row + EXPERTS_PER_GROUP)
        el = jnp.where(in_grp, logits, neg)
        v1 = jnp.max(el, axis=0, keepdims=True)
        i1 = jnp.min(jnp.where(el == v1, ridx, big), axis=0, keepdims=True)
        el2 = jnp.where(ridx == i1, neg, el)
        v2 = jnp.max(el2, axis=0, keepdims=True)
        i2 = jnp.min(jnp.where(el2 == v2, ridx, big), axis=0, keepdims=True)
        e21 = jnp.exp(v2 - v1)
        w1 = g_w / (1.0 + e21)
        w2 = g_w * e21 / (1.0 + e21)

        if mix:
            u = jax.nn.gelu(pu)
            zc = proj(1) * proj(2)
            v = jax.nn.gelu(pv)
            vc = v - jnp.mean(v, axis=-1, keepdims=True)
            vn = vc * lax.rsqrt(jnp.mean(vc * vc, axis=-1, keepdims=True) + EPS) * gsgu_ref[...]
            vnb = vn.astype(BF16)

            row = lax.broadcasted_iota(I32, (TM, CONV_WIDTH), 0)
            h6 = halo_s[6:7, :]
            h7 = halo_s[7:8, :]
            z1 = jnp.where(row == 0, h7, pltpu.roll(zc, 1, 0))
            z2 = jnp.where(row == 0, h6, jnp.where(row == 1, h7, pltpu.roll(zc, 2, 0)))
            conv = z2 * wconv_ref[0:1, :] + z1 * wconv_ref[1:2, :] + zc * wconv_ref[2:3, :]
            halo_s[...] = zc[TM - 8:TM, :]
            mix_s[:, 0:CONV_WIDTH] = (proj(0) * conv).astype(BF16)

        sel1 = ridx == i1
        sel2 = ridx == i2
        onehot = jnp.where(sel1 | sel2, 1.0, 0.0)
        counts = jnp.sum(onehot, axis=1, keepdims=True)
        pad_rows = jnp.zeros((LANES - ROUTE_ROWS, LANES), F32)
        counts_sq = jnp.concatenate([jnp.broadcast_to(counts, (ROUTE_ROWS, LANES)), pad_rows], axis=0)
        blocks_sq = jnp.ceil(counts_sq * (1.0 / BLOCK))
        er = lax.broadcasted_iota(I32, (LANES, LANES), 0)
        ec = lax.broadcasted_iota(I32, (LANES, LANES), 1)
        before = (ec < er).astype(BF16)
        run_start = BLOCK * jnp.dot(before, blocks_sq.astype(BF16),
                                    preferred_element_type=F32)[0:ROUTE_ROWS, 0:1]
        rank = jnp.dot(onehot.astype(BF16), tri_s[...], preferred_element_type=F32) + run_start
        pos1_row = jnp.sum(jnp.where(sel1, rank, 0.0), axis=0, keepdims=True)
        pos2_row = jnp.sum(jnp.where(sel2, rank, 0.0), axis=0, keepdims=True)
        n_s[pl.ds(jnp.where(i == 0, nt, i - 1), 1), :] = counts_sq.T[0:1, :]

        r8 = lax.broadcasted_iota(I32, (SUBLANES, TM), 0)
        rows8 = jnp.where(r8 == 0, i1 - ROUTE_LANE0, 0.0)
        rows8 = jnp.where(r8 == 1, i2 - ROUTE_LANE0, rows8)
        rows8 = jnp.where(r8 == 2, w1, rows8)
        rows8 = jnp.where(r8 == 3, w2, rows8)
        rows8 = jnp.where(r8 == 4, pos1_row, rows8)
        rows8 = jnp.where(r8 == 5, pos2_row, rows8)
        route_ref[...] = jnp.concatenate(
            [rows8, jnp.zeros((LANES - SUBLANES, TM), F32)], axis=0).T

        def sort_rows(lo, hi):
            out_row = (lax.broadcasted_iota(I32, (hi - lo, TM), 0) + lo).astype(F32)
            sort_mat = jnp.where((out_row == pos1_row) | (out_row == pos2_row), 1.0, 0.0).astype(BF16)
            stage_ref[lo:hi, :] = jnp.dot(sort_mat, h2b, preferred_element_type=F32).astype(BF16)

        sort_rows(0, SORT_MAIN // 2)
        if mix:
            left = lax.broadcasted_iota(I32, (CHUNK, LANES), 1) < (LANES // 2)
            zero = jnp.zeros((CHUNK, LANES), BF16)
            for j in range(HEAD_PAIRS):
                cols = []
                for c in range(nch):
                    s = vnb[c * CHUNK:(c + 1) * CHUNK, j * LANES:(j + 1) * LANES]
                    cols.append(jnp.concatenate([jnp.where(left, s, zero), jnp.where(left, zero, s)],
                                                axis=0))
                rhs = jnp.concatenate(cols, axis=1)
                res = jnp.dot(wcat_s[j], rhs, preferred_element_type=F32)
                bias = bsp_ref[:, j * LANES:(j + 1) * LANES]
                for c in range(nch):
                    mixed = res[:, c * LANES:(c + 1) * LANES] + bias
                    uu = u[c * CHUNK:(c + 1) * CHUNK, j * LANES:(j + 1) * LANES]
                    mix_s[c * CHUNK:(c + 1) * CHUNK,
                          CONV_WIDTH + j * LANES:CONV_WIDTH + (j + 1) * LANES] = (uu * mixed).astype(BF16)

        sort_rows(SORT_MAIN // 2, SORT_MAIN)
        if mix:
            x1 = x + jnp.dot(mix_s[...], wout_ref[...], preferred_element_type=F32)
            x1_ref[...] = x1
            h2_s[i % 2] = _rms(x1, gffn_ref[...]).astype(BF16)

        used_rows = BLOCK * jnp.sum(blocks_sq[:, 0:1])
        for lo in range(SORT_MAIN, KOUT, TAIL):
            @pl.when(used_rows > lo)
            def _tail(lo=lo):
                sort_rows(lo, lo + TAIL)

            @pl.when(used_rows <= lo)
            def _empty_tail(lo=lo):
                stage_ref[lo:lo + TAIL, :] = jnp.zeros((TAIL, D_MODEL), BF16)

    pl.when(i < nt)(functools.partial(step, True))

    @pl.when(i == nt)
    def _last():
        step(False)
        _build_tables(n_s, nt, meta_ref, src_ref, dst_ref, used_ref)


def _mixer_router(x2d, g_mix, w_in, w_conv, g_sgu, w_sp, b_sp, w_out, g_ffn, w_rt, b_rt,
                  w_gate, w_up, w_down, seq, n_tiles):
    t = x2d.shape[0]
    nt = t // TM
    tp = -(-n_tiles // SUBLANES) * SUBLANES
    whole = lambda r: pl.BlockSpec((r, LANES), lambda i: (0, 0))
    full = lambda a: pl.BlockSpec(a.shape, lambda i: (0,) * a.ndim, pipeline_mode=pl.Buffered(1))
    any_spec = pl.BlockSpec(memory_space=pl.ANY)
    ins = (g_mix, w_in, w_conv, g_sgu, w_sp, b_sp, w_out, g_ffn, w_rt, b_rt)
    expert_w = (w_gate, w_up, w_down)
    prev = lambda i: (jnp.where(i == 0, nt, i - 1), 0)
    return pl.pallas_call(
        functools.partial(_mixer_router_kernel, seq // TM, nt),
        grid=(nt + 1,),
        in_specs=[pl.BlockSpec((TM, D_MODEL), lambda i: (jnp.minimum(i, nt - 1), 0))]
        + [full(a) for a in ins] + [any_spec] * 3,
        out_specs=[pl.BlockSpec((TM, D_MODEL), lambda i: (jnp.minimum(i, nt - 1), 0)),
                   pl.BlockSpec((TM, LANES), prev),
                   pl.BlockSpec((KOUT, D_MODEL), prev),
                   whole(tp), whole(tp), whole(tp + SUBLANES), whole(nt)] + [any_spec] * 3,
        out_shape=[jax.ShapeDtypeStruct((t, D_MODEL), F32),
                   jax.ShapeDtypeStruct(((nt + 1) * TM, LANES), F32),
                   jax.ShapeDtypeStruct(((nt + 1) * KOUT, D_MODEL), BF16),
                   jax.ShapeDtypeStruct((tp, LANES), I32),
                   jax.ShapeDtypeStruct((tp, LANES), I32),
                   jax.ShapeDtypeStruct((tp + SUBLANES, LANES), I32),
                   jax.ShapeDtypeStruct((nt, LANES), I32)]
        + [jax.ShapeDtypeStruct(w.shape, BF16) for w in expert_w],
        scratch_shapes=[pltpu.VMEM((HEAD_PAIRS, CHUNK, 2 * CHUNK), BF16),
                        pltpu.VMEM((TM, TM), BF16),
                        pltpu.VMEM((8, CONV_WIDTH), F32),
                        pltpu.VMEM((TM, D_MODEL), BF16),
                        pltpu.VMEM((2, TM, D_MODEL), BF16),
                        pltpu.VMEM(w_in.shape, BF16),
                        pltpu.VMEM(w_out.shape, BF16),
                        pltpu.VMEM(w_rt.shape, BF16),
                        pltpu.VMEM((nt + SUBLANES, LANES), F32)]
        + [pltpu.VMEM(w.shape[1:], F32) for w in expert_w]
        + [pltpu.VMEM(w.shape[1:], BF16) for w in expert_w]
        + [pltpu.SemaphoreType.DMA((3,)), pltpu.SemaphoreType.DMA((3,))],
        compiler_params=pltpu.CompilerParams(dimension_semantics=("arbitrary",),
                                             vmem_limit_bytes=VMEM_LIMIT),
        name="mixer_router",
    )(x2d, *ins, *expert_w)


def _expert_kernel(meta_ref, src_ref, dst_ref, stage_ref,
                   wg_hbm, wu_hbm, wd_hbm, out_ref,
                   xbuf, obuf, wg_s, wu_s, wd_s, par_ref, sem, osem, wsem):
    i = pl.program_id(0)
    n_used = meta_ref[3]
    slot = i % N_SLOTS
    bpt = BLOCKS_PER_ROW_TILE
    dump_row = dst_ref.shape[0] // bpt - SUBLANES
    expert_of = lambda tile: meta_ref[tile * META_W]

    def gather_copy(seq, j):
        tile = jnp.minimum(seq, n_used - 1)
        src = pl.multiple_of(src_ref[tile * bpt + j] * BLOCK, BLOCK)
        slot_ = seq % N_SLOTS
        return pltpu.make_async_copy(stage_ref.at[pl.ds(src, BLOCK)],
                                     xbuf.at[slot_, pl.ds(j * BLOCK, BLOCK)], sem.at[slot_])

    def scatter_copy(tile, j):
        row = jnp.where(tile < 0, dump_row, tile)
        dst = pl.multiple_of(dst_ref[row * bpt + j] * BLOCK, BLOCK)
        slot_ = (tile + N_SLOTS) % N_SLOTS
        return pltpu.make_async_copy(obuf.at[slot_, pl.ds(j * BLOCK, BLOCK)],
                                     out_ref.at[pl.ds(dst, BLOCK)], osem.at[slot_])

    def each_block(fn):
        for j in range(BLOCKS_PER_ROW_TILE):
            fn(j)

    def weight_copies(e, p):
        return (pltpu.make_async_copy(wg_hbm.at[e], wg_s.at[p], wsem.at[0]),
                pltpu.make_async_copy(wu_hbm.at[e], wu_s.at[p], wsem.at[1]),
                pltpu.make_async_copy(wd_hbm.at[e], wd_s.at[p], wsem.at[2]))

    @pl.when(i == 0)
    def _first():
        each_block(lambda j: gather_copy(0, j).start())
        each_block(lambda j: gather_copy(1, j).start())
        par_ref[0] = 0
        for cp in weight_copies(expert_of(0), 0):
            cp.start(priority=WEIGHT_DMA_PRIORITY)
        obuf[...] = jnp.zeros_like(obuf)

    @pl.when((i >= 2) & (i - 3 < n_used))
    def _free_out_slot():
        each_block(lambda j: scatter_copy(i - 3, j).wait())

    @pl.when(i == n_used)
    def _after_last_tile():
        each_block(lambda j: gather_copy(n_used, j).wait())
        each_block(lambda j: gather_copy(n_used + 1, j).wait())
        each_block(lambda j: scatter_copy(n_used - 1, j).start())

    @pl.when(i < n_used)
    def _tile():
        e = expert_of(i)
        prev = expert_of(jnp.maximum(i - 1, 0))

        @pl.when((i == 0) | (e != prev))
        def _new_expert():
            p = jnp.where(i == 0, 0, 1 - par_ref[0])
            for cp in weight_copies(e, p):
                cp.wait()
            par_ref[0] = p
            nxt = meta_ref[i * META_W + 1]

            @pl.when(nxt >= 0)
            def _next_weights():
                for cp in weight_copies(nxt, 1 - p):
                    cp.start(priority=WEIGHT_DMA_PRIORITY)

        each_block(lambda j: gather_copy(i, j).wait())

        def mlp(rows):
            p = par_ref[0]
            hb = xbuf[slot, 0:rows, :]
            a = jnp.dot(hb, wg_s[p], preferred_element_type=F32)
            b = jnp.dot(hb, wu_s[p], preferred_element_type=F32)
            each_block(lambda j: gather_copy(i + 2, j).start())
            each_block(lambda j: scatter_copy(i - 1, j).start())
            hid = (a * jax.nn.sigmoid(a) * b).astype(BF16)
            obuf[slot, 0:rows, :] = jnp.dot(hid, wd_s[p], preferred_element_type=F32).astype(BF16)

        rows_needed = meta_ref[i * META_W + 2]
        for rows in range(ROW_STEP, TR + 1, ROW_STEP):
            pl.when(rows_needed == rows)(functools.partial(mlp, rows))


def _experts(meta, src_block, dst_block, stage, w_gate, w_up, w_down, n_tiles):
    any_spec = pl.BlockSpec(memory_space=pl.ANY)
    return pl.pallas_call(
        _expert_kernel,
        grid_spec=pltpu.PrefetchScalarGridSpec(
            num_scalar_prefetch=3,
            grid=(n_tiles,),
            in_specs=[any_spec, any_spec, any_spec, any_spec],
            out_specs=any_spec,
            scratch_shapes=[pltpu.VMEM((N_SLOTS, TR, D_MODEL), BF16),
                            pltpu.VMEM((N_SLOTS, TR, D_MODEL), BF16),
                            pltpu.VMEM((2, D_MODEL, D_EXPERT), BF16),
                            pltpu.VMEM((2, D_MODEL, D_EXPERT), BF16),
                            pltpu.VMEM((2, D_EXPERT, D_MODEL), BF16),
                            pltpu.SMEM((1,), I32),
                            pltpu.SemaphoreType.DMA((N_SLOTS,)),
                            pltpu.SemaphoreType.DMA((N_SLOTS,)),
                            pltpu.SemaphoreType.DMA((3,))]),
        out_shape=jax.ShapeDtypeStruct(stage.shape, stage.dtype),
        input_output_aliases={3: 0},
        compiler_params=pltpu.CompilerParams(dimension_semantics=("arbitrary",),
                                             vmem_limit_bytes=VMEM_LIMIT),
        name="experts",
    )(meta, src_block, dst_block, stage, w_gate, w_up, w_down)


def _combine_kernel(used_ref, x1_ref, route_ref, p_ref, gple_ref, wpg_f32, wpp_f32, gfin_ref,
                    srt_ref, out_ref, y_s, wpg_ref, wpp_ref):
    i = pl.program_id(0)

    @pl.when(i == 0)
    def _init():
        wpg_ref[...] = wpg_f32[...].astype(BF16)
        wpp_ref[...] = wpp_f32[...].astype(BF16)

    route = route_ref[...]
    w1, w2, pos1, pos2 = (route[:, k:k + 1] for k in (2, 3, 4, 5))

    def unsort(lo, hi):
        col = (lax.broadcasted_iota(I32, (TM, hi - lo), 1) + lo).astype(F32)
        mat = jnp.where(col == pos1, w1, jnp.where(col == pos2, w2, 0.0)).astype(BF16)
        return jnp.dot(mat, srt_ref[lo:hi, :], preferred_element_type=F32)

    def finish(y):
        pe = jnp.dot(p_ref[...].astype(BF16), wpp_ref[...], preferred_element_type=F32)
        hrows = TM // 2
        for h in range(2):
            rs = slice(h * hrows, (h + 1) * hrows)
            x2 = x1_ref[rs, :] + y[rs, :]
            hg = _rms(x2, gple_ref[...]).astype(BF16)
            gate = jax.nn.sigmoid(jnp.dot(hg, wpg_ref[...], preferred_element_type=F32))
            x3 = x2 + gate * pe[rs, :]
            out_ref[rs, :] = _rms(x3, gfin_ref[...])

    @pl.when(used_ref[i] <= SORT_MAIN)
    def _common():
        finish(unsort(0, SORT_MAIN))

    @pl.when(used_ref[i] > SORT_MAIN)
    def _with_tails():
        y_s[...] = unsort(0, SORT_MAIN)
        for lo in range(SORT_MAIN, KOUT, TAIL):
            @pl.when(used_ref[i] > lo)
            def _tail(lo=lo):
                y_s[...] += unsort(lo, lo + TAIL)
        finish(y_s[...])


def _combine(used, x1, route, p2d, g_ple, w_pg, w_pp, g_fin, sorted_out):
    t = p2d.shape[0]
    full = lambda a: pl.BlockSpec(a.shape, lambda i, *_: (0,) * a.ndim)
    return pl.pallas_call(
        _combine_kernel,
        grid_spec=pltpu.PrefetchScalarGridSpec(
            num_scalar_prefetch=1,
            grid=(t // TM,),
            in_specs=[pl.BlockSpec((TM, D_MODEL), lambda i, *_: (i, 0)),
                      pl.BlockSpec((TM, LANES), lambda i, *_: (i, 0)),
                      pl.BlockSpec((TM, PLE_DIM), lambda i, *_: (i, 0)),
                      full(g_ple), full(w_pg), full(w_pp), full(g_fin),
                      pl.BlockSpec((KOUT, D_MODEL), lambda i, *_: (i, 0))],
            out_specs=pl.BlockSpec((TM, D_MODEL), lambda i, *_: (i, 0)),
            scratch_shapes=[pltpu.VMEM((TM, D_MODEL), F32),
                            pltpu.VMEM(w_pg.shape, BF16),
                            pltpu.VMEM(w_pp.shape, BF16)]),
        out_shape=jax.ShapeDtypeStruct((t, D_MODEL), F32),
        compiler_params=pltpu.CompilerParams(dimension_semantics=("arbitrary",),
                                             vmem_limit_bytes=VMEM_LIMIT),
        name="combine_ple",
    )(used, x1, route, p2d, g_ple, w_pg, w_pp, g_fin, sorted_out)


def kernel(x, p, g_mix, w_in, w_conv, g_sgu, w_spatial, b_spatial, w_out, g_ffn, w_group, b_group,
           w_router, b_router, w_gate, w_up, w_down, g_ple, w_ple_gate, w_ple_proj, g_final):
    bsz, seq, d = x.shape
    t = bsz * seq
    assert w_in.shape[0] == 1, "single-layer block"
    assert d == D_MODEL and seq % TM == 0 and TM % CHUNK == 0
    nt = t // TM
    max_rows = TOP_K * t + nt * N_EXPERTS * (BLOCK - 1)
    n_tiles = (max_rows + N_EXPERTS * (TR - 1)) // TR + N_SLOTS
    l = 0

    w_rt = jnp.concatenate(
        [w_group[l], jnp.transpose(w_router[l], (1, 0, 2)).reshape(d, N_EXPERTS)], axis=1)
    w_rt = jnp.pad(w_rt, ((0, 0), (0, LANES - w_rt.shape[1]))).T
    b_rt = jnp.pad(jnp.concatenate([b_group[l], b_router[l].reshape(-1)]),
                   (0, LANES - N_GROUPS - N_EXPERTS)).reshape(LANES, 1)
    b_sp = jnp.repeat(b_spatial[l].T, SGU_WIDTH // SGU_HEADS, axis=1)

    x1, route, stage, meta, src, dst, used, wg_b, wu_b, wd_b = _mixer_router(
        x.reshape(t, d), g_mix[l].reshape(1, d), w_in[l], w_conv[l],
        g_sgu[l].reshape(1, -1), w_spatial[l], b_sp, w_out[l],
        g_ffn[l].reshape(1, d), w_rt, b_rt,
        w_gate[l].reshape(N_EXPERTS, d, D_EXPERT), w_up[l].reshape(N_EXPERTS, d, D_EXPERT),
        w_down[l].reshape(N_EXPERTS, D_EXPERT, d), seq, n_tiles)

    bpt = BLOCKS_PER_ROW_TILE
    meta, src, dst, used = (meta[:, :META_W].reshape(-1), src[:, :bpt].reshape(-1),
                            dst[:, :bpt].reshape(-1), used[:, 0])
    sorted_out = _experts(meta, src, dst, stage, wg_b, wu_b, wd_b, n_tiles)
    out = _combine(used, x1, route, p[l].reshape(t, PLE_DIM), g_ple[l].reshape(1, d),
                   w_ple_gate[l], w_ple_proj[l], g_final.reshape(1, d), sorted_out)
    return out.reshape(bsz, seq, d)
```

```python
import functools

import jax
import jax.numpy as jnp
from jax import lax
from jax.experimental import pallas as pl
from jax.experimental.pallas import tpu as pltpu

F32 = jnp.float32
BF16 = jnp.bfloat16
I32 = jnp.int32

EPS = 1e-6
D_MODEL = 1024
CONV_WIDTH = 512
SGU_WIDTH = 512
SGU_HEADS = 8
HEAD_PAIRS = SGU_HEADS // 2
CHUNK = 128
N_GROUPS = 4
EXPERTS_PER_GROUP = 8
N_EXPERTS = N_GROUPS * EXPERTS_PER_GROUP
TOP_K = 2
D_EXPERT = 512
PLE_DIM = 256
LANES = 128
SUBLANES = 8
BLOCK = 2 * SUBLANES
ROUTE_LANE0 = N_GROUPS
ROUTE_ROWS = 40

TM = 512
TR = 640
BLOCKS_PER_ROW_TILE = TR // BLOCK
KOUT = TOP_K * TM + N_EXPERTS * BLOCK
TAIL = 128
SORT_MAIN = KOUT - 2 * TAIL
META_W = 4
ROW_STEP = 128
N_SLOTS = 3
DUMP_SETS = 2
DUMP_BLOCKS = DUMP_SETS * BLOCKS_PER_ROW_TILE
assert DUMP_BLOCKS * BLOCK <= KOUT
WEIGHT_DMA_PRIORITY = 1
VMEM_LIMIT = 56 * 1024 * 1024


def _rms(x, g):
    return x * lax.rsqrt(jnp.mean(x * x, axis=-1, keepdims=True) + EPS) * g


def _build_tables(n_s, nt, meta_ref, src_ref, dst_ref, used_ref):
    tp = meta_ref.shape[0]
    bpt = BLOCKS_PER_ROW_TILE
    kb = KOUT // BLOCK
    n = n_s[0:nt, :]
    lane_n = lax.broadcasted_iota(I32, (nt, LANES), 1)
    is_e = (lane_n >= ROUTE_LANE0) & (lane_n < ROUTE_LANE0 + N_EXPERTS)
    nblk = jnp.where(is_e, jnp.floor((n + (BLOCK - 1)) * (1.0 / BLOCK)), 0.0)
    er = lax.broadcasted_iota(I32, (LANES, LANES), 0)
    ec = lax.broadcasted_iota(I32, (LANES, LANES), 1)
    loc = jnp.dot(nblk.astype(BF16), (er < ec).astype(BF16), preferred_element_type=F32)
    ti = lax.broadcasted_iota(I32, (nt, nt), 0)
    tj = lax.broadcasted_iota(I32, (nt, nt), 1)
    carry = jnp.dot((tj < ti).astype(BF16), nblk.astype(BF16), preferred_element_type=F32)
    total = jnp.sum(nblk, axis=0, keepdims=True)
    tiles_e = jnp.floor((total + (bpt - 1)) * (1.0 / bpt))
    tile_end = jnp.dot(jnp.broadcast_to(tiles_e, (SUBLANES, LANES)).astype(BF16),
                       (er <= ec).astype(BF16), preferred_element_type=F32)[0:1, :]
    tile0 = tile_end - tiles_e
    n_used = jnp.max(tile_end, axis=1, keepdims=True)

    tau = lax.broadcasted_iota(I32, (tp, LANES), 0).astype(F32)
    lane_i = lax.broadcasted_iota(I32, (tp, LANES), 1)
    lane_t = lane_i.astype(F32)
    is_e_t = (lane_i >= ROUTE_LANE0) & (lane_i < ROUTE_LANE0 + N_EXPERTS)
    te = jnp.sum(jnp.where(is_e_t & (tile_end <= jnp.minimum(tau, n_used - 1.0)), 1.0, 0.0),
                 axis=1, keepdims=True)
    oh = lane_t == te + ROUTE_LANE0
    pick = lambda row: jnp.sum(jnp.where(oh, row, 0.0), axis=1, keepdims=True)
    seg0_t = pick(tile0) * bpt
    e_lane = lane_t - ROUTE_LANE0
    cand = jnp.where(is_e_t & (e_lane > te) & (tiles_e > 0.0), e_lane, 99.0)
    nxt = jnp.min(cand, axis=1, keepdims=True)
    nxt = jnp.where(nxt == 99.0, -1.0, nxt)
    rows_t = jnp.clip(pick(total) * BLOCK - (tau[:, 0:1] - pick(tile0)) * TR, ROW_STEP, TR)
    rows_t = jnp.ceil(rows_t * (1.0 / ROW_STEP)) * ROW_STEP
    meta = jnp.where(lane_i == 0, te, 0.0)
    meta = jnp.where(lane_i == 1, nxt, meta)
    meta = jnp.where(lane_i == 2, rows_t, meta)
    meta = jnp.where(lane_i == 3, n_used, meta)
    meta_ref[...] = meta.astype(I32)

    bidx = tau * bpt + lane_t
    off = bidx - seg0_t
    ohb = jnp.where(oh, 1.0, 0.0).astype(BF16)

    def per_tile(tab):
        hi = jnp.floor(tab * (1.0 / 32))
        lo = tab - 32.0 * hi
        nt_dot = lambda a: lax.dot_general(ohb, a.astype(BF16), (((1,), (1,)), ((), ())),
                                           preferred_element_type=F32)
        return 32.0 * nt_dot(hi) + nt_dot(lo)

    run_end = per_tile(carry + nblk)
    run_beg = per_tile(carry)
    run_loc = per_tile(loc)
    tile_of = jnp.zeros((tp, LANES), F32)
    for it in range(nt):
        tile_of = tile_of + jnp.where(run_end[:, it:it + 1] <= off, 1.0, 0.0)
    srcv = off
    for it in range(nt):
        srcv = srcv + jnp.where(tile_of == it,
                                run_loc[:, it:it + 1] - run_beg[:, it:it + 1] + it * kb, 0.0)
    valid = (tile_of < nt) & (bidx < n_used * bpt)
    dump_set = tau - DUMP_SETS * jnp.floor((tau + 0.5) * (1.0 / DUMP_SETS))
    dump = nt * kb + dump_set * bpt + lane_t
    src_ref[...] = jnp.where(valid, srcv, kb - 1.0).astype(I32)
    dst_ref[0:tp, :] = jnp.where(valid, srcv, dump).astype(I32)
    dst_ref[tp:tp + SUBLANES, :] = (nt * kb + DUMP_BLOCKS - bpt
                                    + lax.broadcasted_iota(I32, (SUBLANES, LANES), 1))
    used_ref[...] = jnp.broadcast_to(jnp.sum(nblk, axis=1, keepdims=True) * BLOCK,
                                     (nt, LANES)).astype(I32)


def _mixer_router_kernel(tiles_per_seq, nt,
                         x_ref, gmix_ref, win_f32, wconv_ref, gsgu_ref, wsp_ref, bsp_ref,
                         wout_f32, gffn_ref, wrt_f32, brt_ref, wg_hbm, wu_hbm, wd_hbm,
                         x1_ref, route_ref, stage_ref, meta_ref, src_ref, dst_ref, used_ref,
                         wgb_hbm, wub_hbm, wdb_hbm,
                         wcat_s, tri_s, halo_s, mix_s, h2_s, win_ref, wout_ref, wrt_ref, n_s,
                         wg_f, wu_f, wd_f, wg_b, wu_b, wd_b, wisem, wosem):
    i = pl.program_id(0)
    nch = TM // CHUNK

    @pl.when(i == 0)
    def _init():
        r = lax.broadcasted_iota(I32, (CHUNK, CHUNK), 0)
        c = lax.broadcasted_iota(I32, (CHUNK, CHUNK), 1)
        causal = c <= r
        for j in range(HEAD_PAIRS):
            wa = jnp.where(causal, wsp_ref[2 * j], 0.0)
            wb = jnp.where(causal, wsp_ref[2 * j + 1], 0.0)
            wcat_s[j] = jnp.concatenate([wa, wb], axis=1).astype(BF16)
        rr = lax.broadcasted_iota(I32, (TM, TM), 0)
        cc = lax.broadcasted_iota(I32, (TM, TM), 1)
        tri_s[...] = (rr < cc).astype(BF16)
        h2_s[...] = jnp.zeros_like(h2_s)
        win_ref[...] = win_f32[...].astype(BF16)
        wout_ref[...] = wout_f32[...].astype(BF16)
        wrt_ref[...] = wrt_f32[...].astype(BF16)

    @pl.when(i % tiles_per_seq == 0)
    def _seq_start():
        halo_s[...] = jnp.zeros_like(halo_s)

    def weights_in(e):
        return (pltpu.make_async_copy(wg_hbm.at[e], wg_f, wisem.at[0]),
                pltpu.make_async_copy(wu_hbm.at[e], wu_f, wisem.at[1]),
                pltpu.make_async_copy(wd_hbm.at[e], wd_f, wisem.at[2]))

    def weights_out(e):
        return (pltpu.make_async_copy(wg_b, wgb_hbm.at[e], wosem.at[0]),
                pltpu.make_async_copy(wu_b, wub_hbm.at[e], wosem.at[1]),
                pltpu.make_async_copy(wd_b, wdb_hbm.at[e], wosem.at[2]))

    @pl.when(i == 0)
    def _first_weights():
        for cp in weights_in(0):
            cp.start(priority=WEIGHT_DMA_PRIORITY)

    experts_per_step = -(-N_EXPERTS // (nt + 1))
    for k in range(experts_per_step):
        e = i * experts_per_step + k

        @pl.when(e < N_EXPERTS)
        def _cast_expert(e=e):
            for cp in weights_in(e):
                cp.wait()

            @pl.when(e > 0)
            def _previous_written():
                for cp in weights_out(e - 1):
                    cp.wait()

            wg_b[...] = wg_f[...].astype(BF16)
            wu_b[...] = wu_f[...].astype(BF16)
            wd_b[...] = wd_f[...].astype(BF16)
            for cp in weights_out(e):
                cp.start()

            @pl.when(e + 1 < N_EXPERTS)
            def _next_in():
                for cp in weights_in(e + 1):
                    cp.start(priority=WEIGHT_DMA_PRIORITY)

    @pl.when(i == nt)
    def _last_written():
        for cp in weights_out(N_EXPERTS - 1):
            cp.wait()

    def step(mix):
        h2b = h2_s[(i + 1) % 2]
        if mix:
            x = x_ref[...]
            hb = _rms(x, gmix_ref[...]).astype(BF16)

            def proj(k):
                return jnp.dot(hb, win_ref[:, k * 512:(k + 1) * 512], preferred_element_type=F32)

        logits = lax.dot_general(wrt_ref[...], h2b, (((1,), (1,)), ((), ())),
                                 preferred_element_type=F32)[0:ROUTE_ROWS, :] + brt_ref[0:ROUTE_ROWS, :]
        if mix:
            pu = proj(3)
            pv = proj(4)

        ridx = lax.broadcasted_iota(I32, (ROUTE_ROWS, TM), 0).astype(F32)
        neg = jnp.float32(-jnp.inf)
        big = jnp.float32(1e9)
        is_g = ridx < N_GROUPS
        gl = jnp.where(is_g, logits, neg)
        gmax = jnp.max(gl, axis=0, keepdims=True)
        gsum = jnp.sum(jnp.where(is_g, jnp.exp(gl - gmax), 0.0), axis=0, keepdims=True)
        g_w = 1.0 / gsum
        g_idx = jnp.min(jnp.where(gl == gmax, ridx, big), axis=0, keepdims=True)
        lo_row = ROUTE_LANE0 + EXPERTS_PER_GROUP * g_idx
        in_grp = (ridx >= lo_row) & (ridx < lo_row + EXPERTS_PER_GROUP)
        el = jnp.where(in_grp, logits, neg)
        v1 = jnp.max(el, axis=0, keepdims=True)
        i1 = jnp.min(jnp.where(el == v1, ridx, big), axis=0, keepdims=True)
        el2 = jnp.where(ridx == i1, neg, el)
        v2 = jnp.max(el2, axis=0, keepdims=True)
        i2 = jnp.min(jnp.where(el2 == v2, ridx, big), axis=0, keepdims=True)
        e21 = jnp.exp(v2 - v1)
        w1 = g_w / (1.0 + e21)
        w2 = g_w * e21 / (1.0 + e21)

        if mix:
            u = jax.nn.gelu(pu)
            zc = proj(1) * proj(2)
            v = jax.nn.gelu(pv)
            vc = v - jnp.mean(v, axis=-1, keepdims=True)
            vn = vc * lax.rsqrt(jnp.mean(vc * vc, axis=-1, keepdims=True) + EPS) * gsgu_ref[...]
            vnb = vn.astype(BF16)

            row = lax.broadcasted_iota(I32, (TM, CONV_WIDTH), 0)
            h6 = halo_s[6:7, :]
            h7 = halo_s[7:8, :]
            z1 = jnp.where(row == 0, h7, pltpu.roll(zc, 1, 0))
            z2 = jnp.where(row == 0, h6, jnp.where(row == 1, h7, pltpu.roll(zc, 2, 0)))
            conv = z2 * wconv_ref[0:1, :] + z1 * wconv_ref[1:2, :] + zc * wconv_ref[2:3, :]
            halo_s[...] = zc[TM - 8:TM, :]
            mix_s[:, 0:CONV_WIDTH] = (proj(0) * conv).astype(BF16)

        sel1 = ridx == i1
        sel2 = ridx == i2
        onehot = jnp.where(sel1 | sel2, 1.0, 0.0)
        counts = jnp.sum(onehot, axis=1, keepdims=True)
        pad_rows = jnp.zeros((LANES - ROUTE_ROWS, LANES), F32)
        counts_sq = jnp.concatenate([jnp.broadcast_to(counts, (ROUTE_ROWS, LANES)), pad_rows], axis=0)
        blocks_sq = jnp.ceil(counts_sq * (1.0 / BLOCK))
        er = lax.broadcasted_iota(I32, (LANES, LANES), 0)
        ec = lax.broadcasted_iota(I32, (LANES, LANES), 1)
        before = (ec < er).astype(BF16)
        run_start = BLOCK * jnp.dot(before, blocks_sq.astype(BF16),
                                    preferred_element_type=F32)[0:ROUTE_ROWS, 0:1]
        rank = jnp.dot(onehot.astype(BF16), tri_s[...], preferred_element_type=F32) + run_start
        pos1_row = jnp.sum(jnp.where(sel1, rank, 0.0), axis=0, keepdims=True)
        pos2_row = jnp.sum(jnp.where(sel2, rank, 0.0), axis=0, keepdims=True)
        n_s[pl.ds(jnp.where(i == 0, nt, i - 1), 1), :] = counts_sq.T[0:1, :]

        r8 = lax.broadcasted_iota(I32, (SUBLANES, TM), 0)
        rows8 = jnp.where(r8 == 0, i1 - ROUTE_LANE0, 0.0)
        rows8 = jnp.where(r8 == 1, i2 - ROUTE_LANE0, rows8)
        rows8 = jnp.where(r8 == 2, w1, rows8)
        rows8 = jnp.where(r8 == 3, w2, rows8)
        rows8 = jnp.where(r8 == 4, pos1_row, rows8)
        rows8 = jnp.where(r8 == 5, pos2_row, rows8)
        route_ref[...] = jnp.concatenate(
            [rows8, jnp.zeros((LANES - SUBLANES, TM), F32)], axis=0).T

        def sort_rows(lo, hi):
            out_row = (lax.broadcasted_iota(I32, (hi - lo, TM), 0) + lo).astype(F32)
            sort_mat = jnp.where((out_row == pos1_row) | (out_row == pos2_row), 1.0, 0.0).astype(BF16)
            stage_ref[lo:hi, :] = jnp.dot(sort_mat, h2b, preferred_element_type=F32).astype(BF16)

        sort_rows(0, SORT_MAIN // 2)
        if mix:
            left = lax.broadcasted_iota(I32, (CHUNK, LANES), 1) < (LANES // 2)
            zero = jnp.zeros((CHUNK, LANES), BF16)
            for j in range(HEAD_PAIRS):
                cols = []
                for c in range(nch):
                    s = vnb[c * CHUNK:(c + 1) * CHUNK, j * LANES:(j + 1) * LANES]
                    cols.append(jnp.concatenate([jnp.where(left, s, zero), jnp.where(left, zero, s)],
                                                axis=0))
                rhs = jnp.concatenate(cols, axis=1)
                res = jnp.dot(wcat_s[j], rhs, preferred_element_type=F32)
                bias = bsp_ref[:, j * LANES:(j + 1) * LANES]
                for c in range(nch):
                    mixed = res[:, c * LANES:(c + 1) * LANES] + bias
                    uu = u[c * CHUNK:(c + 1) * CHUNK, j * LANES:(j + 1) * LANES]
                    mix_s[c * CHUNK:(c + 1) * CHUNK,
                          CONV_WIDTH + j * LANES:CONV_WIDTH + (j + 1) * LANES] = (uu * mixed).astype(BF16)

        sort_rows(SORT_MAIN // 2, SORT_MAIN)
        if mix:
            x1 = x + jnp.dot(mix_s[...], wout_ref[...], preferred_element_type=F32)
            x1_ref[...] = x1
            h2_s[i % 2] = _rms(x1, gffn_ref[...]).astype(BF16)

        used_rows = BLOCK * jnp.sum(blocks_sq[:, 0:1])
        for lo in range(SORT_MAIN, KOUT, TAIL):
            @pl.when(used_rows > lo)
            def _tail(lo=lo):
                sort_rows(lo, lo + TAIL)

            @pl.when(used_rows <= lo)
            def _empty_tail(lo=lo):
                stage_ref[lo:lo + TAIL, :] = jnp.zeros((TAIL, D_MODEL), BF16)

    pl.when(i < nt)(functools.partial(step, True))

    @pl.when(i == nt)
    def _last():
        step(False)
        _build_tables(n_s, nt, meta_ref, src_ref, dst_ref, used_ref)


def _mixer_router(x2d, g_mix, w_in, w_conv, g_sgu, w_sp, b_sp, w_out, g_ffn, w_rt, b_rt,
                  w_gate, w_up, w_down, seq, n_tiles):
    t = x2d.shape[0]
    nt = t // TM
    tp = -(-n_tiles // SUBLANES) * SUBLANES
    whole = lambda r: pl.BlockSpec((r, LANES), lambda i: (0, 0))
    full = lambda a: pl.BlockSpec(a.shape, lambda i: (0,) * a.ndim, pipeline_mode=pl.Buffered(1))
    any_spec = pl.BlockSpec(memory_space=pl.ANY)
    ins = (g_mix, w_in, w_conv, g_sgu, w_sp, b_sp, w_out, g_ffn, w_rt, b_rt)
    expert_w = (w_gate, w_up, w_down)
    prev = lambda i: (jnp.where(i == 0, nt, i - 1), 0)
    return pl.pallas_call(
        functools.partial(_mixer_router_kernel, seq // TM, nt),
        grid=(nt + 1,),
        in_specs=[pl.BlockSpec((TM, D_MODEL), lambda i: (jnp.minimum(i, nt - 1), 0))]
        + [full(a) for a in ins] + [any_spec] * 3,
        out_specs=[pl.BlockSpec((TM, D_MODEL), lambda i: (jnp.minimum(i, nt - 1), 0)),
                   pl.BlockSpec((TM, LANES), prev),
                   pl.BlockSpec((KOUT, D_MODEL), prev),
                   whole(tp), whole(tp), whole(tp + SUBLANES), whole(nt)] + [any_spec] * 3,
        out_shape=[jax.ShapeDtypeStruct((t, D_MODEL), F32),
                   jax.ShapeDtypeStruct(((nt + 1) * TM, LANES), F32),
                   jax.ShapeDtypeStruct(((nt + 1) * KOUT, D_MODEL), BF16),
                   jax.ShapeDtypeStruct((tp, LANES), I32),
                   jax.ShapeDtypeStruct((tp, LANES), I32),
                   jax.ShapeDtypeStruct((tp + SUBLANES, LANES), I32),
                   jax.ShapeDtypeStruct((nt, LANES), I32)]
        + [jax.ShapeDtypeStruct(w.shape, BF16) for w in expert_w],
        scratch_shapes=[pltpu.VMEM((HEAD_PAIRS, CHUNK, 2 * CHUNK), BF16),
                        pltpu.VMEM((TM, TM), BF16),
                        pltpu.VMEM((8, CONV_WIDTH), F32),
                        pltpu.VMEM((TM, D_MODEL), BF16),
                        pltpu.VMEM((2, TM, D_MODEL), BF16),
                        pltpu.VMEM(w_in.shape, BF16),
                        pltpu.VMEM(w_out.shape, BF16),
                        pltpu.VMEM(w_rt.shape, BF16),
                        pltpu.VMEM((nt + SUBLANES, LANES), F32)]
        + [pltpu.VMEM(w.shape[1:], F32) for w in expert_w]
        + [pltpu.VMEM(w.shape[1:], BF16) for w in expert_w]
        + [pltpu.SemaphoreType.DMA((3,)), pltpu.SemaphoreType.DMA((3,))],
        compiler_params=pltpu.CompilerParams(dimension_semantics=("arbitrary",),
                                             vmem_limit_bytes=VMEM_LIMIT),
        name="mixer_router",
    )(x2d, *ins, *expert_w)


def _expert_kernel(meta_ref, src_ref, dst_ref, stage_ref,
                   wg_hbm, wu_hbm, wd_hbm, out_ref,
                   xbuf, obuf, wg_s, wu_s, wd_s, par_ref, sem, osem, wsem):
    i = pl.program_id(0)
    n_used = meta_ref[3]
    slot = i % N_SLOTS
    bpt = BLOCKS_PER_ROW_TILE
    dump_row = dst_ref.shape[0] // bpt - SUBLANES
    expert_of = lambda tile: meta_ref[tile * META_W]

    def gather_copy(seq, j):
        tile = jnp.minimum(seq, n_used - 1)
        src = pl.multiple_of(src_ref[tile * bpt + j] * BLOCK, BLOCK)
        slot_ = seq % N_SLOTS
        return pltpu.make_async_copy(stage_ref.at[pl.ds(src, BLOCK)],
                                     xbuf.at[slot_, pl.ds(j * BLOCK, BLOCK)], sem.at[slot_])

    def scatter_copy(tile, j):
        row = jnp.where(tile < 0, dump_row, tile)
        dst = pl.multiple_of(dst_ref[row * bpt + j] * BLOCK, BLOCK)
        slot_ = (tile + N_SLOTS) % N_SLOTS
        return pltpu.make_async_copy(obuf.at[slot_, pl.ds(j * BLOCK, BLOCK)],
                                     out_ref.at[pl.ds(dst, BLOCK)], osem.at[slot_])

    def each_block(fn):
        for j in range(BLOCKS_PER_ROW_TILE):
            fn(j)

    def weight_copies(e, p):
        return (pltpu.make_async_copy(wg_hbm.at[e], wg_s.at[p], wsem.at[0]),
                pltpu.make_async_copy(wu_hbm.at[e], wu_s.at[p], wsem.at[1]),
                pltpu.make_async_copy(wd_hbm.at[e], wd_s.at[p], wsem.at[2]))

    @pl.when(i == 0)
    def _first():
        each_block(lambda j: gather_copy(0, j).start())
        each_block(lambda j: gather_copy(1, j).start())
        par_ref[0] = 0
        for cp in weight_copies(expert_of(0), 0):
            cp.start(priority=WEIGHT_DMA_PRIORITY)
        obuf[...] = jnp.zeros_like(obuf)

    @pl.when((i >= 2) & (i - 3 < n_used))
    def _free_out_slot():
        each_block(lambda j: scatter_copy(i - 3, j).wait())

    @pl.when(i == n_used)
    def _after_last_tile():
        each_block(lambda j: gather_copy(n_used, j).wait())
        each_block(lambda j: gather_copy(n_used + 1, j).wait())
        each_block(lambda j: scatter_copy(n_used - 1, j).start())

    @pl.when(i < n_used)
    def _tile():
        e = expert_of(i)
        prev = expert_of(jnp.maximum(i - 1, 0))

        @pl.when((i == 0) | (e != prev))
        def _new_expert():
            p = jnp.where(i == 0, 0, 1 - par_ref[0])
            for cp in weight_copies(e, p):
                cp.wait()
            par_ref[0] = p
            nxt = meta_ref[i * META_W + 1]

            @pl.when(nxt >= 0)
            def _next_weights():
                for cp in weight_copies(nxt, 1 - p):
                    cp.start(priority=WEIGHT_DMA_PRIORITY)

        each_block(lambda j: gather_copy(i, j).wait())

        def mlp(rows):
            p = par_ref[0]
            hb = xbuf[slot, 0:rows, :]
            a = jnp.dot(hb, wg_s[p], preferred_element_type=F32)
            b = jnp.dot(hb, wu_s[p], preferred_element_type=F32)
            each_block(lambda j: gather_copy(i + 2, j).start())
            each_block(lambda j: scatter_copy(i - 1, j).start())
            hid = (a * jax.nn.sigmoid(a) * b).astype(BF16)
            obuf[slot, 0:rows, :] = jnp.dot(hid, wd_s[p], preferred_element_type=F32).astype(BF16)

        rows_needed = meta_ref[i * META_W + 2]
        for rows in range(ROW_STEP, TR + 1, ROW_STEP):
            pl.when(rows_needed == rows)(functools.partial(mlp, rows))


def _experts(meta, src_block, dst_block, stage, w_gate, w_up, w_down, n_tiles):
    any_spec = pl.BlockSpec(memory_space=pl.ANY)
    return pl.pallas_call(
        _expert_kernel,
        grid_spec=pltpu.PrefetchScalarGridSpec(
            num_scalar_prefetch=3,
            grid=(n_tiles,),
            in_specs=[any_spec, any_spec, any_spec, any_spec],
            out_specs=any_spec,
            scratch_shapes=[pltpu.VMEM((N_SLOTS, TR, D_MODEL), BF16),
                            pltpu.VMEM((N_SLOTS, TR, D_MODEL), BF16),
                            pltpu.VMEM((2, D_MODEL, D_EXPERT), BF16),
                            pltpu.VMEM((2, D_MODEL, D_EXPERT), BF16),
                            pltpu.VMEM((2, D_EXPERT, D_MODEL), BF16),
                            pltpu.SMEM((1,), I32),
                            pltpu.SemaphoreType.DMA((N_SLOTS,)),
                            pltpu.SemaphoreType.DMA((N_SLOTS,)),
                            pltpu.SemaphoreType.DMA((3,))]),
        out_shape=jax.ShapeDtypeStruct(stage.shape, stage.dtype),
        input_output_aliases={3: 0},
        compiler_params=pltpu.CompilerParams(dimension_semantics=("arbitrary",),
                                             vmem_limit_bytes=VMEM_LIMIT),
        name="experts",
    )(meta, src_block, dst_block, stage, w_gate, w_up, w_down)


def _combine_kernel(used_ref, x1_ref, route_ref, p_ref, gple_ref, wpg_f32, wpp_f32, gfin_ref,
                    srt_ref, out_ref, y_s, wpg_ref, wpp_ref):
    i = pl.program_id(0)

    @pl.when(i == 0)
    def _init():
        wpg_ref[...] = wpg_f32[...].astype(BF16)
        wpp_ref[...] = wpp_f32[...].astype(BF16)

    route = route_ref[...]
    w1, w2, pos1, pos2 = (route[:, k:k + 1] for k in (2, 3, 4, 5))

    def unsort(lo, hi):
        col = (lax.broadcasted_iota(I32, (TM, hi - lo), 1) + lo).astype(F32)
        mat = jnp.where(col == pos1, w1, jnp.where(col == pos2, w2, 0.0)).astype(BF16)
        return jnp.dot(mat, srt_ref[lo:hi, :], preferred_element_type=F32)

    def finish(y):
        pe = jnp.dot(p_ref[...].astype(BF16), wpp_ref[...], preferred_element_type=F32)
        hrows = TM // 2
        for h in range(2):
            rs = slice(h * hrows, (h + 1) * hrows)
            x2 = x1_ref[rs, :] + y[rs, :]
            hg = _rms(x2, gple_ref[...]).astype(BF16)
            gate = jax.nn.sigmoid(jnp.dot(hg, wpg_ref[...], preferred_element_type=F32))
            x3 = x2 + gate * pe[rs, :]
            out_ref[rs, :] = _rms(x3, gfin_ref[...])

    @pl.when(used_ref[i] <= SORT_MAIN)
    def _common():
        finish(unsort(0, SORT_MAIN))

    @pl.when(used_ref[i] > SORT_MAIN)
    def _with_tails():
        y_s[...] = unsort(0, SORT_MAIN)
        for lo in range(SORT_MAIN, KOUT, TAIL):
            @pl.when(used_ref[i] > lo)
            def _tail(lo=lo):
                y_s[...] += unsort(lo, lo + TAIL)
        finish(y_s[...])


def _combine(used, x1, route, p2d, g_ple, w_pg, w_pp, g_fin, sorted_out):
    t = p2d.shape[0]
    full = lambda a: pl.BlockSpec(a.shape, lambda i, *_: (0,) * a.ndim)
    return pl.pallas_call(
        _combine_kernel,
        grid_spec=pltpu.PrefetchScalarGridSpec(
            num_scalar_prefetch=1,
            grid=(t // TM,),
            in_specs=[pl.BlockSpec((TM, D_MODEL), lambda i, *_: (i, 0)),
                      pl.BlockSpec((TM, LANES), lambda i, *_: (i, 0)),
                      pl.BlockSpec((TM, PLE_DIM), lambda i, *_: (i, 0)),
                      full(g_ple), full(w_pg), full(w_pp), full(g_fin),
                      pl.BlockSpec((KOUT, D_MODEL), lambda i, *_: (i, 0))],
            out_specs=pl.BlockSpec((TM, D_MODEL), lambda i, *_: (i, 0)),
            scratch_shapes=[pltpu.VMEM((TM, D_MODEL), F32),
                            pltpu.VMEM(w_pg.shape, BF16),
                            pltpu.VMEM(w_pp.shape, BF16)]),
        out_shape=jax.ShapeDtypeStruct((t, D_MODEL), F32),
        compiler_params=pltpu.CompilerParams(dimension_semantics=("arbitrary",),
                                             vmem_limit_bytes=VMEM_LIMIT),
        name="combine_ple",
    )(used, x1, route, p2d, g_ple, w_pg, w_pp, g_fin, sorted_out)


def kernel(x, p, g_mix, w_in, w_conv, g_sgu, w_spatial, b_spatial, w_out, g_ffn, w_group, b_group,
           w_router, b_router, w_gate, w_up, w_down, g_ple, w_ple_gate, w_ple_proj, g_final):
    bsz, seq, d = x.shape
    t = bsz * seq
    assert w_in.shape[0] == 1, "single-layer block"
    assert d == D_MODEL and seq % TM == 0 and TM % CHUNK == 0
    nt = t // TM
    max_rows = TOP_K * t + nt * N_EXPERTS * (BLOCK - 1)
    n_tiles = (max_rows + N_EXPERTS * (TR - 1)) // TR + N_SLOTS
    l = 0

    w_rt = jnp.concatenate(
        [w_group[l], jnp.transpose(w_router[l], (1, 0, 2)).reshape(d, N_EXPERTS)], axis=1)
    w_rt = jnp.pad(w_rt, ((0, 0), (0, LANES - w_rt.shape[1]))).T
    b_rt = jnp.pad(jnp.concatenate([b_group[l], b_router[l].reshape(-1)]),
                   (0, LANES - N_GROUPS - N_EXPERTS)).reshape(LANES, 1)
    b_sp = jnp.repeat(b_spatial[l].T, SGU_WIDTH // SGU_HEADS, axis=1)

    x1, route, stage, meta, src, dst, used, wg_b, wu_b, wd_b = _mixer_router(
        x.reshape(t, d), g_mix[l].reshape(1, d), w_in[l], w_conv[l],
        g_sgu[l].reshape(1, -1), w_spatial[l], b_sp, w_out[l],
        g_ffn[l].reshape(1, d), w_rt, b_rt,
        w_gate[l].reshape(N_EXPERTS, d, D_EXPERT), w_up[l].reshape(N_EXPERTS, d, D_EXPERT),
        w_down[l].reshape(N_EXPERTS, D_EXPERT, d), seq, n_tiles)

    bpt = BLOCKS_PER_ROW_TILE
    meta, src, dst, used = (meta[:, :META_W].reshape(-1), src[:, :bpt].reshape(-1),
                            dst[:, :bpt].reshape(-1), used[:, 0])
    sorted_out = _experts(meta, src, dst, stage, wg_b, wu_b, wd_b, n_tiles)
    out = _combine(used, x1, route, p[l].reshape(t, PLE_DIM), g_ple[l].reshape(1, d),
                   w_ple_gate[l], w_ple_proj[l], g_final.reshape(1, d), sorted_out)
    return out.reshape(bsz, seq, d)
```

```python
import functools

import jax
import jax.numpy as jnp
from jax import lax
from jax.experimental import pallas as pl
from jax.experimental.pallas import tpu as pltpu

F32 = jnp.float32
BF16 = jnp.bfloat16
I32 = jnp.int32

EPS = 1e-6
D_MODEL = 1024
CONV_WIDTH = 512
SGU_WIDTH = 512
SGU_HEADS = 8
HEAD_PAIRS = SGU_HEADS // 2
CHUNK = 128
N_GROUPS = 4
EXPERTS_PER_GROUP = 8
N_EXPERTS = N_GROUPS * EXPERTS_PER_GROUP
TOP_K = 2
D_EXPERT = 512
PLE_DIM = 256
LANES = 128
SUBLANES = 8
BLOCK = 2 * SUBLANES
ROUTE_LANE0 = N_GROUPS
ROUTE_ROWS = 40

TM = 512
TR = 768
BLOCKS_PER_ROW_TILE = TR // BLOCK
KOUT = TOP_K * TM + N_EXPERTS * BLOCK
TAIL = 128
SORT_MAIN = KOUT - 2 * TAIL
META_W = 4
ROW_STEP = 128
N_SLOTS = 3
DUMP_SETS = 2
DUMP_BLOCKS = DUMP_SETS * BLOCKS_PER_ROW_TILE
assert DUMP_BLOCKS * BLOCK <= KOUT
WEIGHT_DMA_PRIORITY = 1
VMEM_LIMIT = 56 * 1024 * 1024


def _rms(x, g):
    return x * lax.rsqrt(jnp.mean(x * x, axis=-1, keepdims=True) + EPS) * g


def _build_tables(n_s, nt, meta_ref, src_ref, dst_ref, used_ref):
    tp = meta_ref.shape[0]
    bpt = BLOCKS_PER_ROW_TILE
    kb = KOUT // BLOCK
    n = n_s[0:nt, :]
    lane_n = lax.broadcasted_iota(I32, (nt, LANES), 1)
    is_e = (lane_n >= ROUTE_LANE0) & (lane_n < ROUTE_LANE0 + N_EXPERTS)
    nblk = jnp.where(is_e, jnp.floor((n + (BLOCK - 1)) * (1.0 / BLOCK)), 0.0)
    er = lax.broadcasted_iota(I32, (LANES, LANES), 0)
    ec = lax.broadcasted_iota(I32, (LANES, LANES), 1)
    loc = jnp.dot(nblk.astype(BF16), (er < ec).astype(BF16), preferred_element_type=F32)
    ti = lax.broadcasted_iota(I32, (nt, nt), 0)
    tj = lax.broadcasted_iota(I32, (nt, nt), 1)
    carry = jnp.dot((tj < ti).astype(BF16), nblk.astype(BF16), preferred_element_type=F32)
    total = jnp.sum(nblk, axis=0, keepdims=True)
    tiles_e = jnp.floor((total + (bpt - 1)) * (1.0 / bpt))
    tile_end = jnp.dot(jnp.broadcast_to(tiles_e, (SUBLANES, LANES)).astype(BF16),
                       (er <= ec).astype(BF16), preferred_element_type=F32)[0:1, :]
    tile0 = tile_end - tiles_e
    n_used = jnp.max(tile_end, axis=1, keepdims=True)

    tau = lax.broadcasted_iota(I32, (tp, LANES), 0).astype(F32)
    lane_i = lax.broadcasted_iota(I32, (tp, LANES), 1)
    lane_t = lane_i.astype(F32)
    is_e_t = (lane_i >= ROUTE_LANE0) & (lane_i < ROUTE_LANE0 + N_EXPERTS)
    te = jnp.sum(jnp.where(is_e_t & (tile_end <= jnp.minimum(tau, n_used - 1.0)), 1.0, 0.0),
                 axis=1, keepdims=True)
    oh = lane_t == te + ROUTE_LANE0
    pick = lambda row: jnp.sum(jnp.where(oh, row, 0.0), axis=1, keepdims=True)
    seg0_t = pick(tile0) * bpt
    e_lane = lane_t - ROUTE_LANE0
    cand = jnp.where(is_e_t & (e_lane > te) & (tiles_e > 0.0), e_lane, 99.0)
    nxt = jnp.min(cand, axis=1, keepdims=True)
    nxt = jnp.where(nxt == 99.0, -1.0, nxt)
    rows_t = jnp.clip(pick(total) * BLOCK - (tau[:, 0:1] - pick(tile0)) * TR, ROW_STEP, TR)
    rows_t = jnp.ceil(rows_t * (1.0 / ROW_STEP)) * ROW_STEP
    meta = jnp.where(lane_i == 0, te, 0.0)
    meta = jnp.where(lane_i == 1, nxt, meta)
    meta = jnp.where(lane_i == 2, rows_t, meta)
    meta = jnp.where(lane_i == 3, n_used, meta)
    meta_ref[...] = meta.astype(I32)

    bidx = tau * bpt + lane_t
    off = bidx - seg0_t
    ohb = jnp.where(oh, 1.0, 0.0).astype(BF16)

    def per_tile(tab):
        hi = jnp.floor(tab * (1.0 / 32))
        lo = tab - 32.0 * hi
        nt_dot = lambda a: lax.dot_general(ohb, a.astype(BF16), (((1,), (1,)), ((), ())),
                                           preferred_element_type=F32)
        return 32.0 * nt_dot(hi) + nt_dot(lo)

    run_end = per_tile(carry + nblk)
    run_beg = per_tile(carry)
    run_loc = per_tile(loc)
    tile_of = jnp.zeros((tp, LANES), F32)
    for it in range(nt):
        tile_of = tile_of + jnp.where(run_end[:, it:it + 1] <= off, 1.0, 0.0)
    srcv = off
    for it in range(nt):
        srcv = srcv + jnp.where(tile_of == it,
                                run_loc[:, it:it + 1] - run_beg[:, it:it + 1] + it * kb, 0.0)
    valid = (tile_of < nt) & (bidx < n_used * bpt)
    dump_set = tau - DUMP_SETS * jnp.floor((tau + 0.5) * (1.0 / DUMP_SETS))
    dump = nt * kb + dump_set * bpt + lane_t
    src_ref[...] = jnp.where(valid, srcv, kb - 1.0).astype(I32)
    dst_ref[0:tp, :] = jnp.where(valid, srcv, dump).astype(I32)
    dst_ref[tp:tp + SUBLANES, :] = (nt * kb + DUMP_BLOCKS - bpt
                                    + lax.broadcasted_iota(I32, (SUBLANES, LANES), 1))
    used_ref[...] = jnp.broadcast_to(jnp.sum(nblk, axis=1, keepdims=True) * BLOCK,
                                     (nt, LANES)).astype(I32)


def _mixer_router_kernel(tiles_per_seq, nt,
                         x_ref, gmix_ref, win_f32, wconv_ref, gsgu_ref, wsp_ref, bsp_ref,
                         wout_f32, gffn_ref, wrt_f32, brt_ref, wg_hbm, wu_hbm, wd_hbm,
                         x1_ref, route_ref, stage_ref, meta_ref, src_ref, dst_ref, used_ref,
                         wgb_hbm, wub_hbm, wdb_hbm,
                         wcat_s, tri_s, halo_s, mix_s, h2_s, win_ref, wout_ref, wrt_ref, n_s,
                         wg_f, wu_f, wd_f, wg_b, wu_b, wd_b, wisem, wosem):
    i = pl.program_id(0)
    nch = TM // CHUNK

    @pl.when(i == 0)
    def _init():
        r = lax.broadcasted_iota(I32, (CHUNK, CHUNK), 0)
        c = lax.broadcasted_iota(I32, (CHUNK, CHUNK), 1)
        causal = c <= r
        for j in range(HEAD_PAIRS):
            wa = jnp.where(causal, wsp_ref[2 * j], 0.0)
            wb = jnp.where(causal, wsp_ref[2 * j + 1], 0.0)
            wcat_s[j] = jnp.concatenate([wa, wb], axis=1).astype(BF16)
        rr = lax.broadcasted_iota(I32, (TM, TM), 0)
        cc = lax.broadcasted_iota(I32, (TM, TM), 1)
        tri_s[...] = (rr < cc).astype(BF16)
        h2_s[...] = jnp.zeros_like(h2_s)
        win_ref[...] = win_f32[...].astype(BF16)
        wout_ref[...] = wout_f32[...].astype(BF16)
        wrt_ref[...] = wrt_f32[...].astype(BF16)

    @pl.when(i % tiles_per_seq == 0)
    def _seq_start():
        halo_s[...] = jnp.zeros_like(halo_s)

    def weights_in(e):
        return (pltpu.make_async_copy(wg_hbm.at[e], wg_f, wisem.at[0]),
                pltpu.make_async_copy(wu_hbm.at[e], wu_f, wisem.at[1]),
                pltpu.make_async_copy(wd_hbm.at[e], wd_f, wisem.at[2]))

    def weights_out(e):
        return (pltpu.make_async_copy(wg_b, wgb_hbm.at[e], wosem.at[0]),
                pltpu.make_async_copy(wu_b, wub_hbm.at[e], wosem.at[1]),
                pltpu.make_async_copy(wd_b, wdb_hbm.at[e], wosem.at[2]))

    @pl.when(i == 0)
    def _first_weights():
        for cp in weights_in(0):
            cp.start(priority=WEIGHT_DMA_PRIORITY)

    experts_per_step = -(-N_EXPERTS // (nt + 1))
    for k in range(experts_per_step):
        e = i * experts_per_step + k

        @pl.when(e < N_EXPERTS)
        def _cast_expert(e=e):
            for cp in weights_in(e):
                cp.wait()

            @pl.when(e > 0)
            def _previous_written():
                for cp in weights_out(e - 1):
                    cp.wait()

            wg_b[...] = wg_f[...].astype(BF16)
            wu_b[...] = wu_f[...].astype(BF16)
            wd_b[...] = wd_f[...].astype(BF16)
            for cp in weights_out(e):
                cp.start()

            @pl.when(e + 1 < N_EXPERTS)
            def _next_in():
                for cp in weights_in(e + 1):
                    cp.start(priority=WEIGHT_DMA_PRIORITY)

    @pl.when(i == nt)
    def _last_written():
        for cp in weights_out(N_EXPERTS - 1):
            cp.wait()

    def step(mix):
        h2b = h2_s[(i + 1) % 2]
        if mix:
            x = x_ref[...]
            hb = _rms(x, gmix_ref[...]).astype(BF16)

            def proj(k):
                return jnp.dot(hb, win_ref[:, k * 512:(k + 1) * 512], preferred_element_type=F32)

        logits = lax.dot_general(wrt_ref[...], h2b, (((1,), (1,)), ((), ())),
                                 preferred_element_type=F32)[0:ROUTE_ROWS, :] + brt_ref[0:ROUTE_ROWS, :]
        if mix:
            pu = proj(3)
            pv = proj(4)

        ridx = lax.broadcasted_iota(I32, (ROUTE_ROWS, TM), 0).astype(F32)
        neg = jnp.float32(-jnp.inf)
        big = jnp.float32(1e9)
        is_g = ridx < N_GROUPS
        gl = jnp.where(is_g, logits, neg)
        gmax = jnp.max(gl, axis=0, keepdims=True)
        gsum = jnp.sum(jnp.where(is_g, jnp.exp(gl - gmax), 0.0), axis=0, keepdims=True)
        g_w = 1.0 / gsum
        g_idx = jnp.min(jnp.where(gl == gmax, ridx, big), axis=0, keepdims=True)
        lo_row = ROUTE_LANE0 + EXPERTS_PER_GROUP * g_idx
        in_grp = (ridx >= lo_row) & (ridx < lo_row + EXPERTS_PER_GROUP)
        el = jnp.where(in_grp, logits, neg)
        v1 = jnp.max(el, axis=0, keepdims=True)
        i1 = jnp.min(jnp.where(el == v1, ridx, big), axis=0, keepdims=True)
        el2 = jnp.where(ridx == i1, neg, el)
        v2 = jnp.max(el2, axis=0, keepdims=True)
        i2 = jnp.min(jnp.where(el2 == v2, ridx, big), axis=0, keepdims=True)
        e21 = jnp.exp(v2 - v1)
        w1 = g_w / (1.0 + e21)
        w2 = g_w * e21 / (1.0 + e21)

        if mix:
            u = jax.nn.gelu(pu)
            zc = proj(1) * proj(2)
            v = jax.nn.gelu(pv)
            vc = v - jnp.mean(v, axis=-1, keepdims=True)
            vn = vc * lax.rsqrt(jnp.mean(vc * vc, axis=-1, keepdims=True) + EPS) * gsgu_ref[...]
            vnb = vn.astype(BF16)

            row = lax.broadcasted_iota(I32, (TM, CONV_WIDTH), 0)
            h6 = halo_s[6:7, :]
            h7 = halo_s[7:8, :]
            z1 = jnp.where(row == 0, h7, pltpu.roll(zc, 1, 0))
            z2 = jnp.where(row == 0, h6, jnp.where(row == 1, h7, pltpu.roll(zc, 2, 0)))
            conv = z2 * wconv_ref[0:1, :] + z1 * wconv_ref[1:2, :] + zc * wconv_ref[2:3, :]
            halo_s[...] = zc[TM - 8:TM, :]
            mix_s[:, 0:CONV_WIDTH] = (proj(0) * conv).astype(BF16)

        sel1 = ridx == i1
        sel2 = ridx == i2
        onehot = jnp.where(sel1 | sel2, 1.0, 0.0)
        counts = jnp.sum(onehot, axis=1, keepdims=True)
        pad_rows = jnp.zeros((LANES - ROUTE_ROWS, LANES), F32)
        counts_sq = jnp.concatenate([jnp.broadcast_to(counts, (ROUTE_ROWS, LANES)), pad_rows], axis=0)
        blocks_sq = jnp.ceil(counts_sq * (1.0 / BLOCK))
        er = lax.broadcasted_iota(I32, (LANES, LANES), 0)
        ec = lax.broadcasted_iota(I32, (LANES, LANES), 1)
        before = (ec < er).astype(BF16)
        run_start = BLOCK * jnp.dot(before, blocks_sq.astype(BF16),
                                    preferred_element_type=F32)[0:ROUTE_ROWS, 0:1]
        rank = jnp.dot(onehot.astype(BF16), tri_s[...], preferred_element_type=F32) + run_start
        pos1_row = jnp.sum(jnp.where(sel1, rank, 0.0), axis=0, keepdims=True)
        pos2_row = jnp.sum(jnp.where(sel2, rank, 0.0), axis=0, keepdims=True)
        n_s[pl.ds(jnp.where(i == 0, nt, i - 1), 1), :] = counts_sq.T[0:1, :]

        r8 = lax.broadcasted_iota(I32, (SUBLANES, TM), 0)
        rows8 = jnp.where(r8 == 0, i1 - ROUTE_LANE0, 0.0)
        rows8 = jnp.where(r8 == 1, i2 - ROUTE_LANE0, rows8)
        rows8 = jnp.where(r8 == 2, w1, rows8)
        rows8 = jnp.where(r8 == 3, w2, rows8)
        rows8 = jnp.where(r8 == 4, pos1_row, rows8)
        rows8 = jnp.where(r8 == 5, pos2_row, rows8)
        route_ref[...] = jnp.concatenate(
            [rows8, jnp.zeros((LANES - SUBLANES, TM), F32)], axis=0).T

        def sort_rows(lo, hi):
            out_row = (lax.broadcasted_iota(I32, (hi - lo, TM), 0) + lo).astype(F32)
            sort_mat = jnp.where((out_row == pos1_row) | (out_row == pos2_row), 1.0, 0.0).astype(BF16)
            stage_ref[lo:hi, :] = jnp.dot(sort_mat, h2b, preferred_element_type=F32).astype(BF16)

        sort_rows(0, SORT_MAIN // 2)
        if mix:
            left = lax.broadcasted_iota(I32, (CHUNK, LANES), 1) < (LANES // 2)
            zero = jnp.zeros((CHUNK, LANES), BF16)
            for j in range(HEAD_PAIRS):
                cols = []
                for c in range(nch):
                    s = vnb[c * CHUNK:(c + 1) * CHUNK, j * LANES:(j + 1) * LANES]
                    cols.append(jnp.concatenate([jnp.where(left, s, zero), jnp.where(left, zero, s)],
                                                axis=0))
                rhs = jnp.concatenate(cols, axis=1)
                res = jnp.dot(wcat_s[j], rhs, preferred_element_type=F32)
                bias = bsp_ref[:, j * LANES:(j + 1) * LANES]
                for c in range(nch):
                    mixed = res[:, c * LANES:(c + 1) * LANES] + bias
                    uu = u[c * CHUNK:(c + 1) * CHUNK, j * LANES:(j + 1) * LANES]
                    mix_s[c * CHUNK:(c + 1) * CHUNK,
                          CONV_WIDTH + j * LANES:CONV_WIDTH + (j + 1) * LANES] = (uu * mixed).astype(BF16)

        sort_rows(SORT_MAIN // 2, SORT_MAIN)
        if mix:
            x1 = x + jnp.dot(mix_s[...], wout_ref[...], preferred_element_type=F32)
            x1_ref[...] = x1
            h2_s[i % 2] = _rms(x1, gffn_ref[...]).astype(BF16)

        used_rows = BLOCK * jnp.sum(blocks_sq[:, 0:1])
        for lo in range(SORT_MAIN, KOUT, TAIL):
            @pl.when(used_rows > lo)
            def _tail(lo=lo):
                sort_rows(lo, lo + TAIL)

            @pl.when(used_rows <= lo)
            def _empty_tail(lo=lo):
                stage_ref[lo:lo + TAIL, :] = jnp.zeros((TAIL, D_MODEL), BF16)

    pl.when(i < nt)(functools.partial(step, True))

    @pl.when(i == nt)
    def _last():
        step(False)
        _build_tables(n_s, nt, meta_ref, src_ref, dst_ref, used_ref)


def _mixer_router(x2d, g_mix, w_in, w_conv, g_sgu, w_sp, b_sp, w_out, g_ffn, w_rt, b_rt,
                  w_gate, w_up, w_down, seq, n_tiles):
    t = x2d.shape[0]
    nt = t // TM
    tp = -(-n_tiles // SUBLANES) * SUBLANES
    whole = lambda r: pl.BlockSpec((r, LANES), lambda i: (0, 0))
    full = lambda a: pl.BlockSpec(a.shape, lambda i: (0,) * a.ndim, pipeline_mode=pl.Buffered(1))
    any_spec = pl.BlockSpec(memory_space=pl.ANY)
    ins = (g_mix, w_in, w_conv, g_sgu, w_sp, b_sp, w_out, g_ffn, w_rt, b_rt)
    expert_w = (w_gate, w_up, w_down)
    prev = lambda i: (jnp.where(i == 0, nt, i - 1), 0)
    return pl.pallas_call(
        functools.partial(_mixer_router_kernel, seq // TM, nt),
        grid=(nt + 1,),
        in_specs=[pl.BlockSpec((TM, D_MODEL), lambda i: (jnp.minimum(i, nt - 1), 0))]
        + [full(a) for a in ins] + [any_spec] * 3,
        out_specs=[pl.BlockSpec((TM, D_MODEL), lambda i: (jnp.minimum(i, nt - 1), 0)),
                   pl.BlockSpec((TM, LANES), prev),
                   pl.BlockSpec((KOUT, D_MODEL), prev),
                   whole(tp), whole(tp), whole(tp + SUBLANES), whole(nt)] + [any_spec] * 3,
        out_shape=[jax.ShapeDtypeStruct((t, D_MODEL), F32),
                   jax.ShapeDtypeStruct(((nt + 1) * TM, LANES), F32),
                   jax.ShapeDtypeStruct(((nt + 1) * KOUT, D_MODEL), BF16),
                   jax.ShapeDtypeStruct((tp, LANES), I32),
                   jax.ShapeDtypeStruct((tp, LANES), I32),
                   jax.ShapeDtypeStruct((tp + SUBLANES, LANES), I32),
                   jax.ShapeDtypeStruct((nt, LANES), I32)]
        + [jax.ShapeDtypeStruct(w.shape, BF16) for w in expert_w],
        scratch_shapes=[pltpu.VMEM((HEAD_PAIRS, CHUNK, 2 * CHUNK), BF16),
                        pltpu.VMEM((TM, TM), BF16),
                        pltpu.VMEM((8, CONV_WIDTH), F32),
                        pltpu.VMEM((TM, D_MODEL), BF16),
                        pltpu.VMEM((2, TM, D_MODEL), BF16),
                        pltpu.VMEM(w_in.shape, BF16),
                        pltpu.VMEM(w_out.shape, BF16),
                        pltpu.VMEM(w_rt.shape, BF16),
                        pltpu.VMEM((nt + SUBLANES, LANES), F32)]
        + [pltpu.VMEM(w.shape[1:], F32) for w in expert_w]
        + [pltpu.VMEM(w.shape[1:], BF16) for w in expert_w]
        + [pltpu.SemaphoreType.DMA((3,)), pltpu.SemaphoreType.DMA((3,))],
        compiler_params=pltpu.CompilerParams(dimension_semantics=("arbitrary",),
                                             vmem_limit_bytes=VMEM_LIMIT),
        name="mixer_router",
    )(x2d, *ins, *expert_w)


def _expert_kernel(meta_ref, src_ref, dst_ref, stage_ref,
                   wg_hbm, wu_hbm, wd_hbm, out_ref,
                   xbuf, obuf, wg_s, wu_s, wd_s, par_ref, sem, osem, wsem):
    i = pl.program_id(0)
    n_used = meta_ref[3]
    slot = i % N_SLOTS
    bpt = BLOCKS_PER_ROW_TILE
    dump_row = dst_ref.shape[0] // bpt - SUBLANES
    expert_of = lambda tile: meta_ref[tile * META_W]

    def gather_copy(seq, j):
        tile = jnp.minimum(seq, n_used - 1)
        src = pl.multiple_of(src_ref[tile * bpt + j] * BLOCK, BLOCK)
        slot_ = seq % N_SLOTS
        return pltpu.make_async_copy(stage_ref.at[pl.ds(src, BLOCK)],
                                     xbuf.at[slot_, pl.ds(j * BLOCK, BLOCK)], sem.at[slot_])

    def scatter_copy(tile, j):
        row = jnp.where(tile < 0, dump_row, tile)
        dst = pl.multiple_of(dst_ref[row * bpt + j] * BLOCK, BLOCK)
        slot_ = (tile + N_SLOTS) % N_SLOTS
        return pltpu.make_async_copy(obuf.at[slot_, pl.ds(j * BLOCK, BLOCK)],
                                     out_ref.at[pl.ds(dst, BLOCK)], osem.at[slot_])

    def each_block(fn):
        for j in range(BLOCKS_PER_ROW_TILE):
            fn(j)

    def weight_copies(e, p):
        return (pltpu.make_async_copy(wg_hbm.at[e], wg_s.at[p], wsem.at[0]),
                pltpu.make_async_copy(wu_hbm.at[e], wu_s.at[p], wsem.at[1]),
                pltpu.make_async_copy(wd_hbm.at[e], wd_s.at[p], wsem.at[2]))

    @pl.when(i == 0)
    def _first():
        each_block(lambda j: gather_copy(0, j).start())
        each_block(lambda j: gather_copy(1, j).start())
        par_ref[0] = 0
        for cp in weight_copies(expert_of(0), 0):
            cp.start(priority=WEIGHT_DMA_PRIORITY)
        obuf[...] = jnp.zeros_like(obuf)

    @pl.when((i >= 2) & (i - 3 < n_used))
    def _free_out_slot():
        each_block(lambda j: scatter_copy(i - 3, j).wait())

    @pl.when(i == n_used)
    def _after_last_tile():
        each_block(lambda j: gather_copy(n_used, j).wait())
        each_block(lambda j: gather_copy(n_used + 1, j).wait())
        each_block(lambda j: scatter_copy(n_used - 1, j).start())

    @pl.when(i < n_used)
    def _tile():
        e = expert_of(i)
        prev = expert_of(jnp.maximum(i - 1, 0))

        @pl.when((i == 0) | (e != prev))
        def _new_expert():
            p = jnp.where(i == 0, 0, 1 - par_ref[0])
            for cp in weight_copies(e, p):
                cp.wait()
            par_ref[0] = p
            nxt = meta_ref[i * META_W + 1]

            @pl.when(nxt >= 0)
            def _next_weights():
                for cp in weight_copies(nxt, 1 - p):
                    cp.start(priority=WEIGHT_DMA_PRIORITY)

        each_block(lambda j: gather_copy(i, j).wait())

        def mlp(rows):
            p = par_ref[0]
            hb = xbuf[slot, 0:rows, :]
            a = jnp.dot(hb, wg_s[p], preferred_element_type=F32)
            b = jnp.dot(hb, wu_s[p], preferred_element_type=F32)
            each_block(lambda j: gather_copy(i + 2, j).start())
            each_block(lambda j: scatter_copy(i - 1, j).start())
            hid = (a * jax.nn.sigmoid(a) * b).astype(BF16)
            obuf[slot, 0:rows, :] = jnp.dot(hid, wd_s[p], preferred_element_type=F32).astype(BF16)

        rows_needed = meta_ref[i * META_W + 2]
        for rows in range(ROW_STEP, TR + 1, ROW_STEP):
            pl.when(rows_needed == rows)(functools.partial(mlp, rows))


def _experts(meta, src_block, dst_block, stage, w_gate, w_up, w_down, n_tiles):
    any_spec = pl.BlockSpec(memory_space=pl.ANY)
    return pl.pallas_call(
        _expert_kernel,
        grid_spec=pltpu.PrefetchScalarGridSpec(
            num_scalar_prefetch=3,
            grid=(n_tiles,),
            in_specs=[any_spec, any_spec, any_spec, any_spec],
            out_specs=any_spec,
            scratch_shapes=[pltpu.VMEM((N_SLOTS, TR, D_MODEL), BF16),
                            pltpu.VMEM((N_SLOTS, TR, D_MODEL), BF16),
                            pltpu.VMEM((2, D_MODEL, D_EXPERT), BF16),
                            pltpu.VMEM((2, D_MODEL, D_EXPERT), BF16),
                            pltpu.VMEM((2, D_EXPERT, D_MODEL), BF16),
                            pltpu.SMEM((1,), I32),
                            pltpu.SemaphoreType.DMA((N_SLOTS,)),
                            pltpu.SemaphoreType.DMA((N_SLOTS,)),
                            pltpu.SemaphoreType.DMA((3,))]),
        out_shape=jax.ShapeDtypeStruct(stage.shape, stage.dtype),
        input_output_aliases={3: 0},
        compiler_params=pltpu.CompilerParams(dimension_semantics=("arbitrary",),
                                             vmem_limit_bytes=VMEM_LIMIT),
        name="experts",
    )(meta, src_block, dst_block, stage, w_gate, w_up, w_down)


def _combine_kernel(used_ref, x1_ref, route_ref, p_ref, gple_ref, wpg_f32, wpp_f32, gfin_ref,
                    srt_ref, out_ref, y_s, wpg_ref, wpp_ref):
    i = pl.program_id(0)

    @pl.when(i == 0)
    def _init():
        wpg_ref[...] = wpg_f32[...].astype(BF16)
        wpp_ref[...] = wpp_f32[...].astype(BF16)

    route = route_ref[...]
    w1, w2, pos1, pos2 = (route[:, k:k + 1] for k in (2, 3, 4, 5))

    def unsort(lo, hi):
        col = (lax.broadcasted_iota(I32, (TM, hi - lo), 1) + lo).astype(F32)
        mat = jnp.where(col == pos1, w1, jnp.where(col == pos2, w2, 0.0)).astype(BF16)
        return jnp.dot(mat, srt_ref[lo:hi, :], preferred_element_type=F32)

    def finish(y):
        pe = jnp.dot(p_ref[...].astype(BF16), wpp_ref[...], preferred_element_type=F32)
        hrows = TM // 2
        for h in range(2):
            rs = slice(h * hrows, (h + 1) * hrows)
            x2 = x1_ref[rs, :] + y[rs, :]
            hg = _rms(x2, gple_ref[...]).astype(BF16)
            gate = jax.nn.sigmoid(jnp.dot(hg, wpg_ref[...], preferred_element_type=F32))
            x3 = x2 + gate * pe[rs, :]
            out_ref[rs, :] = _rms(x3, gfin_ref[...])

    @pl.when(used_ref[i] <= SORT_MAIN)
    def _common():
        finish(unsort(0, SORT_MAIN))

    @pl.when(used_ref[i] > SORT_MAIN)
    def _with_tails():
        y_s[...] = unsort(0, SORT_MAIN)
        for lo in range(SORT_MAIN, KOUT, TAIL):
            @pl.when(used_ref[i] > lo)
            def _tail(lo=lo):
                y_s[...] += unsort(lo, lo + TAIL)
        finish(y_s[...])


def _combine(used, x1, route, p2d, g_ple, w_pg, w_pp, g_fin, sorted_out):
    t = p2d.shape[0]
    full = lambda a: pl.BlockSpec(a.shape, lambda i, *_: (0,) * a.ndim)
    return pl.pallas_call(
        _combine_kernel,
        grid_spec=pltpu.PrefetchScalarGridSpec(
            num_scalar_prefetch=1,
            grid=(t // TM,),
            in_specs=[pl.BlockSpec((TM, D_MODEL), lambda i, *_: (i, 0)),
                      pl.BlockSpec((TM, LANES), lambda i, *_: (i, 0)),
                      pl.BlockSpec((TM, PLE_DIM), lambda i, *_: (i, 0)),
                      full(g_ple), full(w_pg), full(w_pp), full(g_fin),
                      pl.BlockSpec((KOUT, D_MODEL), lambda i, *_: (i, 0))],
            out_specs=pl.BlockSpec((TM, D_MODEL), lambda i, *_: (i, 0)),
            scratch_shapes=[pltpu.VMEM((TM, D_MODEL), F32),
                            pltpu.VMEM(w_pg.shape, BF16),
                            pltpu.VMEM(w_pp.shape, BF16)]),
        out_shape=jax.ShapeDtypeStruct((t, D_MODEL), F32),
        compiler_params=pltpu.CompilerParams(dimension_semantics=("arbitrary",),
                                             vmem_limit_bytes=VMEM_LIMIT),
        name="combine_ple",
    )(used, x1, route, p2d, g_ple, w_pg, w_pp, g_fin, sorted_out)


def kernel(x, p, g_mix, w_in, w_conv, g_sgu, w_spatial, b_spatial, w_out, g_ffn, w_group, b_group,
           w_router, b_router, w_gate, w_up, w_down, g_ple, w_ple_gate, w_ple_proj, g_final):
    bsz, seq, d = x.shape
    t = bsz * seq
    assert w_in.shape[0] == 1, "single-layer block"
    assert d == D_MODEL and seq % TM == 0 and TM % CHUNK == 0
    nt = t // TM
    max_rows = TOP_K * t + nt * N_EXPERTS * (BLOCK - 1)
    n_tiles = (max_rows + N_EXPERTS * (TR - 1)) // TR + N_SLOTS
    l = 0

    w_rt = jnp.concatenate(
        [w_group[l], jnp.transpose(w_router[l], (1, 0, 2)).reshape(d, N_EXPERTS)], axis=1)
    w_rt = jnp.pad(w_rt, ((0, 0), (0, LANES - w_rt.shape[1]))).T
    b_rt = jnp.pad(jnp.concatenate([b_group[l], b_router[l].reshape(-1)]),
                   (0, LANES - N_GROUPS - N_EXPERTS)).reshape(LANES, 1)
    b_sp = jnp.repeat(b_spatial[l].T, SGU_WIDTH // SGU_HEADS, axis=1)

    x1, route, stage, meta, src, dst, used, wg_b, wu_b, wd_b = _mixer_router(
        x.reshape(t, d), g_mix[l].reshape(1, d), w_in[l], w_conv[l],
        g_sgu[l].reshape(1, -1), w_spatial[l], b_sp, w_out[l],
        g_ffn[l].reshape(1, d), w_rt, b_rt,
        w_gate[l].reshape(N_EXPERTS, d, D_EXPERT), w_up[l].reshape(N_EXPERTS, d, D_EXPERT),
        w_down[l].reshape(N_EXPERTS, D_EXPERT, d), seq, n_tiles)

    bpt = BLOCKS_PER_ROW_TILE
    meta, src, dst, used = (meta[:, :META_W].reshape(-1), src[:, :bpt].reshape(-1),
                            dst[:, :bpt].reshape(-1), used[:, 0])
    sorted_out = _experts(meta, src, dst, stage, wg_b, wu_b, wd_b, n_tiles)
    out = _combine(used, x1, route, p[l].reshape(t, PLE_DIM), g_ple[l].reshape(1, d),
                   w_ple_gate[l], w_ple_proj[l], g_final.reshape(1, d), sorted_out)
    return out.reshape(bsz, seq, d)
```

```python
import functools

import jax
import jax.numpy as jnp
from jax import lax
from jax.experimental import pallas as pl
from jax.experimental.pallas import tpu as pltpu

F32 = jnp.float32
BF16 = jnp.bfloat16
I32 = jnp.int32

EPS = 1e-6
D_MODEL = 1024
CONV_WIDTH = 512
SGU_WIDTH = 512
SGU_HEADS = 8
HEAD_PAIRS = SGU_HEADS // 2
CHUNK = 128
N_GROUPS = 4
EXPERTS_PER_GROUP = 8
N_EXPERTS = N_GROUPS * EXPERTS_PER_GROUP
TOP_K = 2
D_EXPERT = 512
PLE_DIM = 256
LANES = 128
SUBLANES = 8
BLOCK = 2 * SUBLANES
ROUTE_LANE0 = N_GROUPS
ROUTE_ROWS = 40

TM = 512
TR = 640
BLOCKS_PER_ROW_TILE = TR // BLOCK
KOUT = TOP_K * TM + N_EXPERTS * BLOCK
TAIL = 128
SORT_MAIN = KOUT - 2 * TAIL
META_W = 4
ROW_STEP = 128
N_SLOTS = 3
DUMP_SETS = 2
DUMP_BLOCKS = DUMP_SETS * BLOCKS_PER_ROW_TILE
assert DUMP_BLOCKS * BLOCK <= KOUT
WEIGHT_DMA_PRIORITY = 1
VMEM_LIMIT = 56 * 1024 * 1024


def _rms(x, g):
    return x * lax.rsqrt(jnp.mean(x * x, axis=-1, keepdims=True) + EPS) * g


def _build_tables(n_s, nt, meta_ref, src_ref, dst_ref, used_ref):
    tp = meta_ref.shape[0]
    bpt = BLOCKS_PER_ROW_TILE
    kb = KOUT // BLOCK
    n = n_s[0:nt, :]
    lane_n = lax.broadcasted_iota(I32, (nt, LANES), 1)
    is_e = (lane_n >= ROUTE_LANE0) & (lane_n < ROUTE_LANE0 + N_EXPERTS)
    nblk = jnp.where(is_e, jnp.floor((n + (BLOCK - 1)) * (1.0 / BLOCK)), 0.0)
    er = lax.broadcasted_iota(I32, (LANES, LANES), 0)
    ec = lax.broadcasted_iota(I32, (LANES, LANES), 1)
    loc = jnp.dot(nblk.astype(BF16), (er < ec).astype(BF16), preferred_element_type=F32)
    ti = lax.broadcasted_iota(I32, (nt, nt), 0)
    tj = lax.broadcasted_iota(I32, (nt, nt), 1)
    carry = jnp.dot((tj < ti).astype(BF16), nblk.astype(BF16), preferred_element_type=F32)
    total = jnp.sum(nblk, axis=0, keepdims=True)
    tiles_e = jnp.floor((total + (bpt - 1)) * (1.0 / bpt))
    tile_end = jnp.dot(jnp.broadcast_to(tiles_e, (SUBLANES, LANES)).astype(BF16),
                       (er <= ec).astype(BF16), preferred_element_type=F32)[0:1, :]
    tile0 = tile_end - tiles_e
    n_used = jnp.max(tile_end, axis=1, keepdims=True)

    tau = lax.broadcasted_iota(I32, (tp, LANES), 0).astype(F32)
    lane_i = lax.broadcasted_iota(I32, (tp, LANES), 1)
    lane_t = lane_i.astype(F32)
    is_e_t = (lane_i >= ROUTE_LANE0) & (lane_i < ROUTE_LANE0 + N_EXPERTS)
    te = jnp.sum(jnp.where(is_e_t & (tile_end <= jnp.minimum(tau, n_used - 1.0)), 1.0, 0.0),
                 axis=1, keepdims=True)
    oh = lane_t == te + ROUTE_LANE0
    pick = lambda row: jnp.sum(jnp.where(oh, row, 0.0), axis=1, keepdims=True)
    seg0_t = pick(tile0) * bpt
    e_lane = lane_t - ROUTE_LANE0
    cand = jnp.where(is_e_t & (e_lane > te) & (tiles_e > 0.0), e_lane, 99.0)
    nxt = jnp.min(cand, axis=1, keepdims=True)
    nxt = jnp.where(nxt == 99.0, -1.0, nxt)
    rows_t = jnp.clip(pick(total) * BLOCK - (tau[:, 0:1] - pick(tile0)) * TR, ROW_STEP, TR)
    rows_t = jnp.ceil(rows_t * (1.0 / ROW_STEP)) * ROW_STEP
    meta = jnp.where(lane_i == 0, te, 0.0)
    meta = jnp.where(lane_i == 1, nxt, meta)
    meta = jnp.where(lane_i == 2, rows_t, meta)
    meta = jnp.where(lane_i == 3, n_used, meta)
    meta_ref[...] = meta.astype(I32)

    bidx = tau * bpt + lane_t
    off = bidx - seg0_t
    ohb = jnp.where(oh, 1.0, 0.0).astype(BF16)

    def per_tile(tab):
        hi = jnp.floor(tab * (1.0 / 32))
        lo = tab - 32.0 * hi
        nt_dot = lambda a: lax.dot_general(ohb, a.astype(BF16), (((1,), (1,)), ((), ())),
                                           preferred_element_type=F32)
        return 32.0 * nt_dot(hi) + nt_dot(lo)

    run_end = per_tile(carry + nblk)
    run_beg = per_tile(carry)
    run_loc = per_tile(loc)
    tile_of = jnp.zeros((tp, LANES), F32)
    for it in range(nt):
        tile_of = tile_of + jnp.where(run_end[:, it:it + 1] <= off, 1.0, 0.0)
    srcv = off
    for it in range(nt):
        srcv = srcv + jnp.where(tile_of == it,
                                run_loc[:, it:it + 1] - run_beg[:, it:it + 1] + it * kb, 0.0)
    valid = (tile_of < nt) & (bidx < n_used * bpt)
    dump_set = tau - DUMP_SETS * jnp.floor((tau + 0.5) * (1.0 / DUMP_SETS))
    dump = nt * kb + dump_set * bpt + lane_t
    src_ref[...] = jnp.where(valid, srcv, kb - 1.0).astype(I32)
    dst_ref[0:tp, :] = jnp.where(valid, srcv, dump).astype(I32)
    dst_ref[tp:tp + SUBLANES, :] = (nt * kb + DUMP_BLOCKS - bpt
                                    + lax.broadcasted_iota(I32, (SUBLANES, LANES), 1))
    used_ref[...] = jnp.broadcast_to(jnp.sum(nblk, axis=1, keepdims=True) * BLOCK,
                                     (nt, LANES)).astype(I32)


def _mixer_router_kernel(tiles_per_seq, nt,
                         x_ref, gmix_ref, win_f32, wconv_ref, gsgu_ref, wsp_ref, bsp_ref,
                         wout_f32, gffn_ref, wrt_f32, brt_ref, wg_hbm, wu_hbm, wd_hbm,
                         x1_ref, route_ref, stage_ref, meta_ref, src_ref, dst_ref, used_ref,
                         wgb_hbm, wub_hbm, wdb_hbm,
                         wcat_s, tri_s, halo_s, mix_s, h2_s, win_ref, wout_ref, wrt_ref, n_s,
                         wg_f, wu_f, wd_f, wg_b, wu_b, wd_b, wisem, wosem):
    i = pl.program_id(0)
    nch = TM // CHUNK

    @pl.when(i == 0)
    def _init():
        r = lax.broadcasted_iota(I32, (CHUNK, CHUNK), 0)
        c = lax.broadcasted_iota(I32, (CHUNK, CHUNK), 1)
        causal = c <= r
        for j in range(HEAD_PAIRS):
            wa = jnp.where(causal, wsp_ref[2 * j], 0.0)
            wb = jnp.where(causal, wsp_ref[2 * j + 1], 0.0)
            wcat_s[j] = jnp.concatenate([wa, wb], axis=1).astype(BF16)
        rr = lax.broadcasted_iota(I32, (TM, TM), 0)
        cc = lax.broadcasted_iota(I32, (TM, TM), 1)
        tri_s[...] = (rr < cc).astype(BF16)
        h2_s[...] = jnp.zeros_like(h2_s)
        win_ref[...] = win_f32[...].astype(BF16)
        wout_ref[...] = wout_f32[...].astype(BF16)
        wrt_ref[...] = wrt_f32[...].astype(BF16)

    @pl.when(i % tiles_per_seq == 0)
    def _seq_start():
        halo_s[...] = jnp.zeros_like(halo_s)

    def weights_in(e):
        return (pltpu.make_async_copy(wg_hbm.at[e], wg_f, wisem.at[0]),
                pltpu.make_async_copy(wu_hbm.at[e], wu_f, wisem.at[1]),
                pltpu.make_async_copy(wd_hbm.at[e], wd_f, wisem.at[2]))

    def weights_out(e):
        return (pltpu.make_async_copy(wg_b, wgb_hbm.at[e], wosem.at[0]),
                pltpu.make_async_copy(wu_b, wub_hbm.at[e], wosem.at[1]),
                pltpu.make_async_copy(wd_b, wdb_hbm.at[e], wosem.at[2]))

    @pl.when(i == 0)
    def _first_weights():
        for cp in weights_in(0):
            cp.start(priority=WEIGHT_DMA_PRIORITY)

    experts_per_step = -(-N_EXPERTS // (nt + 1))
    for k in range(experts_per_step):
        e = i * experts_per_step + k

        @pl.when(e < N_EXPERTS)
        def _cast_expert(e=e):
            for cp in weights_in(e):
                cp.wait()

            @pl.when(e > 0)
            def _previous_written():
                for cp in weights_out(e - 1):
                    cp.wait()

            wg_b[...] = wg_f[...].astype(BF16)
            wu_b[...] = wu_f[...].astype(BF16)
            wd_b[...] = wd_f[...].astype(BF16)
            for cp in weights_out(e):
                cp.start()

            @pl.when(e + 1 < N_EXPERTS)
            def _next_in():
                for cp in weights_in(e + 1):
                    cp.start(priority=WEIGHT_DMA_PRIORITY)

    @pl.when(i == nt)
    def _last_written():
        for cp in weights_out(N_EXPERTS - 1):
            cp.wait()

    def step(mix):
        h2b = h2_s[(i + 1) % 2]
        if mix:
            x = x_ref[...]
            hb = _rms(x, gmix_ref[...]).astype(BF16)

            def proj(k):
                return jnp.dot(hb, win_ref[:, k * 512:(k + 1) * 512], preferred_element_type=F32)

        logits = lax.dot_general(wrt_ref[...], h2b, (((1,), (1,)), ((), ())),
                                 preferred_element_type=F32)[0:ROUTE_ROWS, :] + brt_ref[0:ROUTE_ROWS, :]
        if mix:
            pu = proj(3)
            pv = proj(4)

        ridx = lax.broadcasted_iota(I32, (ROUTE_ROWS, TM), 0).astype(F32)
        neg = jnp.float32(-jnp.inf)
        big = jnp.float32(1e9)
        is_g = ridx < N_GROUPS
        gl = jnp.where(is_g, logits, neg)
        gmax = jnp.max(gl, axis=0, keepdims=True)
        gsum = jnp.sum(jnp.where(is_g, jnp.exp(gl - gmax), 0.0), axis=0, keepdims=True)
        g_w = 1.0 / gsum
        g_idx = jnp.min(jnp.where(gl == gmax, ridx, big), axis=0, keepdims=True)
        lo_row = ROUTE_LANE0 + EXPERTS_PER_GROUP * g_idx
        in_grp = (ridx >= lo_row) & (ridx < lo_row + EXPERTS_PER_GROUP)
        el = jnp.where(in_grp, logits, neg)
        v1 = jnp.max(el, axis=0, keepdims=True)
        i1 = jnp.min(jnp.where(el == v1, ridx, big), axis=0, keepdims=True)
        el2 = jnp.where(ridx == i1, neg, el)
        v2 = jnp.max(el2, axis=0, keepdims=True)
        i2 = jnp.min(jnp.where(el2 == v2, ridx, big), axis=0, keepdims=True)
        e21 = jnp.exp(v2 - v1)
        w1 = g_w / (1.0 + e21)
        w2 = g_w * e21 / (1.0 + e21)

        if mix:
            u = jax.nn.gelu(pu)
            zc = proj(1) * proj(2)
            v = jax.nn.gelu(pv)
            vc = v - jnp.mean(v, axis=-1, keepdims=True)
            vn = vc * lax.rsqrt(jnp.mean(vc * vc, axis=-1, keepdims=True) + EPS) * gsgu_ref[...]
            vnb = vn.astype(BF16)

            row = lax.broadcasted_iota(I32, (TM, CONV_WIDTH), 0)
            h6 = halo_s[6:7, :]
            h7 = halo_s[7:8, :]
            z1 = jnp.where(row == 0, h7, pltpu.roll(zc, 1, 0))
            z2 = jnp.where(row == 0, h6, jnp.where(row == 1, h7, pltpu.roll(zc, 2, 0)))
            conv = z2 * wconv_ref[0:1, :] + z1 * wconv_ref[1:2, :] + zc * wconv_ref[2:3, :]
            halo_s[...] = zc[TM - 8:TM, :]
            mix_s[:, 0:CONV_WIDTH] = (proj(0) * conv).astype(BF16)

        sel1 = ridx == i1
        sel2 = ridx == i2
        onehot = jnp.where(sel1 | sel2, 1.0, 0.0)
        counts = jnp.sum(onehot, axis=1, keepdims=True)
        pad_rows = jnp.zeros((LANES - ROUTE_ROWS, LANES), F32)
        counts_sq = jnp.concatenate([jnp.broadcast_to(counts, (ROUTE_ROWS, LANES)), pad_rows], axis=0)
        blocks_sq = jnp.ceil(counts_sq * (1.0 / BLOCK))
        er = lax.broadcasted_iota(I32, (LANES, LANES), 0)
        ec = lax.broadcasted_iota(I32, (LANES, LANES), 1)
        before = (ec < er).astype(BF16)
        run_start = BLOCK * jnp.dot(before, blocks_sq.astype(BF16),
                                    preferred_element_type=F32)[0:ROUTE_ROWS, 0:1]
        rank = jnp.dot(onehot.astype(BF16), tri_s[...], preferred_element_type=F32) + run_start
        pos1_row = jnp.sum(jnp.where(sel1, rank, 0.0), axis=0, keepdims=True)
        pos2_row = jnp.sum(jnp.where(sel2, rank, 0.0), axis=0, keepdims=True)
        n_s[pl.ds(jnp.where(i == 0, nt, i - 1), 1), :] = counts_sq.T[0:1, :]

        r8 = lax.broadcasted_iota(I32, (SUBLANES, TM), 0)
        rows8 = jnp.where(r8 == 0, i1 - ROUTE_LANE0, 0.0)
        rows8 = jnp.where(r8 == 1, i2 - ROUTE_LANE0, rows8)
        rows8 = jnp.where(r8 == 2, w1, rows8)
        rows8 = jnp.where(r8 == 3, w2, rows8)
        rows8 = jnp.where(r8 == 4, pos1_row, rows8)
        rows8 = jnp.where(r8 == 5, pos2_row, rows8)
        route_ref[...] = jnp.concatenate(
            [rows8, jnp.zeros((LANES - SUBLANES, TM), F32)], axis=0).T

        def sort_rows(lo, hi):
            out_row = (lax.broadcasted_iota(I32, (hi - lo, TM), 0) + lo).astype(F32)
            sort_mat = jnp.where((out_row == pos1_row) | (out_row == pos2_row), 1.0, 0.0).astype(BF16)
            stage_ref[lo:hi, :] = jnp.dot(sort_mat, h2b, preferred_element_type=F32).astype(BF16)

        sort_rows(0, SORT_MAIN // 2)
        if mix:
            left = lax.broadcasted_iota(I32, (CHUNK, LANES), 1) < (LANES // 2)
            zero = jnp.zeros((CHUNK, LANES), BF16)
            for j in range(HEAD_PAIRS):
                cols = []
                for c in range(nch):
                    s = vnb[c * CHUNK:(c + 1) * CHUNK, j * LANES:(j + 1) * LANES]
                    cols.append(jnp.concatenate([jnp.where(left, s, zero), jnp.where(left, zero, s)],
                                                axis=0))
                rhs = jnp.concatenate(cols, axis=1)
                res = jnp.dot(wcat_s[j], rhs, preferred_element_type=F32)
                bias = bsp_ref[:, j * LANES:(j + 1) * LANES]
                for c in range(nch):
                    mixed = res[:, c * LANES:(c + 1) * LANES] + bias
                    uu = u[c * CHUNK:(c + 1) * CHUNK, j * LANES:(j + 1) * LANES]
                    mix_s[c * CHUNK:(c + 1) * CHUNK,
                          CONV_WIDTH + j * LANES:CONV_WIDTH + (j + 1) * LANES] = (uu * mixed).astype(BF16)

        sort_rows(SORT_MAIN // 2, SORT_MAIN)
        if mix:
            x1 = x + jnp.dot(mix_s[...], wout_ref[...], preferred_element_type=F32)
            x1_ref[...] = x1
            h2_s[i % 2] = _rms(x1, gffn_ref[...]).astype(BF16)

        used_rows = BLOCK * jnp.sum(blocks_sq[:, 0:1])
        for lo in range(SORT_MAIN, KOUT, TAIL):
            @pl.when(used_rows > lo)
            def _tail(lo=lo):
                sort_rows(lo, lo + TAIL)

            @pl.when(used_rows <= lo)
            def _empty_tail(lo=lo):
                stage_ref[lo:lo + TAIL, :] = jnp.zeros((TAIL, D_MODEL), BF16)

    pl.when(i < nt)(functools.partial(step, True))

    @pl.when(i == nt)
    def _last():
        step(False)
        _build_tables(n_s, nt, meta_ref, src_ref, dst_ref, used_ref)


def _mixer_router(x2d, g_mix, w_in, w_conv, g_sgu, w_sp, b_sp, w_out, g_ffn, w_rt, b_rt,
                  w_gate, w_up, w_down, seq, n_tiles):
    t = x2d.shape[0]
    nt = t // TM
    tp = -(-n_tiles // SUBLANES) * SUBLANES
    whole = lambda r: pl.BlockSpec((r, LANES), lambda i: (0, 0))
    full = lambda a: pl.BlockSpec(a.shape, lambda i: (0,) * a.ndim, pipeline_mode=pl.Buffered(1))
    any_spec = pl.BlockSpec(memory_space=pl.ANY)
    ins = (g_mix, w_in, w_conv, g_sgu, w_sp, b_sp, w_out, g_ffn, w_rt, b_rt)
    expert_w = (w_gate, w_up, w_down)
    prev = lambda i: (jnp.where(i == 0, nt, i - 1), 0)
    return pl.pallas_call(
        functools.partial(_mixer_router_kernel, seq // TM, nt),
        grid=(nt + 1,),
        in_specs=[pl.BlockSpec((TM, D_MODEL), lambda i: (jnp.minimum(i, nt - 1), 0))]
        + [full(a) for a in ins] + [any_spec] * 3,
        out_specs=[pl.BlockSpec((TM, D_MODEL), lambda i: (jnp.minimum(i, nt - 1), 0)),
                   pl.BlockSpec((TM, LANES), prev),
                   pl.BlockSpec((KOUT, D_MODEL), prev),
                   whole(tp), whole(tp), whole(tp + SUBLANES), whole(nt)] + [any_spec] * 3,
        out_shape=[jax.ShapeDtypeStruct((t, D_MODEL), F32),
                   jax.ShapeDtypeStruct(((nt + 1) * TM, LANES), F32),
                   jax.ShapeDtypeStruct(((nt + 1) * KOUT, D_MODEL), BF16),
                   jax.ShapeDtypeStruct((tp, LANES), I32),
                   jax.ShapeDtypeStruct((tp, LANES), I32),
                   jax.ShapeDtypeStruct((tp + SUBLANES, LANES), I32),
                   jax.ShapeDtypeStruct((nt, LANES), I32)]
        + [jax.ShapeDtypeStruct(w.shape, BF16) for w in expert_w],
        scratch_shapes=[pltpu.VMEM((HEAD_PAIRS, CHUNK, 2 * CHUNK), BF16),
                        pltpu.VMEM((TM, TM), BF16),
                        pltpu.VMEM((8, CONV_WIDTH), F32),
                        pltpu.VMEM((TM, D_MODEL), BF16),
                        pltpu.VMEM((2, TM, D_MODEL), BF16),
                        pltpu.VMEM(w_in.shape, BF16),
                        pltpu.VMEM(w_out.shape, BF16),
                        pltpu.VMEM(w_rt.shape, BF16),
                        pltpu.VMEM((nt + SUBLANES, LANES), F32)]
        + [pltpu.VMEM(w.shape[1:], F32) for w in expert_w]
        + [pltpu.VMEM(w.shape[1:], BF16) for w in expert_w]
        + [pltpu.SemaphoreType.DMA((3,)), pltpu.SemaphoreType.DMA((3,))],
        compiler_params=pltpu.CompilerParams(dimension_semantics=("arbitrary",),
                                             vmem_limit_bytes=VMEM_LIMIT),
        name="mixer_router",
    )(x2d, *ins, *expert_w)


def _expert_kernel(meta_ref, src_ref, dst_ref, stage_ref,
                   wg_hbm, wu_hbm, wd_hbm, out_ref,
                   xbuf, obuf, wg_s, wu_s, wd_s, par_ref, sem, osem, wsem):
    i = pl.program_id(0)
    n_used = meta_ref[3]
    slot = i % N_SLOTS
    bpt = BLOCKS_PER_ROW_TILE
    dump_row = dst_ref.shape[0] // bpt - SUBLANES
    expert_of = lambda tile: meta_ref[tile * META_W]

    def gather_copy(seq, j):
        tile = jnp.minimum(seq, n_used - 1)
        src = pl.multiple_of(src_ref[tile * bpt + j] * BLOCK, BLOCK)
        slot_ = seq % N_SLOTS
        return pltpu.make_async_copy(stage_ref.at[pl.ds(src, BLOCK)],
                                     xbuf.at[slot_, pl.ds(j * BLOCK, BLOCK)], sem.at[slot_])

    def scatter_copy(tile, j):
        row = jnp.where(tile < 0, dump_row, tile)
        dst = pl.multiple_of(dst_ref[row * bpt + j] * BLOCK, BLOCK)
        slot_ = (tile + N_SLOTS) % N_SLOTS
        return pltpu.make_async_copy(obuf.at[slot_, pl.ds(j * BLOCK, BLOCK)],
                                     out_ref.at[pl.ds(dst, BLOCK)], osem.at[slot_])

    def each_block(fn):
        for j in range(BLOCKS_PER_ROW_TILE):
            fn(j)

    def weight_copies(e, p):
        return (pltpu.make_async_copy(wg_hbm.at[e], wg_s.at[p], wsem.at[0]),
                pltpu.make_async_copy(wu_hbm.at[e], wu_s.at[p], wsem.at[1]),
                pltpu.make_async_copy(wd_hbm.at[e], wd_s.at[p], wsem.at[2]))

    @pl.when(i == 0)
    def _first():
        each_block(lambda j: gather_copy(0, j).start())
        each_block(lambda j: gather_copy(1, j).start())
        par_ref[0] = 0
        for cp in weight_copies(expert_of(0), 0):
            cp.start(priority=WEIGHT_DMA_PRIORITY)
        obuf[...] = jnp.zeros_like(obuf)

    @pl.when((i >= 2) & (i - 3 < n_used))
    def _free_out_slot():
        each_block(lambda j: scatter_copy(i - 3, j).wait())

    @pl.when(i == n_used)
    def _after_last_tile():
        each_block(lambda j: gather_copy(n_used, j).wait())
        each_block(lambda j: gather_copy(n_used + 1, j).wait())
        each_block(lambda j: scatter_copy(n_used - 1, j).start())

    @pl.when(i < n_used)
    def _tile():
        e = expert_of(i)
        prev = expert_of(jnp.maximum(i - 1, 0))

        @pl.when((i == 0) | (e != prev))
        def _new_expert():
            p = jnp.where(i == 0, 0, 1 - par_ref[0])
            for cp in weight_copies(e, p):
                cp.wait()
            par_ref[0] = p
            nxt = meta_ref[i * META_W + 1]

            @pl.when(nxt >= 0)
            def _next_weights():
                for cp in weight_copies(nxt, 1 - p):
                    cp.start(priority=WEIGHT_DMA_PRIORITY)

        each_block(lambda j: gather_copy(i, j).wait())

        def mlp(rows):
            p = par_ref[0]
            hb = xbuf[slot, 0:rows, :]
            a = jnp.dot(hb, wg_s[p], preferred_element_type=F32)
            b = jnp.dot(hb, wu_s[p], preferred_element_type=F32)
            each_block(lambda j: gather_copy(i + 2, j).start())
            each_block(lambda j: scatter_copy(i - 1, j).start())
            hid = (a * jax.nn.sigmoid(a) * b).astype(BF16)
            obuf[slot, 0:rows, :] = jnp.dot(hid, wd_s[p], preferred_element_type=F32).astype(BF16)

        rows_needed = meta_ref[i * META_W + 2]
        for rows in range(ROW_STEP, TR + 1, ROW_STEP):
            pl.when(rows_needed == rows)(functools.partial(mlp, rows))


def _experts(meta, src_block, dst_block, stage, w_gate, w_up, w_down, n_tiles):
    any_spec = pl.BlockSpec(memory_space=pl.ANY)
    return pl.pallas_call(
        _expert_kernel,
        grid_spec=pltpu.PrefetchScalarGridSpec(
            num_scalar_prefetch=3,
            grid=(n_tiles,),
            in_specs=[any_spec, any_spec, any_spec, any_spec],
            out_specs=any_spec,
            scratch_shapes=[pltpu.VMEM((N_SLOTS, TR, D_MODEL), BF16),
                            pltpu.VMEM((N_SLOTS, TR, D_MODEL), BF16),
                            pltpu.VMEM((2, D_MODEL, D_EXPERT), BF16),
                            pltpu.VMEM((2, D_MODEL, D_EXPERT), BF16),
                            pltpu.VMEM((2, D_EXPERT, D_MODEL), BF16),
                            pltpu.SMEM((1,), I32),
                            pltpu.SemaphoreType.DMA((N_SLOTS,)),
                            pltpu.SemaphoreType.DMA((N_SLOTS,)),
                            pltpu.SemaphoreType.DMA((3,))]),
        out_shape=jax.ShapeDtypeStruct(stage.shape, stage.dtype),
        input_output_aliases={3: 0},
        compiler_params=pltpu.CompilerParams(dimension_semantics=("arbitrary",),
                                             vmem_limit_bytes=VMEM_LIMIT),
        name="experts",
    )(meta, src_block, dst_block, stage, w_gate, w_up, w_down)


def _combine_kernel(used_ref, x1_ref, route_ref, p_ref, gple_ref, wpg_f32, wpp_f32, gfin_ref,
                    srt_ref, out_ref, y_s, wpg_ref, wpp_ref):
    i = pl.program_id(0)

    @pl.when(i == 0)
    def _init():
        wpg_ref[...] = wpg_f32[...].astype(BF16)
        wpp_ref[...] = wpp_f32[...].astype(BF16)

    route = route_ref[...]
    w1, w2, pos1, pos2 = (route[:, k:k + 1] for k in (2, 3, 4, 5))

    def unsort(lo, hi):
        col = (lax.broadcasted_iota(I32, (TM, hi - lo), 1) + lo).astype(F32)
        mat = jnp.where(col == pos1, w1, jnp.where(col == pos2, w2, 0.0)).astype(BF16)
        return jnp.dot(mat, srt_ref[lo:hi, :], preferred_element_type=F32)

    def finish(y):
        pe = jnp.dot(p_ref[...].astype(BF16), wpp_ref[...], preferred_element_type=F32)
        hrows = TM // 2
        for h in range(2):
            rs = slice(h * hrows, (h + 1) * hrows)
            x2 = x1_ref[rs, :] + y[rs, :]
            hg = _rms(x2, gple_ref[...]).astype(BF16)
            gate = jax.nn.sigmoid(jnp.dot(hg, wpg_ref[...], preferred_element_type=F32))
            x3 = x2 + gate * pe[rs, :]
            out_ref[rs, :] = _rms(x3, gfin_ref[...])

    @pl.when(used_ref[i] <= SORT_MAIN)
    def _common():
        finish(unsort(0, SORT_MAIN))

    @pl.when(used_ref[i] > SORT_MAIN)
    def _with_tails():
        y_s[...] = unsort(0, SORT_MAIN)
        for lo in range(SORT_MAIN, KOUT, TAIL):
            @pl.when(used_ref[i] > lo)
            def _tail(lo=lo):
                y_s[...] += unsort(lo, lo + TAIL)
        finish(y_s[...])


def _combine(used, x1, route, p2d, g_ple, w_pg, w_pp, g_fin, sorted_out):
    t = p2d.shape[0]
    full = lambda a: pl.BlockSpec(a.shape, lambda i, *_: (0,) * a.ndim)
    return pl.pallas_call(
        _combine_kernel,
        grid_spec=pltpu.PrefetchScalarGridSpec(
            num_scalar_prefetch=1,
            grid=(t // TM,),
            in_specs=[pl.BlockSpec((TM, D_MODEL), lambda i, *_: (i, 0)),
                      pl.BlockSpec((TM, LANES), lambda i, *_: (i, 0)),
                      pl.BlockSpec((TM, PLE_DIM), lambda i, *_: (i, 0)),
                      full(g_ple), full(w_pg), full(w_pp), full(g_fin),
                      pl.BlockSpec((KOUT, D_MODEL), lambda i, *_: (i, 0))],
            out_specs=pl.BlockSpec((TM, D_MODEL), lambda i, *_: (i, 0)),
            scratch_shapes=[pltpu.VMEM((TM, D_MODEL), F32),
                            pltpu.VMEM(w_pg.shape, BF16),
                            pltpu.VMEM(w_pp.shape, BF16)]),
        out_shape=jax.ShapeDtypeStruct((t, D_MODEL), F32),
        compiler_params=pltpu.CompilerParams(dimension_semantics=("arbitrary",),
                                             vmem_limit_bytes=VMEM_LIMIT),
        name="combine_ple",
    )(used, x1, route, p2d, g_ple, w_pg, w_pp, g_fin, sorted_out)


def kernel(x, p, g_mix, w_in, w_conv, g_sgu, w_spatial, b_spatial, w_out, g_ffn, w_group, b_group,
           w_router, b_router, w_gate, w_up, w_down, g_ple, w_ple_gate, w_ple_proj, g_final):
    bsz, seq, d = x.shape
    t = bsz * seq
    assert w_in.shape[0] == 1, "single-layer block"
    assert d == D_MODEL and seq % TM == 0 and TM % CHUNK == 0
    nt = t // TM
    max_rows = TOP_K * t + nt * N_EXPERTS * (BLOCK - 1)
    n_tiles = (max_rows + N_EXPERTS * (TR - 1)) // TR + N_SLOTS
    l = 0

    w_rt = jnp.concatenate(
        [w_group[l], jnp.transpose(w_router[l], (1, 0, 2)).reshape(d, N_EXPERTS)], axis=1)
    w_rt = jnp.pad(w_rt, ((0, 0), (0, LANES - w_rt.shape[1]))).T
    b_rt = jnp.pad(jnp.concatenate([b_group[l], b_router[l].reshape(-1)]),
                   (0, LANES - N_GROUPS - N_EXPERTS)).reshape(LANES, 1)
    b_sp = jnp.repeat(b_spatial[l].T, SGU_WIDTH // SGU_HEADS, axis=1)

    x1, route, stage, meta, src, dst, used, wg_b, wu_b, wd_b = _mixer_router(
        x.reshape(t, d), g_mix[l].reshape(1, d), w_in[l], w_conv[l],
        g_sgu[l].reshape(1, -1), w_spatial[l], b_sp, w_out[l],
        g_ffn[l].reshape(1, d), w_rt, b_rt,
        w_gate[l].reshape(N_EXPERTS, d, D_EXPERT), w_up[l].reshape(N_EXPERTS, d, D_EXPERT),
        w_down[l].reshape(N_EXPERTS, D_EXPERT, d), seq, n_tiles)

    bpt = BLOCKS_PER_ROW_TILE
    meta, src, dst, used = (meta[:, :META_W].reshape(-1), src[:, :bpt].reshape(-1),
                            dst[:, :bpt].reshape(-1), used[:, 0])
    sorted_out = _experts(meta, src, dst, stage, wg_b, wu_b, wd_b, n_tiles)
    out = _combine(used, x1, route, p[l].reshape(t, PLE_DIM), g_ple[l].reshape(1, d),
                   w_ple_gate[l], w_ple_proj[l], g_final.reshape(1, d), sorted_out)
    return out.reshape(bsz, seq, d)
```

```python
import functools

import jax
import jax.numpy as jnp
from jax import lax
from jax.experimental import pallas as pl
from jax.experimental.pallas import tpu as pltpu

F32 = jnp.float32
BF16 = jnp.bfloat16
I32 = jnp.int32

EPS = 1e-6
D_MODEL = 1024
CONV_WIDTH = 512
SGU_WIDTH = 512
SGU_HEADS = 8
HEAD_PAIRS = SGU_HEADS // 2
CHUNK = 128
N_GROUPS = 4
EXPERTS_PER_GROUP = 8
N_EXPERTS = N_GROUPS * EXPERTS_PER_GROUP
TOP_K = 2
D_EXPERT = 512
PLE_DIM = 256
LANES = 128
SUBLANES = 8
BLOCK = 2 * SUBLANES
ROUTE_LANE0 = N_GROUPS
ROUTE_ROWS = 40

TM = 512
TR = 640
BLOCKS_PER_ROW_TILE = TR // BLOCK
KOUT = TOP_K * TM + N_EXPERTS * BLOCK
TAIL = 128
SORT_MAIN = KOUT - 2 * TAIL
META_W = 4
ROW_STEP = 128
N_SLOTS = 3
DUMP_SETS = 2
DUMP_BLOCKS = DUMP_SETS * BLOCKS_PER_ROW_TILE
assert DUMP_BLOCKS * BLOCK <= KOUT
WEIGHT_DMA_PRIORITY = 1
VMEM_LIMIT = 56 * 1024 * 1024


def _rms(x, g):
    return x * lax.rsqrt(jnp.mean(x * x, axis=-1, keepdims=True) + EPS) * g


def _build_tables(n_s, nt, meta_ref, src_ref, dst_ref, used_ref):
    tp = meta_ref.shape[0]
    bpt = BLOCKS_PER_ROW_TILE
    kb = KOUT // BLOCK
    n = n_s[0:nt, :]
    lane_n = lax.broadcasted_iota(I32, (nt, LANES), 1)
    is_e = (lane_n >= ROUTE_LANE0) & (lane_n < ROUTE_LANE0 + N_EXPERTS)
    nblk = jnp.where(is_e, jnp.floor((n + (BLOCK - 1)) * (1.0 / BLOCK)), 0.0)
    er = lax.broadcasted_iota(I32, (LANES, LANES), 0)
    ec = lax.broadcasted_iota(I32, (LANES, LANES), 1)
    loc = jnp.dot(nblk.astype(BF16), (er < ec).astype(BF16), preferred_element_type=F32)
    ti = lax.broadcasted_iota(I32, (nt, nt), 0)
    tj = lax.broadcasted_iota(I32, (nt, nt), 1)
    carry = jnp.dot((tj < ti).astype(BF16), nblk.astype(BF16), preferred_element_type=F32)
    total = jnp.sum(nblk, axis=0, keepdims=True)
    tiles_e = jnp.floor((total + (bpt - 1)) * (1.0 / bpt))
    tile_end = jnp.dot(jnp.broadcast_to(tiles_e, (SUBLANES, LANES)).astype(BF16),
                       (er <= ec).astype(BF16), preferred_element_type=F32)[0:1, :]
    tile0 = tile_end - tiles_e
    n_used = jnp.max(tile_end, axis=1, keepdims=True)

    tau = lax.broadcasted_iota(I32, (tp, LANES), 0).astype(F32)
    lane_i = lax.broadcasted_iota(I32, (tp, LANES), 1)
    lane_t = lane_i.astype(F32)
    is_e_t = (lane_i >= ROUTE_LANE0) & (lane_i < ROUTE_LANE0 + N_EXPERTS)
    te = jnp.sum(jnp.where(is_e_t & (tile_end <= jnp.minimum(tau, n_used - 1.0)), 1.0, 0.0),
                 axis=1, keepdims=True)
    oh = lane_t == te + ROUTE_LANE0
    pick = lambda row: jnp.sum(jnp.where(oh, row, 0.0), axis=1, keepdims=True)
    seg0_t = pick(tile0) * bpt
    e_lane = lane_t - ROUTE_LANE0
    cand = jnp.where(is_e_t & (e_lane > te) & (tiles_e > 0.0), e_lane, 99.0)
    nxt = jnp.min(cand, axis=1, keepdims=True)
    nxt = jnp.where(nxt == 99.0, -1.0, nxt)
    rows_t = jnp.clip(pick(total) * BLOCK - (tau[:, 0:1] - pick(tile0)) * TR, ROW_STEP, TR)
    rows_t = jnp.ceil(rows_t * (1.0 / ROW_STEP)) * ROW_STEP
    meta = jnp.where(lane_i == 0, te, 0.0)
    meta = jnp.where(lane_i == 1, nxt, meta)
    meta = jnp.where(lane_i == 2, rows_t, meta)
    meta = jnp.where(lane_i == 3, n_used, meta)
    meta_ref[...] = meta.astype(I32)

    bidx = tau * bpt + lane_t
    off = bidx - seg0_t
    ohb = jnp.where(oh, 1.0, 0.0).astype(BF16)

    def per_tile(tab):
        hi = jnp.floor(tab * (1.0 / 32))
        lo = tab - 32.0 * hi
        nt_dot = lambda a: lax.dot_general(ohb, a.astype(BF16), (((1,), (1,)), ((), ())),
                                           preferred_element_type=F32)
        return 32.0 * nt_dot(hi) + nt_dot(lo)

    run_end = per_tile(carry + nblk)
    run_beg = per_tile(carry)
    run_loc = per_tile(loc)
    tile_of = jnp.zeros((tp, LANES), F32)
    for it in range(nt):
        tile_of = tile_of + jnp.where(run_end[:, it:it + 1] <= off, 1.0, 0.0)
    srcv = off
    for it in range(nt):
        srcv = srcv + jnp.where(tile_of == it,
                                run_loc[:, it:it + 1] - run_beg[:, it:it + 1] + it * kb, 0.0)
    valid = (tile_of < nt) & (bidx < n_used * bpt)
    dump_set = tau - DUMP_SETS * jnp.floor((tau + 0.5) * (1.0 / DUMP_SETS))
    dump = nt * kb + dump_set * bpt + lane_t
    src_ref[...] = jnp.where(valid, srcv, kb - 1.0).astype(I32)
    dst_ref[0:tp, :] = jnp.where(valid, srcv, dump).astype(I32)
    dst_ref[tp:tp + SUBLANES, :] = (nt * kb + DUMP_BLOCKS - bpt
                                    + lax.broadcasted_iota(I32, (SUBLANES, LANES), 1))
    used_ref[...] = jnp.broadcast_to(jnp.sum(nblk, axis=1, keepdims=True) * BLOCK,
                                     (nt, LANES)).astype(I32)


def _mixer_router_kernel(tiles_per_seq, nt,
                         x_ref, gmix_ref, win_f32, wconv_ref, gsgu_ref, wsp_ref, bsp_ref,
                         wout_f32, gffn_ref, wrt_f32, brt_ref, wg_hbm, wu_hbm, wd_hbm,
                         x1_ref, route_ref, stage_ref, meta_ref, src_ref, dst_ref, used_ref,
                         wgb_hbm, wub_hbm, wdb_hbm,
                         wcat_s, tri_s, halo_s, mix_s, h2_s, win_ref, wout_ref, wrt_ref, n_s,
                         wg_f, wu_f, wd_f, wg_b, wu_b, wd_b, wisem, wosem):
    i = pl.program_id(0)
    nch = TM // CHUNK

    @pl.when(i == 0)
    def _init():
        r = lax.broadcasted_iota(I32, (CHUNK, CHUNK), 0)
        c = lax.broadcasted_iota(I32, (CHUNK, CHUNK), 1)
        causal = c <= r
        for j in range(HEAD_PAIRS):
            wa = jnp.where(causal, wsp_ref[2 * j], 0.0)
            wb = jnp.where(causal, wsp_ref[2 * j + 1], 0.0)
            wcat_s[j] = jnp.concatenate([wa, wb], axis=1).astype(BF16)
        rr = lax.broadcasted_iota(I32, (TM, TM), 0)
        cc = lax.broadcasted_iota(I32, (TM, TM), 1)
        tri_s[...] = (rr < cc).astype(BF16)
        h2_s[...] = jnp.zeros_like(h2_s)
        win_ref[...] = win_f32[...].astype(BF16)
        wout_ref[...] = wout_f32[...].astype(BF16)
        wrt_ref[...] = wrt_f32[...].astype(BF16)

    @pl.when(i % tiles_per_seq == 0)
    def _seq_start():
        halo_s[...] = jnp.zeros_like(halo_s)

    def weights_in(e):
        return (pltpu.make_async_copy(wg_hbm.at[e], wg_f, wisem.at[0]),
                pltpu.make_async_copy(wu_hbm.at[e], wu_f, wisem.at[1]),
                pltpu.make_async_copy(wd_hbm.at[e], wd_f, wisem.at[2]))

    def weights_out(e):
        return (pltpu.make_async_copy(wg_b, wgb_hbm.at[e], wosem.at[0]),
                pltpu.make_async_copy(wu_b, wub_hbm.at[e], wosem.at[1]),
                pltpu.make_async_copy(wd_b, wdb_hbm.at[e], wosem.at[2]))

    @pl.when(i == 0)
    def _first_weights():
        for cp in weights_in(0):
            cp.start(priority=WEIGHT_DMA_PRIORITY)

    experts_per_step = -(-N_EXPERTS // (nt + 1))
    for k in range(experts_per_step):
        e = i * experts_per_step + k

        @pl.when(e < N_EXPERTS)
        def _cast_expert(e=e):
            for cp in weights_in(e):
                cp.wait()

            @pl.when(e > 0)
            def _previous_written():
                for cp in weights_out(e - 1):
                    cp.wait()

            wg_b[...] = wg_f[...].astype(BF16)
            wu_b[...] = wu_f[...].astype(BF16)
            wd_b[...] = wd_f[...].astype(BF16)
            for cp in weights_out(e):
                cp.start(priority=WEIGHT_DMA_PRIORITY)

            @pl.when(e + 1 < N_EXPERTS)
            def _next_in():
                for cp in weights_in(e + 1):
                    cp.start(priority=WEIGHT_DMA_PRIORITY)

    @pl.when(i == nt)
    def _last_written():
        for cp in weights_out(N_EXPERTS - 1):
            cp.wait()

    def step(mix):
        h2b = h2_s[(i + 1) % 2]
        if mix:
            x = x_ref[...]
            hb = _rms(x, gmix_ref[...]).astype(BF16)

            def proj(k):
                return jnp.dot(hb, win_ref[:, k * 512:(k + 1) * 512], preferred_element_type=F32)

        logits = lax.dot_general(wrt_ref[...], h2b, (((1,), (1,)), ((), ())),
                                 preferred_element_type=F32)[0:ROUTE_ROWS, :] + brt_ref[0:ROUTE_ROWS, :]
        if mix:
            pu = proj(3)
            pv = proj(4)

        ridx = lax.broadcasted_iota(I32, (ROUTE_ROWS, TM), 0).astype(F32)
        neg = jnp.float32(-jnp.inf)
        big = jnp.float32(1e9)
        is_g = ridx < N_GROUPS
        gl = jnp.where(is_g, logits, neg)
        gmax = jnp.max(gl, axis=0, keepdims=True)
        gsum = jnp.sum(jnp.where(is_g, jnp.exp(gl - gmax), 0.0), axis=0, keepdims=True)
        g_w = 1.0 / gsum
        g_idx = jnp.min(jnp.where(gl == gmax, ridx, big), axis=0, keepdims=True)
        lo_row = ROUTE_LANE0 + EXPERTS_PER_GROUP * g_idx
        in_grp = (ridx >= lo_row) & (ridx < lo_row + EXPERTS_PER_GROUP)
        el = jnp.where(in_grp, logits, neg)
        v1 = jnp.max(el, axis=0, keepdims=True)
        i1 = jnp.min(jnp.where(el == v1, ridx, big), axis=0, keepdims=True)
        el2 = jnp.where(ridx == i1, neg, el)
        v2 = jnp.max(el2, axis=0, keepdims=True)
        i2 = jnp.min(jnp.where(el2 == v2, ridx, big), axis=0, keepdims=True)
        e21 = jnp.exp(v2 - v1)
        w1 = g_w / (1.0 + e21)
        w2 = g_w * e21 / (1.0 + e21)

        if mix:
            u = jax.nn.gelu(pu)
            zc = proj(1) * proj(2)
            v = jax.nn.gelu(pv)
            vc = v - jnp.mean(v, axis=-1, keepdims=True)
            vn = vc * lax.rsqrt(jnp.mean(vc * vc, axis=-1, keepdims=True) + EPS) * gsgu_ref[...]
            vnb = vn.astype(BF16)

            row = lax.broadcasted_iota(I32, (TM, CONV_WIDTH), 0)
            h6 = halo_s[6:7, :]
            h7 = halo_s[7:8, :]
            z1 = jnp.where(row == 0, h7, pltpu.roll(zc, 1, 0))
            z2 = jnp.where(row == 0, h6, jnp.where(row == 1, h7, pltpu.roll(zc, 2, 0)))
            conv = z2 * wconv_ref[0:1, :] + z1 * wconv_ref[1:2, :] + zc * wconv_ref[2:3, :]
            halo_s[...] = zc[TM - 8:TM, :]
            mix_s[:, 0:CONV_WIDTH] = (proj(0) * conv).astype(BF16)

        sel1 = ridx == i1
        sel2 = ridx == i2
        onehot = jnp.where(sel1 | sel2, 1.0, 0.0)
        counts = jnp.sum(onehot, axis=1, keepdims=True)
        pad_rows = jnp.zeros((LANES - ROUTE_ROWS, LANES), F32)
        counts_sq = jnp.concatenate([jnp.broadcast_to(counts, (ROUTE_ROWS, LANES)), pad_rows], axis=0)
        blocks_sq = jnp.ceil(counts_sq * (1.0 / BLOCK))
        er = lax.broadcasted_iota(I32, (LANES, LANES), 0)
        ec = lax.broadcasted_iota(I32, (LANES, LANES), 1)
        before = (ec < er).astype(BF16)
        run_start = BLOCK * jnp.dot(before, blocks_sq.astype(BF16),
                                    preferred_element_type=F32)[0:ROUTE_ROWS, 0:1]
        rank = jnp.dot(onehot.astype(BF16), tri_s[...], preferred_element_type=F32) + run_start
        pos1_row = jnp.sum(jnp.where(sel1, rank, 0.0), axis=0, keepdims=True)
        pos2_row = jnp.sum(jnp.where(sel2, rank, 0.0), axis=0, keepdims=True)
        n_s[pl.ds(jnp.where(i == 0, nt, i - 1), 1), :] = counts_sq.T[0:1, :]

        r8 = lax.broadcasted_iota(I32, (SUBLANES, TM), 0)
        rows8 = jnp.where(r8 == 0, i1 - ROUTE_LANE0, 0.0)
        rows8 = jnp.where(r8 == 1, i2 - ROUTE_LANE0, rows8)
        rows8 = jnp.where(r8 == 2, w1, rows8)
        rows8 = jnp.where(r8 == 3, w2, rows8)
        rows8 = jnp.where(r8 == 4, pos1_row, rows8)
        rows8 = jnp.where(r8 == 5, pos2_row, rows8)
        route_ref[...] = jnp.concatenate(
            [rows8, jnp.zeros((LANES - SUBLANES, TM), F32)], axis=0).T

        def sort_rows(lo, hi):
            out_row = (lax.broadcasted_iota(I32, (hi - lo, TM), 0) + lo).astype(F32)
            sort_mat = jnp.where((out_row == pos1_row) | (out_row == pos2_row), 1.0, 0.0).astype(BF16)
            stage_ref[lo:hi, :] = jnp.dot(sort_mat, h2b, preferred_element_type=F32).astype(BF16)

        sort_rows(0, SORT_MAIN // 2)
        if mix:
            left = lax.broadcasted_iota(I32, (CHUNK, LANES), 1) < (LANES // 2)
            zero = jnp.zeros((CHUNK, LANES), BF16)
            for j in range(HEAD_PAIRS):
                cols = []
                for c in range(nch):
                    s = vnb[c * CHUNK:(c + 1) * CHUNK, j * LANES:(j + 1) * LANES]
                    cols.append(jnp.concatenate([jnp.where(left, s, zero), jnp.where(left, zero, s)],
                                                axis=0))
                rhs = jnp.concatenate(cols, axis=1)
                res = jnp.dot(wcat_s[j], rhs, preferred_element_type=F32)
                bias = bsp_ref[:, j * LANES:(j + 1) * LANES]
                for c in range(nch):
                    mixed = res[:, c * LANES:(c + 1) * LANES] + bias
                    uu = u[c * CHUNK:(c + 1) * CHUNK, j * LANES:(j + 1) * LANES]
                    mix_s[c * CHUNK:(c + 1) * CHUNK,
                          CONV_WIDTH + j * LANES:CONV_WIDTH + (j + 1) * LANES] = (uu * mixed).astype(BF16)

        sort_rows(SORT_MAIN // 2, SORT_MAIN)
        if mix:
            x1 = x + jnp.dot(mix_s[...], wout_ref[...], preferred_element_type=F32)
            x1_ref[...] = x1
            h2_s[i % 2] = _rms(x1, gffn_ref[...]).astype(BF16)

        used_rows = BLOCK * jnp.sum(blocks_sq[:, 0:1])
        for lo in range(SORT_MAIN, KOUT, TAIL):
            @pl.when(used_rows > lo)
            def _tail(lo=lo):
                sort_rows(lo, lo + TAIL)

            @pl.when(used_rows <= lo)
            def _empty_tail(lo=lo):
                stage_ref[lo:lo + TAIL, :] = jnp.zeros((TAIL, D_MODEL), BF16)

    pl.when(i < nt)(functools.partial(step, True))

    @pl.when(i == nt)
    def _last():
        step(False)
        _build_tables(n_s, nt, meta_ref, src_ref, dst_ref, used_ref)


def _mixer_router(x2d, g_mix, w_in, w_conv, g_sgu, w_sp, b_sp, w_out, g_ffn, w_rt, b_rt,
                  w_gate, w_up, w_down, seq, n_tiles):
    t = x2d.shape[0]
    nt = t // TM
    tp = -(-n_tiles // SUBLANES) * SUBLANES
    whole = lambda r: pl.BlockSpec((r, LANES), lambda i: (0, 0))
    full = lambda a: pl.BlockSpec(a.shape, lambda i: (0,) * a.ndim, pipeline_mode=pl.Buffered(1))
    any_spec = pl.BlockSpec(memory_space=pl.ANY)
    ins = (g_mix, w_in, w_conv, g_sgu, w_sp, b_sp, w_out, g_ffn, w_rt, b_rt)
    expert_w = (w_gate, w_up, w_down)
    prev = lambda i: (jnp.where(i == 0, nt, i - 1), 0)
    return pl.pallas_call(
        functools.partial(_mixer_router_kernel, seq // TM, nt),
        grid=(nt + 1,),
        in_specs=[pl.BlockSpec((TM, D_MODEL), lambda i: (jnp.minimum(i, nt - 1), 0))]
        + [full(a) for a in ins] + [any_spec] * 3,
        out_specs=[pl.BlockSpec((TM, D_MODEL), lambda i: (jnp.minimum(i, nt - 1), 0)),
                   pl.BlockSpec((TM, LANES), prev),
                   pl.BlockSpec((KOUT, D_MODEL), prev),
                   whole(tp), whole(tp), whole(tp + SUBLANES), whole(nt)] + [any_spec] * 3,
        out_shape=[jax.ShapeDtypeStruct((t, D_MODEL), F32),
                   jax.ShapeDtypeStruct(((nt + 1) * TM, LANES), F32),
                   jax.ShapeDtypeStruct(((nt + 1) * KOUT, D_MODEL), BF16),
                   jax.ShapeDtypeStruct((tp, LANES), I32),
                   jax.ShapeDtypeStruct((tp, LANES), I32),
                   jax.ShapeDtypeStruct((tp + SUBLANES, LANES), I32),
                   jax.ShapeDtypeStruct((nt, LANES), I32)]
        + [jax.ShapeDtypeStruct(w.shape, BF16) for w in expert_w],
        scratch_shapes=[pltpu.VMEM((HEAD_PAIRS, CHUNK, 2 * CHUNK), BF16),
                        pltpu.VMEM((TM, TM), BF16),
                        pltpu.VMEM((8, CONV_WIDTH), F32),
                        pltpu.VMEM((TM, D_MODEL), BF16),
                        pltpu.VMEM((2, TM, D_MODEL), BF16),
                        pltpu.VMEM(w_in.shape, BF16),
                        pltpu.VMEM(w_out.shape, BF16),
                        pltpu.VMEM(w_rt.shape, BF16),
                        pltpu.VMEM((nt + SUBLANES, LANES), F32)]
        + [pltpu.VMEM(w.shape[1:], F32) for w in expert_w]
        + [pltpu.VMEM(w.shape[1:], BF16) for w in expert_w]
        + [pltpu.SemaphoreType.DMA((3,)), pltpu.SemaphoreType.DMA((3,))],
        compiler_params=pltpu.CompilerParams(dimension_semantics=("arbitrary",),
                                             vmem_limit_bytes=VMEM_LIMIT),
        name="mixer_router",
    )(x2d, *ins, *expert_w)


def _expert_kernel(meta_ref, src_ref, dst_ref, stage_ref,
                   wg_hbm, wu_hbm, wd_hbm, out_ref,
                   xbuf, obuf, wg_s, wu_s, wd_s, par_ref, sem, osem, wsem):
    i = pl.program_id(0)
    n_used = meta_ref[3]
    slot = i % N_SLOTS
    bpt = BLOCKS_PER_ROW_TILE
    dump_row = dst_ref.shape[0] // bpt - SUBLANES
    expert_of = lambda tile: meta_ref[tile * META_W]

    def gather_copy(seq, j):
        tile = jnp.minimum(seq, n_used - 1)
        src = pl.multiple_of(src_ref[tile * bpt + j] * BLOCK, BLOCK)
        slot_ = seq % N_SLOTS
        return pltpu.make_async_copy(stage_ref.at[pl.ds(src, BLOCK)],
                                     xbuf.at[slot_, pl.ds(j * BLOCK, BLOCK)], sem.at[slot_])

    def scatter_copy(tile, j):
        row = jnp.where(tile < 0, dump_row, tile)
        dst = pl.multiple_of(dst_ref[row * bpt + j] * BLOCK, BLOCK)
        slot_ = (tile + N_SLOTS) % N_SLOTS
        return pltpu.make_async_copy(obuf.at[slot_, pl.ds(j * BLOCK, BLOCK)],
                                     out_ref.at[pl.ds(dst, BLOCK)], osem.at[slot_])

    def each_block(fn):
        for j in range(BLOCKS_PER_ROW_TILE):
            fn(j)

    def weight_copies(e, p):
        return (pltpu.make_async_copy(wg_hbm.at[e], wg_s.at[p], wsem.at[0]),
                pltpu.make_async_copy(wu_hbm.at[e], wu_s.at[p], wsem.at[1]),
                pltpu.make_async_copy(wd_hbm.at[e], wd_s.at[p], wsem.at[2]))

    @pl.when(i == 0)
    def _first():
        each_block(lambda j: gather_copy(0, j).start())
        each_block(lambda j: gather_copy(1, j).start())
        par_ref[0] = 0
        for cp in weight_copies(expert_of(0), 0):
            cp.start(priority=WEIGHT_DMA_PRIORITY)
        obuf[...] = jnp.zeros_like(obuf)

    @pl.when((i >= 2) & (i - 3 < n_used))
    def _free_out_slot():
        each_block(lambda j: scatter_copy(i - 3, j).wait())

    @pl.when(i == n_used)
    def _after_last_tile():
        each_block(lambda j: gather_copy(n_used, j).wait())
        each_block(lambda j: gather_copy(n_used + 1, j).wait())
        each_block(lambda j: scatter_copy(n_used - 1, j).start())

    @pl.when(i < n_used)
    def _tile():
        e = expert_of(i)
        prev = expert_of(jnp.maximum(i - 1, 0))

        @pl.when((i == 0) | (e != prev))
        def _new_expert():
            p = jnp.where(i == 0, 0, 1 - par_ref[0])
            for cp in weight_copies(e, p):
                cp.wait()
            par_ref[0] = p
            nxt = meta_ref[i * META_W + 1]

            @pl.when(nxt >= 0)
            def _next_weights():
                for cp in weight_copies(nxt, 1 - p):
                    cp.start(priority=WEIGHT_DMA_PRIORITY)

        each_block(lambda j: gather_copy(i, j).wait())

        def mlp(rows):
            p = par_ref[0]
            hb = xbuf[slot, 0:rows, :]
            a = jnp.dot(hb, wg_s[p], preferred_element_type=F32)
            b = jnp.dot(hb, wu_s[p], preferred_element_type=F32)
            each_block(lambda j: gather_copy(i + 2, j).start())
            each_block(lambda j: scatter_copy(i - 1, j).start())
            hid = (a * jax.nn.sigmoid(a) * b).astype(BF16)
            obuf[slot, 0:rows, :] = jnp.dot(hid, wd_s[p], preferred_element_type=F32).astype(BF16)

        rows_needed = meta_ref[i * META_W + 2]
        for rows in range(ROW_STEP, TR + 1, ROW_STEP):
            pl.when(rows_needed == rows)(functools.partial(mlp, rows))


def _experts(meta, src_block, dst_block, stage, w_gate, w_up, w_down, n_tiles):
    any_spec = pl.BlockSpec(memory_space=pl.ANY)
    return pl.pallas_call(
        _expert_kernel,
        grid_spec=pltpu.PrefetchScalarGridSpec(
            num_scalar_prefetch=3,
            grid=(n_tiles,),
            in_specs=[any_spec, any_spec, any_spec, any_spec],
            out_specs=any_spec,
            scratch_shapes=[pltpu.VMEM((N_SLOTS, TR, D_MODEL), BF16),
                            pltpu.VMEM((N_SLOTS, TR, D_MODEL), BF16),
                            pltpu.VMEM((2, D_MODEL, D_EXPERT), BF16),
                            pltpu.VMEM((2, D_MODEL, D_EXPERT), BF16),
                            pltpu.VMEM((2, D_EXPERT, D_MODEL), BF16),
                            pltpu.SMEM((1,), I32),
                            pltpu.SemaphoreType.DMA((N_SLOTS,)),
                            pltpu.SemaphoreType.DMA((N_SLOTS,)),
                            pltpu.SemaphoreType.DMA((3,))]),
        out_shape=jax.ShapeDtypeStruct(stage.shape, stage.dtype),
        input_output_aliases={3: 0},
        compiler_params=pltpu.CompilerParams(dimension_semantics=("arbitrary",),
                                             vmem_limit_bytes=VMEM_LIMIT),
        name="experts",
    )(meta, src_block, dst_block, stage, w_gate, w_up, w_down)


def _combine_kernel(used_ref, x1_ref, route_ref, p_ref, gple_ref, wpg_f32, wpp_f32, gfin_ref,
                    srt_ref, out_ref, y_s, wpg_ref, wpp_ref):
    i = pl.program_id(0)

    @pl.when(i == 0)
    def _init():
        wpg_ref[...] = wpg_f32[...].astype(BF16)
        wpp_ref[...] = wpp_f32[...].astype(BF16)

    route = route_ref[...]
    w1, w2, pos1, pos2 = (route[:, k:k + 1] for k in (2, 3, 4, 5))

    def unsort(lo, hi):
        col = (lax.broadcasted_iota(I32, (TM, hi - lo), 1) + lo).astype(F32)
        mat = jnp.where(col == pos1, w1, jnp.where(col == pos2, w2, 0.0)).astype(BF16)
        return jnp.dot(mat, srt_ref[lo:hi, :], preferred_element_type=F32)

    def finish(y):
        pe = jnp.dot(p_ref[...].astype(BF16), wpp_ref[...], preferred_element_type=F32)
        hrows = TM // 2
        for h in range(2):
            rs = slice(h * hrows, (h + 1) * hrows)
            x2 = x1_ref[rs, :] + y[rs, :]
            hg = _rms(x2, gple_ref[...]).astype(BF16)
            gate = jax.nn.sigmoid(jnp.dot(hg, wpg_ref[...], preferred_element_type=F32))
            x3 = x2 + gate * pe[rs, :]
            out_ref[rs, :] = _rms(x3, gfin_ref[...])

    @pl.when(used_ref[i] <= SORT_MAIN)
    def _common():
        finish(unsort(0, SORT_MAIN))

    @pl.when(used_ref[i] > SORT_MAIN)
    def _with_tails():
        y_s[...] = unsort(0, SORT_MAIN)
        for lo in range(SORT_MAIN, KOUT, TAIL):
            @pl.when(used_ref[i] > lo)
            def _tail(lo=lo):
                y_s[...] += unsort(lo, lo + TAIL)
        finish(y_s[...])


def _combine(used, x1, route, p2d, g_ple, w_pg, w_pp, g_fin, sorted_out):
    t = p2d.shape[0]
    full = lambda a: pl.BlockSpec(a.shape, lambda i, *_: (0,) * a.ndim)
    return pl.pallas_call(
        _combine_kernel,
        grid_spec=pltpu.PrefetchScalarGridSpec(
            num_scalar_prefetch=1,
            grid=(t // TM,),
            in_specs=[pl.BlockSpec((TM, D_MODEL), lambda i, *_: (i, 0)),
                      pl.BlockSpec((TM, LANES), lambda i, *_: (i, 0)),
                      pl.BlockSpec((TM, PLE_DIM), lambda i, *_: (i, 0)),
                      full(g_ple), full(w_pg), full(w_pp), full(g_fin),
                      pl.BlockSpec((KOUT, D_MODEL), lambda i, *_: (i, 0))],
            out_specs=pl.BlockSpec((TM, D_MODEL), lambda i, *_: (i, 0)),
            scratch_shapes=[pltpu.VMEM((TM, D_MODEL), F32),
                            pltpu.VMEM(w_pg.shape, BF16),
                            pltpu.VMEM(w_pp.shape, BF16)]),
        out_shape=jax.ShapeDtypeStruct((t, D_MODEL), F32),
        compiler_params=pltpu.CompilerParams(dimension_semantics=("arbitrary",),
                                             vmem_limit_bytes=VMEM_LIMIT),
        name="combine_ple",
    )(used, x1, route, p2d, g_ple, w_pg, w_pp, g_fin, sorted_out)


def kernel(x, p, g_mix, w_in, w_conv, g_sgu, w_spatial, b_spatial, w_out, g_ffn, w_group, b_group,
           w_router, b_router, w_gate, w_up, w_down, g_ple, w_ple_gate, w_ple_proj, g_final):
    bsz, seq, d = x.shape
    t = bsz * seq
    assert w_in.shape[0] == 1, "single-layer block"
    assert d == D_MODEL and seq % TM == 0 and TM % CHUNK == 0
    nt = t // TM
    max_rows = TOP_K * t + nt * N_EXPERTS * (BLOCK - 1)
    n_tiles = (max_rows + N_EXPERTS * (TR - 1)) // TR + N_SLOTS
    l = 0

    w_rt = jnp.concatenate(
        [w_group[l], jnp.transpose(w_router[l], (1, 0, 2)).reshape(d, N_EXPERTS)], axis=1)
    w_rt = jnp.pad(w_rt, ((0, 0), (0, LANES - w_rt.shape[1]))).T
    b_rt = jnp.pad(jnp.concatenate([b_group[l], b_router[l].reshape(-1)]),
                   (0, LANES - N_GROUPS - N_EXPERTS)).reshape(LANES, 1)
    b_sp = jnp.repeat(b_spatial[l].T, SGU_WIDTH // SGU_HEADS, axis=1)

    x1, route, stage, meta, src, dst, used, wg_b, wu_b, wd_b = _mixer_router(
        x.reshape(t, d), g_mix[l].reshape(1, d), w_in[l], w_conv[l],
        g_sgu[l].reshape(1, -1), w_spatial[l], b_sp, w_out[l],
        g_ffn[l].reshape(1, d), w_rt, b_rt,
        w_gate[l].reshape(N_EXPERTS, d, D_EXPERT), w_up[l].reshape(N_EXPERTS, d, D_EXPERT),
        w_down[l].reshape(N_EXPERTS, D_EXPERT, d), seq, n_tiles)

    bpt = BLOCKS_PER_ROW_TILE
    meta, src, dst, used = (meta[:, :META_W].reshape(-1), src[:, :bpt].reshape(-1),
                            dst[:, :bpt].reshape(-1), used[:, 0])
    sorted_out = _experts(meta, src, dst, stage, wg_b, wu_b, wd_b, n_tiles)
    out = _combine(used, x1, route, p[l].reshape(t, PLE_DIM), g_ple[l].reshape(1, d),
                   w_ple_gate[l], w_ple_proj[l], g_final.reshape(1, d), sorted_out)
    return out.reshape(bsz, seq, d)
```

```python
import functools

import jax
import jax.numpy as jnp
from jax import lax
from jax.experimental import pallas as pl
from jax.experimental.pallas import tpu as pltpu

F32 = jnp.float32
BF16 = jnp.bfloat16
I32 = jnp.int32

EPS = 1e-6
D_MODEL = 1024
CONV_WIDTH = 512
SGU_WIDTH = 512
SGU_HEADS = 8
HEAD_PAIRS = SGU_HEADS // 2
CHUNK = 128
N_GROUPS = 4
EXPERTS_PER_GROUP = 8
N_EXPERTS = N_GROUPS * EXPERTS_PER_GROUP
TOP_K = 2
D_EXPERT = 512
PLE_DIM = 256
LANES = 128
SUBLANES = 8
BLOCK = 2 * SUBLANES
ROUTE_LANE0 = N_GROUPS
ROUTE_ROWS = 40

TM = 512
TR = 640
BLOCKS_PER_ROW_TILE = TR // BLOCK
KOUT = TOP_K * TM + N_EXPERTS * BLOCK
TAIL = 128
SORT_MAIN = KOUT - 2 * TAIL
META_W = 4
ROW_STEP = 128
N_SLOTS = 3
DUMP_SETS = 2
DUMP_BLOCKS = DUMP_SETS * BLOCKS_PER_ROW_TILE
assert DUMP_BLOCKS * BLOCK <= KOUT
WEIGHT_DMA_PRIORITY = 1
VMEM_LIMIT = 56 * 1024 * 1024


def _rms(x, g):
    return x * lax.rsqrt(jnp.mean(x * x, axis=-1, keepdims=True) + EPS) * g


def _build_tables(n_s, nt, meta_ref, src_ref, dst_ref, used_ref):
    tp = meta_ref.shape[0]
    bpt = BLOCKS_PER_ROW_TILE
    kb = KOUT // BLOCK
    n = n_s[0:nt, :]
    lane_n = lax.broadcasted_iota(I32, (nt, LANES), 1)
    is_e = (lane_n >= ROUTE_LANE0) & (lane_n < ROUTE_LANE0 + N_EXPERTS)
    nblk = jnp.where(is_e, jnp.floor((n + (BLOCK - 1)) * (1.0 / BLOCK)), 0.0)
    er = lax.broadcasted_iota(I32, (LANES, LANES), 0)
    ec = lax.broadcasted_iota(I32, (LANES, LANES), 1)
    loc = jnp.dot(nblk.astype(BF16), (er < ec).astype(BF16), preferred_element_type=F32)
    ti = lax.broadcasted_iota(I32, (nt, nt), 0)
    tj = lax.broadcasted_iota(I32, (nt, nt), 1)
    carry = jnp.dot((tj < ti).astype(BF16), nblk.astype(BF16), preferred_element_type=F32)
    total = jnp.sum(nblk, axis=0, keepdims=True)
    tiles_e = jnp.floor((total + (bpt - 1)) * (1.0 / bpt))
    tile_end = jnp.dot(jnp.broadcast_to(tiles_e, (SUBLANES, LANES)).astype(BF16),
                       (er <= ec).astype(BF16), preferred_element_type=F32)[0:1, :]
    tile0 = tile_end - tiles_e
    n_used = jnp.max(tile_end, axis=1, keepdims=True)

    tau = lax.broadcasted_iota(I32, (tp, LANES), 0).astype(F32)
    lane_i = lax.broadcasted_iota(I32, (tp, LANES), 1)
    lane_t = lane_i.astype(F32)
    is_e_t = (lane_i >= ROUTE_LANE0) & (lane_i < ROUTE_LANE0 + N_EXPERTS)
    te = jnp.sum(jnp.where(is_e_t & (tile_end <= jnp.minimum(tau, n_used - 1.0)), 1.0, 0.0),
                 axis=1, keepdims=True)
    oh = lane_t == te + ROUTE_LANE0
    pick = lambda row: jnp.sum(jnp.where(oh, row, 0.0), axis=1, keepdims=True)
    seg0_t = pick(tile0) * bpt
    e_lane = lane_t - ROUTE_LANE0
    cand = jnp.where(is_e_t & (e_lane > te) & (tiles_e > 0.0), e_lane, 99.0)
    nxt = jnp.min(cand, axis=1, keepdims=True)
    nxt = jnp.where(nxt == 99.0, -1.0, nxt)
    rows_t = jnp.clip(pick(total) * BLOCK - (tau[:, 0:1] - pick(tile0)) * TR, ROW_STEP, TR)
    rows_t = jnp.ceil(rows_t * (1.0 / ROW_STEP)) * ROW_STEP
    meta = jnp.where(lane_i == 0, te, 0.0)
    meta = jnp.where(lane_i == 1, nxt, meta)
    meta = jnp.where(lane_i == 2, rows_t, meta)
    meta = jnp.where(lane_i == 3, n_used, meta)
    meta_ref[...] = meta.astype(I32)

    bidx = tau * bpt + lane_t
    off = bidx - seg0_t
    ohb = jnp.where(oh, 1.0, 0.0).astype(BF16)

    def per_tile(tab):
        hi = jnp.floor(tab * (1.0 / 32))
        lo = tab - 32.0 * hi
        nt_dot = lambda a: lax.dot_general(ohb, a.astype(BF16), (((1,), (1,)), ((), ())),
                                           preferred_element_type=F32)
        return 32.0 * nt_dot(hi) + nt_dot(lo)

    run_end = per_tile(carry + nblk)
    run_beg = per_tile(carry)
    run_loc = per_tile(loc)
    tile_of = jnp.zeros((tp, LANES), F32)
    for it in range(nt):
        tile_of = tile_of + jnp.where(run_end[:, it:it + 1] <= off, 1.0, 0.0)
    srcv = off
    for it in range(nt):
        srcv = srcv + jnp.where(tile_of == it,
                                run_loc[:, it:it + 1] - run_beg[:, it:it + 1] + it * kb, 0.0)
    valid = (tile_of < nt) & (bidx < n_used * bpt)
    dump_set = tau - DUMP_SETS * jnp.floor((tau + 0.5) * (1.0 / DUMP_SETS))
    dump = nt * kb + dump_set * bpt + lane_t
    src_ref[...] = jnp.where(valid, srcv, kb - 1.0).astype(I32)
    dst_ref[0:tp, :] = jnp.where(valid, srcv, dump).astype(I32)
    dst_ref[tp:tp + SUBLANES, :] = (nt * kb + DUMP_BLOCKS - bpt
                                    + lax.broadcasted_iota(I32, (SUBLANES, LANES), 1))
    used_ref[...] = jnp.broadcast_to(jnp.sum(nblk, axis=1, keepdims=True) * BLOCK,
                                     (nt, LANES)).astype(I32)


def _mixer_router_kernel(tiles_per_seq, nt,
                         x_ref, gmix_ref, win_f32, wconv_ref, gsgu_ref, wsp_ref, bsp_ref,
                         wout_f32, gffn_ref, wrt_f32, brt_ref, wg_hbm, wu_hbm, wd_hbm,
                         x1_ref, route_ref, stage_ref, meta_ref, src_ref, dst_ref, used_ref,
                         wgb_hbm, wub_hbm, wdb_hbm,
                         wcat_s, tri_s, halo_s, mix_s, h2_s, win_ref, wout_ref, wrt_ref, n_s,
                         wg_f, wu_f, wd_f, wg_b, wu_b, wd_b, wisem, wosem):
    i = pl.program_id(0)
    nch = TM // CHUNK

    @pl.when(i == 0)
    def _init():
        r = lax.broadcasted_iota(I32, (CHUNK, CHUNK), 0)
        c = lax.broadcasted_iota(I32, (CHUNK, CHUNK), 1)
        causal = c <= r
        for j in range(HEAD_PAIRS):
            wa = jnp.where(causal, wsp_ref[2 * j], 0.0)
            wb = jnp.where(causal, wsp_ref[2 * j + 1], 0.0)
            wcat_s[j] = jnp.concatenate([wa, wb], axis=1).astype(BF16)
        rr = lax.broadcasted_iota(I32, (TM, TM), 0)
        cc = lax.broadcasted_iota(I32, (TM, TM), 1)
        tri_s[...] = (rr < cc).astype(BF16)
        h2_s[...] = jnp.zeros_like(h2_s)
        win_ref[...] = win_f32[...].astype(BF16)
        wout_ref[...] = wout_f32[...].astype(BF16)
        wrt_ref[...] = wrt_f32[...].astype(BF16)

    @pl.when(i % tiles_per_seq == 0)
    def _seq_start():
        halo_s[...] = jnp.zeros_like(halo_s)

    def weights_in(e):
        return (pltpu.make_async_copy(wg_hbm.at[e], wg_f, wisem.at[0]),
                pltpu.make_async_copy(wu_hbm.at[e], wu_f, wisem.at[1]),
                pltpu.make_async_copy(wd_hbm.at[e], wd_f, wisem.at[2]))

    def weights_out(e):
        return (pltpu.make_async_copy(wg_b, wgb_hbm.at[e], wosem.at[0]),
                pltpu.make_async_copy(wu_b, wub_hbm.at[e], wosem.at[1]),
                pltpu.make_async_copy(wd_b, wdb_hbm.at[e], wosem.at[2]))

    @pl.when(i == 0)
    def _first_weights():
        for cp in weights_in(0):
            cp.start(priority=WEIGHT_DMA_PRIORITY)

    experts_per_step = -(-N_EXPERTS // (nt + 1))
    for k in range(experts_per_step):
        e = i * experts_per_step + k

        @pl.when(e < N_EXPERTS)
        def _cast_expert(e=e):
            for cp in weights_in(e):
                cp.wait()

            @pl.when(e > 0)
            def _previous_written():
                for cp in weights_out(e - 1):
                    cp.wait()

            wg_b[...] = wg_f[...].astype(BF16)
            wu_b[...] = wu_f[...].astype(BF16)
            wd_b[...] = wd_f[...].astype(BF16)
            for cp in weights_out(e):
                cp.start()

            @pl.when(e + 1 < N_EXPERTS)
            def _next_in():
                for cp in weights_in(e + 1):
                    cp.start(priority=WEIGHT_DMA_PRIORITY)

    @pl.when(i == nt)
    def _last_written():
        for cp in weights_out(N_EXPERTS - 1):
            cp.wait()

    def step(mix):
        h2b = h2_s[(i + 1) % 2]
        if mix:
            x = x_ref[...]
            hb = _rms(x, gmix_ref[...]).astype(BF16)

            def proj(k):
                return jnp.dot(hb, win_ref[:, k * 512:(k + 1) * 512], preferred_element_type=F32)

        logits = lax.dot_general(wrt_ref[...], h2b, (((1,), (1,)), ((), ())),
                                 preferred_element_type=F32)[0:ROUTE_ROWS, :] + brt_ref[0:ROUTE_ROWS, :]
        if mix:
            pu = proj(3)
            pv = proj(4)

        ridx = lax.broadcasted_iota(I32, (ROUTE_ROWS, TM), 0).astype(F32)
        neg = jnp.float32(-jnp.inf)
        big = jnp.float32(1e9)
        is_g = ridx < N_GROUPS
        gl = jnp.where(is_g, logits, neg)
        gmax = jnp.max(gl, axis=0, keepdims=True)
        gsum = jnp.sum(jnp.where(is_g, jnp.exp(gl - gmax), 0.0), axis=0, keepdims=True)
        g_w = 1.0 / gsum
        g_idx = jnp.min(jnp.where(gl == gmax, ridx, big), axis=0, keepdims=True)
        lo_row = ROUTE_LANE0 + EXPERTS_PER_GROUP * g_idx
        in_grp = (ridx >= lo_row) & (ridx < lo_row + EXPERTS_PER_GROUP)
        el = jnp.where(in_grp, logits, neg)
        v1 = jnp.max(el, axis=0, keepdims=True)
        i1 = jnp.min(jnp.where(el == v1, ridx, big), axis=0, keepdims=True)
        el2 = jnp.where(ridx == i1, neg, el)
        v2 = jnp.max(el2, axis=0, keepdims=True)
        i2 = jnp.min(jnp.where(el2 == v2, ridx, big), axis=0, keepdims=True)
        e21 = jnp.exp(v2 - v1)
        w1 = g_w / (1.0 + e21)
        w2 = g_w * e21 / (1.0 + e21)

        if mix:
            u = jax.nn.gelu(pu)
            zc = proj(1) * proj(2)
            v = jax.nn.gelu(pv)
            vc = v - jnp.mean(v, axis=-1, keepdims=True)
            vn = vc * lax.rsqrt(jnp.mean(vc * vc, axis=-1, keepdims=True) + EPS) * gsgu_ref[...]
            vnb = vn.astype(BF16)

            row = lax.broadcasted_iota(I32, (TM, CONV_WIDTH), 0)
            h6 = halo_s[6:7, :]
            h7 = halo_s[7:8, :]
            z1 = jnp.where(row == 0, h7, pltpu.roll(zc, 1, 0))
            z2 = jnp.where(row == 0, h6, jnp.where(row == 1, h7, pltpu.roll(zc, 2, 0)))
            conv = z2 * wconv_ref[0:1, :] + z1 * wconv_ref[1:2, :] + zc * wconv_ref[2:3, :]
            halo_s[...] = zc[TM - 8:TM, :]
            mix_s[:, 0:CONV_WIDTH] = (proj(0) * conv).astype(BF16)

        sel1 = ridx == i1
        sel2 = ridx == i2
        onehot = jnp.where(sel1 | sel2, 1.0, 0.0)
        counts = jnp.sum(onehot, axis=1, keepdims=True)
        pad_rows = jnp.zeros((LANES - ROUTE_ROWS, LANES), F32)
        counts_sq = jnp.concatenate([jnp.broadcast_to(counts, (ROUTE_ROWS, LANES)), pad_rows], axis=0)
        blocks_sq = jnp.ceil(counts_sq * (1.0 / BLOCK))
        er = lax.broadcasted_iota(I32, (LANES, LANES), 0)
        ec = lax.broadcasted_iota(I32, (LANES, LANES), 1)
        before = (ec < er).astype(BF16)
        run_start = BLOCK * jnp.dot(before, blocks_sq.astype(BF16),
                                    preferred_element_type=F32)[0:ROUTE_ROWS, 0:1]
        rank = jnp.dot(onehot.astype(BF16), tri_s[...], preferred_element_type=F32) + run_start
        pos1_row = jnp.sum(jnp.where(sel1, rank, 0.0), axis=0, keepdims=True)
        pos2_row = jnp.sum(jnp.where(sel2, rank, 0.0), axis=0, keepdims=True)
        n_s[pl.ds(jnp.where(i == 0, nt, i - 1), 1), :] = counts_sq.T[0:1, :]

        r8 = lax.broadcasted_iota(I32, (SUBLANES, TM), 0)
        rows8 = jnp.where(r8 == 0, i1 - ROUTE_LANE0, 0.0)
        rows8 = jnp.where(r8 == 1, i2 - ROUTE_LANE0, rows8)
        rows8 = jnp.where(r8 == 2, w1, rows8)
        rows8 = jnp.where(r8 == 3, w2, rows8)
        rows8 = jnp.where(r8 == 4, pos1_row, rows8)
        rows8 = jnp.where(r8 == 5, pos2_row, rows8)
        route_ref[...] = jnp.concatenate(
            [rows8, jnp.zeros((LANES - SUBLANES, TM), F32)], axis=0).T

        def sort_rows(lo, hi):
            out_row = (lax.broadcasted_iota(I32, (hi - lo, TM), 0) + lo).astype(F32)
            sort_mat = jnp.where((out_row == pos1_row) | (out_row == pos2_row), 1.0, 0.0).astype(BF16)
            stage_ref[lo:hi, :] = jnp.dot(sort_mat, h2b, preferred_element_type=F32).astype(BF16)

        sort_rows(0, SORT_MAIN // 2)
        if mix:
            left = lax.broadcasted_iota(I32, (CHUNK, LANES), 1) < (LANES // 2)
            zero = jnp.zeros((CHUNK, LANES), BF16)
            for j in range(HEAD_PAIRS):
                cols = []
                for c in range(nch):
                    s = vnb[c * CHUNK:(c + 1) * CHUNK, j * LANES:(j + 1) * LANES]
                    cols.append(jnp.concatenate([jnp.where(left, s, zero), jnp.where(left, zero, s)],
                                                axis=0))
                rhs = jnp.concatenate(cols, axis=1)
                res = jnp.dot(wcat_s[j], rhs, preferred_element_type=F32)
                bias = bsp_ref[:, j * LANES:(j + 1) * LANES]
                for c in range(nch):
                    mixed = res[:, c * LANES:(c + 1) * LANES] + bias
                    uu = u[c * CHUNK:(c + 1) * CHUNK, j * LANES:(j + 1) * LANES]
                    mix_s[c * CHUNK:(c + 1) * CHUNK,
                          CONV_WIDTH + j * LANES:CONV_WIDTH + (j + 1) * LANES] = (uu * mixed).astype(BF16)

        sort_rows(SORT_MAIN // 2, SORT_MAIN)
        if mix:
            x1 = x + jnp.dot(mix_s[...], wout_ref[...], preferred_element_type=F32)
            x1_ref[...] = x1
            h2_s[i % 2] = _rms(x1, gffn_ref[...]).astype(BF16)

        used_rows = BLOCK * jnp.sum(blocks_sq[:, 0:1])
        for lo in range(SORT_MAIN, KOUT, TAIL):
            @pl.when(used_rows > lo)
            def _tail(lo=lo):
                sort_rows(lo, lo + TAIL)

            @pl.when(used_rows <= lo)
            def _empty_tail(lo=lo):
                stage_ref[lo:lo + TAIL, :] = jnp.zeros((TAIL, D_MODEL), BF16)

    pl.when(i < nt)(functools.partial(step, True))

    @pl.when(i == nt)
    def _last():
        step(False)
        _build_tables(n_s, nt, meta_ref, src_ref, dst_ref, used_ref)


def _mixer_router(x2d, g_mix, w_in, w_conv, g_sgu, w_sp, b_sp, w_out, g_ffn, w_rt, b_rt,
                  w_gate, w_up, w_down, seq, n_tiles):
    t = x2d.shape[0]
    nt = t // TM
    tp = -(-n_tiles // SUBLANES) * SUBLANES
    whole = lambda r: pl.BlockSpec((r, LANES), lambda i: (0, 0))
    full = lambda a: pl.BlockSpec(a.shape, lambda i: (0,) * a.ndim, pipeline_mode=pl.Buffered(1))
    any_spec = pl.BlockSpec(memory_space=pl.ANY)
    ins = (g_mix, w_in, w_conv, g_sgu, w_sp, b_sp, w_out, g_ffn, w_rt, b_rt)
    expert_w = (w_gate, w_up, w_down)
    prev = lambda i: (jnp.where(i == 0, nt, i - 1), 0)
    return pl.pallas_call(
        functools.partial(_mixer_router_kernel, seq // TM, nt),
        grid=(nt + 1,),
        in_specs=[pl.BlockSpec((TM, D_MODEL), lambda i: (jnp.minimum(i, nt - 1), 0))]
        + [full(a) for a in ins] + [any_spec] * 3,
        out_specs=[pl.BlockSpec((TM, D_MODEL), lambda i: (jnp.minimum(i, nt - 1), 0)),
                   pl.BlockSpec((TM, LANES), prev),
                   pl.BlockSpec((KOUT, D_MODEL), prev),
                   whole(tp), whole(tp), whole(tp + SUBLANES), whole(nt)] + [any_spec] * 3,
        out_shape=[jax.ShapeDtypeStruct((t, D_MODEL), F32),
                   jax.ShapeDtypeStruct(((nt + 1) * TM, LANES), F32),
                   jax.ShapeDtypeStruct(((nt + 1) * KOUT, D_MODEL), BF16),
                   jax.ShapeDtypeStruct((tp, LANES), I32),
                   jax.ShapeDtypeStruct((tp, LANES), I32),
                   jax.ShapeDtypeStruct((tp + SUBLANES, LANES), I32),
                   jax.ShapeDtypeStruct((nt, LANES), I32)]
        + [jax.ShapeDtypeStruct(w.shape, BF16) for w in expert_w],
        scratch_shapes=[pltpu.VMEM((HEAD_PAIRS, CHUNK, 2 * CHUNK), BF16),
                        pltpu.VMEM((TM, TM), BF16),
                        pltpu.VMEM((8, CONV_WIDTH), F32),
                        pltpu.VMEM((TM, D_MODEL), BF16),
                        pltpu.VMEM((2, TM, D_MODEL), BF16),
                        pltpu.VMEM(w_in.shape, BF16),
                        pltpu.VMEM(w_out.shape, BF16),
                        pltpu.VMEM(w_rt.shape, BF16),
                        pltpu.VMEM((nt + SUBLANES, LANES), F32)]
        + [pltpu.VMEM(w.shape[1:], F32) for w in expert_w]
        + [pltpu.VMEM(w.shape[1:], BF16) for w in expert_w]
        + [pltpu.SemaphoreType.DMA((3,)), pltpu.SemaphoreType.DMA((3,))],
        compiler_params=pltpu.CompilerParams(dimension_semantics=("arbitrary",),
                                             vmem_limit_bytes=VMEM_LIMIT),
        name="mixer_router",
    )(x2d, *ins, *expert_w)


def _expert_kernel(meta_ref, src_ref, dst_ref, stage_ref,
                   wg_hbm, wu_hbm, wd_hbm, out_ref,
                   xbuf, obuf, wg_s, wu_s, wd_s, par_ref, sem, osem, wsem):
    i = pl.program_id(0)
    n_used = meta_ref[3]
    slot = i % N_SLOTS
    bpt = BLOCKS_PER_ROW_TILE
    dump_row = dst_ref.shape[0] // bpt - SUBLANES
    expert_of = lambda tile: meta_ref[tile * META_W]

    def gather_copy(seq, j):
        tile = jnp.minimum(seq, n_used - 1)
        src = pl.multiple_of(src_ref[tile * bpt + j] * BLOCK, BLOCK)
        slot_ = seq % N_SLOTS
        return pltpu.make_async_copy(stage_ref.at[pl.ds(src, BLOCK)],
                                     xbuf.at[slot_, pl.ds(j * BLOCK, BLOCK)], sem.at[slot_])

    def scatter_copy(tile, j):
        row = jnp.where(tile < 0, dump_row, tile)
        dst = pl.multiple_of(dst_ref[row * bpt + j] * BLOCK, BLOCK)
        slot_ = (tile + N_SLOTS) % N_SLOTS
        return pltpu.make_async_copy(obuf.at[slot_, pl.ds(j * BLOCK, BLOCK)],
                                     out_ref.at[pl.ds(dst, BLOCK)], osem.at[slot_])

    def each_block(fn):
        for j in range(BLOCKS_PER_ROW_TILE):
            fn(j)

    def weight_copies(e, p):
        return (pltpu.make_async_copy(wg_hbm.at[e], wg_s.at[p], wsem.at[0]),
                pltpu.make_async_copy(wu_hbm.at[e], wu_s.at[p], wsem.at[1]),
                pltpu.make_async_copy(wd_hbm.at[e], wd_s.at[p], wsem.at[2]))

    @pl.when(i == 0)
    def _first():
        each_block(lambda j: gather_copy(0, j).start())
        each_block(lambda j: gather_copy(1, j).start())
        par_ref[0] = 0
        for cp in weight_copies(expert_of(0), 0):
            cp.start(priority=WEIGHT_DMA_PRIORITY)
        obuf[...] = jnp.zeros_like(obuf)

    @pl.when((i >= 2) & (i - 3 < n_used))
    def _free_out_slot():
        each_block(lambda j: scatter_copy(i - 3, j).wait())

    @pl.when(i == n_used)
    def _after_last_tile():
        each_block(lambda j: gather_copy(n_used, j).wait())
        each_block(lambda j: gather_copy(n_used + 1, j).wait())
        each_block(lambda j: scatter_copy(n_used - 1, j).start())

    @pl.when(i < n_used)
    def _tile():
        e = expert_of(i)
        prev = expert_of(jnp.maximum(i - 1, 0))

        @pl.when((i == 0) | (e != prev))
        def _new_expert():
            p = jnp.where(i == 0, 0, 1 - par_ref[0])
            for cp in weight_copies(e, p):
                cp.wait()
            par_ref[0] = p
            nxt = meta_ref[i * META_W + 1]

            @pl.when(nxt >= 0)
            def _next_weights():
                for cp in weight_copies(nxt, 1 - p):
                    cp.start()

        each_block(lambda j: gather_copy(i, j).wait())

        def mlp(rows):
            p = par_ref[0]
            hb = xbuf[slot, 0:rows, :]
            a = jnp.dot(hb, wg_s[p], preferred_element_type=F32)
            b = jnp.dot(hb, wu_s[p], preferred_element_type=F32)
            each_block(lambda j: gather_copy(i + 2, j).start())
            each_block(lambda j: scatter_copy(i - 1, j).start())
            hid = (a * jax.nn.sigmoid(a) * b).astype(BF16)
            obuf[slot, 0:rows, :] = jnp.dot(hid, wd_s[p], preferred_element_type=F32).astype(BF16)

        rows_needed = meta_ref[i * META_W + 2]
        for rows in range(ROW_STEP, TR + 1, ROW_STEP):
            pl.when(rows_needed == rows)(functools.partial(mlp, rows))


def _experts(meta, src_block, dst_block, stage, w_gate, w_up, w_down, n_tiles):
    any_spec = pl.BlockSpec(memory_space=pl.ANY)
    return pl.pallas_call(
        _expert_kernel,
        grid_spec=pltpu.PrefetchScalarGridSpec(
            num_scalar_prefetch=3,
            grid=(n_tiles,),
            in_specs=[any_spec, any_spec, any_spec, any_spec],
            out_specs=any_spec,
            scratch_shapes=[pltpu.VMEM((N_SLOTS, TR, D_MODEL), BF16),
                            pltpu.VMEM((N_SLOTS, TR, D_MODEL), BF16),
                            pltpu.VMEM((2, D_MODEL, D_EXPERT), BF16),
                            pltpu.VMEM((2, D_MODEL, D_EXPERT), BF16),
                            pltpu.VMEM((2, D_EXPERT, D_MODEL), BF16),
                            pltpu.SMEM((1,), I32),
                            pltpu.SemaphoreType.DMA((N_SLOTS,)),
                            pltpu.SemaphoreType.DMA((N_SLOTS,)),
                            pltpu.SemaphoreType.DMA((3,))]),
        out_shape=jax.ShapeDtypeStruct(stage.shape, stage.dtype),
        input_output_aliases={3: 0},
        compiler_params=pltpu.CompilerParams(dimension_semantics=("arbitrary",),
                                             vmem_limit_bytes=VMEM_LIMIT),
        name="experts",
    )(meta, src_block, dst_block, stage, w_gate, w_up, w_down)


def _combine_kernel(used_ref, x1_ref, route_ref, p_ref, gple_ref, wpg_f32, wpp_f32, gfin_ref,
                    srt_ref, out_ref, y_s, wpg_ref, wpp_ref):
    i = pl.program_id(0)

    @pl.when(i == 0)
    def _init():
        wpg_ref[...] = wpg_f32[...].astype(BF16)
        wpp_ref[...] = wpp_f32[...].astype(BF16)

    route = route_ref[...]
    w1, w2, pos1, pos2 = (route[:, k:k + 1] for k in (2, 3, 4, 5))

    def unsort(lo, hi):
        col = (lax.broadcasted_iota(I32, (TM, hi - lo), 1) + lo).astype(F32)
        mat = jnp.where(col == pos1, w1, jnp.where(col == pos2, w2, 0.0)).astype(BF16)
        return jnp.dot(mat, srt_ref[lo:hi, :], preferred_element_type=F32)

    def finish(y):
        pe = jnp.dot(p_ref[...].astype(BF16), wpp_ref[...], preferred_element_type=F32)
        hrows = TM // 2
        for h in range(2):
            rs = slice(h * hrows, (h + 1) * hrows)
            x2 = x1_ref[rs, :] + y[rs, :]
            hg = _rms(x2, gple_ref[...]).astype(BF16)
            gate = jax.nn.sigmoid(jnp.dot(hg, wpg_ref[...], preferred_element_type=F32))
            x3 = x2 + gate * pe[rs, :]
            out_ref[rs, :] = _rms(x3, gfin_ref[...])

    @pl.when(used_ref[i] <= SORT_MAIN)
    def _common():
        finish(unsort(0, SORT_MAIN))

    @pl.when(used_ref[i] > SORT_MAIN)
    def _with_tails():
        y_s[...] = unsort(0, SORT_MAIN)
        for lo in range(SORT_MAIN, KOUT, TAIL):
            @pl.when(used_ref[i] > lo)
            def _tail(lo=lo):
                y_s[...] += unsort(lo, lo + TAIL)
        finish(y_s[...])


def _combine(used, x1, route, p2d, g_ple, w_pg, w_pp, g_fin, sorted_out):
    t = p2d.shape[0]
    full = lambda a: pl.BlockSpec(a.shape, lambda i, *_: (0,) * a.ndim)
    return pl.pallas_call(
        _combine_kernel,
        grid_spec=pltpu.PrefetchScalarGridSpec(
            num_scalar_prefetch=1,
            grid=(t // TM,),
            in_specs=[pl.BlockSpec((TM, D_MODEL), lambda i, *_: (i, 0)),
                      pl.BlockSpec((TM, LANES), lambda i, *_: (i, 0)),
                      pl.BlockSpec((TM, PLE_DIM), lambda i, *_: (i, 0)),
                      full(g_ple), full(w_pg), full(w_pp), full(g_fin),
                      pl.BlockSpec((KOUT, D_MODEL), lambda i, *_: (i, 0))],
            out_specs=pl.BlockSpec((TM, D_MODEL), lambda i, *_: (i, 0)),
            scratch_shapes=[pltpu.VMEM((TM, D_MODEL), F32),
                            pltpu.VMEM(w_pg.shape, BF16),
                            pltpu.VMEM(w_pp.shape, BF16)]),
        out_shape=jax.ShapeDtypeStruct((t, D_MODEL), F32),
        compiler_params=pltpu.CompilerParams(dimension_semantics=("arbitrary",),
                                             vmem_limit_bytes=VMEM_LIMIT),
        name="combine_ple",
    )(used, x1, route, p2d, g_ple, w_pg, w_pp, g_fin, sorted_out)


def kernel(x, p, g_mix, w_in, w_conv, g_sgu, w_spatial, b_spatial, w_out, g_ffn, w_group, b_group,
           w_router, b_router, w_gate, w_up, w_down, g_ple, w_ple_gate, w_ple_proj, g_final):
    bsz, seq, d = x.shape
    t = bsz * seq
    assert w_in.shape[0] == 1, "single-layer block"
    assert d == D_MODEL and seq % TM == 0 and TM % CHUNK == 0
    nt = t // TM
    max_rows = TOP_K * t + nt * N_EXPERTS * (BLOCK - 1)
    n_tiles = (max_rows + N_EXPERTS * (TR - 1)) // TR + N_SLOTS
    l = 0

    w_rt = jnp.concatenate(
        [w_group[l], jnp.transpose(w_router[l], (1, 0, 2)).reshape(d, N_EXPERTS)], axis=1)
    w_rt = jnp.pad(w_rt, ((0, 0), (0, LANES - w_rt.shape[1]))).T
    b_rt = jnp.pad(jnp.concatenate([b_group[l], b_router[l].reshape(-1)]),
                   (0, LANES - N_GROUPS - N_EXPERTS)).reshape(LANES, 1)
    b_sp = jnp.repeat(b_spatial[l].T, SGU_WIDTH // SGU_HEADS, axis=1)

    x1, route, stage, meta, src, dst, used, wg_b, wu_b, wd_b = _mixer_router(
        x.reshape(t, d), g_mix[l].reshape(1, d), w_in[l], w_conv[l],
        g_sgu[l].reshape(1, -1), w_spatial[l], b_sp, w_out[l],
        g_ffn[l].reshape(1, d), w_rt, b_rt,
        w_gate[l].reshape(N_EXPERTS, d, D_EXPERT), w_up[l].reshape(N_EXPERTS, d, D_EXPERT),
        w_down[l].reshape(N_EXPERTS, D_EXPERT, d), seq, n_tiles)

    bpt = BLOCKS_PER_ROW_TILE
    meta, src, dst, used = (meta[:, :META_W].reshape(-1), src[:, :bpt].reshape(-1),
                            dst[:, :bpt].reshape(-1), used[:, 0])
    sorted_out = _experts(meta, src, dst, stage, wg_b, wu_b, wd_b, n_tiles)
    out = _combine(used, x1, route, p[l].reshape(t, PLE_DIM), g_ple[l].reshape(1, d),
                   w_ple_gate[l], w_ple_proj[l], g_final.reshape(1, d), sorted_out)
    return out.reshape(bsz, seq, d)
```

```python
import functools

import jax
import jax.numpy as jnp
from jax import lax
from jax.experimental import pallas as pl
from jax.experimental.pallas import tpu as pltpu

F32 = jnp.float32
BF16 = jnp.bfloat16
I32 = jnp.int32

EPS = 1e-6
D_MODEL = 1024
CONV_WIDTH = 512
SGU_WIDTH = 512
SGU_HEADS = 8
HEAD_PAIRS = SGU_HEADS // 2
CHUNK = 128
N_GROUPS = 4
EXPERTS_PER_GROUP = 8
N_EXPERTS = N_GROUPS * EXPERTS_PER_GROUP
TOP_K = 2
D_EXPERT = 512
PLE_DIM = 256
LANES = 128
SUBLANES = 8
BLOCK = 2 * SUBLANES
ROUTE_LANE0 = N_GROUPS
ROUTE_ROWS = 40

TM = 512
TR = 640
BLOCKS_PER_ROW_TILE = TR // BLOCK
KOUT = TOP_K * TM + N_EXPERTS * BLOCK
TAIL = 128
SORT_MAIN = KOUT - 2 * TAIL
META_W = 4
ROW_STEP = 128
N_SLOTS = 3
DUMP_SETS = 2
DUMP_BLOCKS = DUMP_SETS * BLOCKS_PER_ROW_TILE
assert DUMP_BLOCKS * BLOCK <= KOUT
WEIGHT_DMA_PRIORITY = 1
VMEM_LIMIT = 56 * 1024 * 1024


def _rms(x, g):
    return x * lax.rsqrt(jnp.mean(x * x, axis=-1, keepdims=True) + EPS) * g


def _build_tables(n_s, nt, meta_ref, src_ref, dst_ref, used_ref):
    tp = meta_ref.shape[0]
    bpt = BLOCKS_PER_ROW_TILE
    kb = KOUT // BLOCK
    n = n_s[0:nt, :]
    lane_n = lax.broadcasted_iota(I32, (nt, LANES), 1)
    is_e = (lane_n >= ROUTE_LANE0) & (lane_n < ROUTE_LANE0 + N_EXPERTS)
    nblk = jnp.where(is_e, jnp.floor((n + (BLOCK - 1)) * (1.0 / BLOCK)), 0.0)
    er = lax.broadcasted_iota(I32, (LANES, LANES), 0)
    ec = lax.broadcasted_iota(I32, (LANES, LANES), 1)
    loc = jnp.dot(nblk.astype(BF16), (er < ec).astype(BF16), preferred_element_type=F32)
    ti = lax.broadcasted_iota(I32, (nt, nt), 0)
    tj = lax.broadcasted_iota(I32, (nt, nt), 1)
    carry = jnp.dot((tj < ti).astype(BF16), nblk.astype(BF16), preferred_element_type=F32)
    total = jnp.sum(nblk, axis=0, keepdims=True)
    tiles_e = jnp.floor((total + (bpt - 1)) * (1.0 / bpt))
    tile_end = jnp.dot(jnp.broadcast_to(tiles_e, (SUBLANES, LANES)).astype(BF16),
                       (er <= ec).astype(BF16), preferred_element_type=F32)[0:1, :]
    tile0 = tile_end - tiles_e
    n_used = jnp.max(tile_end, axis=1, keepdims=True)

    tau = lax.broadcasted_iota(I32, (tp, LANES), 0).astype(F32)
    lane_i = lax.broadcasted_iota(I32, (tp, LANES), 1)
    lane_t = lane_i.astype(F32)
    is_e_t = (lane_i >= ROUTE_LANE0) & (lane_i < ROUTE_LANE0 + N_EXPERTS)
    te = jnp.sum(jnp.where(is_e_t & (tile_end <= jnp.minimum(tau, n_used - 1.0)), 1.0, 0.0),
                 axis=1, keepdims=True)
    oh = lane_t == te + ROUTE_LANE0
    pick = lambda row: jnp.sum(jnp.where(oh, row, 0.0), axis=1, keepdims=True)
    seg0_t = pick(tile0) * bpt
    e_lane = lane_t - ROUTE_LANE0
    cand = jnp.where(is_e_t & (e_lane > te) & (tiles_e > 0.0), e_lane, 99.0)
    nxt = jnp.min(cand, axis=1, keepdims=True)
    nxt = jnp.where(nxt == 99.0, -1.0, nxt)
    rows_t = jnp.clip(pick(total) * BLOCK - (tau[:, 0:1] - pick(tile0)) * TR, ROW_STEP, TR)
    rows_t = jnp.ceil(rows_t * (1.0 / ROW_STEP)) * ROW_STEP
    meta = jnp.where(lane_i == 0, te, 0.0)
    meta = jnp.where(lane_i == 1, nxt, meta)
    meta = jnp.where(lane_i == 2, rows_t, meta)
    meta = jnp.where(lane_i == 3, n_used, meta)
    meta_ref[...] = meta.astype(I32)

    bidx = tau * bpt + lane_t
    off = bidx - seg0_t
    ohb = jnp.where(oh, 1.0, 0.0).astype(BF16)

    def per_tile(tab):
        hi = jnp.floor(tab * (1.0 / 32))
        lo = tab - 32.0 * hi
        nt_dot = lambda a: lax.dot_general(ohb, a.astype(BF16), (((1,), (1,)), ((), ())),
                                           preferred_element_type=F32)
        return 32.0 * nt_dot(hi) + nt_dot(lo)

    run_end = per_tile(carry + nblk)
    run_beg = per_tile(carry)
    run_loc = per_tile(loc)
    tile_of = jnp.zeros((tp, LANES), F32)
    for it in range(nt):
        tile_of = tile_of + jnp.where(run_end[:, it:it + 1] <= off, 1.0, 0.0)
    srcv = off
    for it in range(nt):
        srcv = srcv + jnp.where(tile_of == it,
                                run_loc[:, it:it + 1] - run_beg[:, it:it + 1] + it * kb, 0.0)
    valid = (tile_of < nt) & (bidx < n_used * bpt)
    dump_set = tau - DUMP_SETS * jnp.floor((tau + 0.5) * (1.0 / DUMP_SETS))
    dump = nt * kb + dump_set * bpt + lane_t
    src_ref[...] = jnp.where(valid, srcv, kb - 1.0).astype(I32)
    dst_ref[0:tp, :] = jnp.where(valid, srcv, dump).astype(I32)
    dst_ref[tp:tp + SUBLANES, :] = (nt * kb + DUMP_BLOCKS - bpt
                                    + lax.broadcasted_iota(I32, (SUBLANES, LANES), 1))
    used_ref[...] = jnp.broadcast_to(jnp.sum(nblk, axis=1, keepdims=True) * BLOCK,
                                     (nt, LANES)).astype(I32)


def _mixer_router_kernel(tiles_per_seq, nt,
                         x_ref, gmix_ref, win_f32, wconv_ref, gsgu_ref, wsp_ref, bsp_ref,
                         wout_f32, gffn_ref, wrt_f32, brt_ref, wg_hbm, wu_hbm, wd_hbm,
                         x1_ref, route_ref, stage_ref, meta_ref, src_ref, dst_ref, used_ref,
                         wgb_hbm, wub_hbm, wdb_hbm,
                         wcat_s, tri_s, halo_s, mix_s, h2_s, win_ref, wout_ref, wrt_ref, n_s,
                         wg_f, wu_f, wd_f, wg_b, wu_b, wd_b, wisem, wosem):
    i = pl.program_id(0)
    nch = TM // CHUNK

    @pl.when(i == 0)
    def _init():
        r = lax.broadcasted_iota(I32, (CHUNK, CHUNK), 0)
        c = lax.broadcasted_iota(I32, (CHUNK, CHUNK), 1)
        causal = c <= r
        for j in range(HEAD_PAIRS):
            wa = jnp.where(causal, wsp_ref[2 * j], 0.0)
            wb = jnp.where(causal, wsp_ref[2 * j + 1], 0.0)
            wcat_s[j] = jnp.concatenate([wa, wb], axis=1).astype(BF16)
        rr = lax.broadcasted_iota(I32, (TM, TM), 0)
        cc = lax.broadcasted_iota(I32, (TM, TM), 1)
        tri_s[...] = (rr < cc).astype(BF16)
        h2_s[...] = jnp.zeros_like(h2_s)
        win_ref[...] = win_f32[...].astype(BF16)
        wout_ref[...] = wout_f32[...].astype(BF16)
        wrt_ref[...] = wrt_f32[...].astype(BF16)

    @pl.when(i % tiles_per_seq == 0)
    def _seq_start():
        halo_s[...] = jnp.zeros_like(halo_s)

    def weights_in(e):
        return (pltpu.make_async_copy(wg_hbm.at[e], wg_f, wisem.at[0]),
                pltpu.make_async_copy(wu_hbm.at[e], wu_f, wisem.at[1]),
                pltpu.make_async_copy(wd_hbm.at[e], wd_f, wisem.at[2]))

    def weights_out(e):
        return (pltpu.make_async_copy(wg_b, wgb_hbm.at[e], wosem.at[0]),
                pltpu.make_async_copy(wu_b, wub_hbm.at[e], wosem.at[1]),
                pltpu.make_async_copy(wd_b, wdb_hbm.at[e], wosem.at[2]))

    @pl.when(i == 0)
    def _first_weights():
        for cp in weights_in(0):
            cp.start(priority=WEIGHT_DMA_PRIORITY)

    experts_per_step = -(-N_EXPERTS // (nt + 1))
    for k in range(experts_per_step):
        e = i * experts_per_step + k

        @pl.when(e < N_EXPERTS)
        def _cast_expert(e=e):
            for cp in weights_in(e):
                cp.wait()

            @pl.when(e > 0)
            def _previous_written():
                for cp in weights_out(e - 1):
                    cp.wait()

            wg_b[...] = wg_f[...].astype(BF16)
            wu_b[...] = wu_f[...].astype(BF16)
            wd_b[...] = wd_f[...].astype(BF16)
            for cp in weights_out(e):
                cp.start()

            @pl.when(e + 1 < N_EXPERTS)
            def _next_in():
                for cp in weights_in(e + 1):
                    cp.start(priority=WEIGHT_DMA_PRIORITY)

    @pl.when(i == nt)
    def _last_written():
        for cp in weights_out(N_EXPERTS - 1):
            cp.wait()

    def step(mix):
        h2b = h2_s[(i + 1) % 2]
        if mix:
            x = x_ref[...]
            hb = _rms(x, gmix_ref[...]).astype(BF16)

            def proj(k):
                return jnp.dot(hb, win_ref[:, k * 512:(k + 1) * 512], preferred_element_type=F32)

        logits = lax.dot_general(wrt_ref[...], h2b, (((1,), (1,)), ((), ())),
                                 preferred_element_type=F32)[0:ROUTE_ROWS, :] + brt_ref[0:ROUTE_ROWS, :]
        if mix:
            pu = proj(3)
            pv = proj(4)

        ridx = lax.broadcasted_iota(I32, (ROUTE_ROWS, TM), 0).astype(F32)
        neg = jnp.float32(-jnp.inf)
        big = jnp.float32(1e9)
        is_g = ridx < N_GROUPS
        gl = jnp.where(is_g, logits, neg)
        gmax = jnp.max(gl, axis=0, keepdims=True)
        gsum = jnp.sum(jnp.where(is_g, jnp.exp(gl - gmax), 0.0), axis=0, keepdims=True)
        g_w = 1.0 / gsum
        g_idx = jnp.min(jnp.where(gl == gmax, ridx, big), axis=0, keepdims=True)
        lo_row = ROUTE_LANE0 + EXPERTS_PER_GROUP * g_idx
        in_grp = (ridx >= lo_row) & (ridx < lo_row + EXPERTS_PER_GROUP)
        el = jnp.where(in_grp, logits, neg)
        v1 = jnp.max(el, axis=0, keepdims=True)
        i1 = jnp.min(jnp.where(el == v1, ridx, big), axis=0, keepdims=True)
        el2 = jnp.where(ridx == i1, neg, el)
        v2 = jnp.max(el2, axis=0, keepdims=True)
        i2 = jnp.min(jnp.where(el2 == v2, ridx, big), axis=0, keepdims=True)
        e21 = jnp.exp(v2 - v1)
        w1 = g_w / (1.0 + e21)
        w2 = g_w * e21 / (1.0 + e21)

        if mix:
            u = jax.nn.gelu(pu)
            zc = proj(1) * proj(2)
            v = jax.nn.gelu(pv)
            vc = v - jnp.mean(v, axis=-1, keepdims=True)
            vn = vc * lax.rsqrt(jnp.mean(vc * vc, axis=-1, keepdims=True) + EPS) * gsgu_ref[...]
            vnb = vn.astype(BF16)

            row = lax.broadcasted_iota(I32, (TM, CONV_WIDTH), 0)
            h6 = halo_s[6:7, :]
            h7 = halo_s[7:8, :]
            z1 = jnp.where(row == 0, h7, pltpu.roll(zc, 1, 0))
            z2 = jnp.where(row == 0, h6, jnp.where(row == 1, h7, pltpu.roll(zc, 2, 0)))
            conv = z2 * wconv_ref[0:1, :] + z1 * wconv_ref[1:2, :] + zc * wconv_ref[2:3, :]
            halo_s[...] = zc[TM - 8:TM, :]
            mix_s[:, 0:CONV_WIDTH] = (proj(0) * conv).astype(BF16)

        sel1 = ridx == i1
        sel2 = ridx == i2
        onehot = jnp.where(sel1 | sel2, 1.0, 0.0)
        counts = jnp.sum(onehot, axis=1, keepdims=True)
        pad_rows = jnp.zeros((LANES - ROUTE_ROWS, LANES), F32)
        counts_sq = jnp.concatenate([jnp.broadcast_to(counts, (ROUTE_ROWS, LANES)), pad_rows], axis=0)
        blocks_sq = jnp.ceil(counts_sq * (1.0 / BLOCK))
        er = lax.broadcasted_iota(I32, (LANES, LANES), 0)
        ec = lax.broadcasted_iota(I32, (LANES, LANES), 1)
        before = (ec < er).astype(BF16)
        run_start = BLOCK * jnp.dot(before, blocks_sq.astype(BF16),
                                    preferred_element_type=F32)[0:ROUTE_ROWS, 0:1]
        rank = jnp.dot(onehot.astype(BF16), tri_s[...], preferred_element_type=F32) + run_start
        pos1_row = jnp.sum(jnp.where(sel1, rank, 0.0), axis=0, keepdims=True)
        pos2_row = jnp.sum(jnp.where(sel2, rank, 0.0), axis=0, keepdims=True)
        n_s[pl.ds(jnp.where(i == 0, nt, i - 1), 1), :] = counts_sq.T[0:1, :]

        r8 = lax.broadcasted_iota(I32, (SUBLANES, TM), 0)
        rows8 = jnp.where(r8 == 0, i1 - ROUTE_LANE0, 0.0)
        rows8 = jnp.where(r8 == 1, i2 - ROUTE_LANE0, rows8)
        rows8 = jnp.where(r8 == 2, w1, rows8)
        rows8 = jnp.where(r8 == 3, w2, rows8)
        rows8 = jnp.where(r8 == 4, pos1_row, rows8)
        rows8 = jnp.where(r8 == 5, pos2_row, rows8)
        route_ref[...] = jnp.concatenate(
            [rows8, jnp.zeros((LANES - SUBLANES, TM), F32)], axis=0).T

        def sort_rows(lo, hi):
            out_row = (lax.broadcasted_iota(I32, (hi - lo, TM), 0) + lo).astype(F32)
            sort_mat = jnp.where((out_row == pos1_row) | (out_row == pos2_row), 1.0, 0.0).astype(BF16)
            stage_ref[lo:hi, :] = jnp.dot(sort_mat, h2b, preferred_element_type=F32).astype(BF16)

        sort_rows(0, SORT_MAIN // 2)
        if mix:
            left = lax.broadcasted_iota(I32, (CHUNK, LANES), 1) < (LANES // 2)
            zero = jnp.zeros((CHUNK, LANES), BF16)
            for j in range(HEAD_PAIRS):
                cols = []
                for c in range(nch):
                    s = vnb[c * CHUNK:(c + 1) * CHUNK, j * LANES:(j + 1) * LANES]
                    cols.append(jnp.concatenate([jnp.where(left, s, zero), jnp.where(left, zero, s)],
                                                axis=0))
                rhs = jnp.concatenate(cols, axis=1)
                res = jnp.dot(wcat_s[j], rhs, preferred_element_type=F32)
                bias = bsp_ref[:, j * LANES:(j + 1) * LANES]
                for c in range(nch):
                    mixed = res[:, c * LANES:(c + 1) * LANES] + bias
                    uu = u[c * CHUNK:(c + 1) * CHUNK, j * LANES:(j + 1) * LANES]
                    mix_s[c * CHUNK:(c + 1) * CHUNK,
                          CONV_WIDTH + j * LANES:CONV_WIDTH + (j + 1) * LANES] = (uu * mixed).astype(BF16)

        sort_rows(SORT_MAIN // 2, SORT_MAIN)
        if mix:
            x1 = x + jnp.dot(mix_s[...], wout_ref[...], preferred_element_type=F32)
            x1_ref[...] = x1
            h2_s[i % 2] = _rms(x1, gffn_ref[...]).astype(BF16)

        used_rows = BLOCK * jnp.sum(blocks_sq[:, 0:1])
        for lo in range(SORT_MAIN, KOUT, TAIL):
            @pl.when(used_rows > lo)
            def _tail(lo=lo):
                sort_rows(lo, lo + TAIL)

            @pl.when(used_rows <= lo)
            def _empty_tail(lo=lo):
                stage_ref[lo:lo + TAIL, :] = jnp.zeros((TAIL, D_MODEL), BF16)

    pl.when(i < nt)(functools.partial(step, True))

    @pl.when(i == nt)
    def _last():
        step(False)
        _build_tables(n_s, nt, meta_ref, src_ref, dst_ref, used_ref)


def _mixer_router(x2d, g_mix, w_in, w_conv, g_sgu, w_sp, b_sp, w_out, g_ffn, w_rt, b_rt,
                  w_gate, w_up, w_down, seq, n_tiles):
    t = x2d.shape[0]
    nt = t // TM
    tp = -(-n_tiles // SUBLANES) * SUBLANES
    whole = lambda r: pl.BlockSpec((r, LANES), lambda i: (0, 0))
    full = lambda a: pl.BlockSpec(a.shape, lambda i: (0,) * a.ndim, pipeline_mode=pl.Buffered(1))
    any_spec = pl.BlockSpec(memory_space=pl.ANY)
    ins = (g_mix, w_in, w_conv, g_sgu, w_sp, b_sp, w_out, g_ffn, w_rt, b_rt)
    expert_w = (w_gate, w_up, w_down)
    prev = lambda i: (jnp.where(i == 0, nt, i - 1), 0)
    return pl.pallas_call(
        functools.partial(_mixer_router_kernel, seq // TM, nt),
        grid=(nt + 1,),
        in_specs=[pl.BlockSpec((TM, D_MODEL), lambda i: (jnp.minimum(i, nt - 1), 0))]
        + [full(a) for a in ins] + [any_spec] * 3,
        out_specs=[pl.BlockSpec((TM, D_MODEL), lambda i: (jnp.minimum(i, nt - 1), 0)),
                   pl.BlockSpec((TM, LANES), prev),
                   pl.BlockSpec((KOUT, D_MODEL), prev),
                   whole(tp), whole(tp), whole(tp + SUBLANES), whole(nt)] + [any_spec] * 3,
        out_shape=[jax.ShapeDtypeStruct((t, D_MODEL), F32),
                   jax.ShapeDtypeStruct(((nt + 1) * TM, LANES), F32),
                   jax.ShapeDtypeStruct(((nt + 1) * KOUT, D_MODEL), BF16),
                   jax.ShapeDtypeStruct((tp, LANES), I32),
                   jax.ShapeDtypeStruct((tp, LANES), I32),
                   jax.ShapeDtypeStruct((tp + SUBLANES, LANES), I32),
                   jax.ShapeDtypeStruct((nt, LANES), I32)]
        + [jax.ShapeDtypeStruct(w.shape, BF16) for w in expert_w],
        scratch_shapes=[pltpu.VMEM((HEAD_PAIRS, CHUNK, 2 * CHUNK), BF16),
                        pltpu.VMEM((TM, TM), BF16),
                        pltpu.VMEM((8, CONV_WIDTH), F32),
                        pltpu.VMEM((TM, D_MODEL), BF16),
                        pltpu.VMEM((2, TM, D_MODEL), BF16),
                        pltpu.VMEM(w_in.shape, BF16),
                        pltpu.VMEM(w_out.shape, BF16),
                        pltpu.VMEM(w_rt.shape, BF16),
                        pltpu.VMEM((nt + SUBLANES, LANES), F32)]
        + [pltpu.VMEM(w.shape[1:], F32) for w in expert_w]
        + [pltpu.VMEM(w.shape[1:], BF16) for w in expert_w]
        + [pltpu.SemaphoreType.DMA((3,)), pltpu.SemaphoreType.DMA((3,))],
        compiler_params=pltpu.CompilerParams(dimension_semantics=("arbitrary",),
                                             vmem_limit_bytes=VMEM_LIMIT),
        name="mixer_router",
    )(x2d, *ins, *expert_w)


def _expert_kernel(meta_ref, src_ref, dst_ref, stage_ref,
                   wg_hbm, wu_hbm, wd_hbm, out_ref,
                   xbuf, obuf, wg_s, wu_s, wd_s, par_ref, sem, osem, wsem):
    i = pl.program_id(0)
    n_used = meta_ref[3]
    slot = i % N_SLOTS
    bpt = BLOCKS_PER_ROW_TILE
    dump_row = dst_ref.shape[0] // bpt - SUBLANES
    expert_of = lambda tile: meta_ref[tile * META_W]

    def gather_copy(seq, j):
        tile = jnp.minimum(seq, n_used - 1)
        src = pl.multiple_of(src_ref[tile * bpt + j] * BLOCK, BLOCK)
        slot_ = seq % N_SLOTS
        return pltpu.make_async_copy(stage_ref.at[pl.ds(src, BLOCK)],
                                     xbuf.at[slot_, pl.ds(j * BLOCK, BLOCK)], sem.at[slot_])

    def scatter_copy(tile, j):
        row = jnp.where(tile < 0, dump_row, tile)
        dst = pl.multiple_of(dst_ref[row * bpt + j] * BLOCK, BLOCK)
        slot_ = (tile + N_SLOTS) % N_SLOTS
        return pltpu.make_async_copy(obuf.at[slot_, pl.ds(j * BLOCK, BLOCK)],
                                     out_ref.at[pl.ds(dst, BLOCK)], osem.at[slot_])

    def each_block(fn):
        for j in range(BLOCKS_PER_ROW_TILE):
            fn(j)

    def weight_copies(e, p):
        return (pltpu.make_async_copy(wg_hbm.at[e], wg_s.at[p], wsem.at[0]),
                pltpu.make_async_copy(wu_hbm.at[e], wu_s.at[p], wsem.at[1]),
                pltpu.make_async_copy(wd_hbm.at[e], wd_s.at[p], wsem.at[2]))

    @pl.when(i == 0)
    def _first():
        each_block(lambda j: gather_copy(0, j).start())
        each_block(lambda j: gather_copy(1, j).start())
        par_ref[0] = 0
        for cp in weight_copies(expert_of(0), 0):
            cp.start(priority=WEIGHT_DMA_PRIORITY)
        obuf[...] = jnp.zeros_like(obuf)

    @pl.when((i >= 2) & (i - 3 < n_used))
    def _free_out_slot():
        each_block(lambda j: scatter_copy(i - 3, j).wait())

    @pl.when(i == n_used)
    def _after_last_tile():
        each_block(lambda j: gather_copy(n_used, j).wait())
        each_block(lambda j: gather_copy(n_used + 1, j).wait())
        each_block(lambda j: scatter_copy(n_used - 1, j).start())

    @pl.when(i < n_used)
    def _tile():
        e = expert_of(i)
        prev = expert_of(jnp.maximum(i - 1, 0))

        @pl.when((i == 0) | (e != prev))
        def _new_expert():
            p = jnp.where(i == 0, 0, 1 - par_ref[0])
            for cp in weight_copies(e, p):
                cp.wait()
            par_ref[0] = p
            nxt = meta_ref[i * META_W + 1]

            @pl.when(nxt >= 0)
            def _next_weights():
                for cp in weight_copies(nxt, 1 - p):
                    cp.start(priority=WEIGHT_DMA_PRIORITY)

        each_block(lambda j: gather_copy(i, j).wait())

        def mlp(rows):
            p = par_ref[0]
            hb = xbuf[slot, 0:rows, :]
            a = jnp.dot(hb, wg_s[p], preferred_element_type=F32)
            b = jnp.dot(hb, wu_s[p], preferred_element_type=F32)
            each_block(lambda j: gather_copy(i + 2, j).start())
            each_block(lambda j: scatter_copy(i - 1, j).start())
            hid = (a * jax.nn.sigmoid(a) * b).astype(BF16)
            obuf[slot, 0:rows, :] = jnp.dot(hid, wd_s[p], preferred_element_type=F32).astype(BF16)

        rows_needed = meta_ref[i * META_W + 2]
        for rows in range(ROW_STEP, TR + 1, ROW_STEP):
            pl.when(rows_needed == rows)(functools.partial(mlp, rows))


def _experts(meta, src_block, dst_block, stage, w_gate, w_up, w_down, n_tiles):
    any_spec = pl.BlockSpec(memory_space=pl.ANY)
    return pl.pallas_call(
        _expert_kernel,
        grid_spec=pltpu.PrefetchScalarGridSpec(
            num_scalar_prefetch=3,
            grid=(n_tiles,),
            in_specs=[any_spec, any_spec, any_spec, any_spec],
            out_specs=any_spec,
            scratch_shapes=[pltpu.VMEM((N_SLOTS, TR, D_MODEL), BF16),
                            pltpu.VMEM((N_SLOTS, TR, D_MODEL), BF16),
                            pltpu.VMEM((2, D_MODEL, D_EXPERT), BF16),
                            pltpu.VMEM((2, D_MODEL, D_EXPERT), BF16),
                            pltpu.VMEM((2, D_EXPERT, D_MODEL), BF16),
                            pltpu.SMEM((1,), I32),
                            pltpu.SemaphoreType.DMA((N_SLOTS,)),
                            pltpu.SemaphoreType.DMA((N_SLOTS,)),
                            pltpu.SemaphoreType.DMA((3,))]),
        out_shape=jax.ShapeDtypeStruct(stage.shape, stage.dtype),
        input_output_aliases={3: 0},
        compiler_params=pltpu.CompilerParams(dimension_semantics=("arbitrary",),
                                             vmem_limit_bytes=VMEM_LIMIT),
        name="experts",
    )(meta, src_block, dst_block, stage, w_gate, w_up, w_down)


def _combine_kernel(used_ref, x1_ref, route_ref, p_ref, gple_ref, wpg_f32, wpp_f32, gfin_ref,
                    srt_ref, out_ref, y_s, wpg_ref, wpp_ref):
    i = pl.program_id(0)

    @pl.when(i == 0)
    def _init():
        wpg_ref[...] = wpg_f32[...].astype(BF16)
        wpp_ref[...] = wpp_f32[...].astype(BF16)

    route = route_ref[...]
    w1, w2, pos1, pos2 = (route[:, k:k + 1] for k in (2, 3, 4, 5))

    def unsort(lo, hi):
        col = (lax.broadcasted_iota(I32, (TM, hi - lo), 1) + lo).astype(F32)
        mat = jnp.where(col == pos1, w1, jnp.where(col == pos2, w2, 0.0)).astype(BF16)
        return jnp.dot(mat, srt_ref[lo:hi, :], preferred_element_type=F32)

    def finish(y):
        pe = jnp.dot(p_ref[...].astype(BF16), wpp_ref[...], preferred_element_type=F32)
        hrows = TM // 2
        for h in range(2):
            rs = slice(h * hrows, (h + 1) * hrows)
            x2 = x1_ref[rs, :] + y[rs, :]
            hg = _rms(x2, gple_ref[...]).astype(BF16)
            gate = jax.nn.sigmoid(jnp.dot(hg, wpg_ref[...], preferred_element_type=F32))
            x3 = x2 + gate * pe[rs, :]
            out_ref[rs, :] = _rms(x3, gfin_ref[...])

    @pl.when(used_ref[i] <= SORT_MAIN)
    def _common():
        finish(unsort(0, SORT_MAIN))

    @pl.when(used_ref[i] > SORT_MAIN)
    def _with_tails():
        y_s[...] = unsort(0, SORT_MAIN)
        for lo in range(SORT_MAIN, KOUT, TAIL):
            @pl.when(used_ref[i] > lo)
            def _tail(lo=lo):
                y_s[...] += unsort(lo, lo + TAIL)
        finish(y_s[...])


def _combine(used, x1, route, p2d, g_ple, w_pg, w_pp, g_fin, sorted_out):
    t = p2d.shape[0]
    full = lambda a: pl.BlockSpec(a.shape, lambda i, *_: (0,) * a.ndim, pipeline_mode=pl.Buffered(1))
    return pl.pallas_call(
        _combine_kernel,
        grid_spec=pltpu.PrefetchScalarGridSpec(
            num_scalar_prefetch=1,
            grid=(t // TM,),
            in_specs=[pl.BlockSpec((TM, D_MODEL), lambda i, *_: (i, 0)),
                      pl.BlockSpec((TM, LANES), lambda i, *_: (i, 0)),
                      pl.BlockSpec((TM, PLE_DIM), lambda i, *_: (i, 0)),
                      full(g_ple), full(w_pg), full(w_pp), full(g_fin),
                      pl.BlockSpec((KOUT, D_MODEL), lambda i, *_: (i, 0))],
            out_specs=pl.BlockSpec((TM, D_MODEL), lambda i, *_: (i, 0)),
            scratch_shapes=[pltpu.VMEM((TM, D_MODEL), F32),
                            pltpu.VMEM(w_pg.shape, BF16),
                            pltpu.VMEM(w_pp.shape, BF16)]),
        out_shape=jax.ShapeDtypeStruct((t, D_MODEL), F32),
        compiler_params=pltpu.CompilerParams(dimension_semantics=("arbitrary",),
                                             vmem_limit_bytes=VMEM_LIMIT),
        name="combine_ple",
    )(used, x1, route, p2d, g_ple, w_pg, w_pp, g_fin, sorted_out)


def kernel(x, p, g_mix, w_in, w_conv, g_sgu, w_spatial, b_spatial, w_out, g_ffn, w_group, b_group,
           w_router, b_router, w_gate, w_up, w_down, g_ple, w_ple_gate, w_ple_proj, g_final):
    bsz, seq, d = x.shape
    t = bsz * seq
    assert w_in.shape[0] == 1, "single-layer block"
    assert d == D_MODEL and seq % TM == 0 and TM % CHUNK == 0
    nt = t // TM
    max_rows = TOP_K * t + nt * N_EXPERTS * (BLOCK - 1)
    n_tiles = (max_rows + N_EXPERTS * (TR - 1)) // TR + N_SLOTS
    l = 0

    w_rt = jnp.concatenate(
        [w_group[l], jnp.transpose(w_router[l], (1, 0, 2)).reshape(d, N_EXPERTS)], axis=1)
    w_rt = jnp.pad(w_rt, ((0, 0), (0, LANES - w_rt.shape[1]))).T
    b_rt = jnp.pad(jnp.concatenate([b_group[l], b_router[l].reshape(-1)]),
                   (0, LANES - N_GROUPS - N_EXPERTS)).reshape(LANES, 1)
    b_sp = jnp.repeat(b_spatial[l].T, SGU_WIDTH // SGU_HEADS, axis=1)

    x1, route, stage, meta, src, dst, used, wg_b, wu_b, wd_b = _mixer_router(
        x.reshape(t, d), g_mix[l].reshape(1, d), w_in[l], w_conv[l],
        g_sgu[l].reshape(1, -1), w_spatial[l], b_sp, w_out[l],
        g_ffn[l].reshape(1, d), w_rt, b_rt,
        w_gate[l].reshape(N_EXPERTS, d, D_EXPERT), w_up[l].reshape(N_EXPERTS, d, D_EXPERT),
        w_down[l].reshape(N_EXPERTS, D_EXPERT, d), seq, n_tiles)

    bpt = BLOCKS_PER_ROW_TILE
    meta, src, dst, used = (meta[:, :META_W].reshape(-1), src[:, :bpt].reshape(-1),
                            dst[:, :bpt].reshape(-1), used[:, 0])
    sorted_out = _experts(meta, src, dst, stage, wg_b, wu_b, wd_b, n_tiles)
    out = _combine(used, x1, route, p[l].reshape(t, PLE_DIM), g_ple[l].reshape(1, d),
                   w_ple_gate[l], w_ple_proj[l], g_final.reshape(1, d), sorted_out)
    return out.reshape(bsz, seq, d)
```

```python
import functools

import jax
import jax.numpy as jnp
from jax import lax
from jax.experimental import pallas as pl
from jax.experimental.pallas import tpu as pltpu

F32 = jnp.float32
BF16 = jnp.bfloat16
I32 = jnp.int32

EPS = 1e-6
D_MODEL = 1024
CONV_WIDTH = 512
SGU_WIDTH = 512
SGU_HEADS = 8
HEAD_PAIRS = SGU_HEADS // 2
CHUNK = 128
N_GROUPS = 4
EXPERTS_PER_GROUP = 8
N_EXPERTS = N_GROUPS * EXPERTS_PER_GROUP
TOP_K = 2
D_EXPERT = 512
PLE_DIM = 256
LANES = 128
SUBLANES = 8
BLOCK = 2 * SUBLANES
ROUTE_LANE0 = N_GROUPS
ROUTE_ROWS = 40

TM = 512
TR = 640
BLOCKS_PER_ROW_TILE = TR // BLOCK
KOUT = TOP_K * TM + N_EXPERTS * BLOCK
TAIL = 128
SORT_MAIN = KOUT - 2 * TAIL
META_W = 4
ROW_STEP = 128
N_SLOTS = 3
DUMP_SETS = 2
DUMP_BLOCKS = DUMP_SETS * BLOCKS_PER_ROW_TILE
assert DUMP_BLOCKS * BLOCK <= KOUT
WEIGHT_DMA_PRIORITY = 1
VMEM_LIMIT = 56 * 1024 * 1024


def _rms(x, g):
    return x * lax.rsqrt(jnp.mean(x * x, axis=-1, keepdims=True) + EPS) * g


def _build_tables(n_s, nt, meta_ref, src_ref, dst_ref, used_ref):
    tp = meta_ref.shape[0]
    bpt = BLOCKS_PER_ROW_TILE
    kb = KOUT // BLOCK
    n = n_s[0:nt, :]
    lane_n = lax.broadcasted_iota(I32, (nt, LANES), 1)
    is_e = (lane_n >= ROUTE_LANE0) & (lane_n < ROUTE_LANE0 + N_EXPERTS)
    nblk = jnp.where(is_e, jnp.floor((n + (BLOCK - 1)) * (1.0 / BLOCK)), 0.0)
    er = lax.broadcasted_iota(I32, (LANES, LANES), 0)
    ec = lax.broadcasted_iota(I32, (LANES, LANES), 1)
    loc = jnp.dot(nblk.astype(BF16), (er < ec).astype(BF16), preferred_element_type=F32)
    ti = lax.broadcasted_iota(I32, (nt, nt), 0)
    tj = lax.broadcasted_iota(I32, (nt, nt), 1)
    carry = jnp.dot((tj < ti).astype(BF16), nblk.astype(BF16), preferred_element_type=F32)
    total = jnp.sum(nblk, axis=0, keepdims=True)
    tiles_e = jnp.floor((total + (bpt - 1)) * (1.0 / bpt))
    tile_end = jnp.dot(jnp.broadcast_to(tiles_e, (SUBLANES, LANES)).astype(BF16),
                       (er <= ec).astype(BF16), preferred_element_type=F32)[0:1, :]
    tile0 = tile_end - tiles_e
    n_used = jnp.max(tile_end, axis=1, keepdims=True)

    tau = lax.broadcasted_iota(I32, (tp, LANES), 0).astype(F32)
    lane_i = lax.broadcasted_iota(I32, (tp, LANES), 1)
    lane_t = lane_i.astype(F32)
    is_e_t = (lane_i >= ROUTE_LANE0) & (lane_i < ROUTE_LANE0 + N_EXPERTS)
    te = jnp.sum(jnp.where(is_e_t & (tile_end <= jnp.minimum(tau, n_used - 1.0)), 1.0, 0.0),
                 axis=1, keepdims=True)
    oh = lane_t == te + ROUTE_LANE0
    pick = lambda row: jnp.sum(jnp.where(oh, row, 0.0), axis=1, keepdims=True)
    seg0_t = pick(tile0) * bpt
    e_lane = lane_t - ROUTE_LANE0
    cand = jnp.where(is_e_t & (e_lane > te) & (tiles_e > 0.0), e_lane, 99.0)
    nxt = jnp.min(cand, axis=1, keepdims=True)
    nxt = jnp.where(nxt == 99.0, -1.0, nxt)
    rows_t = jnp.clip(pick(total) * BLOCK - (tau[:, 0:1] - pick(tile0)) * TR, ROW_STEP, TR)
    rows_t = jnp.ceil(rows_t * (1.0 / ROW_STEP)) * ROW_STEP
    meta = jnp.where(lane_i == 0, te, 0.0)
    meta = jnp.where(lane_i == 1, nxt, meta)
    meta = jnp.where(lane_i == 2, rows_t, meta)
    meta = jnp.where(lane_i == 3, n_used, meta)
    meta_ref[...] = meta.astype(I32)

    bidx = tau * bpt + lane_t
    off = bidx - seg0_t
    ohb = jnp.where(oh, 1.0, 0.0).astype(BF16)

    def per_tile(tab):
        hi = jnp.floor(tab * (1.0 / 32))
        lo = tab - 32.0 * hi
        nt_dot = lambda a: lax.dot_general(ohb, a.astype(BF16), (((1,), (1,)), ((), ())),
                                           preferred_element_type=F32)
        return 32.0 * nt_dot(hi) + nt_dot(lo)

    run_end = per_tile(carry + nblk)
    run_beg = per_tile(carry)
    run_loc = per_tile(loc)
    tile_of = jnp.zeros((tp, LANES), F32)
    for it in range(nt):
        tile_of = tile_of + jnp.where(run_end[:, it:it + 1] <= off, 1.0, 0.0)
    srcv = off
    for it in range(nt):
        srcv = srcv + jnp.where(tile_of == it,
                                run_loc[:, it:it + 1] - run_beg[:, it:it + 1] + it * kb, 0.0)
    valid = (tile_of < nt) & (bidx < n_used * bpt)
    dump_set = tau - DUMP_SETS * jnp.floor((tau + 0.5) * (1.0 / DUMP_SETS))
    dump = nt * kb + dump_set * bpt + lane_t
    src_ref[...] = jnp.where(valid, srcv, kb - 1.0).astype(I32)
    dst_ref[0:tp, :] = jnp.where(valid, srcv, dump).astype(I32)
    dst_ref[tp:tp + SUBLANES, :] = (nt * kb + DUMP_BLOCKS - bpt
                                    + lax.broadcasted_iota(I32, (SUBLANES, LANES), 1))
    used_ref[...] = jnp.broadcast_to(jnp.sum(nblk, axis=1, keepdims=True) * BLOCK,
                                     (nt, LANES)).astype(I32)


def _mixer_router_kernel(tiles_per_seq, nt,
                         x_ref, gmix_ref, win_f32, wconv_ref, gsgu_ref, wsp_ref, bsp_ref,
                         wout_f32, gffn_ref, wrt_f32, brt_ref, wg_hbm, wu_hbm, wd_hbm,
                         x1_ref, route_ref, stage_ref, meta_ref, src_ref, dst_ref, used_ref,
                         wgb_hbm, wub_hbm, wdb_hbm,
                         wcat_s, tri_s, halo_s, mix_s, h2_s, win_ref, wout_ref, wrt_ref, n_s,
                         wg_f, wu_f, wd_f, wg_b, wu_b, wd_b, wisem, wosem):
    i = pl.program_id(0)
    nch = TM // CHUNK

    @pl.when(i == 0)
    def _init():
        r = lax.broadcasted_iota(I32, (CHUNK, CHUNK), 0)
        c = lax.broadcasted_iota(I32, (CHUNK, CHUNK), 1)
        causal = c <= r
        for j in range(HEAD_PAIRS):
            wa = jnp.where(causal, wsp_ref[2 * j], 0.0)
            wb = jnp.where(causal, wsp_ref[2 * j + 1], 0.0)
            wcat_s[j] = jnp.concatenate([wa, wb], axis=1).astype(BF16)
        rr = lax.broadcasted_iota(I32, (TM, TM), 0)
        cc = lax.broadcasted_iota(I32, (TM, TM), 1)
        tri_s[...] = (rr < cc).astype(BF16)
        h2_s[...] = jnp.zeros_like(h2_s)
        win_ref[...] = win_f32[...].astype(BF16)
        wout_ref[...] = wout_f32[...].astype(BF16)
        wrt_ref[...] = wrt_f32[...].astype(BF16)

    @pl.when(i % tiles_per_seq == 0)
    def _seq_start():
        halo_s[...] = jnp.zeros_like(halo_s)

    def weights_in(e):
        return (pltpu.make_async_copy(wg_hbm.at[e], wg_f, wisem.at[0]),
                pltpu.make_async_copy(wu_hbm.at[e], wu_f, wisem.at[1]),
                pltpu.make_async_copy(wd_hbm.at[e], wd_f, wisem.at[2]))

    def weights_out(e):
        return (pltpu.make_async_copy(wg_b, wgb_hbm.at[e], wosem.at[0]),
                pltpu.make_async_copy(wu_b, wub_hbm.at[e], wosem.at[1]),
                pltpu.make_async_copy(wd_b, wdb_hbm.at[e], wosem.at[2]))

    @pl.when(i == 0)
    def _first_weights():
        for cp in weights_in(0):
            cp.start(priority=WEIGHT_DMA_PRIORITY)

    experts_per_step = -(-N_EXPERTS // (nt + 1))
    for k in range(experts_per_step):
        e = i * experts_per_step + k

        @pl.when(e < N_EXPERTS)
        def _cast_expert(e=e):
            for cp in weights_in(e):
                cp.wait()

            @pl.when(e > 0)
            def _previous_written():
                for cp in weights_out(e - 1):
                    cp.wait()

            wg_b[...] = wg_f[...].astype(BF16)
            wu_b[...] = wu_f[...].astype(BF16)
            wd_b[...] = wd_f[...].astype(BF16)
            for cp in weights_out(e):
                cp.start()

            @pl.when(e + 1 < N_EXPERTS)
            def _next_in():
                for cp in weights_in(e + 1):
                    cp.start(priority=WEIGHT_DMA_PRIORITY)

    @pl.when(i == nt)
    def _last_written():
        for cp in weights_out(N_EXPERTS - 1):
            cp.wait()

    def step(mix):
        h2b = h2_s[(i + 1) % 2]
        if mix:
            x = x_ref[...]
            hb = _rms(x, gmix_ref[...]).astype(BF16)

            def proj(k):
                return jnp.dot(hb, win_ref[:, k * 512:(k + 1) * 512], preferred_element_type=F32)

        logits = lax.dot_general(wrt_ref[...], h2b, (((1,), (1,)), ((), ())),
                                 preferred_element_type=F32)[0:ROUTE_ROWS, :] + brt_ref[0:ROUTE_ROWS, :]
        if mix:
            pu = proj(3)
            pv = proj(4)

        ridx = lax.broadcasted_iota(I32, (ROUTE_ROWS, TM), 0).astype(F32)
        neg = jnp.float32(-jnp.inf)
        big = jnp.float32(1e9)
        is_g = ridx < N_GROUPS
        gl = jnp.where(is_g, logits, neg)
        gmax = jnp.max(gl, axis=0, keepdims=True)
        gsum = jnp.sum(jnp.where(is_g, jnp.exp(gl - gmax), 0.0), axis=0, keepdims=True)
        g_w = 1.0 / gsum
        g_idx = jnp.min(jnp.where(gl == gmax, ridx, big), axis=0, keepdims=True)
        lo_row = ROUTE_LANE0 + EXPERTS_PER_GROUP * g_idx
        in_grp = (ridx >= lo_row) & (ridx < lo_row + EXPERTS_PER_GROUP)
        el = jnp.where(in_grp, logits, neg)
        v1 = jnp.max(el, axis=0, keepdims=True)
        i1 = jnp.min(jnp.where(el == v1, ridx, big), axis=0, keepdims=True)
        el2 = jnp.where(ridx == i1, neg, el)
        v2 = jnp.max(el2, axis=0, keepdims=True)
        i2 = jnp.min(jnp.where(el2 == v2, ridx, big), axis=0, keepdims=True)
        e21 = jnp.exp(v2 - v1)
        w1 = g_w / (1.0 + e21)
        w2 = g_w * e21 / (1.0 + e21)

        if mix:
            u = jax.nn.gelu(pu)
            zc = proj(1) * proj(2)
            v = jax.nn.gelu(pv)
            vc = v - jnp.mean(v, axis=-1, keepdims=True)
            vn = vc * lax.rsqrt(jnp.mean(vc * vc, axis=-1, keepdims=True) + EPS) * gsgu_ref[...]
            vnb = vn.astype(BF16)

            row = lax.broadcasted_iota(I32, (TM, CONV_WIDTH), 0)
            h6 = halo_s[6:7, :]
            h7 = halo_s[7:8, :]
            z1 = jnp.where(row == 0, h7, pltpu.roll(zc, 1, 0))
            z2 = jnp.where(row == 0, h6, jnp.where(row == 1, h7, pltpu.roll(zc, 2, 0)))
            conv = z2 * wconv_ref[0:1, :] + z1 * wconv_ref[1:2, :] + zc * wconv_ref[2:3, :]
            halo_s[...] = zc[TM - 8:TM, :]
            mix_s[:, 0:CONV_WIDTH] = (proj(0) * conv).astype(BF16)

        sel1 = ridx == i1
        sel2 = ridx == i2
        onehot = jnp.where(sel1 | sel2, 1.0, 0.0)
        counts = jnp.sum(onehot, axis=1, keepdims=True)
        pad_rows = jnp.zeros((LANES - ROUTE_ROWS, LANES), F32)
        counts_sq = jnp.concatenate([jnp.broadcast_to(counts, (ROUTE_ROWS, LANES)), pad_rows], axis=0)
        blocks_sq = jnp.ceil(counts_sq * (1.0 / BLOCK))
        er = lax.broadcasted_iota(I32, (LANES, LANES), 0)
        ec = lax.broadcasted_iota(I32, (LANES, LANES), 1)
        before = (ec < er).astype(BF16)
        run_start = BLOCK * jnp.dot(before, blocks_sq.astype(BF16),
                                    preferred_element_type=F32)[0:ROUTE_ROWS, 0:1]
        rank = jnp.dot(onehot.astype(BF16), tri_s[...], preferred_element_type=F32) + run_start
        pos1_row = jnp.sum(jnp.where(sel1, rank, 0.0), axis=0, keepdims=True)
        pos2_row = jnp.sum(jnp.where(sel2, rank, 0.0), axis=0, keepdims=True)
        n_s[pl.ds(jnp.where(i == 0, nt, i - 1), 1), :] = counts_sq.T[0:1, :]

        r8 = lax.broadcasted_iota(I32, (SUBLANES, TM), 0)
        rows8 = jnp.where(r8 == 0, i1 - ROUTE_LANE0, 0.0)
        rows8 = jnp.where(r8 == 1, i2 - ROUTE_LANE0, rows8)
        rows8 = jnp.where(r8 == 2, w1, rows8)
        rows8 = jnp.where(r8 == 3, w2, rows8)
        rows8 = jnp.where(r8 == 4, pos1_row, rows8)
        rows8 = jnp.where(r8 == 5, pos2_row, rows8)
        route_ref[...] = jnp.concatenate(
            [rows8, jnp.zeros((LANES - SUBLANES, TM), F32)], axis=0).T

        def sort_rows(lo, hi):
            out_row = (lax.broadcasted_iota(I32, (hi - lo, TM), 0) + lo).astype(F32)
            sort_mat = jnp.where((out_row == pos1_row) | (out_row == pos2_row), 1.0, 0.0).astype(BF16)
            stage_ref[lo:hi, :] = jnp.dot(sort_mat, h2b, preferred_element_type=F32).astype(BF16)

        sort_rows(0, SORT_MAIN // 2)
        if mix:
            left = lax.broadcasted_iota(I32, (CHUNK, LANES), 1) < (LANES // 2)
            zero = jnp.zeros((CHUNK, LANES), BF16)
            for j in range(HEAD_PAIRS):
                cols = []
                for c in range(nch):
                    s = vnb[c * CHUNK:(c + 1) * CHUNK, j * LANES:(j + 1) * LANES]
                    cols.append(jnp.concatenate([jnp.where(left, s, zero), jnp.where(left, zero, s)],
                                                axis=0))
                rhs = jnp.concatenate(cols, axis=1)
                res = jnp.dot(wcat_s[j], rhs, preferred_element_type=F32)
                bias = bsp_ref[:, j * LANES:(j + 1) * LANES]
                for c in range(nch):
                    mixed = res[:, c * LANES:(c + 1) * LANES] + bias
                    uu = u[c * CHUNK:(c + 1) * CHUNK, j * LANES:(j + 1) * LANES]
                    mix_s[c * CHUNK:(c + 1) * CHUNK,
                          CONV_WIDTH + j * LANES:CONV_WIDTH + (j + 1) * LANES] = (uu * mixed).astype(BF16)

        sort_rows(SORT_MAIN // 2, SORT_MAIN)
        if mix:
            x1 = x + jnp.dot(mix_s[...], wout_ref[...], preferred_element_type=F32)
            x1_ref[...] = x1
            h2_s[i % 2] = _rms(x1, gffn_ref[...]).astype(BF16)

        used_rows = BLOCK * jnp.sum(blocks_sq[:, 0:1])
        for lo in range(SORT_MAIN, KOUT, TAIL):
            @pl.when(used_rows > lo)
            def _tail(lo=lo):
                sort_rows(lo, lo + TAIL)

            @pl.when(used_rows <= lo)
            def _empty_tail(lo=lo):
                stage_ref[lo:lo + TAIL, :] = jnp.zeros((TAIL, D_MODEL), BF16)

    pl.when(i < nt)(functools.partial(step, True))

    @pl.when(i == nt)
    def _last():
        step(False)
        _build_tables(n_s, nt, meta_ref, src_ref, dst_ref, used_ref)


def _mixer_router(x2d, g_mix, w_in, w_conv, g_sgu, w_sp, b_sp, w_out, g_ffn, w_rt, b_rt,
                  w_gate, w_up, w_down, seq, n_tiles):
    t = x2d.shape[0]
    nt = t // TM
    tp = -(-n_tiles // SUBLANES) * SUBLANES
    whole = lambda r: pl.BlockSpec((r, LANES), lambda i: (0, 0))
    full = lambda a: pl.BlockSpec(a.shape, lambda i: (0,) * a.ndim, pipeline_mode=pl.Buffered(1))
    any_spec = pl.BlockSpec(memory_space=pl.ANY)
    ins = (g_mix, w_in, w_conv, g_sgu, w_sp, b_sp, w_out, g_ffn, w_rt, b_rt)
    expert_w = (w_gate, w_up, w_down)
    prev = lambda i: (jnp.where(i == 0, nt, i - 1), 0)
    return pl.pallas_call(
        functools.partial(_mixer_router_kernel, seq // TM, nt),
        grid=(nt + 1,),
        in_specs=[pl.BlockSpec((TM, D_MODEL), lambda i: (jnp.minimum(i, nt - 1), 0))]
        + [full(a) for a in ins] + [any_spec] * 3,
        out_specs=[pl.BlockSpec((TM, D_MODEL), lambda i: (jnp.minimum(i, nt - 1), 0)),
                   pl.BlockSpec((TM, LANES), prev),
                   pl.BlockSpec((KOUT, D_MODEL), prev),
                   whole(tp), whole(tp), whole(tp + SUBLANES), whole(nt)] + [any_spec] * 3,
        out_shape=[jax.ShapeDtypeStruct((t, D_MODEL), F32),
                   jax.ShapeDtypeStruct(((nt + 1) * TM, LANES), F32),
                   jax.ShapeDtypeStruct(((nt + 1) * KOUT, D_MODEL), BF16),
                   jax.ShapeDtypeStruct((tp, LANES), I32),
                   jax.ShapeDtypeStruct((tp, LANES), I32),
                   jax.ShapeDtypeStruct((tp + SUBLANES, LANES), I32),
                   jax.ShapeDtypeStruct((nt, LANES), I32)]
        + [jax.ShapeDtypeStruct(w.shape, BF16) for w in expert_w],
        scratch_shapes=[pltpu.VMEM((HEAD_PAIRS, CHUNK, 2 * CHUNK), BF16),
                        pltpu.VMEM((TM, TM), BF16),
                        pltpu.VMEM((8, CONV_WIDTH), F32),
                        pltpu.VMEM((TM, D_MODEL), BF16),
                        pltpu.VMEM((2, TM, D_MODEL), BF16),
                        pltpu.VMEM(w_in.shape, BF16),
                        pltpu.VMEM(w_out.shape, BF16),
                        pltpu.VMEM(w_rt.shape, BF16),
                        pltpu.VMEM((nt + SUBLANES, LANES), F32)]
        + [pltpu.VMEM(w.shape[1:], F32) for w in expert_w]
        + [pltpu.VMEM(w.shape[1:], BF16) for w in expert_w]
        + [pltpu.SemaphoreType.DMA((3,)), pltpu.SemaphoreType.DMA((3,))],
        compiler_params=pltpu.CompilerParams(dimension_semantics=("arbitrary",),
                                             vmem_limit_bytes=VMEM_LIMIT),
        name="mixer_router",
    )(x2d, *ins, *expert_w)


def _expert_kernel(meta_ref, src_ref, dst_ref, stage_ref,
                   wg_hbm, wu_hbm, wd_hbm, out_ref,
                   xbuf, obuf, wg_s, wu_s, wd_s, par_ref, sem, osem, wsem):
    i = pl.program_id(0)
    n_used = meta_ref[3]
    slot = i % N_SLOTS
    bpt = BLOCKS_PER_ROW_TILE
    dump_row = dst_ref.shape[0] // bpt - SUBLANES
    expert_of = lambda tile: meta_ref[tile * META_W]

    def gather_copy(seq, j):
        tile = jnp.minimum(seq, n_used - 1)
        src = pl.multiple_of(src_ref[tile * bpt + j] * BLOCK, BLOCK)
        slot_ = seq % N_SLOTS
        return pltpu.make_async_copy(stage_ref.at[pl.ds(src, BLOCK)],
                                     xbuf.at[slot_, pl.ds(j * BLOCK, BLOCK)], sem.at[slot_])

    def scatter_copy(tile, j):
        row = jnp.where(tile < 0, dump_row, tile)
        dst = pl.multiple_of(dst_ref[row * bpt + j] * BLOCK, BLOCK)
        slot_ = (tile + N_SLOTS) % N_SLOTS
        return pltpu.make_async_copy(obuf.at[slot_, pl.ds(j * BLOCK, BLOCK)],
                                     out_ref.at[pl.ds(dst, BLOCK)], osem.at[slot_])

    def each_block(fn):
        for j in range(BLOCKS_PER_ROW_TILE):
            fn(j)

    def weight_copies(e, p):
        return (pltpu.make_async_copy(wg_hbm.at[e], wg_s.at[p], wsem.at[0]),
                pltpu.make_async_copy(wu_hbm.at[e], wu_s.at[p], wsem.at[1]),
                pltpu.make_async_copy(wd_hbm.at[e], wd_s.at[p], wsem.at[2]))

    @pl.when(i == 0)
    def _first():
        each_block(lambda j: gather_copy(0, j).start())
        each_block(lambda j: gather_copy(1, j).start())
        par_ref[0] = 0
        for cp in weight_copies(expert_of(0), 0):
            cp.start(priority=WEIGHT_DMA_PRIORITY)
        obuf[...] = jnp.zeros_like(obuf)

    @pl.when((i >= 2) & (i - 3 < n_used))
    def _free_out_slot():
        each_block(lambda j: scatter_copy(i - 3, j).wait())

    @pl.when(i == n_used)
    def _after_last_tile():
        each_block(lambda j: gather_copy(n_used, j).wait())
        each_block(lambda j: gather_copy(n_used + 1, j).wait())
        each_block(lambda j: scatter_copy(n_used - 1, j).start())

    @pl.when(i < n_used)
    def _tile():
        e = expert_of(i)
        prev = expert_of(jnp.maximum(i - 1, 0))

        @pl.when((i == 0) | (e != prev))
        def _new_expert():
            p = jnp.where(i == 0, 0, 1 - par_ref[0])
            for cp in weight_copies(e, p):
                cp.wait()
            par_ref[0] = p
            nxt = meta_ref[i * META_W + 1]

            @pl.when(nxt >= 0)
            def _next_weights():
                for cp in weight_copies(nxt, 1 - p):
                    cp.start(priority=WEIGHT_DMA_PRIORITY)

        each_block(lambda j: gather_copy(i, j).wait())

        def mlp(rows):
            p = par_ref[0]
            hb = xbuf[slot, 0:rows, :]
            a = jnp.dot(hb, wg_s[p], preferred_element_type=F32)
            b = jnp.dot(hb, wu_s[p], preferred_element_type=F32)
            each_block(lambda j: scatter_copy(i - 1, j).start())
            each_block(lambda j: gather_copy(i + 2, j).start())
            hid = (a * jax.nn.sigmoid(a) * b).astype(BF16)
            obuf[slot, 0:rows, :] = jnp.dot(hid, wd_s[p], preferred_element_type=F32).astype(BF16)

        rows_needed = meta_ref[i * META_W + 2]
        for rows in range(ROW_STEP, TR + 1, ROW_STEP):
            pl.when(rows_needed == rows)(functools.partial(mlp, rows))


def _experts(meta, src_block, dst_block, stage, w_gate, w_up, w_down, n_tiles):
    any_spec = pl.BlockSpec(memory_space=pl.ANY)
    return pl.pallas_call(
        _expert_kernel,
        grid_spec=pltpu.PrefetchScalarGridSpec(
            num_scalar_prefetch=3,
            grid=(n_tiles,),
            in_specs=[any_spec, any_spec, any_spec, any_spec],
            out_specs=any_spec,
            scratch_shapes=[pltpu.VMEM((N_SLOTS, TR, D_MODEL), BF16),
                            pltpu.VMEM((N_SLOTS, TR, D_MODEL), BF16),
                            pltpu.VMEM((2, D_MODEL, D_EXPERT), BF16),
                            pltpu.VMEM((2, D_MODEL, D_EXPERT), BF16),
                            pltpu.VMEM((2, D_EXPERT, D_MODEL), BF16),
                            pltpu.SMEM((1,), I32),
                            pltpu.SemaphoreType.DMA((N_SLOTS,)),
                            pltpu.SemaphoreType.DMA((N_SLOTS,)),
                            pltpu.SemaphoreType.DMA((3,))]),
        out_shape=jax.ShapeDtypeStruct(stage.shape, stage.dtype),
        input_output_aliases={3: 0},
        compiler_params=pltpu.CompilerParams(dimension_semantics=("arbitrary",),
                                             vmem_limit_bytes=VMEM_LIMIT),
        name="experts",
    )(meta, src_block, dst_block, stage, w_gate, w_up, w_down)


def _combine_kernel(used_ref, x1_ref, route_ref, p_ref, gple_ref, wpg_f32, wpp_f32, gfin_ref,
                    srt_ref, out_ref, y_s, wpg_ref, wpp_ref):
    i = pl.program_id(0)

    @pl.when(i == 0)
    def _init():
        wpg_ref[...] = wpg_f32[...].astype(BF16)
        wpp_ref[...] = wpp_f32[...].astype(BF16)

    route = route_ref[...]
    w1, w2, pos1, pos2 = (route[:, k:k + 1] for k in (2, 3, 4, 5))

    def unsort(lo, hi):
        col = (lax.broadcasted_iota(I32, (TM, hi - lo), 1) + lo).astype(F32)
        mat = jnp.where(col == pos1, w1, jnp.where(col == pos2, w2, 0.0)).astype(BF16)
        return jnp.dot(mat, srt_ref[lo:hi, :], preferred_element_type=F32)

    def finish(y):
        pe = jnp.dot(p_ref[...].astype(BF16), wpp_ref[...], preferred_element_type=F32)
        hrows = TM // 2
        for h in range(2):
            rs = slice(h * hrows, (h + 1) * hrows)
            x2 = x1_ref[rs, :] + y[rs, :]
            hg = _rms(x2, gple_ref[...]).astype(BF16)
            gate = jax.nn.sigmoid(jnp.dot(hg, wpg_ref[...], preferred_element_type=F32))
            x3 = x2 + gate * pe[rs, :]
            out_ref[rs, :] = _rms(x3, gfin_ref[...])

    @pl.when(used_ref[i] <= SORT_MAIN)
    def _common():
        finish(unsort(0, SORT_MAIN))

    @pl.when(used_ref[i] > SORT_MAIN)
    def _with_tails():
        y_s[...] = unsort(0, SORT_MAIN)
        for lo in range(SORT_MAIN, KOUT, TAIL):
            @pl.when(used_ref[i] > lo)
            def _tail(lo=lo):
                y_s[...] += unsort(lo, lo + TAIL)
        finish(y_s[...])


def _combine(used, x1, route, p2d, g_ple, w_pg, w_pp, g_fin, sorted_out):
    t = p2d.shape[0]
    full = lambda a: pl.BlockSpec(a.shape, lambda i, *_: (0,) * a.ndim, pipeline_mode=pl.Buffered(1))
    return pl.pallas_call(
        _combine_kernel,
        grid_spec=pltpu.PrefetchScalarGridSpec(
            num_scalar_prefetch=1,
            grid=(t // TM,),
            in_specs=[pl.BlockSpec((TM, D_MODEL), lambda i, *_: (i, 0)),
                      pl.BlockSpec((TM, LANES), lambda i, *_: (i, 0)),
                      pl.BlockSpec((TM, PLE_DIM), lambda i, *_: (i, 0)),
                      full(g_ple), full(w_pg), full(w_pp), full(g_fin),
                      pl.BlockSpec((KOUT, D_MODEL), lambda i, *_: (i, 0))],
            out_specs=pl.BlockSpec((TM, D_MODEL), lambda i, *_: (i, 0)),
            scratch_shapes=[pltpu.VMEM((TM, D_MODEL), F32),
                            pltpu.VMEM(w_pg.shape, BF16),
                            pltpu.VMEM(w_pp.shape, BF16)]),
        out_shape=jax.ShapeDtypeStruct((t, D_MODEL), F32),
        compiler_params=pltpu.CompilerParams(dimension_semantics=("arbitrary",),
                                             vmem_limit_bytes=VMEM_LIMIT),
        name="combine_ple",
    )(used, x1, route, p2d, g_ple, w_pg, w_pp, g_fin, sorted_out)


def kernel(x, p, g_mix, w_in, w_conv, g_sgu, w_spatial, b_spatial, w_out, g_ffn, w_group, b_group,
           w_router, b_router, w_gate, w_up, w_down, g_ple, w_ple_gate, w_ple_proj, g_final):
    bsz, seq, d = x.shape
    t = bsz * seq
    assert w_in.shape[0] == 1, "single-layer block"
    assert d == D_MODEL and seq % TM == 0 and TM % CHUNK == 0
    nt = t // TM
    max_rows = TOP_K * t + nt * N_EXPERTS * (BLOCK - 1)
    n_tiles = (max_rows + N_EXPERTS * (TR - 1)) // TR + N_SLOTS
    l = 0

    w_rt = jnp.concatenate(
        [w_group[l], jnp.transpose(w_router[l], (1, 0, 2)).reshape(d, N_EXPERTS)], axis=1)
    w_rt = jnp.pad(w_rt, ((0, 0), (0, LANES - w_rt.shape[1]))).T
    b_rt = jnp.pad(jnp.concatenate([b_group[l], b_router[l].reshape(-1)]),
                   (0, LANES - N_GROUPS - N_EXPERTS)).reshape(LANES, 1)
    b_sp = jnp.repeat(b_spatial[l].T, SGU_WIDTH // SGU_HEADS, axis=1)

    x1, route, stage, meta, src, dst, used, wg_b, wu_b, wd_b = _mixer_router(
        x.reshape(t, d), g_mix[l].reshape(1, d), w_in[l], w_conv[l],
        g_sgu[l].reshape(1, -1), w_spatial[l], b_sp, w_out[l],
        g_ffn[l].reshape(1, d), w_rt, b_rt,
        w_gate[l].reshape(N_EXPERTS, d, D_EXPERT), w_up[l].reshape(N_EXPERTS, d, D_EXPERT),
        w_down[l].reshape(N_EXPERTS, D_EXPERT, d), seq, n_tiles)

    bpt = BLOCKS_PER_ROW_TILE
    meta, src, dst, used = (meta[:, :META_W].reshape(-1), src[:, :bpt].reshape(-1),
                            dst[:, :bpt].reshape(-1), used[:, 0])
    sorted_out = _experts(meta, src, dst, stage, wg_b, wu_b, wd_b, n_tiles)
    out = _combine(used, x1, route, p[l].reshape(t, PLE_DIM), g_ple[l].reshape(1, d),
                   w_ple_gate[l], w_ple_proj[l], g_final.reshape(1, d), sorted_out)
    return out.reshape(bsz, seq, d)
```
